```python
import math
import jax, jax.numpy as jnp
from jax import lax
import numpy as np

D_MODEL = 1024
BATCH = 2
SEQ = 8192
DEPTH = 4
DEC_BATCH = 128
DEC_SEQ = 1
PAST_LEN = 8192
PAGE_SIZE = 128

HEAD_DIM = 64
MIX_WIDTH = D_MODEL
A_HEADS = (MIX_WIDTH // 2) // HEAD_DIM
A_KV_HEADS = 2
A_REP = A_HEADS // A_KV_HEADS
WINDOW = 128
BLOCK = WINDOW
N_BUCKETS = 32
MAX_DISTANCE = 128
HG_WIDTH = MIX_WIDTH // 4
HG_HEADS = 4
HG_DK = HG_WIDTH // HG_HEADS
HG_DV = HG_WIDTH // HG_HEADS
HG_CHUNK = 64
LRU_WIDTH = MIX_WIDTH // 4
LRU_BLOCKS = 4
LRU_BS = LRU_WIDTH // LRU_BLOCKS
CONV_W = 4
LRU_C = 8.0
N_MEM = 256
X_HEADS = 4
X_WIDTH = X_HEADS * HEAD_DIM
N_GROUPS = 4
EXP_PER_GROUP = 4
N_EXPERTS = N_GROUPS * EXP_PER_GROUP
EXP_TOP_K = 2
EXP_FF = D_MODEL // 4
ALPHA = (2 * DEPTH) ** 0.25
BETA = (8 * DEPTH) ** -0.25
LN_EPS = 1e-5
RMS_EPS = 1e-6
IN_SIZES = (A_HEADS * HEAD_DIM, A_KV_HEADS * HEAD_DIM, A_KV_HEADS * HEAD_DIM,
            HG_WIDTH, HG_WIDTH, HG_HEADS * HG_DV, HG_HEADS * HG_DV,
            LRU_WIDTH, LRU_WIDTH)
IN_COLS = sum(IN_SIZES)

kernel_name = "hymba_swa_hgrn2_rglru_hmoe_step"


def layer_norm(x, g, b):
    xf = x.astype(jnp.float32)
    mu = jnp.mean(xf, -1, keepdims=True)
    var = jnp.mean(jnp.square(xf - mu), -1, keepdims=True)
    return ((xf - mu) * lax.rsqrt(var + LN_EPS) * g + b).astype(x.dtype)


def post_norm(x, sub, g, b):
    return layer_norm(ALPHA * x + sub, g, b)


def t5_bucket(dist):
    max_exact = N_BUCKETS // 2
    d = jnp.maximum(dist, 0)
    df = jnp.maximum(d, 1).astype(jnp.float32)
    log_b = max_exact + (jnp.log(df / max_exact) / math.log(MAX_DISTANCE / max_exact)
                         * (N_BUCKETS - max_exact)).astype(jnp.int32)
    return jnp.where(d < max_exact, d, jnp.minimum(log_b, N_BUCKETS - 1))


def sink_probs(s, sink_b):
    m = jnp.maximum(jnp.max(s, -1, keepdims=True), sink_b)
    p = jnp.exp(s - m)
    return p / (jnp.sum(p, -1, keepdims=True) + jnp.exp(sink_b - m))


def window_attn_prompt(q, k, v, sink, rel_bias):
    B, T = q.shape[:2]
    nb = T // BLOCK
    qb = q.reshape(B, nb, BLOCK, A_KV_HEADS, A_REP, HEAD_DIM)
    pad = jnp.zeros((B, BLOCK, A_KV_HEADS, HEAD_DIM), k.dtype)

    def band(t):
        tp = jnp.concatenate([pad, t], 1)
        return jnp.concatenate([tp[:, :T].reshape(B, nb, BLOCK, A_KV_HEADS, HEAD_DIM),
                                tp[:, BLOCK:].reshape(B, nb, BLOCK, A_KV_HEADS, HEAD_DIM)], axis=2)

    kb, vb = band(k), band(v)
    qi = jnp.arange(BLOCK)[:, None]
    kj = jnp.arange(2 * BLOCK)[None, :]
    dist = qi + BLOCK - kj
    bias = rel_bias[t5_bucket(dist)].astype(jnp.float32)
    bias = bias.transpose(2, 0, 1).reshape(A_KV_HEADS, A_REP, BLOCK, 2 * BLOCK)
    key_pos = jnp.arange(nb)[:, None] * BLOCK - BLOCK + kj
    valid = ((dist >= 0) & (dist <= WINDOW))[None] & (key_pos >= 0)[:, None, :]
    s = jnp.einsum('bnqgrd,bnkgd->bngrqk', qb, kb).astype(jnp.float32) * (HEAD_DIM ** -0.5) + bias
    s = jnp.where(valid[None, :, None, None], s, -jnp.inf)
    sink_b = sink.astype(jnp.float32).reshape(A_KV_HEADS, A_REP)[None, None, :, :, None, None]
    p = sink_probs(s, sink_b).astype(v.dtype)
    o = jnp.einsum('bngrqk,bnkgd->bnqgrd', p, vb)
    return o.reshape(B, T, A_HEADS * HEAD_DIM)


def window_attn_decode(q, k, v, buf_k, buf_v, sink, rel_bias):
    B, T = q.shape[:2]
    L = buf_k.shape[1]
    kk = jnp.concatenate([buf_k, k], 1)
    vv = jnp.concatenate([buf_v, v], 1)
    dist = jnp.arange(T)[:, None] + L - jnp.arange(L + T)[None, :]
    valid = (dist >= 0) & (dist <= WINDOW)
    bias = rel_bias[t5_bucket(dist)].astype(jnp.float32)
    bias = bias.transpose(2, 0, 1).reshape(A_KV_HEADS, A_REP, T, L + T)
    s = jnp.einsum('bqgrd,bkgd->bgrqk', q, kk).astype(jnp.float32) * (HEAD_DIM ** -0.5) + bias
    s = jnp.where(valid, s, -jnp.inf)
    sink_b = sink.astype(jnp.float32).reshape(A_KV_HEADS, A_REP)[None, :, :, None, None]
    p = sink_probs(s, sink_b).astype(v.dtype)
    o = jnp.einsum('bgrqk,bkgd->bqgrd', p, vv).reshape(B, T, A_HEADS * HEAD_DIM)
    return o, kk[:, T:], vv[:, T:]


def hgrn2_chunk(S, qc, kc, vc, lfc):
    C = qc.shape[1]
    cum = jnp.cumsum(lfc, axis=1)
    causal = jnp.tril(jnp.ones((C, C), bool))[None, :, :, None, None]
    diff = cum[:, :, None] - cum[:, None, :]
    decay = jnp.exp(jnp.where(causal, diff, -jnp.inf))
    att = jnp.einsum('bthk,btshk,bshk->bhts', qc, decay, kc)
    o = (jnp.einsum('bhts,bshv->bthv', att, vc)
         + jnp.einsum('bthk,bhkv->bthv', qc * jnp.exp(cum), S))
    last = cum[:, -1]
    S_new = (jnp.exp(last)[..., None] * S
             + jnp.einsum('bshk,bshv->bhkv', kc * jnp.exp(last[:, None] - cum), vc))
    return S_new, o


def hgrn2(S0, q, k, v, logf):
    B, T = q.shape[:2]
    C = math.gcd(T, HG_CHUNK)
    nc = T // C

    def to_chunks(t):
        return t.reshape(B, nc, C, *t.shape[2:]).swapaxes(0, 1)

    S, o = lax.scan(lambda s, xs: hgrn2_chunk(s, *xs), S0.astype(jnp.float32),
                    (to_chunks(q), to_chunks(k), to_chunks(v), to_chunks(logf)))
    return o.swapaxes(0, 1).reshape(B, T, HG_HEADS, HG_DV), S


def causal_conv(x, buf, w, b):
    T = x.shape[1]
    xp = jnp.concatenate([buf, x], 1)
    y = b + sum(xp[:, i:i + T] * w[i] for i in range(CONV_W))
    return y, xp[:, xp.shape[1] - (CONV_W - 1):]


def _linear_combine(left, right):
    a_l, b_l = left
    a_r, b_r = right
    return a_l * a_r, a_r * b_l + b_r


def rglru(xc, h0, wa, ba, wx, bx, lam):
    B, T, W = xc.shape
    xb = xc.reshape(B, T, LRU_BLOCKS, LRU_BS)
    r = jax.nn.sigmoid(jnp.einsum('btnj,njk->btnk', xb, wa).reshape(B, T, W) + ba)
    gi = jax.nn.sigmoid(jnp.einsum('btnj,njk->btnk', xb, wx).reshape(B, T, W) + bx)
    log_a = -LRU_C * r * jax.nn.softplus(-lam)
    a = jnp.exp(log_a)
    bterm = jnp.sqrt(-jnp.expm1(2.0 * log_a)) * (gi * xc)
    bterm = bterm.at[:, 0].add(a[:, 0] * h0)
    _, h = lax.associative_scan(_linear_combine, (a, bterm), axis=1)
    return h, h[:, -1]


def token_mixers(x, win_k, win_v, S0, conv_buf, h0, lb, rel_bias,
                 w_in, sink, hg_gain, conv_w, conv_b, wa, ba, wx, bx, lam, w_out):
    B, T, _ = x.shape
    f32 = jnp.float32
    split_pts = [int(s) for s in np.cumsum(IN_SIZES)[:-1]]
    aq, ak, av, hq, hf, hi, hg, lx, lg = jnp.split(x @ w_in, split_pts, axis=-1)
    q = aq.reshape(B, T, A_KV_HEADS, A_REP, HEAD_DIM)
    k = ak.reshape(B, T, A_KV_HEADS, HEAD_DIM)
    v = av.reshape(B, T, A_KV_HEADS, HEAD_DIM)
    if win_k is None:
        oa = window_attn_prompt(q, k, v, sink, rel_bias)
        keep = min(WINDOW, T)
        new_wk, new_wv = k[:, T - keep:], v[:, T - keep:]
    else:
        oa, new_wk, new_wv = window_attn_decode(q, k, v, win_k, win_v, sink, rel_bias)
    logf = jnp.logaddexp(jnp.log(lb), jnp.log1p(-lb) + jax.nn.log_sigmoid(hf.astype(f32)))
    heads = lambda t: t.reshape(B, T, HG_HEADS, -1)
    ob, S_new = hgrn2(S0, heads(jax.nn.silu(hq.astype(f32))), heads(-jnp.expm1(logf)),
                      heads(hi.astype(f32)), heads(logf))
    ob = ob * lax.rsqrt(jnp.mean(jnp.square(ob), -1, keepdims=True) + RMS_EPS) * hg_gain
    ob = (ob * jax.nn.silu(heads(hg.astype(f32)))).reshape(B, T, HG_WIDTH).astype(x.dtype)
    xc, new_buf = causal_conv(lx, conv_buf, conv_w, conv_b)
    hl, h_last = rglru(xc.astype(f32), h0.astype(f32), wa, ba, wx, bx, lam)
    oc = (hl * jax.nn.gelu(lg.astype(f32))).astype(x.dtype)
    out = jnp.concatenate([oa, ob, oc], -1) @ w_out
    return out, new_wk, new_wv, S_new.astype(x.dtype), new_buf, h_last.astype(x.dtype)


def cross_attend(x, mem_k, mem_v, wq, wo):
    B, T, _ = x.shape
    q = (x @ wq).reshape(B, T, X_HEADS, HEAD_DIM)
    s = jnp.einsum('bthd,bmhd->bhtm', q, mem_k).astype(jnp.float32) * (HEAD_DIM ** -0.5)
    p = jax.nn.softmax(s, -1).astype(x.dtype)
    o = jnp.einsum('bhtm,bmhd->bthd', p, mem_v).reshape(B, T, X_WIDTH)
    return o @ wo


def hier_moe(x, rgw, rgb, rew, reb, ewg, ewu, ewd):
    B, T, D = x.shape
    xt = x.reshape(B * T, D)
    pg = jax.nn.softmax((xt @ rgw + rgb).astype(jnp.float32), -1)
    g_val, g_idx = lax.top_k(pg, 1)
    el = jnp.einsum('nd,gde->nge', xt, rew) + reb
    el_sel = jnp.take_along_axis(el, g_idx[:, :, None], axis=1)[:, 0].astype(jnp.float32)
    e_val, e_idx = lax.top_k(el_sel, EXP_TOP_K)
    w2 = jax.nn.softmax(e_val, -1) * g_val
    eid = g_idx * EXP_PER_GROUP + e_idx
    gate = jnp.einsum('nk,nke->ne', w2, jax.nn.one_hot(eid, N_EXPERTS, dtype=jnp.float32))
    h = jax.nn.silu(jnp.einsum('nd,edf->nef', xt, ewg)) * jnp.einsum('nd,edf->nef', xt, ewu)
    y = jnp.einsum('nef,efd->nd', h * gate[..., None].astype(x.dtype), ewd)
    return y.reshape(B, T, D)


def trunk_layer(x, mem_k, mem_v, win_k, win_v, S0, conv_buf, h0, lb, rel_bias, lw):
    (w_in, sink, hg_gain, conv_w, conv_b, wa, ba, wx, bx, lam, w_out, g1, b1,
     wq, wo, g2, b2, rgw, rgb, rew, reb, ewg, ewu, ewd, g3, b3) = lw
    mix, nwk, nwv, nS, nbuf, nh = token_mixers(x, win_k, win_v, S0, conv_buf, h0, lb, rel_bias,
                                               w_in, sink, hg_gain, conv_w, conv_b,
                                               wa, ba, wx, bx, lam, w_out)
    x = post_norm(x, mix, g1, b1)
    x = post_norm(x, cross_attend(x, mem_k, mem_v, wq, wo), g2, b2)
    x = post_norm(x, hier_moe(x, rgw, rgb, rew, reb, ewg, ewu, ewd), g3, b3)
    return x, nwk, nwv, nS, nbuf, nh


def setup_inputs(seed: int = 0) -> dict:
    key = jax.random.key(seed)
    ks = iter(jax.random.split(key, 64))
    f32 = jnp.float32

    def nrm(shape, scale):
        return jax.random.normal(next(ks), shape, f32) * scale

    win_len = min(WINDOW, PAST_LEN)
    u = jax.random.uniform(next(ks), (DEPTH, LRU_WIDTH), f32, minval=0.9, maxval=0.999)
    s_a = u ** (1.0 / LRU_C)
    return {
        "x_prompt": nrm((BATCH, SEQ, D_MODEL), 1.0),
        "x_sample": nrm((DEC_BATCH, DEC_SEQ, D_MODEL), 1.0),
        "mem_prompt": nrm((BATCH, N_MEM, D_MODEL), 1.0),
        "cache_win_k": nrm((DEPTH, DEC_BATCH, win_len, A_KV_HEADS, HEAD_DIM), 1.0),
        "cache_win_v": nrm((DEPTH, DEC_BATCH, win_len, A_KV_HEADS, HEAD_DIM), 1.0),
        "state_hgrn": nrm((DEPTH, DEC_BATCH, HG_HEADS, HG_DK, HG_DV), 0.3),
        "state_conv": nrm((DEPTH, DEC_BATCH, CONV_W - 1, LRU_WIDTH), 1.0),
        "state_lru": nrm((DEPTH, DEC_BATCH, LRU_WIDTH), 0.5),
        "cache_mem_k": nrm((DEPTH, DEC_BATCH, N_MEM, X_HEADS, HEAD_DIM), 1.0),
        "cache_mem_v": nrm((DEPTH, DEC_BATCH, N_MEM, X_HEADS, HEAD_DIM), 1.0),
        "rel_bias": nrm((N_BUCKETS, A_HEADS), 0.5),
        "hg_lb": 1.0 + nrm((DEPTH, HG_WIDTH), 0.5),
        "w_in": nrm((DEPTH, D_MODEL, IN_COLS), D_MODEL ** -0.5),
        "attn_sink": nrm((DEPTH, A_HEADS), 0.5),
        "hg_gain": 1.0 + nrm((DEPTH, HG_DV), 0.05),
        "conv_w": nrm((DEPTH, CONV_W, LRU_WIDTH), CONV_W ** -0.5),
        "conv_b": nrm((DEPTH, LRU_WIDTH), 0.02),
        "lru_wa": nrm((DEPTH, LRU_BLOCKS, LRU_BS, LRU_BS), LRU_BS ** -0.5),
        "lru_ba": nrm((DEPTH, LRU_WIDTH), 0.02),
        "lru_wx": nrm((DEPTH, LRU_BLOCKS, LRU_BS, LRU_BS), LRU_BS ** -0.5),
        "lru_bx": nrm((DEPTH, LRU_WIDTH), 0.02),
        "lru_lam": jnp.log(s_a) - jnp.log1p(-s_a),
        "w_out": nrm((DEPTH, MIX_WIDTH, D_MODEL), BETA * MIX_WIDTH ** -0.5),
        "ln1_g": 1.0 + nrm((DEPTH, D_MODEL), 0.05),
        "ln1_b": nrm((DEPTH, D_MODEL), 0.02),
        "x_wq": nrm((DEPTH, D_MODEL, X_WIDTH), D_MODEL ** -0.5),
        "x_wk": nrm((DEPTH, D_MODEL, X_WIDTH), D_MODEL ** -0.5),
        "x_wv": nrm((DEPTH, D_MODEL, X_WIDTH), D_MODEL ** -0.5),
        "x_wo": nrm((DEPTH, X_WIDTH, D_MODEL), BETA * X_WIDTH ** -0.5),
        "ln2_g": 1.0 + nrm((DEPTH, D_MODEL), 0.05),
        "ln2_b": nrm((DEPTH, D_MODEL), 0.02),
        "r_gw": nrm((DEPTH, D_MODEL, N_GROUPS), D_MODEL ** -0.5),
        "r_gb": nrm((DEPTH, N_GROUPS), 0.01),
        "r_ew": nrm((DEPTH, N_GROUPS, D_MODEL, EXP_PER_GROUP), D_MODEL ** -0.5),
        "r_eb": nrm((DEPTH, N_GROUPS, EXP_PER_GROUP), 0.01),
        "e_wg": nrm((DEPTH, N_EXPERTS, D_MODEL, EXP_FF), D_MODEL ** -0.5),
        "e_wu": nrm((DEPTH, N_EXPERTS, D_MODEL, EXP_FF), D_MODEL ** -0.5),
        "e_wd": nrm((DEPTH, N_EXPERTS, EXP_FF, D_MODEL), BETA * EXP_FF ** -0.5),
        "ln3_g": 1.0 + nrm((DEPTH, D_MODEL), 0.05),
        "ln3_b": nrm((DEPTH, D_MODEL), 0.02),
    }


def reference(x_prompt, x_sample, mem_prompt, cache_win_k, cache_win_v, state_hgrn, state_conv,
              state_lru, cache_mem_k, cache_mem_v, rel_bias, hg_lb, w_in, attn_sink, hg_gain,
              conv_w, conv_b, lru_wa, lru_ba, lru_wx, lru_bx, lru_lam, w_out, ln1_g, ln1_b,
              x_wq, x_wk, x_wv, x_wo, ln2_g, ln2_b, r_gw, r_gb, r_ew, r_eb, e_wg, e_wu, e_wd,
              ln3_g, ln3_b):
    lbs = jnp.cumsum(jax.nn.softmax(hg_lb.astype(jnp.float32), axis=0), axis=0)
    lbs = lbs - lbs[0]
    bp = x_prompt.shape[0]
    xp, xs = x_prompt, x_sample
    p_wk, p_wv, p_S, p_cb, p_h, p_mk, p_mv = [], [], [], [], [], [], []
    s_wk, s_wv, s_S, s_cb, s_h = [], [], [], [], []
    for l in range(DEPTH):
        lw = (w_in[l], attn_sink[l], hg_gain[l], conv_w[l], conv_b[l], lru_wa[l], lru_ba[l],
              lru_wx[l], lru_bx[l], lru_lam[l], w_out[l], ln1_g[l], ln1_b[l], x_wq[l], x_wo[l],
              ln2_g[l], ln2_b[l], r_gw[l], r_gb[l], r_ew[l], r_eb[l], e_wg[l], e_wu[l], e_wd[l],
              ln3_g[l], ln3_b[l])
        mk = (mem_prompt @ x_wk[l]).reshape(bp, N_MEM, X_HEADS, HEAD_DIM)
        mv = (mem_prompt @ x_wv[l]).reshape(bp, N_MEM, X_HEADS, HEAD_DIM)
        S0 = jnp.zeros((bp, HG_HEADS, HG_DK, HG_DV), jnp.float32)
        cb0 = jnp.zeros((bp, CONV_W - 1, LRU_WIDTH), xp.dtype)
        h00 = jnp.zeros((bp, LRU_WIDTH), xp.dtype)
        xp, wk_, wv_, S_, cb_, h_ = trunk_layer(xp, mk, mv, None, None, S0, cb0, h00,
                                                lbs[l], rel_bias, lw)
        p_wk.append(wk_); p_wv.append(wv_); p_S.append(S_); p_cb.append(cb_); p_h.append(h_)
        p_mk.append(mk); p_mv.append(mv)
        xs, wk_, wv_, S_, cb_, h_ = trunk_layer(xs, cache_mem_k[l], cache_mem_v[l],
                                                cache_win_k[l], cache_win_v[l], state_hgrn[l],
                                                state_conv[l], state_lru[l],
                                                lbs[l], rel_bias, lw)
        s_wk.append(wk_); s_wv.append(wv_); s_S.append(S_); s_cb.append(cb_); s_h.append(h_)
    return (xp, xs,
            jnp.stack(p_wk), jnp.stack(p_wv), jnp.stack(p_S), jnp.stack(p_cb), jnp.stack(p_h),
            jnp.stack(p_mk), jnp.stack(p_mv),
            jnp.stack(s_wk), jnp.stack(s_wv), jnp.stack(s_S), jnp.stack(s_cb), jnp.stack(s_h))
```

```python
import functools
import math

import jax
import jax.numpy as jnp
from jax import lax
from jax.experimental import pallas as pl
from jax.experimental.pallas import tpu as pltpu

F32 = jnp.float32
BF16 = jnp.bfloat16

D_MODEL = 1024
DEPTH = 4
HEAD_DIM = 64
A_HEADS = 8
A_KV_HEADS = 2
A_REP = A_HEADS // A_KV_HEADS
WINDOW = 128
N_BUCKETS = 32
MAX_DISTANCE = 128
HG_WIDTH = 256
HG_HEADS = 4
HG_SUB = 16
LRU_WIDTH = 256
LRU_BLOCKS = 4
CONV_W = 4
LRU_C = 8.0
N_MEM = 256
X_HEADS = 4
X_WIDTH = X_HEADS * HEAD_DIM
N_GROUPS = 4
EXP_PER_GROUP = 4
N_EXPERTS = N_GROUPS * EXP_PER_GROUP
EXP_FF = D_MODEL // 4
ALPHA = (2 * DEPTH) ** 0.25
LN_EPS = 1e-5
RMS_EPS = 1e-6
IN_COLS = 2304
SCALE = HEAD_DIM ** -0.5
NEG = -1e30
LANES = 128
ROUTER_LANES = 128
VMEM_LIMIT = 48 * 1024 * 1024


def _cparams(sem):
    return pltpu.CompilerParams(dimension_semantics=sem, vmem_limit_bytes=VMEM_LIMIT)


def _bdot(a, b):
    return jnp.dot(a.astype(BF16), b.astype(BF16), preferred_element_type=F32)


def _bdot_nt(a, b):
    return lax.dot_general(a.astype(BF16), b.astype(BF16), (((1,), (1,)), ((), ())),
                           preferred_element_type=F32)


def _bdot_tn(a, b):
    return lax.dot_general(a.astype(BF16), b.astype(BF16), (((0,), (0,)), ((), ())),
                           preferred_element_type=F32)


def _silu(x):
    return x * jax.nn.sigmoid(x)


def _neg_expm1(x):
    return -jnp.tanh(0.5 * x) * (jnp.exp(x) + 1.0)


def _softplus(x):
    return jnp.maximum(x, 0.0) + jnp.log1p(jnp.exp(-jnp.abs(x)))


def _gelu_tanh(x):
    return 0.5 * x * (1.0 + jnp.tanh(math.sqrt(2.0 / math.pi) * (x + 0.044715 * (x * x * x))))


def _layer_norm(y, g, b):
    mu = jnp.mean(y, -1, keepdims=True)
    yc = y - mu
    var = jnp.mean(yc * yc, -1, keepdims=True)
    return yc * lax.rsqrt(var + LN_EPS) * g + b


def _mm_kernel(x_ref, w_ref, o_ref):
    o_ref[...] = _bdot(x_ref[...], w_ref[...])


def matmul(x, w, tm, tn):
    m, k = x.shape
    n = w.shape[1]
    return pl.pallas_call(
        _mm_kernel,
        grid=(m // tm, n // tn),
        in_specs=[pl.BlockSpec((tm, k), lambda i, j: (i, 0)),
                  pl.BlockSpec((k, tn), lambda i, j: (0, j))],
        out_specs=pl.BlockSpec((tm, tn), lambda i, j: (i, j)),
        out_shape=jax.ShapeDtypeStruct((m, n), F32),
        compiler_params=_cparams(("parallel", "parallel")),
        name="matmul",
    )(x, w)


def _proj_res_ln_kernel(n_in, x_ref, *refs):
    a_refs = refs[:n_in]
    w_refs = refs[n_in:2 * n_in]
    g_ref, b_ref, o_ref = refs[2 * n_in:]
    y = ALPHA * x_ref[...]
    for a_ref, w_ref in zip(a_refs, w_refs):
        y = y + _bdot(a_ref[...], w_ref[...])
    o_ref[...] = _layer_norm(y, g_ref[...], b_ref[...])


def proj_res_ln(x, a_list, w_list, g, b, tm):
    m, d = x.shape
    n_in = len(a_list)
    in_specs = [pl.BlockSpec((tm, d), lambda i: (i, 0))]
    in_specs += [pl.BlockSpec((tm, a.shape[1]), lambda i: (i, 0)) for a in a_list]
    in_specs += [pl.BlockSpec(w.shape, lambda i: (0, 0)) for w in w_list]
    in_specs += [pl.BlockSpec((1, d), lambda i: (0, 0))] * 2
    return pl.pallas_call(
        functools.partial(_proj_res_ln_kernel, n_in),
        grid=(m // tm,),
        in_specs=in_specs,
        out_specs=pl.BlockSpec((tm, d), lambda i: (i, 0)),
        out_shape=jax.ShapeDtypeStruct((m, d), F32),
        compiler_params=_cparams(("parallel",)),
        name="proj_res_ln",
    )(x, *a_list, *w_list, g, b)


def _attn_prompt_kernel(sink_ref, q_ref, kc_ref, kp_ref, vc_ref, vp_ref, bias_ref, o_ref):
    n = pl.program_id(1)
    col = lax.broadcasted_iota(jnp.int32, (WINDOW, 2 * WINDOW), 1)
    first = jnp.where((n == 0) & (col < WINDOW), NEG, 0.0)
    kk = jnp.concatenate([kp_ref[...], kc_ref[...]], axis=0).astype(BF16)
    vv = jnp.concatenate([vp_ref[...], vc_ref[...]], axis=0).astype(BF16)
    q = q_ref[...].astype(BF16)
    outs = []
    for h in range(A_HEADS):
        g = h // A_REP
        qh = q[:, h * HEAD_DIM:(h + 1) * HEAD_DIM]
        kg = kk[:, g * HEAD_DIM:(g + 1) * HEAD_DIM]
        vg = vv[:, g * HEAD_DIM:(g + 1) * HEAD_DIM]
        s = _bdot_nt(qh, kg) * SCALE + bias_ref[h] + first
        sink = sink_ref[h]
        m = jnp.maximum(jnp.max(s, -1, keepdims=True), sink)
        p = jnp.exp(s - m)
        den = jnp.sum(p, -1, keepdims=True) + jnp.exp(sink - m)
        outs.append(_bdot(p / den, vg))
    o_ref[...] = jnp.concatenate(outs, axis=1)


def attn_prompt(proj, sink, bias, batch, seq):
    nb = seq // WINDOW
    qcol = 0
    kcol = (A_HEADS * HEAD_DIM) // LANES
    vcol = kcol + 1

    def cur(c):
        return lambda b, n: (b * nb + n, c)

    def prev(c):
        return lambda b, n: (b * nb + jnp.maximum(n - 1, 0), c)

    return pl.pallas_call(
        _attn_prompt_kernel,
        grid=(batch, nb),
        in_specs=[pl.BlockSpec(memory_space=pltpu.SMEM),
                  pl.BlockSpec((WINDOW, A_HEADS * HEAD_DIM), cur(qcol)),
                  pl.BlockSpec((WINDOW, LANES), cur(kcol)),
                  pl.BlockSpec((WINDOW, LANES), prev(kcol)),
                  pl.BlockSpec((WINDOW, LANES), cur(vcol)),
                  pl.BlockSpec((WINDOW, LANES), prev(vcol)),
                  pl.BlockSpec((A_HEADS, WINDOW, 2 * WINDOW), lambda b, n: (0, 0, 0))],
        out_specs=pl.BlockSpec((WINDOW, A_HEADS * HEAD_DIM), cur(0)),
        out_shape=jax.ShapeDtypeStruct((batch * seq, A_HEADS * HEAD_DIM), F32),
        compiler_params=_cparams(("parallel", "parallel")),
        name="attn_prompt",
    )(sink, proj, proj, proj, proj, proj, bias)


def _seg_sum(x2d, ones_bd):
    return jnp.dot(x2d.astype(BF16), ones_bd, preferred_element_type=F32)


def _attn_decode_kernel(sink_ref, q_ref, kn_ref, vn_ref, ck_ref, cv_ref, bias_ref, ones_ref,
                        o_ref, ok_ref, ov_ref):
    bb = q_ref.shape[0]
    ck = ck_ref[...]
    cv = cv_ref[...]
    kn = kn_ref[...]
    vn = vn_ref[...]
    q = q_ref[...]
    ones_bd = ones_ref[...]
    lane = lax.broadcasted_iota(jnp.int32, (1, LANES), 1)
    pieces = [None] * A_HEADS
    for r in range(A_REP):
        h0, h1 = r, A_REP + r
        qp = jnp.concatenate([q[:, h0 * HEAD_DIM:(h0 + 1) * HEAD_DIM],
                              q[:, h1 * HEAD_DIM:(h1 + 1) * HEAD_DIM]], axis=1)
        prod = (ck * qp[:, None, :]).reshape(bb * WINDOW, LANES)
        s = _seg_sum(prod, ones_bd).reshape(bb, WINDOW, LANES) * SCALE + bias_ref[r][None]
        sn = _seg_sum(kn * qp, ones_bd) * SCALE + bias_ref[A_REP + r][0:1, :]
        sink = jnp.where(lane < HEAD_DIM, sink_ref[h0], sink_ref[h1])
        m = jnp.maximum(jnp.maximum(jnp.max(s, axis=1), sn), sink)
        p = jnp.exp(s - m[:, None, :])
        pn = jnp.exp(sn - m)
        den = jnp.sum(p, axis=1) + pn + jnp.exp(sink - m)
        o = (jnp.sum(p * cv, axis=1) + pn * vn) / den
        pieces[h0] = o[:, :HEAD_DIM]
        pieces[h1] = o[:, HEAD_DIM:]
    o_ref[...] = jnp.concatenate(pieces, axis=1)
    ok_ref[:, 0:WINDOW - 1, :] = ck[:, 1:WINDOW, :]
    ok_ref[:, WINDOW - 1, :] = kn
    ov_ref[:, 0:WINDOW - 1, :] = cv[:, 1:WINDOW, :]
    ov_ref[:, WINDOW - 1, :] = vn


def attn_decode(proj_d, sink, bias_d, ones_bd, cache_k, cache_v, bb):
    nbatch = proj_d.shape[0]
    kcol = (A_HEADS * HEAD_DIM) // LANES
    cache_spec = pl.BlockSpec((bb, WINDOW, LANES), lambda i: (i, 0, 0))
    return pl.pallas_call(
        _attn_decode_kernel,
        grid=(nbatch // bb,),
        in_specs=[pl.BlockSpec(memory_space=pltpu.SMEM),
                  pl.BlockSpec((bb, A_HEADS * HEAD_DIM), lambda i: (i, 0)),
                  pl.BlockSpec((bb, LANES), lambda i: (i, kcol)),
                  pl.BlockSpec((bb, LANES), lambda i: (i, kcol + 1)),
                  cache_spec, cache_spec,
                  pl.BlockSpec((2 * A_REP, WINDOW, LANES), lambda i: (0, 0, 0)),
                  pl.BlockSpec((LANES, LANES), lambda i: (0, 0))],
        out_specs=[pl.BlockSpec((bb, A_HEADS * HEAD_DIM), lambda i: (i, 0)), cache_spec, cache_spec],
        out_shape=[jax.ShapeDtypeStruct((nbatch, A_HEADS * HEAD_DIM), F32),
                   jax.ShapeDtypeStruct(cache_k.shape, F32),
                   jax.ShapeDtypeStruct(cache_v.shape, F32)],
        compiler_params=_cparams(("parallel",)),
        name="attn_decode",
    )(sink, proj_d, proj_d, proj_d, cache_k, cache_v, bias_d, ones_bd)


def _xattn_decode_kernel(q_ref, mk_ref, mv_ref, ones_ref, o_ref):
    bb = q_ref.shape[0]
    q = q_ref[...]
    mk = mk_ref[...]
    prod = (mk * q[:, None, :]).reshape(bb * N_MEM, X_WIDTH)
    s = _seg_sum(prod, ones_ref[...]).reshape(bb, N_MEM, X_WIDTH) * SCALE
    m = jnp.max(s, axis=1, keepdims=True)
    p = jnp.exp(s - m)
    den = jnp.sum(p, axis=1)
    o_ref[...] = jnp.sum(p * mv_ref[...], axis=1) / den


def xattn_decode(q, mem_k, mem_v, ones_bd, bb):
    nbatch = q.shape[0]
    mem_spec = pl.BlockSpec((bb, N_MEM, X_WIDTH), lambda i: (i, 0, 0))
    return pl.pallas_call(
        _xattn_decode_kernel,
        grid=(nbatch // bb,),
        in_specs=[pl.BlockSpec((bb, X_WIDTH), lambda i: (i, 0)), mem_spec, mem_spec,
                  pl.BlockSpec((X_WIDTH, X_WIDTH), lambda i: (0, 0))],
        out_specs=pl.BlockSpec((bb, X_WIDTH), lambda i: (i, 0)),
        out_shape=jax.ShapeDtypeStruct((nbatch, X_WIDTH), F32),
        compiler_params=_cparams(("parallel",)),
        name="xattn_decode",
    )(q, mem_k, mem_v, ones_bd)


def _hgrn_gates(hq, hf, loglb, log1mlb):
    ls = jnp.minimum(hf, 0.0) - jnp.log1p(jnp.exp(-jnp.abs(hf)))
    b = log1mlb + ls
    lf = jnp.maximum(loglb, b) + jnp.log1p(jnp.exp(-jnp.abs(loglb - b)))
    return _silu(hq), lf, _neg_expm1(lf)


def _hgrn_prompt_kernel(hq_ref, hf_ref, hi_ref, hg_ref, loglb_ref, log1mlb_ref, gain_ref,
                        ones_ref, bdm_ref, ob_ref, st_ref, st_scr, q_scr, k_scr, cum_scr, o_scr):
    i = pl.program_id(1)
    tt = hq_ref.shape[0]
    c = HG_SUB

    @pl.when(i == 0)
    def _():
        st_scr[...] = jnp.zeros_like(st_scr)

    qs, lf, kk = _hgrn_gates(hq_ref[...], hf_ref[...], loglb_ref[...], log1mlb_ref[...])
    row = lax.broadcasted_iota(jnp.int32, (tt, HG_WIDTH), 0) & (c - 1)
    cum = lf
    sh = 1
    while sh < c:
        cum = cum + jnp.where(row >= sh, pltpu.roll(cum, sh, 0), 0.0)
        sh *= 2
    q_scr[...] = qs
    k_scr[...] = kk
    cum_scr[...] = cum

    it = lax.broadcasted_iota(jnp.int32, (c, c, HG_WIDTH), 0)
    js = lax.broadcasted_iota(jnp.int32, (c, c, HG_WIDTH), 1)
    causal = it >= js
    ones_bd = ones_ref[...]
    bdm = bdm_ref[...]

    def chunk(ci, carry):
        r = pl.ds(pl.multiple_of(ci * c, c), c)
        cu = cum_scr[r, :]
        q = q_scr[r, :]
        k = k_scr[r, :]
        v = hi_ref[r, :]
        last = cu[c - 1:c, :]
        st = st_scr[...]
        o_inter = _bdot_nt(q * jnp.exp(cu), st)
        dec = jnp.exp(jnp.where(causal, cu[:, None, :] - cu[None, :, :], NEG))
        pr = (dec * q[:, None, :] * k[None, :, :]).reshape(c * c, HG_WIDTH)
        w = _seg_sum(pr, ones_bd).reshape(c, c, HG_WIDTH)
        o_scr[r, :] = o_inter + jnp.sum(w * v[None, :, :], axis=1)
        upd = _bdot_tn(v, k * jnp.exp(last - cu))
        st_scr[...] = st * jnp.exp(last) + upd * bdm
        return carry

    lax.fori_loop(0, tt // c, chunk, 0)

    o = o_scr[...]
    ms = jnp.dot(o * o, bdm, precision=lax.Precision.HIGHEST,
                 preferred_element_type=F32) * (1.0 / HEAD_DIM)
    ob_ref[...] = o * lax.rsqrt(ms + RMS_EPS) * gain_ref[...] * _silu(hg_ref[...])

    @pl.when(i == pl.num_programs(1) - 1)
    def _():
        st_ref[...] = st_scr[...]


def hgrn_prompt(proj, loglb, log1mlb, gain4, ones_bd, bdm, batch, seq, tt):
    nt = seq // tt
    base = (A_HEADS + 2 * A_KV_HEADS) * HEAD_DIM // HG_WIDTH

    def col(cblk):
        return pl.BlockSpec((tt, HG_WIDTH), lambda b, i: (b * nt + i, cblk))

    row_spec = pl.BlockSpec((1, HG_WIDTH), lambda b, i: (0, 0))
    mat_spec = pl.BlockSpec((HG_WIDTH, HG_WIDTH), lambda b, i: (0, 0))
    return pl.pallas_call(
        _hgrn_prompt_kernel,
        grid=(batch, nt),
        in_specs=[col(base), col(base + 1), col(base + 2), col(base + 3),
                  row_spec, row_spec, row_spec, mat_spec, mat_spec],
        out_specs=[pl.BlockSpec((tt, HG_WIDTH), lambda b, i: (b * nt + i, 0)),
                   pl.BlockSpec((None, HG_WIDTH, HG_WIDTH), lambda b, i: (b, 0, 0))],
        out_shape=[jax.ShapeDtypeStruct((batch * seq, HG_WIDTH), F32),
                   jax.ShapeDtypeStruct((batch, HG_WIDTH, HG_WIDTH), F32)],
        scratch_shapes=[pltpu.VMEM((HG_WIDTH, HG_WIDTH), F32),
                        pltpu.VMEM((tt, HG_WIDTH), F32),
                        pltpu.VMEM((tt, HG_WIDTH), F32),
                        pltpu.VMEM((tt, HG_WIDTH), F32),
                        pltpu.VMEM((tt, HG_WIDTH), F32)],
        compiler_params=_cparams(("parallel", "arbitrary")),
        name="hgrn_prompt",
    )(proj, proj, proj, proj, loglb, log1mlb, gain4, ones_bd, bdm)


def _hgrn_decode_kernel(hq_ref, hf_ref, hi_ref, hg_ref, loglb_ref, log1mlb_ref, gain_ref, s_ref,
                        ob_ref, so_ref):
    nb = hq_ref.shape[1]
    qs, lf, kk = _hgrn_gates(hq_ref[...], hf_ref[...], loglb_ref[...], log1mlb_ref[...])
    v = hi_ref[...]
    s = s_ref[...].reshape(HEAD_DIM, HEAD_DIM, nb)
    s_new = jnp.exp(lf)[:, None, :] * s + kk[:, None, :] * v[None, :, :]
    so_ref[...] = s_new.reshape(HEAD_DIM * HEAD_DIM, nb)
    o = jnp.sum(qs[:, None, :] * s_new, axis=0)
    ms = jnp.mean(o * o, axis=0, keepdims=True)
    ob_ref[...] = o * lax.rsqrt(ms + RMS_EPS) * gain_ref[...] * _silu(hg_ref[...])


def hgrn_decode(gates_t, loglb_t, log1mlb_t, gain_t, state_t):
    nb = gates_t.shape[1]

    def blk(off):
        return pl.BlockSpec((HEAD_DIM, nb), lambda h: (off * HG_HEADS + h, 0))

    par = pl.BlockSpec((HEAD_DIM, nb), lambda h: (h, 0))
    st = pl.BlockSpec((HEAD_DIM * HEAD_DIM, nb), lambda h: (h, 0))
    return pl.pallas_call(
        _hgrn_decode_kernel,
        grid=(HG_HEADS,),
        in_specs=[blk(0), blk(1), blk(2), blk(3), par, par,
                  pl.BlockSpec((HEAD_DIM, nb), lambda h: (0, 0)), st],
        out_specs=[par, st],
        out_shape=[jax.ShapeDtypeStruct((HG_WIDTH, nb), F32),
                   jax.ShapeDtypeStruct(state_t.shape, F32)],
        compiler_params=_cparams(("parallel",)),
        name="hgrn_decode",
    )(gates_t, gates_t, gates_t, gates_t, loglb_t, log1mlb_t, gain_t, state_t)


def _lru_gates(xc, wa_ref, ba_ref, wx_ref, bx_ref, lam_ref):
    r = jax.nn.sigmoid(_bdot(xc, wa_ref[...]) + ba_ref[...])
    gi = jax.nn.sigmoid(_bdot(xc, wx_ref[...]) + bx_ref[...])
    log_a = -LRU_C * r * _softplus(-lam_ref[...])
    a = jnp.exp(log_a)
    bterm = jnp.sqrt(_neg_expm1(2.0 * log_a)) * (gi * xc)
    return a, bterm


def _lru_prompt_kernel(lx_ref, lg_ref, cw_ref, cb_ref, wa_ref, ba_ref, wx_ref, bx_ref, lam_ref,
                       oc_ref, hl_ref, ext_scr, h_scr):
    i = pl.program_id(1)
    tt = lx_ref.shape[0]
    pad = 8

    @pl.when(i == 0)
    def _():
        ext_scr[0:pad, :] = jnp.zeros((pad, LRU_WIDTH), F32)
        h_scr[...] = jnp.zeros_like(h_scr)

    x = lx_ref[...]
    ext_scr[pad:pad + tt, :] = x
    xc = cb_ref[...] + cw_ref[CONV_W - 1:CONV_W, :] * x
    for j in range(CONV_W - 1):
        back = CONV_W - 1 - j
        xc = xc + cw_ref[j:j + 1, :] * ext_scr[pad - back:pad - back + tt, :]
    ext_scr[0:pad, :] = x[tt - pad:tt, :]

    a, bterm = _lru_gates(xc, wa_ref, ba_ref, wx_ref, bx_ref, lam_ref)
    row = lax.broadcasted_iota(jnp.int32, (tt, LRU_WIDTH), 0)
    sh = 1
    while sh < tt:
        keep = row >= sh
        b_s = jnp.where(keep, pltpu.roll(bterm, sh, 0), 0.0)
        a_s = jnp.where(keep, pltpu.roll(a, sh, 0), 1.0)
        bterm = a * b_s + bterm
        a = a * a_s
        sh *= 2
    h = a * h_scr[...] + bterm
    h_scr[...] = h[tt - 1:tt, :]
    oc_ref[...] = h * _gelu_tanh(lg_ref[...])

    @pl.when(i == pl.num_programs(1) - 1)
    def _():
        hl_ref[...] = h[tt - 1:tt, :]


def lru_prompt(proj, conv_w, conv_b, wa_bd, ba, wx_bd, bx, lam, batch, seq, tt):
    nt = seq // tt
    base = IN_COLS // LRU_WIDTH - 2

    def col(cblk):
        return pl.BlockSpec((tt, LRU_WIDTH), lambda b, i: (b * nt + i, cblk))

    row_spec = pl.BlockSpec((1, LRU_WIDTH), lambda b, i: (0, 0))
    mat_spec = pl.BlockSpec((LRU_WIDTH, LRU_WIDTH), lambda b, i: (0, 0))
    return pl.pallas_call(
        _lru_prompt_kernel,
        grid=(batch, nt),
        in_specs=[col(base), col(base + 1),
                  pl.BlockSpec((CONV_W, LRU_WIDTH), lambda b, i: (0, 0)), row_spec,
                  mat_spec, row_spec, mat_spec, row_spec, row_spec],
        out_specs=[pl.BlockSpec((tt, LRU_WIDTH), lambda b, i: (b * nt + i, 0)),
                   pl.BlockSpec((None, 1, LRU_WIDTH), lambda b, i: (b, 0, 0))],
        out_shape=[jax.ShapeDtypeStruct((batch * seq, LRU_WIDTH), F32),
                   jax.ShapeDtypeStruct((batch, 1, LRU_WIDTH), F32)],
        scratch_shapes=[pltpu.VMEM((tt + 8, LRU_WIDTH), F32),
                        pltpu.VMEM((1, LRU_WIDTH), F32)],
        compiler_params=_cparams(("parallel", "arbitrary")),
        name="lru_prompt",
    )(proj, proj, conv_w, conv_b, wa_bd, ba, wx_bd, bx, lam)


def _lru_decode_kernel(lx_ref, lg_ref, buf_ref, h0_ref, cw_ref, cb_ref, wa_ref, ba_ref, wx_ref,
                       bx_ref, lam_ref, oc_ref, hn_ref, nbuf_ref):
    x = lx_ref[...]
    buf = buf_ref[...]
    xc = cb_ref[...] + cw_ref[CONV_W - 1:CONV_W, :] * x
    for j in range(CONV_W - 1):
        xc = xc + cw_ref[j:j + 1, :] * buf[:, j * LRU_WIDTH:(j + 1) * LRU_WIDTH]
    a, bterm = _lru_gates(xc, wa_ref, ba_ref, wx_ref, bx_ref, lam_ref)
    h = a * h0_ref[...] + bterm
    hn_ref[...] = h
    oc_ref[...] = h * _gelu_tanh(lg_ref[...])
    nbuf_ref[...] = jnp.concatenate([buf[:, LRU_WIDTH:], x], axis=1)


def lru_decode(proj_d, conv_buf, h0, conv_w, conv_b, wa_bd, ba, wx_bd, bx, lam):
    nb = proj_d.shape[0]
    base = IN_COLS // LRU_WIDTH - 2
    row_spec = pl.BlockSpec((1, LRU_WIDTH), lambda i: (0, 0))
    mat_spec = pl.BlockSpec((LRU_WIDTH, LRU_WIDTH), lambda i: (0, 0))
    act = pl.BlockSpec((nb, LRU_WIDTH), lambda i: (0, 0))
    bufs = pl.BlockSpec((nb, (CONV_W - 1) * LRU_WIDTH), lambda i: (0, 0))
    return pl.pallas_call(
        _lru_decode_kernel,
        grid=(1,),
        in_specs=[pl.BlockSpec((nb, LRU_WIDTH), lambda i: (0, base)),
                  pl.BlockSpec((nb, LRU_WIDTH), lambda i: (0, base + 1)),
                  bufs, act, pl.BlockSpec((CONV_W, LRU_WIDTH), lambda i: (0, 0)), row_spec,
                  mat_spec, row_spec, mat_spec, row_spec, row_spec],
        out_specs=[act, act, bufs],
        out_shape=[jax.ShapeDtypeStruct((nb, LRU_WIDTH), F32),
                   jax.ShapeDtypeStruct((nb, LRU_WIDTH), F32),
                   jax.ShapeDtypeStruct((nb, (CONV_W - 1) * LRU_WIDTH), F32)],
        compiler_params=_cparams(("arbitrary",)),
        name="lru_decode",
    )(proj_d, proj_d, conv_buf, h0, conv_w, conv_b, wa_bd, ba, wx_bd, bx, lam)


def _xattn_prompt_kernel(x_ref, wq_ref, mk_ref, mv_ref, wo_ref, g_ref, b_ref, o_ref):
    x = x_ref[...]
    q = _bdot(x, wq_ref[...]).astype(BF16)
    mk = mk_ref[...].astype(BF16)
    mv = mv_ref[...].astype(BF16)
    outs = []
    for h in range(X_HEADS):
        sl = slice(h * HEAD_DIM, (h + 1) * HEAD_DIM)
        s = _bdot_nt(q[:, sl], mk[:, sl]) * SCALE
        m = jnp.max(s, -1, keepdims=True)
        p = jnp.exp(s - m)
        p = p / jnp.sum(p, -1, keepdims=True)
        outs.append(_bdot(p, mv[:, sl]))
    o = jnp.concatenate(outs, axis=1)
    o_ref[...] = _layer_norm(ALPHA * x + _bdot(o, wo_ref[...]), g_ref[...], b_ref[...])


def xattn_prompt(x, wq, mem_kv, wo, g, b, batch, seq, tt):
    nt = seq // tt
    return pl.pallas_call(
        _xattn_prompt_kernel,
        grid=(batch, nt),
        in_specs=[pl.BlockSpec((tt, D_MODEL), lambda bi, i: (bi * nt + i, 0)),
                  pl.BlockSpec((D_MODEL, X_WIDTH), lambda bi, i: (0, 0)),
                  pl.BlockSpec((N_MEM, X_WIDTH), lambda bi, i: (bi, 0)),
                  pl.BlockSpec((N_MEM, X_WIDTH), lambda bi, i: (bi, 1)),
                  pl.BlockSpec((X_WIDTH, D_MODEL), lambda bi, i: (0, 0)),
                  pl.BlockSpec((1, D_MODEL), lambda bi, i: (0, 0)),
                  pl.BlockSpec((1, D_MODEL), lambda bi, i: (0, 0))],
        out_specs=pl.BlockSpec((tt, D_MODEL), lambda bi, i: (bi * nt + i, 0)),
        out_shape=jax.ShapeDtypeStruct((batch * seq, D_MODEL), F32),
        compiler_params=_cparams(("parallel", "parallel")),
        name="xattn_prompt",
    )(x, wq, mem_kv, mem_kv, wo, g, b)


def _route(logits):
    lane = lax.broadcasted_iota(jnp.int32, logits.shape, 1)
    big = jnp.int32(ROUTER_LANES)
    ninf = -jnp.inf
    gl = jnp.where(lane < N_GROUPS, logits, ninf)
    gm = jnp.max(gl, -1, keepdims=True)
    g_val = 1.0 / jnp.sum(jnp.exp(gl - gm), -1, keepdims=True)
    g_idx = jnp.min(jnp.where(gl == gm, lane, big), -1, keepdims=True)
    lo = N_GROUPS + EXP_PER_GROUP * g_idx
    el = jnp.where((lane >= lo) & (lane < lo + EXP_PER_GROUP), logits, ninf)
    v1 = jnp.max(el, -1, keepdims=True)
    i1 = jnp.min(jnp.where(el == v1, lane, big), -1, keepdims=True)
    el2 = jnp.where(lane == i1, ninf, el)
    v2 = jnp.max(el2, -1, keepdims=True)
    i2 = jnp.min(jnp.where(el2 == v2, lane, big), -1, keepdims=True)
    e2 = jnp.exp(v2 - v1)
    w1 = g_val / (1.0 + e2)
    w2 = g_val * e2 / (1.0 + e2)
    return jnp.where(lane == i1, w1, 0.0) + jnp.where(lane == i2, w2, 0.0)


def _moe_dense_kernel(x_ref, wr_ref, br_ref, wg_ref, wu_ref, wd_ref, g_ref, b_ref, o_ref,
                      gate_scr, acc_scr):
    e = pl.program_id(1)

    @pl.when(e == 0)
    def _():
        logits = jnp.dot(x_ref[...], wr_ref[...], precision=lax.Precision.HIGHEST,
                         preferred_element_type=F32) + br_ref[...]
        gate_scr[...] = _route(logits)
        acc_scr[...] = jnp.zeros_like(acc_scr)

    xb = x_ref[...].astype(BF16)
    lane = lax.broadcasted_iota(jnp.int32, gate_scr.shape, 1)
    gcol = jnp.sum(jnp.where(lane == e + N_GROUPS, gate_scr[...], 0.0), -1, keepdims=True)
    hid = _silu(_bdot(xb, wg_ref[...])) * _bdot(xb, wu_ref[...])
    acc_scr[...] += _bdot(hid * gcol, wd_ref[...])

    @pl.when(e == pl.num_programs(1) - 1)
    def _():
        o_ref[...] = _layer_norm(ALPHA * x_ref[...] + acc_scr[...], g_ref[...], b_ref[...])


def moe_dense(x, wr, br, wg, wu, wd, g, b, layer, tm):
    m = x.shape[0]
    return pl.pallas_call(
        _moe_dense_kernel,
        grid=(m // tm, N_EXPERTS),
        in_specs=[pl.BlockSpec((tm, D_MODEL), lambda i, e: (i, 0)),
                  pl.BlockSpec((D_MODEL, ROUTER_LANES), lambda i, e: (0, 0)),
                  pl.BlockSpec((1, ROUTER_LANES), lambda i, e: (0, 0)),
                  pl.BlockSpec((None, None, D_MODEL, EXP_FF), lambda i, e: (layer, e, 0, 0)),
                  pl.BlockSpec((None, None, D_MODEL, EXP_FF), lambda i, e: (layer, e, 0, 0)),
                  pl.BlockSpec((None, None, EXP_FF, D_MODEL), lambda i, e: (layer, e, 0, 0)),
                  pl.BlockSpec((1, D_MODEL), lambda i, e: (0, 0)),
                  pl.BlockSpec((1, D_MODEL), lambda i, e: (0, 0))],
        out_specs=pl.BlockSpec((tm, D_MODEL), lambda i, e: (i, 0)),
        out_shape=jax.ShapeDtypeStruct((m, D_MODEL), F32),
        scratch_shapes=[pltpu.VMEM((tm, ROUTER_LANES), F32), pltpu.VMEM((tm, D_MODEL), F32)],
        compiler_params=_cparams(("parallel", "arbitrary")),
        name="moe_dense",
    )(x, wr, br, wg, wu, wd, g, b)


def _t5_bucket(dist):
    max_exact = N_BUCKETS // 2
    d = jnp.maximum(dist, 0)
    df = jnp.maximum(d, 1).astype(F32)
    log_b = max_exact + (jnp.log(df / max_exact) / math.log(MAX_DISTANCE / max_exact)
                         * (N_BUCKETS - max_exact)).astype(jnp.int32)
    return jnp.where(d < max_exact, d, jnp.minimum(log_b, N_BUCKETS - 1))


def _prompt_bias(rel_bias):
    qi = jnp.arange(WINDOW)[:, None]
    kj = jnp.arange(2 * WINDOW)[None, :]
    dist = qi + WINDOW - kj
    bias = rel_bias[_t5_bucket(dist)].astype(F32).transpose(2, 0, 1)
    valid = (dist >= 0) & (dist <= WINDOW)
    return jnp.where(valid[None], bias, NEG)


def _decode_bias(rel_bias):
    dist = WINDOW - jnp.arange(WINDOW + 1)
    bias = rel_bias[_t5_bucket(dist)].astype(F32)
    out = []
    for part in (bias[:WINDOW], jnp.broadcast_to(bias[WINDOW:], (WINDOW, A_HEADS))):
        for r in range(A_REP):
            out.append(jnp.concatenate([jnp.repeat(part[:, r:r + 1], HEAD_DIM, 1),
                                        jnp.repeat(part[:, A_REP + r:A_REP + r + 1], HEAD_DIM, 1)], 1))
    return jnp.stack(out)


def _block_ones(width):
    idx = jnp.arange(width) // HEAD_DIM
    return (idx[:, None] == idx[None, :])


def _block_diag(w):
    nblk, s, _ = w.shape
    eye = jnp.eye(nblk, dtype=w.dtype)
    return (eye[:, None, :, None] * w[:, :, None, :]).reshape(nblk * s, nblk * s)


def kernel(x_prompt, x_sample, mem_prompt, cache_win_k, cache_win_v, state_hgrn, state_conv, state_lru, cache_mem_k, cache_mem_v, rel_bias, hg_lb, w_in, attn_sink, hg_gain, conv_w, conv_b, lru_wa, lru_ba, lru_wx, lru_bx, lru_lam, w_out, ln1_g, ln1_b, x_wq, x_wk, x_wv, x_wo, ln2_g, ln2_b, r_gw, r_gb, r_ew, r_eb, e_wg, e_wu, e_wd, ln3_g, ln3_b):
    bp, seq, d = x_prompt.shape
    nd = x_sample.shape[0]
    depth = w_in.shape[0]

    lbs = jnp.cumsum(jax.nn.softmax(hg_lb.astype(F32), axis=0), axis=0)
    lbs = lbs - lbs[0]
    loglb = jnp.log(lbs)
    log1mlb = jnp.log1p(-lbs)
    gain4 = jnp.tile(hg_gain, (1, HG_HEADS))

    bias_p = _prompt_bias(rel_bias)
    bias_d = _decode_bias(rel_bias)
    ones128 = _block_ones(LANES).astype(BF16)
    ones256 = _block_ones(HG_WIDTH).astype(BF16)
    bdm256 = _block_ones(HG_WIDTH).astype(F32)

    w_in_b = w_in.astype(BF16)
    w_out_b = w_out.astype(BF16)
    wq_b = x_wq.astype(BF16)
    wkv_b = jnp.concatenate([x_wk, x_wv], axis=-1).astype(BF16)
    wo_b = x_wo.astype(BF16)
    wg_b = e_wg.astype(BF16)
    wu_b = e_wu.astype(BF16)
    wd_b = e_wd.astype(BF16)
    rew = r_ew.transpose(0, 2, 1, 3).reshape(depth, d, N_EXPERTS)
    wr = jnp.concatenate([r_gw, rew, jnp.zeros((depth, d, ROUTER_LANES - N_GROUPS - N_EXPERTS), F32)], -1)
    br = jnp.concatenate([r_gb, r_eb.reshape(depth, N_EXPERTS),
                          jnp.zeros((depth, ROUTER_LANES - N_GROUPS - N_EXPERTS), F32)], -1)

    a_w = A_HEADS * HEAD_DIM
    xp = x_prompt.reshape(bp * seq, d)
    xs = x_sample.reshape(nd, d)
    mem = mem_prompt.reshape(bp * N_MEM, d)
    ck = cache_win_k.reshape(depth, nd, WINDOW, A_KV_HEADS * HEAD_DIM)
    cv = cache_win_v.reshape(depth, nd, WINDOW, A_KV_HEADS * HEAD_DIM)
    cmk = cache_mem_k.reshape(depth, nd, N_MEM, X_WIDTH)
    cmv = cache_mem_v.reshape(depth, nd, N_MEM, X_WIDTH)

    p_wk, p_wv, p_s, p_cb, p_h, p_mk, p_mv = [], [], [], [], [], [], []
    s_wk, s_wv, s_s, s_cb, s_h = [], [], [], [], []
    for l in range(depth):
        row = lambda v: v[l].reshape(1, -1)
        wa_bd = _block_diag(lru_wa[l]).astype(BF16)
        wx_bd = _block_diag(lru_wx[l]).astype(BF16)
        lru_args = (conv_w[l], row(conv_b), wa_bd, row(lru_ba), wx_bd, row(lru_bx), row(lru_lam))
        wo_parts = [w_out_b[l, :a_w], w_out_b[l, a_w:a_w + HG_WIDTH], w_out_b[l, a_w + HG_WIDTH:]]

        proj = matmul(xp, w_in_b[l], 512, 768)
        oa = attn_prompt(proj, attn_sink[l], bias_p, bp, seq)
        ob, st = hgrn_prompt(proj, row(loglb), row(log1mlb), row(gain4), ones256, bdm256, bp, seq, 256)
        oc, hl = lru_prompt(proj, *lru_args, bp, seq, 256)
        xp = proj_res_ln(xp, [oa, ob, oc], wo_parts, row(ln1_g), row(ln1_b), 512)
        mkv = matmul(mem, wkv_b[l], 256, 512)
        xp = xattn_prompt(xp, wq_b[l], mkv, wo_b[l], row(ln2_g), row(ln2_b), bp, seq, 512)
        xp = moe_dense(xp, wr[l], br[l:l + 1], wg_b, wu_b, wd_b, row(ln3_g), row(ln3_b), l, 512)

        proj3 = proj.reshape(bp, seq, IN_COLS)
        p_wk.append(proj3[:, seq - WINDOW:, a_w:a_w + LANES].reshape(bp, WINDOW, A_KV_HEADS, HEAD_DIM))
        p_wv.append(proj3[:, seq - WINDOW:, a_w + LANES:a_w + 2 * LANES].reshape(bp, WINDOW, A_KV_HEADS, HEAD_DIM))
        st5 = st.reshape(bp, HG_HEADS, HEAD_DIM, HG_HEADS, HEAD_DIM)
        p_s.append(jnp.stack([st5[:, h, :, h, :] for h in range(HG_HEADS)], 1).transpose(0, 1, 3, 2))
        p_cb.append(proj3[:, seq - (CONV_W - 1):, IN_COLS - 2 * LRU_WIDTH:IN_COLS - LRU_WIDTH])
        p_h.append(hl.reshape(bp, LRU_WIDTH))
        p_mk.append(mkv[:, :X_WIDTH].reshape(bp, N_MEM, X_HEADS, HEAD_DIM))
        p_mv.append(mkv[:, X_WIDTH:].reshape(bp, N_MEM, X_HEADS, HEAD_DIM))

        projd = matmul(xs, w_in_b[l], nd, 768)
        oa, nk, nv = attn_decode(projd, attn_sink[l], bias_d, ones128, ck[l], cv[l], 8)
        gates_t = projd[:, a_w + 2 * LANES:a_w + 2 * LANES + 4 * HG_WIDTH].T
        bc = lambda v: jnp.broadcast_to(v[:, None], (v.shape[0], nd))
        state_t = state_hgrn[l].reshape(nd, -1).T
        ob_t, ns_t = hgrn_decode(gates_t, bc(loglb[l]), bc(log1mlb[l]), bc(hg_gain[l]), state_t)
        oc, nh, nbuf = lru_decode(projd, state_conv[l].reshape(nd, -1), state_lru[l], *lru_args)
        xs = proj_res_ln(xs, [oa, ob_t.T, oc], wo_parts, row(ln1_g), row(ln1_b), nd)
        qd = matmul(xs, wq_b[l], nd, X_WIDTH)
        od = xattn_decode(qd, cmk[l], cmv[l], ones256, 8)
        xs = proj_res_ln(xs, [od], [wo_b[l]], row(ln2_g), row(ln2_b), nd)
        xs = moe_dense(xs, wr[l], br[l:l + 1], wg_b, wu_b, wd_b, row(ln3_g), row(ln3_b), l, nd)

        s_wk.append(nk.reshape(nd, WINDOW, A_KV_HEADS, HEAD_DIM))
        s_wv.append(nv.reshape(nd, WINDOW, A_KV_HEADS, HEAD_DIM))
        s_s.append(ns_t.T.reshape(nd, HG_HEADS, HEAD_DIM, HEAD_DIM))
        s_cb.append(nbuf.reshape(nd, CONV_W - 1, LRU_WIDTH))
        s_h.append(nh)

    return (xp.reshape(bp, seq, d), xs.reshape(nd, 1, d),
            jnp.stack(p_wk), jnp.stack(p_wv), jnp.stack(p_s), jnp.stack(p_cb), jnp.stack(p_h),
            jnp.stack(p_mk), jnp.stack(p_mv),
            jnp.stack(s_wk), jnp.stack(s_wv), jnp.stack(s_s), jnp.stack(s_cb), jnp.stack(s_h))
```

```python
import functools
import math

import jax
import jax.numpy as jnp
from jax import lax
from jax.experimental import pallas as pl
from jax.experimental.pallas import tpu as pltpu

F32 = jnp.float32
BF16 = jnp.bfloat16

D_MODEL = 1024
DEPTH = 4
HEAD_DIM = 64
A_HEADS = 8
A_KV_HEADS = 2
A_REP = A_HEADS // A_KV_HEADS
WINDOW = 128
N_BUCKETS = 32
MAX_DISTANCE = 128
HG_WIDTH = 256
HG_HEADS = 4
HG_CHUNK = 64
HG_TB = 16
HG_MROWS = 16
LRU_WIDTH = 256
LRU_BLOCKS = 4
CONV_W = 4
LRU_C = 8.0
N_MEM = 256
X_HEADS = 4
X_WIDTH = X_HEADS * HEAD_DIM
N_GROUPS = 4
EXP_PER_GROUP = 4
N_EXPERTS = N_GROUPS * EXP_PER_GROUP
EXP_FF = D_MODEL // 4
ALPHA = (2 * DEPTH) ** 0.25
LN_EPS = 1e-5
RMS_EPS = 1e-6
IN_COLS = 2304
SCALE = HEAD_DIM ** -0.5
NEG = -1e30
LANES = 128
ROUTER_LANES = 128
XG_WIDTH = D_MODEL + ROUTER_LANES
MOE_TM = 512
VMEM_LIMIT = 48 * 1024 * 1024


def _cparams(sem):
    return pltpu.CompilerParams(dimension_semantics=sem, vmem_limit_bytes=VMEM_LIMIT)


def _bdot(a, b):
    return jnp.dot(a.astype(BF16), b.astype(BF16), preferred_element_type=F32)


def _bdot_nt(a, b):
    return lax.dot_general(a.astype(BF16), b.astype(BF16), (((1,), (1,)), ((), ())),
                           preferred_element_type=F32)


def _bdot_tn(a, b):
    return lax.dot_general(a.astype(BF16), b.astype(BF16), (((0,), (0,)), ((), ())),
                           preferred_element_type=F32)


def _rb(x):
    return x.astype(BF16).astype(F32)


def _silu(x):
    return x * jax.nn.sigmoid(x)


def _neg_expm1(x):
    return -jnp.tanh(0.5 * x) * (jnp.exp(x) + 1.0)


def _softplus(x):
    return jnp.maximum(x, 0.0) + jnp.log1p(jnp.exp(-jnp.abs(x)))


def _gelu_tanh(x):
    return 0.5 * x * (1.0 + jnp.tanh(math.sqrt(2.0 / math.pi) * (x + 0.044715 * (x * x * x))))


def _layer_norm(y, g, b):
    mu = jnp.mean(y, -1, keepdims=True)
    yc = y - mu
    var = jnp.mean(yc * yc, -1, keepdims=True)
    return yc * lax.rsqrt(var + LN_EPS) * g + b


def _mm_kernel(x_ref, w_ref, o_ref):
    o_ref[...] = _bdot(x_ref[...], w_ref[...])


def matmul(x, w, tm, tn, rows=None):
    m, k = x.shape
    m = rows or m
    n = w.shape[1]
    return pl.pallas_call(
        _mm_kernel,
        grid=(m // tm, n // tn),
        in_specs=[pl.BlockSpec((tm, k), lambda i, j: (i, 0)),
                  pl.BlockSpec((k, tn), lambda i, j: (0, j))],
        out_specs=pl.BlockSpec((tm, tn), lambda i, j: (i, j)),
        out_shape=jax.ShapeDtypeStruct((m, n), F32),
        compiler_params=_cparams(("parallel", "parallel")),
        name="matmul",
    )(x, w)


def _proj_res_ln_kernel(n_in, x_ref, *refs):
    a_refs = refs[:n_in]
    w_refs = refs[n_in:2 * n_in]
    g_ref, b_ref, o_ref = refs[2 * n_in:]
    y = ALPHA * x_ref[...]
    for a_ref, w_ref in zip(a_refs, w_refs):
        y = y + _bdot(a_ref[...], w_ref[...])
    o_ref[...] = _layer_norm(y, g_ref[...], b_ref[...])


def proj_res_ln(x, a_list, w_list, g, b, tm):
    d = x.shape[1]
    m = a_list[0].shape[0]
    n_in = len(a_list)
    in_specs = [pl.BlockSpec((tm, d), lambda i: (i, 0))]
    in_specs += [pl.BlockSpec((tm, a.shape[1]), lambda i: (i, 0)) for a in a_list]
    in_specs += [pl.BlockSpec(w.shape, lambda i: (0, 0)) for w in w_list]
    in_specs += [pl.BlockSpec((1, d), lambda i: (0, 0))] * 2
    return pl.pallas_call(
        functools.partial(_proj_res_ln_kernel, n_in),
        grid=(m // tm,),
        in_specs=in_specs,
        out_specs=pl.BlockSpec((tm, d), lambda i: (i, 0)),
        out_shape=jax.ShapeDtypeStruct((m, d), F32),
        compiler_params=_cparams(("parallel",)),
        name="proj_res_ln",
    )(x, *a_list, *w_list, g, b)


def _attn_prompt_kernel(sink_ref, q_ref, kc_ref, kp_ref, vc_ref, vp_ref, bias_ref, o_ref):
    n = pl.program_id(1)
    col = lax.broadcasted_iota(jnp.int32, (WINDOW, 2 * WINDOW), 1)
    first = jnp.where((n == 0) & (col < WINDOW), NEG, 0.0)
    kk = jnp.concatenate([kp_ref[...], kc_ref[...]], axis=0).astype(BF16)
    vv = jnp.concatenate([vp_ref[...], vc_ref[...]], axis=0).astype(BF16)
    q = q_ref[...].astype(BF16)
    outs = []
    for h in range(A_HEADS):
        g = h // A_REP
        qh = q[:, h * HEAD_DIM:(h + 1) * HEAD_DIM]
        kg = kk[:, g * HEAD_DIM:(g + 1) * HEAD_DIM]
        vg = vv[:, g * HEAD_DIM:(g + 1) * HEAD_DIM]
        s = _bdot_nt(qh, kg) * SCALE + bias_ref[h] + first
        sink = sink_ref[h]
        m = jnp.maximum(jnp.max(s, -1, keepdims=True), sink)
        p = jnp.exp(s - m)
        den = jnp.sum(p, -1, keepdims=True) + jnp.exp(sink - m)
        outs.append(_bdot(p / den, vg))
    o_ref[...] = jnp.concatenate(outs, axis=1)


def attn_prompt(proj, sink, bias, batch, seq):
    nb = seq // WINDOW
    qcol = 0
    kcol = (A_HEADS * HEAD_DIM) // LANES
    vcol = kcol + 1

    def cur(c):
        return lambda b, n: (b * nb + n, c)

    def prev(c):
        return lambda b, n: (b * nb + jnp.maximum(n - 1, 0), c)

    return pl.pallas_call(
        _attn_prompt_kernel,
        grid=(batch, nb),
        in_specs=[pl.BlockSpec(memory_space=pltpu.SMEM),
                  pl.BlockSpec((WINDOW, A_HEADS * HEAD_DIM), cur(qcol)),
                  pl.BlockSpec((WINDOW, LANES), cur(kcol)),
                  pl.BlockSpec((WINDOW, LANES), prev(kcol)),
                  pl.BlockSpec((WINDOW, LANES), cur(vcol)),
                  pl.BlockSpec((WINDOW, LANES), prev(vcol)),
                  pl.BlockSpec((A_HEADS, WINDOW, 2 * WINDOW), lambda b, n: (0, 0, 0))],
        out_specs=pl.BlockSpec((WINDOW, A_HEADS * HEAD_DIM), cur(0)),
        out_shape=jax.ShapeDtypeStruct((batch * seq, A_HEADS * HEAD_DIM), F32),
        compiler_params=_cparams(("parallel", "parallel")),
        name="attn_prompt",
    )(sink, proj, proj, proj, proj, proj, bias)


def _seg_sum(x2d, ones_bd):
    hi = x2d.astype(BF16)
    lo = (x2d - hi.astype(F32)).astype(BF16)
    return (jnp.dot(hi, ones_bd, preferred_element_type=F32)
            + jnp.dot(lo, ones_bd, preferred_element_type=F32))


def _attn_decode_kernel(sink_ref, q_ref, kn_ref, vn_ref, ck_ref, cv_ref, bias_ref, ones_ref,
                        o_ref, ok_ref, ov_ref):
    bb = q_ref.shape[0]
    ck = ck_ref[...]
    cv = cv_ref[...]
    kn = kn_ref[...]
    vn = vn_ref[...]
    q = q_ref[...]
    ones_bd = ones_ref[...]
    ckb, cvb, knb, vnb = _rb(ck), _rb(cv), _rb(kn), _rb(vn)
    lane = lax.broadcasted_iota(jnp.int32, (1, LANES), 1)
    pieces = [None] * A_HEADS
    for r in range(A_REP):
        h0, h1 = r, A_REP + r
        qp = jnp.concatenate([q[:, h0 * HEAD_DIM:(h0 + 1) * HEAD_DIM],
                              q[:, h1 * HEAD_DIM:(h1 + 1) * HEAD_DIM]], axis=1)
        qp = _rb(qp)
        prod = (ckb * qp[:, None, :]).reshape(bb * WINDOW, LANES)
        s = _seg_sum(prod, ones_bd).reshape(bb, WINDOW, LANES) * SCALE + bias_ref[r][None]
        sn = _seg_sum(knb * qp, ones_bd) * SCALE + bias_ref[A_REP + r][0:1, :]
        sink = jnp.where(lane < HEAD_DIM, sink_ref[h0], sink_ref[h1])
        m = jnp.maximum(jnp.maximum(jnp.max(s, axis=1), sn), sink)
        p = jnp.exp(s - m[:, None, :])
        pn = jnp.exp(sn - m)
        den = jnp.sum(p, axis=1) + pn + jnp.exp(sink - m)
        o = jnp.sum(_rb(p / den[:, None, :]) * cvb, axis=1) + _rb(pn / den) * vnb
        pieces[h0] = o[:, :HEAD_DIM]
        pieces[h1] = o[:, HEAD_DIM:]
    o_ref[...] = jnp.concatenate(pieces, axis=1)
    ok_ref[:, 0:WINDOW - 1, :] = ck[:, 1:WINDOW, :]
    ok_ref[:, WINDOW - 1, :] = kn
    ov_ref[:, 0:WINDOW - 1, :] = cv[:, 1:WINDOW, :]
    ov_ref[:, WINDOW - 1, :] = vn


def attn_decode(proj_d, sink, bias_d, ones_bd, cache_k, cache_v, bb):
    nbatch = proj_d.shape[0]
    kcol = (A_HEADS * HEAD_DIM) // LANES
    cache_spec = pl.BlockSpec((bb, WINDOW, LANES), lambda i: (i, 0, 0))
    return pl.pallas_call(
        _attn_decode_kernel,
        grid=(nbatch // bb,),
        in_specs=[pl.BlockSpec(memory_space=pltpu.SMEM),
                  pl.BlockSpec((bb, A_HEADS * HEAD_DIM), lambda i: (i, 0)),
                  pl.BlockSpec((bb, LANES), lambda i: (i, kcol)),
                  pl.BlockSpec((bb, LANES), lambda i: (i, kcol + 1)),
                  cache_spec, cache_spec,
                  pl.BlockSpec((2 * A_REP, WINDOW, LANES), lambda i: (0, 0, 0)),
                  pl.BlockSpec((LANES, LANES), lambda i: (0, 0))],
        out_specs=[pl.BlockSpec((bb, A_HEADS * HEAD_DIM), lambda i: (i, 0)), cache_spec, cache_spec],
        out_shape=[jax.ShapeDtypeStruct((nbatch, A_HEADS * HEAD_DIM), F32),
                   jax.ShapeDtypeStruct(cache_k.shape, F32),
                   jax.ShapeDtypeStruct(cache_v.shape, F32)],
        compiler_params=_cparams(("parallel",)),
        name="attn_decode",
    )(sink, proj_d, proj_d, proj_d, cache_k, cache_v, bias_d, ones_bd)


def _xattn_decode_kernel(q_ref, mk_ref, mv_ref, ones_ref, o_ref):
    bb = q_ref.shape[0]
    q = _rb(q_ref[...])
    mk = _rb(mk_ref[...])
    prod = (mk * q[:, None, :]).reshape(bb * N_MEM, X_WIDTH)
    s = _seg_sum(prod, ones_ref[...]).reshape(bb, N_MEM, X_WIDTH) * SCALE
    m = jnp.max(s, axis=1, keepdims=True)
    p = jnp.exp(s - m)
    den = jnp.sum(p, axis=1, keepdims=True)
    o_ref[...] = jnp.sum(_rb(p / den) * _rb(mv_ref[...]), axis=1)


def xattn_decode(q, mem_k, mem_v, ones_bd, bb):
    nbatch = q.shape[0]
    mem_spec = pl.BlockSpec((bb, N_MEM, X_WIDTH), lambda i: (i, 0, 0))
    return pl.pallas_call(
        _xattn_decode_kernel,
        grid=(nbatch // bb,),
        in_specs=[pl.BlockSpec((bb, X_WIDTH), lambda i: (i, 0)), mem_spec, mem_spec,
                  pl.BlockSpec((X_WIDTH, X_WIDTH), lambda i: (0, 0))],
        out_specs=pl.BlockSpec((bb, X_WIDTH), lambda i: (i, 0)),
        out_shape=jax.ShapeDtypeStruct((nbatch, X_WIDTH), F32),
        compiler_params=_cparams(("parallel",)),
        name="xattn_decode",
    )(q, mem_k, mem_v, ones_bd)


def _hgrn_gates(hq, hf, loglb, log1mlb):
    ls = jnp.minimum(hf, 0.0) - jnp.log1p(jnp.exp(-jnp.abs(hf)))
    b = log1mlb + ls
    lf = jnp.maximum(loglb, b) + jnp.log1p(jnp.exp(-jnp.abs(loglb - b)))
    return _silu(hq), lf, _neg_expm1(lf)


def _hgrn_prompt_kernel(hq_ref, hf_ref, hi_ref, hg_ref, loglb_ref, log1mlb_ref, gain_ref,
                        bdm_ref, hm_ref, ob_ref, st_ref, st_scr, q_scr, k_scr, cum_scr, o_scr):
    i = pl.program_id(1)
    tt = hq_ref.shape[0]
    c = HG_CHUNK
    tb = HG_TB

    @pl.when(i == 0)
    def _():
        st_scr[...] = jnp.zeros_like(st_scr)

    qs, lf, kk = _hgrn_gates(hq_ref[...], hf_ref[...], loglb_ref[...], log1mlb_ref[...])
    row = lax.broadcasted_iota(jnp.int32, (tt, HG_WIDTH), 0) & (c - 1)
    cum = lf
    sh = 1
    while sh < c:
        cum = cum + jnp.where(row >= sh, pltpu.roll(cum, sh, 0), 0.0)
        sh *= 2
    q_scr[...] = qs
    k_scr[...] = kk
    cum_scr[...] = cum

    bdm = bdm_ref[...]
    hmask = hm_ref[...]

    def chunk(ci, carry):
        r0 = pl.multiple_of(ci * c, c)
        r = pl.ds(r0, c)
        cu = cum_scr[r, :]
        q = q_scr[r, :]
        k = k_scr[r, :]
        v = hi_ref[r, :]
        vb = v.astype(BF16)
        qb = _rb(q)
        last = cu[c - 1:c, :]
        st = st_scr[...]
        o_inter = _bdot_nt(q * jnp.exp(cu), st)
        for j in range(c // tb):
            ns = tb * (j + 1)
            ti = lax.broadcasted_iota(jnp.int32, (tb, ns, HG_WIDTH), 0) + tb * j
            si = lax.broadcasted_iota(jnp.int32, (tb, ns, HG_WIDTH), 1)
            cut = cu[tb * j:tb * (j + 1), :]
            dec = jnp.exp(jnp.where(ti >= si, cut[:, None, :] - cu[None, :ns, :], NEG))
            a2 = (dec * k[None, :ns, :]).astype(BF16)
            q4 = (qb[tb * j:tb * (j + 1), None, :] * hmask[None, :, :]).astype(BF16)
            att = lax.dot_general(q4, a2, (((2,), (2,)), ((0,), (0,))),
                                  preferred_element_type=F32)
            w = jnp.dot(att.reshape(tb * HG_MROWS, ns).astype(BF16), vb[:ns, :],
                        preferred_element_type=F32).reshape(tb, HG_MROWS, HG_WIDTH)
            o_intra = jnp.sum(w * hmask[None, :, :], axis=1)
            o_scr[pl.ds(r0 + tb * j, tb), :] = o_intra + o_inter[tb * j:tb * (j + 1), :]
        upd = _bdot_tn(v, k * jnp.exp(last - cu))
        st_scr[...] = st * jnp.exp(last) + upd * bdm
        return carry

    lax.fori_loop(0, tt // c, chunk, 0)

    o = o_scr[...]
    ms = jnp.dot(o * o, bdm, precision=lax.Precision.HIGHEST,
                 preferred_element_type=F32) * (1.0 / HEAD_DIM)
    ob_ref[...] = o * lax.rsqrt(ms + RMS_EPS) * gain_ref[...] * _silu(hg_ref[...])

    @pl.when(i == pl.num_programs(1) - 1)
    def _():
        st_ref[...] = st_scr[...]


def hgrn_prompt(proj, loglb, log1mlb, gain4, bdm, hmask, batch, seq, tt):
    nt = seq // tt
    base = (A_HEADS + 2 * A_KV_HEADS) * HEAD_DIM // HG_WIDTH

    def col(cblk):
        return pl.BlockSpec((tt, HG_WIDTH), lambda b, i: (b * nt + i, cblk))

    row_spec = pl.BlockSpec((1, HG_WIDTH), lambda b, i: (0, 0))
    mat_spec = pl.BlockSpec((HG_WIDTH, HG_WIDTH), lambda b, i: (0, 0))
    return pl.pallas_call(
        _hgrn_prompt_kernel,
        grid=(batch, nt),
        in_specs=[col(base), col(base + 1), col(base + 2), col(base + 3),
                  row_spec, row_spec, row_spec, mat_spec,
                  pl.BlockSpec((HG_MROWS, HG_WIDTH), lambda b, i: (0, 0))],
        out_specs=[pl.BlockSpec((tt, HG_WIDTH), lambda b, i: (b * nt + i, 0)),
                   pl.BlockSpec((None, HG_WIDTH, HG_WIDTH), lambda b, i: (b, 0, 0))],
        out_shape=[jax.ShapeDtypeStruct((batch * seq, HG_WIDTH), F32),
                   jax.ShapeDtypeStruct((batch, HG_WIDTH, HG_WIDTH), F32)],
        scratch_shapes=[pltpu.VMEM((HG_WIDTH, HG_WIDTH), F32),
                        pltpu.VMEM((tt, HG_WIDTH), F32),
                        pltpu.VMEM((tt, HG_WIDTH), F32),
                        pltpu.VMEM((tt, HG_WIDTH), F32),
                        pltpu.VMEM((tt, HG_WIDTH), F32)],
        compiler_params=_cparams(("parallel", "arbitrary")),
        name="hgrn_prompt",
    )(proj, proj, proj, proj, loglb, log1mlb, gain4, bdm, hmask)


def _hgrn_decode_kernel(hq_ref, hf_ref, hi_ref, hg_ref, loglb_ref, log1mlb_ref, gain_ref, s_ref,
                        ob_ref, so_ref):
    nb = hq_ref.shape[1]
    qs, lf, kk = _hgrn_gates(hq_ref[...], hf_ref[...], loglb_ref[...], log1mlb_ref[...])
    v = hi_ref[...]
    f = jnp.exp(lf)
    s = s_ref[...].reshape(HEAD_DIM, HEAD_DIM, nb)
    att = jnp.sum(_rb(qs) * _rb(kk), axis=0, keepdims=True)
    o = _rb(att) * _rb(v) + jnp.sum(_rb(qs * f)[:, None, :] * _rb(s), axis=0)
    s_new = f[:, None, :] * s + _rb(kk)[:, None, :] * _rb(v)[None, :, :]
    so_ref[...] = s_new.reshape(HEAD_DIM * HEAD_DIM, nb)
    ms = jnp.mean(o * o, axis=0, keepdims=True)
    ob_ref[...] = o * lax.rsqrt(ms + RMS_EPS) * gain_ref[...] * _silu(hg_ref[...])


def hgrn_decode(gates_t, loglb_t, log1mlb_t, gain_t, state_t):
    nb = gates_t.shape[1]

    def blk(off):
        return pl.BlockSpec((HEAD_DIM, nb), lambda h: (off * HG_HEADS + h, 0))

    par = pl.BlockSpec((HEAD_DIM, nb), lambda h: (h, 0))
    st = pl.BlockSpec((HEAD_DIM * HEAD_DIM, nb), lambda h: (h, 0))
    return pl.pallas_call(
        _hgrn_decode_kernel,
        grid=(HG_HEADS,),
        in_specs=[blk(0), blk(1), blk(2), blk(3), par, par,
                  pl.BlockSpec((HEAD_DIM, nb), lambda h: (0, 0)), st],
        out_specs=[par, st],
        out_shape=[jax.ShapeDtypeStruct((HG_WIDTH, nb), F32),
                   jax.ShapeDtypeStruct(state_t.shape, F32)],
        compiler_params=_cparams(("parallel",)),
        name="hgrn_decode",
    )(gates_t, gates_t, gates_t, gates_t, loglb_t, log1mlb_t, gain_t, state_t)


def _lru_gates(xc, wa_ref, ba_ref, wx_ref, bx_ref, lam_ref):
    r = jax.nn.sigmoid(_bdot(xc, wa_ref[...]) + ba_ref[...])
    gi = jax.nn.sigmoid(_bdot(xc, wx_ref[...]) + bx_ref[...])
    log_a = -LRU_C * r * _softplus(-lam_ref[...])
    a = jnp.exp(log_a)
    bterm = jnp.sqrt(_neg_expm1(2.0 * log_a)) * (gi * xc)
    return a, bterm


def _lru_prompt_kernel(lx_ref, lg_ref, cw_ref, cb_ref, wa_ref, ba_ref, wx_ref, bx_ref, lam_ref,
                       oc_ref, hl_ref, ext_scr, h_scr):
    i = pl.program_id(1)
    tt = lx_ref.shape[0]
    pad = 8

    @pl.when(i == 0)
    def _():
        ext_scr[0:pad, :] = jnp.zeros((pad, LRU_WIDTH), F32)
        h_scr[...] = jnp.zeros_like(h_scr)

    x = lx_ref[...]
    ext_scr[pad:pad + tt, :] = x
    xc = cb_ref[...] + cw_ref[CONV_W - 1:CONV_W, :] * x
    for j in range(CONV_W - 1):
        back = CONV_W - 1 - j
        xc = xc + cw_ref[j:j + 1, :] * ext_scr[pad - back:pad - back + tt, :]
    ext_scr[0:pad, :] = x[tt - pad:tt, :]

    a, bterm = _lru_gates(xc, wa_ref, ba_ref, wx_ref, bx_ref, lam_ref)
    row = lax.broadcasted_iota(jnp.int32, (tt, LRU_WIDTH), 0)
    sh = 1
    while sh < tt:
        keep = row >= sh
        b_s = jnp.where(keep, pltpu.roll(bterm, sh, 0), 0.0)
        a_s = jnp.where(keep, pltpu.roll(a, sh, 0), 1.0)
        bterm = a * b_s + bterm
        a = a * a_s
        sh *= 2
    h = a * h_scr[...] + bterm
    h_scr[...] = h[tt - 1:tt, :]
    oc_ref[...] = h * _gelu_tanh(lg_ref[...])

    @pl.when(i == pl.num_programs(1) - 1)
    def _():
        hl_ref[...] = h[tt - 1:tt, :]


def lru_prompt(proj, conv_w, conv_b, wa_bd, ba, wx_bd, bx, lam, batch, seq, tt):
    nt = seq // tt
    base = IN_COLS // LRU_WIDTH - 2

    def col(cblk):
        return pl.BlockSpec((tt, LRU_WIDTH), lambda b, i: (b * nt + i, cblk))

    row_spec = pl.BlockSpec((1, LRU_WIDTH), lambda b, i: (0, 0))
    mat_spec = pl.BlockSpec((LRU_WIDTH, LRU_WIDTH), lambda b, i: (0, 0))
    return pl.pallas_call(
        _lru_prompt_kernel,
        grid=(batch, nt),
        in_specs=[col(base), col(base + 1),
                  pl.BlockSpec((CONV_W, LRU_WIDTH), lambda b, i: (0, 0)), row_spec,
                  mat_spec, row_spec, mat_spec, row_spec, row_spec],
        out_specs=[pl.BlockSpec((tt, LRU_WIDTH), lambda b, i: (b * nt + i, 0)),
                   pl.BlockSpec((None, 1, LRU_WIDTH), lambda b, i: (b, 0, 0))],
        out_shape=[jax.ShapeDtypeStruct((batch * seq, LRU_WIDTH), F32),
                   jax.ShapeDtypeStruct((batch, 1, LRU_WIDTH), F32)],
        scratch_shapes=[pltpu.VMEM((tt + 8, LRU_WIDTH), F32),
                        pltpu.VMEM((1, LRU_WIDTH), F32)],
        compiler_params=_cparams(("parallel", "arbitrary")),
        name="lru_prompt",
    )(proj, proj, conv_w, conv_b, wa_bd, ba, wx_bd, bx, lam)


def _lru_decode_kernel(lx_ref, lg_ref, buf_ref, h0_ref, cw_ref, cb_ref, wa_ref, ba_ref, wx_ref,
                       bx_ref, lam_ref, oc_ref, hn_ref, nbuf_ref):
    x = lx_ref[...]
    buf = buf_ref[...]
    xc = cb_ref[...] + cw_ref[CONV_W - 1:CONV_W, :] * x
    for j in range(CONV_W - 1):
        xc = xc + cw_ref[j:j + 1, :] * buf[:, j * LRU_WIDTH:(j + 1) * LRU_WIDTH]
    a, bterm = _lru_gates(xc, wa_ref, ba_ref, wx_ref, bx_ref, lam_ref)
    h = a * h0_ref[...] + bterm
    hn_ref[...] = h
    oc_ref[...] = h * _gelu_tanh(lg_ref[...])
    nbuf_ref[...] = jnp.concatenate([buf[:, LRU_WIDTH:], x], axis=1)


def lru_decode(proj_d, conv_buf, h0, conv_w, conv_b, wa_bd, ba, wx_bd, bx, lam):
    nb = proj_d.shape[0]
    base = IN_COLS // LRU_WIDTH - 2
    row_spec = pl.BlockSpec((1, LRU_WIDTH), lambda i: (0, 0))
    mat_spec = pl.BlockSpec((LRU_WIDTH, LRU_WIDTH), lambda i: (0, 0))
    act = pl.BlockSpec((nb, LRU_WIDTH), lambda i: (0, 0))
    bufs = pl.BlockSpec((nb, (CONV_W - 1) * LRU_WIDTH), lambda i: (0, 0))
    return pl.pallas_call(
        _lru_decode_kernel,
        grid=(1,),
        in_specs=[pl.BlockSpec((nb, LRU_WIDTH), lambda i: (0, base)),
                  pl.BlockSpec((nb, LRU_WIDTH), lambda i: (0, base + 1)),
                  bufs, act, pl.BlockSpec((CONV_W, LRU_WIDTH), lambda i: (0, 0)), row_spec,
                  mat_spec, row_spec, mat_spec, row_spec, row_spec],
        out_specs=[act, act, bufs],
        out_shape=[jax.ShapeDtypeStruct((nb, LRU_WIDTH), F32),
                   jax.ShapeDtypeStruct((nb, LRU_WIDTH), F32),
                   jax.ShapeDtypeStruct((nb, (CONV_W - 1) * LRU_WIDTH), F32)],
        compiler_params=_cparams(("arbitrary",)),
        name="lru_decode",
    )(proj_d, proj_d, conv_buf, h0, conv_w, conv_b, wa_bd, ba, wx_bd, bx, lam)


def _xattn_prompt_kernel(x_ref, wq_ref, mk_ref, mv_ref, wo_ref, g_ref, b_ref, wr_ref, br_ref, o_ref):
    x = x_ref[...]
    q = _bdot(x, wq_ref[...]).astype(BF16)
    mk = mk_ref[...].astype(BF16)
    mv = mv_ref[...].astype(BF16)
    outs = []
    for h in range(X_HEADS):
        sl = slice(h * HEAD_DIM, (h + 1) * HEAD_DIM)
        s = _bdot_nt(q[:, sl], mk[:, sl]) * SCALE
        m = jnp.max(s, -1, keepdims=True)
        p = jnp.exp(s - m)
        p = p / jnp.sum(p, -1, keepdims=True)
        outs.append(_bdot(p, mv[:, sl]))
    o = jnp.concatenate(outs, axis=1)
    y = _layer_norm(ALPHA * x + _bdot(o, wo_ref[...]), g_ref[...], b_ref[...])
    logits = _bdot(y, wr_ref[...]) + br_ref[...]
    gate, g_idx = _route(logits)
    lane = lax.broadcasted_iota(jnp.int32, gate.shape, 1)
    o_ref[:, :D_MODEL] = y
    o_ref[:, D_MODEL:] = jnp.where(lane == 0, g_idx.astype(F32), gate)


def xattn_prompt(x, wq, mem_kv, wo, g, b, wr, br, batch, seq, tt):
    nt = seq // tt
    const = lambda bi, i: (0, 0)
    return pl.pallas_call(
        _xattn_prompt_kernel,
        grid=(batch, nt),
        in_specs=[pl.BlockSpec((tt, D_MODEL), lambda bi, i: (bi * nt + i, 0)),
                  pl.BlockSpec((D_MODEL, X_WIDTH), const),
                  pl.BlockSpec((N_MEM, X_WIDTH), lambda bi, i: (bi, 0)),
                  pl.BlockSpec((N_MEM, X_WIDTH), lambda bi, i: (bi, 1)),
                  pl.BlockSpec((X_WIDTH, D_MODEL), const),
                  pl.BlockSpec((1, D_MODEL), const),
                  pl.BlockSpec((1, D_MODEL), const),
                  pl.BlockSpec((D_MODEL, ROUTER_LANES), const),
                  pl.BlockSpec((1, ROUTER_LANES), const)],
        out_specs=pl.BlockSpec((tt, XG_WIDTH), lambda bi, i: (bi * nt + i, 0)),
        out_shape=jax.ShapeDtypeStruct((batch * seq, XG_WIDTH), F32),
        compiler_params=_cparams(("parallel", "parallel")),
        name="xattn_prompt",
    )(x, wq, mem_kv, mem_kv, wo, g, b, wr, br)


def _route(logits):
    lane = lax.broadcasted_iota(jnp.int32, logits.shape, 1)
    big = jnp.int32(ROUTER_LANES)
    ninf = -jnp.inf
    gl = jnp.where(lane < N_GROUPS, logits, ninf)
    gm = jnp.max(gl, -1, keepdims=True)
    g_val = 1.0 / jnp.sum(jnp.exp(gl - gm), -1, keepdims=True)
    g_idx = jnp.min(jnp.where(gl == gm, lane, big), -1, keepdims=True)
    lo = N_GROUPS + EXP_PER_GROUP * g_idx
    el = jnp.where((lane >= lo) & (lane < lo + EXP_PER_GROUP), logits, ninf)
    v1 = jnp.max(el, -1, keepdims=True)
    i1 = jnp.min(jnp.where(el == v1, lane, big), -1, keepdims=True)
    el2 = jnp.where(lane == i1, ninf, el)
    v2 = jnp.max(el2, -1, keepdims=True)
    i2 = jnp.min(jnp.where(el2 == v2, lane, big), -1, keepdims=True)
    e2 = jnp.exp(v2 - v1)
    w1 = g_val / (1.0 + e2)
    w2 = g_val * e2 / (1.0 + e2)
    return jnp.where(lane == i1, w1, 0.0) + jnp.where(lane == i2, w2, 0.0), g_idx


def _moe_dense_kernel(x_ref, wr_ref, br_ref, wg_ref, wu_ref, wd_ref, g_ref, b_ref, o_ref,
                      gate_scr, acc_scr):
    e = pl.program_id(1)

    @pl.when(e == 0)
    def _():
        logits = _bdot(x_ref[...], wr_ref[...]) + br_ref[...]
        gate_scr[...] = _route(logits)[0]
        acc_scr[...] = jnp.zeros_like(acc_scr)

    xb = x_ref[...].astype(BF16)
    lane = lax.broadcasted_iota(jnp.int32, gate_scr.shape, 1)
    gcol = jnp.sum(jnp.where(lane == e + N_GROUPS, gate_scr[...], 0.0), -1, keepdims=True)
    hid = _silu(_bdot(xb, wg_ref[...])) * _bdot(xb, wu_ref[...])
    acc_scr[...] += _bdot(hid * gcol, wd_ref[...])

    @pl.when(e == pl.num_programs(1) - 1)
    def _():
        o_ref[...] = _layer_norm(ALPHA * x_ref[...] + acc_scr[...], g_ref[...], b_ref[...])


def moe_dense(x, wr, br, wg, wu, wd, g, b, layer, tm):
    m = x.shape[0]
    return pl.pallas_call(
        _moe_dense_kernel,
        grid=(m // tm, N_EXPERTS),
        in_specs=[pl.BlockSpec((tm, D_MODEL), lambda i, e: (i, 0)),
                  pl.BlockSpec((D_MODEL, ROUTER_LANES), lambda i, e: (0, 0)),
                  pl.BlockSpec((1, ROUTER_LANES), lambda i, e: (0, 0)),
                  pl.BlockSpec((None, None, D_MODEL, EXP_FF), lambda i, e: (layer, e, 0, 0)),
                  pl.BlockSpec((None, None, D_MODEL, EXP_FF), lambda i, e: (layer, e, 0, 0)),
                  pl.BlockSpec((None, None, EXP_FF, D_MODEL), lambda i, e: (layer, e, 0, 0)),
                  pl.BlockSpec((1, D_MODEL), lambda i, e: (0, 0)),
                  pl.BlockSpec((1, D_MODEL), lambda i, e: (0, 0))],
        out_specs=pl.BlockSpec((tm, D_MODEL), lambda i, e: (i, 0)),
        out_shape=jax.ShapeDtypeStruct((m, D_MODEL), F32),
        scratch_shapes=[pltpu.VMEM((tm, ROUTER_LANES), F32), pltpu.VMEM((tm, D_MODEL), F32)],
        compiler_params=_cparams(("parallel", "arbitrary")),
        name="moe_dense",
    )(x, wr, br, wg, wu, wd, g, b)


def _moe_routed_kernel(src_ref, dst_ref, tg_ref, xg_hbm, wg_ref, wu_ref, wd_ref, g_ref, b_ref,
                       y_hbm, xbuf, ybuf, gsem, ssem):
    t = pl.program_id(0)
    nt = pl.num_programs(0)
    slot = t % 2
    tm = xbuf.shape[1]

    def gather_row(tile, sl, r):
        return pltpu.make_async_copy(xg_hbm.at[pl.ds(src_ref[tile * tm + r], 1), :],
                                     xbuf.at[sl, pl.ds(r, 1), :], gsem.at[sl])

    def scatter_row(tile, sl, r):
        return pltpu.make_async_copy(ybuf.at[sl, pl.ds(r, 1), :],
                                     y_hbm.at[pl.ds(dst_ref[tile * tm + r], 1), :], ssem.at[sl])

    def start_gather(tile, sl):
        def body(r, c):
            gather_row(tile, sl, r).start()
            return c
        lax.fori_loop(0, tm, body, 0, unroll=8)

    @pl.when(t == 0)
    def _():
        start_gather(0, 0)

    @pl.when(t + 1 < nt)
    def _():
        start_gather(t + 1, 1 - slot)

    pltpu.make_async_copy(xbuf.at[slot], xbuf.at[slot], gsem.at[slot]).wait()

    @pl.when(t >= 2)
    def _():
        pltpu.make_async_copy(ybuf.at[slot], ybuf.at[slot], ssem.at[slot]).wait()

    x = xbuf[slot, :, :D_MODEL]
    gate = xbuf[slot, :, D_MODEL:]
    xb = x.astype(BF16)
    lane = lax.broadcasted_iota(jnp.int32, gate.shape, 1)
    first = N_GROUPS + EXP_PER_GROUP * tg_ref[t]
    acc = jnp.zeros((tm, D_MODEL), F32)
    for e in range(EXP_PER_GROUP):
        gcol = jnp.sum(jnp.where(lane == first + e, gate, 0.0), -1, keepdims=True)
        hid = _silu(_bdot(xb, wg_ref[e])) * _bdot(xb, wu_ref[e])
        acc = acc + _bdot(hid * gcol, wd_ref[e])
    ybuf[slot] = _layer_norm(ALPHA * x + acc, g_ref[...], b_ref[...])

    def body(r, c):
        scatter_row(t, slot, r).start()
        return c
    lax.fori_loop(0, tm, body, 0, unroll=8)

    @pl.when(t == nt - 1)
    def _():
        pltpu.make_async_copy(ybuf.at[slot], ybuf.at[slot], ssem.at[slot]).wait()
        pltpu.make_async_copy(ybuf.at[1 - slot], ybuf.at[1 - slot], ssem.at[1 - slot]).wait()


def _moe_plan(group_idx, n, tm):
    n_tiles = n // tm + N_GROUPS
    n_slots = n_tiles * tm
    onehot = (group_idx[:, None] == jnp.arange(N_GROUPS)[None, :]).astype(jnp.int32)
    csum = jnp.cumsum(onehot, axis=0)
    counts = csum[-1]
    rank = jnp.sum(onehot * csum, axis=1) - 1
    tiles_g = (counts + tm - 1) // tm
    tile_end = jnp.cumsum(tiles_g)
    slot_base = (tile_end - tiles_g) * tm
    slot = jnp.sum(onehot * slot_base[None, :], axis=1) + rank
    tok = jnp.arange(n, dtype=jnp.int32)
    src = jnp.zeros((n_slots,), jnp.int32).at[slot].set(tok, unique_indices=True)
    valid = jnp.zeros((n_slots,), jnp.int32).at[slot].set(1, unique_indices=True)
    pad_rank = jnp.cumsum(1 - valid) - 1
    dst = jnp.where(valid == 1, src, n + pad_rank).astype(jnp.int32)
    tile_group = jnp.sum((jnp.arange(n_tiles)[:, None] >= tile_end[None, :]).astype(jnp.int32), axis=1)
    tile_group = jnp.minimum(tile_group, N_GROUPS - 1).astype(jnp.int32)
    return src, dst, tile_group


def moe_routed(xg, wg, wu, wd, g, b, layer, n, tm):
    group_idx = xg[:n, D_MODEL].astype(jnp.int32)
    src, dst, tile_group = _moe_plan(group_idx, n, tm)
    n_tiles = tile_group.shape[0]
    wspec = lambda shp: pl.BlockSpec((None, None, EXP_PER_GROUP) + shp,
                                     lambda t, s, d, tg: (layer, tg[t], 0, 0, 0))
    grid_spec = pltpu.PrefetchScalarGridSpec(
        num_scalar_prefetch=3,
        grid=(n_tiles,),
        in_specs=[pl.BlockSpec(memory_space=pl.ANY),
                  wspec((D_MODEL, EXP_FF)), wspec((D_MODEL, EXP_FF)), wspec((EXP_FF, D_MODEL)),
                  pl.BlockSpec((1, D_MODEL), lambda t, s, d, tg: (0, 0)),
                  pl.BlockSpec((1, D_MODEL), lambda t, s, d, tg: (0, 0))],
        out_specs=pl.BlockSpec(memory_space=pl.ANY),
        scratch_shapes=[pltpu.VMEM((2, tm, XG_WIDTH), F32),
                        pltpu.VMEM((2, tm, D_MODEL), F32),
                        pltpu.SemaphoreType.DMA((2,)),
                        pltpu.SemaphoreType.DMA((2,))])
    grouped = lambda w: w.reshape(w.shape[0], N_GROUPS, EXP_PER_GROUP, w.shape[2], w.shape[3])
    return pl.pallas_call(
        _moe_routed_kernel,
        grid_spec=grid_spec,
        out_shape=jax.ShapeDtypeStruct((n_tiles * tm, D_MODEL), F32),
        compiler_params=_cparams(("arbitrary",)),
        name="moe_routed",
    )(src, dst, tile_group, xg, grouped(wg), grouped(wu), grouped(wd), g, b)


def _t5_bucket(dist):
    max_exact = N_BUCKETS // 2
    d = jnp.maximum(dist, 0)
    df = jnp.maximum(d, 1).astype(F32)
    log_b = max_exact + (jnp.log(df / max_exact) / math.log(MAX_DISTANCE / max_exact)
                         * (N_BUCKETS - max_exact)).astype(jnp.int32)
    return jnp.where(d < max_exact, d, jnp.minimum(log_b, N_BUCKETS - 1))


def _bucket_lookup(rel_bias, bucket):
    out = jnp.zeros(bucket.shape + (rel_bias.shape[1],), F32)
    for i in range(N_BUCKETS):
        out = jnp.where((bucket == i)[..., None], rel_bias[i].astype(F32), out)
    return out


def _prompt_bias(rel_bias):
    qi = jnp.arange(WINDOW)[:, None]
    kj = jnp.arange(2 * WINDOW)[None, :]
    dist = qi + WINDOW - kj
    bias = _bucket_lookup(rel_bias, _t5_bucket(dist)).transpose(2, 0, 1)
    valid = (dist >= 0) & (dist <= WINDOW)
    return jnp.where(valid[None], bias, NEG)


def _decode_bias(rel_bias):
    dist = WINDOW - jnp.arange(WINDOW + 1)
    bias = _bucket_lookup(rel_bias, _t5_bucket(dist))
    out = []
    for part in (bias[:WINDOW], jnp.broadcast_to(bias[WINDOW:], (WINDOW, A_HEADS))):
        for r in range(A_REP):
            out.append(jnp.concatenate([jnp.repeat(part[:, r:r + 1], HEAD_DIM, 1),
                                        jnp.repeat(part[:, A_REP + r:A_REP + r + 1], HEAD_DIM, 1)], 1))
    return jnp.stack(out)


def _block_ones(width):
    idx = jnp.arange(width) // HEAD_DIM
    return (idx[:, None] == idx[None, :])


def _head_rows_mask():
    head = jnp.arange(HG_WIDTH)[None, :] // HEAD_DIM
    return (head == jnp.arange(HG_MROWS)[:, None]).astype(F32)


def _block_diag(w):
    nblk, s, _ = w.shape
    eye = jnp.eye(nblk, dtype=w.dtype)
    return (eye[:, None, :, None] * w[:, :, None, :]).reshape(nblk * s, nblk * s)


def kernel(x_prompt, x_sample, mem_prompt, cache_win_k, cache_win_v, state_hgrn, state_conv, state_lru, cache_mem_k, cache_mem_v, rel_bias, hg_lb, w_in, attn_sink, hg_gain, conv_w, conv_b, lru_wa, lru_ba, lru_wx, lru_bx, lru_lam, w_out, ln1_g, ln1_b, x_wq, x_wk, x_wv, x_wo, ln2_g, ln2_b, r_gw, r_gb, r_ew, r_eb, e_wg, e_wu, e_wd, ln3_g, ln3_b):
    bp, seq, d = x_prompt.shape
    n_tok = bp * seq
    nd = x_sample.shape[0]
    depth = w_in.shape[0]

    lbs = jnp.cumsum(jax.nn.softmax(hg_lb.astype(F32), axis=0), axis=0)
    lbs = lbs - lbs[0]
    loglb = jnp.log(lbs)
    log1mlb = jnp.log1p(-lbs)
    gain4 = jnp.tile(hg_gain, (1, HG_HEADS))

    bias_p = _prompt_bias(rel_bias)
    bias_d = _decode_bias(rel_bias)
    ones128 = _block_ones(LANES).astype(BF16)
    ones256 = _block_ones(HG_WIDTH).astype(BF16)
    bdm256 = _block_ones(HG_WIDTH).astype(F32)
    hmask = _head_rows_mask()

    w_in_b = w_in.astype(BF16)
    w_out_b = w_out.astype(BF16)
    wq_b = x_wq.astype(BF16)
    wkv_b = jnp.concatenate([x_wk, x_wv], axis=-1).astype(BF16)
    wo_b = x_wo.astype(BF16)
    wg_b = e_wg.astype(BF16)
    wu_b = e_wu.astype(BF16)
    wd_b = e_wd.astype(BF16)
    rew = r_ew.transpose(0, 2, 1, 3).reshape(depth, d, N_EXPERTS)
    wr = jnp.concatenate([r_gw, rew, jnp.zeros((depth, d, ROUTER_LANES - N_GROUPS - N_EXPERTS), F32)], -1)
    br = jnp.concatenate([r_gb, r_eb.reshape(depth, N_EXPERTS),
                          jnp.zeros((depth, ROUTER_LANES - N_GROUPS - N_EXPERTS), F32)], -1)

    a_w = A_HEADS * HEAD_DIM
    xp = x_prompt.reshape(bp * seq, d)
    xs = x_sample.reshape(nd, d)
    mem = mem_prompt.reshape(bp * N_MEM, d)
    ck = cache_win_k.reshape(depth, nd, WINDOW, A_KV_HEADS * HEAD_DIM)
    cv = cache_win_v.reshape(depth, nd, WINDOW, A_KV_HEADS * HEAD_DIM)
    cmk = cache_mem_k.reshape(depth, nd, N_MEM, X_WIDTH)
    cmv = cache_mem_v.reshape(depth, nd, N_MEM, X_WIDTH)

    p_wk, p_wv, p_s, p_cb, p_h, p_mk, p_mv = [], [], [], [], [], [], []
    s_wk, s_wv, s_s, s_cb, s_h = [], [], [], [], []
    for l in range(depth):
        row = lambda v: v[l].reshape(1, -1)
        wa_bd = _block_diag(lru_wa[l]).astype(BF16)
        wx_bd = _block_diag(lru_wx[l]).astype(BF16)
        lru_args = (conv_w[l], row(conv_b), wa_bd, row(lru_ba), wx_bd, row(lru_bx), row(lru_lam))
        wo_parts = [w_out_b[l, :a_w], w_out_b[l, a_w:a_w + HG_WIDTH], w_out_b[l, a_w + HG_WIDTH:]]

        proj = matmul(xp, w_in_b[l], 512, 768, rows=n_tok)
        oa = attn_prompt(proj, attn_sink[l], bias_p, bp, seq)
        ob, st = hgrn_prompt(proj, row(loglb), row(log1mlb), row(gain4), bdm256, hmask, bp, seq, 256)
        oc, hl = lru_prompt(proj, *lru_args, bp, seq, 256)
        xp = proj_res_ln(xp, [oa, ob, oc], wo_parts, row(ln1_g), row(ln1_b), 512)
        mkv = matmul(mem, wkv_b[l], 256, 512)
        xg = xattn_prompt(xp, wq_b[l], mkv, wo_b[l], row(ln2_g), row(ln2_b), wr[l], br[l:l + 1],
                          bp, seq, 512)
        xp = moe_routed(xg, wg_b, wu_b, wd_b, row(ln3_g), row(ln3_b), l, n_tok, MOE_TM)

        proj3 = proj.reshape(bp, seq, IN_COLS)
        p_wk.append(proj3[:, seq - WINDOW:, a_w:a_w + LANES].reshape(bp, WINDOW, A_KV_HEADS, HEAD_DIM))
        p_wv.append(proj3[:, seq - WINDOW:, a_w + LANES:a_w + 2 * LANES].reshape(bp, WINDOW, A_KV_HEADS, HEAD_DIM))
        st5 = st.reshape(bp, HG_HEADS, HEAD_DIM, HG_HEADS, HEAD_DIM)
        p_s.append(jnp.stack([st5[:, h, :, h, :] for h in range(HG_HEADS)], 1).transpose(0, 1, 3, 2))
        p_cb.append(proj3[:, seq - (CONV_W - 1):, IN_COLS - 2 * LRU_WIDTH:IN_COLS - LRU_WIDTH])
        p_h.append(hl.reshape(bp, LRU_WIDTH))
        p_mk.append(mkv[:, :X_WIDTH].reshape(bp, N_MEM, X_HEADS, HEAD_DIM))
        p_mv.append(mkv[:, X_WIDTH:].reshape(bp, N_MEM, X_HEADS, HEAD_DIM))

        projd = matmul(xs, w_in_b[l], nd, 768)
        oa, nk, nv = attn_decode(projd, attn_sink[l], bias_d, ones128, ck[l], cv[l], 8)
        gates_t = projd[:, a_w + 2 * LANES:a_w + 2 * LANES + 4 * HG_WIDTH].T
        bc = lambda v: jnp.broadcast_to(v[:, None], (v.shape[0], nd))
        state_t = state_hgrn[l].reshape(nd, -1).T
        ob_t, ns_t = hgrn_decode(gates_t, bc(loglb[l]), bc(log1mlb[l]), bc(hg_gain[l]), state_t)
        oc, nh, nbuf = lru_decode(projd, state_conv[l].reshape(nd, -1), state_lru[l], *lru_args)
        xs = proj_res_ln(xs, [oa, ob_t.T, oc], wo_parts, row(ln1_g), row(ln1_b), nd)
        qd = matmul(xs, wq_b[l], nd, X_WIDTH)
        od = xattn_decode(qd, cmk[l], cmv[l], ones256, 8)
        xs = proj_res_ln(xs, [od], [wo_b[l]], row(ln2_g), row(ln2_b), nd)
        xs = moe_dense(xs, wr[l], br[l:l + 1], wg_b, wu_b, wd_b, row(ln3_g), row(ln3_b), l, nd)

        s_wk.append(nk.reshape(nd, WINDOW, A_KV_HEADS, HEAD_DIM))
        s_wv.append(nv.reshape(nd, WINDOW, A_KV_HEADS, HEAD_DIM))
        s_s.append(ns_t.T.reshape(nd, HG_HEADS, HEAD_DIM, HEAD_DIM))
        s_cb.append(nbuf.reshape(nd, CONV_W - 1, LRU_WIDTH))
        s_h.append(nh)

    return (xp[:n_tok].reshape(bp, seq, d), xs.reshape(nd, 1, d),
            jnp.stack(p_wk), jnp.stack(p_wv), jnp.stack(p_s), jnp.stack(p_cb), jnp.stack(p_h),
            jnp.stack(p_mk), jnp.stack(p_mv),
            jnp.stack(s_wk), jnp.stack(s_wv), jnp.stack(s_s), jnp.stack(s_cb), jnp.stack(s_h))
```

```python
import functools
import math

import jax
import jax.numpy as jnp
from jax import lax
from jax.experimental import pallas as pl
from jax.experimental.pallas import tpu as pltpu
from jax.experimental.pallas import tpu_sc as plsc

F32 = jnp.float32
BF16 = jnp.bfloat16

D_MODEL = 1024
DEPTH = 4
HEAD_DIM = 64
A_HEADS = 8
A_KV_HEADS = 2
A_REP = A_HEADS // A_KV_HEADS
WINDOW = 128
N_BUCKETS = 32
MAX_DISTANCE = 128
HG_WIDTH = 256
HG_HEADS = 4
HG_CHUNK = 64
HG_TB = 16
HG_MROWS = 16
LRU_WIDTH = 256
LRU_BLOCKS = 4
CONV_W = 4
LRU_C = 8.0
N_MEM = 256
X_HEADS = 4
X_WIDTH = X_HEADS * HEAD_DIM
N_GROUPS = 4
EXP_PER_GROUP = 4
N_EXPERTS = N_GROUPS * EXP_PER_GROUP
EXP_FF = D_MODEL // 4
ALPHA = (2 * DEPTH) ** 0.25
LN_EPS = 1e-5
RMS_EPS = 1e-6
IN_COLS = 2304
SCALE = HEAD_DIM ** -0.5
NEG = -1e30
LANES = 128
ROUTER_LANES = 128
XG_WIDTH = D_MODEL + ROUTER_LANES
MOE_TM = 512
SC_WINDOW = 128
VMEM_LIMIT = 48 * 1024 * 1024


def _cparams(sem):
    return pltpu.CompilerParams(dimension_semantics=sem, vmem_limit_bytes=VMEM_LIMIT)


def _bdot(a, b):
    return jnp.dot(a.astype(BF16), b.astype(BF16), preferred_element_type=F32)


def _bdot_nt(a, b):
    return lax.dot_general(a.astype(BF16), b.astype(BF16), (((1,), (1,)), ((), ())),
                           preferred_element_type=F32)


def _bdot_tn(a, b):
    return lax.dot_general(a.astype(BF16), b.astype(BF16), (((0,), (0,)), ((), ())),
                           preferred_element_type=F32)


def _rb(x):
    return x.astype(BF16).astype(F32)


def _silu(x):
    return x * jax.nn.sigmoid(x)


def _neg_expm1(x):
    return -jnp.tanh(0.5 * x) * (jnp.exp(x) + 1.0)


def _softplus(x):
    return jnp.maximum(x, 0.0) + jnp.log1p(jnp.exp(-jnp.abs(x)))


def _gelu_tanh(x):
    return 0.5 * x * (1.0 + jnp.tanh(math.sqrt(2.0 / math.pi) * (x + 0.044715 * (x * x * x))))


def _layer_norm(y, g, b):
    mu = jnp.mean(y, -1, keepdims=True)
    yc = y - mu
    var = jnp.mean(yc * yc, -1, keepdims=True)
    return yc * lax.rsqrt(var + LN_EPS) * g + b


def _mm_kernel(x_ref, w_ref, o_ref):
    o_ref[...] = _bdot(x_ref[...], w_ref[...])


def matmul(x, w, tm, tn, rows=None):
    m, k = x.shape
    m = rows or m
    n = w.shape[1]
    return pl.pallas_call(
        _mm_kernel,
        grid=(m // tm, n // tn),
        in_specs=[pl.BlockSpec((tm, k), lambda i, j: (i, 0)),
                  pl.BlockSpec((k, tn), lambda i, j: (0, j))],
        out_specs=pl.BlockSpec((tm, tn), lambda i, j: (i, j)),
        out_shape=jax.ShapeDtypeStruct((m, n), F32),
        compiler_params=_cparams(("parallel", "parallel")),
        name="matmul",
    )(x, w)


def _proj_res_ln_kernel(n_in, x_ref, *refs):
    a_refs = refs[:n_in]
    w_refs = refs[n_in:2 * n_in]
    g_ref, b_ref, o_ref = refs[2 * n_in:]
    y = ALPHA * x_ref[...]
    for a_ref, w_ref in zip(a_refs, w_refs):
        y = y + _bdot(a_ref[...], w_ref[...])
    o_ref[...] = _layer_norm(y, g_ref[...], b_ref[...])


def proj_res_ln(x, a_list, w_list, g, b, tm):
    d = x.shape[1]
    m = a_list[0].shape[0]
    n_in = len(a_list)
    in_specs = [pl.BlockSpec((tm, d), lambda i: (i, 0))]
    in_specs += [pl.BlockSpec((tm, a.shape[1]), lambda i: (i, 0)) for a in a_list]
    in_specs += [pl.BlockSpec(w.shape, lambda i: (0, 0)) for w in w_list]
    in_specs += [pl.BlockSpec((1, d), lambda i: (0, 0))] * 2
    return pl.pallas_call(
        functools.partial(_proj_res_ln_kernel, n_in),
        grid=(m // tm,),
        in_specs=in_specs,
        out_specs=pl.BlockSpec((tm, d), lambda i: (i, 0)),
        out_shape=jax.ShapeDtypeStruct((m, d), F32),
        compiler_params=_cparams(("parallel",)),
        name="proj_res_ln",
    )(x, *a_list, *w_list, g, b)


def _attn_prompt_kernel(sink_ref, q_ref, kc_ref, kp_ref, vc_ref, vp_ref, bias_ref, o_ref):
    n = pl.program_id(1)
    col = lax.broadcasted_iota(jnp.int32, (WINDOW, 2 * WINDOW), 1)
    first = jnp.where((n == 0) & (col < WINDOW), NEG, 0.0)
    kk = jnp.concatenate([kp_ref[...], kc_ref[...]], axis=0).astype(BF16)
    vv = jnp.concatenate([vp_ref[...], vc_ref[...]], axis=0).astype(BF16)
    q = q_ref[...].astype(BF16)
    outs = []
    for h in range(A_HEADS):
        g = h // A_REP
        qh = q[:, h * HEAD_DIM:(h + 1) * HEAD_DIM]
        kg = kk[:, g * HEAD_DIM:(g + 1) * HEAD_DIM]
        vg = vv[:, g * HEAD_DIM:(g + 1) * HEAD_DIM]
        s = _bdot_nt(qh, kg) * SCALE + bias_ref[h] + first
        sink = sink_ref[h]
        m = jnp.maximum(jnp.max(s, -1, keepdims=True), sink)
        p = jnp.exp(s - m)
        den = jnp.sum(p, -1, keepdims=True) + jnp.exp(sink - m)
        outs.append(_bdot(p / den, vg))
    o_ref[...] = jnp.concatenate(outs, axis=1)


def attn_prompt(proj, sink, bias, batch, seq):
    nb = seq // WINDOW
    qcol = 0
    kcol = (A_HEADS * HEAD_DIM) // LANES
    vcol = kcol + 1

    def cur(c):
        return lambda b, n: (b * nb + n, c)

    def prev(c):
        return lambda b, n: (b * nb + jnp.maximum(n - 1, 0), c)

    return pl.pallas_call(
        _attn_prompt_kernel,
        grid=(batch, nb),
        in_specs=[pl.BlockSpec(memory_space=pltpu.SMEM),
                  pl.BlockSpec((WINDOW, A_HEADS * HEAD_DIM), cur(qcol)),
                  pl.BlockSpec((WINDOW, LANES), cur(kcol)),
                  pl.BlockSpec((WINDOW, LANES), prev(kcol)),
                  pl.BlockSpec((WINDOW, LANES), cur(vcol)),
                  pl.BlockSpec((WINDOW, LANES), prev(vcol)),
                  pl.BlockSpec((A_HEADS, WINDOW, 2 * WINDOW), lambda b, n: (0, 0, 0))],
        out_specs=pl.BlockSpec((WINDOW, A_HEADS * HEAD_DIM), cur(0)),
        out_shape=jax.ShapeDtypeStruct((batch * seq, A_HEADS * HEAD_DIM), F32),
        compiler_params=_cparams(("parallel", "parallel")),
        name="attn_prompt",
    )(sink, proj, proj, proj, proj, proj, bias)


def _seg_sum(x2d, ones_bd):
    hi = x2d.astype(BF16)
    lo = (x2d - hi.astype(F32)).astype(BF16)
    return (jnp.dot(hi, ones_bd, preferred_element_type=F32)
            + jnp.dot(lo, ones_bd, preferred_element_type=F32))


def _attn_decode_kernel(sink_ref, q_ref, kn_ref, vn_ref, ck_ref, cv_ref, bias_ref, ones_ref,
                        o_ref, ok_ref, ov_ref):
    bb = q_ref.shape[0]
    ck = ck_ref[...]
    cv = cv_ref[...]
    kn = kn_ref[...]
    vn = vn_ref[...]
    q = q_ref[...]
    ones_bd = ones_ref[...]
    ckb, cvb, knb, vnb = _rb(ck), _rb(cv), _rb(kn), _rb(vn)
    lane = lax.broadcasted_iota(jnp.int32, (1, LANES), 1)
    pieces = [None] * A_HEADS
    for r in range(A_REP):
        h0, h1 = r, A_REP + r
        qp = jnp.concatenate([q[:, h0 * HEAD_DIM:(h0 + 1) * HEAD_DIM],
                              q[:, h1 * HEAD_DIM:(h1 + 1) * HEAD_DIM]], axis=1)
        qp = _rb(qp)
        prod = (ckb * qp[:, None, :]).reshape(bb * WINDOW, LANES)
        s = _seg_sum(prod, ones_bd).reshape(bb, WINDOW, LANES) * SCALE + bias_ref[r][None]
        sn = _seg_sum(knb * qp, ones_bd) * SCALE + bias_ref[A_REP + r][0:1, :]
        sink = jnp.where(lane < HEAD_DIM, sink_ref[h0], sink_ref[h1])
        m = jnp.maximum(jnp.maximum(jnp.max(s, axis=1), sn), sink)
        p = jnp.exp(s - m[:, None, :])
        pn = jnp.exp(sn - m)
        den = jnp.sum(p, axis=1) + pn + jnp.exp(sink - m)
        o = jnp.sum(_rb(p / den[:, None, :]) * cvb, axis=1) + _rb(pn / den) * vnb
        pieces[h0] = o[:, :HEAD_DIM]
        pieces[h1] = o[:, HEAD_DIM:]
    o_ref[...] = jnp.concatenate(pieces, axis=1)
    ok_ref[:, 0:WINDOW - 1, :] = ck[:, 1:WINDOW, :]
    ok_ref[:, WINDOW - 1, :] = kn
    ov_ref[:, 0:WINDOW - 1, :] = cv[:, 1:WINDOW, :]
    ov_ref[:, WINDOW - 1, :] = vn


def attn_decode(proj_d, sink, bias_d, ones_bd, cache_k, cache_v, bb):
    nbatch = proj_d.shape[0]
    kcol = (A_HEADS * HEAD_DIM) // LANES
    cache_spec = pl.BlockSpec((bb, WINDOW, LANES), lambda i: (i, 0, 0))
    return pl.pallas_call(
        _attn_decode_kernel,
        grid=(nbatch // bb,),
        in_specs=[pl.BlockSpec(memory_space=pltpu.SMEM),
                  pl.BlockSpec((bb, A_HEADS * HEAD_DIM), lambda i: (i, 0)),
                  pl.BlockSpec((bb, LANES), lambda i: (i, kcol)),
                  pl.BlockSpec((bb, LANES), lambda i: (i, kcol + 1)),
                  cache_spec, cache_spec,
                  pl.BlockSpec((2 * A_REP, WINDOW, LANES), lambda i: (0, 0, 0)),
                  pl.BlockSpec((LANES, LANES), lambda i: (0, 0))],
        out_specs=[pl.BlockSpec((bb, A_HEADS * HEAD_DIM), lambda i: (i, 0)), cache_spec, cache_spec],
        out_shape=[jax.ShapeDtypeStruct((nbatch, A_HEADS * HEAD_DIM), F32),
                   jax.ShapeDtypeStruct(cache_k.shape, F32),
                   jax.ShapeDtypeStruct(cache_v.shape, F32)],
        compiler_params=_cparams(("parallel",)),
        name="attn_decode",
    )(sink, proj_d, proj_d, proj_d, cache_k, cache_v, bias_d, ones_bd)


def _xattn_decode_kernel(q_ref, mk_ref, mv_ref, ones_ref, o_ref):
    bb = q_ref.shape[0]
    q = _rb(q_ref[...])
    mk = _rb(mk_ref[...])
    prod = (mk * q[:, None, :]).reshape(bb * N_MEM, X_WIDTH)
    s = _seg_sum(prod, ones_ref[...]).reshape(bb, N_MEM, X_WIDTH) * SCALE
    m = jnp.max(s, axis=1, keepdims=True)
    p = jnp.exp(s - m)
    den = jnp.sum(p, axis=1, keepdims=True)
    o_ref[...] = jnp.sum(_rb(p / den) * _rb(mv_ref[...]), axis=1)


def xattn_decode(q, mem_k, mem_v, ones_bd, bb):
    nbatch = q.shape[0]
    mem_spec = pl.BlockSpec((bb, N_MEM, X_WIDTH), lambda i: (i, 0, 0))
    return pl.pallas_call(
        _xattn_decode_kernel,
        grid=(nbatch // bb,),
        in_specs=[pl.BlockSpec((bb, X_WIDTH), lambda i: (i, 0)), mem_spec, mem_spec,
                  pl.BlockSpec((X_WIDTH, X_WIDTH), lambda i: (0, 0))],
        out_specs=pl.BlockSpec((bb, X_WIDTH), lambda i: (i, 0)),
        out_shape=jax.ShapeDtypeStruct((nbatch, X_WIDTH), F32),
        compiler_params=_cparams(("parallel",)),
        name="xattn_decode",
    )(q, mem_k, mem_v, ones_bd)


def _hgrn_gates(hq, hf, loglb, log1mlb):
    ls = jnp.minimum(hf, 0.0) - jnp.log1p(jnp.exp(-jnp.abs(hf)))
    b = log1mlb + ls
    lf = jnp.maximum(loglb, b) + jnp.log1p(jnp.exp(-jnp.abs(loglb - b)))
    return _silu(hq), lf, _neg_expm1(lf)


def _hgrn_prompt_kernel(hq_ref, hf_ref, hi_ref, hg_ref, loglb_ref, log1mlb_ref, gain_ref,
                        bdm_ref, hm_ref, ob_ref, st_ref, st_scr, q_scr, k_scr, cum_scr, o_scr):
    i = pl.program_id(1)
    tt = hq_ref.shape[0]
    c = HG_CHUNK
    tb = HG_TB

    @pl.when(i == 0)
    def _():
        st_scr[...] = jnp.zeros_like(st_scr)

    qs, lf, kk = _hgrn_gates(hq_ref[...], hf_ref[...], loglb_ref[...], log1mlb_ref[...])
    row = lax.broadcasted_iota(jnp.int32, (tt, HG_WIDTH), 0) & (c - 1)
    cum = lf
    sh = 1
    while sh < c:
        cum = cum + jnp.where(row >= sh, pltpu.roll(cum, sh, 0), 0.0)
        sh *= 2
    q_scr[...] = qs
    k_scr[...] = kk
    cum_scr[...] = cum

    bdm = bdm_ref[...]
    hmask = hm_ref[...]

    def chunk(ci, carry):
        r0 = pl.multiple_of(ci * c, c)
        r = pl.ds(r0, c)
        cu = cum_scr[r, :]
        q = q_scr[r, :]
        k = k_scr[r, :]
        v = hi_ref[r, :]
        vb = v.astype(BF16)
        qb = _rb(q)
        last = cu[c - 1:c, :]
        st = st_scr[...]
        o_inter = _bdot_nt(q * jnp.exp(cu), st)
        for j in range(c // tb):
            ns = tb * (j + 1)
            ti = lax.broadcasted_iota(jnp.int32, (tb, ns, HG_WIDTH), 0) + tb * j
            si = lax.broadcasted_iota(jnp.int32, (tb, ns, HG_WIDTH), 1)
            cut = cu[tb * j:tb * (j + 1), :]
            dec = jnp.exp(jnp.where(ti >= si, cut[:, None, :] - cu[None, :ns, :], NEG))
            a2 = (dec * k[None, :ns, :]).astype(BF16)
            q4 = (qb[tb * j:tb * (j + 1), None, :] * hmask[None, :, :]).astype(BF16)
            att = lax.dot_general(q4, a2, (((2,), (2,)), ((0,), (0,))),
                                  preferred_element_type=F32)
            w = jnp.dot(att.reshape(tb * HG_MROWS, ns).astype(BF16), vb[:ns, :],
                        preferred_element_type=F32).reshape(tb, HG_MROWS, HG_WIDTH)
            o_intra = jnp.sum(w * hmask[None, :, :], axis=1)
            o_scr[pl.ds(r0 + tb * j, tb), :] = o_intra + o_inter[tb * j:tb * (j + 1), :]
        upd = _bdot_tn(v, k * jnp.exp(last - cu))
        st_scr[...] = st * jnp.exp(last) + upd * bdm
        return carry

    lax.fori_loop(0, tt // c, chunk, 0)

    o = o_scr[...]
    ms = jnp.dot(o * o, bdm, precision=lax.Precision.HIGHEST,
                 preferred_element_type=F32) * (1.0 / HEAD_DIM)
    ob_ref[...] = o * lax.rsqrt(ms + RMS_EPS) * gain_ref[...] * _silu(hg_ref[...])

    @pl.when(i == pl.num_programs(1) - 1)
    def _():
        st_ref[...] = st_scr[...]


def hgrn_prompt(proj, loglb, log1mlb, gain4, bdm, hmask, batch, seq, tt):
    nt = seq // tt
    base = (A_HEADS + 2 * A_KV_HEADS) * HEAD_DIM // HG_WIDTH

    def col(cblk):
        return pl.BlockSpec((tt, HG_WIDTH), lambda b, i: (b * nt + i, cblk))

    row_spec = pl.BlockSpec((1, HG_WIDTH), lambda b, i: (0, 0))
    mat_spec = pl.BlockSpec((HG_WIDTH, HG_WIDTH), lambda b, i: (0, 0))
    return pl.pallas_call(
        _hgrn_prompt_kernel,
        grid=(batch, nt),
        in_specs=[col(base), col(base + 1), col(base + 2), col(base + 3),
                  row_spec, row_spec, row_spec, mat_spec,
                  pl.BlockSpec((HG_MROWS, HG_WIDTH), lambda b, i: (0, 0))],
        out_specs=[pl.BlockSpec((tt, HG_WIDTH), lambda b, i: (b * nt + i, 0)),
                   pl.BlockSpec((None, HG_WIDTH, HG_WIDTH), lambda b, i: (b, 0, 0))],
        out_shape=[jax.ShapeDtypeStruct((batch * seq, HG_WIDTH), F32),
                   jax.ShapeDtypeStruct((batch, HG_WIDTH, HG_WIDTH), F32)],
        scratch_shapes=[pltpu.VMEM((HG_WIDTH, HG_WIDTH), F32),
                        pltpu.VMEM((tt, HG_WIDTH), F32),
                        pltpu.VMEM((tt, HG_WIDTH), F32),
                        pltpu.VMEM((tt, HG_WIDTH), F32),
                        pltpu.VMEM((tt, HG_WIDTH), F32)],
        compiler_params=_cparams(("parallel", "arbitrary")),
        name="hgrn_prompt",
    )(proj, proj, proj, proj, loglb, log1mlb, gain4, bdm, hmask)


def _hgrn_decode_kernel(hq_ref, hf_ref, hi_ref, hg_ref, loglb_ref, log1mlb_ref, gain_ref, s_ref,
                        ob_ref, so_ref):
    nb = hq_ref.shape[1]
    qs, lf, kk = _hgrn_gates(hq_ref[...], hf_ref[...], loglb_ref[...], log1mlb_ref[...])
    v = hi_ref[...]
    f = jnp.exp(lf)
    s = s_ref[...].reshape(HEAD_DIM, HEAD_DIM, nb)
    att = jnp.sum(_rb(qs) * _rb(kk), axis=0, keepdims=True)
    o = _rb(att) * _rb(v) + jnp.sum(_rb(qs * f)[:, None, :] * _rb(s), axis=0)
    s_new = f[:, None, :] * s + _rb(kk)[:, None, :] * _rb(v)[None, :, :]
    so_ref[...] = s_new.reshape(HEAD_DIM * HEAD_DIM, nb)
    ms = jnp.mean(o * o, axis=0, keepdims=True)
    ob_ref[...] = o * lax.rsqrt(ms + RMS_EPS) * gain_ref[...] * _silu(hg_ref[...])


def hgrn_decode(gates_t, loglb_t, log1mlb_t, gain_t, state_t):
    nb = gates_t.shape[1]

    def blk(off):
        return pl.BlockSpec((HEAD_DIM, nb), lambda h: (off * HG_HEADS + h, 0))

    par = pl.BlockSpec((HEAD_DIM, nb), lambda h: (h, 0))
    st = pl.BlockSpec((HEAD_DIM * HEAD_DIM, nb), lambda h: (h, 0))
    return pl.pallas_call(
        _hgrn_decode_kernel,
        grid=(HG_HEADS,),
        in_specs=[blk(0), blk(1), blk(2), blk(3), par, par,
                  pl.BlockSpec((HEAD_DIM, nb), lambda h: (0, 0)), st],
        out_specs=[par, st],
        out_shape=[jax.ShapeDtypeStruct((HG_WIDTH, nb), F32),
                   jax.ShapeDtypeStruct(state_t.shape, F32)],
        compiler_params=_cparams(("parallel",)),
        name="hgrn_decode",
    )(gates_t, gates_t, gates_t, gates_t, loglb_t, log1mlb_t, gain_t, state_t)


def _lru_gates(xc, wa_ref, ba_ref, wx_ref, bx_ref, lam_ref):
    r = jax.nn.sigmoid(_bdot(xc, wa_ref[...]) + ba_ref[...])
    gi = jax.nn.sigmoid(_bdot(xc, wx_ref[...]) + bx_ref[...])
    log_a = -LRU_C * r * _softplus(-lam_ref[...])
    a = jnp.exp(log_a)
    bterm = jnp.sqrt(_neg_expm1(2.0 * log_a)) * (gi * xc)
    return a, bterm


def _lru_prompt_kernel(lx_ref, lg_ref, cw_ref, cb_ref, wa_ref, ba_ref, wx_ref, bx_ref, lam_ref,
                       oc_ref, hl_ref, ext_scr, h_scr):
    i = pl.program_id(1)
    tt = lx_ref.shape[0]
    pad = 8

    @pl.when(i == 0)
    def _():
        ext_scr[0:pad, :] = jnp.zeros((pad, LRU_WIDTH), F32)
        h_scr[...] = jnp.zeros_like(h_scr)

    x = lx_ref[...]
    ext_scr[pad:pad + tt, :] = x
    xc = cb_ref[...] + cw_ref[CONV_W - 1:CONV_W, :] * x
    for j in range(CONV_W - 1):
        back = CONV_W - 1 - j
        xc = xc + cw_ref[j:j + 1, :] * ext_scr[pad - back:pad - back + tt, :]
    ext_scr[0:pad, :] = x[tt - pad:tt, :]

    a, bterm = _lru_gates(xc, wa_ref, ba_ref, wx_ref, bx_ref, lam_ref)
    row = lax.broadcasted_iota(jnp.int32, (tt, LRU_WIDTH), 0)
    sh = 1
    while sh < tt:
        keep = row >= sh
        b_s = jnp.where(keep, pltpu.roll(bterm, sh, 0), 0.0)
        a_s = jnp.where(keep, pltpu.roll(a, sh, 0), 1.0)
        bterm = a * b_s + bterm
        a = a * a_s
        sh *= 2
    h = a * h_scr[...] + bterm
    h_scr[...] = h[tt - 1:tt, :]
    oc_ref[...] = h * _gelu_tanh(lg_ref[...])

    @pl.when(i == pl.num_programs(1) - 1)
    def _():
        hl_ref[...] = h[tt - 1:tt, :]


def lru_prompt(proj, conv_w, conv_b, wa_bd, ba, wx_bd, bx, lam, batch, seq, tt):
    nt = seq // tt
    base = IN_COLS // LRU_WIDTH - 2

    def col(cblk):
        return pl.BlockSpec((tt, LRU_WIDTH), lambda b, i: (b * nt + i, cblk))

    row_spec = pl.BlockSpec((1, LRU_WIDTH), lambda b, i: (0, 0))
    mat_spec = pl.BlockSpec((LRU_WIDTH, LRU_WIDTH), lambda b, i: (0, 0))
    return pl.pallas_call(
        _lru_prompt_kernel,
        grid=(batch, nt),
        in_specs=[col(base), col(base + 1),
                  pl.BlockSpec((CONV_W, LRU_WIDTH), lambda b, i: (0, 0)), row_spec,
                  mat_spec, row_spec, mat_spec, row_spec, row_spec],
        out_specs=[pl.BlockSpec((tt, LRU_WIDTH), lambda b, i: (b * nt + i, 0)),
                   pl.BlockSpec((None, 1, LRU_WIDTH), lambda b, i: (b, 0, 0))],
        out_shape=[jax.ShapeDtypeStruct((batch * seq, LRU_WIDTH), F32),
                   jax.ShapeDtypeStruct((batch, 1, LRU_WIDTH), F32)],
        scratch_shapes=[pltpu.VMEM((tt + 8, LRU_WIDTH), F32),
                        pltpu.VMEM((1, LRU_WIDTH), F32)],
        compiler_params=_cparams(("parallel", "arbitrary")),
        name="lru_prompt",
    )(proj, proj, conv_w, conv_b, wa_bd, ba, wx_bd, bx, lam)


def _lru_decode_kernel(lx_ref, lg_ref, buf_ref, h0_ref, cw_ref, cb_ref, wa_ref, ba_ref, wx_ref,
                       bx_ref, lam_ref, oc_ref, hn_ref, nbuf_ref):
    x = lx_ref[...]
    buf = buf_ref[...]
    xc = cb_ref[...] + cw_ref[CONV_W - 1:CONV_W, :] * x
    for j in range(CONV_W - 1):
        xc = xc + cw_ref[j:j + 1, :] * buf[:, j * LRU_WIDTH:(j + 1) * LRU_WIDTH]
    a, bterm = _lru_gates(xc, wa_ref, ba_ref, wx_ref, bx_ref, lam_ref)
    h = a * h0_ref[...] + bterm
    hn_ref[...] = h
    oc_ref[...] = h * _gelu_tanh(lg_ref[...])
    nbuf_ref[...] = jnp.concatenate([buf[:, LRU_WIDTH:], x], axis=1)


def lru_decode(proj_d, conv_buf, h0, conv_w, conv_b, wa_bd, ba, wx_bd, bx, lam):
    nb = proj_d.shape[0]
    base = IN_COLS // LRU_WIDTH - 2
    row_spec = pl.BlockSpec((1, LRU_WIDTH), lambda i: (0, 0))
    mat_spec = pl.BlockSpec((LRU_WIDTH, LRU_WIDTH), lambda i: (0, 0))
    act = pl.BlockSpec((nb, LRU_WIDTH), lambda i: (0, 0))
    bufs = pl.BlockSpec((nb, (CONV_W - 1) * LRU_WIDTH), lambda i: (0, 0))
    return pl.pallas_call(
        _lru_decode_kernel,
        grid=(1,),
        in_specs=[pl.BlockSpec((nb, LRU_WIDTH), lambda i: (0, base)),
                  pl.BlockSpec((nb, LRU_WIDTH), lambda i: (0, base + 1)),
                  bufs, act, pl.BlockSpec((CONV_W, LRU_WIDTH), lambda i: (0, 0)), row_spec,
                  mat_spec, row_spec, mat_spec, row_spec, row_spec],
        out_specs=[act, act, bufs],
        out_shape=[jax.ShapeDtypeStruct((nb, LRU_WIDTH), F32),
                   jax.ShapeDtypeStruct((nb, LRU_WIDTH), F32),
                   jax.ShapeDtypeStruct((nb, (CONV_W - 1) * LRU_WIDTH), F32)],
        compiler_params=_cparams(("arbitrary",)),
        name="lru_decode",
    )(proj_d, proj_d, conv_buf, h0, conv_w, conv_b, wa_bd, ba, wx_bd, bx, lam)


def _xattn_prompt_kernel(x_ref, wq_ref, mk_ref, mv_ref, wo_ref, g_ref, b_ref, wr_ref, br_ref, o_ref):
    x = x_ref[...]
    q = _bdot(x, wq_ref[...]).astype(BF16)
    mk = mk_ref[...].astype(BF16)
    mv = mv_ref[...].astype(BF16)
    outs = []
    for h in range(X_HEADS):
        sl = slice(h * HEAD_DIM, (h + 1) * HEAD_DIM)
        s = _bdot_nt(q[:, sl], mk[:, sl]) * SCALE
        m = jnp.max(s, -1, keepdims=True)
        p = jnp.exp(s - m)
        p = p / jnp.sum(p, -1, keepdims=True)
        outs.append(_bdot(p, mv[:, sl]))
    o = jnp.concatenate(outs, axis=1)
    y = _layer_norm(ALPHA * x + _bdot(o, wo_ref[...]), g_ref[...], b_ref[...])
    logits = _bdot(y, wr_ref[...]) + br_ref[...]
    gate, g_idx = _route(logits)
    lane = lax.broadcasted_iota(jnp.int32, gate.shape, 1)
    o_ref[:, :D_MODEL] = y
    o_ref[:, D_MODEL:] = jnp.where(lane == 0, g_idx.astype(F32), gate)


def xattn_prompt(x, wq, mem_kv, wo, g, b, wr, br, batch, seq, tt):
    nt = seq // tt
    const = lambda bi, i: (0, 0)
    return pl.pallas_call(
        _xattn_prompt_kernel,
        grid=(batch, nt),
        in_specs=[pl.BlockSpec((tt, D_MODEL), lambda bi, i: (bi * nt + i, 0)),
                  pl.BlockSpec((D_MODEL, X_WIDTH), const),
                  pl.BlockSpec((N_MEM, X_WIDTH), lambda bi, i: (bi, 0)),
                  pl.BlockSpec((N_MEM, X_WIDTH), lambda bi, i: (bi, 1)),
                  pl.BlockSpec((X_WIDTH, D_MODEL), const),
                  pl.BlockSpec((1, D_MODEL), const),
                  pl.BlockSpec((1, D_MODEL), const),
                  pl.BlockSpec((D_MODEL, ROUTER_LANES), const),
                  pl.BlockSpec((1, ROUTER_LANES), const)],
        out_specs=pl.BlockSpec((tt, XG_WIDTH), lambda bi, i: (bi * nt + i, 0)),
        out_shape=jax.ShapeDtypeStruct((batch * seq, XG_WIDTH), F32),
        compiler_params=_cparams(("parallel", "parallel")),
        name="xattn_prompt",
    )(x, wq, mem_kv, mem_kv, wo, g, b, wr, br)


def _route(logits):
    lane = lax.broadcasted_iota(jnp.int32, logits.shape, 1)
    big = jnp.int32(ROUTER_LANES)
    ninf = -jnp.inf
    gl = jnp.where(lane < N_GROUPS, logits, ninf)
    gm = jnp.max(gl, -1, keepdims=True)
    g_val = 1.0 / jnp.sum(jnp.exp(gl - gm), -1, keepdims=True)
    g_idx = jnp.min(jnp.where(gl == gm, lane, big), -1, keepdims=True)
    lo = N_GROUPS + EXP_PER_GROUP * g_idx
    el = jnp.where((lane >= lo) & (lane < lo + EXP_PER_GROUP), logits, ninf)
    v1 = jnp.max(el, -1, keepdims=True)
    i1 = jnp.min(jnp.where(el == v1, lane, big), -1, keepdims=True)
    el2 = jnp.where(lane == i1, ninf, el)
    v2 = jnp.max(el2, -1, keepdims=True)
    i2 = jnp.min(jnp.where(el2 == v2, lane, big), -1, keepdims=True)
    e2 = jnp.exp(v2 - v1)
    w1 = g_val / (1.0 + e2)
    w2 = g_val * e2 / (1.0 + e2)
    return jnp.where(lane == i1, w1, 0.0) + jnp.where(lane == i2, w2, 0.0), g_idx


def _moe_dense_kernel(x_ref, wr_ref, br_ref, wg_ref, wu_ref, wd_ref, g_ref, b_ref, o_ref,
                      gate_scr, acc_scr):
    e = pl.program_id(1)

    @pl.when(e == 0)
    def _():
        logits = _bdot(x_ref[...], wr_ref[...]) + br_ref[...]
        gate_scr[...] = _route(logits)[0]
        acc_scr[...] = jnp.zeros_like(acc_scr)

    xb = x_ref[...].astype(BF16)
    lane = lax.broadcasted_iota(jnp.int32, gate_scr.shape, 1)
    gcol = jnp.sum(jnp.where(lane == e + N_GROUPS, gate_scr[...], 0.0), -1, keepdims=True)
    hid = _silu(_bdot(xb, wg_ref[...])) * _bdot(xb, wu_ref[...])
    acc_scr[...] += _bdot(hid * gcol, wd_ref[...])

    @pl.when(e == pl.num_programs(1) - 1)
    def _():
        o_ref[...] = _layer_norm(ALPHA * x_ref[...] + acc_scr[...], g_ref[...], b_ref[...])


def moe_dense(x, wr, br, wg, wu, wd, g, b, layer, tm):
    m = x.shape[0]
    return pl.pallas_call(
        _moe_dense_kernel,
        grid=(m // tm, N_EXPERTS),
        in_specs=[pl.BlockSpec((tm, D_MODEL), lambda i, e: (i, 0)),
                  pl.BlockSpec((D_MODEL, ROUTER_LANES), lambda i, e: (0, 0)),
                  pl.BlockSpec((1, ROUTER_LANES), lambda i, e: (0, 0)),
                  pl.BlockSpec((None, None, D_MODEL, EXP_FF), lambda i, e: (layer, e, 0, 0)),
                  pl.BlockSpec((None, None, D_MODEL, EXP_FF), lambda i, e: (layer, e, 0, 0)),
                  pl.BlockSpec((None, None, EXP_FF, D_MODEL), lambda i, e: (layer, e, 0, 0)),
                  pl.BlockSpec((1, D_MODEL), lambda i, e: (0, 0)),
                  pl.BlockSpec((1, D_MODEL), lambda i, e: (0, 0))],
        out_specs=pl.BlockSpec((tm, D_MODEL), lambda i, e: (i, 0)),
        out_shape=jax.ShapeDtypeStruct((m, D_MODEL), F32),
        scratch_shapes=[pltpu.VMEM((tm, ROUTER_LANES), F32), pltpu.VMEM((tm, D_MODEL), F32)],
        compiler_params=_cparams(("parallel", "arbitrary")),
        name="moe_dense",
    )(x, wr, br, wg, wu, wd, g, b)


def _moe_routed_kernel(src_ref, dst_ref, tg_ref, xg_hbm, wg_ref, wu_ref, wd_ref, g_ref, b_ref,
                       y_hbm, xbuf, ybuf, gsem, ssem):
    t = pl.program_id(0)
    nt = pl.num_programs(0)
    slot = t % 2
    tm = xbuf.shape[1]

    def gather_row(tile, sl, r):
        return pltpu.make_async_copy(xg_hbm.at[pl.ds(src_ref[tile * tm + r], 1), :],
                                     xbuf.at[sl, pl.ds(r, 1), :], gsem.at[sl])

    def scatter_row(tile, sl, r):
        return pltpu.make_async_copy(ybuf.at[sl, pl.ds(r, 1), :],
                                     y_hbm.at[pl.ds(dst_ref[tile * tm + r], 1), :], ssem.at[sl])

    def start_gather(tile, sl):
        def body(r, c):
            gather_row(tile, sl, r).start()
            return c
        lax.fori_loop(0, tm, body, 0, unroll=8)

    @pl.when(t == 0)
    def _():
        start_gather(0, 0)

    @pl.when(t + 1 < nt)
    def _():
        start_gather(t + 1, 1 - slot)

    pltpu.make_async_copy(xbuf.at[slot], xbuf.at[slot], gsem.at[slot]).wait()

    @pl.when(t >= 2)
    def _():
        pltpu.make_async_copy(ybuf.at[slot], ybuf.at[slot], ssem.at[slot]).wait()

    x = xbuf[slot, :, :D_MODEL]
    gate = xbuf[slot, :, D_MODEL:]
    xb = x.astype(BF16)
    lane = lax.broadcasted_iota(jnp.int32, gate.shape, 1)
    first = N_GROUPS + EXP_PER_GROUP * tg_ref[t]
    acc = jnp.zeros((tm, D_MODEL), F32)
    for e in range(EXP_PER_GROUP):
        gcol = jnp.sum(jnp.where(lane == first + e, gate, 0.0), -1, keepdims=True)
        hid = _silu(_bdot(xb, wg_ref[e])) * _bdot(xb, wu_ref[e])
        acc = acc + _bdot(hid * gcol, wd_ref[e])
    ybuf[slot] = _layer_norm(ALPHA * x + acc, g_ref[...], b_ref[...])

    def body(r, c):
        scatter_row(t, slot, r).start()
        return c
    lax.fori_loop(0, tm, body, 0, unroll=8)

    @pl.when(t == nt - 1)
    def _():
        pltpu.make_async_copy(ybuf.at[slot], ybuf.at[slot], ssem.at[slot]).wait()
        pltpu.make_async_copy(ybuf.at[1 - slot], ybuf.at[1 - slot], ssem.at[1 - slot]).wait()


def _moe_plan(group_idx, n, tm):
    n_tiles = n // tm + N_GROUPS
    n_slots = n_tiles * tm
    onehot = (group_idx[:, None] == jnp.arange(N_GROUPS)[None, :]).astype(jnp.int32)
    csum = jnp.cumsum(onehot, axis=0)
    counts = csum[-1]
    rank = jnp.sum(onehot * csum, axis=1) - 1
    tiles_g = (counts + tm - 1) // tm
    tile_end = jnp.cumsum(tiles_g)
    slot_base = (tile_end - tiles_g) * tm
    slot = jnp.sum(onehot * slot_base[None, :], axis=1) + rank
    tok = jnp.arange(n, dtype=jnp.int32)
    src = jnp.zeros((n_slots,), jnp.int32).at[slot].set(tok, unique_indices=True)
    valid = jnp.zeros((n_slots,), jnp.int32).at[slot].set(1, unique_indices=True)
    pad_rank = jnp.cumsum(1 - valid) - 1
    dst = jnp.where(valid == 1, src, n + pad_rank).astype(jnp.int32)
    tile_group = jnp.sum((jnp.arange(n_tiles)[:, None] >= tile_end[None, :]).astype(jnp.int32), axis=1)
    tile_group = jnp.minimum(tile_group, N_GROUPS - 1).astype(jnp.int32)
    return src, dst, tile_group


def moe_routed(xg, wg, wu, wd, g, b, layer, n, tm):
    group_idx = xg[:n, D_MODEL].astype(jnp.int32)
    src, dst, tile_group = _moe_plan(group_idx, n, tm)
    n_tiles = tile_group.shape[0]
    wspec = lambda shp: pl.BlockSpec((None, None, EXP_PER_GROUP) + shp,
                                     lambda t, s, d, tg: (layer, tg[t], 0, 0, 0))
    grid_spec = pltpu.PrefetchScalarGridSpec(
        num_scalar_prefetch=3,
        grid=(n_tiles,),
        in_specs=[pl.BlockSpec(memory_space=pl.ANY),
                  wspec((D_MODEL, EXP_FF)), wspec((D_MODEL, EXP_FF)), wspec((EXP_FF, D_MODEL)),
                  pl.BlockSpec((1, D_MODEL), lambda t, s, d, tg: (0, 0)),
                  pl.BlockSpec((1, D_MODEL), lambda t, s, d, tg: (0, 0))],
        out_specs=pl.BlockSpec(memory_space=pl.ANY),
        scratch_shapes=[pltpu.VMEM((2, tm, XG_WIDTH), F32),
                        pltpu.VMEM((2, tm, D_MODEL), F32),
                        pltpu.SemaphoreType.DMA((2,)),
                        pltpu.SemaphoreType.DMA((2,))])
    grouped = lambda w: w.reshape(w.shape[0], N_GROUPS, EXP_PER_GROUP, w.shape[2], w.shape[3])
    return pl.pallas_call(
        _moe_routed_kernel,
        grid_spec=grid_spec,
        out_shape=jax.ShapeDtypeStruct((n_tiles * tm, D_MODEL), F32),
        compiler_params=_cparams(("arbitrary",)),
        name="moe_routed",
    )(src, dst, tile_group, xg, grouped(wg), grouped(wu), grouped(wd), g, b)


def _sc_mesh():
    return plsc.VectorSubcoreMesh(core_axis_name="core", subcore_axis_name="subcore")


def sc_scatter_rows(x, idx, n_out):
    r = x.shape[0]

    @functools.partial(pl.kernel, out_type=jax.ShapeDtypeStruct((n_out, LANES), x.dtype),
                       mesh=_sc_mesh(), scratch_types=[], name="sc_scatter_rows")
    def k(x_hbm, i_hbm, o_hbm):
        def body(x_vmem, i_vmem):
            pltpu.sync_copy(x_vmem, o_hbm.at[i_vmem.at[0]])

        pltpu.emit_pipeline(
            body,
            grid=(r // SC_WINDOW,),
            in_specs=[pl.BlockSpec((SC_WINDOW, LANES), lambda i: (i, 0)),
                      pl.BlockSpec((1, SC_WINDOW), lambda i: (0, i))],
            out_specs=[],
            core_axis_name=("core", "subcore"),
            dimension_semantics=(pltpu.PARALLEL,),
        )(x_hbm, i_hbm)

    return k(x, idx.reshape(1, r))


def sc_gather_rows(table, idx):
    r = idx.shape[0]

    @functools.partial(pl.kernel, out_type=jax.ShapeDtypeStruct((r, LANES), table.dtype),
                       mesh=_sc_mesh(), scratch_types=[], name="sc_gather_rows")
    def k(t_hbm, i_hbm, o_hbm):
        def body(i_vmem, o_vmem):
            pltpu.sync_copy(t_hbm.at[i_vmem.at[0]], o_vmem)

        pltpu.emit_pipeline(
            body,
            grid=(r // SC_WINDOW,),
            in_specs=[pl.BlockSpec((1, SC_WINDOW), lambda i: (0, i))],
            out_specs=[pl.BlockSpec((SC_WINDOW, LANES), lambda i: (i, 0))],
            core_axis_name=("core", "subcore"),
            dimension_semantics=(pltpu.PARALLEL,),
        )(i_hbm, o_hbm)

    return k(table, idx.reshape(1, r))


def _moe_sorted_kernel(tg_ref, nused_ref, xs_ref, wg_ref, wu_ref, wd_ref, g_ref, b_ref, o_ref):
    t = pl.program_id(0)

    @pl.when(t < nused_ref[0])
    def _():
        x = xs_ref[:, :D_MODEL]
        gate = xs_ref[:, D_MODEL:]
        xb = x.astype(BF16)
        lane = lax.broadcasted_iota(jnp.int32, gate.shape, 1)
        first = N_GROUPS + EXP_PER_GROUP * tg_ref[t]
        acc = jnp.zeros(x.shape, F32)
        for e in range(EXP_PER_GROUP):
            gcol = jnp.sum(jnp.where(lane == first + e, gate, 0.0), -1, keepdims=True)
            hid = _silu(_bdot(xb, wg_ref[e])) * _bdot(xb, wu_ref[e])
            acc = acc + _bdot(hid * gcol, wd_ref[e])
        o_ref[...] = _layer_norm(ALPHA * x + acc, g_ref[...], b_ref[...])


def _group_slots(group_idx, n, tm):
    n_tiles = n // tm + N_GROUPS
    onehot = (group_idx[:, None] == jnp.arange(N_GROUPS)[None, :]).astype(jnp.int32)
    csum = jnp.cumsum(onehot, axis=0)
    counts = csum[-1]
    rank = jnp.sum(onehot * csum, axis=1) - 1
    tiles_g = (counts + tm - 1) // tm
    tile_end = jnp.cumsum(tiles_g)
    slot_base = (tile_end - tiles_g) * tm
    slot = (jnp.sum(onehot * slot_base[None, :], axis=1) + rank).astype(jnp.int32)
    tile_group = jnp.sum((jnp.arange(n_tiles)[:, None] >= tile_end[None, :]).astype(jnp.int32), axis=1)
    tile_group = jnp.minimum(tile_group, N_GROUPS - 1).astype(jnp.int32)
    return slot, tile_group, tile_end[-1:].astype(jnp.int32)


def moe_routed_sc(xg, wg, wu, wd, g, b, layer, n, tm):
    slot, tile_group, n_used = _group_slots(xg[:, D_MODEL].astype(jnp.int32), n, tm)
    n_tiles = tile_group.shape[0]
    n_slots = n_tiles * tm
    pin, pout = XG_WIDTH // LANES, D_MODEL // LANES
    idx_in = (slot[:, None] * pin + jnp.arange(pin, dtype=jnp.int32)[None, :]).reshape(-1)
    idx_out = (slot[:, None] * pout + jnp.arange(pout, dtype=jnp.int32)[None, :]).reshape(-1)
    xs = sc_scatter_rows(xg.reshape(n * pin, LANES), idx_in, n_slots * pin).reshape(n_slots, XG_WIDTH)
    wspec = lambda shp: pl.BlockSpec((None, None, EXP_PER_GROUP) + shp,
                                     lambda t, tg, nu: (layer, tg[t], 0, 0, 0))
    grid_spec = pltpu.PrefetchScalarGridSpec(
        num_scalar_prefetch=2,
        grid=(n_tiles,),
        in_specs=[pl.BlockSpec((tm, XG_WIDTH), lambda t, tg, nu: (t, 0)),
                  wspec((D_MODEL, EXP_FF)), wspec((D_MODEL, EXP_FF)), wspec((EXP_FF, D_MODEL)),
                  pl.BlockSpec((1, D_MODEL), lambda t, tg, nu: (0, 0)),
                  pl.BlockSpec((1, D_MODEL), lambda t, tg, nu: (0, 0))],
        out_specs=pl.BlockSpec((tm, D_MODEL), lambda t, tg, nu: (t, 0)))
    grouped = lambda w: w.reshape(w.shape[0], N_GROUPS, EXP_PER_GROUP, w.shape[2], w.shape[3])
    ys = pl.pallas_call(
        _moe_sorted_kernel,
        grid_spec=grid_spec,
        out_shape=jax.ShapeDtypeStruct((n_slots, D_MODEL), F32),
        compiler_params=_cparams(("arbitrary",)),
        name="moe_sorted",
    )(tile_group, n_used, xs, grouped(wg), grouped(wu), grouped(wd), g, b)
    return sc_gather_rows(ys.reshape(n_slots * pout, LANES), idx_out).reshape(n, D_MODEL)


def _t5_bucket(dist):
    max_exact = N_BUCKETS // 2
    d = jnp.maximum(dist, 0)
    df = jnp.maximum(d, 1).astype(F32)
    log_b = max_exact + (jnp.log(df / max_exact) / math.log(MAX_DISTANCE / max_exact)
                         * (N_BUCKETS - max_exact)).astype(jnp.int32)
    return jnp.where(d < max_exact, d, jnp.minimum(log_b, N_BUCKETS - 1))


def _bucket_lookup(rel_bias, bucket):
    out = jnp.zeros(bucket.shape + (rel_bias.shape[1],), F32)
    for i in range(N_BUCKETS):
        out = jnp.where((bucket == i)[..., None], rel_bias[i].astype(F32), out)
    return out


def _prompt_bias(rel_bias):
    qi = jnp.arange(WINDOW)[:, None]
    kj = jnp.arange(2 * WINDOW)[None, :]
    dist = qi + WINDOW - kj
    bias = _bucket_lookup(rel_bias, _t5_bucket(dist)).transpose(2, 0, 1)
    valid = (dist >= 0) & (dist <= WINDOW)
    return jnp.where(valid[None], bias, NEG)


def _decode_bias(rel_bias):
    dist = WINDOW - jnp.arange(WINDOW + 1)
    bias = _bucket_lookup(rel_bias, _t5_bucket(dist))
    out = []
    for part in (bias[:WINDOW], jnp.broadcast_to(bias[WINDOW:], (WINDOW, A_HEADS))):
        for r in range(A_REP):
            out.append(jnp.concatenate([jnp.repeat(part[:, r:r + 1], HEAD_DIM, 1),
                                        jnp.repeat(part[:, A_REP + r:A_REP + r + 1], HEAD_DIM, 1)], 1))
    return jnp.stack(out)


def _block_ones(width):
    idx = jnp.arange(width) // HEAD_DIM
    return (idx[:, None] == idx[None, :])


def _head_rows_mask():
    head = jnp.arange(HG_WIDTH)[None, :] // HEAD_DIM
    return (head == jnp.arange(HG_MROWS)[:, None]).astype(F32)


def _block_diag(w):
    nblk, s, _ = w.shape
    eye = jnp.eye(nblk, dtype=w.dtype)
    return (eye[:, None, :, None] * w[:, :, None, :]).reshape(nblk * s, nblk * s)


def kernel(x_prompt, x_sample, mem_prompt, cache_win_k, cache_win_v, state_hgrn, state_conv, state_lru, cache_mem_k, cache_mem_v, rel_bias, hg_lb, w_in, attn_sink, hg_gain, conv_w, conv_b, lru_wa, lru_ba, lru_wx, lru_bx, lru_lam, w_out, ln1_g, ln1_b, x_wq, x_wk, x_wv, x_wo, ln2_g, ln2_b, r_gw, r_gb, r_ew, r_eb, e_wg, e_wu, e_wd, ln3_g, ln3_b):
    bp, seq, d = x_prompt.shape
    n_tok = bp * seq
    nd = x_sample.shape[0]
    depth = w_in.shape[0]

    lbs = jnp.cumsum(jax.nn.softmax(hg_lb.astype(F32), axis=0), axis=0)
    lbs = lbs - lbs[0]
    loglb = jnp.log(lbs)
    log1mlb = jnp.log1p(-lbs)
    gain4 = jnp.tile(hg_gain, (1, HG_HEADS))

    bias_p = _prompt_bias(rel_bias)
    bias_d = _decode_bias(rel_bias)
    ones128 = _block_ones(LANES).astype(BF16)
    ones256 = _block_ones(HG_WIDTH).astype(BF16)
    bdm256 = _block_ones(HG_WIDTH).astype(F32)
    hmask = _head_rows_mask()

    w_in_b = w_in.astype(BF16)
    w_out_b = w_out.astype(BF16)
    wq_b = x_wq.astype(BF16)
    wkv_b = jnp.concatenate([x_wk, x_wv], axis=-1).astype(BF16)
    wo_b = x_wo.astype(BF16)
    wg_b = e_wg.astype(BF16)
    wu_b = e_wu.astype(BF16)
    wd_b = e_wd.astype(BF16)
    rew = r_ew.transpose(0, 2, 1, 3).reshape(depth, d, N_EXPERTS)
    wr = jnp.concatenate([r_gw, rew, jnp.zeros((depth, d, ROUTER_LANES - N_GROUPS - N_EXPERTS), F32)], -1)
    br = jnp.concatenate([r_gb, r_eb.reshape(depth, N_EXPERTS),
                          jnp.zeros((depth, ROUTER_LANES - N_GROUPS - N_EXPERTS), F32)], -1)

    a_w = A_HEADS * HEAD_DIM
    xp = x_prompt.reshape(bp * seq, d)
    xs = x_sample.reshape(nd, d)
    mem = mem_prompt.reshape(bp * N_MEM, d)
    ck = cache_win_k.reshape(depth, nd, WINDOW, A_KV_HEADS * HEAD_DIM)
    cv = cache_win_v.reshape(depth, nd, WINDOW, A_KV_HEADS * HEAD_DIM)
    cmk = cache_mem_k.reshape(depth, nd, N_MEM, X_WIDTH)
    cmv = cache_mem_v.reshape(depth, nd, N_MEM, X_WIDTH)

    p_wk, p_wv, p_s, p_cb, p_h, p_mk, p_mv = [], [], [], [], [], [], []
    s_wk, s_wv, s_s, s_cb, s_h = [], [], [], [], []
    for l in range(depth):
        row = lambda v: v[l].reshape(1, -1)
        wa_bd = _block_diag(lru_wa[l]).astype(BF16)
        wx_bd = _block_diag(lru_wx[l]).astype(BF16)
        lru_args = (conv_w[l], row(conv_b), wa_bd, row(lru_ba), wx_bd, row(lru_bx), row(lru_lam))
        wo_parts = [w_out_b[l, :a_w], w_out_b[l, a_w:a_w + HG_WIDTH], w_out_b[l, a_w + HG_WIDTH:]]

        proj = matmul(xp, w_in_b[l], 512, 768, rows=n_tok)
        oa = attn_prompt(proj, attn_sink[l], bias_p, bp, seq)
        ob, st = hgrn_prompt(proj, row(loglb), row(log1mlb), row(gain4), bdm256, hmask, bp, seq, 256)
        oc, hl = lru_prompt(proj, *lru_args, bp, seq, 256)
        xp = proj_res_ln(xp, [oa, ob, oc], wo_parts, row(ln1_g), row(ln1_b), 512)
        mkv = matmul(mem, wkv_b[l], 256, 512)
        xg = xattn_prompt(xp, wq_b[l], mkv, wo_b[l], row(ln2_g), row(ln2_b), wr[l], br[l:l + 1],
                          bp, seq, 512)
        xp = moe_routed_sc(xg, wg_b, wu_b, wd_b, row(ln3_g), row(ln3_b), l, n_tok, MOE_TM)

        proj3 = proj.reshape(bp, seq, IN_COLS)
        p_wk.append(proj3[:, seq - WINDOW:, a_w:a_w + LANES].reshape(bp, WINDOW, A_KV_HEADS, HEAD_DIM))
        p_wv.append(proj3[:, seq - WINDOW:, a_w + LANES:a_w + 2 * LANES].reshape(bp, WINDOW, A_KV_HEADS, HEAD_DIM))
        st5 = st.reshape(bp, HG_HEADS, HEAD_DIM, HG_HEADS, HEAD_DIM)
        p_s.append(jnp.stack([st5[:, h, :, h, :] for h in range(HG_HEADS)], 1).transpose(0, 1, 3, 2))
        p_cb.append(proj3[:, seq - (CONV_W - 1):, IN_COLS - 2 * LRU_WIDTH:IN_COLS - LRU_WIDTH])
        p_h.append(hl.reshape(bp, LRU_WIDTH))
        p_mk.append(mkv[:, :X_WIDTH].reshape(bp, N_MEM, X_HEADS, HEAD_DIM))
        p_mv.append(mkv[:, X_WIDTH:].reshape(bp, N_MEM, X_HEADS, HEAD_DIM))

        projd = matmul(xs, w_in_b[l], nd, 768)
        oa, nk, nv = attn_decode(projd, attn_sink[l], bias_d, ones128, ck[l], cv[l], 8)
        gates_t = projd[:, a_w + 2 * LANES:a_w + 2 * LANES + 4 * HG_WIDTH].T
        bc = lambda v: jnp.broadcast_to(v[:, None], (v.shape[0], nd))
        state_t = state_hgrn[l].reshape(nd, -1).T
        ob_t, ns_t = hgrn_decode(gates_t, bc(loglb[l]), bc(log1mlb[l]), bc(hg_gain[l]), state_t)
        oc, nh, nbuf = lru_decode(projd, state_conv[l].reshape(nd, -1), state_lru[l], *lru_args)
        xs = proj_res_ln(xs, [oa, ob_t.T, oc], wo_parts, row(ln1_g), row(ln1_b), nd)
        qd = matmul(xs, wq_b[l], nd, X_WIDTH)
        od = xattn_decode(qd, cmk[l], cmv[l], ones256, 8)
        xs = proj_res_ln(xs, [od], [wo_b[l]], row(ln2_g), row(ln2_b), nd)
        xs = moe_dense(xs, wr[l], br[l:l + 1], wg_b, wu_b, wd_b, row(ln3_g), row(ln3_b), l, nd)

        s_wk.append(nk.reshape(nd, WINDOW, A_KV_HEADS, HEAD_DIM))
        s_wv.append(nv.reshape(nd, WINDOW, A_KV_HEADS, HEAD_DIM))
        s_s.append(ns_t.T.reshape(nd, HG_HEADS, HEAD_DIM, HEAD_DIM))
        s_cb.append(nbuf.reshape(nd, CONV_W - 1, LRU_WIDTH))
        s_h.append(nh)

    return (xp[:n_tok].reshape(bp, seq, d), xs.reshape(nd, 1, d),
            jnp.stack(p_wk), jnp.stack(p_wv), jnp.stack(p_s), jnp.stack(p_cb), jnp.stack(p_h),
            jnp.stack(p_mk), jnp.stack(p_mv),
            jnp.stack(s_wk), jnp.stack(s_wv), jnp.stack(s_s), jnp.stack(s_cb), jnp.stack(s_h))
```

```python
import functools
import math

import jax
import jax.numpy as jnp
from jax import lax
from jax.experimental import pallas as pl
from jax.experimental.pallas import tpu as pltpu
from jax.experimental.pallas import tpu_sc as plsc

F32 = jnp.float32
BF16 = jnp.bfloat16

D_MODEL = 1024
DEPTH = 4
HEAD_DIM = 64
A_HEADS = 8
A_KV_HEADS = 2
A_REP = A_HEADS // A_KV_HEADS
WINDOW = 128
N_BUCKETS = 32
MAX_DISTANCE = 128
HG_WIDTH = 256
HG_HEADS = 4
HG_CHUNK = 64
HG_TB = 16
HG_MROWS = 16
LRU_WIDTH = 256
LRU_BLOCKS = 4
CONV_W = 4
LRU_C = 8.0
N_MEM = 256
X_HEADS = 4
X_WIDTH = X_HEADS * HEAD_DIM
N_GROUPS = 4
EXP_PER_GROUP = 4
N_EXPERTS = N_GROUPS * EXP_PER_GROUP
EXP_FF = D_MODEL // 4
ALPHA = (2 * DEPTH) ** 0.25
LN_EPS = 1e-5
RMS_EPS = 1e-6
IN_COLS = 2304
SCALE = HEAD_DIM ** -0.5
NEG = -1e30
LANES = 128
ROUTER_LANES = 128
XG_WIDTH = D_MODEL + ROUTER_LANES
XG_PLANES = XG_WIDTH // LANES
X_PLANES = D_MODEL // LANES
MOE_TM = 512
SC_WINDOW = 128
VMEM_LIMIT = 48 * 1024 * 1024


def _cparams(sem):
    return pltpu.CompilerParams(dimension_semantics=sem, vmem_limit_bytes=VMEM_LIMIT)


def _bdot(a, b):
    return jnp.dot(a.astype(BF16), b.astype(BF16), preferred_element_type=F32)


def _bdot_nt(a, b):
    return lax.dot_general(a.astype(BF16), b.astype(BF16), (((1,), (1,)), ((), ())),
                           preferred_element_type=F32)


def _bdot_tn(a, b):
    return lax.dot_general(a.astype(BF16), b.astype(BF16), (((0,), (0,)), ((), ())),
                           preferred_element_type=F32)


def _rb(x):
    return x.astype(BF16).astype(F32)


def _silu(x):
    return x * jax.nn.sigmoid(x)


def _neg_expm1(x):
    return -jnp.tanh(0.5 * x) * (jnp.exp(x) + 1.0)


def _softplus(x):
    return jnp.maximum(x, 0.0) + jnp.log1p(jnp.exp(-jnp.abs(x)))


def _gelu_tanh(x):
    return 0.5 * x * (1.0 + jnp.tanh(math.sqrt(2.0 / math.pi) * (x + 0.044715 * (x * x * x))))


def _layer_norm(y, g, b):
    mu = jnp.mean(y, -1, keepdims=True)
    yc = y - mu
    var = jnp.mean(yc * yc, -1, keepdims=True)
    return yc * lax.rsqrt(var + LN_EPS) * g + b


def _rows(x_ref):
    if len(x_ref.shape) == 2:
        return x_ref[...]
    return jnp.concatenate([x_ref[j] for j in range(x_ref.shape[0])], axis=1)


def _rows_spec(x, tm, nargs):
    if x.ndim == 2:
        return pl.BlockSpec((tm, x.shape[1]), (lambda i: (i, 0)) if nargs == 1 else (lambda i, j: (i, 0)))
    blk = (x.shape[0], tm, LANES)
    return pl.BlockSpec(blk, (lambda i: (0, i, 0)) if nargs == 1 else (lambda i, j: (0, i, 0)))


def _mm_kernel(x_ref, w_ref, o_ref):
    o_ref[...] = _bdot(_rows(x_ref), w_ref[...])


def matmul(x, w, tm, tn):
    m = x.shape[-2]
    k, n = w.shape
    return pl.pallas_call(
        _mm_kernel,
        grid=(m // tm, n // tn),
        in_specs=[_rows_spec(x, tm, 2),
                  pl.BlockSpec((k, tn), lambda i, j: (0, j))],
        out_specs=pl.BlockSpec((tm, tn), lambda i, j: (i, j)),
        out_shape=jax.ShapeDtypeStruct((m, n), F32),
        compiler_params=_cparams(("parallel", "parallel")),
        name="matmul",
    )(x, w)


def _proj_res_ln_kernel(n_in, x_ref, *refs):
    a_refs = refs[:n_in]
    w_refs = refs[n_in:2 * n_in]
    g_ref, b_ref, o_ref = refs[2 * n_in:]
    y = ALPHA * _rows(x_ref)
    for a_ref, w_ref in zip(a_refs, w_refs):
        y = y + _bdot(a_ref[...], w_ref[...])
    o_ref[...] = _layer_norm(y, g_ref[...], b_ref[...])


def proj_res_ln(x, a_list, w_list, g, b, tm):
    m = x.shape[-2]
    d = w_list[0].shape[1]
    n_in = len(a_list)
    in_specs = [_rows_spec(x, tm, 1)]
    in_specs += [pl.BlockSpec((tm, a.shape[1]), lambda i: (i, 0)) for a in a_list]
    in_specs += [pl.BlockSpec(w.shape, lambda i: (0, 0)) for w in w_list]
    in_specs += [pl.BlockSpec((1, d), lambda i: (0, 0))] * 2
    return pl.pallas_call(
        functools.partial(_proj_res_ln_kernel, n_in),
        grid=(m // tm,),
        in_specs=in_specs,
        out_specs=pl.BlockSpec((tm, d), lambda i: (i, 0)),
        out_shape=jax.ShapeDtypeStruct((m, d), F32),
        compiler_params=_cparams(("parallel",)),
        name="proj_res_ln",
    )(x, *a_list, *w_list, g, b)


def _attn_prompt_kernel(sink_ref, q_ref, kc_ref, kp_ref, vc_ref, vp_ref, bias_ref, o_ref):
    n = pl.program_id(1)
    col = lax.broadcasted_iota(jnp.int32, (WINDOW, 2 * WINDOW), 1)
    first = jnp.where((n == 0) & (col < WINDOW), NEG, 0.0)
    kk = jnp.concatenate([kp_ref[...], kc_ref[...]], axis=0).astype(BF16)
    vv = jnp.concatenate([vp_ref[...], vc_ref[...]], axis=0).astype(BF16)
    q = q_ref[...].astype(BF16)
    outs = []
    for h in range(A_HEADS):
        g = h // A_REP
        qh = q[:, h * HEAD_DIM:(h + 1) * HEAD_DIM]
        kg = kk[:, g * HEAD_DIM:(g + 1) * HEAD_DIM]
        vg = vv[:, g * HEAD_DIM:(g + 1) * HEAD_DIM]
        s = _bdot_nt(qh, kg) * SCALE + bias_ref[h] + first
        sink = sink_ref[h]
        m = jnp.maximum(jnp.max(s, -1, keepdims=True), sink)
        p = jnp.exp(s - m)
        den = jnp.sum(p, -1, keepdims=True) + jnp.exp(sink - m)
        outs.append(_bdot(p / den, vg))
    o_ref[...] = jnp.concatenate(outs, axis=1)


def attn_prompt(proj, sink, bias, batch, seq):
    nb = seq // WINDOW
    qcol = 0
    kcol = (A_HEADS * HEAD_DIM) // LANES
    vcol = kcol + 1

    def cur(c):
        return lambda b, n: (b * nb + n, c)

    def prev(c):
        return lambda b, n: (b * nb + jnp.maximum(n - 1, 0), c)

    return pl.pallas_call(
        _attn_prompt_kernel,
        grid=(batch, nb),
        in_specs=[pl.BlockSpec(memory_space=pltpu.SMEM),
                  pl.BlockSpec((WINDOW, A_HEADS * HEAD_DIM), cur(qcol)),
                  pl.BlockSpec((WINDOW, LANES), cur(kcol)),
                  pl.BlockSpec((WINDOW, LANES), prev(kcol)),
                  pl.BlockSpec((WINDOW, LANES), cur(vcol)),
                  pl.BlockSpec((WINDOW, LANES), prev(vcol)),
                  pl.BlockSpec((A_HEADS, WINDOW, 2 * WINDOW), lambda b, n: (0, 0, 0))],
        out_specs=pl.BlockSpec((WINDOW, A_HEADS * HEAD_DIM), cur(0)),
        out_shape=jax.ShapeDtypeStruct((batch * seq, A_HEADS * HEAD_DIM), F32),
        compiler_params=_cparams(("parallel", "parallel")),
        name="attn_prompt",
    )(sink, proj, proj, proj, proj, proj, bias)


def _seg_sum(x2d, ones_bd):
    hi = x2d.astype(BF16)
    lo = (x2d - hi.astype(F32)).astype(BF16)
    return (jnp.dot(hi, ones_bd, preferred_element_type=F32)
            + jnp.dot(lo, ones_bd, preferred_element_type=F32))


def _attn_decode_kernel(sink_ref, q_ref, kn_ref, vn_ref, ck_ref, cv_ref, bias_ref, ones_ref,
                        o_ref, ok_ref, ov_ref):
    bb = q_ref.shape[0]
    ck = ck_ref[...]
    cv = cv_ref[...]
    kn = kn_ref[...]
    vn = vn_ref[...]
    q = q_ref[...]
    ones_bd = ones_ref[...]
    ckb, cvb, knb, vnb = _rb(ck), _rb(cv), _rb(kn), _rb(vn)
    lane = lax.broadcasted_iota(jnp.int32, (1, LANES), 1)
    pieces = [None] * A_HEADS
    for r in range(A_REP):
        h0, h1 = r, A_REP + r
        qp = jnp.concatenate([q[:, h0 * HEAD_DIM:(h0 + 1) * HEAD_DIM],
                              q[:, h1 * HEAD_DIM:(h1 + 1) * HEAD_DIM]], axis=1)
        qp = _rb(qp)
        prod = (ckb * qp[:, None, :]).reshape(bb * WINDOW, LANES)
        s = _seg_sum(prod, ones_bd).reshape(bb, WINDOW, LANES) * SCALE + bias_ref[r][None]
        sn = _seg_sum(knb * qp, ones_bd) * SCALE + bias_ref[A_REP + r][0:1, :]
        sink = jnp.where(lane < HEAD_DIM, sink_ref[h0], sink_ref[h1])
        m = jnp.maximum(jnp.maximum(jnp.max(s, axis=1), sn), sink)
        p = jnp.exp(s - m[:, None, :])
        pn = jnp.exp(sn - m)
        den = jnp.sum(p, axis=1) + pn + jnp.exp(sink - m)
        o = jnp.sum(_rb(p / den[:, None, :]) * cvb, axis=1) + _rb(pn / den) * vnb
        pieces[h0] = o[:, :HEAD_DIM]
        pieces[h1] = o[:, HEAD_DIM:]
    o_ref[...] = jnp.concatenate(pieces, axis=1)
    ok_ref[:, 0:WINDOW - 1, :] = ck[:, 1:WINDOW, :]
    ok_ref[:, WINDOW - 1, :] = kn
    ov_ref[:, 0:WINDOW - 1, :] = cv[:, 1:WINDOW, :]
    ov_ref[:, WINDOW - 1, :] = vn


def attn_decode(proj_d, sink, bias_d, ones_bd, cache_k, cache_v, bb):
    nbatch = proj_d.shape[0]
    kcol = (A_HEADS * HEAD_DIM) // LANES
    cache_spec = pl.BlockSpec((bb, WINDOW, LANES), lambda i: (i, 0, 0))
    return pl.pallas_call(
        _attn_decode_kernel,
        grid=(nbatch // bb,),
        in_specs=[pl.BlockSpec(memory_space=pltpu.SMEM),
                  pl.BlockSpec((bb, A_HEADS * HEAD_DIM), lambda i: (i, 0)),
                  pl.BlockSpec((bb, LANES), lambda i: (i, kcol)),
                  pl.BlockSpec((bb, LANES), lambda i: (i, kcol + 1)),
                  cache_spec, cache_spec,
                  pl.BlockSpec((2 * A_REP, WINDOW, LANES), lambda i: (0, 0, 0)),
                  pl.BlockSpec((LANES, LANES), lambda i: (0, 0))],
        out_specs=[pl.BlockSpec((bb, A_HEADS * HEAD_DIM), lambda i: (i, 0)), cache_spec, cache_spec],
        out_shape=[jax.ShapeDtypeStruct((nbatch, A_HEADS * HEAD_DIM), F32),
                   jax.ShapeDtypeStruct(cache_k.shape, F32),
                   jax.ShapeDtypeStruct(cache_v.shape, F32)],
        compiler_params=_cparams(("parallel",)),
        name="attn_decode",
    )(sink, proj_d, proj_d, proj_d, cache_k, cache_v, bias_d, ones_bd)


def _xattn_decode_kernel(q_ref, mk_ref, mv_ref, ones_ref, o_ref):
    bb = q_ref.shape[0]
    q = _rb(q_ref[...])
    mk = _rb(mk_ref[...])
    prod = (mk * q[:, None, :]).reshape(bb * N_MEM, X_WIDTH)
    s = _seg_sum(prod, ones_ref[...]).reshape(bb, N_MEM, X_WIDTH) * SCALE
    m = jnp.max(s, axis=1, keepdims=True)
    p = jnp.exp(s - m)
    den = jnp.sum(p, axis=1, keepdims=True)
    o_ref[...] = jnp.sum(_rb(p / den) * _rb(mv_ref[...]), axis=1)


def xattn_decode(q, mem_k, mem_v, ones_bd, bb):
    nbatch = q.shape[0]
    mem_spec = pl.BlockSpec((bb, N_MEM, X_WIDTH), lambda i: (i, 0, 0))
    return pl.pallas_call(
        _xattn_decode_kernel,
        grid=(nbatch // bb,),
        in_specs=[pl.BlockSpec((bb, X_WIDTH), lambda i: (i, 0)), mem_spec, mem_spec,
                  pl.BlockSpec((X_WIDTH, X_WIDTH), lambda i: (0, 0))],
        out_specs=pl.BlockSpec((bb, X_WIDTH), lambda i: (i, 0)),
        out_shape=jax.ShapeDtypeStruct((nbatch, X_WIDTH), F32),
        compiler_params=_cparams(("parallel",)),
        name="xattn_decode",
    )(q, mem_k, mem_v, ones_bd)


def _hgrn_gates(hq, hf, loglb, log1mlb):
    ls = jnp.minimum(hf, 0.0) - jnp.log1p(jnp.exp(-jnp.abs(hf)))
    b = log1mlb + ls
    lf = jnp.maximum(loglb, b) + jnp.log1p(jnp.exp(-jnp.abs(loglb - b)))
    return _silu(hq), lf, _neg_expm1(lf)


def _hgrn_prompt_kernel(hq_ref, hf_ref, hi_ref, hg_ref, loglb_ref, log1mlb_ref, gain_ref,
                        bdm_ref, hm_ref, ob_ref, st_ref, st_scr, q_scr, k_scr, cum_scr, o_scr):
    i = pl.program_id(1)
    tt = hq_ref.shape[0]
    c = HG_CHUNK
    tb = HG_TB

    @pl.when(i == 0)
    def _():
        st_scr[...] = jnp.zeros_like(st_scr)

    qs, lf, kk = _hgrn_gates(hq_ref[...], hf_ref[...], loglb_ref[...], log1mlb_ref[...])
    row = lax.broadcasted_iota(jnp.int32, (tt, HG_WIDTH), 0) & (c - 1)
    cum = lf
    sh = 1
    while sh < c:
        cum = cum + jnp.where(row >= sh, pltpu.roll(cum, sh, 0), 0.0)
        sh *= 2
    q_scr[...] = qs
    k_scr[...] = kk
    cum_scr[...] = cum

    bdm = bdm_ref[...]
    hmask = hm_ref[...]

    def chunk(ci, carry):
        r0 = pl.multiple_of(ci * c, c)
        r = pl.ds(r0, c)
        cu = cum_scr[r, :]
        q = q_scr[r, :]
        k = k_scr[r, :]
        v = hi_ref[r, :]
        vb = v.astype(BF16)
        qb = _rb(q)
        last = cu[c - 1:c, :]
        st = st_scr[...]
        o_inter = _bdot_nt(q * jnp.exp(cu), st)
        for j in range(c // tb):
            ns = tb * (j + 1)
            ti = lax.broadcasted_iota(jnp.int32, (tb, ns, HG_WIDTH), 0) + tb * j
            si = lax.broadcasted_iota(jnp.int32, (tb, ns, HG_WIDTH), 1)
            cut = cu[tb * j:tb * (j + 1), :]
            dec = jnp.exp(jnp.where(ti >= si, cut[:, None, :] - cu[None, :ns, :], NEG))
            a2 = (dec * k[None, :ns, :]).astype(BF16)
            q4 = (qb[tb * j:tb * (j + 1), None, :] * hmask[None, :, :]).astype(BF16)
            att = lax.dot_general(q4, a2, (((2,), (2,)), ((0,), (0,))),
                                  preferred_element_type=F32)
            w = jnp.dot(att.reshape(tb * HG_MROWS, ns).astype(BF16), vb[:ns, :],
                        preferred_element_type=F32).reshape(tb, HG_MROWS, HG_WIDTH)
            o_intra = jnp.sum(w * hmask[None, :, :], axis=1)
            o_scr[pl.ds(r0 + tb * j, tb), :] = o_intra + o_inter[tb * j:tb * (j + 1), :]
        upd = _bdot_tn(v, k * jnp.exp(last - cu))
        st_scr[...] = st * jnp.exp(last) + upd * bdm
        return carry

    lax.fori_loop(0, tt // c, chunk, 0)

    o = o_scr[...]
    ms = jnp.dot(o * o, bdm, precision=lax.Precision.HIGHEST,
                 preferred_element_type=F32) * (1.0 / HEAD_DIM)
    ob_ref[...] = o * lax.rsqrt(ms + RMS_EPS) * gain_ref[...] * _silu(hg_ref[...])

    @pl.when(i == pl.num_programs(1) - 1)
    def _():
        st_ref[...] = st_scr[...]


def hgrn_prompt(proj, loglb, log1mlb, gain4, bdm, hmask, batch, seq, tt):
    nt = seq // tt
    base = (A_HEADS + 2 * A_KV_HEADS) * HEAD_DIM // HG_WIDTH

    def col(cblk):
        return pl.BlockSpec((tt, HG_WIDTH), lambda b, i: (b * nt + i, cblk))

    row_spec = pl.BlockSpec((1, HG_WIDTH), lambda b, i: (0, 0))
    mat_spec = pl.BlockSpec((HG_WIDTH, HG_WIDTH), lambda b, i: (0, 0))
    return pl.pallas_call(
        _hgrn_prompt_kernel,
        grid=(batch, nt),
        in_specs=[col(base), col(base + 1), col(base + 2), col(base + 3),
                  row_spec, row_spec, row_spec, mat_spec,
                  pl.BlockSpec((HG_MROWS, HG_WIDTH), lambda b, i: (0, 0))],
        out_specs=[pl.BlockSpec((tt, HG_WIDTH), lambda b, i: (b * nt + i, 0)),
                   pl.BlockSpec((None, HG_WIDTH, HG_WIDTH), lambda b, i: (b, 0, 0))],
        out_shape=[jax.ShapeDtypeStruct((batch * seq, HG_WIDTH), F32),
                   jax.ShapeDtypeStruct((batch, HG_WIDTH, HG_WIDTH), F32)],
        scratch_shapes=[pltpu.VMEM((HG_WIDTH, HG_WIDTH), F32),
                        pltpu.VMEM((tt, HG_WIDTH), F32),
                        pltpu.VMEM((tt, HG_WIDTH), F32),
                        pltpu.VMEM((tt, HG_WIDTH), F32),
                        pltpu.VMEM((tt, HG_WIDTH), F32)],
        compiler_params=_cparams(("parallel", "arbitrary")),
        name="hgrn_prompt",
    )(proj, proj, proj, proj, loglb, log1mlb, gain4, bdm, hmask)


def _hgrn_decode_kernel(hq_ref, hf_ref, hi_ref, hg_ref, loglb_ref, log1mlb_ref, gain_ref, s_ref,
                        ob_ref, so_ref):
    nb = hq_ref.shape[1]
    qs, lf, kk = _hgrn_gates(hq_ref[...], hf_ref[...], loglb_ref[...], log1mlb_ref[...])
    v = hi_ref[...]
    f = jnp.exp(lf)
    s = s_ref[...].reshape(HEAD_DIM, HEAD_DIM, nb)
    att = jnp.sum(_rb(qs) * _rb(kk), axis=0, keepdims=True)
    o = _rb(att) * _rb(v) + jnp.sum(_rb(qs * f)[:, None, :] * _rb(s), axis=0)
    s_new = f[:, None, :] * s + _rb(kk)[:, None, :] * _rb(v)[None, :, :]
    so_ref[...] = s_new.reshape(HEAD_DIM * HEAD_DIM, nb)
    ms = jnp.mean(o * o, axis=0, keepdims=True)
    ob_ref[...] = o * lax.rsqrt(ms + RMS_EPS) * gain_ref[...] * _silu(hg_ref[...])


def hgrn_decode(gates_t, loglb_t, log1mlb_t, gain_t, state_t):
    nb = gates_t.shape[1]

    def blk(off):
        return pl.BlockSpec((HEAD_DIM, nb), lambda h: (off * HG_HEADS + h, 0))

    par = pl.BlockSpec((HEAD_DIM, nb), lambda h: (h, 0))
    st = pl.BlockSpec((HEAD_DIM * HEAD_DIM, nb), lambda h: (h, 0))
    return pl.pallas_call(
        _hgrn_decode_kernel,
        grid=(HG_HEADS,),
        in_specs=[blk(0), blk(1), blk(2), blk(3), par, par,
                  pl.BlockSpec((HEAD_DIM, nb), lambda h: (0, 0)), st],
        out_specs=[par, st],
        out_shape=[jax.ShapeDtypeStruct((HG_WIDTH, nb), F32),
                   jax.ShapeDtypeStruct(state_t.shape, F32)],
        compiler_params=_cparams(("parallel",)),
        name="hgrn_decode",
    )(gates_t, gates_t, gates_t, gates_t, loglb_t, log1mlb_t, gain_t, state_t)


def _lru_gates(xc, wa_ref, ba_ref, wx_ref, bx_ref, lam_ref):
    r = jax.nn.sigmoid(_bdot(xc, wa_ref[...]) + ba_ref[...])
    gi = jax.nn.sigmoid(_bdot(xc, wx_ref[...]) + bx_ref[...])
    log_a = -LRU_C * r * _softplus(-lam_ref[...])
    a = jnp.exp(log_a)
    bterm = jnp.sqrt(_neg_expm1(2.0 * log_a)) * (gi * xc)
    return a, bterm


def _lru_prompt_kernel(lx_ref, lg_ref, cw_ref, cb_ref, wa_ref, ba_ref, wx_ref, bx_ref, lam_ref,
                       oc_ref, hl_ref, ext_scr, h_scr):
    i = pl.program_id(1)
    tt = lx_ref.shape[0]
    pad = 8

    @pl.when(i == 0)
    def _():
        ext_scr[0:pad, :] = jnp.zeros((pad, LRU_WIDTH), F32)
        h_scr[...] = jnp.zeros_like(h_scr)

    x = lx_ref[...]
    ext_scr[pad:pad + tt, :] = x
    xc = cb_ref[...] + cw_ref[CONV_W - 1:CONV_W, :] * x
    for j in range(CONV_W - 1):
        back = CONV_W - 1 - j
        xc = xc + cw_ref[j:j + 1, :] * ext_scr[pad - back:pad - back + tt, :]
    ext_scr[0:pad, :] = x[tt - pad:tt, :]

    a, bterm = _lru_gates(xc, wa_ref, ba_ref, wx_ref, bx_ref, lam_ref)
    row = lax.broadcasted_iota(jnp.int32, (tt, LRU_WIDTH), 0)
    sh = 1
    while sh < tt:
        keep = row >= sh
        b_s = jnp.where(keep, pltpu.roll(bterm, sh, 0), 0.0)
        a_s = jnp.where(keep, pltpu.roll(a, sh, 0), 1.0)
        bterm = a * b_s + bterm
        a = a * a_s
        sh *= 2
    h = a * h_scr[...] + bterm
    h_scr[...] = h[tt - 1:tt, :]
    oc_ref[...] = h * _gelu_tanh(lg_ref[...])

    @pl.when(i == pl.num_programs(1) - 1)
    def _():
        hl_ref[...] = h[tt - 1:tt, :]


def lru_prompt(proj, conv_w, conv_b, wa_bd, ba, wx_bd, bx, lam, batch, seq, tt):
    nt = seq // tt
    base = IN_COLS // LRU_WIDTH - 2

    def col(cblk):
        return pl.BlockSpec((tt, LRU_WIDTH), lambda b, i: (b * nt + i, cblk))

    row_spec = pl.BlockSpec((1, LRU_WIDTH), lambda b, i: (0, 0))
    mat_spec = pl.BlockSpec((LRU_WIDTH, LRU_WIDTH), lambda b, i: (0, 0))
    return pl.pallas_call(
        _lru_prompt_kernel,
        grid=(batch, nt),
        in_specs=[col(base), col(base + 1),
                  pl.BlockSpec((CONV_W, LRU_WIDTH), lambda b, i: (0, 0)), row_spec,
                  mat_spec, row_spec, mat_spec, row_spec, row_spec],
        out_specs=[pl.BlockSpec((tt, LRU_WIDTH), lambda b, i: (b * nt + i, 0)),
                   pl.BlockSpec((None, 1, LRU_WIDTH), lambda b, i: (b, 0, 0))],
        out_shape=[jax.ShapeDtypeStruct((batch * seq, LRU_WIDTH), F32),
                   jax.ShapeDtypeStruct((batch, 1, LRU_WIDTH), F32)],
        scratch_shapes=[pltpu.VMEM((tt + 8, LRU_WIDTH), F32),
                        pltpu.VMEM((1, LRU_WIDTH), F32)],
        compiler_params=_cparams(("parallel", "arbitrary")),
        name="lru_prompt",
    )(proj, proj, conv_w, conv_b, wa_bd, ba, wx_bd, bx, lam)


def _lru_decode_kernel(lx_ref, lg_ref, buf_ref, h0_ref, cw_ref, cb_ref, wa_ref, ba_ref, wx_ref,
                       bx_ref, lam_ref, oc_ref, hn_ref, nbuf_ref):
    x = lx_ref[...]
    buf = buf_ref[...]
    xc = cb_ref[...] + cw_ref[CONV_W - 1:CONV_W, :] * x
    for j in range(CONV_W - 1):
        xc = xc + cw_ref[j:j + 1, :] * buf[:, j * LRU_WIDTH:(j + 1) * LRU_WIDTH]
    a, bterm = _lru_gates(xc, wa_ref, ba_ref, wx_ref, bx_ref, lam_ref)
    h = a * h0_ref[...] + bterm
    hn_ref[...] = h
    oc_ref[...] = h * _gelu_tanh(lg_ref[...])
    nbuf_ref[...] = jnp.concatenate([buf[:, LRU_WIDTH:], x], axis=1)


def lru_decode(proj_d, conv_buf, h0, conv_w, conv_b, wa_bd, ba, wx_bd, bx, lam):
    nb = proj_d.shape[0]
    base = IN_COLS // LRU_WIDTH - 2
    row_spec = pl.BlockSpec((1, LRU_WIDTH), lambda i: (0, 0))
    mat_spec = pl.BlockSpec((LRU_WIDTH, LRU_WIDTH), lambda i: (0, 0))
    act = pl.BlockSpec((nb, LRU_WIDTH), lambda i: (0, 0))
    bufs = pl.BlockSpec((nb, (CONV_W - 1) * LRU_WIDTH), lambda i: (0, 0))
    return pl.pallas_call(
        _lru_decode_kernel,
        grid=(1,),
        in_specs=[pl.BlockSpec((nb, LRU_WIDTH), lambda i: (0, base)),
                  pl.BlockSpec((nb, LRU_WIDTH), lambda i: (0, base + 1)),
                  bufs, act, pl.BlockSpec((CONV_W, LRU_WIDTH), lambda i: (0, 0)), row_spec,
                  mat_spec, row_spec, mat_spec, row_spec, row_spec],
        out_specs=[act, act, bufs],
        out_shape=[jax.ShapeDtypeStruct((nb, LRU_WIDTH), F32),
                   jax.ShapeDtypeStruct((nb, LRU_WIDTH), F32),
                   jax.ShapeDtypeStruct((nb, (CONV_W - 1) * LRU_WIDTH), F32)],
        compiler_params=_cparams(("arbitrary",)),
        name="lru_decode",
    )(proj_d, proj_d, conv_buf, h0, conv_w, conv_b, wa_bd, ba, wx_bd, bx, lam)


def _xattn_prompt_kernel(x_ref, wq_ref, mk_ref, mv_ref, wo_ref, g_ref, b_ref, wr_ref, br_ref, o_ref):
    x = x_ref[...]
    q = _bdot(x, wq_ref[...]).astype(BF16)
    mk = mk_ref[...].astype(BF16)
    mv = mv_ref[...].astype(BF16)
    outs = []
    for h in range(X_HEADS):
        sl = slice(h * HEAD_DIM, (h + 1) * HEAD_DIM)
        s = _bdot_nt(q[:, sl], mk[:, sl]) * SCALE
        m = jnp.max(s, -1, keepdims=True)
        p = jnp.exp(s - m)
        p = p / jnp.sum(p, -1, keepdims=True)
        outs.append(_bdot(p, mv[:, sl]))
    o = jnp.concatenate(outs, axis=1)
    y = _layer_norm(ALPHA * x + _bdot(o, wo_ref[...]), g_ref[...], b_ref[...])
    logits = _bdot(y, wr_ref[...]) + br_ref[...]
    gate, g_idx = _route(logits)
    lane = lax.broadcasted_iota(jnp.int32, gate.shape, 1)
    for j in range(D_MODEL // LANES):
        o_ref[j] = y[:, j * LANES:(j + 1) * LANES]
    o_ref[D_MODEL // LANES] = jnp.where(lane == 0, g_idx.astype(F32), gate)


def xattn_prompt(x, wq, mem_kv, wo, g, b, wr, br, batch, seq, tt):
    nt = seq // tt
    const = lambda bi, i: (0, 0)
    return pl.pallas_call(
        _xattn_prompt_kernel,
        grid=(batch, nt),
        in_specs=[pl.BlockSpec((tt, D_MODEL), lambda bi, i: (bi * nt + i, 0)),
                  pl.BlockSpec((D_MODEL, X_WIDTH), const),
                  pl.BlockSpec((N_MEM, X_WIDTH), lambda bi, i: (bi, 0)),
                  pl.BlockSpec((N_MEM, X_WIDTH), lambda bi, i: (bi, 1)),
                  pl.BlockSpec((X_WIDTH, D_MODEL), const),
                  pl.BlockSpec((1, D_MODEL), const),
                  pl.BlockSpec((1, D_MODEL), const),
                  pl.BlockSpec((D_MODEL, ROUTER_LANES), const),
                  pl.BlockSpec((1, ROUTER_LANES), const)],
        out_specs=pl.BlockSpec((XG_PLANES, tt, LANES), lambda bi, i: (0, bi * nt + i, 0)),
        out_shape=jax.ShapeDtypeStruct((XG_PLANES, batch * seq, LANES), F32),
        compiler_params=_cparams(("parallel", "parallel")),
        name="xattn_prompt",
    )(x, wq, mem_kv, mem_kv, wo, g, b, wr, br)


def _route(logits):
    lane = lax.broadcasted_iota(jnp.int32, logits.shape, 1)
    big = jnp.int32(ROUTER_LANES)
    ninf = -jnp.inf
    gl = jnp.where(lane < N_GROUPS, logits, ninf)
    gm = jnp.max(gl, -1, keepdims=True)
    g_val = 1.0 / jnp.sum(jnp.exp(gl - gm), -1, keepdims=True)
    g_idx = jnp.min(jnp.where(gl == gm, lane, big), -1, keepdims=True)
    lo = N_GROUPS + EXP_PER_GROUP * g_idx
    el = jnp.where((lane >= lo) & (lane < lo + EXP_PER_GROUP), logits, ninf)
    v1 = jnp.max(el, -1, keepdims=True)
    i1 = jnp.min(jnp.where(el == v1, lane, big), -1, keepdims=True)
    el2 = jnp.where(lane == i1, ninf, el)
    v2 = jnp.max(el2, -1, keepdims=True)
    i2 = jnp.min(jnp.where(el2 == v2, lane, big), -1, keepdims=True)
    e2 = jnp.exp(v2 - v1)
    w1 = g_val / (1.0 + e2)
    w2 = g_val * e2 / (1.0 + e2)
    return jnp.where(lane == i1, w1, 0.0) + jnp.where(lane == i2, w2, 0.0), g_idx


def _moe_dense_kernel(x_ref, wr_ref, br_ref, wg_ref, wu_ref, wd_ref, g_ref, b_ref, o_ref,
                      gate_scr, acc_scr):
    e = pl.program_id(1)

    @pl.when(e == 0)
    def _():
        logits = _bdot(x_ref[...], wr_ref[...]) + br_ref[...]
        gate_scr[...] = _route(logits)[0]
        acc_scr[...] = jnp.zeros_like(acc_scr)

    xb = x_ref[...].astype(BF16)
    lane = lax.broadcasted_iota(jnp.int32, gate_scr.shape, 1)
    gcol = jnp.sum(jnp.where(lane == e + N_GROUPS, gate_scr[...], 0.0), -1, keepdims=True)
    hid = _silu(_bdot(xb, wg_ref[...])) * _bdot(xb, wu_ref[...])
    acc_scr[...] += _bdot(hid * gcol, wd_ref[...])

    @pl.when(e == pl.num_programs(1) - 1)
    def _():
        o_ref[...] = _layer_norm(ALPHA * x_ref[...] + acc_scr[...], g_ref[...], b_ref[...])


def moe_dense(x, wr, br, wg, wu, wd, g, b, layer, tm):
    m = x.shape[0]
    return pl.pallas_call(
        _moe_dense_kernel,
        grid=(m // tm, N_EXPERTS),
        in_specs=[pl.BlockSpec((tm, D_MODEL), lambda i, e: (i, 0)),
                  pl.BlockSpec((D_MODEL, ROUTER_LANES), lambda i, e: (0, 0)),
                  pl.BlockSpec((1, ROUTER_LANES), lambda i, e: (0, 0)),
                  pl.BlockSpec((None, None, D_MODEL, EXP_FF), lambda i, e: (layer, e, 0, 0)),
                  pl.BlockSpec((None, None, D_MODEL, EXP_FF), lambda i, e: (layer, e, 0, 0)),
                  pl.BlockSpec((None, None, EXP_FF, D_MODEL), lambda i, e: (layer, e, 0, 0)),
                  pl.BlockSpec((1, D_MODEL), lambda i, e: (0, 0)),
                  pl.BlockSpec((1, D_MODEL), lambda i, e: (0, 0))],
        out_specs=pl.BlockSpec((tm, D_MODEL), lambda i, e: (i, 0)),
        out_shape=jax.ShapeDtypeStruct((m, D_MODEL), F32),
        scratch_shapes=[pltpu.VMEM((tm, ROUTER_LANES), F32), pltpu.VMEM((tm, D_MODEL), F32)],
        compiler_params=_cparams(("parallel", "arbitrary")),
        name="moe_dense",
    )(x, wr, br, wg, wu, wd, g, b)


def _moe_routed_kernel(src_ref, dst_ref, tg_ref, xg_hbm, wg_ref, wu_ref, wd_ref, g_ref, b_ref,
                       y_hbm, xbuf, ybuf, gsem, ssem):
    t = pl.program_id(0)
    nt = pl.num_programs(0)
    slot = t % 2
    tm = xbuf.shape[1]

    def gather_row(tile, sl, r):
        return pltpu.make_async_copy(xg_hbm.at[pl.ds(src_ref[tile * tm + r], 1), :],
                                     xbuf.at[sl, pl.ds(r, 1), :], gsem.at[sl])

    def scatter_row(tile, sl, r):
        return pltpu.make_async_copy(ybuf.at[sl, pl.ds(r, 1), :],
                                     y_hbm.at[pl.ds(dst_ref[tile * tm + r], 1), :], ssem.at[sl])

    def start_gather(tile, sl):
        def body(r, c):
            gather_row(tile, sl, r).start()
            return c
        lax.fori_loop(0, tm, body, 0, unroll=8)

    @pl.when(t == 0)
    def _():
        start_gather(0, 0)

    @pl.when(t + 1 < nt)
    def _():
        start_gather(t + 1, 1 - slot)

    pltpu.make_async_copy(xbuf.at[slot], xbuf.at[slot], gsem.at[slot]).wait()

    @pl.when(t >= 2)
    def _():
        pltpu.make_async_copy(ybuf.at[slot], ybuf.at[slot], ssem.at[slot]).wait()

    x = xbuf[slot, :, :D_MODEL]
    gate = xbuf[slot, :, D_MODEL:]
    xb = x.astype(BF16)
    lane = lax.broadcasted_iota(jnp.int32, gate.shape, 1)
    first = N_GROUPS + EXP_PER_GROUP * tg_ref[t]
    acc = jnp.zeros((tm, D_MODEL), F32)
    for e in range(EXP_PER_GROUP):
        gcol = jnp.sum(jnp.where(lane == first + e, gate, 0.0), -1, keepdims=True)
        hid = _silu(_bdot(xb, wg_ref[e])) * _bdot(xb, wu_ref[e])
        acc = acc + _bdot(hid * gcol, wd_ref[e])
    ybuf[slot] = _layer_norm(ALPHA * x + acc, g_ref[...], b_ref[...])

    def body(r, c):
        scatter_row(t, slot, r).start()
        return c
    lax.fori_loop(0, tm, body, 0, unroll=8)

    @pl.when(t == nt - 1)
    def _():
        pltpu.make_async_copy(ybuf.at[slot], ybuf.at[slot], ssem.at[slot]).wait()
        pltpu.make_async_copy(ybuf.at[1 - slot], ybuf.at[1 - slot], ssem.at[1 - slot]).wait()


def _moe_plan(group_idx, n, tm):
    n_tiles = n // tm + N_GROUPS
    n_slots = n_tiles * tm
    onehot = (group_idx[:, None] == jnp.arange(N_GROUPS)[None, :]).astype(jnp.int32)
    csum = jnp.cumsum(onehot, axis=0)
    counts = csum[-1]
    rank = jnp.sum(onehot * csum, axis=1) - 1
    tiles_g = (counts + tm - 1) // tm
    tile_end = jnp.cumsum(tiles_g)
    slot_base = (tile_end - tiles_g) * tm
    slot = jnp.sum(onehot * slot_base[None, :], axis=1) + rank
    tok = jnp.arange(n, dtype=jnp.int32)
    src = jnp.zeros((n_slots,), jnp.int32).at[slot].set(tok, unique_indices=True)
    valid = jnp.zeros((n_slots,), jnp.int32).at[slot].set(1, unique_indices=True)
    pad_rank = jnp.cumsum(1 - valid) - 1
    dst = jnp.where(valid == 1, src, n + pad_rank).astype(jnp.int32)
    tile_group = jnp.sum((jnp.arange(n_tiles)[:, None] >= tile_end[None, :]).astype(jnp.int32), axis=1)
    tile_group = jnp.minimum(tile_group, N_GROUPS - 1).astype(jnp.int32)
    return src, dst, tile_group


def moe_routed(xg, wg, wu, wd, g, b, layer, n, tm):
    group_idx = xg[:n, D_MODEL].astype(jnp.int32)
    src, dst, tile_group = _moe_plan(group_idx, n, tm)
    n_tiles = tile_group.shape[0]
    wspec = lambda shp: pl.BlockSpec((None, None, EXP_PER_GROUP) + shp,
                                     lambda t, s, d, tg: (layer, tg[t], 0, 0, 0))
    grid_spec = pltpu.PrefetchScalarGridSpec(
        num_scalar_prefetch=3,
        grid=(n_tiles,),
        in_specs=[pl.BlockSpec(memory_space=pl.ANY),
                  wspec((D_MODEL, EXP_FF)), wspec((D_MODEL, EXP_FF)), wspec((EXP_FF, D_MODEL)),
                  pl.BlockSpec((1, D_MODEL), lambda t, s, d, tg: (0, 0)),
                  pl.BlockSpec((1, D_MODEL), lambda t, s, d, tg: (0, 0))],
        out_specs=pl.BlockSpec(memory_space=pl.ANY),
        scratch_shapes=[pltpu.VMEM((2, tm, XG_WIDTH), F32),
                        pltpu.VMEM((2, tm, D_MODEL), F32),
                        pltpu.SemaphoreType.DMA((2,)),
                        pltpu.SemaphoreType.DMA((2,))])
    grouped = lambda w: w.reshape(w.shape[0], N_GROUPS, EXP_PER_GROUP, w.shape[2], w.shape[3])
    return pl.pallas_call(
        _moe_routed_kernel,
        grid_spec=grid_spec,
        out_shape=jax.ShapeDtypeStruct((n_tiles * tm, D_MODEL), F32),
        compiler_params=_cparams(("arbitrary",)),
        name="moe_routed",
    )(src, dst, tile_group, xg, grouped(wg), grouped(wu), grouped(wd), g, b)


def _sc_mesh():
    return plsc.VectorSubcoreMesh(core_axis_name="core", subcore_axis_name="subcore")


def sc_scatter_rows(x, idx, n_out):
    r = x.shape[0]

    @functools.partial(pl.kernel, out_type=jax.ShapeDtypeStruct((n_out, LANES), x.dtype),
                       mesh=_sc_mesh(), scratch_types=[], name="sc_scatter_rows")
    def k(x_hbm, i_hbm, o_hbm):
        def body(x_vmem, i_vmem):
            pltpu.sync_copy(x_vmem, o_hbm.at[i_vmem.at[0]])

        pltpu.emit_pipeline(
            body,
            grid=(r // SC_WINDOW,),
            in_specs=[pl.BlockSpec((SC_WINDOW, LANES), lambda i: (i, 0)),
                      pl.BlockSpec((1, SC_WINDOW), lambda i: (0, i))],
            out_specs=[],
            core_axis_name=("core", "subcore"),
            dimension_semantics=(pltpu.PARALLEL,),
        )(x_hbm, i_hbm)

    return k(x, idx.reshape(1, r))


def sc_gather_rows(table, idx):
    r = idx.shape[0]

    @functools.partial(pl.kernel, out_type=jax.ShapeDtypeStruct((r, LANES), table.dtype),
                       mesh=_sc_mesh(), scratch_types=[], name="sc_gather_rows")
    def k(t_hbm, i_hbm, o_hbm):
        def body(i_vmem, o_vmem):
            pltpu.sync_copy(t_hbm.at[i_vmem.at[0]], o_vmem)

        pltpu.emit_pipeline(
            body,
            grid=(r // SC_WINDOW,),
            in_specs=[pl.BlockSpec((1, SC_WINDOW), lambda i: (0, i))],
            out_specs=[pl.BlockSpec((SC_WINDOW, LANES), lambda i: (i, 0))],
            core_axis_name=("core", "subcore"),
            dimension_semantics=(pltpu.PARALLEL,),
        )(i_hbm, o_hbm)

    return k(table, idx.reshape(1, r))


def _moe_sorted_kernel(tg_ref, nused_ref, xs_ref, wg_ref, wu_ref, wd_ref, g_ref, b_ref, o_ref):
    t = pl.program_id(0)

    @pl.when(t < nused_ref[0])
    def _():
        x = jnp.concatenate([xs_ref[j] for j in range(X_PLANES)], axis=1)
        gate = xs_ref[X_PLANES]
        xb = x.astype(BF16)
        lane = lax.broadcasted_iota(jnp.int32, gate.shape, 1)
        first = N_GROUPS + EXP_PER_GROUP * tg_ref[t]
        acc = jnp.zeros(x.shape, F32)
        for e in range(EXP_PER_GROUP):
            gcol = jnp.sum(jnp.where(lane == first + e, gate, 0.0), -1, keepdims=True)
            hid = _silu(_bdot(xb, wg_ref[e])) * _bdot(xb, wu_ref[e])
            acc = acc + _bdot(hid * gcol, wd_ref[e])
        y = _layer_norm(ALPHA * x + acc, g_ref[...], b_ref[...])
        for j in range(X_PLANES):
            o_ref[j] = y[:, j * LANES:(j + 1) * LANES]


def _group_slots(group_idx, n, tm):
    n_tiles = n // tm + N_GROUPS
    onehot = (group_idx[:, None] == jnp.arange(N_GROUPS)[None, :]).astype(jnp.int32)
    csum = jnp.cumsum(onehot, axis=0)
    counts = csum[-1]
    rank = jnp.sum(onehot * csum, axis=1) - 1
    tiles_g = (counts + tm - 1) // tm
    tile_end = jnp.cumsum(tiles_g)
    slot_base = (tile_end - tiles_g) * tm
    slot = (jnp.sum(onehot * slot_base[None, :], axis=1) + rank).astype(jnp.int32)
    tile_group = jnp.sum((jnp.arange(n_tiles)[:, None] >= tile_end[None, :]).astype(jnp.int32), axis=1)
    tile_group = jnp.minimum(tile_group, N_GROUPS - 1).astype(jnp.int32)
    return slot, tile_group, tile_end[-1:].astype(jnp.int32)


def moe_routed_sc(xg, wg, wu, wd, g, b, layer, n, tm):
    slot, tile_group, n_used = _group_slots(xg[X_PLANES, :, 0].astype(jnp.int32), n, tm)
    n_tiles = tile_group.shape[0]
    n_slots = n_tiles * tm
    plane_base = lambda planes: jnp.arange(planes, dtype=jnp.int32)[:, None] * n_slots
    idx_in = (plane_base(XG_PLANES) + slot[None, :]).reshape(-1)
    idx_out = (plane_base(X_PLANES) + slot[None, :]).reshape(-1)
    xs = sc_scatter_rows(xg.reshape(XG_PLANES * n, LANES), idx_in, XG_PLANES * n_slots)
    xs = xs.reshape(XG_PLANES, n_slots, LANES)
    wspec = lambda shp: pl.BlockSpec((None, None, EXP_PER_GROUP) + shp,
                                     lambda t, tg, nu: (layer, tg[t], 0, 0, 0))
    grid_spec = pltpu.PrefetchScalarGridSpec(
        num_scalar_prefetch=2,
        grid=(n_tiles,),
        in_specs=[pl.BlockSpec((XG_PLANES, tm, LANES), lambda t, tg, nu: (0, t, 0)),
                  wspec((D_MODEL, EXP_FF)), wspec((D_MODEL, EXP_FF)), wspec((EXP_FF, D_MODEL)),
                  pl.BlockSpec((1, D_MODEL), lambda t, tg, nu: (0, 0)),
                  pl.BlockSpec((1, D_MODEL), lambda t, tg, nu: (0, 0))],
        out_specs=pl.BlockSpec((X_PLANES, tm, LANES), lambda t, tg, nu: (0, t, 0)))
    grouped = lambda w: w.reshape(w.shape[0], N_GROUPS, EXP_PER_GROUP, w.shape[2], w.shape[3])
    ys = pl.pallas_call(
        _moe_sorted_kernel,
        grid_spec=grid_spec,
        out_shape=jax.ShapeDtypeStruct((X_PLANES, n_slots, LANES), F32),
        compiler_params=_cparams(("arbitrary",)),
        name="moe_sorted",
    )(tile_group, n_used, xs, grouped(wg), grouped(wu), grouped(wd), g, b)
    y = sc_gather_rows(ys.reshape(X_PLANES * n_slots, LANES), idx_out)
    return y.reshape(X_PLANES, n, LANES)


def _t5_bucket(dist):
    max_exact = N_BUCKETS // 2
    d = jnp.maximum(dist, 0)
    df = jnp.maximum(d, 1).astype(F32)
    log_b = max_exact + (jnp.log(df / max_exact) / math.log(MAX_DISTANCE / max_exact)
                         * (N_BUCKETS - max_exact)).astype(jnp.int32)
    return jnp.where(d < max_exact, d, jnp.minimum(log_b, N_BUCKETS - 1))


def _bucket_lookup(rel_bias, bucket):
    out = jnp.zeros(bucket.shape + (rel_bias.shape[1],), F32)
    for i in range(N_BUCKETS):
        out = jnp.where((bucket == i)[..., None], rel_bias[i].astype(F32), out)
    return out


def _prompt_bias(rel_bias):
    qi = jnp.arange(WINDOW)[:, None]
    kj = jnp.arange(2 * WINDOW)[None, :]
    dist = qi + WINDOW - kj
    bias = _bucket_lookup(rel_bias, _t5_bucket(dist)).transpose(2, 0, 1)
    valid = (dist >= 0) & (dist <= WINDOW)
    return jnp.where(valid[None], bias, NEG)


def _decode_bias(rel_bias):
    dist = WINDOW - jnp.arange(WINDOW + 1)
    bias = _bucket_lookup(rel_bias, _t5_bucket(dist))
    out = []
    for part in (bias[:WINDOW], jnp.broadcast_to(bias[WINDOW:], (WINDOW, A_HEADS))):
        for r in range(A_REP):
            out.append(jnp.concatenate([jnp.repeat(part[:, r:r + 1], HEAD_DIM, 1),
                                        jnp.repeat(part[:, A_REP + r:A_REP + r + 1], HEAD_DIM, 1)], 1))
    return jnp.stack(out)


def _block_ones(width):
    idx = jnp.arange(width) // HEAD_DIM
    return (idx[:, None] == idx[None, :])


def _head_rows_mask():
    head = jnp.arange(HG_WIDTH)[None, :] // HEAD_DIM
    return (head == jnp.arange(HG_MROWS)[:, None]).astype(F32)


def _block_diag(w):
    nblk, s, _ = w.shape
    eye = jnp.eye(nblk, dtype=w.dtype)
    return (eye[:, None, :, None] * w[:, :, None, :]).reshape(nblk * s, nblk * s)


def kernel(x_prompt, x_sample, mem_prompt, cache_win_k, cache_win_v, state_hgrn, state_conv, state_lru, cache_mem_k, cache_mem_v, rel_bias, hg_lb, w_in, attn_sink, hg_gain, conv_w, conv_b, lru_wa, lru_ba, lru_wx, lru_bx, lru_lam, w_out, ln1_g, ln1_b, x_wq, x_wk, x_wv, x_wo, ln2_g, ln2_b, r_gw, r_gb, r_ew, r_eb, e_wg, e_wu, e_wd, ln3_g, ln3_b):
    bp, seq, d = x_prompt.shape
    n_tok = bp * seq
    nd = x_sample.shape[0]
    depth = w_in.shape[0]

    lbs = jnp.cumsum(jax.nn.softmax(hg_lb.astype(F32), axis=0), axis=0)
    lbs = lbs - lbs[0]
    loglb = jnp.log(lbs)
    log1mlb = jnp.log1p(-lbs)
    gain4 = jnp.tile(hg_gain, (1, HG_HEADS))

    bias_p = _prompt_bias(rel_bias)
    bias_d = _decode_bias(rel_bias)
    ones128 = _block_ones(LANES).astype(BF16)
    ones256 = _block_ones(HG_WIDTH).astype(BF16)
    bdm256 = _block_ones(HG_WIDTH).astype(F32)
    hmask = _head_rows_mask()

    w_in_b = w_in.astype(BF16)
    w_out_b = w_out.astype(BF16)
    wq_b = x_wq.astype(BF16)
    wkv_b = jnp.concatenate([x_wk, x_wv], axis=-1).astype(BF16)
    wo_b = x_wo.astype(BF16)
    wg_b = e_wg.astype(BF16)
    wu_b = e_wu.astype(BF16)
    wd_b = e_wd.astype(BF16)
    rew = r_ew.transpose(0, 2, 1, 3).reshape(depth, d, N_EXPERTS)
    wr = jnp.concatenate([r_gw, rew, jnp.zeros((depth, d, ROUTER_LANES - N_GROUPS - N_EXPERTS), F32)], -1)
    br = jnp.concatenate([r_gb, r_eb.reshape(depth, N_EXPERTS),
                          jnp.zeros((depth, ROUTER_LANES - N_GROUPS - N_EXPERTS), F32)], -1)

    a_w = A_HEADS * HEAD_DIM
    xp = x_prompt.reshape(bp * seq, d)
    xs = x_sample.reshape(nd, d)
    mem = mem_prompt.reshape(bp * N_MEM, d)
    ck = cache_win_k.reshape(depth, nd, WINDOW, A_KV_HEADS * HEAD_DIM)
    cv = cache_win_v.reshape(depth, nd, WINDOW, A_KV_HEADS * HEAD_DIM)
    cmk = cache_mem_k.reshape(depth, nd, N_MEM, X_WIDTH)
    cmv = cache_mem_v.reshape(depth, nd, N_MEM, X_WIDTH)

    p_wk, p_wv, p_s, p_cb, p_h, p_mk, p_mv = [], [], [], [], [], [], []
    s_wk, s_wv, s_s, s_cb, s_h = [], [], [], [], []
    for l in range(depth):
        row = lambda v: v[l].reshape(1, -1)
        wa_bd = _block_diag(lru_wa[l]).astype(BF16)
        wx_bd = _block_diag(lru_wx[l]).astype(BF16)
        lru_args = (conv_w[l], row(conv_b), wa_bd, row(lru_ba), wx_bd, row(lru_bx), row(lru_lam))
        wo_parts = [w_out_b[l, :a_w], w_out_b[l, a_w:a_w + HG_WIDTH], w_out_b[l, a_w + HG_WIDTH:]]

        proj = matmul(xp, w_in_b[l], 512, 768)
        oa = attn_prompt(proj, attn_sink[l], bias_p, bp, seq)
        ob, st = hgrn_prompt(proj, row(loglb), row(log1mlb), row(gain4), bdm256, hmask, bp, seq, 256)
        oc, hl = lru_prompt(proj, *lru_args, bp, seq, 256)
        xp = proj_res_ln(xp, [oa, ob, oc], wo_parts, row(ln1_g), row(ln1_b), 512)
        mkv = matmul(mem, wkv_b[l], 256, 512)
        xg = xattn_prompt(xp, wq_b[l], mkv, wo_b[l], row(ln2_g), row(ln2_b), wr[l], br[l:l + 1],
                          bp, seq, 512)
        xp = moe_routed_sc(xg, wg_b, wu_b, wd_b, row(ln3_g), row(ln3_b), l, n_tok, MOE_TM)

        proj3 = proj.reshape(bp, seq, IN_COLS)
        p_wk.append(proj3[:, seq - WINDOW:, a_w:a_w + LANES].reshape(bp, WINDOW, A_KV_HEADS, HEAD_DIM))
        p_wv.append(proj3[:, seq - WINDOW:, a_w + LANES:a_w + 2 * LANES].reshape(bp, WINDOW, A_KV_HEADS, HEAD_DIM))
        st5 = st.reshape(bp, HG_HEADS, HEAD_DIM, HG_HEADS, HEAD_DIM)
        p_s.append(jnp.stack([st5[:, h, :, h, :] for h in range(HG_HEADS)], 1).transpose(0, 1, 3, 2))
        p_cb.append(proj3[:, seq - (CONV_W - 1):, IN_COLS - 2 * LRU_WIDTH:IN_COLS - LRU_WIDTH])
        p_h.append(hl.reshape(bp, LRU_WIDTH))
        p_mk.append(mkv[:, :X_WIDTH].reshape(bp, N_MEM, X_HEADS, HEAD_DIM))
        p_mv.append(mkv[:, X_WIDTH:].reshape(bp, N_MEM, X_HEADS, HEAD_DIM))

        projd = matmul(xs, w_in_b[l], nd, 768)
        oa, nk, nv = attn_decode(projd, attn_sink[l], bias_d, ones128, ck[l], cv[l], 8)
        gates_t = projd[:, a_w + 2 * LANES:a_w + 2 * LANES + 4 * HG_WIDTH].T
        bc = lambda v: jnp.broadcast_to(v[:, None], (v.shape[0], nd))
        state_t = state_hgrn[l].reshape(nd, -1).T
        ob_t, ns_t = hgrn_decode(gates_t, bc(loglb[l]), bc(log1mlb[l]), bc(hg_gain[l]), state_t)
        oc, nh, nbuf = lru_decode(projd, state_conv[l].reshape(nd, -1), state_lru[l], *lru_args)
        xs = proj_res_ln(xs, [oa, ob_t.T, oc], wo_parts, row(ln1_g), row(ln1_b), nd)
        qd = matmul(xs, wq_b[l], nd, X_WIDTH)
        od = xattn_decode(qd, cmk[l], cmv[l], ones256, 8)
        xs = proj_res_ln(xs, [od], [wo_b[l]], row(ln2_g), row(ln2_b), nd)
        xs = moe_dense(xs, wr[l], br[l:l + 1], wg_b, wu_b, wd_b, row(ln3_g), row(ln3_b), l, nd)

        s_wk.append(nk.reshape(nd, WINDOW, A_KV_HEADS, HEAD_DIM))
        s_wv.append(nv.reshape(nd, WINDOW, A_KV_HEADS, HEAD_DIM))
        s_s.append(ns_t.T.reshape(nd, HG_HEADS, HEAD_DIM, HEAD_DIM))
        s_cb.append(nbuf.reshape(nd, CONV_W - 1, LRU_WIDTH))
        s_h.append(nh)

    xp = jnp.transpose(xp, (1, 0, 2))
    return (xp.reshape(bp, seq, d), xs.reshape(nd, 1, d),
            jnp.stack(p_wk), jnp.stack(p_wv), jnp.stack(p_s), jnp.stack(p_cb), jnp.stack(p_h),
            jnp.stack(p_mk), jnp.stack(p_mv),
            jnp.stack(s_wk), jnp.stack(s_wv), jnp.stack(s_s), jnp.stack(s_cb), jnp.stack(s_h))
```

```python
import functools
import math

import jax
import jax.numpy as jnp
from jax import lax
from jax.experimental import pallas as pl
from jax.experimental.pallas import tpu as pltpu
from jax.experimental.pallas import tpu_sc as plsc

F32 = jnp.float32
BF16 = jnp.bfloat16

D_MODEL = 1024
DEPTH = 4
HEAD_DIM = 64
A_HEADS = 8
A_KV_HEADS = 2
A_REP = A_HEADS // A_KV_HEADS
WINDOW = 128
A_QB = 4
N_BUCKETS = 32
MAX_DISTANCE = 128
HG_WIDTH = 256
HG_HEADS = 4
HG_CHUNK = 64
HG_TB = 16
HG_MROWS = 16
LRU_WIDTH = 256
LRU_BLOCKS = 4
CONV_W = 4
LRU_C = 8.0
N_MEM = 256
X_HEADS = 4
X_WIDTH = X_HEADS * HEAD_DIM
X_QROWS = 8
N_GROUPS = 4
EXP_PER_GROUP = 4
N_EXPERTS = N_GROUPS * EXP_PER_GROUP
EXP_FF = D_MODEL // 4
ALPHA = (2 * DEPTH) ** 0.25
LN_EPS = 1e-5
RMS_EPS = 1e-6
IN_COLS = 2304
SCALE = HEAD_DIM ** -0.5
NEG = -1e30
LANES = 128
ROUTER_LANES = 128
XG_WIDTH = D_MODEL + ROUTER_LANES
XG_PLANES = XG_WIDTH // LANES
X_PLANES = D_MODEL // LANES
MOE_TM = 512
SC_WINDOW = 128
VMEM_LIMIT = 48 * 1024 * 1024


def _cparams(sem):
    return pltpu.CompilerParams(dimension_semantics=sem, vmem_limit_bytes=VMEM_LIMIT)


def _bdot(a, b):
    return jnp.dot(a.astype(BF16), b.astype(BF16), preferred_element_type=F32)


def _bdot_nt(a, b):
    return lax.dot_general(a.astype(BF16), b.astype(BF16), (((1,), (1,)), ((), ())),
                           preferred_element_type=F32)


def _bdot_tn(a, b):
    return lax.dot_general(a.astype(BF16), b.astype(BF16), (((0,), (0,)), ((), ())),
                           preferred_element_type=F32)


def _rb(x):
    return x.astype(BF16).astype(F32)


def _silu(x):
    return x * jax.nn.sigmoid(x)


def _neg_expm1(x):
    return -jnp.tanh(0.5 * x) * (jnp.exp(x) + 1.0)


def _softplus(x):
    return jnp.maximum(x, 0.0) + jnp.log1p(jnp.exp(-jnp.abs(x)))


def _gelu_tanh(x):
    return 0.5 * x * (1.0 + jnp.tanh(math.sqrt(2.0 / math.pi) * (x + 0.044715 * (x * x * x))))


def _layer_norm(y, g, b):
    mu = jnp.mean(y, -1, keepdims=True)
    yc = y - mu
    var = jnp.mean(yc * yc, -1, keepdims=True)
    return yc * lax.rsqrt(var + LN_EPS) * g + b


def _rows(x_ref):
    if len(x_ref.shape) == 2:
        return x_ref[...]
    return jnp.concatenate([x_ref[j] for j in range(x_ref.shape[0])], axis=1)


def _rows_spec(x, tm, nargs):
    if x.ndim == 2:
        return pl.BlockSpec((tm, x.shape[1]), (lambda i: (i, 0)) if nargs == 1 else (lambda i, j: (i, 0)))
    blk = (x.shape[0], tm, LANES)
    return pl.BlockSpec(blk, (lambda i: (0, i, 0)) if nargs == 1 else (lambda i, j: (0, i, 0)))


def _mm_kernel(x_ref, w_ref, o_ref):
    o_ref[...] = _bdot(_rows(x_ref), w_ref[...])


def matmul(x, w, tm, tn):
    m = x.shape[-2]
    k, n = w.shape
    return pl.pallas_call(
        _mm_kernel,
        grid=(m // tm, n // tn),
        in_specs=[_rows_spec(x, tm, 2),
                  pl.BlockSpec((k, tn), lambda i, j: (0, j))],
        out_specs=pl.BlockSpec((tm, tn), lambda i, j: (i, j)),
        out_shape=jax.ShapeDtypeStruct((m, n), F32),
        compiler_params=_cparams(("parallel", "parallel")),
        name="matmul",
    )(x, w)


def _proj_res_ln_kernel(n_in, x_ref, *refs):
    a_refs = refs[:n_in]
    w_refs = refs[n_in:2 * n_in]
    g_ref, b_ref, o_ref = refs[2 * n_in:]
    y = ALPHA * _rows(x_ref)
    for a_ref, w_ref in zip(a_refs, w_refs):
        y = y + _bdot(a_ref[...], w_ref[...])
    o_ref[...] = _layer_norm(y, g_ref[...], b_ref[...])


def proj_res_ln(x, a_list, w_list, g, b, tm):
    m = x.shape[-2]
    d = w_list[0].shape[1]
    n_in = len(a_list)
    in_specs = [_rows_spec(x, tm, 1)]
    in_specs += [pl.BlockSpec((tm, a.shape[1]), lambda i: (i, 0)) for a in a_list]
    in_specs += [pl.BlockSpec(w.shape, lambda i: (0, 0)) for w in w_list]
    in_specs += [pl.BlockSpec((1, d), lambda i: (0, 0))] * 2
    return pl.pallas_call(
        functools.partial(_proj_res_ln_kernel, n_in),
        grid=(m // tm,),
        in_specs=in_specs,
        out_specs=pl.BlockSpec((tm, d), lambda i: (i, 0)),
        out_shape=jax.ShapeDtypeStruct((m, d), F32),
        compiler_params=_cparams(("parallel",)),
        name="proj_res_ln",
    )(x, *a_list, *w_list, g, b)


def _attn_prompt_kernel(sink_ref, q_ref, kc_ref, kp_ref, vc_ref, vp_ref, bias_ref, o_ref):
    n = pl.program_id(1)
    col = lax.broadcasted_iota(jnp.int32, (WINDOW, 2 * WINDOW), 1)
    first = jnp.where((n == 0) & (col < WINDOW), NEG, 0.0)
    kk = jnp.concatenate([kp_ref[...], kc_ref[...]], axis=0).astype(BF16)
    vv = jnp.concatenate([vp_ref[...], vc_ref[...]], axis=0).astype(BF16)
    q = q_ref[...].astype(BF16)
    for u in range(A_QB):
        rows = slice(u * WINDOW, (u + 1) * WINDOW)
        keys = slice(u * WINDOW, (u + 2) * WINDOW)
        outs = []
        for h in range(A_HEADS):
            g = h // A_REP
            qh = q[rows, h * HEAD_DIM:(h + 1) * HEAD_DIM]
            kg = kk[keys, g * HEAD_DIM:(g + 1) * HEAD_DIM]
            vg = vv[keys, g * HEAD_DIM:(g + 1) * HEAD_DIM]
            s = _bdot_nt(qh, kg) * SCALE + bias_ref[h]
            if u == 0:
                s = s + first
            sink = sink_ref[h]
            m = jnp.maximum(jnp.max(s, -1, keepdims=True), sink)
            p = jnp.exp(s - m)
            den = jnp.sum(p, -1, keepdims=True) + jnp.exp(sink - m)
            outs.append(_bdot(p / den, vg))
        o_ref[rows, :] = jnp.concatenate(outs, axis=1)


def attn_prompt(proj, sink, bias, batch, seq):
    nb = seq // WINDOW
    ns = nb // A_QB
    tq = A_QB * WINDOW
    qcol = 0
    kcol = (A_HEADS * HEAD_DIM) // LANES
    vcol = kcol + 1

    def cur(c):
        return lambda b, n: (b * ns + n, c)

    def prev(c):
        return lambda b, n: (b * nb + jnp.maximum(n * A_QB - 1, 0), c)

    return pl.pallas_call(
        _attn_prompt_kernel,
        grid=(batch, ns),
        in_specs=[pl.BlockSpec(memory_space=pltpu.SMEM),
                  pl.BlockSpec((tq, A_HEADS * HEAD_DIM), cur(qcol)),
                  pl.BlockSpec((tq, LANES), cur(kcol)),
                  pl.BlockSpec((WINDOW, LANES), prev(kcol)),
                  pl.BlockSpec((tq, LANES), cur(vcol)),
                  pl.BlockSpec((WINDOW, LANES), prev(vcol)),
                  pl.BlockSpec((A_HEADS, WINDOW, 2 * WINDOW), lambda b, n: (0, 0, 0))],
        out_specs=pl.BlockSpec((tq, A_HEADS * HEAD_DIM), cur(0)),
        out_shape=jax.ShapeDtypeStruct((batch * seq, A_HEADS * HEAD_DIM), F32),
        compiler_params=_cparams(("parallel", "parallel")),
        name="attn_prompt",
    )(sink, proj, proj, proj, proj, proj, bias)


def _attn_decode_kernel(qb_ref, kn_ref, vn_ref, knt_ref, vnt_ref, ck_ref, cv_ref, tab_ref,
                        o_ref, ok_ref, ov_ref):
    bb = qb_ref.shape[0]
    ck = ck_ref[...]
    cv = cv_ref[...]
    qb = qb_ref[...]
    kn = kn_ref[...]
    vn = vn_ref[...]
    bias_j = tab_ref[0]
    bias_n = tab_ref[1][:, 0:1]
    sink = tab_ref[2][:, 0:1]
    s = lax.dot_general(qb.astype(BF16), ck.astype(BF16), (((2,), (1,)), ((0,), (0,))),
                        preferred_element_type=F32) * SCALE + bias_j[None]
    sn = jnp.sum(_rb(qb) * _rb(kn)[:, None, :], -1, keepdims=True) * SCALE + bias_n[None]
    m = jnp.maximum(jnp.maximum(jnp.max(s, -1, keepdims=True), sn), sink[None])
    p = jnp.exp(s - m)
    pn = jnp.exp(sn - m)
    den = jnp.sum(p, -1, keepdims=True) + pn + jnp.exp(sink[None] - m)
    o = lax.dot_general((p / den).astype(BF16), cv.astype(BF16), (((2,), (2,)), ((0,), (0,))),
                        preferred_element_type=F32)
    o_ref[...] = o + _rb(pn / den) * _rb(vn)[:, None, :]
    lane = lax.broadcasted_iota(jnp.int32, (LANES, LANES), 1)
    for b in range(bb):
        ok_ref[b] = jnp.where(lane == WINDOW - 1, knt_ref[:, b:b + 1], pltpu.roll(ck[b], WINDOW - 1, 1))
        ov_ref[b] = jnp.where(lane == WINDOW - 1, vnt_ref[:, b:b + 1], pltpu.roll(cv[b], WINDOW - 1, 1))


def attn_decode(qblk, proj_d, table, cache_k, cache_v, layer, bb):
    nbatch = proj_d.shape[0]
    a_w = A_HEADS * HEAD_DIM
    cols = lambda c: proj_d[:, c:c + LANES].reshape(nbatch // bb, bb, LANES).transpose(0, 2, 1)
    knt, vnt = cols(a_w), cols(a_w + LANES)
    kcol = (A_HEADS * HEAD_DIM) // LANES
    cache_spec = pl.BlockSpec((None, bb, LANES, WINDOW), lambda i: (layer, i, 0, 0))
    col_spec = pl.BlockSpec((None, LANES, bb), lambda i: (i, 0, 0))
    return pl.pallas_call(
        _attn_decode_kernel,
        grid=(nbatch // bb,),
        in_specs=[pl.BlockSpec((bb, A_HEADS, LANES), lambda i: (i, 0, 0)),
                  pl.BlockSpec((bb, LANES), lambda i: (i, kcol)),
                  pl.BlockSpec((bb, LANES), lambda i: (i, kcol + 1)),
                  col_spec, col_spec, cache_spec, cache_spec,
                  pl.BlockSpec((3, A_HEADS, WINDOW), lambda i: (0, 0, 0))],
        out_specs=[pl.BlockSpec((bb, A_HEADS, LANES), lambda i: (i, 0, 0)), cache_spec, cache_spec],
        out_shape=[jax.ShapeDtypeStruct((nbatch, A_HEADS, LANES), F32),
                   jax.ShapeDtypeStruct(cache_k.shape, F32),
                   jax.ShapeDtypeStruct(cache_v.shape, F32)],
        input_output_aliases={5: 1, 6: 2},
        compiler_params=_cparams(("arbitrary",)),
        name="attn_decode",
    )(qblk, proj_d, proj_d, knt, vnt, cache_k, cache_v, table)


def _xattn_decode_kernel(qb_ref, mk_ref, mv_ref, hm_ref, o_ref):
    qb = qb_ref[...]
    s = lax.dot_general(qb.astype(BF16), mk_ref[...].astype(BF16), (((2,), (1,)), ((0,), (0,))),
                        preferred_element_type=F32) * SCALE
    m = jnp.max(s, -1, keepdims=True)
    p = jnp.exp(s - m)
    p = p / jnp.sum(p, -1, keepdims=True)
    o = lax.dot_general(p.astype(BF16), mv_ref[...].astype(BF16), (((2,), (2,)), ((0,), (0,))),
                        preferred_element_type=F32)
    o_ref[...] = jnp.sum(o * hm_ref[...][None], axis=1)


def xattn_decode(qblk, mem_k, mem_v, hmask, layer, bb):
    nbatch = qblk.shape[0]
    mem_spec = pl.BlockSpec((None, bb, X_WIDTH, N_MEM), lambda i: (layer, i, 0, 0))
    return pl.pallas_call(
        _xattn_decode_kernel,
        grid=(nbatch // bb,),
        in_specs=[pl.BlockSpec((bb, X_QROWS, X_WIDTH), lambda i: (i, 0, 0)), mem_spec, mem_spec,
                  pl.BlockSpec((X_QROWS, X_WIDTH), lambda i: (0, 0))],
        out_specs=pl.BlockSpec((bb, X_WIDTH), lambda i: (i, 0)),
        out_shape=jax.ShapeDtypeStruct((nbatch, X_WIDTH), F32),
        compiler_params=_cparams(("parallel",)),
        name="xattn_decode",
    )(qblk, mem_k, mem_v, hmask)


def _hgrn_gates(hq, hf, loglb, log1mlb):
    ls = jnp.minimum(hf, 0.0) - jnp.log1p(jnp.exp(-jnp.abs(hf)))
    b = log1mlb + ls
    lf = jnp.maximum(loglb, b) + jnp.log1p(jnp.exp(-jnp.abs(loglb - b)))
    return _silu(hq), lf, _neg_expm1(lf)


def _hgrn_prompt_kernel(hq_ref, hf_ref, hi_ref, hg_ref, loglb_ref, log1mlb_ref, gain_ref,
                        bdm_ref, hm_ref, ob_ref, st_ref, st_scr, q_scr, k_scr, cum_scr, o_scr):
    i = pl.program_id(1)
    tt = hq_ref.shape[0]
    c = HG_CHUNK
    tb = HG_TB

    @pl.when(i == 0)
    def _():
        st_scr[...] = jnp.zeros_like(st_scr)

    qs, lf, kk = _hgrn_gates(hq_ref[...], hf_ref[...], loglb_ref[...], log1mlb_ref[...])
    row = lax.broadcasted_iota(jnp.int32, (tt, HG_WIDTH), 0) & (c - 1)
    cum = lf
    sh = 1
    while sh < c:
        cum = cum + jnp.where(row >= sh, pltpu.roll(cum, sh, 0), 0.0)
        sh *= 2
    q_scr[...] = qs
    k_scr[...] = kk
    cum_scr[...] = cum

    bdm = bdm_ref[...]
    hmask = hm_ref[...]
    def chunk(ci, carry):
        r0 = pl.multiple_of(ci * c, c)
        r = pl.ds(r0, c)
        cu = cum_scr[r, :]
        q = q_scr[r, :]
        k = k_scr[r, :]
        v = hi_ref[r, :]
        vb = v.astype(BF16)
        qb = _rb(q)
        last = cu[c - 1:c, :]
        st = st_scr[...]
        o_inter = _bdot_nt(q * jnp.exp(cu), st)
        for j in range(c // tb):
            ns = tb * (j + 1)
            ti = lax.broadcasted_iota(jnp.int32, (tb, ns, HG_WIDTH), 0) + tb * j
            si = lax.broadcasted_iota(jnp.int32, (tb, ns, HG_WIDTH), 1)
            cut = cu[tb * j:tb * (j + 1), :]
            dec = jnp.exp(jnp.where(ti >= si, cut[:, None, :] - cu[None, :ns, :], NEG))
            a2 = (dec * k[None, :ns, :]).astype(BF16)
            q4 = (qb[tb * j:tb * (j + 1), None, :] * hmask[None, :, :]).astype(BF16)
            att = lax.dot_general(q4, a2, (((2,), (2,)), ((0,), (0,))),
                                  preferred_element_type=F32)
            w = jnp.dot(att.reshape(tb * HG_MROWS, ns).astype(BF16), vb[:ns, :],
                        preferred_element_type=F32).reshape(tb, HG_MROWS, HG_WIDTH)
            o_intra = jnp.sum(w * hmask[None, :, :], axis=1)
            o_scr[pl.ds(r0 + tb * j, tb), :] = o_intra + o_inter[tb * j:tb * (j + 1), :]
        upd = _bdot_tn(v, k * jnp.exp(last - cu))
        st_scr[...] = st * jnp.exp(last) + upd * bdm
        return carry

    lax.fori_loop(0, tt // c, chunk, 0)

    o = o_scr[...]
    ms = jnp.dot(o * o, bdm, precision=lax.Precision.HIGHEST,
                 preferred_element_type=F32) * (1.0 / HEAD_DIM)
    ob_ref[...] = o * lax.rsqrt(ms + RMS_EPS) * gain_ref[...] * _silu(hg_ref[...])

    @pl.when(i == pl.num_programs(1) - 1)
    def _():
        st_ref[...] = st_scr[...]


def hgrn_prompt(proj, loglb, log1mlb, gain4, bdm, hmask, batch, seq, tt):
    nt = seq // tt
    base = (A_HEADS + 2 * A_KV_HEADS) * HEAD_DIM // HG_WIDTH

    def col(cblk):
        return pl.BlockSpec((tt, HG_WIDTH), lambda b, i: (b * nt + i, cblk))

    row_spec = pl.BlockSpec((1, HG_WIDTH), lambda b, i: (0, 0))
    mat_spec = pl.BlockSpec((HG_WIDTH, HG_WIDTH), lambda b, i: (0, 0))
    return pl.pallas_call(
        _hgrn_prompt_kernel,
        grid=(batch, nt),
        in_specs=[col(base), col(base + 1), col(base + 2), col(base + 3),
                  row_spec, row_spec, row_spec, mat_spec,
                  pl.BlockSpec((HG_MROWS, HG_WIDTH), lambda b, i: (0, 0))],
        out_specs=[pl.BlockSpec((tt, HG_WIDTH), lambda b, i: (b * nt + i, 0)),
                   pl.BlockSpec((None, HG_WIDTH, HG_WIDTH), lambda b, i: (b, 0, 0))],
        out_shape=[jax.ShapeDtypeStruct((batch * seq, HG_WIDTH), F32),
                   jax.ShapeDtypeStruct((batch, HG_WIDTH, HG_WIDTH), F32)],
        scratch_shapes=[pltpu.VMEM((HG_WIDTH, HG_WIDTH), F32),
                        pltpu.VMEM((tt, HG_WIDTH), F32),
                        pltpu.VMEM((tt, HG_WIDTH), F32),
                        pltpu.VMEM((tt, HG_WIDTH), F32),
                        pltpu.VMEM((tt, HG_WIDTH), F32)],
        compiler_params=_cparams(("parallel", "arbitrary")),
        name="hgrn_prompt",
    )(proj, proj, proj, proj, loglb, log1mlb, gain4, bdm, hmask)


def _hgrn_decode_kernel(hq_ref, hf_ref, hi_ref, hg_ref, loglb_ref, log1mlb_ref, gain_ref, s_ref,
                        ob_ref, so_ref):
    nb = hq_ref.shape[1]
    qs, lf, kk = _hgrn_gates(hq_ref[...], hf_ref[...], loglb_ref[...], log1mlb_ref[...])
    v = hi_ref[...]
    f = jnp.exp(lf)
    s = s_ref[...].reshape(HEAD_DIM, HEAD_DIM, nb)
    att = jnp.sum(_rb(qs) * _rb(kk), axis=0, keepdims=True)
    o = _rb(att) * _rb(v) + jnp.sum(_rb(qs * f)[:, None, :] * _rb(s), axis=0)
    s_new = f[:, None, :] * s + _rb(kk)[:, None, :] * _rb(v)[None, :, :]
    so_ref[...] = s_new.reshape(HEAD_DIM * HEAD_DIM, nb)
    ms = jnp.mean(o * o, axis=0, keepdims=True)
    ob_ref[...] = o * lax.rsqrt(ms + RMS_EPS) * gain_ref[...] * _silu(hg_ref[...])


def hgrn_decode(gates_t, loglb_t, log1mlb_t, gain_t, state_t, layer):
    nb = gates_t.shape[1]

    def blk(off):
        return pl.BlockSpec((HEAD_DIM, nb), lambda h: (off * HG_HEADS + h, 0))

    par = pl.BlockSpec((HEAD_DIM, nb), lambda h: (h, 0))
    st = pl.BlockSpec((None, HEAD_DIM * HEAD_DIM, nb), lambda h: (layer, h, 0))
    return pl.pallas_call(
        _hgrn_decode_kernel,
        grid=(HG_HEADS,),
        in_specs=[blk(0), blk(1), blk(2), blk(3), par, par,
                  pl.BlockSpec((HEAD_DIM, nb), lambda h: (0, 0)), st],
        out_specs=[par, st],
        out_shape=[jax.ShapeDtypeStruct((HG_WIDTH, nb), F32),
                   jax.ShapeDtypeStruct(state_t.shape, F32)],
        input_output_aliases={7: 1},
        compiler_params=_cparams(("arbitrary",)),
        name="hgrn_decode",
    )(gates_t, gates_t, gates_t, gates_t, loglb_t, log1mlb_t, gain_t, state_t)


def _lru_gates(xc, wa_ref, ba_ref, wx_ref, bx_ref, lam_ref):
    r = jax.nn.sigmoid(_bdot(xc, wa_ref[...]) + ba_ref[...])
    gi = jax.nn.sigmoid(_bdot(xc, wx_ref[...]) + bx_ref[...])
    log_a = -LRU_C * r * _softplus(-lam_ref[...])
    a = jnp.exp(log_a)
    bterm = jnp.sqrt(_neg_expm1(2.0 * log_a)) * (gi * xc)
    return a, bterm


def _lru_prompt_kernel(lx_ref, lg_ref, cw_ref, cb_ref, wa_ref, ba_ref, wx_ref, bx_ref, lam_ref,
                       oc_ref, hl_ref, ext_scr, h_scr):
    i = pl.program_id(1)
    tt = lx_ref.shape[0]
    pad = 8

    @pl.when(i == 0)
    def _():
        ext_scr[0:pad, :] = jnp.zeros((pad, LRU_WIDTH), F32)
        h_scr[...] = jnp.zeros_like(h_scr)

    x = lx_ref[...]
    ext_scr[pad:pad + tt, :] = x
    xc = cb_ref[...] + cw_ref[CONV_W - 1:CONV_W, :] * x
    for j in range(CONV_W - 1):
        back = CONV_W - 1 - j
        xc = xc + cw_ref[j:j + 1, :] * ext_scr[pad - back:pad - back + tt, :]
    ext_scr[0:pad, :] = x[tt - pad:tt, :]

    a, bterm = _lru_gates(xc, wa_ref, ba_ref, wx_ref, bx_ref, lam_ref)
    row = lax.broadcasted_iota(jnp.int32, (tt, LRU_WIDTH), 0)
    sh = 1
    while sh < tt:
        keep = row >= sh
        b_s = jnp.where(keep, pltpu.roll(bterm, sh, 0), 0.0)
        a_s = jnp.where(keep, pltpu.roll(a, sh, 0), 1.0)
        bterm = a * b_s + bterm
        a = a * a_s
        sh *= 2
    h = a * h_scr[...] + bterm
    h_scr[...] = h[tt - 1:tt, :]
    oc_ref[...] = h * _gelu_tanh(lg_ref[...])

    @pl.when(i == pl.num_programs(1) - 1)
    def _():
        hl_ref[...] = h[tt - 1:tt, :]


def lru_prompt(proj, conv_w, conv_b, wa_bd, ba, wx_bd, bx, lam, batch, seq, tt):
    nt = seq // tt
    base = IN_COLS // LRU_WIDTH - 2

    def col(cblk):
        return pl.BlockSpec((tt, LRU_WIDTH), lambda b, i: (b * nt + i, cblk))

    row_spec = pl.BlockSpec((1, LRU_WIDTH), lambda b, i: (0, 0))
    mat_spec = pl.BlockSpec((LRU_WIDTH, LRU_WIDTH), lambda b, i: (0, 0))
    return pl.pallas_call(
        _lru_prompt_kernel,
        grid=(batch, nt),
        in_specs=[col(base), col(base + 1),
                  pl.BlockSpec((CONV_W, LRU_WIDTH), lambda b, i: (0, 0)), row_spec,
                  mat_spec, row_spec, mat_spec, row_spec, row_spec],
        out_specs=[pl.BlockSpec((tt, LRU_WIDTH), lambda b, i: (b * nt + i, 0)),
                   pl.BlockSpec((None, 1, LRU_WIDTH), lambda b, i: (b, 0, 0))],
        out_shape=[jax.ShapeDtypeStruct((batch * seq, LRU_WIDTH), F32),
                   jax.ShapeDtypeStruct((batch, 1, LRU_WIDTH), F32)],
        scratch_shapes=[pltpu.VMEM((tt + 8, LRU_WIDTH), F32),
                        pltpu.VMEM((1, LRU_WIDTH), F32)],
        compiler_params=_cparams(("parallel", "arbitrary")),
        name="lru_prompt",
    )(proj, proj, conv_w, conv_b, wa_bd, ba, wx_bd, bx, lam)


def _lru_decode_kernel(lx_ref, lg_ref, buf_ref, h0_ref, cw_ref, cb_ref, wa_ref, ba_ref, wx_ref,
                       bx_ref, lam_ref, oc_ref, hn_ref, nbuf_ref):
    x = lx_ref[...]
    buf = buf_ref[...]
    xc = cb_ref[...] + cw_ref[CONV_W - 1:CONV_W, :] * x
    for j in range(CONV_W - 1):
        xc = xc + cw_ref[j:j + 1, :] * buf[:, j * LRU_WIDTH:(j + 1) * LRU_WIDTH]
    a, bterm = _lru_gates(xc, wa_ref, ba_ref, wx_ref, bx_ref, lam_ref)
    h = a * h0_ref[...] + bterm
    hn_ref[...] = h
    oc_ref[...] = h * _gelu_tanh(lg_ref[...])
    nbuf_ref[...] = jnp.concatenate([buf[:, LRU_WIDTH:], x], axis=1)


def lru_decode(proj_d, conv_buf, h0, conv_w, conv_b, wa_bd, ba, wx_bd, bx, lam):
    nb = proj_d.shape[0]
    base = IN_COLS // LRU_WIDTH - 2
    row_spec = pl.BlockSpec((1, LRU_WIDTH), lambda i: (0, 0))
    mat_spec = pl.BlockSpec((LRU_WIDTH, LRU_WIDTH), lambda i: (0, 0))
    act = pl.BlockSpec((nb, LRU_WIDTH), lambda i: (0, 0))
    bufs = pl.BlockSpec((nb, (CONV_W - 1) * LRU_WIDTH), lambda i: (0, 0))
    return pl.pallas_call(
        _lru_decode_kernel,
        grid=(1,),
        in_specs=[pl.BlockSpec((nb, LRU_WIDTH), lambda i: (0, base)),
                  pl.BlockSpec((nb, LRU_WIDTH), lambda i: (0, base + 1)),
                  bufs, act, pl.BlockSpec((CONV_W, LRU_WIDTH), lambda i: (0, 0)), row_spec,
                  mat_spec, row_spec, mat_spec, row_spec, row_spec],
        out_specs=[act, act, bufs],
        out_shape=[jax.ShapeDtypeStruct((nb, LRU_WIDTH), F32),
                   jax.ShapeDtypeStruct((nb, LRU_WIDTH), F32),
                   jax.ShapeDtypeStruct((nb, (CONV_W - 1) * LRU_WIDTH), F32)],
        compiler_params=_cparams(("arbitrary",)),
        name="lru_decode",
    )(proj_d, proj_d, conv_buf, h0, conv_w, conv_b, wa_bd, ba, wx_bd, bx, lam)


def _xattn_prompt_kernel(x_ref, wq_ref, mk_ref, mv_ref, wo_ref, g_ref, b_ref, wr_ref, br_ref, o_ref):
    x = x_ref[...]
    q = _bdot(x, wq_ref[...]).astype(BF16)
    mk = mk_ref[...].astype(BF16)
    mv = mv_ref[...].astype(BF16)
    outs = []
    for h in range(X_HEADS):
        sl = slice(h * HEAD_DIM, (h + 1) * HEAD_DIM)
        s = _bdot_nt(q[:, sl], mk[:, sl]) * SCALE
        m = jnp.max(s, -1, keepdims=True)
        p = jnp.exp(s - m)
        p = p / jnp.sum(p, -1, keepdims=True)
        outs.append(_bdot(p, mv[:, sl]))
    o = jnp.concatenate(outs, axis=1)
    y = _layer_norm(ALPHA * x + _bdot(o, wo_ref[...]), g_ref[...], b_ref[...])
    logits = _bdot(y, wr_ref[...]) + br_ref[...]
    gate, g_idx = _route(logits)
    lane = lax.broadcasted_iota(jnp.int32, gate.shape, 1)
    for j in range(D_MODEL // LANES):
        o_ref[j] = y[:, j * LANES:(j + 1) * LANES]
    o_ref[D_MODEL // LANES] = jnp.where(lane == 0, g_idx.astype(F32), gate)


def xattn_prompt(x, wq, mem_kv, wo, g, b, wr, br, batch, seq, tt):
    nt = seq // tt
    const = lambda bi, i: (0, 0)
    return pl.pallas_call(
        _xattn_prompt_kernel,
        grid=(batch, nt),
        in_specs=[pl.BlockSpec((tt, D_MODEL), lambda bi, i: (bi * nt + i, 0)),
                  pl.BlockSpec((D_MODEL, X_WIDTH), const),
                  pl.BlockSpec((N_MEM, X_WIDTH), lambda bi, i: (bi, 0)),
                  pl.BlockSpec((N_MEM, X_WIDTH), lambda bi, i: (bi, 1)),
                  pl.BlockSpec((X_WIDTH, D_MODEL), const),
                  pl.BlockSpec((1, D_MODEL), const),
                  pl.BlockSpec((1, D_MODEL), const),
                  pl.BlockSpec((D_MODEL, ROUTER_LANES), const),
                  pl.BlockSpec((1, ROUTER_LANES), const)],
        out_specs=pl.BlockSpec((XG_PLANES, tt, LANES), lambda bi, i: (0, bi * nt + i, 0)),
        out_shape=jax.ShapeDtypeStruct((XG_PLANES, batch * seq, LANES), F32),
        compiler_params=_cparams(("parallel", "parallel")),
        name="xattn_prompt",
    )(x, wq, mem_kv, mem_kv, wo, g, b, wr, br)


def _route(logits):
    lane = lax.broadcasted_iota(jnp.int32, logits.shape, 1)
    big = jnp.int32(ROUTER_LANES)
    ninf = -jnp.inf
    gl = jnp.where(lane < N_GROUPS, logits, ninf)
    gm = jnp.max(gl, -1, keepdims=True)
    g_val = 1.0 / jnp.sum(jnp.exp(gl - gm), -1, keepdims=True)
    g_idx = jnp.min(jnp.where(gl == gm, lane, big), -1, keepdims=True)
    lo = N_GROUPS + EXP_PER_GROUP * g_idx
    el = jnp.where((lane >= lo) & (lane < lo + EXP_PER_GROUP), logits, ninf)
    v1 = jnp.max(el, -1, keepdims=True)
    i1 = jnp.min(jnp.where(el == v1, lane, big), -1, keepdims=True)
    el2 = jnp.where(lane == i1, ninf, el)
    v2 = jnp.max(el2, -1, keepdims=True)
    i2 = jnp.min(jnp.where(el2 == v2, lane, big), -1, keepdims=True)
    e2 = jnp.exp(v2 - v1)
    w1 = g_val / (1.0 + e2)
    w2 = g_val * e2 / (1.0 + e2)
    return jnp.where(lane == i1, w1, 0.0) + jnp.where(lane == i2, w2, 0.0), g_idx


def _moe_dense_kernel(x_ref, wr_ref, br_ref, wg_ref, wu_ref, wd_ref, g_ref, b_ref, o_ref,
                      gate_scr, acc_scr):
    e = pl.program_id(1)

    @pl.when(e == 0)
    def _():
        logits = _bdot(x_ref[...], wr_ref[...]) + br_ref[...]
        gate_scr[...] = _route(logits)[0]
        acc_scr[...] = jnp.zeros_like(acc_scr)

    xb = x_ref[...].astype(BF16)
    lane = lax.broadcasted_iota(jnp.int32, gate_scr.shape, 1)
    gcol = jnp.sum(jnp.where(lane == e + N_GROUPS, gate_scr[...], 0.0), -1, keepdims=True)
    hid = _silu(_bdot(xb, wg_ref[...])) * _bdot(xb, wu_ref[...])
    acc_scr[...] += _bdot(hid * gcol, wd_ref[...])

    @pl.when(e == pl.num_programs(1) - 1)
    def _():
        o_ref[...] = _layer_norm(ALPHA * x_ref[...] + acc_scr[...], g_ref[...], b_ref[...])


def moe_dense(x, wr, br, wg, wu, wd, g, b, layer, tm):
    m = x.shape[0]
    return pl.pallas_call(
        _moe_dense_kernel,
        grid=(m // tm, N_EXPERTS),
        in_specs=[pl.BlockSpec((tm, D_MODEL), lambda i, e: (i, 0)),
                  pl.BlockSpec((D_MODEL, ROUTER_LANES), lambda i, e: (0, 0)),
                  pl.BlockSpec((1, ROUTER_LANES), lambda i, e: (0, 0)),
                  pl.BlockSpec((None, None, D_MODEL, EXP_FF), lambda i, e: (layer, e, 0, 0)),
                  pl.BlockSpec((None, None, D_MODEL, EXP_FF), lambda i, e: (layer, e, 0, 0)),
                  pl.BlockSpec((None, None, EXP_FF, D_MODEL), lambda i, e: (layer, e, 0, 0)),
                  pl.BlockSpec((1, D_MODEL), lambda i, e: (0, 0)),
                  pl.BlockSpec((1, D_MODEL), lambda i, e: (0, 0))],
        out_specs=pl.BlockSpec((tm, D_MODEL), lambda i, e: (i, 0)),
        out_shape=jax.ShapeDtypeStruct((m, D_MODEL), F32),
        scratch_shapes=[pltpu.VMEM((tm, ROUTER_LANES), F32), pltpu.VMEM((tm, D_MODEL), F32)],
        compiler_params=_cparams(("parallel", "arbitrary")),
        name="moe_dense",
    )(x, wr, br, wg, wu, wd, g, b)


def _moe_routed_kernel(src_ref, dst_ref, tg_ref, xg_hbm, wg_ref, wu_ref, wd_ref, g_ref, b_ref,
                       y_hbm, xbuf, ybuf, gsem, ssem):
    t = pl.program_id(0)
    nt = pl.num_programs(0)
    slot = t % 2
    tm = xbuf.shape[1]

    def gather_row(tile, sl, r):
        return pltpu.make_async_copy(xg_hbm.at[pl.ds(src_ref[tile * tm + r], 1), :],
                                     xbuf.at[sl, pl.ds(r, 1), :], gsem.at[sl])

    def scatter_row(tile, sl, r):
        return pltpu.make_async_copy(ybuf.at[sl, pl.ds(r, 1), :],
                                     y_hbm.at[pl.ds(dst_ref[tile * tm + r], 1), :], ssem.at[sl])

    def start_gather(tile, sl):
        def body(r, c):
            gather_row(tile, sl, r).start()
            return c
        lax.fori_loop(0, tm, body, 0, unroll=8)

    @pl.when(t == 0)
    def _():
        start_gather(0, 0)

    @pl.when(t + 1 < nt)
    def _():
        start_gather(t + 1, 1 - slot)

    pltpu.make_async_copy(xbuf.at[slot], xbuf.at[slot], gsem.at[slot]).wait()

    @pl.when(t >= 2)
    def _():
        pltpu.make_async_copy(ybuf.at[slot], ybuf.at[slot], ssem.at[slot]).wait()

    x = xbuf[slot, :, :D_MODEL]
    gate = xbuf[slot, :, D_MODEL:]
    xb = x.astype(BF16)
    lane = lax.broadcasted_iota(jnp.int32, gate.shape, 1)
    first = N_GROUPS + EXP_PER_GROUP * tg_ref[t]
    acc = jnp.zeros((tm, D_MODEL), F32)
    for e in range(EXP_PER_GROUP):
        gcol = jnp.sum(jnp.where(lane == first + e, gate, 0.0), -1, keepdims=True)
        hid = _silu(_bdot(xb, wg_ref[e])) * _bdot(xb, wu_ref[e])
        acc = acc + _bdot(hid * gcol, wd_ref[e])
    ybuf[slot] = _layer_norm(ALPHA * x + acc, g_ref[...], b_ref[...])

    def body(r, c):
        scatter_row(t, slot, r).start()
        return c
    lax.fori_loop(0, tm, body, 0, unroll=8)

    @pl.when(t == nt - 1)
    def _():
        pltpu.make_async_copy(ybuf.at[slot], ybuf.at[slot], ssem.at[slot]).wait()
        pltpu.make_async_copy(ybuf.at[1 - slot], ybuf.at[1 - slot], ssem.at[1 - slot]).wait()


def _moe_plan(group_idx, n, tm):
    n_tiles = n // tm + N_GROUPS
    n_slots = n_tiles * tm
    onehot = (group_idx[:, None] == jnp.arange(N_GROUPS)[None, :]).astype(jnp.int32)
    csum = jnp.cumsum(onehot, axis=0)
    counts = csum[-1]
    rank = jnp.sum(onehot * csum, axis=1) - 1
    tiles_g = (counts + tm - 1) // tm
    tile_end = jnp.cumsum(tiles_g)
    slot_base = (tile_end - tiles_g) * tm
    slot = jnp.sum(onehot * slot_base[None, :], axis=1) + rank
    tok = jnp.arange(n, dtype=jnp.int32)
    src = jnp.zeros((n_slots,), jnp.int32).at[slot].set(tok, unique_indices=True)
    valid = jnp.zeros((n_slots,), jnp.int32).at[slot].set(1, unique_indices=True)
    pad_rank = jnp.cumsum(1 - valid) - 1
    dst = jnp.where(valid == 1, src, n + pad_rank).astype(jnp.int32)
    tile_group = jnp.sum((jnp.arange(n_tiles)[:, None] >= tile_end[None, :]).astype(jnp.int32), axis=1)
    tile_group = jnp.minimum(tile_group, N_GROUPS - 1).astype(jnp.int32)
    return src, dst, tile_group


def moe_routed(xg, wg, wu, wd, g, b, layer, n, tm):
    group_idx = xg[:n, D_MODEL].astype(jnp.int32)
    src, dst, tile_group = _moe_plan(group_idx, n, tm)
    n_tiles = tile_group.shape[0]
    wspec = lambda shp: pl.BlockSpec((None, None, EXP_PER_GROUP) + shp,
                                     lambda t, s, d, tg: (layer, tg[t], 0, 0, 0))
    grid_spec = pltpu.PrefetchScalarGridSpec(
        num_scalar_prefetch=3,
        grid=(n_tiles,),
        in_specs=[pl.BlockSpec(memory_space=pl.ANY),
                  wspec((D_MODEL, EXP_FF)), wspec((D_MODEL, EXP_FF)), wspec((EXP_FF, D_MODEL)),
                  pl.BlockSpec((1, D_MODEL), lambda t, s, d, tg: (0, 0)),
                  pl.BlockSpec((1, D_MODEL), lambda t, s, d, tg: (0, 0))],
        out_specs=pl.BlockSpec(memory_space=pl.ANY),
        scratch_shapes=[pltpu.VMEM((2, tm, XG_WIDTH), F32),
                        pltpu.VMEM((2, tm, D_MODEL), F32),
                        pltpu.SemaphoreType.DMA((2,)),
                        pltpu.SemaphoreType.DMA((2,))])
    grouped = lambda w: w.reshape(w.shape[0], N_GROUPS, EXP_PER_GROUP, w.shape[2], w.shape[3])
    return pl.pallas_call(
        _moe_routed_kernel,
        grid_spec=grid_spec,
        out_shape=jax.ShapeDtypeStruct((n_tiles * tm, D_MODEL), F32),
        compiler_params=_cparams(("arbitrary",)),
        name="moe_routed",
    )(src, dst, tile_group, xg, grouped(wg), grouped(wu), grouped(wd), g, b)


def _sc_mesh():
    return plsc.VectorSubcoreMesh(core_axis_name="core", subcore_axis_name="subcore")


def sc_scatter_rows(x, idx, n_out):
    r = x.shape[0]

    @functools.partial(pl.kernel, out_type=jax.ShapeDtypeStruct((n_out, LANES), x.dtype),
                       mesh=_sc_mesh(), scratch_types=[], name="sc_scatter_rows")
    def k(x_hbm, i_hbm, o_hbm):
        def body(x_vmem, i_vmem):
            pltpu.sync_copy(x_vmem, o_hbm.at[i_vmem.at[0]])

        pltpu.emit_pipeline(
            body,
            grid=(r // SC_WINDOW,),
            in_specs=[pl.BlockSpec((SC_WINDOW, LANES), lambda i: (i, 0)),
                      pl.BlockSpec((1, SC_WINDOW), lambda i: (0, i))],
            out_specs=[],
            core_axis_name=("core", "subcore"),
            dimension_semantics=(pltpu.PARALLEL,),
        )(x_hbm, i_hbm)

    return k(x, idx.reshape(1, r))


def sc_gather_rows(table, idx):
    r = idx.shape[0]

    @functools.partial(pl.kernel, out_type=jax.ShapeDtypeStruct((r, LANES), table.dtype),
                       mesh=_sc_mesh(), scratch_types=[], name="sc_gather_rows")
    def k(t_hbm, i_hbm, o_hbm):
        def body(i_vmem, o_vmem):
            pltpu.sync_copy(t_hbm.at[i_vmem.at[0]], o_vmem)

        pltpu.emit_pipeline(
            body,
            grid=(r // SC_WINDOW,),
            in_specs=[pl.BlockSpec((1, SC_WINDOW), lambda i: (0, i))],
            out_specs=[pl.BlockSpec((SC_WINDOW, LANES), lambda i: (i, 0))],
            core_axis_name=("core", "subcore"),
            dimension_semantics=(pltpu.PARALLEL,),
        )(i_hbm, o_hbm)

    return k(table, idx.reshape(1, r))


def _moe_sorted_kernel(tg_ref, nused_ref, xs_ref, wg_ref, wu_ref, wd_ref, g_ref, b_ref, o_ref):
    t = pl.program_id(0)

    @pl.when(t < nused_ref[0])
    def _():
        x = jnp.concatenate([xs_ref[j] for j in range(X_PLANES)], axis=1)
        gate = xs_ref[X_PLANES]
        xb = x.astype(BF16)
        lane = lax.broadcasted_iota(jnp.int32, gate.shape, 1)
        first = N_GROUPS + EXP_PER_GROUP * tg_ref[t]
        acc = jnp.zeros(x.shape, F32)
        for e in range(EXP_PER_GROUP):
            gcol = jnp.sum(jnp.where(lane == first + e, gate, 0.0), -1, keepdims=True)
            hid = _silu(_bdot(xb, wg_ref[e])) * _bdot(xb, wu_ref[e])
            acc = acc + _bdot(hid * gcol, wd_ref[e])
        y = _layer_norm(ALPHA * x + acc, g_ref[...], b_ref[...])
        for j in range(X_PLANES):
            o_ref[j] = y[:, j * LANES:(j + 1) * LANES]


def _group_slots(group_idx, n, tm):
    n_tiles = n // tm + N_GROUPS
    onehot = (group_idx[:, None] == jnp.arange(N_GROUPS)[None, :]).astype(jnp.int32)
    csum = jnp.cumsum(onehot, axis=0)
    counts = csum[-1]
    rank = jnp.sum(onehot * csum, axis=1) - 1
    tiles_g = (counts + tm - 1) // tm
    tile_end = jnp.cumsum(tiles_g)
    slot_base = (tile_end - tiles_g) * tm
    slot = (jnp.sum(onehot * slot_base[None, :], axis=1) + rank).astype(jnp.int32)
    tile_group = jnp.sum((jnp.arange(n_tiles)[:, None] >= tile_end[None, :]).astype(jnp.int32), axis=1)
    tile_group = jnp.minimum(tile_group, N_GROUPS - 1).astype(jnp.int32)
    return slot, tile_group, tile_end[-1:].astype(jnp.int32)


def moe_routed_sc(xg, wg, wu, wd, g, b, layer, n, tm):
    slot, tile_group, n_used = _group_slots(xg[X_PLANES, :, 0].astype(jnp.int32), n, tm)
    n_tiles = tile_group.shape[0]
    n_slots = n_tiles * tm
    plane_base = lambda planes: jnp.arange(planes, dtype=jnp.int32)[:, None] * n_slots
    idx_in = (plane_base(XG_PLANES) + slot[None, :]).reshape(-1)
    idx_out = (plane_base(X_PLANES) + slot[None, :]).reshape(-1)
    xs = sc_scatter_rows(xg.reshape(XG_PLANES * n, LANES), idx_in, XG_PLANES * n_slots)
    xs = xs.reshape(XG_PLANES, n_slots, LANES)
    wspec = lambda shp: pl.BlockSpec((None, None, EXP_PER_GROUP) + shp,
                                     lambda t, tg, nu: (layer, tg[t], 0, 0, 0))
    grid_spec = pltpu.PrefetchScalarGridSpec(
        num_scalar_prefetch=2,
        grid=(n_tiles,),
        in_specs=[pl.BlockSpec((XG_PLANES, tm, LANES), lambda t, tg, nu: (0, t, 0)),
                  wspec((D_MODEL, EXP_FF)), wspec((D_MODEL, EXP_FF)), wspec((EXP_FF, D_MODEL)),
                  pl.BlockSpec((1, D_MODEL), lambda t, tg, nu: (0, 0)),
                  pl.BlockSpec((1, D_MODEL), lambda t, tg, nu: (0, 0))],
        out_specs=pl.BlockSpec((X_PLANES, tm, LANES), lambda t, tg, nu: (0, t, 0)))
    grouped = lambda w: w.reshape(w.shape[0], N_GROUPS, EXP_PER_GROUP, w.shape[2], w.shape[3])
    ys = pl.pallas_call(
        _moe_sorted_kernel,
        grid_spec=grid_spec,
        out_shape=jax.ShapeDtypeStruct((X_PLANES, n_slots, LANES), F32),
        compiler_params=_cparams(("arbitrary",)),
        name="moe_sorted",
    )(tile_group, n_used, xs, grouped(wg), grouped(wu), grouped(wd), g, b)
    y = sc_gather_rows(ys.reshape(X_PLANES * n_slots, LANES), idx_out)
    return y.reshape(X_PLANES, n, LANES)


def _t5_bucket(dist):
    max_exact = N_BUCKETS // 2
    d = jnp.maximum(dist, 0)
    df = jnp.maximum(d, 1).astype(F32)
    log_b = max_exact + (jnp.log(df / max_exact) / math.log(MAX_DISTANCE / max_exact)
                         * (N_BUCKETS - max_exact)).astype(jnp.int32)
    return jnp.where(d < max_exact, d, jnp.minimum(log_b, N_BUCKETS - 1))


def _bucket_lookup(rel_bias, bucket):
    out = jnp.zeros(bucket.shape + (rel_bias.shape[1],), F32)
    for i in range(N_BUCKETS):
        out = jnp.where((bucket == i)[..., None], rel_bias[i].astype(F32), out)
    return out


def _prompt_bias(rel_bias):
    qi = jnp.arange(WINDOW)[:, None]
    kj = jnp.arange(2 * WINDOW)[None, :]
    dist = qi + WINDOW - kj
    bias = _bucket_lookup(rel_bias, _t5_bucket(dist)).transpose(2, 0, 1)
    valid = (dist >= 0) & (dist <= WINDOW)
    return jnp.where(valid[None], bias, NEG)


def _decode_table(rel_bias, attn_sink):
    dist = WINDOW - jnp.arange(WINDOW + 1)
    bias = _bucket_lookup(rel_bias, _t5_bucket(dist)).T
    depth = attn_sink.shape[0]
    wide = lambda v: jnp.broadcast_to(v[..., None], v.shape + (WINDOW,))
    per_layer = lambda t: jnp.broadcast_to(t[None], (depth,) + t.shape)
    return jnp.stack([per_layer(bias[:, :WINDOW]), per_layer(wide(bias[:, WINDOW])),
                      wide(attn_sink.astype(F32))], axis=1)


def _block_ones(width):
    idx = jnp.arange(width) // HEAD_DIM
    return (idx[:, None] == idx[None, :])


def _head_rows_mask():
    head = jnp.arange(HG_WIDTH)[None, :] // HEAD_DIM
    return (head == jnp.arange(HG_MROWS)[:, None]).astype(F32)


def _block_diag(w):
    nblk, s, _ = w.shape
    eye = jnp.eye(nblk, dtype=w.dtype)
    return (eye[:, None, :, None] * w[:, :, None, :]).reshape(nblk * s, nblk * s)


def kernel(x_prompt, x_sample, mem_prompt, cache_win_k, cache_win_v, state_hgrn, state_conv, state_lru, cache_mem_k, cache_mem_v, rel_bias, hg_lb, w_in, attn_sink, hg_gain, conv_w, conv_b, lru_wa, lru_ba, lru_wx, lru_bx, lru_lam, w_out, ln1_g, ln1_b, x_wq, x_wk, x_wv, x_wo, ln2_g, ln2_b, r_gw, r_gb, r_ew, r_eb, e_wg, e_wu, e_wd, ln3_g, ln3_b):
    bp, seq, d = x_prompt.shape
    n_tok = bp * seq
    nd = x_sample.shape[0]
    depth = w_in.shape[0]

    lbs = jnp.cumsum(jax.nn.softmax(hg_lb.astype(F32), axis=0), axis=0)
    lbs = lbs - lbs[0]
    loglb = jnp.log(lbs)
    log1mlb = jnp.log1p(-lbs)
    gain4 = jnp.tile(hg_gain, (1, HG_HEADS))

    bias_p = _prompt_bias(rel_bias)
    bdm256 = _block_ones(HG_WIDTH).astype(F32)
    hmask = _head_rows_mask()

    w_in_b = w_in.astype(BF16)
    w_out_b = w_out.astype(BF16)
    wq_b = x_wq.astype(BF16)
    wkv_b = jnp.concatenate([x_wk, x_wv], axis=-1).astype(BF16)
    wo_b = x_wo.astype(BF16)
    wg_b = e_wg.astype(BF16)
    wu_b = e_wu.astype(BF16)
    wd_b = e_wd.astype(BF16)
    rew = r_ew.transpose(0, 2, 1, 3).reshape(depth, d, N_EXPERTS)
    wr = jnp.concatenate([r_gw, rew, jnp.zeros((depth, d, ROUTER_LANES - N_GROUPS - N_EXPERTS), F32)], -1)
    br = jnp.concatenate([r_gb, r_eb.reshape(depth, N_EXPERTS),
                          jnp.zeros((depth, ROUTER_LANES - N_GROUPS - N_EXPERTS), F32)], -1)

    a_w = A_HEADS * HEAD_DIM
    xp = x_prompt.reshape(bp * seq, d)
    xs = x_sample.reshape(nd, d)
    mem = mem_prompt.reshape(bp * N_MEM, d)
    ckt = cache_win_k.transpose(0, 1, 3, 4, 2).reshape(depth, nd, LANES, WINDOW)
    cvt = cache_win_v.transpose(0, 1, 3, 4, 2).reshape(depth, nd, LANES, WINDOW)
    cmkt = cache_mem_k.transpose(0, 1, 3, 4, 2).reshape(depth, nd, X_WIDTH, N_MEM)
    cmvt = cache_mem_v.transpose(0, 1, 3, 4, 2).reshape(depth, nd, X_WIDTH, N_MEM)
    state_t = state_hgrn.transpose(0, 2, 3, 4, 1).reshape(depth, HG_HEADS * HEAD_DIM * HEAD_DIM, nd)
    dec_tab = _decode_table(rel_bias, attn_sink)
    xq_mask = _head_rows_mask()[:X_QROWS]
    head_group = jnp.arange(A_HEADS) // A_REP

    p_wk, p_wv, p_s, p_cb, p_h, p_mk, p_mv = [], [], [], [], [], [], []
    s_cb, s_h = [], []
    for l in range(depth):
        row = lambda v: v[l].reshape(1, -1)
        wa_bd = _block_diag(lru_wa[l]).astype(BF16)
        wx_bd = _block_diag(lru_wx[l]).astype(BF16)
        lru_args = (conv_w[l], row(conv_b), wa_bd, row(lru_ba), wx_bd, row(lru_bx), row(lru_lam))
        wo_parts = [w_out_b[l, :a_w], w_out_b[l, a_w:a_w + HG_WIDTH], w_out_b[l, a_w + HG_WIDTH:]]

        proj = matmul(xp, w_in_b[l], 512, 768)
        oa = attn_prompt(proj, attn_sink[l], bias_p, bp, seq)
        ob, st = hgrn_prompt(proj, row(loglb), row(log1mlb), row(gain4), bdm256, hmask, bp, seq, 256)
        oc, hl = lru_prompt(proj, *lru_args, bp, seq, 256)
        xp = proj_res_ln(xp, [oa, ob, oc], wo_parts, row(ln1_g), row(ln1_b), 512)
        mkv = matmul(mem, wkv_b[l], 256, 512)
        xg = xattn_prompt(xp, wq_b[l], mkv, wo_b[l], row(ln2_g), row(ln2_b), wr[l], br[l:l + 1],
                          bp, seq, 512)
        xp = moe_routed_sc(xg, wg_b, wu_b, wd_b, row(ln3_g), row(ln3_b), l, n_tok, MOE_TM)

        proj3 = proj.reshape(bp, seq, IN_COLS)
        p_wk.append(proj3[:, seq - WINDOW:, a_w:a_w + LANES].reshape(bp, WINDOW, A_KV_HEADS, HEAD_DIM))
        p_wv.append(proj3[:, seq - WINDOW:, a_w + LANES:a_w + 2 * LANES].reshape(bp, WINDOW, A_KV_HEADS, HEAD_DIM))
        st5 = st.reshape(bp, HG_HEADS, HEAD_DIM, HG_HEADS, HEAD_DIM)
        p_s.append(jnp.stack([st5[:, h, :, h, :] for h in range(HG_HEADS)], 1).transpose(0, 1, 3, 2))
        p_cb.append(proj3[:, seq - (CONV_W - 1):, IN_COLS - 2 * LRU_WIDTH:IN_COLS - LRU_WIDTH])
        p_h.append(hl.reshape(bp, LRU_WIDTH))
        p_mk.append(mkv[:, :X_WIDTH].reshape(bp, N_MEM, X_HEADS, HEAD_DIM))
        p_mv.append(mkv[:, X_WIDTH:].reshape(bp, N_MEM, X_HEADS, HEAD_DIM))

        projd = matmul(xs, w_in_b[l], nd, 768)
        q3 = projd[:, :a_w].reshape(nd, A_HEADS, 1, HEAD_DIM)
        on_group = head_group[None, :, None, None] == jnp.arange(A_KV_HEADS)[None, None, :, None]
        qblk = jnp.where(on_group, q3, 0.0).reshape(nd, A_HEADS, LANES)
        o3, ckt, cvt = attn_decode(qblk, projd, dec_tab[l], ckt, cvt, l, 16)
        o4 = o3.reshape(nd, A_HEADS, A_KV_HEADS, HEAD_DIM)
        oa = jnp.sum(jnp.where(on_group, o4, 0.0), axis=2).reshape(nd, a_w)
        gates_t = projd[:, a_w + 2 * LANES:a_w + 2 * LANES + 4 * HG_WIDTH].T
        bc = lambda v: jnp.broadcast_to(v[:, None], (v.shape[0], nd))
        ob_t, state_t = hgrn_decode(gates_t, bc(loglb[l]), bc(log1mlb[l]), bc(hg_gain[l]), state_t, l)
        oc, nh, nbuf = lru_decode(projd, state_conv[l].reshape(nd, -1), state_lru[l], *lru_args)
        xs = proj_res_ln(xs, [oa, ob_t.T, oc], wo_parts, row(ln1_g), row(ln1_b), nd)
        qd = matmul(xs, wq_b[l], nd, X_WIDTH)
        qdb = qd[:, None, :] * xq_mask[None, :, :]
        od = xattn_decode(qdb, cmkt, cmvt, xq_mask, l, 8)
        xs = proj_res_ln(xs, [od], [wo_b[l]], row(ln2_g), row(ln2_b), nd)
        xs = moe_dense(xs, wr[l], br[l:l + 1], wg_b, wu_b, wd_b, row(ln3_g), row(ln3_b), l, nd)

        s_cb.append(nbuf.reshape(nd, CONV_W - 1, LRU_WIDTH))
        s_h.append(nh)

    xp = jnp.transpose(xp, (1, 0, 2))
    unkey = lambda c: c.reshape(depth, nd, A_KV_HEADS, HEAD_DIM, WINDOW).transpose(0, 1, 4, 2, 3)
    s_s = state_t.reshape(depth, HG_HEADS, HEAD_DIM, HEAD_DIM, nd).transpose(0, 4, 1, 2, 3)
    return (xp.reshape(bp, seq, d), xs.reshape(nd, 1, d),
            jnp.stack(p_wk), jnp.stack(p_wv), jnp.stack(p_s), jnp.stack(p_cb), jnp.stack(p_h),
            jnp.stack(p_mk), jnp.stack(p_mv),
            unkey(ckt), unkey(cvt), s_s, jnp.stack(s_cb), jnp.stack(s_h))
```

```python
import functools
import math

import jax
import jax.numpy as jnp
from jax import lax
from jax.experimental import pallas as pl
from jax.experimental.pallas import tpu as pltpu
from jax.experimental.pallas import tpu_sc as plsc

F32 = jnp.float32
BF16 = jnp.bfloat16

D_MODEL = 1024
DEPTH = 4
HEAD_DIM = 64
A_HEADS = 8
A_KV_HEADS = 2
A_REP = A_HEADS // A_KV_HEADS
WINDOW = 128
A_QB = 4
N_BUCKETS = 32
MAX_DISTANCE = 128
HG_WIDTH = 256
HG_HEADS = 4
HG_CHUNK = 64
HG_TB = 16
HG_MROWS = 16
LRU_WIDTH = 256
LRU_BLOCKS = 4
CONV_W = 4
LRU_C = 8.0
N_MEM = 256
X_HEADS = 4
X_WIDTH = X_HEADS * HEAD_DIM
X_QROWS = 8
N_GROUPS = 4
EXP_PER_GROUP = 4
N_EXPERTS = N_GROUPS * EXP_PER_GROUP
EXP_FF = D_MODEL // 4
ALPHA = (2 * DEPTH) ** 0.25
LN_EPS = 1e-5
RMS_EPS = 1e-6
IN_COLS = 2304
SCALE = HEAD_DIM ** -0.5
NEG = -1e30
LANES = 128
ROUTER_LANES = 128
XG_WIDTH = D_MODEL + ROUTER_LANES
XG_PLANES = XG_WIDTH // LANES
X_PLANES = D_MODEL // LANES
MOE_TM = 512
SC_WINDOW = 128
VMEM_LIMIT = 48 * 1024 * 1024


def _cparams(sem):
    return pltpu.CompilerParams(dimension_semantics=sem, vmem_limit_bytes=VMEM_LIMIT)


def _bdot(a, b):
    return jnp.dot(a.astype(BF16), b.astype(BF16), preferred_element_type=F32)


def _bdot_nt(a, b):
    return lax.dot_general(a.astype(BF16), b.astype(BF16), (((1,), (1,)), ((), ())),
                           preferred_element_type=F32)


def _bdot_tn(a, b):
    return lax.dot_general(a.astype(BF16), b.astype(BF16), (((0,), (0,)), ((), ())),
                           preferred_element_type=F32)


def _rb(x):
    return x.astype(BF16).astype(F32)


def _silu(x):
    return x * jax.nn.sigmoid(x)


def _neg_expm1(x):
    return -jnp.tanh(0.5 * x) * (jnp.exp(x) + 1.0)


def _softplus(x):
    return jnp.maximum(x, 0.0) + jnp.log1p(jnp.exp(-jnp.abs(x)))


def _gelu_tanh(x):
    return 0.5 * x * (1.0 + jnp.tanh(math.sqrt(2.0 / math.pi) * (x + 0.044715 * (x * x * x))))


def _layer_norm(y, g, b):
    mu = jnp.mean(y, -1, keepdims=True)
    yc = y - mu
    var = jnp.mean(yc * yc, -1, keepdims=True)
    return yc * lax.rsqrt(var + LN_EPS) * g + b


def _rows(x_ref):
    if len(x_ref.shape) == 2:
        return x_ref[...]
    return jnp.concatenate([x_ref[j] for j in range(x_ref.shape[0])], axis=1)


def _rows_spec(x, tm, nargs):
    if x.ndim == 2:
        return pl.BlockSpec((tm, x.shape[1]), (lambda i: (i, 0)) if nargs == 1 else (lambda i, j: (i, 0)))
    blk = (x.shape[0], tm, LANES)
    return pl.BlockSpec(blk, (lambda i: (0, i, 0)) if nargs == 1 else (lambda i, j: (0, i, 0)))


def _mm_kernel(x_ref, w_ref, o_ref):
    o_ref[...] = _bdot(_rows(x_ref), w_ref[...])


def matmul(x, w, tm, tn):
    m = x.shape[-2]
    k, n = w.shape
    return pl.pallas_call(
        _mm_kernel,
        grid=(m // tm, n // tn),
        in_specs=[_rows_spec(x, tm, 2),
                  pl.BlockSpec((k, tn), lambda i, j: (0, j))],
        out_specs=pl.BlockSpec((tm, tn), lambda i, j: (i, j)),
        out_shape=jax.ShapeDtypeStruct((m, n), F32),
        compiler_params=_cparams(("parallel", "parallel")),
        name="matmul",
    )(x, w)


def _proj_res_ln_kernel(n_in, x_ref, *refs):
    a_refs = refs[:n_in]
    w_refs = refs[n_in:2 * n_in]
    g_ref, b_ref, o_ref = refs[2 * n_in:]
    y = ALPHA * _rows(x_ref)
    for a_ref, w_ref in zip(a_refs, w_refs):
        y = y + _bdot(a_ref[...], w_ref[...])
    o_ref[...] = _layer_norm(y, g_ref[...], b_ref[...])


def proj_res_ln(x, a_list, w_list, g, b, tm):
    m = x.shape[-2]
    d = w_list[0].shape[1]
    n_in = len(a_list)
    in_specs = [_rows_spec(x, tm, 1)]
    in_specs += [pl.BlockSpec((tm, a.shape[1]), lambda i: (i, 0)) for a in a_list]
    in_specs += [pl.BlockSpec(w.shape, lambda i: (0, 0)) for w in w_list]
    in_specs += [pl.BlockSpec((1, d), lambda i: (0, 0))] * 2
    return pl.pallas_call(
        functools.partial(_proj_res_ln_kernel, n_in),
        grid=(m // tm,),
        in_specs=in_specs,
        out_specs=pl.BlockSpec((tm, d), lambda i: (i, 0)),
        out_shape=jax.ShapeDtypeStruct((m, d), F32),
        compiler_params=_cparams(("parallel",)),
        name="proj_res_ln",
    )(x, *a_list, *w_list, g, b)


def _attn_prompt_kernel(sink_ref, q_ref, kc_ref, kp_ref, vc_ref, vp_ref, bias_ref, o_ref):
    n = pl.program_id(1)
    col = lax.broadcasted_iota(jnp.int32, (WINDOW, 2 * WINDOW), 1)
    first = jnp.where((n == 0) & (col < WINDOW), NEG, 0.0)
    kk = jnp.concatenate([kp_ref[...], kc_ref[...]], axis=0).astype(BF16)
    vv = jnp.concatenate([vp_ref[...], vc_ref[...]], axis=0).astype(BF16)
    q = q_ref[...].astype(BF16)
    for u in range(A_QB):
        rows = slice(u * WINDOW, (u + 1) * WINDOW)
        keys = slice(u * WINDOW, (u + 2) * WINDOW)
        outs = []
        for h in range(A_HEADS):
            g = h // A_REP
            qh = q[rows, h * HEAD_DIM:(h + 1) * HEAD_DIM]
            kg = kk[keys, g * HEAD_DIM:(g + 1) * HEAD_DIM]
            vg = vv[keys, g * HEAD_DIM:(g + 1) * HEAD_DIM]
            s = _bdot_nt(qh, kg) * SCALE + bias_ref[h]
            if u == 0:
                s = s + first
            sink = sink_ref[h]
            m = jnp.maximum(jnp.max(s, -1, keepdims=True), sink)
            p = jnp.exp(s - m)
            den = jnp.sum(p, -1, keepdims=True) + jnp.exp(sink - m)
            outs.append(_bdot(p / den, vg))
        o_ref[rows, :] = jnp.concatenate(outs, axis=1)


def attn_prompt(proj, sink, bias, batch, seq):
    nb = seq // WINDOW
    ns = nb // A_QB
    tq = A_QB * WINDOW
    qcol = 0
    kcol = (A_HEADS * HEAD_DIM) // LANES
    vcol = kcol + 1

    def cur(c):
        return lambda b, n: (b * ns + n, c)

    def prev(c):
        return lambda b, n: (b * nb + jnp.maximum(n * A_QB - 1, 0), c)

    return pl.pallas_call(
        _attn_prompt_kernel,
        grid=(batch, ns),
        in_specs=[pl.BlockSpec(memory_space=pltpu.SMEM),
                  pl.BlockSpec((tq, A_HEADS * HEAD_DIM), cur(qcol)),
                  pl.BlockSpec((tq, LANES), cur(kcol)),
                  pl.BlockSpec((WINDOW, LANES), prev(kcol)),
                  pl.BlockSpec((tq, LANES), cur(vcol)),
                  pl.BlockSpec((WINDOW, LANES), prev(vcol)),
                  pl.BlockSpec((A_HEADS, WINDOW, 2 * WINDOW), lambda b, n: (0, 0, 0))],
        out_specs=pl.BlockSpec((tq, A_HEADS * HEAD_DIM), cur(0)),
        out_shape=jax.ShapeDtypeStruct((batch * seq, A_HEADS * HEAD_DIM), F32),
        compiler_params=_cparams(("parallel", "parallel")),
        name="attn_prompt",
    )(sink, proj, proj, proj, proj, proj, bias)


def _attn_decode_kernel(qb_ref, kn_ref, vn_ref, knt_ref, vnt_ref, ck_ref, cv_ref, tab_ref,
                        o_ref, ok_ref, ov_ref):
    bb = qb_ref.shape[0]
    ck = ck_ref[...]
    cv = cv_ref[...]
    qb = qb_ref[...]
    kn = kn_ref[...]
    vn = vn_ref[...]
    bias_j = tab_ref[0]
    bias_n = tab_ref[1][:, 0:1]
    sink = tab_ref[2][:, 0:1]
    s = lax.dot_general(qb.astype(BF16), ck.astype(BF16), (((2,), (1,)), ((0,), (0,))),
                        preferred_element_type=F32) * SCALE + bias_j[None]
    sn = jnp.sum(_rb(qb) * _rb(kn)[:, None, :], -1, keepdims=True) * SCALE + bias_n[None]
    m = jnp.maximum(jnp.maximum(jnp.max(s, -1, keepdims=True), sn), sink[None])
    p = jnp.exp(s - m)
    pn = jnp.exp(sn - m)
    den = jnp.sum(p, -1, keepdims=True) + pn + jnp.exp(sink[None] - m)
    o = lax.dot_general((p / den).astype(BF16), cv.astype(BF16), (((2,), (2,)), ((0,), (0,))),
                        preferred_element_type=F32)
    o_ref[...] = o + _rb(pn / den) * _rb(vn)[:, None, :]
    lane = lax.broadcasted_iota(jnp.int32, (LANES, LANES), 1)
    for b in range(bb):
        ok_ref[b] = jnp.where(lane == WINDOW - 1, knt_ref[:, b:b + 1], pltpu.roll(ck[b], WINDOW - 1, 1))
        ov_ref[b] = jnp.where(lane == WINDOW - 1, vnt_ref[:, b:b + 1], pltpu.roll(cv[b], WINDOW - 1, 1))


def attn_decode(qblk, proj_d, table, cache_k, cache_v, layer, bb):
    nbatch = proj_d.shape[0]
    a_w = A_HEADS * HEAD_DIM
    cols = lambda c: proj_d[:, c:c + LANES].reshape(nbatch // bb, bb, LANES).transpose(0, 2, 1)
    knt, vnt = cols(a_w), cols(a_w + LANES)
    kcol = (A_HEADS * HEAD_DIM) // LANES
    cache_spec = pl.BlockSpec((None, bb, LANES, WINDOW), lambda i: (layer, i, 0, 0))
    col_spec = pl.BlockSpec((None, LANES, bb), lambda i: (i, 0, 0))
    return pl.pallas_call(
        _attn_decode_kernel,
        grid=(nbatch // bb,),
        in_specs=[pl.BlockSpec((bb, A_HEADS, LANES), lambda i: (i, 0, 0)),
                  pl.BlockSpec((bb, LANES), lambda i: (i, kcol)),
                  pl.BlockSpec((bb, LANES), lambda i: (i, kcol + 1)),
                  col_spec, col_spec, cache_spec, cache_spec,
                  pl.BlockSpec((3, A_HEADS, WINDOW), lambda i: (0, 0, 0))],
        out_specs=[pl.BlockSpec((bb, A_HEADS, LANES), lambda i: (i, 0, 0)), cache_spec, cache_spec],
        out_shape=[jax.ShapeDtypeStruct((nbatch, A_HEADS, LANES), F32),
                   jax.ShapeDtypeStruct(cache_k.shape, F32),
                   jax.ShapeDtypeStruct(cache_v.shape, F32)],
        input_output_aliases={5: 1, 6: 2},
        compiler_params=_cparams(("arbitrary",)),
        name="attn_decode",
    )(qblk, proj_d, proj_d, knt, vnt, cache_k, cache_v, table)


def _xattn_decode_kernel(qb_ref, mk_ref, mv_ref, hm_ref, o_ref):
    qb = qb_ref[...]
    s = lax.dot_general(qb.astype(BF16), mk_ref[...].astype(BF16), (((2,), (1,)), ((0,), (0,))),
                        preferred_element_type=F32) * SCALE
    m = jnp.max(s, -1, keepdims=True)
    p = jnp.exp(s - m)
    p = p / jnp.sum(p, -1, keepdims=True)
    o = lax.dot_general(p.astype(BF16), mv_ref[...].astype(BF16), (((2,), (2,)), ((0,), (0,))),
                        preferred_element_type=F32)
    o_ref[...] = jnp.sum(o * hm_ref[...][None], axis=1)


def xattn_decode(qblk, mem_k, mem_v, hmask, layer, bb):
    nbatch = qblk.shape[0]
    mem_spec = pl.BlockSpec((None, bb, X_WIDTH, N_MEM), lambda i: (layer, i, 0, 0))
    return pl.pallas_call(
        _xattn_decode_kernel,
        grid=(nbatch // bb,),
        in_specs=[pl.BlockSpec((bb, X_QROWS, X_WIDTH), lambda i: (i, 0, 0)), mem_spec, mem_spec,
                  pl.BlockSpec((X_QROWS, X_WIDTH), lambda i: (0, 0))],
        out_specs=pl.BlockSpec((bb, X_WIDTH), lambda i: (i, 0)),
        out_shape=jax.ShapeDtypeStruct((nbatch, X_WIDTH), F32),
        compiler_params=_cparams(("parallel",)),
        name="xattn_decode",
    )(qblk, mem_k, mem_v, hmask)


def _hgrn_gates(hq, hf, loglb, log1mlb):
    ls = jnp.minimum(hf, 0.0) - jnp.log1p(jnp.exp(-jnp.abs(hf)))
    b = log1mlb + ls
    lf = jnp.maximum(loglb, b) + jnp.log1p(jnp.exp(-jnp.abs(loglb - b)))
    return _silu(hq), lf, _neg_expm1(lf)


def _hgrn_prompt_kernel(hq_ref, hf_ref, hi_ref, hg_ref, loglb_ref, log1mlb_ref, gain_ref,
                        bdm_ref, hm_ref, ob_ref, st_ref, st_scr, q_scr, k_scr, cum_scr, o_scr):
    i = pl.program_id(1)
    tt = hq_ref.shape[0]
    c = HG_CHUNK
    tb = HG_TB

    @pl.when(i == 0)
    def _():
        st_scr[...] = jnp.zeros_like(st_scr)

    qs, lf, kk = _hgrn_gates(hq_ref[...], hf_ref[...], loglb_ref[...], log1mlb_ref[...])
    row = lax.broadcasted_iota(jnp.int32, (tt, HG_WIDTH), 0) & (c - 1)
    cum = lf
    sh = 1
    while sh < c:
        cum = cum + jnp.where(row >= sh, pltpu.roll(cum, sh, 0), 0.0)
        sh *= 2
    q_scr[...] = qs
    k_scr[...] = kk
    cum_scr[...] = cum

    bdm = bdm_ref[...]
    hmask = hm_ref[...]
    def chunk(ci, carry):
        r0 = pl.multiple_of(ci * c, c)
        r = pl.ds(r0, c)
        cu = cum_scr[r, :]
        q = q_scr[r, :]
        k = k_scr[r, :]
        v = hi_ref[r, :]
        vb = v.astype(BF16)
        qb = _rb(q)
        last = cu[c - 1:c, :]
        st = st_scr[...]
        o_inter = _bdot_nt(q * jnp.exp(cu), st)
        for j in range(c // tb):
            ns = tb * (j + 1)
            ti = lax.broadcasted_iota(jnp.int32, (tb, ns, HG_WIDTH), 0) + tb * j
            si = lax.broadcasted_iota(jnp.int32, (tb, ns, HG_WIDTH), 1)
            cut = cu[tb * j:tb * (j + 1), :]
            dec = jnp.exp(jnp.where(ti >= si, cut[:, None, :] - cu[None, :ns, :], NEG))
            a2 = (dec * k[None, :ns, :]).astype(BF16)
            q4 = (qb[tb * j:tb * (j + 1), None, :] * hmask[None, :, :]).astype(BF16)
            att = lax.dot_general(q4, a2, (((2,), (2,)), ((0,), (0,))),
                                  preferred_element_type=F32)
            w = jnp.dot(att.reshape(tb * HG_MROWS, ns).astype(BF16), vb[:ns, :],
                        preferred_element_type=F32).reshape(tb, HG_MROWS, HG_WIDTH)
            o_intra = jnp.sum(w * hmask[None, :, :], axis=1)
            o_scr[pl.ds(r0 + tb * j, tb), :] = o_intra + o_inter[tb * j:tb * (j + 1), :]
        upd = _bdot_tn(v, k * jnp.exp(last - cu))
        st_scr[...] = st * jnp.exp(last) + upd * bdm
        return carry

    lax.fori_loop(0, tt // c, chunk, 0)

    o = o_scr[...]
    ms = jnp.dot(o * o, bdm, precision=lax.Precision.HIGHEST,
                 preferred_element_type=F32) * (1.0 / HEAD_DIM)
    ob_ref[...] = o * lax.rsqrt(ms + RMS_EPS) * gain_ref[...] * _silu(hg_ref[...])

    @pl.when(i == pl.num_programs(1) - 1)
    def _():
        st_ref[...] = st_scr[...]


def hgrn_prompt(proj, loglb, log1mlb, gain4, bdm, hmask, batch, seq, tt):
    nt = seq // tt
    base = (A_HEADS + 2 * A_KV_HEADS) * HEAD_DIM // HG_WIDTH

    def col(cblk):
        return pl.BlockSpec((tt, HG_WIDTH), lambda b, i: (b * nt + i, cblk))

    row_spec = pl.BlockSpec((1, HG_WIDTH), lambda b, i: (0, 0))
    mat_spec = pl.BlockSpec((HG_WIDTH, HG_WIDTH), lambda b, i: (0, 0))
    return pl.pallas_call(
        _hgrn_prompt_kernel,
        grid=(batch, nt),
        in_specs=[col(base), col(base + 1), col(base + 2), col(base + 3),
                  row_spec, row_spec, row_spec, mat_spec,
                  pl.BlockSpec((HG_MROWS, HG_WIDTH), lambda b, i: (0, 0))],
        out_specs=[pl.BlockSpec((tt, HG_WIDTH), lambda b, i: (b * nt + i, 0)),
                   pl.BlockSpec((None, HG_WIDTH, HG_WIDTH), lambda b, i: (b, 0, 0))],
        out_shape=[jax.ShapeDtypeStruct((batch * seq, HG_WIDTH), F32),
                   jax.ShapeDtypeStruct((batch, HG_WIDTH, HG_WIDTH), F32)],
        scratch_shapes=[pltpu.VMEM((HG_WIDTH, HG_WIDTH), F32),
                        pltpu.VMEM((tt, HG_WIDTH), F32),
                        pltpu.VMEM((tt, HG_WIDTH), F32),
                        pltpu.VMEM((tt, HG_WIDTH), F32),
                        pltpu.VMEM((tt, HG_WIDTH), F32)],
        compiler_params=_cparams(("parallel", "arbitrary")),
        name="hgrn_prompt",
    )(proj, proj, proj, proj, loglb, log1mlb, gain4, bdm, hmask)


def _hgrn_decode_kernel(hq_ref, hf_ref, hi_ref, hg_ref, loglb_ref, log1mlb_ref, gain_ref, s_ref,
                        ob_ref, so_ref):
    nb = hq_ref.shape[1]
    qs, lf, kk = _hgrn_gates(hq_ref[...], hf_ref[...], loglb_ref[...], log1mlb_ref[...])
    v = hi_ref[...]
    f = jnp.exp(lf)
    s = s_ref[...].reshape(HEAD_DIM, HEAD_DIM, nb)
    att = jnp.sum(_rb(qs) * _rb(kk), axis=0, keepdims=True)
    o = _rb(att) * _rb(v) + jnp.sum(_rb(qs * f)[:, None, :] * _rb(s), axis=0)
    s_new = f[:, None, :] * s + _rb(kk)[:, None, :] * _rb(v)[None, :, :]
    so_ref[...] = s_new.reshape(HEAD_DIM * HEAD_DIM, nb)
    ms = jnp.mean(o * o, axis=0, keepdims=True)
    ob_ref[...] = o * lax.rsqrt(ms + RMS_EPS) * gain_ref[...] * _silu(hg_ref[...])


def hgrn_decode(gates_t, loglb_t, log1mlb_t, gain_t, state_t, layer):
    nb = gates_t.shape[1]

    def blk(off):
        return pl.BlockSpec((HEAD_DIM, nb), lambda h: (off * HG_HEADS + h, 0))

    par = pl.BlockSpec((HEAD_DIM, nb), lambda h: (h, 0))
    st = pl.BlockSpec((None, HEAD_DIM * HEAD_DIM, nb), lambda h: (layer, h, 0))
    return pl.pallas_call(
        _hgrn_decode_kernel,
        grid=(HG_HEADS,),
        in_specs=[blk(0), blk(1), blk(2), blk(3), par, par,
                  pl.BlockSpec((HEAD_DIM, nb), lambda h: (0, 0)), st],
        out_specs=[par, st],
        out_shape=[jax.ShapeDtypeStruct((HG_WIDTH, nb), F32),
                   jax.ShapeDtypeStruct(state_t.shape, F32)],
        input_output_aliases={7: 1},
        compiler_params=_cparams(("arbitrary",)),
        name="hgrn_decode",
    )(gates_t, gates_t, gates_t, gates_t, loglb_t, log1mlb_t, gain_t, state_t)


def _lru_gates(xc, wa_ref, ba_ref, wx_ref, bx_ref, lam_ref):
    r = jax.nn.sigmoid(_bdot(xc, wa_ref[...]) + ba_ref[...])
    gi = jax.nn.sigmoid(_bdot(xc, wx_ref[...]) + bx_ref[...])
    log_a = -LRU_C * r * _softplus(-lam_ref[...])
    a = jnp.exp(log_a)
    bterm = jnp.sqrt(_neg_expm1(2.0 * log_a)) * (gi * xc)
    return a, bterm


def _lru_prompt_kernel(lx_ref, lg_ref, cw_ref, cb_ref, wa_ref, ba_ref, wx_ref, bx_ref, lam_ref,
                       oc_ref, hl_ref, ext_scr, h_scr):
    i = pl.program_id(1)
    tt = lx_ref.shape[0]
    pad = 8

    @pl.when(i == 0)
    def _():
        ext_scr[0:pad, :] = jnp.zeros((pad, LRU_WIDTH), F32)
        h_scr[...] = jnp.zeros_like(h_scr)

    x = lx_ref[...]
    ext_scr[pad:pad + tt, :] = x
    xc = cb_ref[...] + cw_ref[CONV_W - 1:CONV_W, :] * x
    for j in range(CONV_W - 1):
        back = CONV_W - 1 - j
        xc = xc + cw_ref[j:j + 1, :] * ext_scr[pad - back:pad - back + tt, :]
    ext_scr[0:pad, :] = x[tt - pad:tt, :]

    a, bterm = _lru_gates(xc, wa_ref, ba_ref, wx_ref, bx_ref, lam_ref)
    row = lax.broadcasted_iota(jnp.int32, (tt, LRU_WIDTH), 0)
    sh = 1
    while sh < tt:
        keep = row >= sh
        b_s = jnp.where(keep, pltpu.roll(bterm, sh, 0), 0.0)
        a_s = jnp.where(keep, pltpu.roll(a, sh, 0), 1.0)
        bterm = a * b_s + bterm
        a = a * a_s
        sh *= 2
    h = a * h_scr[...] + bterm
    h_scr[...] = h[tt - 1:tt, :]
    oc_ref[...] = h * _gelu_tanh(lg_ref[...])

    @pl.when(i == pl.num_programs(1) - 1)
    def _():
        hl_ref[...] = h[tt - 1:tt, :]


def lru_prompt(proj, conv_w, conv_b, wa_bd, ba, wx_bd, bx, lam, batch, seq, tt):
    nt = seq // tt
    base = IN_COLS // LRU_WIDTH - 2

    def col(cblk):
        return pl.BlockSpec((tt, LRU_WIDTH), lambda b, i: (b * nt + i, cblk))

    row_spec = pl.BlockSpec((1, LRU_WIDTH), lambda b, i: (0, 0))
    mat_spec = pl.BlockSpec((LRU_WIDTH, LRU_WIDTH), lambda b, i: (0, 0))
    return pl.pallas_call(
        _lru_prompt_kernel,
        grid=(batch, nt),
        in_specs=[col(base), col(base + 1),
                  pl.BlockSpec((CONV_W, LRU_WIDTH), lambda b, i: (0, 0)), row_spec,
                  mat_spec, row_spec, mat_spec, row_spec, row_spec],
        out_specs=[pl.BlockSpec((tt, LRU_WIDTH), lambda b, i: (b * nt + i, 0)),
                   pl.BlockSpec((None, 1, LRU_WIDTH), lambda b, i: (b, 0, 0))],
        out_shape=[jax.ShapeDtypeStruct((batch * seq, LRU_WIDTH), F32),
                   jax.ShapeDtypeStruct((batch, 1, LRU_WIDTH), F32)],
        scratch_shapes=[pltpu.VMEM((tt + 8, LRU_WIDTH), F32),
                        pltpu.VMEM((1, LRU_WIDTH), F32)],
        compiler_params=_cparams(("parallel", "arbitrary")),
        name="lru_prompt",
    )(proj, proj, conv_w, conv_b, wa_bd, ba, wx_bd, bx, lam)


def _lru_decode_kernel(lx_ref, lg_ref, buf_ref, h0_ref, cw_ref, cb_ref, wa_ref, ba_ref, wx_ref,
                       bx_ref, lam_ref, oc_ref, hn_ref, nbuf_ref):
    x = lx_ref[...]
    buf = buf_ref[...]
    xc = cb_ref[...] + cw_ref[CONV_W - 1:CONV_W, :] * x
    for j in range(CONV_W - 1):
        xc = xc + cw_ref[j:j + 1, :] * buf[:, j * LRU_WIDTH:(j + 1) * LRU_WIDTH]
    a, bterm = _lru_gates(xc, wa_ref, ba_ref, wx_ref, bx_ref, lam_ref)
    h = a * h0_ref[...] + bterm
    hn_ref[...] = h
    oc_ref[...] = h * _gelu_tanh(lg_ref[...])
    nbuf_ref[...] = jnp.concatenate([buf[:, LRU_WIDTH:], x], axis=1)


def lru_decode(proj_d, conv_buf, h0, conv_w, conv_b, wa_bd, ba, wx_bd, bx, lam):
    nb = proj_d.shape[0]
    base = IN_COLS // LRU_WIDTH - 2
    row_spec = pl.BlockSpec((1, LRU_WIDTH), lambda i: (0, 0))
    mat_spec = pl.BlockSpec((LRU_WIDTH, LRU_WIDTH), lambda i: (0, 0))
    act = pl.BlockSpec((nb, LRU_WIDTH), lambda i: (0, 0))
    bufs = pl.BlockSpec((nb, (CONV_W - 1) * LRU_WIDTH), lambda i: (0, 0))
    return pl.pallas_call(
        _lru_decode_kernel,
        grid=(1,),
        in_specs=[pl.BlockSpec((nb, LRU_WIDTH), lambda i: (0, base)),
                  pl.BlockSpec((nb, LRU_WIDTH), lambda i: (0, base + 1)),
                  bufs, act, pl.BlockSpec((CONV_W, LRU_WIDTH), lambda i: (0, 0)), row_spec,
                  mat_spec, row_spec, mat_spec, row_spec, row_spec],
        out_specs=[act, act, bufs],
        out_shape=[jax.ShapeDtypeStruct((nb, LRU_WIDTH), F32),
                   jax.ShapeDtypeStruct((nb, LRU_WIDTH), F32),
                   jax.ShapeDtypeStruct((nb, (CONV_W - 1) * LRU_WIDTH), F32)],
        compiler_params=_cparams(("arbitrary",)),
        name="lru_decode",
    )(proj_d, proj_d, conv_buf, h0, conv_w, conv_b, wa_bd, ba, wx_bd, bx, lam)


def _xattn_prompt_kernel(x_ref, wq_ref, mk_ref, mv_ref, wo_ref, g_ref, b_ref, wr_ref, br_ref, o_ref):
    x = x_ref[...]
    q = _bdot(x, wq_ref[...]).astype(BF16)
    mk = mk_ref[...].astype(BF16)
    mv = mv_ref[...].astype(BF16)
    outs = []
    for h in range(X_HEADS):
        sl = slice(h * HEAD_DIM, (h + 1) * HEAD_DIM)
        s = _bdot_nt(q[:, sl], mk[:, sl]) * SCALE
        m = jnp.max(s, -1, keepdims=True)
        p = jnp.exp(s - m)
        p = p / jnp.sum(p, -1, keepdims=True)
        outs.append(_bdot(p, mv[:, sl]))
    o = jnp.concatenate(outs, axis=1)
    y = _layer_norm(ALPHA * x + _bdot(o, wo_ref[...]), g_ref[...], b_ref[...])
    logits = _bdot(y, wr_ref[...]) + br_ref[...]
    gate, g_idx = _route(logits)
    lane = lax.broadcasted_iota(jnp.int32, gate.shape, 1)
    for j in range(D_MODEL // LANES):
        o_ref[j] = y[:, j * LANES:(j + 1) * LANES]
    o_ref[D_MODEL // LANES] = jnp.where(lane == 0, g_idx.astype(F32), gate)


def xattn_prompt(x, wq, mem_kv, wo, g, b, wr, br, batch, seq, tt):
    nt = seq // tt
    const = lambda bi, i: (0, 0)
    return pl.pallas_call(
        _xattn_prompt_kernel,
        grid=(batch, nt),
        in_specs=[pl.BlockSpec((tt, D_MODEL), lambda bi, i: (bi * nt + i, 0)),
                  pl.BlockSpec((D_MODEL, X_WIDTH), const),
                  pl.BlockSpec((N_MEM, X_WIDTH), lambda bi, i: (bi, 0)),
                  pl.BlockSpec((N_MEM, X_WIDTH), lambda bi, i: (bi, 1)),
                  pl.BlockSpec((X_WIDTH, D_MODEL), const),
                  pl.BlockSpec((1, D_MODEL), const),
                  pl.BlockSpec((1, D_MODEL), const),
                  pl.BlockSpec((D_MODEL, ROUTER_LANES), const),
                  pl.BlockSpec((1, ROUTER_LANES), const)],
        out_specs=pl.BlockSpec((XG_PLANES, tt, LANES), lambda bi, i: (0, bi * nt + i, 0)),
        out_shape=jax.ShapeDtypeStruct((XG_PLANES, batch * seq, LANES), F32),
        compiler_params=_cparams(("parallel", "parallel")),
        name="xattn_prompt",
    )(x, wq, mem_kv, mem_kv, wo, g, b, wr, br)


def _route(logits):
    lane = lax.broadcasted_iota(jnp.int32, logits.shape, 1)
    big = jnp.int32(ROUTER_LANES)
    ninf = -jnp.inf
    gl = jnp.where(lane < N_GROUPS, logits, ninf)
    gm = jnp.max(gl, -1, keepdims=True)
    g_val = 1.0 / jnp.sum(jnp.exp(gl - gm), -1, keepdims=True)
    g_idx = jnp.min(jnp.where(gl == gm, lane, big), -1, keepdims=True)
    lo = N_GROUPS + EXP_PER_GROUP * g_idx
    el = jnp.where((lane >= lo) & (lane < lo + EXP_PER_GROUP), logits, ninf)
    v1 = jnp.max(el, -1, keepdims=True)
    i1 = jnp.min(jnp.where(el == v1, lane, big), -1, keepdims=True)
    el2 = jnp.where(lane == i1, ninf, el)
    v2 = jnp.max(el2, -1, keepdims=True)
    i2 = jnp.min(jnp.where(el2 == v2, lane, big), -1, keepdims=True)
    e2 = jnp.exp(v2 - v1)
    w1 = g_val / (1.0 + e2)
    w2 = g_val * e2 / (1.0 + e2)
    return jnp.where(lane == i1, w1, 0.0) + jnp.where(lane == i2, w2, 0.0), g_idx


def _moe_dense_kernel(x_ref, wr_ref, br_ref, wg_ref, wu_ref, wd_ref, g_ref, b_ref, o_ref,
                      gate_scr, acc_scr):
    e = pl.program_id(1)

    @pl.when(e == 0)
    def _():
        logits = _bdot(x_ref[...], wr_ref[...]) + br_ref[...]
        gate_scr[...] = _route(logits)[0]
        acc_scr[...] = jnp.zeros_like(acc_scr)

    xb = x_ref[...].astype(BF16)
    lane = lax.broadcasted_iota(jnp.int32, gate_scr.shape, 1)
    gcol = jnp.sum(jnp.where(lane == e + N_GROUPS, gate_scr[...], 0.0), -1, keepdims=True)
    hid = _silu(_bdot(xb, wg_ref[...])) * _bdot(xb, wu_ref[...])
    acc_scr[...] += _bdot(hid * gcol, wd_ref[...])

    @pl.when(e == pl.num_programs(1) - 1)
    def _():
        o_ref[...] = _layer_norm(ALPHA * x_ref[...] + acc_scr[...], g_ref[...], b_ref[...])


def moe_dense(x, wr, br, wg, wu, wd, g, b, layer, tm):
    m = x.shape[0]
    return pl.pallas_call(
        _moe_dense_kernel,
        grid=(m // tm, N_EXPERTS),
        in_specs=[pl.BlockSpec((tm, D_MODEL), lambda i, e: (i, 0)),
                  pl.BlockSpec((D_MODEL, ROUTER_LANES), lambda i, e: (0, 0)),
                  pl.BlockSpec((1, ROUTER_LANES), lambda i, e: (0, 0)),
                  pl.BlockSpec((None, None, D_MODEL, EXP_FF), lambda i, e: (layer, e, 0, 0)),
                  pl.BlockSpec((None, None, D_MODEL, EXP_FF), lambda i, e: (layer, e, 0, 0)),
                  pl.BlockSpec((None, None, EXP_FF, D_MODEL), lambda i, e: (layer, e, 0, 0)),
                  pl.BlockSpec((1, D_MODEL), lambda i, e: (0, 0)),
                  pl.BlockSpec((1, D_MODEL), lambda i, e: (0, 0))],
        out_specs=pl.BlockSpec((tm, D_MODEL), lambda i, e: (i, 0)),
        out_shape=jax.ShapeDtypeStruct((m, D_MODEL), F32),
        scratch_shapes=[pltpu.VMEM((tm, ROUTER_LANES), F32), pltpu.VMEM((tm, D_MODEL), F32)],
        compiler_params=_cparams(("parallel", "arbitrary")),
        name="moe_dense",
    )(x, wr, br, wg, wu, wd, g, b)


def _moe_routed_kernel(src_ref, dst_ref, tg_ref, xg_hbm, wg_ref, wu_ref, wd_ref, g_ref, b_ref,
                       y_hbm, xbuf, ybuf, gsem, ssem):
    t = pl.program_id(0)
    nt = pl.num_programs(0)
    slot = t % 2
    tm = xbuf.shape[1]

    def gather_row(tile, sl, r):
        return pltpu.make_async_copy(xg_hbm.at[pl.ds(src_ref[tile * tm + r], 1), :],
                                     xbuf.at[sl, pl.ds(r, 1), :], gsem.at[sl])

    def scatter_row(tile, sl, r):
        return pltpu.make_async_copy(ybuf.at[sl, pl.ds(r, 1), :],
                                     y_hbm.at[pl.ds(dst_ref[tile * tm + r], 1), :], ssem.at[sl])

    def start_gather(tile, sl):
        def body(r, c):
            gather_row(tile, sl, r).start()
            return c
        lax.fori_loop(0, tm, body, 0, unroll=8)

    @pl.when(t == 0)
    def _():
        start_gather(0, 0)

    @pl.when(t + 1 < nt)
    def _():
        start_gather(t + 1, 1 - slot)

    pltpu.make_async_copy(xbuf.at[slot], xbuf.at[slot], gsem.at[slot]).wait()

    @pl.when(t >= 2)
    def _():
        pltpu.make_async_copy(ybuf.at[slot], ybuf.at[slot], ssem.at[slot]).wait()

    x = xbuf[slot, :, :D_MODEL]
    gate = xbuf[slot, :, D_MODEL:]
    xb = x.astype(BF16)
    lane = lax.broadcasted_iota(jnp.int32, gate.shape, 1)
    first = N_GROUPS + EXP_PER_GROUP * tg_ref[t]
    acc = jnp.zeros((tm, D_MODEL), F32)
    for e in range(EXP_PER_GROUP):
        gcol = jnp.sum(jnp.where(lane == first + e, gate, 0.0), -1, keepdims=True)
        hid = _silu(_bdot(xb, wg_ref[e])) * _bdot(xb, wu_ref[e])
        acc = acc + _bdot(hid * gcol, wd_ref[e])
    ybuf[slot] = _layer_norm(ALPHA * x + acc, g_ref[...], b_ref[...])

    def body(r, c):
        scatter_row(t, slot, r).start()
        return c
    lax.fori_loop(0, tm, body, 0, unroll=8)

    @pl.when(t == nt - 1)
    def _():
        pltpu.make_async_copy(ybuf.at[slot], ybuf.at[slot], ssem.at[slot]).wait()
        pltpu.make_async_copy(ybuf.at[1 - slot], ybuf.at[1 - slot], ssem.at[1 - slot]).wait()


def _moe_plan(group_idx, n, tm):
    n_tiles = n // tm + N_GROUPS
    n_slots = n_tiles * tm
    onehot = (group_idx[:, None] == jnp.arange(N_GROUPS)[None, :]).astype(jnp.int32)
    csum = jnp.cumsum(onehot, axis=0)
    counts = csum[-1]
    rank = jnp.sum(onehot * csum, axis=1) - 1
    tiles_g = (counts + tm - 1) // tm
    tile_end = jnp.cumsum(tiles_g)
    slot_base = (tile_end - tiles_g) * tm
    slot = jnp.sum(onehot * slot_base[None, :], axis=1) + rank
    tok = jnp.arange(n, dtype=jnp.int32)
    src = jnp.zeros((n_slots,), jnp.int32).at[slot].set(tok, unique_indices=True)
    valid = jnp.zeros((n_slots,), jnp.int32).at[slot].set(1, unique_indices=True)
    pad_rank = jnp.cumsum(1 - valid) - 1
    dst = jnp.where(valid == 1, src, n + pad_rank).astype(jnp.int32)
    tile_group = jnp.sum((jnp.arange(n_tiles)[:, None] >= tile_end[None, :]).astype(jnp.int32), axis=1)
    tile_group = jnp.minimum(tile_group, N_GROUPS - 1).astype(jnp.int32)
    return src, dst, tile_group


def moe_routed(xg, wg, wu, wd, g, b, layer, n, tm):
    group_idx = xg[:n, D_MODEL].astype(jnp.int32)
    src, dst, tile_group = _moe_plan(group_idx, n, tm)
    n_tiles = tile_group.shape[0]
    wspec = lambda shp: pl.BlockSpec((None, None, EXP_PER_GROUP) + shp,
                                     lambda t, s, d, tg: (layer, tg[t], 0, 0, 0))
    grid_spec = pltpu.PrefetchScalarGridSpec(
        num_scalar_prefetch=3,
        grid=(n_tiles,),
        in_specs=[pl.BlockSpec(memory_space=pl.ANY),
                  wspec((D_MODEL, EXP_FF)), wspec((D_MODEL, EXP_FF)), wspec((EXP_FF, D_MODEL)),
                  pl.BlockSpec((1, D_MODEL), lambda t, s, d, tg: (0, 0)),
                  pl.BlockSpec((1, D_MODEL), lambda t, s, d, tg: (0, 0))],
        out_specs=pl.BlockSpec(memory_space=pl.ANY),
        scratch_shapes=[pltpu.VMEM((2, tm, XG_WIDTH), F32),
                        pltpu.VMEM((2, tm, D_MODEL), F32),
                        pltpu.SemaphoreType.DMA((2,)),
                        pltpu.SemaphoreType.DMA((2,))])
    grouped = lambda w: w.reshape(w.shape[0], N_GROUPS, EXP_PER_GROUP, w.shape[2], w.shape[3])
    return pl.pallas_call(
        _moe_routed_kernel,
        grid_spec=grid_spec,
        out_shape=jax.ShapeDtypeStruct((n_tiles * tm, D_MODEL), F32),
        compiler_params=_cparams(("arbitrary",)),
        name="moe_routed",
    )(src, dst, tile_group, xg, grouped(wg), grouped(wu), grouped(wd), g, b)


def _sc_mesh():
    return plsc.VectorSubcoreMesh(core_axis_name="core", subcore_axis_name="subcore")


def sc_scatter_rows(x, idx, n_out):
    r = x.shape[0]

    @functools.partial(pl.kernel, out_type=jax.ShapeDtypeStruct((n_out, LANES), x.dtype),
                       mesh=_sc_mesh(), scratch_types=[], name="sc_scatter_rows")
    def k(x_hbm, i_hbm, o_hbm):
        def body(x_vmem, i_vmem):
            pltpu.sync_copy(x_vmem, o_hbm.at[i_vmem.at[0]])

        pltpu.emit_pipeline(
            body,
            grid=(r // SC_WINDOW,),
            in_specs=[pl.BlockSpec((SC_WINDOW, LANES), lambda i: (i, 0)),
                      pl.BlockSpec((1, SC_WINDOW), lambda i: (0, i))],
            out_specs=[],
            core_axis_name=("core", "subcore"),
            dimension_semantics=(pltpu.PARALLEL,),
        )(x_hbm, i_hbm)

    return k(x, idx.reshape(1, r))


def sc_gather_rows(table, idx):
    r = idx.shape[0]

    @functools.partial(pl.kernel, out_type=jax.ShapeDtypeStruct((r, LANES), table.dtype),
                       mesh=_sc_mesh(), scratch_types=[], name="sc_gather_rows")
    def k(t_hbm, i_hbm, o_hbm):
        def body(i_vmem, o_vmem):
            pltpu.sync_copy(t_hbm.at[i_vmem.at[0]], o_vmem)

        pltpu.emit_pipeline(
            body,
            grid=(r // SC_WINDOW,),
            in_specs=[pl.BlockSpec((1, SC_WINDOW), lambda i: (0, i))],
            out_specs=[pl.BlockSpec((SC_WINDOW, LANES), lambda i: (i, 0))],
            core_axis_name=("core", "subcore"),
            dimension_semantics=(pltpu.PARALLEL,),
        )(i_hbm, o_hbm)

    return k(table, idx.reshape(1, r))


def _moe_sorted_kernel(tg_ref, nused_ref, xs_ref, wg_ref, wu_ref, wd_ref, g_ref, b_ref, o_ref):
    t = pl.program_id(0)

    @pl.when(t < nused_ref[0])
    def _():
        x = jnp.concatenate([xs_ref[j] for j in range(X_PLANES)], axis=1)
        gate = xs_ref[X_PLANES]
        xb = x.astype(BF16)
        lane = lax.broadcasted_iota(jnp.int32, gate.shape, 1)
        first = N_GROUPS + EXP_PER_GROUP * tg_ref[t]
        acc = jnp.zeros(x.shape, F32)
        for e in range(EXP_PER_GROUP):
            gcol = jnp.sum(jnp.where(lane == first + e, gate, 0.0), -1, keepdims=True)
            hid = _silu(_bdot(xb, wg_ref[e])) * _bdot(xb, wu_ref[e])
            acc = acc + _bdot(hid * gcol, wd_ref[e])
        y = _layer_norm(ALPHA * x + acc, g_ref[...], b_ref[...])
        for j in range(X_PLANES):
            o_ref[j] = y[:, j * LANES:(j + 1) * LANES]


def _group_slots(group_idx, n, tm):
    n_tiles = n // tm + N_GROUPS
    onehot = (group_idx[:, None] == jnp.arange(N_GROUPS)[None, :]).astype(jnp.int32)
    csum = jnp.cumsum(onehot, axis=0)
    counts = csum[-1]
    rank = jnp.sum(onehot * csum, axis=1) - 1
    tiles_g = (counts + tm - 1) // tm
    tile_end = jnp.cumsum(tiles_g)
    slot_base = (tile_end - tiles_g) * tm
    slot = (jnp.sum(onehot * slot_base[None, :], axis=1) + rank).astype(jnp.int32)
    tile_group = jnp.sum((jnp.arange(n_tiles)[:, None] >= tile_end[None, :]).astype(jnp.int32), axis=1)
    tile_group = jnp.minimum(tile_group, N_GROUPS - 1).astype(jnp.int32)
    return slot, tile_group, tile_end[-1:].astype(jnp.int32)


def moe_routed_sc(xg, wg, wu, wd, g, b, layer, n, tm, during_scatter, during_gather):
    slot, tile_group, n_used = _group_slots(xg[X_PLANES, :, 0].astype(jnp.int32), n, tm)
    n_tiles = tile_group.shape[0]
    n_slots = n_tiles * tm
    plane_base = lambda planes: jnp.arange(planes, dtype=jnp.int32)[:, None] * n_slots
    idx_in = (plane_base(XG_PLANES) + slot[None, :]).reshape(-1)
    idx_out = (plane_base(X_PLANES) + slot[None, :]).reshape(-1)
    xs = sc_scatter_rows(xg.reshape(XG_PLANES * n, LANES), idx_in, XG_PLANES * n_slots)
    xs = xs.reshape(XG_PLANES, n_slots, LANES)
    xs, side = lax.optimization_barrier((xs, during_scatter()))
    wspec = lambda shp: pl.BlockSpec((None, None, EXP_PER_GROUP) + shp,
                                     lambda t, tg, nu: (layer, tg[t], 0, 0, 0))
    grid_spec = pltpu.PrefetchScalarGridSpec(
        num_scalar_prefetch=2,
        grid=(n_tiles,),
        in_specs=[pl.BlockSpec((XG_PLANES, tm, LANES), lambda t, tg, nu: (0, t, 0)),
                  wspec((D_MODEL, EXP_FF)), wspec((D_MODEL, EXP_FF)), wspec((EXP_FF, D_MODEL)),
                  pl.BlockSpec((1, D_MODEL), lambda t, tg, nu: (0, 0)),
                  pl.BlockSpec((1, D_MODEL), lambda t, tg, nu: (0, 0))],
        out_specs=pl.BlockSpec((X_PLANES, tm, LANES), lambda t, tg, nu: (0, t, 0)))
    grouped = lambda w: w.reshape(w.shape[0], N_GROUPS, EXP_PER_GROUP, w.shape[2], w.shape[3])
    ys = pl.pallas_call(
        _moe_sorted_kernel,
        grid_spec=grid_spec,
        out_shape=jax.ShapeDtypeStruct((X_PLANES, n_slots, LANES), F32),
        compiler_params=_cparams(("arbitrary",)),
        name="moe_sorted",
    )(tile_group, n_used, xs, grouped(wg), grouped(wu), grouped(wd), g, b)
    y = sc_gather_rows(ys.reshape(X_PLANES * n_slots, LANES), idx_out)
    y, side = lax.optimization_barrier((y, during_gather(side)))
    return y.reshape(X_PLANES, n, LANES), side


def _t5_bucket(dist):
    max_exact = N_BUCKETS // 2
    d = jnp.maximum(dist, 0)
    df = jnp.maximum(d, 1).astype(F32)
    log_b = max_exact + (jnp.log(df / max_exact) / math.log(MAX_DISTANCE / max_exact)
                         * (N_BUCKETS - max_exact)).astype(jnp.int32)
    return jnp.where(d < max_exact, d, jnp.minimum(log_b, N_BUCKETS - 1))


def _bucket_lookup(rel_bias, bucket):
    out = jnp.zeros(bucket.shape + (rel_bias.shape[1],), F32)
    for i in range(N_BUCKETS):
        out = jnp.where((bucket == i)[..., None], rel_bias[i].astype(F32), out)
    return out


def _prompt_bias(rel_bias):
    qi = jnp.arange(WINDOW)[:, None]
    kj = jnp.arange(2 * WINDOW)[None, :]
    dist = qi + WINDOW - kj
    bias = _bucket_lookup(rel_bias, _t5_bucket(dist)).transpose(2, 0, 1)
    valid = (dist >= 0) & (dist <= WINDOW)
    return jnp.where(valid[None], bias, NEG)


def _decode_table(rel_bias, attn_sink):
    dist = WINDOW - jnp.arange(WINDOW + 1)
    bias = _bucket_lookup(rel_bias, _t5_bucket(dist)).T
    depth = attn_sink.shape[0]
    wide = lambda v: jnp.broadcast_to(v[..., None], v.shape + (WINDOW,))
    per_layer = lambda t: jnp.broadcast_to(t[None], (depth,) + t.shape)
    return jnp.stack([per_layer(bias[:, :WINDOW]), per_layer(wide(bias[:, WINDOW])),
                      wide(attn_sink.astype(F32))], axis=1)


def _block_ones(width):
    idx = jnp.arange(width) // HEAD_DIM
    return (idx[:, None] == idx[None, :])


def _head_rows_mask():
    head = jnp.arange(HG_WIDTH)[None, :] // HEAD_DIM
    return (head == jnp.arange(HG_MROWS)[:, None]).astype(F32)


def _block_diag(w):
    nblk, s, _ = w.shape
    eye = jnp.eye(nblk, dtype=w.dtype)
    return (eye[:, None, :, None] * w[:, :, None, :]).reshape(nblk * s, nblk * s)


def kernel(x_prompt, x_sample, mem_prompt, cache_win_k, cache_win_v, state_hgrn, state_conv, state_lru, cache_mem_k, cache_mem_v, rel_bias, hg_lb, w_in, attn_sink, hg_gain, conv_w, conv_b, lru_wa, lru_ba, lru_wx, lru_bx, lru_lam, w_out, ln1_g, ln1_b, x_wq, x_wk, x_wv, x_wo, ln2_g, ln2_b, r_gw, r_gb, r_ew, r_eb, e_wg, e_wu, e_wd, ln3_g, ln3_b):
    bp, seq, d = x_prompt.shape
    n_tok = bp * seq
    nd = x_sample.shape[0]
    depth = w_in.shape[0]

    lbs = jnp.cumsum(jax.nn.softmax(hg_lb.astype(F32), axis=0), axis=0)
    lbs = lbs - lbs[0]
    loglb = jnp.log(lbs)
    log1mlb = jnp.log1p(-lbs)
    gain4 = jnp.tile(hg_gain, (1, HG_HEADS))

    bias_p = _prompt_bias(rel_bias)
    bdm256 = _block_ones(HG_WIDTH).astype(F32)
    hmask = _head_rows_mask()

    w_in_b = w_in.astype(BF16)
    w_out_b = w_out.astype(BF16)
    wq_b = x_wq.astype(BF16)
    wkv_b = jnp.concatenate([x_wk, x_wv], axis=-1).astype(BF16)
    wo_b = x_wo.astype(BF16)
    wg_b = e_wg.astype(BF16)
    wu_b = e_wu.astype(BF16)
    wd_b = e_wd.astype(BF16)
    rew = r_ew.transpose(0, 2, 1, 3).reshape(depth, d, N_EXPERTS)
    wr = jnp.concatenate([r_gw, rew, jnp.zeros((depth, d, ROUTER_LANES - N_GROUPS - N_EXPERTS), F32)], -1)
    br = jnp.concatenate([r_gb, r_eb.reshape(depth, N_EXPERTS),
                          jnp.zeros((depth, ROUTER_LANES - N_GROUPS - N_EXPERTS), F32)], -1)

    a_w = A_HEADS * HEAD_DIM
    xp = x_prompt.reshape(bp * seq, d)
    xs = x_sample.reshape(nd, d)
    mem = mem_prompt.reshape(bp * N_MEM, d)
    ckt = cache_win_k.transpose(0, 1, 3, 4, 2).reshape(depth, nd, LANES, WINDOW)
    cvt = cache_win_v.transpose(0, 1, 3, 4, 2).reshape(depth, nd, LANES, WINDOW)
    cmkt = cache_mem_k.transpose(0, 1, 3, 4, 2).reshape(depth, nd, X_WIDTH, N_MEM)
    cmvt = cache_mem_v.transpose(0, 1, 3, 4, 2).reshape(depth, nd, X_WIDTH, N_MEM)
    state_t = state_hgrn.transpose(0, 2, 3, 4, 1).reshape(depth, HG_HEADS * HEAD_DIM * HEAD_DIM, nd)
    dec_tab = _decode_table(rel_bias, attn_sink)
    xq_mask = _head_rows_mask()[:X_QROWS]
    head_group = jnp.arange(A_HEADS) // A_REP

    p_wk, p_wv, p_s, p_cb, p_h, p_mk, p_mv = [], [], [], [], [], [], []
    s_cb, s_h = [], []
    for l in range(depth):
        row = lambda v: v[l].reshape(1, -1)
        wa_bd = _block_diag(lru_wa[l]).astype(BF16)
        wx_bd = _block_diag(lru_wx[l]).astype(BF16)
        lru_args = (conv_w[l], row(conv_b), wa_bd, row(lru_ba), wx_bd, row(lru_bx), row(lru_lam))
        wo_parts = [w_out_b[l, :a_w], w_out_b[l, a_w:a_w + HG_WIDTH], w_out_b[l, a_w + HG_WIDTH:]]

        proj = matmul(xp, w_in_b[l], 512, 768)
        oa = attn_prompt(proj, attn_sink[l], bias_p, bp, seq)
        ob, st = hgrn_prompt(proj, row(loglb), row(log1mlb), row(gain4), bdm256, hmask, bp, seq, 256)
        oc, hl = lru_prompt(proj, *lru_args, bp, seq, 256)
        xp = proj_res_ln(xp, [oa, ob, oc], wo_parts, row(ln1_g), row(ln1_b), 512)
        mkv = matmul(mem, wkv_b[l], 256, 512)
        xg = xattn_prompt(xp, wq_b[l], mkv, wo_b[l], row(ln2_g), row(ln2_b), wr[l], br[l:l + 1],
                          bp, seq, 512)
        def decode_mixers(xs=xs, ckt=ckt, cvt=cvt, state_t=state_t, l=l, lru_args=lru_args,
                          wo_parts=wo_parts, row=row):
            projd = matmul(xs, w_in_b[l], nd, 768)
            q3 = projd[:, :a_w].reshape(nd, A_HEADS, 1, HEAD_DIM)
            on_group = head_group[None, :, None, None] == jnp.arange(A_KV_HEADS)[None, None, :, None]
            qblk = jnp.where(on_group, q3, 0.0).reshape(nd, A_HEADS, LANES)
            o3, ckt, cvt = attn_decode(qblk, projd, dec_tab[l], ckt, cvt, l, 16)
            o4 = o3.reshape(nd, A_HEADS, A_KV_HEADS, HEAD_DIM)
            oa = jnp.sum(jnp.where(on_group, o4, 0.0), axis=2).reshape(nd, a_w)
            gates_t = projd[:, a_w + 2 * LANES:a_w + 2 * LANES + 4 * HG_WIDTH].T
            bc = lambda v: jnp.broadcast_to(v[:, None], (v.shape[0], nd))
            ob_t, state_t = hgrn_decode(gates_t, bc(loglb[l]), bc(log1mlb[l]), bc(hg_gain[l]), state_t, l)
            oc, nh, nbuf = lru_decode(projd, state_conv[l].reshape(nd, -1), state_lru[l], *lru_args)
            xs = proj_res_ln(xs, [oa, ob_t.T, oc], wo_parts, row(ln1_g), row(ln1_b), nd)
            return xs, ckt, cvt, state_t, nh, nbuf

        def decode_rest(side, l=l, row=row):
            xs = side[0]
            qd = matmul(xs, wq_b[l], nd, X_WIDTH)
            qdb = qd[:, None, :] * xq_mask[None, :, :]
            od = xattn_decode(qdb, cmkt, cmvt, xq_mask, l, 8)
            xs = proj_res_ln(xs, [od], [wo_b[l]], row(ln2_g), row(ln2_b), nd)
            xs = moe_dense(xs, wr[l], br[l:l + 1], wg_b, wu_b, wd_b, row(ln3_g), row(ln3_b), l, nd)
            return (xs,) + tuple(side[1:])

        xp, (xs, ckt, cvt, state_t, nh, nbuf) = moe_routed_sc(
            xg, wg_b, wu_b, wd_b, row(ln3_g), row(ln3_b), l, n_tok, MOE_TM, decode_mixers, decode_rest)

        proj3 = proj.reshape(bp, seq, IN_COLS)
        p_wk.append(proj3[:, seq - WINDOW:, a_w:a_w + LANES].reshape(bp, WINDOW, A_KV_HEADS, HEAD_DIM))
        p_wv.append(proj3[:, seq - WINDOW:, a_w + LANES:a_w + 2 * LANES].reshape(bp, WINDOW, A_KV_HEADS, HEAD_DIM))
        st5 = st.reshape(bp, HG_HEADS, HEAD_DIM, HG_HEADS, HEAD_DIM)
        p_s.append(jnp.stack([st5[:, h, :, h, :] for h in range(HG_HEADS)], 1).transpose(0, 1, 3, 2))
        p_cb.append(proj3[:, seq - (CONV_W - 1):, IN_COLS - 2 * LRU_WIDTH:IN_COLS - LRU_WIDTH])
        p_h.append(hl.reshape(bp, LRU_WIDTH))
        p_mk.append(mkv[:, :X_WIDTH].reshape(bp, N_MEM, X_HEADS, HEAD_DIM))
        p_mv.append(mkv[:, X_WIDTH:].reshape(bp, N_MEM, X_HEADS, HEAD_DIM))

        s_cb.append(nbuf.reshape(nd, CONV_W - 1, LRU_WIDTH))
        s_h.append(nh)

    xp = jnp.transpose(xp, (1, 0, 2))
    unkey = lambda c: c.reshape(depth, nd, A_KV_HEADS, HEAD_DIM, WINDOW).transpose(0, 1, 4, 2, 3)
    s_s = state_t.reshape(depth, HG_HEADS, HEAD_DIM, HEAD_DIM, nd).transpose(0, 4, 1, 2, 3)
    return (xp.reshape(bp, seq, d), xs.reshape(nd, 1, d),
            jnp.stack(p_wk), jnp.stack(p_wv), jnp.stack(p_s), jnp.stack(p_cb), jnp.stack(p_h),
            jnp.stack(p_mk), jnp.stack(p_mv),
            unkey(ckt), unkey(cvt), s_s, jnp.stack(s_cb), jnp.stack(s_h))
```

```python
import functools
import math

import jax
import jax.numpy as jnp
from jax import lax
from jax.experimental import pallas as pl
from jax.experimental.pallas import tpu as pltpu
from jax.experimental.pallas import tpu_sc as plsc

F32 = jnp.float32
BF16 = jnp.bfloat16

D_MODEL = 1024
DEPTH = 4
HEAD_DIM = 64
A_HEADS = 8
A_KV_HEADS = 2
A_REP = A_HEADS // A_KV_HEADS
WINDOW = 128
A_QB = 4
N_BUCKETS = 32
MAX_DISTANCE = 128
HG_WIDTH = 256
HG_HEADS = 4
HG_CHUNK = 64
HG_TB = 16
HG_MROWS = 8
LOG2E = math.log2(math.e)
LRU_WIDTH = 256
LRU_BLOCKS = 4
CONV_W = 4
LRU_C = 8.0
N_MEM = 256
X_HEADS = 4
X_WIDTH = X_HEADS * HEAD_DIM
X_QROWS = 8
N_GROUPS = 4
EXP_PER_GROUP = 4
N_EXPERTS = N_GROUPS * EXP_PER_GROUP
EXP_FF = D_MODEL // 4
ALPHA = (2 * DEPTH) ** 0.25
LN_EPS = 1e-5
RMS_EPS = 1e-6
IN_COLS = 2304
SCALE = HEAD_DIM ** -0.5
NEG = -1e30
LANES = 128
ROUTER_LANES = 128
XG_WIDTH = D_MODEL + ROUTER_LANES
XG_PLANES = XG_WIDTH // LANES
X_PLANES = D_MODEL // LANES
MOE_TM = 512
SC_WINDOW = 128
VMEM_LIMIT = 48 * 1024 * 1024


def _cparams(sem):
    return pltpu.CompilerParams(dimension_semantics=sem, vmem_limit_bytes=VMEM_LIMIT)


def _bdot(a, b):
    return jnp.dot(a.astype(BF16), b.astype(BF16), preferred_element_type=F32)


def _bdot_nt(a, b):
    return lax.dot_general(a.astype(BF16), b.astype(BF16), (((1,), (1,)), ((), ())),
                           preferred_element_type=F32)


def _bdot_tn(a, b):
    return lax.dot_general(a.astype(BF16), b.astype(BF16), (((0,), (0,)), ((), ())),
                           preferred_element_type=F32)


def _rb(x):
    return x.astype(BF16).astype(F32)


def _silu(x):
    return x * jax.nn.sigmoid(x)


def _neg_expm1(x):
    return -jnp.tanh(0.5 * x) * (jnp.exp(x) + 1.0)


def _softplus(x):
    return jnp.maximum(x, 0.0) + jnp.log1p(jnp.exp(-jnp.abs(x)))


def _gelu_tanh(x):
    return 0.5 * x * (1.0 + jnp.tanh(math.sqrt(2.0 / math.pi) * (x + 0.044715 * (x * x * x))))


def _layer_norm(y, g, b):
    mu = jnp.mean(y, -1, keepdims=True)
    yc = y - mu
    var = jnp.mean(yc * yc, -1, keepdims=True)
    return yc * lax.rsqrt(var + LN_EPS) * g + b


def _rows(x_ref):
    if len(x_ref.shape) == 2:
        return x_ref[...]
    return jnp.concatenate([x_ref[j] for j in range(x_ref.shape[0])], axis=1)


def _rows_spec(x, tm, nargs):
    if x.ndim == 2:
        return pl.BlockSpec((tm, x.shape[1]), (lambda i: (i, 0)) if nargs == 1 else (lambda i, j: (i, 0)))
    blk = (x.shape[0], tm, LANES)
    return pl.BlockSpec(blk, (lambda i: (0, i, 0)) if nargs == 1 else (lambda i, j: (0, i, 0)))


def _mm_kernel(x_ref, w_ref, o_ref):
    o_ref[...] = _bdot(_rows(x_ref), w_ref[...])


def matmul(x, w, tm, tn):
    m = x.shape[-2]
    k, n = w.shape
    return pl.pallas_call(
        _mm_kernel,
        grid=(m // tm, n // tn),
        in_specs=[_rows_spec(x, tm, 2),
                  pl.BlockSpec((k, tn), lambda i, j: (0, j))],
        out_specs=pl.BlockSpec((tm, tn), lambda i, j: (i, j)),
        out_shape=jax.ShapeDtypeStruct((m, n), F32),
        compiler_params=_cparams(("parallel", "parallel")),
        name="matmul",
    )(x, w)


def _proj_res_ln_kernel(n_in, x_ref, *refs):
    a_refs = refs[:n_in]
    w_refs = refs[n_in:2 * n_in]
    g_ref, b_ref, o_ref = refs[2 * n_in:]
    y = ALPHA * _rows(x_ref)
    for a_ref, w_ref in zip(a_refs, w_refs):
        y = y + _bdot(a_ref[...], w_ref[...])
    o_ref[...] = _layer_norm(y, g_ref[...], b_ref[...])


def proj_res_ln(x, a_list, w_list, g, b, tm):
    m = x.shape[-2]
    d = w_list[0].shape[1]
    n_in = len(a_list)
    in_specs = [_rows_spec(x, tm, 1)]
    in_specs += [pl.BlockSpec((tm, a.shape[1]), lambda i: (i, 0)) for a in a_list]
    in_specs += [pl.BlockSpec(w.shape, lambda i: (0, 0)) for w in w_list]
    in_specs += [pl.BlockSpec((1, d), lambda i: (0, 0))] * 2
    return pl.pallas_call(
        functools.partial(_proj_res_ln_kernel, n_in),
        grid=(m // tm,),
        in_specs=in_specs,
        out_specs=pl.BlockSpec((tm, d), lambda i: (i, 0)),
        out_shape=jax.ShapeDtypeStruct((m, d), F32),
        compiler_params=_cparams(("parallel",)),
        name="proj_res_ln",
    )(x, *a_list, *w_list, g, b)


def _attn_prompt_kernel(sink_ref, q_ref, kc_ref, kp_ref, vc_ref, vp_ref, bias_ref, o_ref):
    n = pl.program_id(1)
    col = lax.broadcasted_iota(jnp.int32, (WINDOW, 2 * WINDOW), 1)
    first = jnp.where((n == 0) & (col < WINDOW), NEG, 0.0)
    kk = jnp.concatenate([kp_ref[...], kc_ref[...]], axis=0).astype(BF16)
    vv = jnp.concatenate([vp_ref[...], vc_ref[...]], axis=0).astype(BF16)
    q = q_ref[...].astype(BF16)
    for u in range(A_QB):
        rows = slice(u * WINDOW, (u + 1) * WINDOW)
        keys = slice(u * WINDOW, (u + 2) * WINDOW)
        outs = []
        for h in range(A_HEADS):
            g = h // A_REP
            qh = q[rows, h * HEAD_DIM:(h + 1) * HEAD_DIM]
            kg = kk[keys, g * HEAD_DIM:(g + 1) * HEAD_DIM]
            vg = vv[keys, g * HEAD_DIM:(g + 1) * HEAD_DIM]
            s = _bdot_nt(qh, kg) * SCALE + bias_ref[h]
            if u == 0:
                s = s + first
            sink = sink_ref[h]
            m = jnp.maximum(jnp.max(s, -1, keepdims=True), sink)
            p = jnp.exp(s - m)
            den = jnp.sum(p, -1, keepdims=True) + jnp.exp(sink - m)
            outs.append(_bdot(p / den, vg))
        o_ref[rows, :] = jnp.concatenate(outs, axis=1)


def attn_prompt(proj, sink, bias, batch, seq):
    nb = seq // WINDOW
    ns = nb // A_QB
    tq = A_QB * WINDOW
    qcol = 0
    kcol = (A_HEADS * HEAD_DIM) // LANES
    vcol = kcol + 1

    def cur(c):
        return lambda b, n: (b * ns + n, c)

    def prev(c):
        return lambda b, n: (b * nb + jnp.maximum(n * A_QB - 1, 0), c)

    return pl.pallas_call(
        _attn_prompt_kernel,
        grid=(batch, ns),
        in_specs=[pl.BlockSpec(memory_space=pltpu.SMEM),
                  pl.BlockSpec((tq, A_HEADS * HEAD_DIM), cur(qcol)),
                  pl.BlockSpec((tq, LANES), cur(kcol)),
                  pl.BlockSpec((WINDOW, LANES), prev(kcol)),
                  pl.BlockSpec((tq, LANES), cur(vcol)),
                  pl.BlockSpec((WINDOW, LANES), prev(vcol)),
                  pl.BlockSpec((A_HEADS, WINDOW, 2 * WINDOW), lambda b, n: (0, 0, 0))],
        out_specs=pl.BlockSpec((tq, A_HEADS * HEAD_DIM), cur(0)),
        out_shape=jax.ShapeDtypeStruct((batch * seq, A_HEADS * HEAD_DIM), F32),
        compiler_params=_cparams(("parallel", "parallel")),
        name="attn_prompt",
    )(sink, proj, proj, proj, proj, proj, bias)


def _attn_decode_kernel(qb_ref, kn_ref, vn_ref, knt_ref, vnt_ref, ck_ref, cv_ref, tab_ref,
                        o_ref, ok_ref, ov_ref):
    bb = qb_ref.shape[0]
    ck = ck_ref[...]
    cv = cv_ref[...]
    qb = qb_ref[...]
    kn = kn_ref[...]
    vn = vn_ref[...]
    bias_j = tab_ref[0]
    bias_n = tab_ref[1][:, 0:1]
    sink = tab_ref[2][:, 0:1]
    s = lax.dot_general(qb.astype(BF16), ck.astype(BF16), (((2,), (1,)), ((0,), (0,))),
                        preferred_element_type=F32) * SCALE + bias_j[None]
    sn = jnp.sum(_rb(qb) * _rb(kn)[:, None, :], -1, keepdims=True) * SCALE + bias_n[None]
    m = jnp.maximum(jnp.maximum(jnp.max(s, -1, keepdims=True), sn), sink[None])
    p = jnp.exp(s - m)
    pn = jnp.exp(sn - m)
    den = jnp.sum(p, -1, keepdims=True) + pn + jnp.exp(sink[None] - m)
    o = lax.dot_general((p / den).astype(BF16), cv.astype(BF16), (((2,), (2,)), ((0,), (0,))),
                        preferred_element_type=F32)
    o_ref[...] = o + _rb(pn / den) * _rb(vn)[:, None, :]
    lane = lax.broadcasted_iota(jnp.int32, (LANES, LANES), 1)
    for b in range(bb):
        ok_ref[b] = jnp.where(lane == WINDOW - 1, knt_ref[:, b:b + 1], pltpu.roll(ck[b], WINDOW - 1, 1))
        ov_ref[b] = jnp.where(lane == WINDOW - 1, vnt_ref[:, b:b + 1], pltpu.roll(cv[b], WINDOW - 1, 1))


def attn_decode(qblk, proj_d, table, cache_k, cache_v, layer, bb):
    nbatch = proj_d.shape[0]
    a_w = A_HEADS * HEAD_DIM
    cols = lambda c: proj_d[:, c:c + LANES].reshape(nbatch // bb, bb, LANES).transpose(0, 2, 1)
    knt, vnt = cols(a_w), cols(a_w + LANES)
    kcol = (A_HEADS * HEAD_DIM) // LANES
    cache_spec = pl.BlockSpec((None, bb, LANES, WINDOW), lambda i: (layer, i, 0, 0))
    col_spec = pl.BlockSpec((None, LANES, bb), lambda i: (i, 0, 0))
    return pl.pallas_call(
        _attn_decode_kernel,
        grid=(nbatch // bb,),
        in_specs=[pl.BlockSpec((bb, A_HEADS, LANES), lambda i: (i, 0, 0)),
                  pl.BlockSpec((bb, LANES), lambda i: (i, kcol)),
                  pl.BlockSpec((bb, LANES), lambda i: (i, kcol + 1)),
                  col_spec, col_spec, cache_spec, cache_spec,
                  pl.BlockSpec((3, A_HEADS, WINDOW), lambda i: (0, 0, 0))],
        out_specs=[pl.BlockSpec((bb, A_HEADS, LANES), lambda i: (i, 0, 0)), cache_spec, cache_spec],
        out_shape=[jax.ShapeDtypeStruct((nbatch, A_HEADS, LANES), F32),
                   jax.ShapeDtypeStruct(cache_k.shape, F32),
                   jax.ShapeDtypeStruct(cache_v.shape, F32)],
        input_output_aliases={5: 1, 6: 2},
        compiler_params=_cparams(("arbitrary",)),
        name="attn_decode",
    )(qblk, proj_d, proj_d, knt, vnt, cache_k, cache_v, table)


def _xattn_decode_kernel(qb_ref, mk_ref, mv_ref, hm_ref, o_ref):
    qb = qb_ref[...]
    s = lax.dot_general(qb.astype(BF16), mk_ref[...].astype(BF16), (((2,), (1,)), ((0,), (0,))),
                        preferred_element_type=F32) * SCALE
    m = jnp.max(s, -1, keepdims=True)
    p = jnp.exp(s - m)
    p = p / jnp.sum(p, -1, keepdims=True)
    o = lax.dot_general(p.astype(BF16), mv_ref[...].astype(BF16), (((2,), (2,)), ((0,), (0,))),
                        preferred_element_type=F32)
    o_ref[...] = jnp.sum(o * hm_ref[...][None], axis=1)


def xattn_decode(qblk, mem_k, mem_v, hmask, layer, bb):
    nbatch = qblk.shape[0]
    mem_spec = pl.BlockSpec((None, bb, X_WIDTH, N_MEM), lambda i: (layer, i, 0, 0))
    return pl.pallas_call(
        _xattn_decode_kernel,
        grid=(nbatch // bb,),
        in_specs=[pl.BlockSpec((bb, X_QROWS, X_WIDTH), lambda i: (i, 0, 0)), mem_spec, mem_spec,
                  pl.BlockSpec((X_QROWS, X_WIDTH), lambda i: (0, 0))],
        out_specs=pl.BlockSpec((bb, X_WIDTH), lambda i: (i, 0)),
        out_shape=jax.ShapeDtypeStruct((nbatch, X_WIDTH), F32),
        compiler_params=_cparams(("parallel",)),
        name="xattn_decode",
    )(qblk, mem_k, mem_v, hmask)


def _hgrn_gates(hq, hf, loglb, log1mlb):
    ls = jnp.minimum(hf, 0.0) - jnp.log1p(jnp.exp(-jnp.abs(hf)))
    b = log1mlb + ls
    lf = jnp.maximum(loglb, b) + jnp.log1p(jnp.exp(-jnp.abs(loglb - b)))
    return _silu(hq), lf, _neg_expm1(lf)


def _hgrn_prompt_kernel(hq_ref, hf_ref, hi_ref, hg_ref, loglb_ref, log1mlb_ref, gain_ref,
                        bdm_ref, hm_ref, ob_ref, st_ref, st_scr, q_scr, k_scr, cum_scr, o_scr):
    i = pl.program_id(1)
    tt = hq_ref.shape[0]
    c = HG_CHUNK
    tb = HG_TB

    @pl.when(i == 0)
    def _():
        st_scr[...] = jnp.zeros_like(st_scr)

    qs, lf, kk = _hgrn_gates(hq_ref[...], hf_ref[...], loglb_ref[...], log1mlb_ref[...])
    row = lax.broadcasted_iota(jnp.int32, (tt, HG_WIDTH), 0) & (c - 1)
    cum = lf
    sh = 1
    while sh < c:
        cum = cum + jnp.where(row >= sh, pltpu.roll(cum, sh, 0), 0.0)
        sh *= 2
    q_scr[...] = qs
    k_scr[...] = kk
    cum_scr[...] = cum

    bdm = bdm_ref[...]
    hmask = hm_ref[...]
    def chunk(ci, carry):
        r0 = pl.multiple_of(ci * c, c)
        r = pl.ds(r0, c)
        cu = cum_scr[r, :]
        q = q_scr[r, :]
        k = k_scr[r, :]
        v = hi_ref[r, :]
        vb = v.astype(BF16)
        qb = _rb(q)
        cu2 = cu * LOG2E
        last = cu[c - 1:c, :]
        st = st_scr[...]
        o_inter = _bdot_nt(q * jnp.exp(cu), st)
        for j in range(c // tb):
            ns = tb * (j + 1)
            ti = lax.broadcasted_iota(jnp.int32, (tb, ns, HG_WIDTH), 0) + tb * j
            si = lax.broadcasted_iota(jnp.int32, (tb, ns, HG_WIDTH), 1)
            cut = cu2[tb * j:tb * (j + 1), :]
            dec = jnp.exp2(jnp.where(ti >= si, cut[:, None, :] - cu2[None, :ns, :], NEG))
            a2 = (dec * k[None, :ns, :]).astype(BF16)
            q4 = qb[tb * j:tb * (j + 1), None, :] * hmask[None, :, :]
            att = lax.dot_general(q4, a2, (((2,), (2,)), ((0,), (0,))),
                                  preferred_element_type=F32)
            w = jnp.dot(_rb(att.reshape(tb * HG_MROWS, ns)), vb[:ns, :],
                        preferred_element_type=F32).reshape(tb, HG_MROWS, HG_WIDTH)
            o_intra = jnp.sum(w * hmask[None, :, :], axis=1)
            o_scr[pl.ds(r0 + tb * j, tb), :] = o_intra + o_inter[tb * j:tb * (j + 1), :]
        upd = _bdot_tn(v, k * jnp.exp(last - cu))
        st_scr[...] = st * jnp.exp(last) + upd * bdm
        return carry

    lax.fori_loop(0, tt // c, chunk, 0)

    o = o_scr[...]
    ms = jnp.dot(o * o, bdm, precision=lax.Precision.HIGHEST,
                 preferred_element_type=F32) * (1.0 / HEAD_DIM)
    ob_ref[...] = o * lax.rsqrt(ms + RMS_EPS) * gain_ref[...] * _silu(hg_ref[...])

    @pl.when(i == pl.num_programs(1) - 1)
    def _():
        st_ref[...] = st_scr[...]


def hgrn_prompt(proj, loglb, log1mlb, gain4, bdm, hmask, batch, seq, tt):
    nt = seq // tt
    base = (A_HEADS + 2 * A_KV_HEADS) * HEAD_DIM // HG_WIDTH

    def col(cblk):
        return pl.BlockSpec((tt, HG_WIDTH), lambda b, i: (b * nt + i, cblk))

    row_spec = pl.BlockSpec((1, HG_WIDTH), lambda b, i: (0, 0))
    mat_spec = pl.BlockSpec((HG_WIDTH, HG_WIDTH), lambda b, i: (0, 0))
    return pl.pallas_call(
        _hgrn_prompt_kernel,
        grid=(batch, nt),
        in_specs=[col(base), col(base + 1), col(base + 2), col(base + 3),
                  row_spec, row_spec, row_spec, mat_spec,
                  pl.BlockSpec((HG_MROWS, HG_WIDTH), lambda b, i: (0, 0))],
        out_specs=[pl.BlockSpec((tt, HG_WIDTH), lambda b, i: (b * nt + i, 0)),
                   pl.BlockSpec((None, HG_WIDTH, HG_WIDTH), lambda b, i: (b, 0, 0))],
        out_shape=[jax.ShapeDtypeStruct((batch * seq, HG_WIDTH), F32),
                   jax.ShapeDtypeStruct((batch, HG_WIDTH, HG_WIDTH), F32)],
        scratch_shapes=[pltpu.VMEM((HG_WIDTH, HG_WIDTH), F32),
                        pltpu.VMEM((tt, HG_WIDTH), F32),
                        pltpu.VMEM((tt, HG_WIDTH), F32),
                        pltpu.VMEM((tt, HG_WIDTH), F32),
                        pltpu.VMEM((tt, HG_WIDTH), F32)],
        compiler_params=_cparams(("parallel", "arbitrary")),
        name="hgrn_prompt",
    )(proj, proj, proj, proj, loglb, log1mlb, gain4, bdm, hmask)


def _hgrn_decode_kernel(hq_ref, hf_ref, hi_ref, hg_ref, loglb_ref, log1mlb_ref, gain_ref, s_ref,
                        ob_ref, so_ref):
    nb = hq_ref.shape[1]
    qs, lf, kk = _hgrn_gates(hq_ref[...], hf_ref[...], loglb_ref[...], log1mlb_ref[...])
    v = hi_ref[...]
    f = jnp.exp(lf)
    s = s_ref[...].reshape(HEAD_DIM, HEAD_DIM, nb)
    att = jnp.sum(_rb(qs) * _rb(kk), axis=0, keepdims=True)
    o = _rb(att) * _rb(v) + jnp.sum(_rb(qs * f)[:, None, :] * _rb(s), axis=0)
    s_new = f[:, None, :] * s + _rb(kk)[:, None, :] * _rb(v)[None, :, :]
    so_ref[...] = s_new.reshape(HEAD_DIM * HEAD_DIM, nb)
    ms = jnp.mean(o * o, axis=0, keepdims=True)
    ob_ref[...] = o * lax.rsqrt(ms + RMS_EPS) * gain_ref[...] * _silu(hg_ref[...])


def hgrn_decode(gates_t, loglb_t, log1mlb_t, gain_t, state_t, layer):
    nb = gates_t.shape[1]

    def blk(off):
        return pl.BlockSpec((HEAD_DIM, nb), lambda h: (off * HG_HEADS + h, 0))

    par = pl.BlockSpec((HEAD_DIM, nb), lambda h: (h, 0))
    st = pl.BlockSpec((None, HEAD_DIM * HEAD_DIM, nb), lambda h: (layer, h, 0))
    return pl.pallas_call(
        _hgrn_decode_kernel,
        grid=(HG_HEADS,),
        in_specs=[blk(0), blk(1), blk(2), blk(3), par, par,
                  pl.BlockSpec((HEAD_DIM, nb), lambda h: (0, 0)), st],
        out_specs=[par, st],
        out_shape=[jax.ShapeDtypeStruct((HG_WIDTH, nb), F32),
                   jax.ShapeDtypeStruct(state_t.shape, F32)],
        input_output_aliases={7: 1},
        compiler_params=_cparams(("arbitrary",)),
        name="hgrn_decode",
    )(gates_t, gates_t, gates_t, gates_t, loglb_t, log1mlb_t, gain_t, state_t)


def _lru_gates(xc, wa_ref, ba_ref, wx_ref, bx_ref, lam_ref):
    r = jax.nn.sigmoid(_bdot(xc, wa_ref[...]) + ba_ref[...])
    gi = jax.nn.sigmoid(_bdot(xc, wx_ref[...]) + bx_ref[...])
    log_a = -LRU_C * r * _softplus(-lam_ref[...])
    a = jnp.exp(log_a)
    bterm = jnp.sqrt(_neg_expm1(2.0 * log_a)) * (gi * xc)
    return a, bterm


def _lru_prompt_kernel(lx_ref, lg_ref, cw_ref, cb_ref, wa_ref, ba_ref, wx_ref, bx_ref, lam_ref,
                       oc_ref, hl_ref, ext_scr, h_scr):
    i = pl.program_id(1)
    tt = lx_ref.shape[0]
    pad = 8

    @pl.when(i == 0)
    def _():
        ext_scr[0:pad, :] = jnp.zeros((pad, LRU_WIDTH), F32)
        h_scr[...] = jnp.zeros_like(h_scr)

    x = lx_ref[...]
    ext_scr[pad:pad + tt, :] = x
    xc = cb_ref[...] + cw_ref[CONV_W - 1:CONV_W, :] * x
    for j in range(CONV_W - 1):
        back = CONV_W - 1 - j
        xc = xc + cw_ref[j:j + 1, :] * ext_scr[pad - back:pad - back + tt, :]
    ext_scr[0:pad, :] = x[tt - pad:tt, :]

    a, bterm = _lru_gates(xc, wa_ref, ba_ref, wx_ref, bx_ref, lam_ref)
    row = lax.broadcasted_iota(jnp.int32, (tt, LRU_WIDTH), 0)
    sh = 1
    while sh < tt:
        keep = row >= sh
        b_s = jnp.where(keep, pltpu.roll(bterm, sh, 0), 0.0)
        a_s = jnp.where(keep, pltpu.roll(a, sh, 0), 1.0)
        bterm = a * b_s + bterm
        a = a * a_s
        sh *= 2
    h = a * h_scr[...] + bterm
    h_scr[...] = h[tt - 1:tt, :]
    oc_ref[...] = h * _gelu_tanh(lg_ref[...])

    @pl.when(i == pl.num_programs(1) - 1)
    def _():
        hl_ref[...] = h[tt - 1:tt, :]


def lru_prompt(proj, conv_w, conv_b, wa_bd, ba, wx_bd, bx, lam, batch, seq, tt):
    nt = seq // tt
    base = IN_COLS // LRU_WIDTH - 2

    def col(cblk):
        return pl.BlockSpec((tt, LRU_WIDTH), lambda b, i: (b * nt + i, cblk))

    row_spec = pl.BlockSpec((1, LRU_WIDTH), lambda b, i: (0, 0))
    mat_spec = pl.BlockSpec((LRU_WIDTH, LRU_WIDTH), lambda b, i: (0, 0))
    return pl.pallas_call(
        _lru_prompt_kernel,
        grid=(batch, nt),
        in_specs=[col(base), col(base + 1),
                  pl.BlockSpec((CONV_W, LRU_WIDTH), lambda b, i: (0, 0)), row_spec,
                  mat_spec, row_spec, mat_spec, row_spec, row_spec],
        out_specs=[pl.BlockSpec((tt, LRU_WIDTH), lambda b, i: (b * nt + i, 0)),
                   pl.BlockSpec((None, 1, LRU_WIDTH), lambda b, i: (b, 0, 0))],
        out_shape=[jax.ShapeDtypeStruct((batch * seq, LRU_WIDTH), F32),
                   jax.ShapeDtypeStruct((batch, 1, LRU_WIDTH), F32)],
        scratch_shapes=[pltpu.VMEM((tt + 8, LRU_WIDTH), F32),
                        pltpu.VMEM((1, LRU_WIDTH), F32)],
        compiler_params=_cparams(("parallel", "arbitrary")),
        name="lru_prompt",
    )(proj, proj, conv_w, conv_b, wa_bd, ba, wx_bd, bx, lam)


def _lru_decode_kernel(lx_ref, lg_ref, buf_ref, h0_ref, cw_ref, cb_ref, wa_ref, ba_ref, wx_ref,
                       bx_ref, lam_ref, oc_ref, hn_ref, nbuf_ref):
    x = lx_ref[...]
    buf = buf_ref[...]
    xc = cb_ref[...] + cw_ref[CONV_W - 1:CONV_W, :] * x
    for j in range(CONV_W - 1):
        xc = xc + cw_ref[j:j + 1, :] * buf[:, j * LRU_WIDTH:(j + 1) * LRU_WIDTH]
    a, bterm = _lru_gates(xc, wa_ref, ba_ref, wx_ref, bx_ref, lam_ref)
    h = a * h0_ref[...] + bterm
    hn_ref[...] = h
    oc_ref[...] = h * _gelu_tanh(lg_ref[...])
    nbuf_ref[...] = jnp.concatenate([buf[:, LRU_WIDTH:], x], axis=1)


def lru_decode(proj_d, conv_buf, h0, conv_w, conv_b, wa_bd, ba, wx_bd, bx, lam):
    nb = proj_d.shape[0]
    base = IN_COLS // LRU_WIDTH - 2
    row_spec = pl.BlockSpec((1, LRU_WIDTH), lambda i: (0, 0))
    mat_spec = pl.BlockSpec((LRU_WIDTH, LRU_WIDTH), lambda i: (0, 0))
    act = pl.BlockSpec((nb, LRU_WIDTH), lambda i: (0, 0))
    bufs = pl.BlockSpec((nb, (CONV_W - 1) * LRU_WIDTH), lambda i: (0, 0))
    return pl.pallas_call(
        _lru_decode_kernel,
        grid=(1,),
        in_specs=[pl.BlockSpec((nb, LRU_WIDTH), lambda i: (0, base)),
                  pl.BlockSpec((nb, LRU_WIDTH), lambda i: (0, base + 1)),
                  bufs, act, pl.BlockSpec((CONV_W, LRU_WIDTH), lambda i: (0, 0)), row_spec,
                  mat_spec, row_spec, mat_spec, row_spec, row_spec],
        out_specs=[act, act, bufs],
        out_shape=[jax.ShapeDtypeStruct((nb, LRU_WIDTH), F32),
                   jax.ShapeDtypeStruct((nb, LRU_WIDTH), F32),
                   jax.ShapeDtypeStruct((nb, (CONV_W - 1) * LRU_WIDTH), F32)],
        compiler_params=_cparams(("arbitrary",)),
        name="lru_decode",
    )(proj_d, proj_d, conv_buf, h0, conv_w, conv_b, wa_bd, ba, wx_bd, bx, lam)


def _xattn_prompt_kernel(x_ref, wq_ref, mk_ref, mv_ref, wo_ref, g_ref, b_ref, wr_ref, br_ref, o_ref):
    x = x_ref[...]
    q = _bdot(x, wq_ref[...]).astype(BF16)
    mk = mk_ref[...].astype(BF16)
    mv = mv_ref[...].astype(BF16)
    outs = []
    for h in range(X_HEADS):
        sl = slice(h * HEAD_DIM, (h + 1) * HEAD_DIM)
        s = _bdot_nt(q[:, sl], mk[:, sl]) * SCALE
        m = jnp.max(s, -1, keepdims=True)
        p = jnp.exp(s - m)
        p = p / jnp.sum(p, -1, keepdims=True)
        outs.append(_bdot(p, mv[:, sl]))
    o = jnp.concatenate(outs, axis=1)
    y = _layer_norm(ALPHA * x + _bdot(o, wo_ref[...]), g_ref[...], b_ref[...])
    logits = _bdot(y, wr_ref[...]) + br_ref[...]
    gate, g_idx = _route(logits)
    lane = lax.broadcasted_iota(jnp.int32, gate.shape, 1)
    for j in range(D_MODEL // LANES):
        o_ref[j] = y[:, j * LANES:(j + 1) * LANES]
    o_ref[D_MODEL // LANES] = jnp.where(lane == 0, g_idx.astype(F32), gate)


def xattn_prompt(x, wq, mem_kv, wo, g, b, wr, br, batch, seq, tt):
    nt = seq // tt
    const = lambda bi, i: (0, 0)
    return pl.pallas_call(
        _xattn_prompt_kernel,
        grid=(batch, nt),
        in_specs=[pl.BlockSpec((tt, D_MODEL), lambda bi, i: (bi * nt + i, 0)),
                  pl.BlockSpec((D_MODEL, X_WIDTH), const),
                  pl.BlockSpec((N_MEM, X_WIDTH), lambda bi, i: (bi, 0)),
                  pl.BlockSpec((N_MEM, X_WIDTH), lambda bi, i: (bi, 1)),
                  pl.BlockSpec((X_WIDTH, D_MODEL), const),
                  pl.BlockSpec((1, D_MODEL), const),
                  pl.BlockSpec((1, D_MODEL), const),
                  pl.BlockSpec((D_MODEL, ROUTER_LANES), const),
                  pl.BlockSpec((1, ROUTER_LANES), const)],
        out_specs=pl.BlockSpec((XG_PLANES, tt, LANES), lambda bi, i: (0, bi * nt + i, 0)),
        out_shape=jax.ShapeDtypeStruct((XG_PLANES, batch * seq, LANES), F32),
        compiler_params=_cparams(("parallel", "parallel")),
        name="xattn_prompt",
    )(x, wq, mem_kv, mem_kv, wo, g, b, wr, br)


def _route(logits):
    lane = lax.broadcasted_iota(jnp.int32, logits.shape, 1)
    big = jnp.int32(ROUTER_LANES)
    ninf = -jnp.inf
    gl = jnp.where(lane < N_GROUPS, logits, ninf)
    gm = jnp.max(gl, -1, keepdims=True)
    g_val = 1.0 / jnp.sum(jnp.exp(gl - gm), -1, keepdims=True)
    g_idx = jnp.min(jnp.where(gl == gm, lane, big), -1, keepdims=True)
    lo = N_GROUPS + EXP_PER_GROUP * g_idx
    el = jnp.where((lane >= lo) & (lane < lo + EXP_PER_GROUP), logits, ninf)
    v1 = jnp.max(el, -1, keepdims=True)
    i1 = jnp.min(jnp.where(el == v1, lane, big), -1, keepdims=True)
    el2 = jnp.where(lane == i1, ninf, el)
    v2 = jnp.max(el2, -1, keepdims=True)
    i2 = jnp.min(jnp.where(el2 == v2, lane, big), -1, keepdims=True)
    e2 = jnp.exp(v2 - v1)
    w1 = g_val / (1.0 + e2)
    w2 = g_val * e2 / (1.0 + e2)
    return jnp.where(lane == i1, w1, 0.0) + jnp.where(lane == i2, w2, 0.0), g_idx


def _moe_dense_kernel(x_ref, wr_ref, br_ref, wg_ref, wu_ref, wd_ref, g_ref, b_ref, o_ref,
                      gate_scr, acc_scr):
    e = pl.program_id(1)

    @pl.when(e == 0)
    def _():
        logits = _bdot(x_ref[...], wr_ref[...]) + br_ref[...]
        gate_scr[...] = _route(logits)[0]
        acc_scr[...] = jnp.zeros_like(acc_scr)

    xb = x_ref[...].astype(BF16)
    lane = lax.broadcasted_iota(jnp.int32, gate_scr.shape, 1)
    gcol = jnp.sum(jnp.where(lane == e + N_GROUPS, gate_scr[...], 0.0), -1, keepdims=True)
    hid = _silu(_bdot(xb, wg_ref[...])) * _bdot(xb, wu_ref[...])
    acc_scr[...] += _bdot(hid * gcol, wd_ref[...])

    @pl.when(e == pl.num_programs(1) - 1)
    def _():
        o_ref[...] = _layer_norm(ALPHA * x_ref[...] + acc_scr[...], g_ref[...], b_ref[...])


def moe_dense(x, wr, br, wg, wu, wd, g, b, layer, tm):
    m = x.shape[0]
    return pl.pallas_call(
        _moe_dense_kernel,
        grid=(m // tm, N_EXPERTS),
        in_specs=[pl.BlockSpec((tm, D_MODEL), lambda i, e: (i, 0)),
                  pl.BlockSpec((D_MODEL, ROUTER_LANES), lambda i, e: (0, 0)),
                  pl.BlockSpec((1, ROUTER_LANES), lambda i, e: (0, 0)),
                  pl.BlockSpec((None, None, D_MODEL, EXP_FF), lambda i, e: (layer, e, 0, 0)),
                  pl.BlockSpec((None, None, D_MODEL, EXP_FF), lambda i, e: (layer, e, 0, 0)),
                  pl.BlockSpec((None, None, EXP_FF, D_MODEL), lambda i, e: (layer, e, 0, 0)),
                  pl.BlockSpec((1, D_MODEL), lambda i, e: (0, 0)),
                  pl.BlockSpec((1, D_MODEL), lambda i, e: (0, 0))],
        out_specs=pl.BlockSpec((tm, D_MODEL), lambda i, e: (i, 0)),
        out_shape=jax.ShapeDtypeStruct((m, D_MODEL), F32),
        scratch_shapes=[pltpu.VMEM((tm, ROUTER_LANES), F32), pltpu.VMEM((tm, D_MODEL), F32)],
        compiler_params=_cparams(("parallel", "arbitrary")),
        name="moe_dense",
    )(x, wr, br, wg, wu, wd, g, b)


def _sc_mesh():
    return plsc.VectorSubcoreMesh(core_axis_name="core", subcore_axis_name="subcore")


def sc_scatter_rows(x, idx, n_out):
    r = x.shape[0]

    @functools.partial(pl.kernel, out_type=jax.ShapeDtypeStruct((n_out, LANES), x.dtype),
                       mesh=_sc_mesh(), scratch_types=[], name="sc_scatter_rows")
    def k(x_hbm, i_hbm, o_hbm):
        def body(x_vmem, i_vmem):
            pltpu.sync_copy(x_vmem, o_hbm.at[i_vmem.at[0]])

        pltpu.emit_pipeline(
            body,
            grid=(r // SC_WINDOW,),
            in_specs=[pl.BlockSpec((SC_WINDOW, LANES), lambda i: (i, 0)),
                      pl.BlockSpec((1, SC_WINDOW), lambda i: (0, i))],
            out_specs=[],
            core_axis_name=("core", "subcore"),
            dimension_semantics=(pltpu.PARALLEL,),
        )(x_hbm, i_hbm)

    return k(x, idx.reshape(1, r))


def sc_gather_rows(table, idx):
    r = idx.shape[0]

    @functools.partial(pl.kernel, out_type=jax.ShapeDtypeStruct((r, LANES), table.dtype),
                       mesh=_sc_mesh(), scratch_types=[], name="sc_gather_rows")
    def k(t_hbm, i_hbm, o_hbm):
        def body(i_vmem, o_vmem):
            pltpu.sync_copy(t_hbm.at[i_vmem.at[0]], o_vmem)

        pltpu.emit_pipeline(
            body,
            grid=(r // SC_WINDOW,),
            in_specs=[pl.BlockSpec((1, SC_WINDOW), lambda i: (0, i))],
            out_specs=[pl.BlockSpec((SC_WINDOW, LANES), lambda i: (i, 0))],
            core_axis_name=("core", "subcore"),
            dimension_semantics=(pltpu.PARALLEL,),
        )(i_hbm, o_hbm)

    return k(table, idx.reshape(1, r))


def _moe_sorted_kernel(tg_ref, nused_ref, xs_ref, wg_ref, wu_ref, wd_ref, g_ref, b_ref, o_ref):
    t = pl.program_id(0)

    @pl.when(t < nused_ref[0])
    def _():
        x = jnp.concatenate([xs_ref[j] for j in range(X_PLANES)], axis=1)
        gate = xs_ref[X_PLANES]
        xb = x.astype(BF16)
        lane = lax.broadcasted_iota(jnp.int32, gate.shape, 1)
        first = N_GROUPS + EXP_PER_GROUP * tg_ref[t]
        acc = jnp.zeros(x.shape, F32)
        for e in range(EXP_PER_GROUP):
            gcol = jnp.sum(jnp.where(lane == first + e, gate, 0.0), -1, keepdims=True)
            hid = _silu(_bdot(xb, wg_ref[e])) * _bdot(xb, wu_ref[e])
            acc = acc + _bdot(hid * gcol, wd_ref[e])
        y = _layer_norm(ALPHA * x + acc, g_ref[...], b_ref[...])
        for j in range(X_PLANES):
            o_ref[j] = y[:, j * LANES:(j + 1) * LANES]


def _group_slots(group_idx, n, tm):
    n_tiles = n // tm + N_GROUPS
    onehot = (group_idx[:, None] == jnp.arange(N_GROUPS)[None, :]).astype(jnp.int32)
    csum = jnp.cumsum(onehot, axis=0)
    counts = csum[-1]
    rank = jnp.sum(onehot * csum, axis=1) - 1
    tiles_g = (counts + tm - 1) // tm
    tile_end = jnp.cumsum(tiles_g)
    slot_base = (tile_end - tiles_g) * tm
    slot = (jnp.sum(onehot * slot_base[None, :], axis=1) + rank).astype(jnp.int32)
    tile_group = jnp.sum((jnp.arange(n_tiles)[:, None] >= tile_end[None, :]).astype(jnp.int32), axis=1)
    tile_group = jnp.minimum(tile_group, N_GROUPS - 1).astype(jnp.int32)
    return slot, tile_group, tile_end[-1:].astype(jnp.int32)


def moe_routed_sc(xg, wg, wu, wd, g, b, layer, n, tm, during_scatter, during_gather):
    slot, tile_group, n_used = _group_slots(xg[X_PLANES, :, 0].astype(jnp.int32), n, tm)
    n_tiles = tile_group.shape[0]
    n_slots = n_tiles * tm
    plane_base = lambda planes: jnp.arange(planes, dtype=jnp.int32)[:, None] * n_slots
    idx_in = (plane_base(XG_PLANES) + slot[None, :]).reshape(-1)
    idx_out = (plane_base(X_PLANES) + slot[None, :]).reshape(-1)
    xs = sc_scatter_rows(xg.reshape(XG_PLANES * n, LANES), idx_in, XG_PLANES * n_slots)
    xs = xs.reshape(XG_PLANES, n_slots, LANES)
    xs, side = lax.optimization_barrier((xs, during_scatter()))
    wspec = lambda shp: pl.BlockSpec((None, None, EXP_PER_GROUP) + shp,
                                     lambda t, tg, nu: (layer, tg[t], 0, 0, 0))
    grid_spec = pltpu.PrefetchScalarGridSpec(
        num_scalar_prefetch=2,
        grid=(n_tiles,),
        in_specs=[pl.BlockSpec((XG_PLANES, tm, LANES), lambda t, tg, nu: (0, t, 0)),
                  wspec((D_MODEL, EXP_FF)), wspec((D_MODEL, EXP_FF)), wspec((EXP_FF, D_MODEL)),
                  pl.BlockSpec((1, D_MODEL), lambda t, tg, nu: (0, 0)),
                  pl.BlockSpec((1, D_MODEL), lambda t, tg, nu: (0, 0))],
        out_specs=pl.BlockSpec((X_PLANES, tm, LANES), lambda t, tg, nu: (0, t, 0)))
    grouped = lambda w: w.reshape(w.shape[0], N_GROUPS, EXP_PER_GROUP, w.shape[2], w.shape[3])
    ys = pl.pallas_call(
        _moe_sorted_kernel,
        grid_spec=grid_spec,
        out_shape=jax.ShapeDtypeStruct((X_PLANES, n_slots, LANES), F32),
        compiler_params=_cparams(("arbitrary",)),
        name="moe_sorted",
    )(tile_group, n_used, xs, grouped(wg), grouped(wu), grouped(wd), g, b)
    y = sc_gather_rows(ys.reshape(X_PLANES * n_slots, LANES), idx_out)
    y, side = lax.optimization_barrier((y, during_gather(side)))
    return y.reshape(X_PLANES, n, LANES), side


def _t5_bucket(dist):
    max_exact = N_BUCKETS // 2
    d = jnp.maximum(dist, 0)
    df = jnp.maximum(d, 1).astype(F32)
    log_b = max_exact + (jnp.log(df / max_exact) / math.log(MAX_DISTANCE / max_exact)
                         * (N_BUCKETS - max_exact)).astype(jnp.int32)
    return jnp.where(d < max_exact, d, jnp.minimum(log_b, N_BUCKETS - 1))


def _bucket_lookup(rel_bias, bucket):
    out = jnp.zeros(bucket.shape + (rel_bias.shape[1],), F32)
    for i in range(N_BUCKETS):
        out = jnp.where((bucket == i)[..., None], rel_bias[i].astype(F32), out)
    return out


def _prompt_bias(rel_bias):
    qi = jnp.arange(WINDOW)[:, None]
    kj = jnp.arange(2 * WINDOW)[None, :]
    dist = qi + WINDOW - kj
    bias = _bucket_lookup(rel_bias, _t5_bucket(dist)).transpose(2, 0, 1)
    valid = (dist >= 0) & (dist <= WINDOW)
    return jnp.where(valid[None], bias, NEG)


def _decode_table(rel_bias, attn_sink):
    dist = WINDOW - jnp.arange(WINDOW + 1)
    bias = _bucket_lookup(rel_bias, _t5_bucket(dist)).T
    depth = attn_sink.shape[0]
    wide = lambda v: jnp.broadcast_to(v[..., None], v.shape + (WINDOW,))
    per_layer = lambda t: jnp.broadcast_to(t[None], (depth,) + t.shape)
    return jnp.stack([per_layer(bias[:, :WINDOW]), per_layer(wide(bias[:, WINDOW])),
                      wide(attn_sink.astype(F32))], axis=1)


def _block_ones(width):
    idx = jnp.arange(width) // HEAD_DIM
    return (idx[:, None] == idx[None, :])


def _head_rows_mask():
    head = jnp.arange(HG_WIDTH)[None, :] // HEAD_DIM
    return (head == jnp.arange(HG_MROWS)[:, None]).astype(F32)


def _block_diag(w):
    nblk, s, _ = w.shape
    eye = jnp.eye(nblk, dtype=w.dtype)
    return (eye[:, None, :, None] * w[:, :, None, :]).reshape(nblk * s, nblk * s)


def kernel(x_prompt, x_sample, mem_prompt, cache_win_k, cache_win_v, state_hgrn, state_conv, state_lru, cache_mem_k, cache_mem_v, rel_bias, hg_lb, w_in, attn_sink, hg_gain, conv_w, conv_b, lru_wa, lru_ba, lru_wx, lru_bx, lru_lam, w_out, ln1_g, ln1_b, x_wq, x_wk, x_wv, x_wo, ln2_g, ln2_b, r_gw, r_gb, r_ew, r_eb, e_wg, e_wu, e_wd, ln3_g, ln3_b):
    bp, seq, d = x_prompt.shape
    n_tok = bp * seq
    nd = x_sample.shape[0]
    depth = w_in.shape[0]

    lbs = jnp.cumsum(jax.nn.softmax(hg_lb.astype(F32), axis=0), axis=0)
    lbs = lbs - lbs[0]
    loglb = jnp.log(lbs)
    log1mlb = jnp.log1p(-lbs)
    gain4 = jnp.tile(hg_gain, (1, HG_HEADS))

    bias_p = _prompt_bias(rel_bias)
    bdm256 = _block_ones(HG_WIDTH).astype(F32)
    hmask = _head_rows_mask()

    w_in_b = w_in.astype(BF16)
    w_out_b = w_out.astype(BF16)
    wq_b = x_wq.astype(BF16)
    wkv_b = jnp.concatenate([x_wk, x_wv], axis=-1).astype(BF16)
    wo_b = x_wo.astype(BF16)
    wg_b = e_wg.astype(BF16)
    wu_b = e_wu.astype(BF16)
    wd_b = e_wd.astype(BF16)
    rew = r_ew.transpose(0, 2, 1, 3).reshape(depth, d, N_EXPERTS)
    wr = jnp.concatenate([r_gw, rew, jnp.zeros((depth, d, ROUTER_LANES - N_GROUPS - N_EXPERTS), F32)], -1)
    br = jnp.concatenate([r_gb, r_eb.reshape(depth, N_EXPERTS),
                          jnp.zeros((depth, ROUTER_LANES - N_GROUPS - N_EXPERTS), F32)], -1)

    a_w = A_HEADS * HEAD_DIM
    xp = x_prompt.reshape(bp * seq, d)
    xs = x_sample.reshape(nd, d)
    mem = mem_prompt.reshape(bp * N_MEM, d)
    ckt = cache_win_k.transpose(0, 1, 3, 4, 2).reshape(depth, nd, LANES, WINDOW)
    cvt = cache_win_v.transpose(0, 1, 3, 4, 2).reshape(depth, nd, LANES, WINDOW)
    cmkt = cache_mem_k.transpose(0, 1, 3, 4, 2).reshape(depth, nd, X_WIDTH, N_MEM)
    cmvt = cache_mem_v.transpose(0, 1, 3, 4, 2).reshape(depth, nd, X_WIDTH, N_MEM)
    state_t = state_hgrn.transpose(0, 2, 3, 4, 1).reshape(depth, HG_HEADS * HEAD_DIM * HEAD_DIM, nd)
    dec_tab = _decode_table(rel_bias, attn_sink)
    xq_mask = _head_rows_mask()[:X_QROWS]
    head_group = jnp.arange(A_HEADS) // A_REP

    p_wk, p_wv, p_s, p_cb, p_h, p_mk, p_mv = [], [], [], [], [], [], []
    s_cb, s_h = [], []
    for l in range(depth):
        row = lambda v: v[l].reshape(1, -1)
        wa_bd = _block_diag(lru_wa[l]).astype(BF16)
        wx_bd = _block_diag(lru_wx[l]).astype(BF16)
        lru_args = (conv_w[l], row(conv_b), wa_bd, row(lru_ba), wx_bd, row(lru_bx), row(lru_lam))
        wo_parts = [w_out_b[l, :a_w], w_out_b[l, a_w:a_w + HG_WIDTH], w_out_b[l, a_w + HG_WIDTH:]]

        proj = matmul(xp, w_in_b[l], 1024, 768)
        oa = attn_prompt(proj, attn_sink[l], bias_p, bp, seq)
        ob, st = hgrn_prompt(proj, row(loglb), row(log1mlb), row(gain4), bdm256, hmask, bp, seq, 256)
        oc, hl = lru_prompt(proj, *lru_args, bp, seq, 256)
        xp = proj_res_ln(xp, [oa, ob, oc], wo_parts, row(ln1_g), row(ln1_b), 1024)
        mkv = matmul(mem, wkv_b[l], 256, 512)
        xg = xattn_prompt(xp, wq_b[l], mkv, wo_b[l], row(ln2_g), row(ln2_b), wr[l], br[l:l + 1],
                          bp, seq, 1024)
        def decode_mixers(xs=xs, ckt=ckt, cvt=cvt, state_t=state_t, l=l, lru_args=lru_args,
                          wo_parts=wo_parts, row=row):
            projd = matmul(xs, w_in_b[l], nd, 768)
            q3 = projd[:, :a_w].reshape(nd, A_HEADS, 1, HEAD_DIM)
            on_group = head_group[None, :, None, None] == jnp.arange(A_KV_HEADS)[None, None, :, None]
            qblk = jnp.where(on_group, q3, 0.0).reshape(nd, A_HEADS, LANES)
            o3, ckt, cvt = attn_decode(qblk, projd, dec_tab[l], ckt, cvt, l, 16)
            o4 = o3.reshape(nd, A_HEADS, A_KV_HEADS, HEAD_DIM)
            oa = jnp.sum(jnp.where(on_group, o4, 0.0), axis=2).reshape(nd, a_w)
            gates_t = projd[:, a_w + 2 * LANES:a_w + 2 * LANES + 4 * HG_WIDTH].T
            bc = lambda v: jnp.broadcast_to(v[:, None], (v.shape[0], nd))
            ob_t, state_t = hgrn_decode(gates_t, bc(loglb[l]), bc(log1mlb[l]), bc(hg_gain[l]), state_t, l)
            oc, nh, nbuf = lru_decode(projd, state_conv[l].reshape(nd, -1), state_lru[l], *lru_args)
            xs = proj_res_ln(xs, [oa, ob_t.T, oc], wo_parts, row(ln1_g), row(ln1_b), nd)
            return xs, ckt, cvt, state_t, nh, nbuf

        def decode_rest(side, l=l, row=row):
            xs = side[0]
            qd = matmul(xs, wq_b[l], nd, X_WIDTH)
            qdb = qd[:, None, :] * xq_mask[None, :, :]
            od = xattn_decode(qdb, cmkt, cmvt, xq_mask, l, 8)
            xs = proj_res_ln(xs, [od], [wo_b[l]], row(ln2_g), row(ln2_b), nd)
            xs = moe_dense(xs, wr[l], br[l:l + 1], wg_b, wu_b, wd_b, row(ln3_g), row(ln3_b), l, nd)
            return (xs,) + tuple(side[1:])

        xp, (xs, ckt, cvt, state_t, nh, nbuf) = moe_routed_sc(
            xg, wg_b, wu_b, wd_b, row(ln3_g), row(ln3_b), l, n_tok, MOE_TM, decode_mixers, decode_rest)

        proj3 = proj.reshape(bp, seq, IN_COLS)
        p_wk.append(proj3[:, seq - WINDOW:, a_w:a_w + LANES].reshape(bp, WINDOW, A_KV_HEADS, HEAD_DIM))
        p_wv.append(proj3[:, seq - WINDOW:, a_w + LANES:a_w + 2 * LANES].reshape(bp, WINDOW, A_KV_HEADS, HEAD_DIM))
        st5 = st.reshape(bp, HG_HEADS, HEAD_DIM, HG_HEADS, HEAD_DIM)
        p_s.append(jnp.stack([st5[:, h, :, h, :] for h in range(HG_HEADS)], 1).transpose(0, 1, 3, 2))
        p_cb.append(proj3[:, seq - (CONV_W - 1):, IN_COLS - 2 * LRU_WIDTH:IN_COLS - LRU_WIDTH])
        p_h.append(hl.reshape(bp, LRU_WIDTH))
        p_mk.append(mkv[:, :X_WIDTH].reshape(bp, N_MEM, X_HEADS, HEAD_DIM))
        p_mv.append(mkv[:, X_WIDTH:].reshape(bp, N_MEM, X_HEADS, HEAD_DIM))

        s_cb.append(nbuf.reshape(nd, CONV_W - 1, LRU_WIDTH))
        s_h.append(nh)

    xp = jnp.transpose(xp, (1, 0, 2))
    unkey = lambda c: c.reshape(depth, nd, A_KV_HEADS, HEAD_DIM, WINDOW).transpose(0, 1, 4, 2, 3)
    s_s = state_t.reshape(depth, HG_HEADS, HEAD_DIM, HEAD_DIM, nd).transpose(0, 4, 1, 2, 3)
    return (xp.reshape(bp, seq, d), xs.reshape(nd, 1, d),
            jnp.stack(p_wk), jnp.stack(p_wv), jnp.stack(p_s), jnp.stack(p_cb), jnp.stack(p_h),
            jnp.stack(p_mk), jnp.stack(p_mv),
            unkey(ckt), unkey(cvt), s_s, jnp.stack(s_cb), jnp.stack(s_h))
```

```python
import functools
import math

import jax
import jax.numpy as jnp
from jax import lax
from jax.experimental import pallas as pl
from jax.experimental.pallas import tpu as pltpu
from jax.experimental.pallas import tpu_sc as plsc

F32 = jnp.float32
BF16 = jnp.bfloat16

D_MODEL = 1024
DEPTH = 4
HEAD_DIM = 64
A_HEADS = 8
A_KV_HEADS = 2
A_REP = A_HEADS // A_KV_HEADS
WINDOW = 128
A_QB = 4
N_BUCKETS = 32
MAX_DISTANCE = 128
HG_WIDTH = 256
HG_HEADS = 4
HG_CHUNK = 64
HG_TB = 16
HG_MROWS = 8
LOG2E = math.log2(math.e)
LRU_WIDTH = 256
LRU_BLOCKS = 4
CONV_W = 4
LRU_C = 8.0
N_MEM = 256
X_HEADS = 4
X_WIDTH = X_HEADS * HEAD_DIM
X_QROWS = 8
N_GROUPS = 4
EXP_PER_GROUP = 4
N_EXPERTS = N_GROUPS * EXP_PER_GROUP
EXP_FF = D_MODEL // 4
ALPHA = (2 * DEPTH) ** 0.25
LN_EPS = 1e-5
RMS_EPS = 1e-6
IN_COLS = 2304
SCALE = HEAD_DIM ** -0.5
NEG = -1e30
LANES = 128
ROUTER_LANES = 128
XG_WIDTH = D_MODEL + ROUTER_LANES
XG_PLANES = XG_WIDTH // LANES
X_PLANES = D_MODEL // LANES
MOE_TM = 512
SC_WINDOW = 128
VMEM_LIMIT = 48 * 1024 * 1024


def _cparams(sem):
    return pltpu.CompilerParams(dimension_semantics=sem, vmem_limit_bytes=VMEM_LIMIT)


def _bdot(a, b):
    return jnp.dot(a.astype(BF16), b.astype(BF16), preferred_element_type=F32)


def _bdot_nt(a, b):
    return lax.dot_general(a.astype(BF16), b.astype(BF16), (((1,), (1,)), ((), ())),
                           preferred_element_type=F32)


def _bdot_tn(a, b):
    return lax.dot_general(a.astype(BF16), b.astype(BF16), (((0,), (0,)), ((), ())),
                           preferred_element_type=F32)


def _rb(x):
    return x.astype(BF16).astype(F32)


def _silu(x):
    return x * jax.nn.sigmoid(x)


def _neg_expm1(x):
    return -jnp.tanh(0.5 * x) * (jnp.exp(x) + 1.0)


def _softplus(x):
    return jnp.maximum(x, 0.0) + jnp.log1p(jnp.exp(-jnp.abs(x)))


def _gelu_tanh(x):
    return 0.5 * x * (1.0 + jnp.tanh(math.sqrt(2.0 / math.pi) * (x + 0.044715 * (x * x * x))))


def _layer_norm(y, g, b):
    mu = jnp.mean(y, -1, keepdims=True)
    yc = y - mu
    var = jnp.mean(yc * yc, -1, keepdims=True)
    return yc * lax.rsqrt(var + LN_EPS) * g + b


def _rows(x_ref):
    if len(x_ref.shape) == 2:
        return x_ref[...]
    return jnp.concatenate([x_ref[j] for j in range(x_ref.shape[0])], axis=1)


def _rows_spec(x, tm, nargs):
    if x.ndim == 2:
        return pl.BlockSpec((tm, x.shape[1]), (lambda i: (i, 0)) if nargs == 1 else (lambda i, j: (i, 0)))
    blk = (x.shape[0], tm, LANES)
    return pl.BlockSpec(blk, (lambda i: (0, i, 0)) if nargs == 1 else (lambda i, j: (0, i, 0)))


def _mm_kernel(x_ref, w_ref, o_ref):
    o_ref[...] = _bdot(_rows(x_ref), w_ref[...])


def matmul(x, w, tm, tn):
    m = x.shape[-2]
    k, n = w.shape
    return pl.pallas_call(
        _mm_kernel,
        grid=(m // tm, n // tn),
        in_specs=[_rows_spec(x, tm, 2),
                  pl.BlockSpec((k, tn), lambda i, j: (0, j))],
        out_specs=pl.BlockSpec((tm, tn), lambda i, j: (i, j)),
        out_shape=jax.ShapeDtypeStruct((m, n), F32),
        compiler_params=_cparams(("parallel", "parallel")),
        name="matmul",
    )(x, w)


def _proj_res_ln_kernel(n_in, x_ref, *refs):
    a_refs = refs[:n_in]
    w_refs = refs[n_in:2 * n_in]
    g_ref, b_ref, o_ref = refs[2 * n_in:]
    y = ALPHA * _rows(x_ref)
    for a_ref, w_ref in zip(a_refs, w_refs):
        y = y + _bdot(a_ref[...], w_ref[...])
    o_ref[...] = _layer_norm(y, g_ref[...], b_ref[...])


def proj_res_ln(x, a_list, w_list, g, b, tm):
    m = x.shape[-2]
    d = w_list[0].shape[1]
    n_in = len(a_list)
    in_specs = [_rows_spec(x, tm, 1)]
    in_specs += [pl.BlockSpec((tm, a.shape[1]), lambda i: (i, 0)) for a in a_list]
    in_specs += [pl.BlockSpec(w.shape, lambda i: (0, 0)) for w in w_list]
    in_specs += [pl.BlockSpec((1, d), lambda i: (0, 0))] * 2
    return pl.pallas_call(
        functools.partial(_proj_res_ln_kernel, n_in),
        grid=(m // tm,),
        in_specs=in_specs,
        out_specs=pl.BlockSpec((tm, d), lambda i: (i, 0)),
        out_shape=jax.ShapeDtypeStruct((m, d), F32),
        compiler_params=_cparams(("parallel",)),
        name="proj_res_ln",
    )(x, *a_list, *w_list, g, b)


def _attn_prompt_kernel(sink_ref, q_ref, kc_ref, kp_ref, vc_ref, vp_ref, bias_ref, o_ref):
    n = pl.program_id(1)
    col = lax.broadcasted_iota(jnp.int32, (WINDOW, 2 * WINDOW), 1)
    first = jnp.where((n == 0) & (col < WINDOW), NEG, 0.0)
    kk = jnp.concatenate([kp_ref[...], kc_ref[...]], axis=0).astype(BF16)
    vv = jnp.concatenate([vp_ref[...], vc_ref[...]], axis=0).astype(BF16)
    q = q_ref[...].astype(BF16)
    for u in range(A_QB):
        rows = slice(u * WINDOW, (u + 1) * WINDOW)
        keys = slice(u * WINDOW, (u + 2) * WINDOW)
        outs = []
        for h in range(A_HEADS):
            g = h // A_REP
            qh = q[rows, h * HEAD_DIM:(h + 1) * HEAD_DIM]
            kg = kk[keys, g * HEAD_DIM:(g + 1) * HEAD_DIM]
            vg = vv[keys, g * HEAD_DIM:(g + 1) * HEAD_DIM]
            s = _bdot_nt(qh, kg) * SCALE + bias_ref[h]
            if u == 0:
                s = s + first
            sink = sink_ref[h]
            m = jnp.maximum(jnp.max(s, -1, keepdims=True), sink)
            p = jnp.exp(s - m)
            den = jnp.sum(p, -1, keepdims=True) + jnp.exp(sink - m)
            outs.append(_bdot(p / den, vg))
        o_ref[rows, :] = jnp.concatenate(outs, axis=1)


def attn_prompt(proj, sink, bias, batch, seq):
    nb = seq // WINDOW
    ns = nb // A_QB
    tq = A_QB * WINDOW
    qcol = 0
    kcol = (A_HEADS * HEAD_DIM) // LANES
    vcol = kcol + 1

    def cur(c):
        return lambda b, n: (b * ns + n, c)

    def prev(c):
        return lambda b, n: (b * nb + jnp.maximum(n * A_QB - 1, 0), c)

    return pl.pallas_call(
        _attn_prompt_kernel,
        grid=(batch, ns),
        in_specs=[pl.BlockSpec(memory_space=pltpu.SMEM),
                  pl.BlockSpec((tq, A_HEADS * HEAD_DIM), cur(qcol)),
                  pl.BlockSpec((tq, LANES), cur(kcol)),
                  pl.BlockSpec((WINDOW, LANES), prev(kcol)),
                  pl.BlockSpec((tq, LANES), cur(vcol)),
                  pl.BlockSpec((WINDOW, LANES), prev(vcol)),
                  pl.BlockSpec((A_HEADS, WINDOW, 2 * WINDOW), lambda b, n: (0, 0, 0))],
        out_specs=pl.BlockSpec((tq, A_HEADS * HEAD_DIM), cur(0)),
        out_shape=jax.ShapeDtypeStruct((batch * seq, A_HEADS * HEAD_DIM), F32),
        compiler_params=_cparams(("parallel", "parallel")),
        name="attn_prompt",
    )(sink, proj, proj, proj, proj, proj, bias)


def _attn_decode_kernel(qb_ref, kn_ref, vn_ref, knt_ref, vnt_ref, ck_ref, cv_ref, tab_ref,
                        o_ref, ok_ref, ov_ref):
    bb = qb_ref.shape[0]
    ck = ck_ref[...]
    cv = cv_ref[...]
    qb = qb_ref[...]
    kn = kn_ref[...]
    vn = vn_ref[...]
    bias_j = tab_ref[0]
    bias_n = tab_ref[1][:, 0:1]
    sink = tab_ref[2][:, 0:1]
    s = lax.dot_general(qb.astype(BF16), ck.astype(BF16), (((2,), (1,)), ((0,), (0,))),
                        preferred_element_type=F32) * SCALE + bias_j[None]
    sn = jnp.sum(_rb(qb) * _rb(kn)[:, None, :], -1, keepdims=True) * SCALE + bias_n[None]
    m = jnp.maximum(jnp.maximum(jnp.max(s, -1, keepdims=True), sn), sink[None])
    p = jnp.exp(s - m)
    pn = jnp.exp(sn - m)
    den = jnp.sum(p, -1, keepdims=True) + pn + jnp.exp(sink[None] - m)
    o = lax.dot_general((p / den).astype(BF16), cv.astype(BF16), (((2,), (2,)), ((0,), (0,))),
                        preferred_element_type=F32)
    o_ref[...] = o + _rb(pn / den) * _rb(vn)[:, None, :]
    lane = lax.broadcasted_iota(jnp.int32, (LANES, LANES), 1)
    for b in range(bb):
        ok_ref[b] = jnp.where(lane == WINDOW - 1, knt_ref[:, b:b + 1], pltpu.roll(ck[b], WINDOW - 1, 1))
        ov_ref[b] = jnp.where(lane == WINDOW - 1, vnt_ref[:, b:b + 1], pltpu.roll(cv[b], WINDOW - 1, 1))


def attn_decode(qblk, proj_d, table, cache_k, cache_v, layer, bb):
    nbatch = proj_d.shape[0]
    a_w = A_HEADS * HEAD_DIM
    cols = lambda c: proj_d[:, c:c + LANES].reshape(nbatch // bb, bb, LANES).transpose(0, 2, 1)
    knt, vnt = cols(a_w), cols(a_w + LANES)
    kcol = (A_HEADS * HEAD_DIM) // LANES
    cache_spec = pl.BlockSpec((None, bb, LANES, WINDOW), lambda i: (layer, i, 0, 0))
    col_spec = pl.BlockSpec((None, LANES, bb), lambda i: (i, 0, 0))
    return pl.pallas_call(
        _attn_decode_kernel,
        grid=(nbatch // bb,),
        in_specs=[pl.BlockSpec((bb, A_HEADS, LANES), lambda i: (i, 0, 0)),
                  pl.BlockSpec((bb, LANES), lambda i: (i, kcol)),
                  pl.BlockSpec((bb, LANES), lambda i: (i, kcol + 1)),
                  col_spec, col_spec, cache_spec, cache_spec,
                  pl.BlockSpec((3, A_HEADS, WINDOW), lambda i: (0, 0, 0))],
        out_specs=[pl.BlockSpec((bb, A_HEADS, LANES), lambda i: (i, 0, 0)), cache_spec, cache_spec],
        out_shape=[jax.ShapeDtypeStruct((nbatch, A_HEADS, LANES), F32),
                   jax.ShapeDtypeStruct(cache_k.shape, F32),
                   jax.ShapeDtypeStruct(cache_v.shape, F32)],
        input_output_aliases={5: 1, 6: 2},
        compiler_params=_cparams(("arbitrary",)),
        name="attn_decode",
    )(qblk, proj_d, proj_d, knt, vnt, cache_k, cache_v, table)


def _xattn_decode_kernel(qb_ref, mk_ref, mv_ref, hm_ref, o_ref):
    qb = qb_ref[...]
    s = lax.dot_general(qb.astype(BF16), mk_ref[...].astype(BF16), (((2,), (1,)), ((0,), (0,))),
                        preferred_element_type=F32) * SCALE
    m = jnp.max(s, -1, keepdims=True)
    p = jnp.exp(s - m)
    p = p / jnp.sum(p, -1, keepdims=True)
    o = lax.dot_general(p.astype(BF16), mv_ref[...].astype(BF16), (((2,), (2,)), ((0,), (0,))),
                        preferred_element_type=F32)
    o_ref[...] = jnp.sum(o * hm_ref[...][None], axis=1)


def xattn_decode(qblk, mem_k, mem_v, hmask, layer, bb):
    nbatch = qblk.shape[0]
    mem_spec = pl.BlockSpec((None, bb, X_WIDTH, N_MEM), lambda i: (layer, i, 0, 0))
    return pl.pallas_call(
        _xattn_decode_kernel,
        grid=(nbatch // bb,),
        in_specs=[pl.BlockSpec((bb, X_QROWS, X_WIDTH), lambda i: (i, 0, 0)), mem_spec, mem_spec,
                  pl.BlockSpec((X_QROWS, X_WIDTH), lambda i: (0, 0))],
        out_specs=pl.BlockSpec((bb, X_WIDTH), lambda i: (i, 0)),
        out_shape=jax.ShapeDtypeStruct((nbatch, X_WIDTH), F32),
        compiler_params=_cparams(("parallel",)),
        name="xattn_decode",
    )(qblk, mem_k, mem_v, hmask)


def _hgrn_gates(hq, hf, loglb, log1mlb):
    ls = jnp.minimum(hf, 0.0) - jnp.log1p(jnp.exp(-jnp.abs(hf)))
    b = log1mlb + ls
    lf = jnp.maximum(loglb, b) + jnp.log1p(jnp.exp(-jnp.abs(loglb - b)))
    return _silu(hq), lf, _neg_expm1(lf)


def _hgrn_prompt_kernel(hq_ref, hf_ref, hi_ref, hg_ref, loglb_ref, log1mlb_ref, gain_ref,
                        bdm_ref, hm_ref, ob_ref, st_ref, st_scr, q_scr, k_scr, cum_scr, o_scr):
    i = pl.program_id(1)
    tt = hq_ref.shape[0]
    c = HG_CHUNK
    tb = HG_TB

    @pl.when(i == 0)
    def _():
        st_scr[...] = jnp.zeros_like(st_scr)

    qs, lf, kk = _hgrn_gates(hq_ref[...], hf_ref[...], loglb_ref[...], log1mlb_ref[...])
    row = lax.broadcasted_iota(jnp.int32, (tt, HG_WIDTH), 0) & (c - 1)
    cum = lf
    sh = 1
    while sh < c:
        cum = cum + jnp.where(row >= sh, pltpu.roll(cum, sh, 0), 0.0)
        sh *= 2
    q_scr[...] = qs
    k_scr[...] = kk
    cum_scr[...] = cum

    bdm = bdm_ref[...]
    hmask = hm_ref[...]
    def chunk(ci, carry):
        r0 = pl.multiple_of(ci * c, c)
        r = pl.ds(r0, c)
        cu = cum_scr[r, :]
        q = q_scr[r, :]
        k = k_scr[r, :]
        v = hi_ref[r, :]
        vb = v.astype(BF16)
        qb = _rb(q)
        cu2 = cu * LOG2E
        last = cu[c - 1:c, :]
        st = st_scr[...]
        o_inter = _bdot_nt(q * jnp.exp(cu), st)
        for j in range(c // tb):
            ns = tb * (j + 1)
            ti = lax.broadcasted_iota(jnp.int32, (tb, ns, HG_WIDTH), 0) + tb * j
            si = lax.broadcasted_iota(jnp.int32, (tb, ns, HG_WIDTH), 1)
            cut = cu2[tb * j:tb * (j + 1), :]
            dec = jnp.exp2(jnp.where(ti >= si, cut[:, None, :] - cu2[None, :ns, :], NEG))
            a2 = (dec * k[None, :ns, :]).astype(BF16)
            q4 = qb[tb * j:tb * (j + 1), None, :] * hmask[None, :, :]
            att = lax.dot_general(q4, a2, (((2,), (2,)), ((0,), (0,))),
                                  preferred_element_type=F32)
            w = jnp.dot(_rb(att.reshape(tb * HG_MROWS, ns)), vb[:ns, :],
                        preferred_element_type=F32).reshape(tb, HG_MROWS, HG_WIDTH)
            o_intra = jnp.sum(w * hmask[None, :, :], axis=1)
            o_scr[pl.ds(r0 + tb * j, tb), :] = o_intra + o_inter[tb * j:tb * (j + 1), :]
        upd = _bdot_tn(v, k * jnp.exp(last - cu))
        st_scr[...] = st * jnp.exp(last) + upd * bdm
        return carry

    lax.fori_loop(0, tt // c, chunk, 0)

    o = o_scr[...]
    ms = jnp.dot(o * o, bdm, precision=lax.Precision.HIGHEST,
                 preferred_element_type=F32) * (1.0 / HEAD_DIM)
    ob_ref[...] = o * lax.rsqrt(ms + RMS_EPS) * gain_ref[...] * _silu(hg_ref[...])

    @pl.when(i == pl.num_programs(1) - 1)
    def _():
        st_ref[...] = st_scr[...]


def hgrn_prompt(proj, loglb, log1mlb, gain4, bdm, hmask, batch, seq, tt):
    nt = seq // tt
    base = (A_HEADS + 2 * A_KV_HEADS) * HEAD_DIM // HG_WIDTH

    def col(cblk):
        return pl.BlockSpec((tt, HG_WIDTH), lambda b, i: (b * nt + i, cblk))

    row_spec = pl.BlockSpec((1, HG_WIDTH), lambda b, i: (0, 0))
    mat_spec = pl.BlockSpec((HG_WIDTH, HG_WIDTH), lambda b, i: (0, 0))
    return pl.pallas_call(
        _hgrn_prompt_kernel,
        grid=(batch, nt),
        in_specs=[col(base), col(base + 1), col(base + 2), col(base + 3),
                  row_spec, row_spec, row_spec, mat_spec,
                  pl.BlockSpec((HG_MROWS, HG_WIDTH), lambda b, i: (0, 0))],
        out_specs=[pl.BlockSpec((tt, HG_WIDTH), lambda b, i: (b * nt + i, 0)),
                   pl.BlockSpec((None, HG_WIDTH, HG_WIDTH), lambda b, i: (b, 0, 0))],
        out_shape=[jax.ShapeDtypeStruct((batch * seq, HG_WIDTH), F32),
                   jax.ShapeDtypeStruct((batch, HG_WIDTH, HG_WIDTH), F32)],
        scratch_shapes=[pltpu.VMEM((HG_WIDTH, HG_WIDTH), F32),
                        pltpu.VMEM((tt, HG_WIDTH), F32),
                        pltpu.VMEM((tt, HG_WIDTH), F32),
                        pltpu.VMEM((tt, HG_WIDTH), F32),
                        pltpu.VMEM((tt, HG_WIDTH), F32)],
        compiler_params=_cparams(("parallel", "arbitrary")),
        name="hgrn_prompt",
    )(proj, proj, proj, proj, loglb, log1mlb, gain4, bdm, hmask)


def _hgrn_decode_kernel(hq_ref, hf_ref, hi_ref, hg_ref, loglb_ref, log1mlb_ref, gain_ref, s_ref,
                        ob_ref, so_ref):
    nb = hq_ref.shape[1]
    qs, lf, kk = _hgrn_gates(hq_ref[...], hf_ref[...], loglb_ref[...], log1mlb_ref[...])
    v = hi_ref[...]
    f = jnp.exp(lf)
    s = s_ref[...].reshape(HEAD_DIM, HEAD_DIM, nb)
    att = jnp.sum(_rb(qs) * _rb(kk), axis=0, keepdims=True)
    o = _rb(att) * _rb(v) + jnp.sum(_rb(qs * f)[:, None, :] * _rb(s), axis=0)
    s_new = f[:, None, :] * s + _rb(kk)[:, None, :] * _rb(v)[None, :, :]
    so_ref[...] = s_new.reshape(HEAD_DIM * HEAD_DIM, nb)
    ms = jnp.mean(o * o, axis=0, keepdims=True)
    ob_ref[...] = o * lax.rsqrt(ms + RMS_EPS) * gain_ref[...] * _silu(hg_ref[...])


def hgrn_decode(gates_t, loglb_t, log1mlb_t, gain_t, state_t, layer):
    nb = gates_t.shape[1]

    def blk(off):
        return pl.BlockSpec((HEAD_DIM, nb), lambda h: (off * HG_HEADS + h, 0))

    par = pl.BlockSpec((HEAD_DIM, nb), lambda h: (h, 0))
    st = pl.BlockSpec((None, HEAD_DIM * HEAD_DIM, nb), lambda h: (layer, h, 0))
    return pl.pallas_call(
        _hgrn_decode_kernel,
        grid=(HG_HEADS,),
        in_specs=[blk(0), blk(1), blk(2), blk(3), par, par,
                  pl.BlockSpec((HEAD_DIM, nb), lambda h: (0, 0)), st],
        out_specs=[par, st],
        out_shape=[jax.ShapeDtypeStruct((HG_WIDTH, nb), F32),
                   jax.ShapeDtypeStruct(state_t.shape, F32)],
        input_output_aliases={7: 1},
        compiler_params=_cparams(("arbitrary",)),
        name="hgrn_decode",
    )(gates_t, gates_t, gates_t, gates_t, loglb_t, log1mlb_t, gain_t, state_t)


def _lru_gates(xc, wa_ref, ba_ref, wx_ref, bx_ref, lam_ref):
    r = jax.nn.sigmoid(_bdot(xc, wa_ref[...]) + ba_ref[...])
    gi = jax.nn.sigmoid(_bdot(xc, wx_ref[...]) + bx_ref[...])
    log_a = -LRU_C * r * _softplus(-lam_ref[...])
    a = jnp.exp(log_a)
    bterm = jnp.sqrt(_neg_expm1(2.0 * log_a)) * (gi * xc)
    return a, bterm


def _lru_prompt_kernel(lx_ref, lg_ref, cw_ref, cb_ref, wa_ref, ba_ref, wx_ref, bx_ref, lam_ref,
                       oc_ref, hl_ref, ext_scr, h_scr):
    i = pl.program_id(1)
    tt = lx_ref.shape[0]
    pad = 8

    @pl.when(i == 0)
    def _():
        ext_scr[0:pad, :] = jnp.zeros((pad, LRU_WIDTH), F32)
        h_scr[...] = jnp.zeros_like(h_scr)

    x = lx_ref[...]
    ext_scr[pad:pad + tt, :] = x
    xc = cb_ref[...] + cw_ref[CONV_W - 1:CONV_W, :] * x
    for j in range(CONV_W - 1):
        back = CONV_W - 1 - j
        xc = xc + cw_ref[j:j + 1, :] * ext_scr[pad - back:pad - back + tt, :]
    ext_scr[0:pad, :] = x[tt - pad:tt, :]

    a, bterm = _lru_gates(xc, wa_ref, ba_ref, wx_ref, bx_ref, lam_ref)
    row = lax.broadcasted_iota(jnp.int32, (tt, LRU_WIDTH), 0)
    sh = 1
    while sh < tt:
        keep = row >= sh
        b_s = jnp.where(keep, pltpu.roll(bterm, sh, 0), 0.0)
        a_s = jnp.where(keep, pltpu.roll(a, sh, 0), 1.0)
        bterm = a * b_s + bterm
        a = a * a_s
        sh *= 2
    h = a * h_scr[...] + bterm
    h_scr[...] = h[tt - 1:tt, :]
    oc_ref[...] = h * _gelu_tanh(lg_ref[...])

    @pl.when(i == pl.num_programs(1) - 1)
    def _():
        hl_ref[...] = h[tt - 1:tt, :]


def lru_prompt(proj, conv_w, conv_b, wa_bd, ba, wx_bd, bx, lam, batch, seq, tt):
    nt = seq // tt
    base = IN_COLS // LRU_WIDTH - 2

    def col(cblk):
        return pl.BlockSpec((tt, LRU_WIDTH), lambda b, i: (b * nt + i, cblk))

    row_spec = pl.BlockSpec((1, LRU_WIDTH), lambda b, i: (0, 0))
    mat_spec = pl.BlockSpec((LRU_WIDTH, LRU_WIDTH), lambda b, i: (0, 0))
    return pl.pallas_call(
        _lru_prompt_kernel,
        grid=(batch, nt),
        in_specs=[col(base), col(base + 1),
                  pl.BlockSpec((CONV_W, LRU_WIDTH), lambda b, i: (0, 0)), row_spec,
                  mat_spec, row_spec, mat_spec, row_spec, row_spec],
        out_specs=[pl.BlockSpec((tt, LRU_WIDTH), lambda b, i: (b * nt + i, 0)),
                   pl.BlockSpec((None, 1, LRU_WIDTH), lambda b, i: (b, 0, 0))],
        out_shape=[jax.ShapeDtypeStruct((batch * seq, LRU_WIDTH), F32),
                   jax.ShapeDtypeStruct((batch, 1, LRU_WIDTH), F32)],
        scratch_shapes=[pltpu.VMEM((tt + 8, LRU_WIDTH), F32),
                        pltpu.VMEM((1, LRU_WIDTH), F32)],
        compiler_params=_cparams(("parallel", "arbitrary")),
        name="lru_prompt",
    )(proj, proj, conv_w, conv_b, wa_bd, ba, wx_bd, bx, lam)


def _lru_decode_kernel(lx_ref, lg_ref, buf_ref, h0_ref, cw_ref, cb_ref, wa_ref, ba_ref, wx_ref,
                       bx_ref, lam_ref, oc_ref, hn_ref, nbuf_ref):
    x = lx_ref[...]
    buf = buf_ref[...]
    xc = cb_ref[...] + cw_ref[CONV_W - 1:CONV_W, :] * x
    for j in range(CONV_W - 1):
        xc = xc + cw_ref[j:j + 1, :] * buf[:, j * LRU_WIDTH:(j + 1) * LRU_WIDTH]
    a, bterm = _lru_gates(xc, wa_ref, ba_ref, wx_ref, bx_ref, lam_ref)
    h = a * h0_ref[...] + bterm
    hn_ref[...] = h
    oc_ref[...] = h * _gelu_tanh(lg_ref[...])
    nbuf_ref[...] = jnp.concatenate([buf[:, LRU_WIDTH:], x], axis=1)


def lru_decode(proj_d, conv_buf, h0, conv_w, conv_b, wa_bd, ba, wx_bd, bx, lam):
    nb = proj_d.shape[0]
    base = IN_COLS // LRU_WIDTH - 2
    row_spec = pl.BlockSpec((1, LRU_WIDTH), lambda i: (0, 0))
    mat_spec = pl.BlockSpec((LRU_WIDTH, LRU_WIDTH), lambda i: (0, 0))
    act = pl.BlockSpec((nb, LRU_WIDTH), lambda i: (0, 0))
    bufs = pl.BlockSpec((nb, (CONV_W - 1) * LRU_WIDTH), lambda i: (0, 0))
    return pl.pallas_call(
        _lru_decode_kernel,
        grid=(1,),
        in_specs=[pl.BlockSpec((nb, LRU_WIDTH), lambda i: (0, base)),
                  pl.BlockSpec((nb, LRU_WIDTH), lambda i: (0, base + 1)),
                  bufs, act, pl.BlockSpec((CONV_W, LRU_WIDTH), lambda i: (0, 0)), row_spec,
                  mat_spec, row_spec, mat_spec, row_spec, row_spec],
        out_specs=[act, act, bufs],
        out_shape=[jax.ShapeDtypeStruct((nb, LRU_WIDTH), F32),
                   jax.ShapeDtypeStruct((nb, LRU_WIDTH), F32),
                   jax.ShapeDtypeStruct((nb, (CONV_W - 1) * LRU_WIDTH), F32)],
        compiler_params=_cparams(("arbitrary",)),
        name="lru_decode",
    )(proj_d, proj_d, conv_buf, h0, conv_w, conv_b, wa_bd, ba, wx_bd, bx, lam)


def _xattn_prompt_kernel(x_ref, wq_ref, mk_ref, mv_ref, wo_ref, g_ref, b_ref, wr_ref, br_ref, o_ref):
    x = x_ref[...]
    q = _bdot(x, wq_ref[...]).astype(BF16)
    mk = mk_ref[...].astype(BF16)
    mv = mv_ref[...].astype(BF16)
    outs = []
    for h in range(X_HEADS):
        sl = slice(h * HEAD_DIM, (h + 1) * HEAD_DIM)
        s = _bdot_nt(q[:, sl], mk[:, sl]) * SCALE
        m = jnp.max(s, -1, keepdims=True)
        p = jnp.exp(s - m)
        p = p / jnp.sum(p, -1, keepdims=True)
        outs.append(_bdot(p, mv[:, sl]))
    o = jnp.concatenate(outs, axis=1)
    y = _layer_norm(ALPHA * x + _bdot(o, wo_ref[...]), g_ref[...], b_ref[...])
    logits = _bdot(y, wr_ref[...]) + br_ref[...]
    gate, g_idx = _route(logits)
    lane = lax.broadcasted_iota(jnp.int32, gate.shape, 1)
    for j in range(D_MODEL // LANES):
        o_ref[j] = y[:, j * LANES:(j + 1) * LANES]
    o_ref[D_MODEL // LANES] = jnp.where(lane == 0, g_idx.astype(F32), gate)


def xattn_prompt(x, wq, mem_kv, wo, g, b, wr, br, batch, seq, tt):
    nt = seq // tt
    const = lambda bi, i: (0, 0)
    return pl.pallas_call(
        _xattn_prompt_kernel,
        grid=(batch, nt),
        in_specs=[pl.BlockSpec((tt, D_MODEL), lambda bi, i: (bi * nt + i, 0)),
                  pl.BlockSpec((D_MODEL, X_WIDTH), const),
                  pl.BlockSpec((N_MEM, X_WIDTH), lambda bi, i: (bi, 0)),
                  pl.BlockSpec((N_MEM, X_WIDTH), lambda bi, i: (bi, 1)),
                  pl.BlockSpec((X_WIDTH, D_MODEL), const),
                  pl.BlockSpec((1, D_MODEL), const),
                  pl.BlockSpec((1, D_MODEL), const),
                  pl.BlockSpec((D_MODEL, ROUTER_LANES), const),
                  pl.BlockSpec((1, ROUTER_LANES), const)],
        out_specs=pl.BlockSpec((XG_PLANES, tt, LANES), lambda bi, i: (0, bi * nt + i, 0)),
        out_shape=jax.ShapeDtypeStruct((XG_PLANES, batch * seq, LANES), F32),
        compiler_params=_cparams(("parallel", "parallel")),
        name="xattn_prompt",
    )(x, wq, mem_kv, mem_kv, wo, g, b, wr, br)


def _route(logits):
    lane = lax.broadcasted_iota(jnp.int32, logits.shape, 1)
    big = jnp.int32(ROUTER_LANES)
    ninf = -jnp.inf
    gl = jnp.where(lane < N_GROUPS, logits, ninf)
    gm = jnp.max(gl, -1, keepdims=True)
    g_val = 1.0 / jnp.sum(jnp.exp(gl - gm), -1, keepdims=True)
    g_idx = jnp.min(jnp.where(gl == gm, lane, big), -1, keepdims=True)
    lo = N_GROUPS + EXP_PER_GROUP * g_idx
    el = jnp.where((lane >= lo) & (lane < lo + EXP_PER_GROUP), logits, ninf)
    v1 = jnp.max(el, -1, keepdims=True)
    i1 = jnp.min(jnp.where(el == v1, lane, big), -1, keepdims=True)
    el2 = jnp.where(lane == i1, ninf, el)
    v2 = jnp.max(el2, -1, keepdims=True)
    i2 = jnp.min(jnp.where(el2 == v2, lane, big), -1, keepdims=True)
    e2 = jnp.exp(v2 - v1)
    w1 = g_val / (1.0 + e2)
    w2 = g_val * e2 / (1.0 + e2)
    return jnp.where(lane == i1, w1, 0.0) + jnp.where(lane == i2, w2, 0.0), g_idx


def _moe_dense_kernel(x_ref, wr_ref, br_ref, wg_ref, wu_ref, wd_ref, g_ref, b_ref, o_ref,
                      gate_scr, acc_scr):
    e = pl.program_id(1)

    @pl.when(e == 0)
    def _():
        logits = _bdot(x_ref[...], wr_ref[...]) + br_ref[...]
        gate_scr[...] = _route(logits)[0]
        acc_scr[...] = jnp.zeros_like(acc_scr)

    xb = x_ref[...].astype(BF16)
    lane = lax.broadcasted_iota(jnp.int32, gate_scr.shape, 1)
    gcol = jnp.sum(jnp.where(lane == e + N_GROUPS, gate_scr[...], 0.0), -1, keepdims=True)
    hid = _silu(_bdot(xb, wg_ref[...])) * _bdot(xb, wu_ref[...])
    acc_scr[...] += _bdot(hid * gcol, wd_ref[...])

    @pl.when(e == pl.num_programs(1) - 1)
    def _():
        o_ref[...] = _layer_norm(ALPHA * x_ref[...] + acc_scr[...], g_ref[...], b_ref[...])


def moe_dense(x, wr, br, wg, wu, wd, g, b, layer, tm):
    m = x.shape[0]
    return pl.pallas_call(
        _moe_dense_kernel,
        grid=(m // tm, N_EXPERTS),
        in_specs=[pl.BlockSpec((tm, D_MODEL), lambda i, e: (i, 0)),
                  pl.BlockSpec((D_MODEL, ROUTER_LANES), lambda i, e: (0, 0)),
                  pl.BlockSpec((1, ROUTER_LANES), lambda i, e: (0, 0)),
                  pl.BlockSpec((None, None, D_MODEL, EXP_FF), lambda i, e: (layer, e, 0, 0)),
                  pl.BlockSpec((None, None, D_MODEL, EXP_FF), lambda i, e: (layer, e, 0, 0)),
                  pl.BlockSpec((None, None, EXP_FF, D_MODEL), lambda i, e: (layer, e, 0, 0)),
                  pl.BlockSpec((1, D_MODEL), lambda i, e: (0, 0)),
                  pl.BlockSpec((1, D_MODEL), lambda i, e: (0, 0))],
        out_specs=pl.BlockSpec((tm, D_MODEL), lambda i, e: (i, 0)),
        out_shape=jax.ShapeDtypeStruct((m, D_MODEL), F32),
        scratch_shapes=[pltpu.VMEM((tm, ROUTER_LANES), F32), pltpu.VMEM((tm, D_MODEL), F32)],
        compiler_params=_cparams(("parallel", "arbitrary")),
        name="moe_dense",
    )(x, wr, br, wg, wu, wd, g, b)


def _sc_mesh():
    return plsc.VectorSubcoreMesh(core_axis_name="core", subcore_axis_name="subcore")


def sc_scatter_rows(x, idx, n_out):
    r = x.shape[0]

    @functools.partial(pl.kernel, out_type=jax.ShapeDtypeStruct((n_out, LANES), x.dtype),
                       mesh=_sc_mesh(), scratch_types=[], name="sc_scatter_rows")
    def k(x_hbm, i_hbm, o_hbm):
        def body(x_vmem, i_vmem):
            pltpu.sync_copy(x_vmem, o_hbm.at[i_vmem.at[0]])

        pltpu.emit_pipeline(
            body,
            grid=(r // SC_WINDOW,),
            in_specs=[pl.BlockSpec((SC_WINDOW, LANES), lambda i: (i, 0)),
                      pl.BlockSpec((1, SC_WINDOW), lambda i: (0, i))],
            out_specs=[],
            core_axis_name=("core", "subcore"),
            dimension_semantics=(pltpu.PARALLEL,),
        )(x_hbm, i_hbm)

    return k(x, idx.reshape(1, r))


def sc_gather_rows(table, idx):
    r = idx.shape[0]

    @functools.partial(pl.kernel, out_type=jax.ShapeDtypeStruct((r, LANES), table.dtype),
                       mesh=_sc_mesh(), scratch_types=[], name="sc_gather_rows")
    def k(t_hbm, i_hbm, o_hbm):
        def body(i_vmem, o_vmem):
            pltpu.sync_copy(t_hbm.at[i_vmem.at[0]], o_vmem)

        pltpu.emit_pipeline(
            body,
            grid=(r // SC_WINDOW,),
            in_specs=[pl.BlockSpec((1, SC_WINDOW), lambda i: (0, i))],
            out_specs=[pl.BlockSpec((SC_WINDOW, LANES), lambda i: (i, 0))],
            core_axis_name=("core", "subcore"),
            dimension_semantics=(pltpu.PARALLEL,),
        )(i_hbm, o_hbm)

    return k(table, idx.reshape(1, r))


def _moe_sorted_kernel(tg_ref, nused_ref, xs_ref, wg_ref, wu_ref, wd_ref, g_ref, b_ref, o_ref):
    t = pl.program_id(0)

    @pl.when(t < nused_ref[0])
    def _():
        x = jnp.concatenate([xs_ref[j] for j in range(X_PLANES)], axis=1)
        gate = xs_ref[X_PLANES]
        xb = x.astype(BF16)
        lane = lax.broadcasted_iota(jnp.int32, gate.shape, 1)
        first = N_GROUPS + EXP_PER_GROUP * tg_ref[t]
        acc = jnp.zeros(x.shape, F32)
        for e in range(EXP_PER_GROUP):
            gcol = jnp.sum(jnp.where(lane == first + e, gate, 0.0), -1, keepdims=True)
            hid = _silu(_bdot(xb, wg_ref[e])) * _bdot(xb, wu_ref[e])
            acc = acc + _bdot(hid * gcol, wd_ref[e])
        y = _layer_norm(ALPHA * x + acc, g_ref[...], b_ref[...])
        for j in range(X_PLANES):
            o_ref[j] = y[:, j * LANES:(j + 1) * LANES]


def _group_slots(group_idx, n, tm):
    n_tiles = n // tm + N_GROUPS
    onehot = (group_idx[:, None] == jnp.arange(N_GROUPS)[None, :]).astype(jnp.int32)
    csum = jnp.cumsum(onehot, axis=0)
    counts = csum[-1]
    rank = jnp.sum(onehot * csum, axis=1) - 1
    tiles_g = (counts + tm - 1) // tm
    tile_end = jnp.cumsum(tiles_g)
    slot_base = (tile_end - tiles_g) * tm
    slot = (jnp.sum(onehot * slot_base[None, :], axis=1) + rank).astype(jnp.int32)
    tile_group = jnp.sum((jnp.arange(n_tiles)[:, None] >= tile_end[None, :]).astype(jnp.int32), axis=1)
    tile_group = jnp.minimum(tile_group, N_GROUPS - 1).astype(jnp.int32)
    return slot, tile_group, tile_end[-1:].astype(jnp.int32)


def moe_routed_sc(xg, wg, wu, wd, g, b, layer, n, tm, during_scatter, during_gather):
    slot, tile_group, n_used = _group_slots(xg[X_PLANES, :, 0].astype(jnp.int32), n, tm)
    n_tiles = tile_group.shape[0]
    n_slots = n_tiles * tm
    plane_base = lambda planes: jnp.arange(planes, dtype=jnp.int32)[:, None] * n_slots
    idx_in = (plane_base(XG_PLANES) + slot[None, :]).reshape(-1)
    idx_out = (plane_base(X_PLANES) + slot[None, :]).reshape(-1)
    xs = sc_scatter_rows(xg.reshape(XG_PLANES * n, LANES), idx_in, XG_PLANES * n_slots)
    xs = xs.reshape(XG_PLANES, n_slots, LANES)
    xs, side = lax.optimization_barrier((xs, during_scatter()))
    wspec = lambda shp: pl.BlockSpec((None, None, EXP_PER_GROUP) + shp,
                                     lambda t, tg, nu: (layer, tg[t], 0, 0, 0))
    grid_spec = pltpu.PrefetchScalarGridSpec(
        num_scalar_prefetch=2,
        grid=(n_tiles,),
        in_specs=[pl.BlockSpec((XG_PLANES, tm, LANES), lambda t, tg, nu: (0, t, 0)),
                  wspec((D_MODEL, EXP_FF)), wspec((D_MODEL, EXP_FF)), wspec((EXP_FF, D_MODEL)),
                  pl.BlockSpec((1, D_MODEL), lambda t, tg, nu: (0, 0)),
                  pl.BlockSpec((1, D_MODEL), lambda t, tg, nu: (0, 0))],
        out_specs=pl.BlockSpec((X_PLANES, tm, LANES), lambda t, tg, nu: (0, t, 0)))
    grouped = lambda w: w.reshape(w.shape[0], N_GROUPS, EXP_PER_GROUP, w.shape[2], w.shape[3])
    ys = pl.pallas_call(
        _moe_sorted_kernel,
        grid_spec=grid_spec,
        out_shape=jax.ShapeDtypeStruct((X_PLANES, n_slots, LANES), F32),
        compiler_params=_cparams(("arbitrary",)),
        name="moe_sorted",
    )(tile_group, n_used, xs, grouped(wg), grouped(wu), grouped(wd), g, b)
    y = sc_gather_rows(ys.reshape(X_PLANES * n_slots, LANES), idx_out)
    y, side = lax.optimization_barrier((y, during_gather(side)))
    return y.reshape(X_PLANES, n, LANES), side


def _t5_bucket(dist):
    max_exact = N_BUCKETS // 2
    d = jnp.maximum(dist, 0)
    df = jnp.maximum(d, 1).astype(F32)
    log_b = max_exact + (jnp.log(df / max_exact) / math.log(MAX_DISTANCE / max_exact)
                         * (N_BUCKETS - max_exact)).astype(jnp.int32)
    return jnp.where(d < max_exact, d, jnp.minimum(log_b, N_BUCKETS - 1))


def _bucket_lookup(rel_bias, bucket):
    out = jnp.zeros(bucket.shape + (rel_bias.shape[1],), F32)
    for i in range(N_BUCKETS):
        out = jnp.where((bucket == i)[..., None], rel_bias[i].astype(F32), out)
    return out


def _prompt_bias(rel_bias):
    qi = jnp.arange(WINDOW)[:, None]
    kj = jnp.arange(2 * WINDOW)[None, :]
    dist = qi + WINDOW - kj
    bias = _bucket_lookup(rel_bias, _t5_bucket(dist)).transpose(2, 0, 1)
    valid = (dist >= 0) & (dist <= WINDOW)
    return jnp.where(valid[None], bias, NEG)


def _decode_table(rel_bias, attn_sink):
    dist = WINDOW - jnp.arange(WINDOW + 1)
    bias = _bucket_lookup(rel_bias, _t5_bucket(dist)).T
    depth = attn_sink.shape[0]
    wide = lambda v: jnp.broadcast_to(v[..., None], v.shape + (WINDOW,))
    per_layer = lambda t: jnp.broadcast_to(t[None], (depth,) + t.shape)
    return jnp.stack([per_layer(bias[:, :WINDOW]), per_layer(wide(bias[:, WINDOW])),
                      wide(attn_sink.astype(F32))], axis=1)


def _block_ones(width):
    idx = jnp.arange(width) // HEAD_DIM
    return (idx[:, None] == idx[None, :])


def _head_rows_mask():
    head = jnp.arange(HG_WIDTH)[None, :] // HEAD_DIM
    return (head == jnp.arange(HG_MROWS)[:, None]).astype(F32)


def _block_diag(w):
    nblk, s, _ = w.shape
    eye = jnp.eye(nblk, dtype=w.dtype)
    return (eye[:, None, :, None] * w[:, :, None, :]).reshape(nblk * s, nblk * s)


def kernel(x_prompt, x_sample, mem_prompt, cache_win_k, cache_win_v, state_hgrn, state_conv, state_lru, cache_mem_k, cache_mem_v, rel_bias, hg_lb, w_in, attn_sink, hg_gain, conv_w, conv_b, lru_wa, lru_ba, lru_wx, lru_bx, lru_lam, w_out, ln1_g, ln1_b, x_wq, x_wk, x_wv, x_wo, ln2_g, ln2_b, r_gw, r_gb, r_ew, r_eb, e_wg, e_wu, e_wd, ln3_g, ln3_b):
    bp, seq, d = x_prompt.shape
    n_tok = bp * seq
    nd = x_sample.shape[0]
    depth = w_in.shape[0]

    lbs = jnp.cumsum(jax.nn.softmax(hg_lb.astype(F32), axis=0), axis=0)
    lbs = lbs - lbs[0]
    loglb = jnp.log(lbs)
    log1mlb = jnp.log1p(-lbs)
    gain4 = jnp.tile(hg_gain, (1, HG_HEADS))

    bias_p = _prompt_bias(rel_bias)
    bdm256 = _block_ones(HG_WIDTH).astype(F32)
    hmask = _head_rows_mask()

    w_in_b = w_in.astype(BF16)
    w_out_b = w_out.astype(BF16)
    wq_b = x_wq.astype(BF16)
    wkv_b = jnp.concatenate([x_wk, x_wv], axis=-1).astype(BF16)
    wo_b = x_wo.astype(BF16)
    rew = r_ew.transpose(0, 2, 1, 3).reshape(depth, d, N_EXPERTS)
    wr = jnp.concatenate([r_gw, rew, jnp.zeros((depth, d, ROUTER_LANES - N_GROUPS - N_EXPERTS), F32)], -1)
    br = jnp.concatenate([r_gb, r_eb.reshape(depth, N_EXPERTS),
                          jnp.zeros((depth, ROUTER_LANES - N_GROUPS - N_EXPERTS), F32)], -1)

    a_w = A_HEADS * HEAD_DIM
    xp = x_prompt.reshape(bp * seq, d)
    xs = x_sample.reshape(nd, d)
    mem = mem_prompt.reshape(bp * N_MEM, d)
    ckt = cache_win_k.transpose(0, 1, 3, 4, 2).reshape(depth, nd, LANES, WINDOW)
    cvt = cache_win_v.transpose(0, 1, 3, 4, 2).reshape(depth, nd, LANES, WINDOW)
    cmkt = cache_mem_k.transpose(0, 1, 3, 4, 2).reshape(depth, nd, X_WIDTH, N_MEM)
    cmvt = cache_mem_v.transpose(0, 1, 3, 4, 2).reshape(depth, nd, X_WIDTH, N_MEM)
    state_t = state_hgrn.transpose(0, 2, 3, 4, 1).reshape(depth, HG_HEADS * HEAD_DIM * HEAD_DIM, nd)
    dec_tab = _decode_table(rel_bias, attn_sink)
    xq_mask = _head_rows_mask()[:X_QROWS]
    head_group = jnp.arange(A_HEADS) // A_REP

    p_wk, p_wv, p_s, p_cb, p_h, p_mk, p_mv = [], [], [], [], [], [], []
    s_cb, s_h = [], []
    for l in range(depth):
        row = lambda v: v[l].reshape(1, -1)
        wa_bd = _block_diag(lru_wa[l]).astype(BF16)
        wx_bd = _block_diag(lru_wx[l]).astype(BF16)
        lru_args = (conv_w[l], row(conv_b), wa_bd, row(lru_ba), wx_bd, row(lru_bx), row(lru_lam))
        wo_parts = [w_out_b[l, :a_w], w_out_b[l, a_w:a_w + HG_WIDTH], w_out_b[l, a_w + HG_WIDTH:]]

        proj = matmul(xp, w_in_b[l], 1024, 768)
        oa = attn_prompt(proj, attn_sink[l], bias_p, bp, seq)
        ob, st = hgrn_prompt(proj, row(loglb), row(log1mlb), row(gain4), bdm256, hmask, bp, seq, 256)
        oc, hl = lru_prompt(proj, *lru_args, bp, seq, 256)
        xp = proj_res_ln(xp, [oa, ob, oc], wo_parts, row(ln1_g), row(ln1_b), 1024)
        mkv = matmul(mem, wkv_b[l], 256, 512)
        xg = xattn_prompt(xp, wq_b[l], mkv, wo_b[l], row(ln2_g), row(ln2_b), wr[l], br[l:l + 1],
                          bp, seq, 1024)
        def decode_mixers(xs=xs, ckt=ckt, cvt=cvt, state_t=state_t, l=l, lru_args=lru_args,
                          wo_parts=wo_parts, row=row):
            projd = matmul(xs, w_in_b[l], nd, 768)
            q3 = projd[:, :a_w].reshape(nd, A_HEADS, 1, HEAD_DIM)
            on_group = head_group[None, :, None, None] == jnp.arange(A_KV_HEADS)[None, None, :, None]
            qblk = jnp.where(on_group, q3, 0.0).reshape(nd, A_HEADS, LANES)
            o3, ckt, cvt = attn_decode(qblk, projd, dec_tab[l], ckt, cvt, l, 16)
            o4 = o3.reshape(nd, A_HEADS, A_KV_HEADS, HEAD_DIM)
            oa = jnp.sum(jnp.where(on_group, o4, 0.0), axis=2).reshape(nd, a_w)
            gates_t = projd[:, a_w + 2 * LANES:a_w + 2 * LANES + 4 * HG_WIDTH].T
            bc = lambda v: jnp.broadcast_to(v[:, None], (v.shape[0], nd))
            ob_t, state_t = hgrn_decode(gates_t, bc(loglb[l]), bc(log1mlb[l]), bc(hg_gain[l]), state_t, l)
            oc, nh, nbuf = lru_decode(projd, state_conv[l].reshape(nd, -1), state_lru[l], *lru_args)
            xs = proj_res_ln(xs, [oa, ob_t.T, oc], wo_parts, row(ln1_g), row(ln1_b), nd)
            return xs, ckt, cvt, state_t, nh, nbuf

        def decode_rest(side, l=l, row=row):
            xs = side[0]
            qd = matmul(xs, wq_b[l], nd, X_WIDTH)
            qdb = qd[:, None, :] * xq_mask[None, :, :]
            od = xattn_decode(qdb, cmkt, cmvt, xq_mask, l, 8)
            xs = proj_res_ln(xs, [od], [wo_b[l]], row(ln2_g), row(ln2_b), nd)
            xs = moe_dense(xs, wr[l], br[l:l + 1], e_wg, e_wu, e_wd, row(ln3_g), row(ln3_b), l, nd)
            return (xs,) + tuple(side[1:])

        xp, (xs, ckt, cvt, state_t, nh, nbuf) = moe_routed_sc(
            xg, e_wg, e_wu, e_wd, row(ln3_g), row(ln3_b), l, n_tok, MOE_TM, decode_mixers, decode_rest)

        proj3 = proj.reshape(bp, seq, IN_COLS)
        p_wk.append(proj3[:, seq - WINDOW:, a_w:a_w + LANES].reshape(bp, WINDOW, A_KV_HEADS, HEAD_DIM))
        p_wv.append(proj3[:, seq - WINDOW:, a_w + LANES:a_w + 2 * LANES].reshape(bp, WINDOW, A_KV_HEADS, HEAD_DIM))
        st5 = st.reshape(bp, HG_HEADS, HEAD_DIM, HG_HEADS, HEAD_DIM)
        p_s.append(jnp.stack([st5[:, h, :, h, :] for h in range(HG_HEADS)], 1).transpose(0, 1, 3, 2))
        p_cb.append(proj3[:, seq - (CONV_W - 1):, IN_COLS - 2 * LRU_WIDTH:IN_COLS - LRU_WIDTH])
        p_h.append(hl.reshape(bp, LRU_WIDTH))
        p_mk.append(mkv[:, :X_WIDTH].reshape(bp, N_MEM, X_HEADS, HEAD_DIM))
        p_mv.append(mkv[:, X_WIDTH:].reshape(bp, N_MEM, X_HEADS, HEAD_DIM))

        s_cb.append(nbuf.reshape(nd, CONV_W - 1, LRU_WIDTH))
        s_h.append(nh)

    xp = jnp.transpose(xp, (1, 0, 2))
    unkey = lambda c: c.reshape(depth, nd, A_KV_HEADS, HEAD_DIM, WINDOW).transpose(0, 1, 4, 2, 3)
    s_s = state_t.reshape(depth, HG_HEADS, HEAD_DIM, HEAD_DIM, nd).transpose(0, 4, 1, 2, 3)
    return (xp.reshape(bp, seq, d), xs.reshape(nd, 1, d),
            jnp.stack(p_wk), jnp.stack(p_wv), jnp.stack(p_s), jnp.stack(p_cb), jnp.stack(p_h),
            jnp.stack(p_mk), jnp.stack(p_mv),
            unkey(ckt), unkey(cvt), s_s, jnp.stack(s_cb), jnp.stack(s_h))
```

```python
import functools
import math

import jax
import jax.numpy as jnp
from jax import lax
from jax.experimental import pallas as pl
from jax.experimental.pallas import tpu as pltpu
from jax.experimental.pallas import tpu_sc as plsc

F32 = jnp.float32
BF16 = jnp.bfloat16

D_MODEL = 1024
DEPTH = 4
HEAD_DIM = 64
A_HEADS = 8
A_KV_HEADS = 2
A_REP = A_HEADS // A_KV_HEADS
WINDOW = 128
A_QB = 4
N_BUCKETS = 32
MAX_DISTANCE = 128
HG_WIDTH = 256
HG_HEADS = 4
HG_CHUNK = 64
HG_TB = 16
HG_MROWS = 8
LOG2E = math.log2(math.e)
LRU_WIDTH = 256
LRU_BLOCKS = 4
CONV_W = 4
LRU_C = 8.0
N_MEM = 256
X_HEADS = 4
X_WIDTH = X_HEADS * HEAD_DIM
X_QROWS = 8
N_GROUPS = 4
EXP_PER_GROUP = 4
N_EXPERTS = N_GROUPS * EXP_PER_GROUP
EXP_FF = D_MODEL // 4
ALPHA = (2 * DEPTH) ** 0.25
LN_EPS = 1e-5
RMS_EPS = 1e-6
IN_COLS = 2304
SCALE = HEAD_DIM ** -0.5
NEG = -1e30
LANES = 128
ROUTER_LANES = 128
XG_WIDTH = D_MODEL + ROUTER_LANES
XG_PLANES = XG_WIDTH // LANES
X_PLANES = D_MODEL // LANES
MOE_TM = 512
SC_WINDOW = 128
VMEM_LIMIT = 48 * 1024 * 1024


def _cparams(sem):
    return pltpu.CompilerParams(dimension_semantics=sem, vmem_limit_bytes=VMEM_LIMIT)


def _bdot(a, b):
    return jnp.dot(a.astype(BF16), b.astype(BF16), preferred_element_type=F32)


def _bdot_nt(a, b):
    return lax.dot_general(a.astype(BF16), b.astype(BF16), (((1,), (1,)), ((), ())),
                           preferred_element_type=F32)


def _bdot_tn(a, b):
    return lax.dot_general(a.astype(BF16), b.astype(BF16), (((0,), (0,)), ((), ())),
                           preferred_element_type=F32)


def _rb(x):
    return x.astype(BF16).astype(F32)


def _silu(x):
    return x * jax.nn.sigmoid(x)


def _neg_expm1(x):
    return -jnp.tanh(0.5 * x) * (jnp.exp(x) + 1.0)


def _softplus(x):
    return jnp.maximum(x, 0.0) + jnp.log1p(jnp.exp(-jnp.abs(x)))


def _gelu_tanh(x):
    return 0.5 * x * (1.0 + jnp.tanh(math.sqrt(2.0 / math.pi) * (x + 0.044715 * (x * x * x))))


def _layer_norm(y, g, b):
    mu = jnp.mean(y, -1, keepdims=True)
    yc = y - mu
    var = jnp.mean(yc * yc, -1, keepdims=True)
    return yc * lax.rsqrt(var + LN_EPS) * g + b


def _rows(x_ref):
    if len(x_ref.shape) == 2:
        return x_ref[...]
    return jnp.concatenate([x_ref[j] for j in range(x_ref.shape[0])], axis=1)


def _rows_spec(x, tm, nargs):
    if x.ndim == 2:
        return pl.BlockSpec((tm, x.shape[1]), (lambda i: (i, 0)) if nargs == 1 else (lambda i, j: (i, 0)))
    blk = (x.shape[0], tm, LANES)
    return pl.BlockSpec(blk, (lambda i: (0, i, 0)) if nargs == 1 else (lambda i, j: (0, i, 0)))


def _mm_kernel(x_ref, w_ref, o_ref):
    o_ref[...] = _bdot(_rows(x_ref), w_ref[...])


def matmul(x, w, tm, tn):
    m = x.shape[-2]
    k, n = w.shape
    return pl.pallas_call(
        _mm_kernel,
        grid=(m // tm, n // tn),
        in_specs=[_rows_spec(x, tm, 2),
                  pl.BlockSpec((k, tn), lambda i, j: (0, j))],
        out_specs=pl.BlockSpec((tm, tn), lambda i, j: (i, j)),
        out_shape=jax.ShapeDtypeStruct((m, n), F32),
        compiler_params=_cparams(("parallel", "parallel")),
        name="matmul",
    )(x, w)


def _proj_res_ln_kernel(n_in, x_ref, *refs):
    a_refs = refs[:n_in]
    w_refs = refs[n_in:2 * n_in]
    g_ref, b_ref, o_ref = refs[2 * n_in:]
    y = ALPHA * _rows(x_ref)
    for a_ref, w_ref in zip(a_refs, w_refs):
        y = y + _bdot(a_ref[...], w_ref[...])
    o_ref[...] = _layer_norm(y, g_ref[...], b_ref[...])


def proj_res_ln(x, a_list, w_list, g, b, tm):
    m = x.shape[-2]
    d = w_list[0].shape[1]
    n_in = len(a_list)
    in_specs = [_rows_spec(x, tm, 1)]
    in_specs += [pl.BlockSpec((tm, a.shape[1]), lambda i: (i, 0)) for a in a_list]
    in_specs += [pl.BlockSpec(w.shape, lambda i: (0, 0)) for w in w_list]
    in_specs += [pl.BlockSpec((1, d), lambda i: (0, 0))] * 2
    return pl.pallas_call(
        functools.partial(_proj_res_ln_kernel, n_in),
        grid=(m // tm,),
        in_specs=in_specs,
        out_specs=pl.BlockSpec((tm, d), lambda i: (i, 0)),
        out_shape=jax.ShapeDtypeStruct((m, d), F32),
        compiler_params=_cparams(("parallel",)),
        name="proj_res_ln",
    )(x, *a_list, *w_list, g, b)


def _attn_prompt_kernel(sink_ref, q_ref, kc_ref, kp_ref, vc_ref, vp_ref, bias_ref, o_ref):
    n = pl.program_id(1)
    col = lax.broadcasted_iota(jnp.int32, (WINDOW, 2 * WINDOW), 1)
    first = jnp.where((n == 0) & (col < WINDOW), NEG, 0.0)
    kk = jnp.concatenate([kp_ref[...], kc_ref[...]], axis=0).astype(BF16)
    vv = jnp.concatenate([vp_ref[...], vc_ref[...]], axis=0).astype(BF16)
    q = q_ref[...].astype(BF16)
    for u in range(A_QB):
        rows = slice(u * WINDOW, (u + 1) * WINDOW)
        keys = slice(u * WINDOW, (u + 2) * WINDOW)
        outs = []
        for h in range(A_HEADS):
            g = h // A_REP
            qh = q[rows, h * HEAD_DIM:(h + 1) * HEAD_DIM]
            kg = kk[keys, g * HEAD_DIM:(g + 1) * HEAD_DIM]
            vg = vv[keys, g * HEAD_DIM:(g + 1) * HEAD_DIM]
            s = _bdot_nt(qh, kg) * SCALE + bias_ref[h]
            if u == 0:
                s = s + first
            sink = sink_ref[h]
            m = jnp.maximum(jnp.max(s, -1, keepdims=True), sink)
            p = jnp.exp(s - m)
            den = jnp.sum(p, -1, keepdims=True) + jnp.exp(sink - m)
            outs.append(_bdot(p / den, vg))
        o_ref[rows, :] = jnp.concatenate(outs, axis=1)


def attn_prompt(proj, sink, bias, batch, seq):
    nb = seq // WINDOW
    ns = nb // A_QB
    tq = A_QB * WINDOW
    qcol = 0
    kcol = (A_HEADS * HEAD_DIM) // LANES
    vcol = kcol + 1

    def cur(c):
        return lambda b, n: (b * ns + n, c)

    def prev(c):
        return lambda b, n: (b * nb + jnp.maximum(n * A_QB - 1, 0), c)

    return pl.pallas_call(
        _attn_prompt_kernel,
        grid=(batch, ns),
        in_specs=[pl.BlockSpec(memory_space=pltpu.SMEM),
                  pl.BlockSpec((tq, A_HEADS * HEAD_DIM), cur(qcol)),
                  pl.BlockSpec((tq, LANES), cur(kcol)),
                  pl.BlockSpec((WINDOW, LANES), prev(kcol)),
                  pl.BlockSpec((tq, LANES), cur(vcol)),
                  pl.BlockSpec((WINDOW, LANES), prev(vcol)),
                  pl.BlockSpec((A_HEADS, WINDOW, 2 * WINDOW), lambda b, n: (0, 0, 0))],
        out_specs=pl.BlockSpec((tq, A_HEADS * HEAD_DIM), cur(0)),
        out_shape=jax.ShapeDtypeStruct((batch * seq, A_HEADS * HEAD_DIM), F32),
        compiler_params=_cparams(("parallel", "parallel")),
        name="attn_prompt",
    )(sink, proj, proj, proj, proj, proj, bias)


def _attn_decode_kernel(qb_ref, kn_ref, vn_ref, knt_ref, vnt_ref, ck_ref, cv_ref, tab_ref,
                        o_ref, ok_ref, ov_ref):
    bb = qb_ref.shape[0]
    ck = ck_ref[...]
    cv = cv_ref[...]
    qb = qb_ref[...]
    kn = kn_ref[...]
    vn = vn_ref[...]
    bias_j = tab_ref[0]
    bias_n = tab_ref[1][:, 0:1]
    sink = tab_ref[2][:, 0:1]
    s = lax.dot_general(qb.astype(BF16), ck.astype(BF16), (((2,), (1,)), ((0,), (0,))),
                        preferred_element_type=F32) * SCALE + bias_j[None]
    sn = jnp.sum(_rb(qb) * _rb(kn)[:, None, :], -1, keepdims=True) * SCALE + bias_n[None]
    m = jnp.maximum(jnp.maximum(jnp.max(s, -1, keepdims=True), sn), sink[None])
    p = jnp.exp(s - m)
    pn = jnp.exp(sn - m)
    den = jnp.sum(p, -1, keepdims=True) + pn + jnp.exp(sink[None] - m)
    o = lax.dot_general((p / den).astype(BF16), cv.astype(BF16), (((2,), (2,)), ((0,), (0,))),
                        preferred_element_type=F32)
    o_ref[...] = o + _rb(pn / den) * _rb(vn)[:, None, :]
    lane = lax.broadcasted_iota(jnp.int32, (LANES, LANES), 1)
    for b in range(bb):
        ok_ref[b] = jnp.where(lane == WINDOW - 1, knt_ref[:, b:b + 1], pltpu.roll(ck[b], WINDOW - 1, 1))
        ov_ref[b] = jnp.where(lane == WINDOW - 1, vnt_ref[:, b:b + 1], pltpu.roll(cv[b], WINDOW - 1, 1))


def attn_decode(qblk, proj_d, table, cache_k, cache_v, layer, bb):
    nbatch = proj_d.shape[0]
    a_w = A_HEADS * HEAD_DIM
    cols = lambda c: proj_d[:, c:c + LANES].reshape(nbatch // bb, bb, LANES).transpose(0, 2, 1)
    knt, vnt = cols(a_w), cols(a_w + LANES)
    kcol = (A_HEADS * HEAD_DIM) // LANES
    cache_spec = pl.BlockSpec((None, bb, LANES, WINDOW), lambda i: (layer, i, 0, 0))
    col_spec = pl.BlockSpec((None, LANES, bb), lambda i: (i, 0, 0))
    return pl.pallas_call(
        _attn_decode_kernel,
        grid=(nbatch // bb,),
        in_specs=[pl.BlockSpec((bb, A_HEADS, LANES), lambda i: (i, 0, 0)),
                  pl.BlockSpec((bb, LANES), lambda i: (i, kcol)),
                  pl.BlockSpec((bb, LANES), lambda i: (i, kcol + 1)),
                  col_spec, col_spec, cache_spec, cache_spec,
                  pl.BlockSpec((3, A_HEADS, WINDOW), lambda i: (0, 0, 0))],
        out_specs=[pl.BlockSpec((bb, A_HEADS, LANES), lambda i: (i, 0, 0)), cache_spec, cache_spec],
        out_shape=[jax.ShapeDtypeStruct((nbatch, A_HEADS, LANES), F32),
                   jax.ShapeDtypeStruct(cache_k.shape, F32),
                   jax.ShapeDtypeStruct(cache_v.shape, F32)],
        input_output_aliases={5: 1, 6: 2},
        compiler_params=_cparams(("arbitrary",)),
        name="attn_decode",
    )(qblk, proj_d, proj_d, knt, vnt, cache_k, cache_v, table)


def _xattn_decode_kernel(qb_ref, mk_ref, mv_ref, hm_ref, o_ref):
    qb = qb_ref[...]
    s = lax.dot_general(qb.astype(BF16), mk_ref[...].astype(BF16), (((2,), (1,)), ((0,), (0,))),
                        preferred_element_type=F32) * SCALE
    m = jnp.max(s, -1, keepdims=True)
    p = jnp.exp(s - m)
    p = p / jnp.sum(p, -1, keepdims=True)
    o = lax.dot_general(p.astype(BF16), mv_ref[...].astype(BF16), (((2,), (2,)), ((0,), (0,))),
                        preferred_element_type=F32)
    o_ref[...] = jnp.sum(o * hm_ref[...][None], axis=1)


def xattn_decode(qblk, mem_k, mem_v, hmask, layer, bb):
    nbatch = qblk.shape[0]
    mem_spec = pl.BlockSpec((None, bb, X_WIDTH, N_MEM), lambda i: (layer, i, 0, 0))
    return pl.pallas_call(
        _xattn_decode_kernel,
        grid=(nbatch // bb,),
        in_specs=[pl.BlockSpec((bb, X_QROWS, X_WIDTH), lambda i: (i, 0, 0)), mem_spec, mem_spec,
                  pl.BlockSpec((X_QROWS, X_WIDTH), lambda i: (0, 0))],
        out_specs=pl.BlockSpec((bb, X_WIDTH), lambda i: (i, 0)),
        out_shape=jax.ShapeDtypeStruct((nbatch, X_WIDTH), F32),
        compiler_params=_cparams(("parallel",)),
        name="xattn_decode",
    )(qblk, mem_k, mem_v, hmask)


def _hgrn_gates(hq, hf, loglb, log1mlb):
    ls = jnp.minimum(hf, 0.0) - jnp.log1p(jnp.exp(-jnp.abs(hf)))
    b = log1mlb + ls
    lf = jnp.maximum(loglb, b) + jnp.log1p(jnp.exp(-jnp.abs(loglb - b)))
    return _silu(hq), lf, _neg_expm1(lf)


def _hgrn_prompt_kernel(hq_ref, hf_ref, hi_ref, hg_ref, loglb_ref, log1mlb_ref, gain_ref,
                        bdm_ref, hm_ref, ob_ref, st_ref, st_scr, q_scr, k_scr, cum_scr, o_scr):
    i = pl.program_id(1)
    tt = hq_ref.shape[0]
    c = HG_CHUNK
    tb = HG_TB

    @pl.when(i == 0)
    def _():
        st_scr[...] = jnp.zeros_like(st_scr)

    qs, lf, kk = _hgrn_gates(hq_ref[...], hf_ref[...], loglb_ref[...], log1mlb_ref[...])
    row = lax.broadcasted_iota(jnp.int32, (tt, HG_WIDTH), 0) & (c - 1)
    cum = lf
    sh = 1
    while sh < c:
        cum = cum + jnp.where(row >= sh, pltpu.roll(cum, sh, 0), 0.0)
        sh *= 2
    q_scr[...] = qs
    k_scr[...] = kk
    cum_scr[...] = cum

    bdm = bdm_ref[...]
    hmask = hm_ref[...]
    def chunk(ci, carry):
        r0 = pl.multiple_of(ci * c, c)
        r = pl.ds(r0, c)
        cu = cum_scr[r, :]
        q = q_scr[r, :]
        k = k_scr[r, :]
        v = hi_ref[r, :]
        vb = v.astype(BF16)
        qb = _rb(q)
        cu2 = cu * LOG2E
        last = cu[c - 1:c, :]
        st = st_scr[...]
        o_inter = _bdot_nt(q * jnp.exp(cu), st)
        for j in range(c // tb):
            ns = tb * (j + 1)
            ti = lax.broadcasted_iota(jnp.int32, (tb, ns, HG_WIDTH), 0) + tb * j
            si = lax.broadcasted_iota(jnp.int32, (tb, ns, HG_WIDTH), 1)
            cut = cu2[tb * j:tb * (j + 1), :]
            dec = jnp.exp2(jnp.where(ti >= si, cut[:, None, :] - cu2[None, :ns, :], NEG))
            a2 = (dec * k[None, :ns, :]).astype(BF16)
            q4 = qb[tb * j:tb * (j + 1), None, :] * hmask[None, :, :]
            att = lax.dot_general(q4, a2, (((2,), (2,)), ((0,), (0,))),
                                  preferred_element_type=F32)
            w = jnp.dot(_rb(att.reshape(tb * HG_MROWS, ns)), vb[:ns, :],
                        preferred_element_type=F32).reshape(tb, HG_MROWS, HG_WIDTH)
            o_intra = jnp.sum(w * hmask[None, :, :], axis=1)
            o_scr[pl.ds(r0 + tb * j, tb), :] = o_intra + o_inter[tb * j:tb * (j + 1), :]
        upd = _bdot_tn(v, k * jnp.exp(last - cu))
        st_scr[...] = st * jnp.exp(last) + upd * bdm
        return carry

    lax.fori_loop(0, tt // c, chunk, 0, unroll=2)

    o = o_scr[...]
    ms = jnp.dot(o * o, bdm, precision=lax.Precision.HIGHEST,
                 preferred_element_type=F32) * (1.0 / HEAD_DIM)
    ob_ref[...] = o * lax.rsqrt(ms + RMS_EPS) * gain_ref[...] * _silu(hg_ref[...])

    @pl.when(i == pl.num_programs(1) - 1)
    def _():
        st_ref[...] = st_scr[...]


def hgrn_prompt(proj, loglb, log1mlb, gain4, bdm, hmask, batch, seq, tt):
    nt = seq // tt
    base = (A_HEADS + 2 * A_KV_HEADS) * HEAD_DIM // HG_WIDTH

    def col(cblk):
        return pl.BlockSpec((tt, HG_WIDTH), lambda b, i: (b * nt + i, cblk))

    row_spec = pl.BlockSpec((1, HG_WIDTH), lambda b, i: (0, 0))
    mat_spec = pl.BlockSpec((HG_WIDTH, HG_WIDTH), lambda b, i: (0, 0))
    return pl.pallas_call(
        _hgrn_prompt_kernel,
        grid=(batch, nt),
        in_specs=[col(base), col(base + 1), col(base + 2), col(base + 3),
                  row_spec, row_spec, row_spec, mat_spec,
                  pl.BlockSpec((HG_MROWS, HG_WIDTH), lambda b, i: (0, 0))],
        out_specs=[pl.BlockSpec((tt, HG_WIDTH), lambda b, i: (b * nt + i, 0)),
                   pl.BlockSpec((None, HG_WIDTH, HG_WIDTH), lambda b, i: (b, 0, 0))],
        out_shape=[jax.ShapeDtypeStruct((batch * seq, HG_WIDTH), F32),
                   jax.ShapeDtypeStruct((batch, HG_WIDTH, HG_WIDTH), F32)],
        scratch_shapes=[pltpu.VMEM((HG_WIDTH, HG_WIDTH), F32),
                        pltpu.VMEM((tt, HG_WIDTH), F32),
                        pltpu.VMEM((tt, HG_WIDTH), F32),
                        pltpu.VMEM((tt, HG_WIDTH), F32),
                        pltpu.VMEM((tt, HG_WIDTH), F32)],
        compiler_params=_cparams(("parallel", "arbitrary")),
        name="hgrn_prompt",
    )(proj, proj, proj, proj, loglb, log1mlb, gain4, bdm, hmask)


def _hgrn_decode_kernel(hq_ref, hf_ref, hi_ref, hg_ref, loglb_ref, log1mlb_ref, gain_ref, s_ref,
                        ob_ref, so_ref):
    nb = hq_ref.shape[1]
    qs, lf, kk = _hgrn_gates(hq_ref[...], hf_ref[...], loglb_ref[...], log1mlb_ref[...])
    v = hi_ref[...]
    f = jnp.exp(lf)
    s = s_ref[...].reshape(HEAD_DIM, HEAD_DIM, nb)
    att = jnp.sum(_rb(qs) * _rb(kk), axis=0, keepdims=True)
    o = _rb(att) * _rb(v) + jnp.sum(_rb(qs * f)[:, None, :] * _rb(s), axis=0)
    s_new = f[:, None, :] * s + _rb(kk)[:, None, :] * _rb(v)[None, :, :]
    so_ref[...] = s_new.reshape(HEAD_DIM * HEAD_DIM, nb)
    ms = jnp.mean(o * o, axis=0, keepdims=True)
    ob_ref[...] = o * lax.rsqrt(ms + RMS_EPS) * gain_ref[...] * _silu(hg_ref[...])


def hgrn_decode(gates_t, loglb_t, log1mlb_t, gain_t, state_t, layer):
    nb = gates_t.shape[1]

    def blk(off):
        return pl.BlockSpec((HEAD_DIM, nb), lambda h: (off * HG_HEADS + h, 0))

    par = pl.BlockSpec((HEAD_DIM, nb), lambda h: (h, 0))
    st = pl.BlockSpec((None, HEAD_DIM * HEAD_DIM, nb), lambda h: (layer, h, 0))
    return pl.pallas_call(
        _hgrn_decode_kernel,
        grid=(HG_HEADS,),
        in_specs=[blk(0), blk(1), blk(2), blk(3), par, par,
                  pl.BlockSpec((HEAD_DIM, nb), lambda h: (0, 0)), st],
        out_specs=[par, st],
        out_shape=[jax.ShapeDtypeStruct((HG_WIDTH, nb), F32),
                   jax.ShapeDtypeStruct(state_t.shape, F32)],
        input_output_aliases={7: 1},
        compiler_params=_cparams(("arbitrary",)),
        name="hgrn_decode",
    )(gates_t, gates_t, gates_t, gates_t, loglb_t, log1mlb_t, gain_t, state_t)


def _lru_gates(xc, wa_ref, ba_ref, wx_ref, bx_ref, lam_ref):
    r = jax.nn.sigmoid(_bdot(xc, wa_ref[...]) + ba_ref[...])
    gi = jax.nn.sigmoid(_bdot(xc, wx_ref[...]) + bx_ref[...])
    log_a = -LRU_C * r * _softplus(-lam_ref[...])
    a = jnp.exp(log_a)
    bterm = jnp.sqrt(_neg_expm1(2.0 * log_a)) * (gi * xc)
    return a, bterm


def _lru_prompt_kernel(lx_ref, lg_ref, cw_ref, cb_ref, wa_ref, ba_ref, wx_ref, bx_ref, lam_ref,
                       oc_ref, hl_ref, ext_scr, h_scr):
    i = pl.program_id(1)
    tt = lx_ref.shape[0]
    pad = 8

    @pl.when(i == 0)
    def _():
        ext_scr[0:pad, :] = jnp.zeros((pad, LRU_WIDTH), F32)
        h_scr[...] = jnp.zeros_like(h_scr)

    x = lx_ref[...]
    ext_scr[pad:pad + tt, :] = x
    xc = cb_ref[...] + cw_ref[CONV_W - 1:CONV_W, :] * x
    for j in range(CONV_W - 1):
        back = CONV_W - 1 - j
        xc = xc + cw_ref[j:j + 1, :] * ext_scr[pad - back:pad - back + tt, :]
    ext_scr[0:pad, :] = x[tt - pad:tt, :]

    a, bterm = _lru_gates(xc, wa_ref, ba_ref, wx_ref, bx_ref, lam_ref)
    row = lax.broadcasted_iota(jnp.int32, (tt, LRU_WIDTH), 0)
    sh = 1
    while sh < tt:
        keep = row >= sh
        b_s = jnp.where(keep, pltpu.roll(bterm, sh, 0), 0.0)
        a_s = jnp.where(keep, pltpu.roll(a, sh, 0), 1.0)
        bterm = a * b_s + bterm
        a = a * a_s
        sh *= 2
    h = a * h_scr[...] + bterm
    h_scr[...] = h[tt - 1:tt, :]
    oc_ref[...] = h * _gelu_tanh(lg_ref[...])

    @pl.when(i == pl.num_programs(1) - 1)
    def _():
        hl_ref[...] = h[tt - 1:tt, :]


def lru_prompt(proj, conv_w, conv_b, wa_bd, ba, wx_bd, bx, lam, batch, seq, tt):
    nt = seq // tt
    base = IN_COLS // LRU_WIDTH - 2

    def col(cblk):
        return pl.BlockSpec((tt, LRU_WIDTH), lambda b, i: (b * nt + i, cblk))

    row_spec = pl.BlockSpec((1, LRU_WIDTH), lambda b, i: (0, 0))
    mat_spec = pl.BlockSpec((LRU_WIDTH, LRU_WIDTH), lambda b, i: (0, 0))
    return pl.pallas_call(
        _lru_prompt_kernel,
        grid=(batch, nt),
        in_specs=[col(base), col(base + 1),
                  pl.BlockSpec((CONV_W, LRU_WIDTH), lambda b, i: (0, 0)), row_spec,
                  mat_spec, row_spec, mat_spec, row_spec, row_spec],
        out_specs=[pl.BlockSpec((tt, LRU_WIDTH), lambda b, i: (b * nt + i, 0)),
                   pl.BlockSpec((None, 1, LRU_WIDTH), lambda b, i: (b, 0, 0))],
        out_shape=[jax.ShapeDtypeStruct((batch * seq, LRU_WIDTH), F32),
                   jax.ShapeDtypeStruct((batch, 1, LRU_WIDTH), F32)],
        scratch_shapes=[pltpu.VMEM((tt + 8, LRU_WIDTH), F32),
                        pltpu.VMEM((1, LRU_WIDTH), F32)],
        compiler_params=_cparams(("parallel", "arbitrary")),
        name="lru_prompt",
    )(proj, proj, conv_w, conv_b, wa_bd, ba, wx_bd, bx, lam)


def _lru_decode_kernel(lx_ref, lg_ref, buf_ref, h0_ref, cw_ref, cb_ref, wa_ref, ba_ref, wx_ref,
                       bx_ref, lam_ref, oc_ref, hn_ref, nbuf_ref):
    x = lx_ref[...]
    buf = buf_ref[...]
    xc = cb_ref[...] + cw_ref[CONV_W - 1:CONV_W, :] * x
    for j in range(CONV_W - 1):
        xc = xc + cw_ref[j:j + 1, :] * buf[:, j * LRU_WIDTH:(j + 1) * LRU_WIDTH]
    a, bterm = _lru_gates(xc, wa_ref, ba_ref, wx_ref, bx_ref, lam_ref)
    h = a * h0_ref[...] + bterm
    hn_ref[...] = h
    oc_ref[...] = h * _gelu_tanh(lg_ref[...])
    nbuf_ref[...] = jnp.concatenate([buf[:, LRU_WIDTH:], x], axis=1)


def lru_decode(proj_d, conv_buf, h0, conv_w, conv_b, wa_bd, ba, wx_bd, bx, lam):
    nb = proj_d.shape[0]
    base = IN_COLS // LRU_WIDTH - 2
    row_spec = pl.BlockSpec((1, LRU_WIDTH), lambda i: (0, 0))
    mat_spec = pl.BlockSpec((LRU_WIDTH, LRU_WIDTH), lambda i: (0, 0))
    act = pl.BlockSpec((nb, LRU_WIDTH), lambda i: (0, 0))
    bufs = pl.BlockSpec((nb, (CONV_W - 1) * LRU_WIDTH), lambda i: (0, 0))
    return pl.pallas_call(
        _lru_decode_kernel,
        grid=(1,),
        in_specs=[pl.BlockSpec((nb, LRU_WIDTH), lambda i: (0, base)),
                  pl.BlockSpec((nb, LRU_WIDTH), lambda i: (0, base + 1)),
                  bufs, act, pl.BlockSpec((CONV_W, LRU_WIDTH), lambda i: (0, 0)), row_spec,
                  mat_spec, row_spec, mat_spec, row_spec, row_spec],
        out_specs=[act, act, bufs],
        out_shape=[jax.ShapeDtypeStruct((nb, LRU_WIDTH), F32),
                   jax.ShapeDtypeStruct((nb, LRU_WIDTH), F32),
                   jax.ShapeDtypeStruct((nb, (CONV_W - 1) * LRU_WIDTH), F32)],
        compiler_params=_cparams(("arbitrary",)),
        name="lru_decode",
    )(proj_d, proj_d, conv_buf, h0, conv_w, conv_b, wa_bd, ba, wx_bd, bx, lam)


def _xattn_prompt_kernel(x_ref, wq_ref, mk_ref, mv_ref, wo_ref, g_ref, b_ref, wr_ref, br_ref, o_ref):
    x = x_ref[...]
    q = _bdot(x, wq_ref[...]).astype(BF16)
    mk = mk_ref[...].astype(BF16)
    mv = mv_ref[...].astype(BF16)
    outs = []
    for h in range(X_HEADS):
        sl = slice(h * HEAD_DIM, (h + 1) * HEAD_DIM)
        s = _bdot_nt(q[:, sl], mk[:, sl]) * SCALE
        m = jnp.max(s, -1, keepdims=True)
        p = jnp.exp(s - m)
        p = p / jnp.sum(p, -1, keepdims=True)
        outs.append(_bdot(p, mv[:, sl]))
    o = jnp.concatenate(outs, axis=1)
    y = _layer_norm(ALPHA * x + _bdot(o, wo_ref[...]), g_ref[...], b_ref[...])
    logits = _bdot(y, wr_ref[...]) + br_ref[...]
    gate, g_idx = _route(logits)
    lane = lax.broadcasted_iota(jnp.int32, gate.shape, 1)
    for j in range(D_MODEL // LANES):
        o_ref[j] = y[:, j * LANES:(j + 1) * LANES]
    o_ref[D_MODEL // LANES] = jnp.where(lane == 0, g_idx.astype(F32), gate)


def xattn_prompt(x, wq, mem_kv, wo, g, b, wr, br, batch, seq, tt):
    nt = seq // tt
    const = lambda bi, i: (0, 0)
    return pl.pallas_call(
        _xattn_prompt_kernel,
        grid=(batch, nt),
        in_specs=[pl.BlockSpec((tt, D_MODEL), lambda bi, i: (bi * nt + i, 0)),
                  pl.BlockSpec((D_MODEL, X_WIDTH), const),
                  pl.BlockSpec((N_MEM, X_WIDTH), lambda bi, i: (bi, 0)),
                  pl.BlockSpec((N_MEM, X_WIDTH), lambda bi, i: (bi, 1)),
                  pl.BlockSpec((X_WIDTH, D_MODEL), const),
                  pl.BlockSpec((1, D_MODEL), const),
                  pl.BlockSpec((1, D_MODEL), const),
                  pl.BlockSpec((D_MODEL, ROUTER_LANES), const),
                  pl.BlockSpec((1, ROUTER_LANES), const)],
        out_specs=pl.BlockSpec((XG_PLANES, tt, LANES), lambda bi, i: (0, bi * nt + i, 0)),
        out_shape=jax.ShapeDtypeStruct((XG_PLANES, batch * seq, LANES), F32),
        compiler_params=_cparams(("parallel", "parallel")),
        name="xattn_prompt",
    )(x, wq, mem_kv, mem_kv, wo, g, b, wr, br)


def _route(logits):
    lane = lax.broadcasted_iota(jnp.int32, logits.shape, 1)
    big = jnp.int32(ROUTER_LANES)
    ninf = -jnp.inf
    gl = jnp.where(lane < N_GROUPS, logits, ninf)
    gm = jnp.max(gl, -1, keepdims=True)
    g_val = 1.0 / jnp.sum(jnp.exp(gl - gm), -1, keepdims=True)
    g_idx = jnp.min(jnp.where(gl == gm, lane, big), -1, keepdims=True)
    lo = N_GROUPS + EXP_PER_GROUP * g_idx
    el = jnp.where((lane >= lo) & (lane < lo + EXP_PER_GROUP), logits, ninf)
    v1 = jnp.max(el, -1, keepdims=True)
    i1 = jnp.min(jnp.where(el == v1, lane, big), -1, keepdims=True)
    el2 = jnp.where(lane == i1, ninf, el)
    v2 = jnp.max(el2, -1, keepdims=True)
    i2 = jnp.min(jnp.where(el2 == v2, lane, big), -1, keepdims=True)
    e2 = jnp.exp(v2 - v1)
    w1 = g_val / (1.0 + e2)
    w2 = g_val * e2 / (1.0 + e2)
    return jnp.where(lane == i1, w1, 0.0) + jnp.where(lane == i2, w2, 0.0), g_idx


def _moe_dense_kernel(x_ref, wr_ref, br_ref, wg_ref, wu_ref, wd_ref, g_ref, b_ref, o_ref,
                      gate_scr, acc_scr):
    e = pl.program_id(1)

    @pl.when(e == 0)
    def _():
        logits = _bdot(x_ref[...], wr_ref[...]) + br_ref[...]
        gate_scr[...] = _route(logits)[0]
        acc_scr[...] = jnp.zeros_like(acc_scr)

    xb = x_ref[...].astype(BF16)
    lane = lax.broadcasted_iota(jnp.int32, gate_scr.shape, 1)
    gcol = jnp.sum(jnp.where(lane == e + N_GROUPS, gate_scr[...], 0.0), -1, keepdims=True)
    hid = _silu(_bdot(xb, wg_ref[...])) * _bdot(xb, wu_ref[...])
    acc_scr[...] += _bdot(hid * gcol, wd_ref[...])

    @pl.when(e == pl.num_programs(1) - 1)
    def _():
        o_ref[...] = _layer_norm(ALPHA * x_ref[...] + acc_scr[...], g_ref[...], b_ref[...])


def moe_dense(x, wr, br, wg, wu, wd, g, b, layer, tm):
    m = x.shape[0]
    return pl.pallas_call(
        _moe_dense_kernel,
        grid=(m // tm, N_EXPERTS),
        in_specs=[pl.BlockSpec((tm, D_MODEL), lambda i, e: (i, 0)),
                  pl.BlockSpec((D_MODEL, ROUTER_LANES), lambda i, e: (0, 0)),
                  pl.BlockSpec((1, ROUTER_LANES), lambda i, e: (0, 0)),
                  pl.BlockSpec((None, None, D_MODEL, EXP_FF), lambda i, e: (layer, e, 0, 0)),
                  pl.BlockSpec((None, None, D_MODEL, EXP_FF), lambda i, e: (layer, e, 0, 0)),
                  pl.BlockSpec((None, None, EXP_FF, D_MODEL), lambda i, e: (layer, e, 0, 0)),
                  pl.BlockSpec((1, D_MODEL), lambda i, e: (0, 0)),
                  pl.BlockSpec((1, D_MODEL), lambda i, e: (0, 0))],
        out_specs=pl.BlockSpec((tm, D_MODEL), lambda i, e: (i, 0)),
        out_shape=jax.ShapeDtypeStruct((m, D_MODEL), F32),
        scratch_shapes=[pltpu.VMEM((tm, ROUTER_LANES), F32), pltpu.VMEM((tm, D_MODEL), F32)],
        compiler_params=_cparams(("parallel", "arbitrary")),
        name="moe_dense",
    )(x, wr, br, wg, wu, wd, g, b)


def _sc_mesh():
    return plsc.VectorSubcoreMesh(core_axis_name="core", subcore_axis_name="subcore")


def sc_scatter_rows(x, idx, n_out):
    r = x.shape[0]

    @functools.partial(pl.kernel, out_type=jax.ShapeDtypeStruct((n_out, LANES), x.dtype),
                       mesh=_sc_mesh(), scratch_types=[], name="sc_scatter_rows")
    def k(x_hbm, i_hbm, o_hbm):
        def body(x_vmem, i_vmem):
            pltpu.sync_copy(x_vmem, o_hbm.at[i_vmem.at[0]])

        pltpu.emit_pipeline(
            body,
            grid=(r // SC_WINDOW,),
            in_specs=[pl.BlockSpec((SC_WINDOW, LANES), lambda i: (i, 0)),
                      pl.BlockSpec((1, SC_WINDOW), lambda i: (0, i))],
            out_specs=[],
            core_axis_name=("core", "subcore"),
            dimension_semantics=(pltpu.PARALLEL,),
        )(x_hbm, i_hbm)

    return k(x, idx.reshape(1, r))


def sc_gather_rows(table, idx):
    r = idx.shape[0]

    @functools.partial(pl.kernel, out_type=jax.ShapeDtypeStruct((r, LANES), table.dtype),
                       mesh=_sc_mesh(), scratch_types=[], name="sc_gather_rows")
    def k(t_hbm, i_hbm, o_hbm):
        def body(i_vmem, o_vmem):
            pltpu.sync_copy(t_hbm.at[i_vmem.at[0]], o_vmem)

        pltpu.emit_pipeline(
            body,
            grid=(r // SC_WINDOW,),
            in_specs=[pl.BlockSpec((1, SC_WINDOW), lambda i: (0, i))],
            out_specs=[pl.BlockSpec((SC_WINDOW, LANES), lambda i: (i, 0))],
            core_axis_name=("core", "subcore"),
            dimension_semantics=(pltpu.PARALLEL,),
        )(i_hbm, o_hbm)

    return k(table, idx.reshape(1, r))


def _moe_sorted_kernel(tg_ref, nused_ref, xs_ref, wg32_ref, wu32_ref, wd32_ref, g_ref, b_ref, o_ref,
                       wg_ref, wu_ref, wd_ref):
    t = pl.program_id(0)
    used = t < nused_ref[0]

    @pl.when(used & ((t == 0) | (tg_ref[t] != tg_ref[jnp.maximum(t - 1, 0)])))
    def _():
        wg_ref[...] = wg32_ref[...].astype(BF16)
        wu_ref[...] = wu32_ref[...].astype(BF16)
        wd_ref[...] = wd32_ref[...].astype(BF16)

    @pl.when(used)
    def _():
        x = jnp.concatenate([xs_ref[j] for j in range(X_PLANES)], axis=1)
        gate = xs_ref[X_PLANES]
        xb = x.astype(BF16)
        lane = lax.broadcasted_iota(jnp.int32, gate.shape, 1)
        first = N_GROUPS + EXP_PER_GROUP * tg_ref[t]
        acc = jnp.zeros(x.shape, F32)
        for e in range(EXP_PER_GROUP):
            gcol = jnp.sum(jnp.where(lane == first + e, gate, 0.0), -1, keepdims=True)
            hid = _silu(_bdot(xb, wg_ref[e])) * _bdot(xb, wu_ref[e])
            acc = acc + _bdot(hid * gcol, wd_ref[e])
        y = _layer_norm(ALPHA * x + acc, g_ref[...], b_ref[...])
        for j in range(X_PLANES):
            o_ref[j] = y[:, j * LANES:(j + 1) * LANES]


def _group_slots(group_idx, n, tm):
    n_tiles = n // tm + N_GROUPS
    onehot = (group_idx[:, None] == jnp.arange(N_GROUPS)[None, :]).astype(jnp.int32)
    csum = jnp.cumsum(onehot, axis=0)
    counts = csum[-1]
    rank = jnp.sum(onehot * csum, axis=1) - 1
    tiles_g = (counts + tm - 1) // tm
    tile_end = jnp.cumsum(tiles_g)
    slot_base = (tile_end - tiles_g) * tm
    slot = (jnp.sum(onehot * slot_base[None, :], axis=1) + rank).astype(jnp.int32)
    tile_group = jnp.sum((jnp.arange(n_tiles)[:, None] >= tile_end[None, :]).astype(jnp.int32), axis=1)
    tile_group = jnp.minimum(tile_group, N_GROUPS - 1).astype(jnp.int32)
    return slot, tile_group, tile_end[-1:].astype(jnp.int32)


def moe_routed_sc(xg, wg, wu, wd, g, b, layer, n, tm, during_scatter, during_gather):
    slot, tile_group, n_used = _group_slots(xg[X_PLANES, :, 0].astype(jnp.int32), n, tm)
    n_tiles = tile_group.shape[0]
    n_slots = n_tiles * tm
    plane_base = lambda planes: jnp.arange(planes, dtype=jnp.int32)[:, None] * n_slots
    idx_in = (plane_base(XG_PLANES) + slot[None, :]).reshape(-1)
    idx_out = (plane_base(X_PLANES) + slot[None, :]).reshape(-1)
    xs = sc_scatter_rows(xg.reshape(XG_PLANES * n, LANES), idx_in, XG_PLANES * n_slots)
    xs = xs.reshape(XG_PLANES, n_slots, LANES)
    xs, side = lax.optimization_barrier((xs, during_scatter()))
    wspec = lambda shp: pl.BlockSpec((None, None, EXP_PER_GROUP) + shp,
                                     lambda t, tg, nu: (layer, tg[t], 0, 0, 0))
    grid_spec = pltpu.PrefetchScalarGridSpec(
        num_scalar_prefetch=2,
        grid=(n_tiles,),
        in_specs=[pl.BlockSpec((XG_PLANES, tm, LANES), lambda t, tg, nu: (0, t, 0)),
                  wspec((D_MODEL, EXP_FF)), wspec((D_MODEL, EXP_FF)), wspec((EXP_FF, D_MODEL)),
                  pl.BlockSpec((1, D_MODEL), lambda t, tg, nu: (0, 0)),
                  pl.BlockSpec((1, D_MODEL), lambda t, tg, nu: (0, 0))],
        out_specs=pl.BlockSpec((X_PLANES, tm, LANES), lambda t, tg, nu: (0, t, 0)),
        scratch_shapes=[pltpu.VMEM((EXP_PER_GROUP, D_MODEL, EXP_FF), BF16),
                        pltpu.VMEM((EXP_PER_GROUP, D_MODEL, EXP_FF), BF16),
                        pltpu.VMEM((EXP_PER_GROUP, EXP_FF, D_MODEL), BF16)])
    grouped = lambda w: w.reshape(w.shape[0], N_GROUPS, EXP_PER_GROUP, w.shape[2], w.shape[3])
    ys = pl.pallas_call(
        _moe_sorted_kernel,
        grid_spec=grid_spec,
        out_shape=jax.ShapeDtypeStruct((X_PLANES, n_slots, LANES), F32),
        compiler_params=_cparams(("arbitrary",)),
        name="moe_sorted",
    )(tile_group, n_used, xs, grouped(wg), grouped(wu), grouped(wd), g, b)
    y = sc_gather_rows(ys.reshape(X_PLANES * n_slots, LANES), idx_out)
    y, side = lax.optimization_barrier((y, during_gather(side)))
    return y.reshape(X_PLANES, n, LANES), side


def _t5_bucket(dist):
    max_exact = N_BUCKETS // 2
    d = jnp.maximum(dist, 0)
    df = jnp.maximum(d, 1).astype(F32)
    log_b = max_exact + (jnp.log(df / max_exact) / math.log(MAX_DISTANCE / max_exact)
                         * (N_BUCKETS - max_exact)).astype(jnp.int32)
    return jnp.where(d < max_exact, d, jnp.minimum(log_b, N_BUCKETS - 1))


def _bucket_lookup(rel_bias, bucket):
    out = jnp.zeros(bucket.shape + (rel_bias.shape[1],), F32)
    for i in range(N_BUCKETS):
        out = jnp.where((bucket == i)[..., None], rel_bias[i].astype(F32), out)
    return out


def _prompt_bias(rel_bias):
    qi = jnp.arange(WINDOW)[:, None]
    kj = jnp.arange(2 * WINDOW)[None, :]
    dist = qi + WINDOW - kj
    bias = _bucket_lookup(rel_bias, _t5_bucket(dist)).transpose(2, 0, 1)
    valid = (dist >= 0) & (dist <= WINDOW)
    return jnp.where(valid[None], bias, NEG)


def _decode_table(rel_bias, attn_sink):
    dist = WINDOW - jnp.arange(WINDOW + 1)
    bias = _bucket_lookup(rel_bias, _t5_bucket(dist)).T
    depth = attn_sink.shape[0]
    wide = lambda v: jnp.broadcast_to(v[..., None], v.shape + (WINDOW,))
    per_layer = lambda t: jnp.broadcast_to(t[None], (depth,) + t.shape)
    return jnp.stack([per_layer(bias[:, :WINDOW]), per_layer(wide(bias[:, WINDOW])),
                      wide(attn_sink.astype(F32))], axis=1)


def _block_ones(width):
    idx = jnp.arange(width) // HEAD_DIM
    return (idx[:, None] == idx[None, :])


def _head_rows_mask():
    head = jnp.arange(HG_WIDTH)[None, :] // HEAD_DIM
    return (head == jnp.arange(HG_MROWS)[:, None]).astype(F32)


def _block_diag(w):
    nblk, s, _ = w.shape
    eye = jnp.eye(nblk, dtype=w.dtype)
    return (eye[:, None, :, None] * w[:, :, None, :]).reshape(nblk * s, nblk * s)


def kernel(x_prompt, x_sample, mem_prompt, cache_win_k, cache_win_v, state_hgrn, state_conv, state_lru, cache_mem_k, cache_mem_v, rel_bias, hg_lb, w_in, attn_sink, hg_gain, conv_w, conv_b, lru_wa, lru_ba, lru_wx, lru_bx, lru_lam, w_out, ln1_g, ln1_b, x_wq, x_wk, x_wv, x_wo, ln2_g, ln2_b, r_gw, r_gb, r_ew, r_eb, e_wg, e_wu, e_wd, ln3_g, ln3_b):
    bp, seq, d = x_prompt.shape
    n_tok = bp * seq
    nd = x_sample.shape[0]
    depth = w_in.shape[0]

    lbs = jnp.cumsum(jax.nn.softmax(hg_lb.astype(F32), axis=0), axis=0)
    lbs = lbs - lbs[0]
    loglb = jnp.log(lbs)
    log1mlb = jnp.log1p(-lbs)
    gain4 = jnp.tile(hg_gain, (1, HG_HEADS))

    bias_p = _prompt_bias(rel_bias)
    bdm256 = _block_ones(HG_WIDTH).astype(F32)
    hmask = _head_rows_mask()

    w_in_b = w_in.astype(BF16)
    w_out_b = w_out.astype(BF16)
    wq_b = x_wq.astype(BF16)
    wkv_b = jnp.concatenate([x_wk, x_wv], axis=-1).astype(BF16)
    wo_b = x_wo.astype(BF16)
    rew = r_ew.transpose(0, 2, 1, 3).reshape(depth, d, N_EXPERTS)
    wr = jnp.concatenate([r_gw, rew, jnp.zeros((depth, d, ROUTER_LANES - N_GROUPS - N_EXPERTS), F32)], -1)
    br = jnp.concatenate([r_gb, r_eb.reshape(depth, N_EXPERTS),
                          jnp.zeros((depth, ROUTER_LANES - N_GROUPS - N_EXPERTS), F32)], -1)

    a_w = A_HEADS * HEAD_DIM
    xp = x_prompt.reshape(bp * seq, d)
    xs = x_sample.reshape(nd, d)
    mem = mem_prompt.reshape(bp * N_MEM, d)
    ckt = cache_win_k.transpose(0, 1, 3, 4, 2).reshape(depth, nd, LANES, WINDOW)
    cvt = cache_win_v.transpose(0, 1, 3, 4, 2).reshape(depth, nd, LANES, WINDOW)
    cmkt = cache_mem_k.transpose(0, 1, 3, 4, 2).reshape(depth, nd, X_WIDTH, N_MEM)
    cmvt = cache_mem_v.transpose(0, 1, 3, 4, 2).reshape(depth, nd, X_WIDTH, N_MEM)
    state_t = state_hgrn.transpose(0, 2, 3, 4, 1).reshape(depth, HG_HEADS * HEAD_DIM * HEAD_DIM, nd)
    dec_tab = _decode_table(rel_bias, attn_sink)
    xq_mask = _head_rows_mask()[:X_QROWS]
    head_group = jnp.arange(A_HEADS) // A_REP

    p_wk, p_wv, p_s, p_cb, p_h, p_mk, p_mv = [], [], [], [], [], [], []
    s_cb, s_h = [], []
    for l in range(depth):
        row = lambda v: v[l].reshape(1, -1)
        wa_bd = _block_diag(lru_wa[l]).astype(BF16)
        wx_bd = _block_diag(lru_wx[l]).astype(BF16)
        lru_args = (conv_w[l], row(conv_b), wa_bd, row(lru_ba), wx_bd, row(lru_bx), row(lru_lam))
        wo_parts = [w_out_b[l, :a_w], w_out_b[l, a_w:a_w + HG_WIDTH], w_out_b[l, a_w + HG_WIDTH:]]

        proj = matmul(xp, w_in_b[l], 1024, 768)
        oa = attn_prompt(proj, attn_sink[l], bias_p, bp, seq)
        ob, st = hgrn_prompt(proj, row(loglb), row(log1mlb), row(gain4), bdm256, hmask, bp, seq, 256)
        oc, hl = lru_prompt(proj, *lru_args, bp, seq, 256)
        xp = proj_res_ln(xp, [oa, ob, oc], wo_parts, row(ln1_g), row(ln1_b), 1024)
        mkv = matmul(mem, wkv_b[l], 256, 512)
        xg = xattn_prompt(xp, wq_b[l], mkv, wo_b[l], row(ln2_g), row(ln2_b), wr[l], br[l:l + 1],
                          bp, seq, 1024)
        def decode_mixers(xs=xs, ckt=ckt, cvt=cvt, state_t=state_t, l=l, lru_args=lru_args,
                          wo_parts=wo_parts, row=row):
            projd = matmul(xs, w_in_b[l], nd, 768)
            q3 = projd[:, :a_w].reshape(nd, A_HEADS, 1, HEAD_DIM)
            on_group = head_group[None, :, None, None] == jnp.arange(A_KV_HEADS)[None, None, :, None]
            qblk = jnp.where(on_group, q3, 0.0).reshape(nd, A_HEADS, LANES)
            o3, ckt, cvt = attn_decode(qblk, projd, dec_tab[l], ckt, cvt, l, 16)
            o4 = o3.reshape(nd, A_HEADS, A_KV_HEADS, HEAD_DIM)
            oa = jnp.sum(jnp.where(on_group, o4, 0.0), axis=2).reshape(nd, a_w)
            gates_t = projd[:, a_w + 2 * LANES:a_w + 2 * LANES + 4 * HG_WIDTH].T
            bc = lambda v: jnp.broadcast_to(v[:, None], (v.shape[0], nd))
            ob_t, state_t = hgrn_decode(gates_t, bc(loglb[l]), bc(log1mlb[l]), bc(hg_gain[l]), state_t, l)
            oc, nh, nbuf = lru_decode(projd, state_conv[l].reshape(nd, -1), state_lru[l], *lru_args)
            xs = proj_res_ln(xs, [oa, ob_t.T, oc], wo_parts, row(ln1_g), row(ln1_b), nd)
            return xs, ckt, cvt, state_t, nh, nbuf

        def decode_rest(side, l=l, row=row):
            xs = side[0]
            qd = matmul(xs, wq_b[l], nd, X_WIDTH)
            qdb = qd[:, None, :] * xq_mask[None, :, :]
            od = xattn_decode(qdb, cmkt, cmvt, xq_mask, l, 8)
            xs = proj_res_ln(xs, [od], [wo_b[l]], row(ln2_g), row(ln2_b), nd)
            xs = moe_dense(xs, wr[l], br[l:l + 1], e_wg, e_wu, e_wd, row(ln3_g), row(ln3_b), l, nd)
            return (xs,) + tuple(side[1:])

        xp, (xs, ckt, cvt, state_t, nh, nbuf) = moe_routed_sc(
            xg, e_wg, e_wu, e_wd, row(ln3_g), row(ln3_b), l, n_tok, MOE_TM, decode_mixers, decode_rest)

        proj3 = proj.reshape(bp, seq, IN_COLS)
        p_wk.append(proj3[:, seq - WINDOW:, a_w:a_w + LANES].reshape(bp, WINDOW, A_KV_HEADS, HEAD_DIM))
        p_wv.append(proj3[:, seq - WINDOW:, a_w + LANES:a_w + 2 * LANES].reshape(bp, WINDOW, A_KV_HEADS, HEAD_DIM))
        st5 = st.reshape(bp, HG_HEADS, HEAD_DIM, HG_HEADS, HEAD_DIM)
        p_s.append(jnp.stack([st5[:, h, :, h, :] for h in range(HG_HEADS)], 1).transpose(0, 1, 3, 2))
        p_cb.append(proj3[:, seq - (CONV_W - 1):, IN_COLS - 2 * LRU_WIDTH:IN_COLS - LRU_WIDTH])
        p_h.append(hl.reshape(bp, LRU_WIDTH))
        p_mk.append(mkv[:, :X_WIDTH].reshape(bp, N_MEM, X_HEADS, HEAD_DIM))
        p_mv.append(mkv[:, X_WIDTH:].reshape(bp, N_MEM, X_HEADS, HEAD_DIM))

        s_cb.append(nbuf.reshape(nd, CONV_W - 1, LRU_WIDTH))
        s_h.append(nh)

    xp = jnp.transpose(xp, (1, 0, 2))
    unkey = lambda c: c.reshape(depth, nd, A_KV_HEADS, HEAD_DIM, WINDOW).transpose(0, 1, 4, 2, 3)
    s_s = state_t.reshape(depth, HG_HEADS, HEAD_DIM, HEAD_DIM, nd).transpose(0, 4, 1, 2, 3)
    return (xp.reshape(bp, seq, d), xs.reshape(nd, 1, d),
            jnp.stack(p_wk), jnp.stack(p_wv), jnp.stack(p_s), jnp.stack(p_cb), jnp.stack(p_h),
            jnp.stack(p_mk), jnp.stack(p_mv),
            unkey(ckt), unkey(cvt), s_s, jnp.stack(s_cb), jnp.stack(s_h))
```

```python
import functools
import math

import jax
import jax.numpy as jnp
from jax import lax
from jax.experimental import pallas as pl
from jax.experimental.pallas import tpu as pltpu
from jax.experimental.pallas import tpu_sc as plsc

F32 = jnp.float32
BF16 = jnp.bfloat16

D_MODEL = 1024
DEPTH = 4
HEAD_DIM = 64
A_HEADS = 8
A_KV_HEADS = 2
A_REP = A_HEADS // A_KV_HEADS
WINDOW = 128
A_QB = 2
N_BUCKETS = 32
MAX_DISTANCE = 128
HG_WIDTH = 256
HG_HEADS = 4
HG_CHUNK = 64
HG_TB = 16
HG_MROWS = 8
LOG2E = math.log2(math.e)
LRU_WIDTH = 256
LRU_BLOCKS = 4
CONV_W = 4
LRU_C = 8.0
N_MEM = 256
X_HEADS = 4
X_WIDTH = X_HEADS * HEAD_DIM
X_QROWS = 8
N_GROUPS = 4
EXP_PER_GROUP = 4
N_EXPERTS = N_GROUPS * EXP_PER_GROUP
EXP_FF = D_MODEL // 4
ALPHA = (2 * DEPTH) ** 0.25
LN_EPS = 1e-5
RMS_EPS = 1e-6
IN_COLS = 2304
SCALE = HEAD_DIM ** -0.5
NEG = -1e30
LANES = 128
ROUTER_LANES = 128
XG_WIDTH = D_MODEL + ROUTER_LANES
XG_PLANES = XG_WIDTH // LANES
X_PLANES = D_MODEL // LANES
MOE_TM = 512
SC_WINDOW = 128
VMEM_LIMIT = 48 * 1024 * 1024


def _cparams(sem):
    return pltpu.CompilerParams(dimension_semantics=sem, vmem_limit_bytes=VMEM_LIMIT)


def _bdot(a, b):
    return jnp.dot(a.astype(BF16), b.astype(BF16), preferred_element_type=F32)


def _bdot_nt(a, b):
    return lax.dot_general(a.astype(BF16), b.astype(BF16), (((1,), (1,)), ((), ())),
                           preferred_element_type=F32)


def _bdot_tn(a, b):
    return lax.dot_general(a.astype(BF16), b.astype(BF16), (((0,), (0,)), ((), ())),
                           preferred_element_type=F32)


def _rb(x):
    return x.astype(BF16).astype(F32)


def _silu(x):
    return x * jax.nn.sigmoid(x)


def _neg_expm1(x):
    return -jnp.tanh(0.5 * x) * (jnp.exp(x) + 1.0)


def _softplus(x):
    return jnp.maximum(x, 0.0) + jnp.log1p(jnp.exp(-jnp.abs(x)))


def _gelu_tanh(x):
    return 0.5 * x * (1.0 + jnp.tanh(math.sqrt(2.0 / math.pi) * (x + 0.044715 * (x * x * x))))


def _layer_norm(y, g, b):
    mu = jnp.mean(y, -1, keepdims=True)
    yc = y - mu
    var = jnp.mean(yc * yc, -1, keepdims=True)
    return yc * lax.rsqrt(var + LN_EPS) * g + b


def _rows(x_ref):
    if len(x_ref.shape) == 2:
        return x_ref[...]
    return jnp.concatenate([x_ref[j] for j in range(x_ref.shape[0])], axis=1)


def _rows_spec(x, tm, nargs):
    if x.ndim == 2:
        return pl.BlockSpec((tm, x.shape[1]), (lambda i: (i, 0)) if nargs == 1 else (lambda i, j: (i, 0)))
    blk = (x.shape[0], tm, LANES)
    return pl.BlockSpec(blk, (lambda i: (0, i, 0)) if nargs == 1 else (lambda i, j: (0, i, 0)))


def _mm_kernel(x_ref, w_ref, o_ref):
    o_ref[...] = _bdot(_rows(x_ref), w_ref[...])


def matmul(x, w, tm, tn):
    m = x.shape[-2]
    k, n = w.shape
    return pl.pallas_call(
        _mm_kernel,
        grid=(m // tm, n // tn),
        in_specs=[_rows_spec(x, tm, 2),
                  pl.BlockSpec((k, tn), lambda i, j: (0, j))],
        out_specs=pl.BlockSpec((tm, tn), lambda i, j: (i, j)),
        out_shape=jax.ShapeDtypeStruct((m, n), F32),
        compiler_params=_cparams(("parallel", "parallel")),
        name="matmul",
    )(x, w)


def _proj_res_ln_kernel(n_in, x_ref, *refs):
    a_refs = refs[:n_in]
    w_refs = refs[n_in:2 * n_in]
    g_ref, b_ref, o_ref = refs[2 * n_in:]
    y = ALPHA * _rows(x_ref)
    for a_ref, w_ref in zip(a_refs, w_refs):
        y = y + _bdot(a_ref[...], w_ref[...])
    o_ref[...] = _layer_norm(y, g_ref[...], b_ref[...])


def proj_res_ln(x, a_list, w_list, g, b, tm):
    m = x.shape[-2]
    d = w_list[0].shape[1]
    n_in = len(a_list)
    in_specs = [_rows_spec(x, tm, 1)]
    in_specs += [pl.BlockSpec((tm, a.shape[1]), lambda i: (i, 0)) for a in a_list]
    in_specs += [pl.BlockSpec(w.shape, lambda i: (0, 0)) for w in w_list]
    in_specs += [pl.BlockSpec((1, d), lambda i: (0, 0))] * 2
    return pl.pallas_call(
        functools.partial(_proj_res_ln_kernel, n_in),
        grid=(m // tm,),
        in_specs=in_specs,
        out_specs=pl.BlockSpec((tm, d), lambda i: (i, 0)),
        out_shape=jax.ShapeDtypeStruct((m, d), F32),
        compiler_params=_cparams(("parallel",)),
        name="proj_res_ln",
    )(x, *a_list, *w_list, g, b)


def _attn_prompt_kernel(sink_ref, q_ref, kc_ref, kp_ref, vc_ref, vp_ref, bias_ref, o_ref):
    n = pl.program_id(1)
    col = lax.broadcasted_iota(jnp.int32, (WINDOW, 2 * WINDOW), 1)
    first = jnp.where((n == 0) & (col < WINDOW), NEG, 0.0)
    kk = jnp.concatenate([kp_ref[...], kc_ref[...]], axis=0).astype(BF16)
    vv = jnp.concatenate([vp_ref[...], vc_ref[...]], axis=0).astype(BF16)
    q = q_ref[...].astype(BF16)
    for u in range(A_QB):
        rows = slice(u * WINDOW, (u + 1) * WINDOW)
        keys = slice(u * WINDOW, (u + 2) * WINDOW)
        outs = []
        for h in range(A_HEADS):
            g = h // A_REP
            qh = q[rows, h * HEAD_DIM:(h + 1) * HEAD_DIM]
            kg = kk[keys, g * HEAD_DIM:(g + 1) * HEAD_DIM]
            vg = vv[keys, g * HEAD_DIM:(g + 1) * HEAD_DIM]
            s = _bdot_nt(qh, kg) * SCALE + bias_ref[h]
            if u == 0:
                s = s + first
            sink = sink_ref[h]
            m = jnp.maximum(jnp.max(s, -1, keepdims=True), sink)
            p = jnp.exp(s - m)
            den = jnp.sum(p, -1, keepdims=True) + jnp.exp(sink - m)
            outs.append(_bdot(p / den, vg))
        o_ref[rows, :] = jnp.concatenate(outs, axis=1)


def attn_prompt(proj, sink, bias, batch, seq):
    nb = seq // WINDOW
    ns = nb // A_QB
    tq = A_QB * WINDOW
    qcol = 0
    kcol = (A_HEADS * HEAD_DIM) // LANES
    vcol = kcol + 1

    def cur(c):
        return lambda b, n: (b * ns + n, c)

    def prev(c):
        return lambda b, n: (b * nb + jnp.maximum(n * A_QB - 1, 0), c)

    return pl.pallas_call(
        _attn_prompt_kernel,
        grid=(batch, ns),
        in_specs=[pl.BlockSpec(memory_space=pltpu.SMEM),
                  pl.BlockSpec((tq, A_HEADS * HEAD_DIM), cur(qcol)),
                  pl.BlockSpec((tq, LANES), cur(kcol)),
                  pl.BlockSpec((WINDOW, LANES), prev(kcol)),
                  pl.BlockSpec((tq, LANES), cur(vcol)),
                  pl.BlockSpec((WINDOW, LANES), prev(vcol)),
                  pl.BlockSpec((A_HEADS, WINDOW, 2 * WINDOW), lambda b, n: (0, 0, 0))],
        out_specs=pl.BlockSpec((tq, A_HEADS * HEAD_DIM), cur(0)),
        out_shape=jax.ShapeDtypeStruct((batch * seq, A_HEADS * HEAD_DIM), F32),
        compiler_params=_cparams(("parallel", "parallel")),
        name="attn_prompt",
    )(sink, proj, proj, proj, proj, proj, bias)


def _attn_decode_kernel(qb_ref, kn_ref, vn_ref, knt_ref, vnt_ref, ck_ref, cv_ref, tab_ref,
                        o_ref, ok_ref, ov_ref):
    bb = qb_ref.shape[0]
    ck = ck_ref[...]
    cv = cv_ref[...]
    qb = qb_ref[...]
    kn = kn_ref[...]
    vn = vn_ref[...]
    bias_j = tab_ref[0]
    bias_n = tab_ref[1][:, 0:1]
    sink = tab_ref[2][:, 0:1]
    s = lax.dot_general(qb.astype(BF16), ck.astype(BF16), (((2,), (1,)), ((0,), (0,))),
                        preferred_element_type=F32) * SCALE + bias_j[None]
    sn = jnp.sum(_rb(qb) * _rb(kn)[:, None, :], -1, keepdims=True) * SCALE + bias_n[None]
    m = jnp.maximum(jnp.maximum(jnp.max(s, -1, keepdims=True), sn), sink[None])
    p = jnp.exp(s - m)
    pn = jnp.exp(sn - m)
    den = jnp.sum(p, -1, keepdims=True) + pn + jnp.exp(sink[None] - m)
    o = lax.dot_general((p / den).astype(BF16), cv.astype(BF16), (((2,), (2,)), ((0,), (0,))),
                        preferred_element_type=F32)
    o_ref[...] = o + _rb(pn / den) * _rb(vn)[:, None, :]
    lane = lax.broadcasted_iota(jnp.int32, (LANES, LANES), 1)
    for b in range(bb):
        ok_ref[b] = jnp.where(lane == WINDOW - 1, knt_ref[:, b:b + 1], pltpu.roll(ck[b], WINDOW - 1, 1))
        ov_ref[b] = jnp.where(lane == WINDOW - 1, vnt_ref[:, b:b + 1], pltpu.roll(cv[b], WINDOW - 1, 1))


def attn_decode(qblk, proj_d, table, cache_k, cache_v, layer, bb):
    nbatch = proj_d.shape[0]
    a_w = A_HEADS * HEAD_DIM
    cols = lambda c: proj_d[:, c:c + LANES].reshape(nbatch // bb, bb, LANES).transpose(0, 2, 1)
    knt, vnt = cols(a_w), cols(a_w + LANES)
    kcol = (A_HEADS * HEAD_DIM) // LANES
    cache_spec = pl.BlockSpec((None, bb, LANES, WINDOW), lambda i: (layer, i, 0, 0))
    col_spec = pl.BlockSpec((None, LANES, bb), lambda i: (i, 0, 0))
    return pl.pallas_call(
        _attn_decode_kernel,
        grid=(nbatch // bb,),
        in_specs=[pl.BlockSpec((bb, A_HEADS, LANES), lambda i: (i, 0, 0)),
                  pl.BlockSpec((bb, LANES), lambda i: (i, kcol)),
                  pl.BlockSpec((bb, LANES), lambda i: (i, kcol + 1)),
                  col_spec, col_spec, cache_spec, cache_spec,
                  pl.BlockSpec((3, A_HEADS, WINDOW), lambda i: (0, 0, 0))],
        out_specs=[pl.BlockSpec((bb, A_HEADS, LANES), lambda i: (i, 0, 0)), cache_spec, cache_spec],
        out_shape=[jax.ShapeDtypeStruct((nbatch, A_HEADS, LANES), F32),
                   jax.ShapeDtypeStruct(cache_k.shape, F32),
                   jax.ShapeDtypeStruct(cache_v.shape, F32)],
        input_output_aliases={5: 1, 6: 2},
        compiler_params=_cparams(("arbitrary",)),
        name="attn_decode",
    )(qblk, proj_d, proj_d, knt, vnt, cache_k, cache_v, table)


def _xattn_decode_kernel(qb_ref, mk_ref, mv_ref, hm_ref, o_ref):
    qb = qb_ref[...]
    s = lax.dot_general(qb.astype(BF16), mk_ref[...].astype(BF16), (((2,), (1,)), ((0,), (0,))),
                        preferred_element_type=F32) * SCALE
    m = jnp.max(s, -1, keepdims=True)
    p = jnp.exp(s - m)
    p = p / jnp.sum(p, -1, keepdims=True)
    o = lax.dot_general(p.astype(BF16), mv_ref[...].astype(BF16), (((2,), (2,)), ((0,), (0,))),
                        preferred_element_type=F32)
    o_ref[...] = jnp.sum(o * hm_ref[...][None], axis=1)


def xattn_decode(qblk, mem_k, mem_v, hmask, layer, bb):
    nbatch = qblk.shape[0]
    mem_spec = pl.BlockSpec((None, bb, X_WIDTH, N_MEM), lambda i: (layer, i, 0, 0))
    return pl.pallas_call(
        _xattn_decode_kernel,
        grid=(nbatch // bb,),
        in_specs=[pl.BlockSpec((bb, X_QROWS, X_WIDTH), lambda i: (i, 0, 0)), mem_spec, mem_spec,
                  pl.BlockSpec((X_QROWS, X_WIDTH), lambda i: (0, 0))],
        out_specs=pl.BlockSpec((bb, X_WIDTH), lambda i: (i, 0)),
        out_shape=jax.ShapeDtypeStruct((nbatch, X_WIDTH), F32),
        compiler_params=_cparams(("parallel",)),
        name="xattn_decode",
    )(qblk, mem_k, mem_v, hmask)


def _hgrn_gates(hq, hf, loglb, log1mlb):
    ls = jnp.minimum(hf, 0.0) - jnp.log1p(jnp.exp(-jnp.abs(hf)))
    b = log1mlb + ls
    lf = jnp.maximum(loglb, b) + jnp.log1p(jnp.exp(-jnp.abs(loglb - b)))
    return _silu(hq), lf, _neg_expm1(lf)


def _hgrn_prompt_kernel(hq_ref, hf_ref, hi_ref, hg_ref, loglb_ref, log1mlb_ref, gain_ref,
                        bdm_ref, hm_ref, ob_ref, st_ref, st_scr, q_scr, k_scr, cum_scr, o_scr):
    i = pl.program_id(1)
    tt = hq_ref.shape[0]
    c = HG_CHUNK
    tb = HG_TB

    @pl.when(i == 0)
    def _():
        st_scr[...] = jnp.zeros_like(st_scr)

    qs, lf, kk = _hgrn_gates(hq_ref[...], hf_ref[...], loglb_ref[...], log1mlb_ref[...])
    row = lax.broadcasted_iota(jnp.int32, (tt, HG_WIDTH), 0) & (c - 1)
    cum = lf
    sh = 1
    while sh < c:
        cum = cum + jnp.where(row >= sh, pltpu.roll(cum, sh, 0), 0.0)
        sh *= 2
    q_scr[...] = qs
    k_scr[...] = kk
    cum_scr[...] = cum

    bdm = bdm_ref[...]
    hmask = hm_ref[...]
    def chunk(ci, carry):
        r0 = pl.multiple_of(ci * c, c)
        r = pl.ds(r0, c)
        cu = cum_scr[r, :]
        q = q_scr[r, :]
        k = k_scr[r, :]
        v = hi_ref[r, :]
        vb = v.astype(BF16)
        qb = _rb(q)
        cu2 = cu * LOG2E
        last = cu[c - 1:c, :]
        st = st_scr[...]
        o_inter = _bdot_nt(q * jnp.exp(cu), st)
        for j in range(c // tb):
            ns = tb * (j + 1)
            ti = lax.broadcasted_iota(jnp.int32, (tb, ns, HG_WIDTH), 0) + tb * j
            si = lax.broadcasted_iota(jnp.int32, (tb, ns, HG_WIDTH), 1)
            cut = cu2[tb * j:tb * (j + 1), :]
            dec = jnp.exp2(jnp.where(ti >= si, cut[:, None, :] - cu2[None, :ns, :], NEG))
            a2 = (dec * k[None, :ns, :]).astype(BF16)
            q4 = qb[tb * j:tb * (j + 1), None, :] * hmask[None, :, :]
            att = lax.dot_general(q4, a2, (((2,), (2,)), ((0,), (0,))),
                                  preferred_element_type=F32)
            w = jnp.dot(_rb(att.reshape(tb * HG_MROWS, ns)), vb[:ns, :],
                        preferred_element_type=F32).reshape(tb, HG_MROWS, HG_WIDTH)
            o_intra = jnp.sum(w * hmask[None, :, :], axis=1)
            o_scr[pl.ds(r0 + tb * j, tb), :] = o_intra + o_inter[tb * j:tb * (j + 1), :]
        upd = _bdot_tn(v, k * jnp.exp(last - cu))
        st_scr[...] = st * jnp.exp(last) + upd * bdm
        return carry

    lax.fori_loop(0, tt // c, chunk, 0, unroll=2)

    o = o_scr[...]
    ms = jnp.dot(o * o, bdm, precision=lax.Precision.HIGHEST,
                 preferred_element_type=F32) * (1.0 / HEAD_DIM)
    ob_ref[...] = o * lax.rsqrt(ms + RMS_EPS) * gain_ref[...] * _silu(hg_ref[...])

    @pl.when(i == pl.num_programs(1) - 1)
    def _():
        st_ref[...] = st_scr[...]


def hgrn_prompt(proj, loglb, log1mlb, gain4, bdm, hmask, batch, seq, tt):
    nt = seq // tt
    base = (A_HEADS + 2 * A_KV_HEADS) * HEAD_DIM // HG_WIDTH

    def col(cblk):
        return pl.BlockSpec((tt, HG_WIDTH), lambda b, i: (b * nt + i, cblk))

    row_spec = pl.BlockSpec((1, HG_WIDTH), lambda b, i: (0, 0))
    mat_spec = pl.BlockSpec((HG_WIDTH, HG_WIDTH), lambda b, i: (0, 0))
    return pl.pallas_call(
        _hgrn_prompt_kernel,
        grid=(batch, nt),
        in_specs=[col(base), col(base + 1), col(base + 2), col(base + 3),
                  row_spec, row_spec, row_spec, mat_spec,
                  pl.BlockSpec((HG_MROWS, HG_WIDTH), lambda b, i: (0, 0))],
        out_specs=[pl.BlockSpec((tt, HG_WIDTH), lambda b, i: (b * nt + i, 0)),
                   pl.BlockSpec((None, HG_WIDTH, HG_WIDTH), lambda b, i: (b, 0, 0))],
        out_shape=[jax.ShapeDtypeStruct((batch * seq, HG_WIDTH), F32),
                   jax.ShapeDtypeStruct((batch, HG_WIDTH, HG_WIDTH), F32)],
        scratch_shapes=[pltpu.VMEM((HG_WIDTH, HG_WIDTH), F32),
                        pltpu.VMEM((tt, HG_WIDTH), F32),
                        pltpu.VMEM((tt, HG_WIDTH), F32),
                        pltpu.VMEM((tt, HG_WIDTH), F32),
                        pltpu.VMEM((tt, HG_WIDTH), F32)],
        compiler_params=_cparams(("parallel", "arbitrary")),
        name="hgrn_prompt",
    )(proj, proj, proj, proj, loglb, log1mlb, gain4, bdm, hmask)


def _hgrn_decode_kernel(hq_ref, hf_ref, hi_ref, hg_ref, loglb_ref, log1mlb_ref, gain_ref, s_ref,
                        ob_ref, so_ref):
    nb = hq_ref.shape[1]
    qs, lf, kk = _hgrn_gates(hq_ref[...], hf_ref[...], loglb_ref[...], log1mlb_ref[...])
    v = hi_ref[...]
    f = jnp.exp(lf)
    s = s_ref[...].reshape(HEAD_DIM, HEAD_DIM, nb)
    att = jnp.sum(_rb(qs) * _rb(kk), axis=0, keepdims=True)
    o = _rb(att) * _rb(v) + jnp.sum(_rb(qs * f)[:, None, :] * _rb(s), axis=0)
    s_new = f[:, None, :] * s + _rb(kk)[:, None, :] * _rb(v)[None, :, :]
    so_ref[...] = s_new.reshape(HEAD_DIM * HEAD_DIM, nb)
    ms = jnp.mean(o * o, axis=0, keepdims=True)
    ob_ref[...] = o * lax.rsqrt(ms + RMS_EPS) * gain_ref[...] * _silu(hg_ref[...])


def hgrn_decode(gates_t, loglb_t, log1mlb_t, gain_t, state_t, layer):
    nb = gates_t.shape[1]

    def blk(off):
        return pl.BlockSpec((HEAD_DIM, nb), lambda h: (off * HG_HEADS + h, 0))

    par = pl.BlockSpec((HEAD_DIM, nb), lambda h: (h, 0))
    st = pl.BlockSpec((None, HEAD_DIM * HEAD_DIM, nb), lambda h: (layer, h, 0))
    return pl.pallas_call(
        _hgrn_decode_kernel,
        grid=(HG_HEADS,),
        in_specs=[blk(0), blk(1), blk(2), blk(3), par, par,
                  pl.BlockSpec((HEAD_DIM, nb), lambda h: (0, 0)), st],
        out_specs=[par, st],
        out_shape=[jax.ShapeDtypeStruct((HG_WIDTH, nb), F32),
                   jax.ShapeDtypeStruct(state_t.shape, F32)],
        input_output_aliases={7: 1},
        compiler_params=_cparams(("arbitrary",)),
        name="hgrn_decode",
    )(gates_t, gates_t, gates_t, gates_t, loglb_t, log1mlb_t, gain_t, state_t)


def _lru_gates(xc, wa_ref, ba_ref, wx_ref, bx_ref, lam_ref):
    r = jax.nn.sigmoid(_bdot(xc, wa_ref[...]) + ba_ref[...])
    gi = jax.nn.sigmoid(_bdot(xc, wx_ref[...]) + bx_ref[...])
    log_a = -LRU_C * r * _softplus(-lam_ref[...])
    a = jnp.exp(log_a)
    bterm = jnp.sqrt(_neg_expm1(2.0 * log_a)) * (gi * xc)
    return a, bterm


def _lru_prompt_kernel(lx_ref, lg_ref, cw_ref, cb_ref, wa_ref, ba_ref, wx_ref, bx_ref, lam_ref,
                       oc_ref, hl_ref, ext_scr, h_scr):
    i = pl.program_id(1)
    tt = lx_ref.shape[0]
    pad = 8

    @pl.when(i == 0)
    def _():
        ext_scr[0:pad, :] = jnp.zeros((pad, LRU_WIDTH), F32)
        h_scr[...] = jnp.zeros_like(h_scr)

    x = lx_ref[...]
    ext_scr[pad:pad + tt, :] = x
    xc = cb_ref[...] + cw_ref[CONV_W - 1:CONV_W, :] * x
    for j in range(CONV_W - 1):
        back = CONV_W - 1 - j
        xc = xc + cw_ref[j:j + 1, :] * ext_scr[pad - back:pad - back + tt, :]
    ext_scr[0:pad, :] = x[tt - pad:tt, :]

    a, bterm = _lru_gates(xc, wa_ref, ba_ref, wx_ref, bx_ref, lam_ref)
    row = lax.broadcasted_iota(jnp.int32, (tt, LRU_WIDTH), 0)
    sh = 1
    while sh < tt:
        keep = row >= sh
        b_s = jnp.where(keep, pltpu.roll(bterm, sh, 0), 0.0)
        a_s = jnp.where(keep, pltpu.roll(a, sh, 0), 1.0)
        bterm = a * b_s + bterm
        a = a * a_s
        sh *= 2
    h = a * h_scr[...] + bterm
    h_scr[...] = h[tt - 1:tt, :]
    oc_ref[...] = h * _gelu_tanh(lg_ref[...])

    @pl.when(i == pl.num_programs(1) - 1)
    def _():
        hl_ref[...] = h[tt - 1:tt, :]


def lru_prompt(proj, conv_w, conv_b, wa_bd, ba, wx_bd, bx, lam, batch, seq, tt):
    nt = seq // tt
    base = IN_COLS // LRU_WIDTH - 2

    def col(cblk):
        return pl.BlockSpec((tt, LRU_WIDTH), lambda b, i: (b * nt + i, cblk))

    row_spec = pl.BlockSpec((1, LRU_WIDTH), lambda b, i: (0, 0))
    mat_spec = pl.BlockSpec((LRU_WIDTH, LRU_WIDTH), lambda b, i: (0, 0))
    return pl.pallas_call(
        _lru_prompt_kernel,
        grid=(batch, nt),
        in_specs=[col(base), col(base + 1),
                  pl.BlockSpec((CONV_W, LRU_WIDTH), lambda b, i: (0, 0)), row_spec,
                  mat_spec, row_spec, mat_spec, row_spec, row_spec],
        out_specs=[pl.BlockSpec((tt, LRU_WIDTH), lambda b, i: (b * nt + i, 0)),
                   pl.BlockSpec((None, 1, LRU_WIDTH), lambda b, i: (b, 0, 0))],
        out_shape=[jax.ShapeDtypeStruct((batch * seq, LRU_WIDTH), F32),
                   jax.ShapeDtypeStruct((batch, 1, LRU_WIDTH), F32)],
        scratch_shapes=[pltpu.VMEM((tt + 8, LRU_WIDTH), F32),
                        pltpu.VMEM((1, LRU_WIDTH), F32)],
        compiler_params=_cparams(("parallel", "arbitrary")),
        name="lru_prompt",
    )(proj, proj, conv_w, conv_b, wa_bd, ba, wx_bd, bx, lam)


def _lru_decode_kernel(lx_ref, lg_ref, buf_ref, h0_ref, cw_ref, cb_ref, wa_ref, ba_ref, wx_ref,
                       bx_ref, lam_ref, oc_ref, hn_ref, nbuf_ref):
    x = lx_ref[...]
    buf = buf_ref[...]
    xc = cb_ref[...] + cw_ref[CONV_W - 1:CONV_W, :] * x
    for j in range(CONV_W - 1):
        xc = xc + cw_ref[j:j + 1, :] * buf[:, j * LRU_WIDTH:(j + 1) * LRU_WIDTH]
    a, bterm = _lru_gates(xc, wa_ref, ba_ref, wx_ref, bx_ref, lam_ref)
    h = a * h0_ref[...] + bterm
    hn_ref[...] = h
    oc_ref[...] = h * _gelu_tanh(lg_ref[...])
    nbuf_ref[...] = jnp.concatenate([buf[:, LRU_WIDTH:], x], axis=1)


def lru_decode(proj_d, conv_buf, h0, conv_w, conv_b, wa_bd, ba, wx_bd, bx, lam):
    nb = proj_d.shape[0]
    base = IN_COLS // LRU_WIDTH - 2
    row_spec = pl.BlockSpec((1, LRU_WIDTH), lambda i: (0, 0))
    mat_spec = pl.BlockSpec((LRU_WIDTH, LRU_WIDTH), lambda i: (0, 0))
    act = pl.BlockSpec((nb, LRU_WIDTH), lambda i: (0, 0))
    bufs = pl.BlockSpec((nb, (CONV_W - 1) * LRU_WIDTH), lambda i: (0, 0))
    return pl.pallas_call(
        _lru_decode_kernel,
        grid=(1,),
        in_specs=[pl.BlockSpec((nb, LRU_WIDTH), lambda i: (0, base)),
                  pl.BlockSpec((nb, LRU_WIDTH), lambda i: (0, base + 1)),
                  bufs, act, pl.BlockSpec((CONV_W, LRU_WIDTH), lambda i: (0, 0)), row_spec,
                  mat_spec, row_spec, mat_spec, row_spec, row_spec],
        out_specs=[act, act, bufs],
        out_shape=[jax.ShapeDtypeStruct((nb, LRU_WIDTH), F32),
                   jax.ShapeDtypeStruct((nb, LRU_WIDTH), F32),
                   jax.ShapeDtypeStruct((nb, (CONV_W - 1) * LRU_WIDTH), F32)],
        compiler_params=_cparams(("arbitrary",)),
        name="lru_decode",
    )(proj_d, proj_d, conv_buf, h0, conv_w, conv_b, wa_bd, ba, wx_bd, bx, lam)


def _xattn_prompt_kernel(x_ref, wq_ref, mk_ref, mv_ref, wo_ref, g_ref, b_ref, wr_ref, br_ref, o_ref):
    x = x_ref[...]
    q = _bdot(x, wq_ref[...]).astype(BF16)
    mk = mk_ref[...].astype(BF16)
    mv = mv_ref[...].astype(BF16)
    outs = []
    for h in range(X_HEADS):
        sl = slice(h * HEAD_DIM, (h + 1) * HEAD_DIM)
        s = _bdot_nt(q[:, sl], mk[:, sl]) * SCALE
        m = jnp.max(s, -1, keepdims=True)
        p = jnp.exp(s - m)
        p = p / jnp.sum(p, -1, keepdims=True)
        outs.append(_bdot(p, mv[:, sl]))
    o = jnp.concatenate(outs, axis=1)
    y = _layer_norm(ALPHA * x + _bdot(o, wo_ref[...]), g_ref[...], b_ref[...])
    logits = _bdot(y, wr_ref[...]) + br_ref[...]
    gate, g_idx = _route(logits)
    lane = lax.broadcasted_iota(jnp.int32, gate.shape, 1)
    for j in range(D_MODEL // LANES):
        o_ref[j] = y[:, j * LANES:(j + 1) * LANES]
    o_ref[D_MODEL // LANES] = jnp.where(lane == 0, g_idx.astype(F32), gate)


def xattn_prompt(x, wq, mem_kv, wo, g, b, wr, br, batch, seq, tt):
    nt = seq // tt
    const = lambda bi, i: (0, 0)
    return pl.pallas_call(
        _xattn_prompt_kernel,
        grid=(batch, nt),
        in_specs=[pl.BlockSpec((tt, D_MODEL), lambda bi, i: (bi * nt + i, 0)),
                  pl.BlockSpec((D_MODEL, X_WIDTH), const),
                  pl.BlockSpec((N_MEM, X_WIDTH), lambda bi, i: (bi, 0)),
                  pl.BlockSpec((N_MEM, X_WIDTH), lambda bi, i: (bi, 1)),
                  pl.BlockSpec((X_WIDTH, D_MODEL), const),
                  pl.BlockSpec((1, D_MODEL), const),
                  pl.BlockSpec((1, D_MODEL), const),
                  pl.BlockSpec((D_MODEL, ROUTER_LANES), const),
                  pl.BlockSpec((1, ROUTER_LANES), const)],
        out_specs=pl.BlockSpec((XG_PLANES, tt, LANES), lambda bi, i: (0, bi * nt + i, 0)),
        out_shape=jax.ShapeDtypeStruct((XG_PLANES, batch * seq, LANES), F32),
        compiler_params=_cparams(("parallel", "parallel")),
        name="xattn_prompt",
    )(x, wq, mem_kv, mem_kv, wo, g, b, wr, br)


def _route(logits):
    lane = lax.broadcasted_iota(jnp.int32, logits.shape, 1)
    big = jnp.int32(ROUTER_LANES)
    ninf = -jnp.inf
    gl = jnp.where(lane < N_GROUPS, logits, ninf)
    gm = jnp.max(gl, -1, keepdims=True)
    g_val = 1.0 / jnp.sum(jnp.exp(gl - gm), -1, keepdims=True)
    g_idx = jnp.min(jnp.where(gl == gm, lane, big), -1, keepdims=True)
    lo = N_GROUPS + EXP_PER_GROUP * g_idx
    el = jnp.where((lane >= lo) & (lane < lo + EXP_PER_GROUP), logits, ninf)
    v1 = jnp.max(el, -1, keepdims=True)
    i1 = jnp.min(jnp.where(el == v1, lane, big), -1, keepdims=True)
    el2 = jnp.where(lane == i1, ninf, el)
    v2 = jnp.max(el2, -1, keepdims=True)
    i2 = jnp.min(jnp.where(el2 == v2, lane, big), -1, keepdims=True)
    e2 = jnp.exp(v2 - v1)
    w1 = g_val / (1.0 + e2)
    w2 = g_val * e2 / (1.0 + e2)
    return jnp.where(lane == i1, w1, 0.0) + jnp.where(lane == i2, w2, 0.0), g_idx


def _moe_dense_kernel(x_ref, wr_ref, br_ref, wg_ref, wu_ref, wd_ref, g_ref, b_ref, o_ref,
                      gate_scr, acc_scr):
    e = pl.program_id(1)

    @pl.when(e == 0)
    def _():
        logits = _bdot(x_ref[...], wr_ref[...]) + br_ref[...]
        gate_scr[...] = _route(logits)[0]
        acc_scr[...] = jnp.zeros_like(acc_scr)

    xb = x_ref[...].astype(BF16)
    lane = lax.broadcasted_iota(jnp.int32, gate_scr.shape, 1)
    gcol = jnp.sum(jnp.where(lane == e + N_GROUPS, gate_scr[...], 0.0), -1, keepdims=True)
    hid = _silu(_bdot(xb, wg_ref[...])) * _bdot(xb, wu_ref[...])
    acc_scr[...] += _bdot(hid * gcol, wd_ref[...])

    @pl.when(e == pl.num_programs(1) - 1)
    def _():
        o_ref[...] = _layer_norm(ALPHA * x_ref[...] + acc_scr[...], g_ref[...], b_ref[...])


def moe_dense(x, wr, br, wg, wu, wd, g, b, layer, tm):
    m = x.shape[0]
    return pl.pallas_call(
        _moe_dense_kernel,
        grid=(m // tm, N_EXPERTS),
        in_specs=[pl.BlockSpec((tm, D_MODEL), lambda i, e: (i, 0)),
                  pl.BlockSpec((D_MODEL, ROUTER_LANES), lambda i, e: (0, 0)),
                  pl.BlockSpec((1, ROUTER_LANES), lambda i, e: (0, 0)),
                  pl.BlockSpec((None, None, D_MODEL, EXP_FF), lambda i, e: (layer, e, 0, 0)),
                  pl.BlockSpec((None, None, D_MODEL, EXP_FF), lambda i, e: (layer, e, 0, 0)),
                  pl.BlockSpec((None, None, EXP_FF, D_MODEL), lambda i, e: (layer, e, 0, 0)),
                  pl.BlockSpec((1, D_MODEL), lambda i, e: (0, 0)),
                  pl.BlockSpec((1, D_MODEL), lambda i, e: (0, 0))],
        out_specs=pl.BlockSpec((tm, D_MODEL), lambda i, e: (i, 0)),
        out_shape=jax.ShapeDtypeStruct((m, D_MODEL), F32),
        scratch_shapes=[pltpu.VMEM((tm, ROUTER_LANES), F32), pltpu.VMEM((tm, D_MODEL), F32)],
        compiler_params=_cparams(("parallel", "arbitrary")),
        name="moe_dense",
    )(x, wr, br, wg, wu, wd, g, b)


def _sc_mesh():
    return plsc.VectorSubcoreMesh(core_axis_name="core", subcore_axis_name="subcore")


def sc_scatter_rows(x, idx, n_out):
    r = x.shape[0]

    @functools.partial(pl.kernel, out_type=jax.ShapeDtypeStruct((n_out, LANES), x.dtype),
                       mesh=_sc_mesh(), scratch_types=[], name="sc_scatter_rows")
    def k(x_hbm, i_hbm, o_hbm):
        def body(x_vmem, i_vmem):
            pltpu.sync_copy(x_vmem, o_hbm.at[i_vmem.at[0]])

        pltpu.emit_pipeline(
            body,
            grid=(r // SC_WINDOW,),
            in_specs=[pl.BlockSpec((SC_WINDOW, LANES), lambda i: (i, 0)),
                      pl.BlockSpec((1, SC_WINDOW), lambda i: (0, i))],
            out_specs=[],
            core_axis_name=("core", "subcore"),
            dimension_semantics=(pltpu.PARALLEL,),
        )(x_hbm, i_hbm)

    return k(x, idx.reshape(1, r))


def sc_gather_rows(table, idx):
    r = idx.shape[0]

    @functools.partial(pl.kernel, out_type=jax.ShapeDtypeStruct((r, LANES), table.dtype),
                       mesh=_sc_mesh(), scratch_types=[], name="sc_gather_rows")
    def k(t_hbm, i_hbm, o_hbm):
        def body(i_vmem, o_vmem):
            pltpu.sync_copy(t_hbm.at[i_vmem.at[0]], o_vmem)

        pltpu.emit_pipeline(
            body,
            grid=(r // SC_WINDOW,),
            in_specs=[pl.BlockSpec((1, SC_WINDOW), lambda i: (0, i))],
            out_specs=[pl.BlockSpec((SC_WINDOW, LANES), lambda i: (i, 0))],
            core_axis_name=("core", "subcore"),
            dimension_semantics=(pltpu.PARALLEL,),
        )(i_hbm, o_hbm)

    return k(table, idx.reshape(1, r))


def _moe_sorted_kernel(tg_ref, nused_ref, xs_ref, wg32_ref, wu32_ref, wd32_ref, g_ref, b_ref, o_ref,
                       wg_ref, wu_ref, wd_ref):
    t = pl.program_id(0)
    used = t < nused_ref[0]

    @pl.when(used & ((t == 0) | (tg_ref[t] != tg_ref[jnp.maximum(t - 1, 0)])))
    def _():
        wg_ref[...] = wg32_ref[...].astype(BF16)
        wu_ref[...] = wu32_ref[...].astype(BF16)
        wd_ref[...] = wd32_ref[...].astype(BF16)

    @pl.when(used)
    def _():
        x = jnp.concatenate([xs_ref[j] for j in range(X_PLANES)], axis=1)
        gate = xs_ref[X_PLANES]
        xb = x.astype(BF16)
        lane = lax.broadcasted_iota(jnp.int32, gate.shape, 1)
        first = N_GROUPS + EXP_PER_GROUP * tg_ref[t]
        acc = jnp.zeros(x.shape, F32)
        for e in range(EXP_PER_GROUP):
            gcol = jnp.sum(jnp.where(lane == first + e, gate, 0.0), -1, keepdims=True)
            hid = _silu(_bdot(xb, wg_ref[e])) * _bdot(xb, wu_ref[e])
            acc = acc + _bdot(hid * gcol, wd_ref[e])
        y = _layer_norm(ALPHA * x + acc, g_ref[...], b_ref[...])
        for j in range(X_PLANES):
            o_ref[j] = y[:, j * LANES:(j + 1) * LANES]


def _group_slots(group_idx, n, tm):
    n_tiles = n // tm + N_GROUPS
    onehot = (group_idx[:, None] == jnp.arange(N_GROUPS)[None, :]).astype(jnp.int32)
    csum = jnp.cumsum(onehot, axis=0)
    counts = csum[-1]
    rank = jnp.sum(onehot * csum, axis=1) - 1
    tiles_g = (counts + tm - 1) // tm
    tile_end = jnp.cumsum(tiles_g)
    slot_base = (tile_end - tiles_g) * tm
    slot = (jnp.sum(onehot * slot_base[None, :], axis=1) + rank).astype(jnp.int32)
    tile_group = jnp.sum((jnp.arange(n_tiles)[:, None] >= tile_end[None, :]).astype(jnp.int32), axis=1)
    tile_group = jnp.minimum(tile_group, N_GROUPS - 1).astype(jnp.int32)
    return slot, tile_group, tile_end[-1:].astype(jnp.int32)


def moe_routed_sc(xg, wg, wu, wd, g, b, layer, n, tm, during_scatter, during_gather):
    slot, tile_group, n_used = _group_slots(xg[X_PLANES, :, 0].astype(jnp.int32), n, tm)
    n_tiles = tile_group.shape[0]
    n_slots = n_tiles * tm
    plane_base = lambda planes: jnp.arange(planes, dtype=jnp.int32)[:, None] * n_slots
    idx_in = (plane_base(XG_PLANES) + slot[None, :]).reshape(-1)
    idx_out = (plane_base(X_PLANES) + slot[None, :]).reshape(-1)
    xs = sc_scatter_rows(xg.reshape(XG_PLANES * n, LANES), idx_in, XG_PLANES * n_slots)
    xs = xs.reshape(XG_PLANES, n_slots, LANES)
    xs, side = lax.optimization_barrier((xs, during_scatter()))
    wspec = lambda shp: pl.BlockSpec((None, None, EXP_PER_GROUP) + shp,
                                     lambda t, tg, nu: (layer, tg[t], 0, 0, 0))
    grid_spec = pltpu.PrefetchScalarGridSpec(
        num_scalar_prefetch=2,
        grid=(n_tiles,),
        in_specs=[pl.BlockSpec((XG_PLANES, tm, LANES), lambda t, tg, nu: (0, t, 0)),
                  wspec((D_MODEL, EXP_FF)), wspec((D_MODEL, EXP_FF)), wspec((EXP_FF, D_MODEL)),
                  pl.BlockSpec((1, D_MODEL), lambda t, tg, nu: (0, 0)),
                  pl.BlockSpec((1, D_MODEL), lambda t, tg, nu: (0, 0))],
        out_specs=pl.BlockSpec((X_PLANES, tm, LANES), lambda t, tg, nu: (0, t, 0)),
        scratch_shapes=[pltpu.VMEM((EXP_PER_GROUP, D_MODEL, EXP_FF), BF16),
                        pltpu.VMEM((EXP_PER_GROUP, D_MODEL, EXP_FF), BF16),
                        pltpu.VMEM((EXP_PER_GROUP, EXP_FF, D_MODEL), BF16)])
    grouped = lambda w: w.reshape(w.shape[0], N_GROUPS, EXP_PER_GROUP, w.shape[2], w.shape[3])
    ys = pl.pallas_call(
        _moe_sorted_kernel,
        grid_spec=grid_spec,
        out_shape=jax.ShapeDtypeStruct((X_PLANES, n_slots, LANES), F32),
        compiler_params=_cparams(("arbitrary",)),
        name="moe_sorted",
    )(tile_group, n_used, xs, grouped(wg), grouped(wu), grouped(wd), g, b)
    y = sc_gather_rows(ys.reshape(X_PLANES * n_slots, LANES), idx_out)
    y, side = lax.optimization_barrier((y, during_gather(side)))
    return y.reshape(X_PLANES, n, LANES), side


def _t5_bucket(dist):
    max_exact = N_BUCKETS // 2
    d = jnp.maximum(dist, 0)
    df = jnp.maximum(d, 1).astype(F32)
    log_b = max_exact + (jnp.log(df / max_exact) / math.log(MAX_DISTANCE / max_exact)
                         * (N_BUCKETS - max_exact)).astype(jnp.int32)
    return jnp.where(d < max_exact, d, jnp.minimum(log_b, N_BUCKETS - 1))


def _bucket_lookup(rel_bias, bucket):
    out = jnp.zeros(bucket.shape + (rel_bias.shape[1],), F32)
    for i in range(N_BUCKETS):
        out = jnp.where((bucket == i)[..., None], rel_bias[i].astype(F32), out)
    return out


def _prompt_bias(rel_bias):
    qi = jnp.arange(WINDOW)[:, None]
    kj = jnp.arange(2 * WINDOW)[None, :]
    dist = qi + WINDOW - kj
    bias = _bucket_lookup(rel_bias, _t5_bucket(dist)).transpose(2, 0, 1)
    valid = (dist >= 0) & (dist <= WINDOW)
    return jnp.where(valid[None], bias, NEG)


def _decode_table(rel_bias, attn_sink):
    dist = WINDOW - jnp.arange(WINDOW + 1)
    bias = _bucket_lookup(rel_bias, _t5_bucket(dist)).T
    depth = attn_sink.shape[0]
    wide = lambda v: jnp.broadcast_to(v[..., None], v.shape + (WINDOW,))
    per_layer = lambda t: jnp.broadcast_to(t[None], (depth,) + t.shape)
    return jnp.stack([per_layer(bias[:, :WINDOW]), per_layer(wide(bias[:, WINDOW])),
                      wide(attn_sink.astype(F32))], axis=1)


def _block_ones(width):
    idx = jnp.arange(width) // HEAD_DIM
    return (idx[:, None] == idx[None, :])


def _head_rows_mask():
    head = jnp.arange(HG_WIDTH)[None, :] // HEAD_DIM
    return (head == jnp.arange(HG_MROWS)[:, None]).astype(F32)


def _block_diag(w):
    nblk, s, _ = w.shape
    eye = jnp.eye(nblk, dtype=w.dtype)
    return (eye[:, None, :, None] * w[:, :, None, :]).reshape(nblk * s, nblk * s)


def kernel(x_prompt, x_sample, mem_prompt, cache_win_k, cache_win_v, state_hgrn, state_conv, state_lru, cache_mem_k, cache_mem_v, rel_bias, hg_lb, w_in, attn_sink, hg_gain, conv_w, conv_b, lru_wa, lru_ba, lru_wx, lru_bx, lru_lam, w_out, ln1_g, ln1_b, x_wq, x_wk, x_wv, x_wo, ln2_g, ln2_b, r_gw, r_gb, r_ew, r_eb, e_wg, e_wu, e_wd, ln3_g, ln3_b):
    bp, seq, d = x_prompt.shape
    n_tok = bp * seq
    nd = x_sample.shape[0]
    depth = w_in.shape[0]

    lbs = jnp.cumsum(jax.nn.softmax(hg_lb.astype(F32), axis=0), axis=0)
    lbs = lbs - lbs[0]
    loglb = jnp.log(lbs)
    log1mlb = jnp.log1p(-lbs)
    gain4 = jnp.tile(hg_gain, (1, HG_HEADS))

    bias_p = _prompt_bias(rel_bias)
    bdm256 = _block_ones(HG_WIDTH).astype(F32)
    hmask = _head_rows_mask()

    w_in_b = w_in.astype(BF16)
    w_out_b = w_out.astype(BF16)
    wq_b = x_wq.astype(BF16)
    wkv_b = jnp.concatenate([x_wk, x_wv], axis=-1).astype(BF16)
    wo_b = x_wo.astype(BF16)
    rew = r_ew.transpose(0, 2, 1, 3).reshape(depth, d, N_EXPERTS)
    wr = jnp.concatenate([r_gw, rew, jnp.zeros((depth, d, ROUTER_LANES - N_GROUPS - N_EXPERTS), F32)], -1)
    br = jnp.concatenate([r_gb, r_eb.reshape(depth, N_EXPERTS),
                          jnp.zeros((depth, ROUTER_LANES - N_GROUPS - N_EXPERTS), F32)], -1)

    a_w = A_HEADS * HEAD_DIM
    xp = x_prompt.reshape(bp * seq, d)
    xs = x_sample.reshape(nd, d)
    mem = mem_prompt.reshape(bp * N_MEM, d)
    ckt = cache_win_k.transpose(0, 1, 3, 4, 2).reshape(depth, nd, LANES, WINDOW)
    cvt = cache_win_v.transpose(0, 1, 3, 4, 2).reshape(depth, nd, LANES, WINDOW)
    cmkt = cache_mem_k.transpose(0, 1, 3, 4, 2).reshape(depth, nd, X_WIDTH, N_MEM)
    cmvt = cache_mem_v.transpose(0, 1, 3, 4, 2).reshape(depth, nd, X_WIDTH, N_MEM)
    state_t = state_hgrn.transpose(0, 2, 3, 4, 1).reshape(depth, HG_HEADS * HEAD_DIM * HEAD_DIM, nd)
    dec_tab = _decode_table(rel_bias, attn_sink)
    xq_mask = _head_rows_mask()[:X_QROWS]
    head_group = jnp.arange(A_HEADS) // A_REP

    p_wk, p_wv, p_s, p_cb, p_h, p_mk, p_mv = [], [], [], [], [], [], []
    s_cb, s_h = [], []
    for l in range(depth):
        row = lambda v: v[l].reshape(1, -1)
        wa_bd = _block_diag(lru_wa[l]).astype(BF16)
        wx_bd = _block_diag(lru_wx[l]).astype(BF16)
        lru_args = (conv_w[l], row(conv_b), wa_bd, row(lru_ba), wx_bd, row(lru_bx), row(lru_lam))
        wo_parts = [w_out_b[l, :a_w], w_out_b[l, a_w:a_w + HG_WIDTH], w_out_b[l, a_w + HG_WIDTH:]]

        proj = matmul(xp, w_in_b[l], 1024, 768)
        oa = attn_prompt(proj, attn_sink[l], bias_p, bp, seq)
        ob, st = hgrn_prompt(proj, row(loglb), row(log1mlb), row(gain4), bdm256, hmask, bp, seq, 512)
        oc, hl = lru_prompt(proj, *lru_args, bp, seq, 512)
        xp = proj_res_ln(xp, [oa, ob, oc], wo_parts, row(ln1_g), row(ln1_b), 1024)
        mkv = matmul(mem, wkv_b[l], 256, 512)
        xg = xattn_prompt(xp, wq_b[l], mkv, wo_b[l], row(ln2_g), row(ln2_b), wr[l], br[l:l + 1],
                          bp, seq, 1024)
        def decode_mixers(xs=xs, ckt=ckt, cvt=cvt, state_t=state_t, l=l, lru_args=lru_args,
                          wo_parts=wo_parts, row=row):
            projd = matmul(xs, w_in_b[l], nd, 768)
            q3 = projd[:, :a_w].reshape(nd, A_HEADS, 1, HEAD_DIM)
            on_group = head_group[None, :, None, None] == jnp.arange(A_KV_HEADS)[None, None, :, None]
            qblk = jnp.where(on_group, q3, 0.0).reshape(nd, A_HEADS, LANES)
            o3, ckt, cvt = attn_decode(qblk, projd, dec_tab[l], ckt, cvt, l, 16)
            o4 = o3.reshape(nd, A_HEADS, A_KV_HEADS, HEAD_DIM)
            oa = jnp.sum(jnp.where(on_group, o4, 0.0), axis=2).reshape(nd, a_w)
            gates_t = projd[:, a_w + 2 * LANES:a_w + 2 * LANES + 4 * HG_WIDTH].T
            bc = lambda v: jnp.broadcast_to(v[:, None], (v.shape[0], nd))
            ob_t, state_t = hgrn_decode(gates_t, bc(loglb[l]), bc(log1mlb[l]), bc(hg_gain[l]), state_t, l)
            oc, nh, nbuf = lru_decode(projd, state_conv[l].reshape(nd, -1), state_lru[l], *lru_args)
            xs = proj_res_ln(xs, [oa, ob_t.T, oc], wo_parts, row(ln1_g), row(ln1_b), nd)
            return xs, ckt, cvt, state_t, nh, nbuf

        def decode_rest(side, l=l, row=row):
            xs = side[0]
            qd = matmul(xs, wq_b[l], nd, X_WIDTH)
            qdb = qd[:, None, :] * xq_mask[None, :, :]
            od = xattn_decode(qdb, cmkt, cmvt, xq_mask, l, 16)
            xs = proj_res_ln(xs, [od], [wo_b[l]], row(ln2_g), row(ln2_b), nd)
            xs = moe_dense(xs, wr[l], br[l:l + 1], e_wg, e_wu, e_wd, row(ln3_g), row(ln3_b), l, nd)
            return (xs,) + tuple(side[1:])

        xp, (xs, ckt, cvt, state_t, nh, nbuf) = moe_routed_sc(
            xg, e_wg, e_wu, e_wd, row(ln3_g), row(ln3_b), l, n_tok, MOE_TM, decode_mixers, decode_rest)

        proj3 = proj.reshape(bp, seq, IN_COLS)
        p_wk.append(proj3[:, seq - WINDOW:, a_w:a_w + LANES].reshape(bp, WINDOW, A_KV_HEADS, HEAD_DIM))
        p_wv.append(proj3[:, seq - WINDOW:, a_w + LANES:a_w + 2 * LANES].reshape(bp, WINDOW, A_KV_HEADS, HEAD_DIM))
        st5 = st.reshape(bp, HG_HEADS, HEAD_DIM, HG_HEADS, HEAD_DIM)
        p_s.append(jnp.stack([st5[:, h, :, h, :] for h in range(HG_HEADS)], 1).transpose(0, 1, 3, 2))
        p_cb.append(proj3[:, seq - (CONV_W - 1):, IN_COLS - 2 * LRU_WIDTH:IN_COLS - LRU_WIDTH])
        p_h.append(hl.reshape(bp, LRU_WIDTH))
        p_mk.append(mkv[:, :X_WIDTH].reshape(bp, N_MEM, X_HEADS, HEAD_DIM))
        p_mv.append(mkv[:, X_WIDTH:].reshape(bp, N_MEM, X_HEADS, HEAD_DIM))

        s_cb.append(nbuf.reshape(nd, CONV_W - 1, LRU_WIDTH))
        s_h.append(nh)

    xp = jnp.transpose(xp, (1, 0, 2))
    unkey = lambda c: c.reshape(depth, nd, A_KV_HEADS, HEAD_DIM, WINDOW).transpose(0, 1, 4, 2, 3)
    s_s = state_t.reshape(depth, HG_HEADS, HEAD_DIM, HEAD_DIM, nd).transpose(0, 4, 1, 2, 3)
    return (xp.reshape(bp, seq, d), xs.reshape(nd, 1, d),
            jnp.stack(p_wk), jnp.stack(p_wv), jnp.stack(p_s), jnp.stack(p_cb), jnp.stack(p_h),
            jnp.stack(p_mk), jnp.stack(p_mv),
            unkey(ckt), unkey(cvt), s_s, jnp.stack(s_cb), jnp.stack(s_h))
```

```python
import functools
import math

import jax
import jax.numpy as jnp
from jax import lax
from jax.experimental import pallas as pl
from jax.experimental.pallas import tpu as pltpu
from jax.experimental.pallas import tpu_sc as plsc

F32 = jnp.float32
BF16 = jnp.bfloat16

D_MODEL = 1024
DEPTH = 4
HEAD_DIM = 64
A_HEADS = 8
A_KV_HEADS = 2
A_REP = A_HEADS // A_KV_HEADS
WINDOW = 128
A_QB = 2
N_BUCKETS = 32
MAX_DISTANCE = 128
HG_WIDTH = 256
HG_HEADS = 4
HG_CHUNK = 64
HG_TB = 16
HG_MROWS = 8
LOG2E = math.log2(math.e)
LRU_WIDTH = 256
LRU_BLOCKS = 4
CONV_W = 4
LRU_C = 8.0
N_MEM = 256
X_HEADS = 4
X_WIDTH = X_HEADS * HEAD_DIM
X_QROWS = 8
N_GROUPS = 4
EXP_PER_GROUP = 4
N_EXPERTS = N_GROUPS * EXP_PER_GROUP
EXP_FF = D_MODEL // 4
ALPHA = (2 * DEPTH) ** 0.25
LN_EPS = 1e-5
RMS_EPS = 1e-6
IN_COLS = 2304
SCALE = HEAD_DIM ** -0.5
NEG = -1e30
LANES = 128
ROUTER_LANES = 128
XG_WIDTH = D_MODEL + ROUTER_LANES
XG_PLANES = XG_WIDTH // LANES
X_PLANES = D_MODEL // LANES
MOE_TM = 512
SC_WINDOW = 128
VMEM_LIMIT = 48 * 1024 * 1024


def _cparams(sem):
    return pltpu.CompilerParams(dimension_semantics=sem, vmem_limit_bytes=VMEM_LIMIT)


def _bdot(a, b):
    return jnp.dot(a.astype(BF16), b.astype(BF16), preferred_element_type=F32)


def _bdot_nt(a, b):
    return lax.dot_general(a.astype(BF16), b.astype(BF16), (((1,), (1,)), ((), ())),
                           preferred_element_type=F32)


def _bdot_tn(a, b):
    return lax.dot_general(a.astype(BF16), b.astype(BF16), (((0,), (0,)), ((), ())),
                           preferred_element_type=F32)


def _rb(x):
    return x.astype(BF16).astype(F32)


def _silu(x):
    return x * jax.nn.sigmoid(x)


def _neg_expm1(x):
    return -jnp.tanh(0.5 * x) * (jnp.exp(x) + 1.0)


def _softplus(x):
    return jnp.maximum(x, 0.0) + jnp.log1p(jnp.exp(-jnp.abs(x)))


def _gelu_tanh(x):
    return 0.5 * x * (1.0 + jnp.tanh(math.sqrt(2.0 / math.pi) * (x + 0.044715 * (x * x * x))))


def _layer_norm(y, g, b):
    mu = jnp.mean(y, -1, keepdims=True)
    yc = y - mu
    var = jnp.mean(yc * yc, -1, keepdims=True)
    return yc * lax.rsqrt(var + LN_EPS) * g + b


def _rows(x_ref):
    if len(x_ref.shape) == 2:
        return x_ref[...]
    return jnp.concatenate([x_ref[j] for j in range(x_ref.shape[0])], axis=1)


def _rows_spec(x, tm, nargs):
    if x.ndim == 2:
        return pl.BlockSpec((tm, x.shape[1]), (lambda i: (i, 0)) if nargs == 1 else (lambda i, j: (i, 0)))
    blk = (x.shape[0], tm, LANES)
    return pl.BlockSpec(blk, (lambda i: (0, i, 0)) if nargs == 1 else (lambda i, j: (0, i, 0)))


def _mm_kernel(x_ref, w_ref, o_ref):
    o_ref[...] = _bdot(_rows(x_ref), w_ref[...])


def matmul(x, w, tm, tn):
    m = x.shape[-2]
    k, n = w.shape
    return pl.pallas_call(
        _mm_kernel,
        grid=(m // tm, n // tn),
        in_specs=[_rows_spec(x, tm, 2),
                  pl.BlockSpec((k, tn), lambda i, j: (0, j))],
        out_specs=pl.BlockSpec((tm, tn), lambda i, j: (i, j)),
        out_shape=jax.ShapeDtypeStruct((m, n), F32),
        compiler_params=_cparams(("parallel", "parallel")),
        name="matmul",
    )(x, w)


def _proj_res_ln_kernel(n_in, x_ref, *refs):
    a_refs = refs[:n_in]
    w_refs = refs[n_in:2 * n_in]
    g_ref, b_ref, o_ref = refs[2 * n_in:]
    y = ALPHA * _rows(x_ref)
    for a_ref, w_ref in zip(a_refs, w_refs):
        y = y + _bdot(a_ref[...], w_ref[...])
    o_ref[...] = _layer_norm(y, g_ref[...], b_ref[...])


def proj_res_ln(x, a_list, w_list, g, b, tm):
    m = x.shape[-2]
    d = w_list[0].shape[1]
    n_in = len(a_list)
    in_specs = [_rows_spec(x, tm, 1)]
    in_specs += [pl.BlockSpec((tm, a.shape[1]), lambda i: (i, 0)) for a in a_list]
    in_specs += [pl.BlockSpec(w.shape, lambda i: (0, 0)) for w in w_list]
    in_specs += [pl.BlockSpec((1, d), lambda i: (0, 0))] * 2
    return pl.pallas_call(
        functools.partial(_proj_res_ln_kernel, n_in),
        grid=(m // tm,),
        in_specs=in_specs,
        out_specs=pl.BlockSpec((tm, d), lambda i: (i, 0)),
        out_shape=jax.ShapeDtypeStruct((m, d), F32),
        compiler_params=_cparams(("parallel",)),
        name="proj_res_ln",
    )(x, *a_list, *w_list, g, b)


def _attn_prompt_kernel(sink_ref, q_ref, kc_ref, kp_ref, vc_ref, vp_ref, bias_ref, o_ref):
    n = pl.program_id(1)
    col = lax.broadcasted_iota(jnp.int32, (WINDOW, 2 * WINDOW), 1)
    first = jnp.where((n == 0) & (col < WINDOW), NEG, 0.0)
    kk = jnp.concatenate([kp_ref[...], kc_ref[...]], axis=0).astype(BF16)
    vv = jnp.concatenate([vp_ref[...], vc_ref[...]], axis=0).astype(BF16)
    q = q_ref[...].astype(BF16)
    for u in range(A_QB):
        rows = slice(u * WINDOW, (u + 1) * WINDOW)
        keys = slice(u * WINDOW, (u + 2) * WINDOW)
        outs = []
        for h in range(A_HEADS):
            g = h // A_REP
            qh = q[rows, h * HEAD_DIM:(h + 1) * HEAD_DIM]
            kg = kk[keys, g * HEAD_DIM:(g + 1) * HEAD_DIM]
            vg = vv[keys, g * HEAD_DIM:(g + 1) * HEAD_DIM]
            s = _bdot_nt(qh, kg) * SCALE + bias_ref[h]
            if u == 0:
                s = s + first
            sink = sink_ref[h]
            m = jnp.maximum(jnp.max(s, -1, keepdims=True), sink)
            p = jnp.exp(s - m)
            den = jnp.sum(p, -1, keepdims=True) + jnp.exp(sink - m)
            outs.append(_bdot(p / den, vg))
        o_ref[rows, :] = jnp.concatenate(outs, axis=1)


def attn_prompt(proj, sink, bias, batch, seq):
    nb = seq // WINDOW
    ns = nb // A_QB
    tq = A_QB * WINDOW
    qcol = 0
    kcol = (A_HEADS * HEAD_DIM) // LANES
    vcol = kcol + 1

    def cur(c):
        return lambda b, n: (b * ns + n, c)

    def prev(c):
        return lambda b, n: (b * nb + jnp.maximum(n * A_QB - 1, 0), c)

    return pl.pallas_call(
        _attn_prompt_kernel,
        grid=(batch, ns),
        in_specs=[pl.BlockSpec(memory_space=pltpu.SMEM),
                  pl.BlockSpec((tq, A_HEADS * HEAD_DIM), cur(qcol)),
                  pl.BlockSpec((tq, LANES), cur(kcol)),
                  pl.BlockSpec((WINDOW, LANES), prev(kcol)),
                  pl.BlockSpec((tq, LANES), cur(vcol)),
                  pl.BlockSpec((WINDOW, LANES), prev(vcol)),
                  pl.BlockSpec((A_HEADS, WINDOW, 2 * WINDOW), lambda b, n: (0, 0, 0))],
        out_specs=pl.BlockSpec((tq, A_HEADS * HEAD_DIM), cur(0)),
        out_shape=jax.ShapeDtypeStruct((batch * seq, A_HEADS * HEAD_DIM), F32),
        compiler_params=_cparams(("parallel", "parallel")),
        name="attn_prompt",
    )(sink, proj, proj, proj, proj, proj, bias)


def _attn_decode_kernel(qb_ref, kn_ref, vn_ref, knt_ref, vnt_ref, ck_ref, cv_ref, tab_ref,
                        o_ref, ok_ref, ov_ref):
    bb = qb_ref.shape[0]
    ck = ck_ref[...]
    cv = cv_ref[...]
    qb = qb_ref[...]
    kn = kn_ref[...]
    vn = vn_ref[...]
    bias_j = tab_ref[0]
    bias_n = tab_ref[1][:, 0:1]
    sink = tab_ref[2][:, 0:1]
    s = lax.dot_general(qb.astype(BF16), ck.astype(BF16), (((2,), (1,)), ((0,), (0,))),
                        preferred_element_type=F32) * SCALE + bias_j[None]
    sn = jnp.sum(_rb(qb) * _rb(kn)[:, None, :], -1, keepdims=True) * SCALE + bias_n[None]
    m = jnp.maximum(jnp.maximum(jnp.max(s, -1, keepdims=True), sn), sink[None])
    p = jnp.exp(s - m)
    pn = jnp.exp(sn - m)
    den = jnp.sum(p, -1, keepdims=True) + pn + jnp.exp(sink[None] - m)
    o = lax.dot_general((p / den).astype(BF16), cv.astype(BF16), (((2,), (2,)), ((0,), (0,))),
                        preferred_element_type=F32)
    o_ref[...] = o + _rb(pn / den) * _rb(vn)[:, None, :]
    lane = lax.broadcasted_iota(jnp.int32, (LANES, LANES), 1)
    for b in range(bb):
        ok_ref[b] = jnp.where(lane == WINDOW - 1, knt_ref[:, b:b + 1], pltpu.roll(ck[b], WINDOW - 1, 1))
        ov_ref[b] = jnp.where(lane == WINDOW - 1, vnt_ref[:, b:b + 1], pltpu.roll(cv[b], WINDOW - 1, 1))


def attn_decode(qblk, proj_d, table, cache_k, cache_v, layer, bb):
    nbatch = proj_d.shape[0]
    a_w = A_HEADS * HEAD_DIM
    cols = lambda c: proj_d[:, c:c + LANES].reshape(nbatch // bb, bb, LANES).transpose(0, 2, 1)
    knt, vnt = cols(a_w), cols(a_w + LANES)
    kcol = (A_HEADS * HEAD_DIM) // LANES
    cache_spec = pl.BlockSpec((None, bb, LANES, WINDOW), lambda i: (layer, i, 0, 0))
    col_spec = pl.BlockSpec((None, LANES, bb), lambda i: (i, 0, 0))
    return pl.pallas_call(
        _attn_decode_kernel,
        grid=(nbatch // bb,),
        in_specs=[pl.BlockSpec((bb, A_HEADS, LANES), lambda i: (i, 0, 0)),
                  pl.BlockSpec((bb, LANES), lambda i: (i, kcol)),
                  pl.BlockSpec((bb, LANES), lambda i: (i, kcol + 1)),
                  col_spec, col_spec, cache_spec, cache_spec,
                  pl.BlockSpec((3, A_HEADS, WINDOW), lambda i: (0, 0, 0))],
        out_specs=[pl.BlockSpec((bb, A_HEADS, LANES), lambda i: (i, 0, 0)), cache_spec, cache_spec],
        out_shape=[jax.ShapeDtypeStruct((nbatch, A_HEADS, LANES), F32),
                   jax.ShapeDtypeStruct(cache_k.shape, F32),
                   jax.ShapeDtypeStruct(cache_v.shape, F32)],
        input_output_aliases={5: 1, 6: 2},
        compiler_params=_cparams(("arbitrary",)),
        name="attn_decode",
    )(qblk, proj_d, proj_d, knt, vnt, cache_k, cache_v, table)


def _xattn_decode_kernel(qb_ref, mk_ref, mv_ref, hm_ref, o_ref):
    qb = qb_ref[...]
    s = lax.dot_general(qb.astype(BF16), mk_ref[...].astype(BF16), (((2,), (1,)), ((0,), (0,))),
                        preferred_element_type=F32) * SCALE
    m = jnp.max(s, -1, keepdims=True)
    p = jnp.exp(s - m)
    p = p / jnp.sum(p, -1, keepdims=True)
    o = lax.dot_general(p.astype(BF16), mv_ref[...].astype(BF16), (((2,), (2,)), ((0,), (0,))),
                        preferred_element_type=F32)
    o_ref[...] = jnp.sum(o * hm_ref[...][None], axis=1)


def xattn_decode(qblk, mem_k, mem_v, hmask, layer, bb):
    nbatch = qblk.shape[0]
    mem_spec = pl.BlockSpec((None, bb, X_WIDTH, N_MEM), lambda i: (layer, i, 0, 0))
    return pl.pallas_call(
        _xattn_decode_kernel,
        grid=(nbatch // bb,),
        in_specs=[pl.BlockSpec((bb, X_QROWS, X_WIDTH), lambda i: (i, 0, 0)), mem_spec, mem_spec,
                  pl.BlockSpec((X_QROWS, X_WIDTH), lambda i: (0, 0))],
        out_specs=pl.BlockSpec((bb, X_WIDTH), lambda i: (i, 0)),
        out_shape=jax.ShapeDtypeStruct((nbatch, X_WIDTH), F32),
        compiler_params=_cparams(("parallel",)),
        name="xattn_decode",
    )(qblk, mem_k, mem_v, hmask)


def _hgrn_gates(hq, hf, loglb, log1mlb):
    ls = jnp.minimum(hf, 0.0) - jnp.log1p(jnp.exp(-jnp.abs(hf)))
    b = log1mlb + ls
    lf = jnp.maximum(loglb, b) + jnp.log1p(jnp.exp(-jnp.abs(loglb - b)))
    return _silu(hq), lf, _neg_expm1(lf)


def _hgrn_prompt_kernel(hq_ref, hf_ref, hi_ref, hg_ref, loglb_ref, log1mlb_ref, gain_ref,
                        bdm_ref, hm_ref, ob_ref, st_ref, st_scr, q_scr, k_scr, cum_scr, o_scr):
    i = pl.program_id(1)
    tt = hq_ref.shape[0]
    c = HG_CHUNK
    tb = HG_TB

    @pl.when(i == 0)
    def _():
        st_scr[...] = jnp.zeros_like(st_scr)

    qs, lf, kk = _hgrn_gates(hq_ref[...], hf_ref[...], loglb_ref[...], log1mlb_ref[...])
    row = lax.broadcasted_iota(jnp.int32, (tt, HG_WIDTH), 0) & (c - 1)
    cum = lf
    sh = 1
    while sh < c:
        cum = cum + jnp.where(row >= sh, pltpu.roll(cum, sh, 0), 0.0)
        sh *= 2
    q_scr[...] = qs
    k_scr[...] = kk
    cum_scr[...] = cum

    bdm = bdm_ref[...]
    hmask = hm_ref[...]
    def chunk(ci, carry):
        r0 = pl.multiple_of(ci * c, c)
        r = pl.ds(r0, c)
        cu = cum_scr[r, :]
        q = q_scr[r, :]
        k = k_scr[r, :]
        v = hi_ref[r, :]
        vb = v.astype(BF16)
        qb = _rb(q)
        cu2 = cu * LOG2E
        last = cu[c - 1:c, :]
        st = st_scr[...]
        o_inter = _bdot_nt(q * jnp.exp(cu), st)
        for j in range(c // tb):
            ns = tb * (j + 1)
            ti = lax.broadcasted_iota(jnp.int32, (tb, ns, HG_WIDTH), 0) + tb * j
            si = lax.broadcasted_iota(jnp.int32, (tb, ns, HG_WIDTH), 1)
            cut = cu2[tb * j:tb * (j + 1), :]
            dec = jnp.exp2(jnp.where(ti >= si, cut[:, None, :] - cu2[None, :ns, :], NEG))
            a2 = (dec * k[None, :ns, :]).astype(BF16)
            q4 = qb[tb * j:tb * (j + 1), None, :] * hmask[None, :, :]
            att = lax.dot_general(q4, a2, (((2,), (2,)), ((0,), (0,))),
                                  preferred_element_type=F32)
            w = jnp.dot(_rb(att.reshape(tb * HG_MROWS, ns)), vb[:ns, :],
                        preferred_element_type=F32).reshape(tb, HG_MROWS, HG_WIDTH)
            o_intra = jnp.sum(w * hmask[None, :, :], axis=1)
            o_scr[pl.ds(r0 + tb * j, tb), :] = o_intra + o_inter[tb * j:tb * (j + 1), :]
        upd = _bdot_tn(v, k * jnp.exp(last - cu))
        st_scr[...] = st * jnp.exp(last) + upd * bdm
        return carry

    lax.fori_loop(0, tt // c, chunk, 0, unroll=2)

    o = o_scr[...]
    ms = jnp.dot(o * o, bdm, precision=lax.Precision.HIGHEST,
                 preferred_element_type=F32) * (1.0 / HEAD_DIM)
    ob_ref[...] = o * lax.rsqrt(ms + RMS_EPS) * gain_ref[...] * _silu(hg_ref[...])

    @pl.when(i == pl.num_programs(1) - 1)
    def _():
        st_ref[...] = st_scr[...]


def hgrn_prompt(proj, loglb, log1mlb, gain4, bdm, hmask, batch, seq, tt):
    nt = seq // tt
    base = (A_HEADS + 2 * A_KV_HEADS) * HEAD_DIM // HG_WIDTH

    def col(cblk):
        return pl.BlockSpec((tt, HG_WIDTH), lambda b, i: (b * nt + i, cblk))

    row_spec = pl.BlockSpec((1, HG_WIDTH), lambda b, i: (0, 0))
    mat_spec = pl.BlockSpec((HG_WIDTH, HG_WIDTH), lambda b, i: (0, 0))
    return pl.pallas_call(
        _hgrn_prompt_kernel,
        grid=(batch, nt),
        in_specs=[col(base), col(base + 1), col(base + 2), col(base + 3),
                  row_spec, row_spec, row_spec, mat_spec,
                  pl.BlockSpec((HG_MROWS, HG_WIDTH), lambda b, i: (0, 0))],
        out_specs=[pl.BlockSpec((tt, HG_WIDTH), lambda b, i: (b * nt + i, 0)),
                   pl.BlockSpec((None, HG_WIDTH, HG_WIDTH), lambda b, i: (b, 0, 0))],
        out_shape=[jax.ShapeDtypeStruct((batch * seq, HG_WIDTH), F32),
                   jax.ShapeDtypeStruct((batch, HG_WIDTH, HG_WIDTH), F32)],
        scratch_shapes=[pltpu.VMEM((HG_WIDTH, HG_WIDTH), F32),
                        pltpu.VMEM((tt, HG_WIDTH), F32),
                        pltpu.VMEM((tt, HG_WIDTH), F32),
                        pltpu.VMEM((tt, HG_WIDTH), F32),
                        pltpu.VMEM((tt, HG_WIDTH), F32)],
        compiler_params=_cparams(("parallel", "arbitrary")),
        name="hgrn_prompt",
    )(proj, proj, proj, proj, loglb, log1mlb, gain4, bdm, hmask)


def _hgrn_decode_kernel(hq_ref, hf_ref, hi_ref, hg_ref, loglb_ref, log1mlb_ref, gain_ref, s_ref,
                        ob_ref, so_ref):
    nb = hq_ref.shape[1]
    qs, lf, kk = _hgrn_gates(hq_ref[...], hf_ref[...], loglb_ref[...], log1mlb_ref[...])
    v = hi_ref[...]
    f = jnp.exp(lf)
    s = s_ref[...].reshape(HEAD_DIM, HEAD_DIM, nb)
    att = jnp.sum(_rb(qs) * _rb(kk), axis=0, keepdims=True)
    o = _rb(att) * _rb(v) + jnp.sum(_rb(qs * f)[:, None, :] * _rb(s), axis=0)
    s_new = f[:, None, :] * s + _rb(kk)[:, None, :] * _rb(v)[None, :, :]
    so_ref[...] = s_new.reshape(HEAD_DIM * HEAD_DIM, nb)
    ms = jnp.mean(o * o, axis=0, keepdims=True)
    ob_ref[...] = o * lax.rsqrt(ms + RMS_EPS) * gain_ref[...] * _silu(hg_ref[...])


def hgrn_decode(gates_t, loglb_t, log1mlb_t, gain_t, state_t, layer):
    nb = gates_t.shape[1]

    def blk(off):
        return pl.BlockSpec((HEAD_DIM, nb), lambda h: (off * HG_HEADS + h, 0))

    par = pl.BlockSpec((HEAD_DIM, nb), lambda h: (h, 0))
    st = pl.BlockSpec((None, HEAD_DIM * HEAD_DIM, nb), lambda h: (layer, h, 0))
    return pl.pallas_call(
        _hgrn_decode_kernel,
        grid=(HG_HEADS,),
        in_specs=[blk(0), blk(1), blk(2), blk(3), par, par,
                  pl.BlockSpec((HEAD_DIM, nb), lambda h: (0, 0)), st],
        out_specs=[par, st],
        out_shape=[jax.ShapeDtypeStruct((HG_WIDTH, nb), F32),
                   jax.ShapeDtypeStruct(state_t.shape, F32)],
        input_output_aliases={7: 1},
        compiler_params=_cparams(("arbitrary",)),
        name="hgrn_decode",
    )(gates_t, gates_t, gates_t, gates_t, loglb_t, log1mlb_t, gain_t, state_t)


def _lru_gates(xc, wa_ref, ba_ref, wx_ref, bx_ref, lam_ref):
    r = jax.nn.sigmoid(_bdot(xc, wa_ref[...]) + ba_ref[...])
    gi = jax.nn.sigmoid(_bdot(xc, wx_ref[...]) + bx_ref[...])
    log_a = -LRU_C * r * _softplus(-lam_ref[...])
    a = jnp.exp(log_a)
    bterm = jnp.sqrt(_neg_expm1(2.0 * log_a)) * (gi * xc)
    return a, bterm


def _lru_prompt_kernel(lx_ref, lg_ref, cw_ref, cb_ref, wa_ref, ba_ref, wx_ref, bx_ref, lam_ref,
                       oc_ref, hl_ref, ext_scr, h_scr):
    i = pl.program_id(1)
    tt = lx_ref.shape[0]
    pad = 8

    @pl.when(i == 0)
    def _():
        ext_scr[0:pad, :] = jnp.zeros((pad, LRU_WIDTH), F32)
        h_scr[...] = jnp.zeros_like(h_scr)

    x = lx_ref[...]
    ext_scr[pad:pad + tt, :] = x
    xc = cb_ref[...] + cw_ref[CONV_W - 1:CONV_W, :] * x
    for j in range(CONV_W - 1):
        back = CONV_W - 1 - j
        xc = xc + cw_ref[j:j + 1, :] * ext_scr[pad - back:pad - back + tt, :]
    ext_scr[0:pad, :] = x[tt - pad:tt, :]

    a, bterm = _lru_gates(xc, wa_ref, ba_ref, wx_ref, bx_ref, lam_ref)
    row = lax.broadcasted_iota(jnp.int32, (tt, LRU_WIDTH), 0)
    sh = 1
    while sh < tt:
        keep = row >= sh
        b_s = jnp.where(keep, pltpu.roll(bterm, sh, 0), 0.0)
        a_s = jnp.where(keep, pltpu.roll(a, sh, 0), 1.0)
        bterm = a * b_s + bterm
        a = a * a_s
        sh *= 2
    h = a * h_scr[...] + bterm
    h_scr[...] = h[tt - 1:tt, :]
    oc_ref[...] = h * _gelu_tanh(lg_ref[...])

    @pl.when(i == pl.num_programs(1) - 1)
    def _():
        hl_ref[...] = h[tt - 1:tt, :]


def lru_prompt(proj, conv_w, conv_b, wa_bd, ba, wx_bd, bx, lam, batch, seq, tt):
    nt = seq // tt
    base = IN_COLS // LRU_WIDTH - 2

    def col(cblk):
        return pl.BlockSpec((tt, LRU_WIDTH), lambda b, i: (b * nt + i, cblk))

    row_spec = pl.BlockSpec((1, LRU_WIDTH), lambda b, i: (0, 0))
    mat_spec = pl.BlockSpec((LRU_WIDTH, LRU_WIDTH), lambda b, i: (0, 0))
    return pl.pallas_call(
        _lru_prompt_kernel,
        grid=(batch, nt),
        in_specs=[col(base), col(base + 1),
                  pl.BlockSpec((CONV_W, LRU_WIDTH), lambda b, i: (0, 0)), row_spec,
                  mat_spec, row_spec, mat_spec, row_spec, row_spec],
        out_specs=[pl.BlockSpec((tt, LRU_WIDTH), lambda b, i: (b * nt + i, 0)),
                   pl.BlockSpec((None, 1, LRU_WIDTH), lambda b, i: (b, 0, 0))],
        out_shape=[jax.ShapeDtypeStruct((batch * seq, LRU_WIDTH), F32),
                   jax.ShapeDtypeStruct((batch, 1, LRU_WIDTH), F32)],
        scratch_shapes=[pltpu.VMEM((tt + 8, LRU_WIDTH), F32),
                        pltpu.VMEM((1, LRU_WIDTH), F32)],
        compiler_params=_cparams(("parallel", "arbitrary")),
        name="lru_prompt",
    )(proj, proj, conv_w, conv_b, wa_bd, ba, wx_bd, bx, lam)


def _lru_decode_kernel(lx_ref, lg_ref, buf_ref, h0_ref, cw_ref, cb_ref, wa_ref, ba_ref, wx_ref,
                       bx_ref, lam_ref, oc_ref, hn_ref, nbuf_ref):
    x = lx_ref[...]
    buf = buf_ref[...]
    xc = cb_ref[...] + cw_ref[CONV_W - 1:CONV_W, :] * x
    for j in range(CONV_W - 1):
        xc = xc + cw_ref[j:j + 1, :] * buf[:, j * LRU_WIDTH:(j + 1) * LRU_WIDTH]
    a, bterm = _lru_gates(xc, wa_ref, ba_ref, wx_ref, bx_ref, lam_ref)
    h = a * h0_ref[...] + bterm
    hn_ref[...] = h
    oc_ref[...] = h * _gelu_tanh(lg_ref[...])
    nbuf_ref[...] = jnp.concatenate([buf[:, LRU_WIDTH:], x], axis=1)


def lru_decode(proj_d, conv_buf, h0, conv_w, conv_b, wa_bd, ba, wx_bd, bx, lam):
    nb = proj_d.shape[0]
    base = IN_COLS // LRU_WIDTH - 2
    row_spec = pl.BlockSpec((1, LRU_WIDTH), lambda i: (0, 0))
    mat_spec = pl.BlockSpec((LRU_WIDTH, LRU_WIDTH), lambda i: (0, 0))
    act = pl.BlockSpec((nb, LRU_WIDTH), lambda i: (0, 0))
    bufs = pl.BlockSpec((nb, (CONV_W - 1) * LRU_WIDTH), lambda i: (0, 0))
    return pl.pallas_call(
        _lru_decode_kernel,
        grid=(1,),
        in_specs=[pl.BlockSpec((nb, LRU_WIDTH), lambda i: (0, base)),
                  pl.BlockSpec((nb, LRU_WIDTH), lambda i: (0, base + 1)),
                  bufs, act, pl.BlockSpec((CONV_W, LRU_WIDTH), lambda i: (0, 0)), row_spec,
                  mat_spec, row_spec, mat_spec, row_spec, row_spec],
        out_specs=[act, act, bufs],
        out_shape=[jax.ShapeDtypeStruct((nb, LRU_WIDTH), F32),
                   jax.ShapeDtypeStruct((nb, LRU_WIDTH), F32),
                   jax.ShapeDtypeStruct((nb, (CONV_W - 1) * LRU_WIDTH), F32)],
        compiler_params=_cparams(("arbitrary",)),
        name="lru_decode",
    )(proj_d, proj_d, conv_buf, h0, conv_w, conv_b, wa_bd, ba, wx_bd, bx, lam)


def _xattn_prompt_kernel(x_ref, wq_ref, mk_ref, mv_ref, wo_ref, g_ref, b_ref, wr_ref, br_ref, o_ref):
    x = x_ref[...]
    q = _bdot(x, wq_ref[...]).astype(BF16)
    mk = mk_ref[...].astype(BF16)
    mv = mv_ref[...].astype(BF16)
    outs = []
    for h in range(X_HEADS):
        sl = slice(h * HEAD_DIM, (h + 1) * HEAD_DIM)
        s = _bdot_nt(q[:, sl], mk[:, sl]) * SCALE
        m = jnp.max(s, -1, keepdims=True)
        p = jnp.exp(s - m)
        p = p / jnp.sum(p, -1, keepdims=True)
        outs.append(_bdot(p, mv[:, sl]))
    o = jnp.concatenate(outs, axis=1)
    y = _layer_norm(ALPHA * x + _bdot(o, wo_ref[...]), g_ref[...], b_ref[...])
    logits = _bdot(y, wr_ref[...]) + br_ref[...]
    gate, g_idx = _route(logits)
    lane = lax.broadcasted_iota(jnp.int32, gate.shape, 1)
    for j in range(D_MODEL // LANES):
        o_ref[j] = y[:, j * LANES:(j + 1) * LANES]
    o_ref[D_MODEL // LANES] = jnp.where(lane == 0, g_idx.astype(F32), gate)


def xattn_prompt(x, wq, mem_kv, wo, g, b, wr, br, batch, seq, tt):
    nt = seq // tt
    const = lambda bi, i: (0, 0)
    return pl.pallas_call(
        _xattn_prompt_kernel,
        grid=(batch, nt),
        in_specs=[pl.BlockSpec((tt, D_MODEL), lambda bi, i: (bi * nt + i, 0)),
                  pl.BlockSpec((D_MODEL, X_WIDTH), const),
                  pl.BlockSpec((N_MEM, X_WIDTH), lambda bi, i: (bi, 0)),
                  pl.BlockSpec((N_MEM, X_WIDTH), lambda bi, i: (bi, 1)),
                  pl.BlockSpec((X_WIDTH, D_MODEL), const),
                  pl.BlockSpec((1, D_MODEL), const),
                  pl.BlockSpec((1, D_MODEL), const),
                  pl.BlockSpec((D_MODEL, ROUTER_LANES), const),
                  pl.BlockSpec((1, ROUTER_LANES), const)],
        out_specs=pl.BlockSpec((XG_PLANES, tt, LANES), lambda bi, i: (0, bi * nt + i, 0)),
        out_shape=jax.ShapeDtypeStruct((XG_PLANES, batch * seq, LANES), F32),
        compiler_params=_cparams(("parallel", "parallel")),
        name="xattn_prompt",
    )(x, wq, mem_kv, mem_kv, wo, g, b, wr, br)


def _route(logits):
    lane = lax.broadcasted_iota(jnp.int32, logits.shape, 1)
    big = jnp.int32(ROUTER_LANES)
    ninf = -jnp.inf
    gl = jnp.where(lane < N_GROUPS, logits, ninf)
    gm = jnp.max(gl, -1, keepdims=True)
    g_val = 1.0 / jnp.sum(jnp.exp(gl - gm), -1, keepdims=True)
    g_idx = jnp.min(jnp.where(gl == gm, lane, big), -1, keepdims=True)
    lo = N_GROUPS + EXP_PER_GROUP * g_idx
    el = jnp.where((lane >= lo) & (lane < lo + EXP_PER_GROUP), logits, ninf)
    v1 = jnp.max(el, -1, keepdims=True)
    i1 = jnp.min(jnp.where(el == v1, lane, big), -1, keepdims=True)
    el2 = jnp.where(lane == i1, ninf, el)
    v2 = jnp.max(el2, -1, keepdims=True)
    i2 = jnp.min(jnp.where(el2 == v2, lane, big), -1, keepdims=True)
    e2 = jnp.exp(v2 - v1)
    w1 = g_val / (1.0 + e2)
    w2 = g_val * e2 / (1.0 + e2)
    return jnp.where(lane == i1, w1, 0.0) + jnp.where(lane == i2, w2, 0.0), g_idx


def _moe_dense_kernel(x_ref, wr_ref, br_ref, wg_ref, wu_ref, wd_ref, g_ref, b_ref, o_ref,
                      gate_scr, acc_scr):
    e = pl.program_id(1)

    @pl.when(e == 0)
    def _():
        logits = _bdot(x_ref[...], wr_ref[...]) + br_ref[...]
        gate_scr[...] = _route(logits)[0]
        acc_scr[...] = jnp.zeros_like(acc_scr)

    xb = x_ref[...].astype(BF16)
    lane = lax.broadcasted_iota(jnp.int32, gate_scr.shape, 1)
    gcol = jnp.sum(jnp.where(lane == e + N_GROUPS, gate_scr[...], 0.0), -1, keepdims=True)
    hid = _silu(_bdot(xb, wg_ref[...])) * _bdot(xb, wu_ref[...])
    acc_scr[...] += _bdot(hid * gcol, wd_ref[...])

    @pl.when(e == pl.num_programs(1) - 1)
    def _():
        o_ref[...] = _layer_norm(ALPHA * x_ref[...] + acc_scr[...], g_ref[...], b_ref[...])


def moe_dense(x, wr, br, wg, wu, wd, g, b, layer, tm):
    m = x.shape[0]
    return pl.pallas_call(
        _moe_dense_kernel,
        grid=(m // tm, N_EXPERTS),
        in_specs=[pl.BlockSpec((tm, D_MODEL), lambda i, e: (i, 0)),
                  pl.BlockSpec((D_MODEL, ROUTER_LANES), lambda i, e: (0, 0)),
                  pl.BlockSpec((1, ROUTER_LANES), lambda i, e: (0, 0)),
                  pl.BlockSpec((None, None, D_MODEL, EXP_FF), lambda i, e: (layer, e, 0, 0)),
                  pl.BlockSpec((None, None, D_MODEL, EXP_FF), lambda i, e: (layer, e, 0, 0)),
                  pl.BlockSpec((None, None, EXP_FF, D_MODEL), lambda i, e: (layer, e, 0, 0)),
                  pl.BlockSpec((1, D_MODEL), lambda i, e: (0, 0)),
                  pl.BlockSpec((1, D_MODEL), lambda i, e: (0, 0))],
        out_specs=pl.BlockSpec((tm, D_MODEL), lambda i, e: (i, 0)),
        out_shape=jax.ShapeDtypeStruct((m, D_MODEL), F32),
        scratch_shapes=[pltpu.VMEM((tm, ROUTER_LANES), F32), pltpu.VMEM((tm, D_MODEL), F32)],
        compiler_params=_cparams(("parallel", "arbitrary")),
        name="moe_dense",
    )(x, wr, br, wg, wu, wd, g, b)


def _sc_mesh():
    return plsc.VectorSubcoreMesh(core_axis_name="core", subcore_axis_name="subcore")


def sc_scatter_rows(x, idx, n_out):
    r = x.shape[0]

    @functools.partial(pl.kernel, out_type=jax.ShapeDtypeStruct((n_out, LANES), x.dtype),
                       mesh=_sc_mesh(), scratch_types=[], name="sc_scatter_rows")
    def k(x_hbm, i_hbm, o_hbm):
        def body(x_vmem, i_vmem):
            pltpu.sync_copy(x_vmem, o_hbm.at[i_vmem.at[0]])

        pltpu.emit_pipeline(
            body,
            grid=(r // SC_WINDOW,),
            in_specs=[pl.BlockSpec((SC_WINDOW, LANES), lambda i: (i, 0)),
                      pl.BlockSpec((1, SC_WINDOW), lambda i: (0, i))],
            out_specs=[],
            core_axis_name=("core", "subcore"),
            dimension_semantics=(pltpu.PARALLEL,),
        )(x_hbm, i_hbm)

    return k(x, idx.reshape(1, r))


def sc_gather_rows(table, idx):
    r = idx.shape[0]

    @functools.partial(pl.kernel, out_type=jax.ShapeDtypeStruct((r, LANES), table.dtype),
                       mesh=_sc_mesh(), scratch_types=[], name="sc_gather_rows")
    def k(t_hbm, i_hbm, o_hbm):
        def body(i_vmem, o_vmem):
            pltpu.sync_copy(t_hbm.at[i_vmem.at[0]], o_vmem)

        pltpu.emit_pipeline(
            body,
            grid=(r // SC_WINDOW,),
            in_specs=[pl.BlockSpec((1, SC_WINDOW), lambda i: (0, i))],
            out_specs=[pl.BlockSpec((SC_WINDOW, LANES), lambda i: (i, 0))],
            core_axis_name=("core", "subcore"),
            dimension_semantics=(pltpu.PARALLEL,),
        )(i_hbm, o_hbm)

    return k(table, idx.reshape(1, r))


def _moe_sorted_kernel(tg_ref, nused_ref, xs_ref, wg32_ref, wu32_ref, wd32_ref, g_ref, b_ref, o_ref,
                       wg_ref, wu_ref, wd_ref):
    t = pl.program_id(0)
    used = t < nused_ref[0]

    @pl.when(used & ((t == 0) | (tg_ref[t] != tg_ref[jnp.maximum(t - 1, 0)])))
    def _():
        wg_ref[...] = wg32_ref[...].astype(BF16)
        wu_ref[...] = wu32_ref[...].astype(BF16)
        wd_ref[...] = wd32_ref[...].astype(BF16)

    @pl.when(used)
    def _():
        x = jnp.concatenate([xs_ref[j] for j in range(X_PLANES)], axis=1)
        gate = xs_ref[X_PLANES]
        xb = x.astype(BF16)
        lane = lax.broadcasted_iota(jnp.int32, gate.shape, 1)
        first = N_GROUPS + EXP_PER_GROUP * tg_ref[t]
        acc = jnp.zeros(x.shape, F32)
        for e in range(EXP_PER_GROUP):
            gcol = jnp.sum(jnp.where(lane == first + e, gate, 0.0), -1, keepdims=True)
            hid = _silu(_bdot(xb, wg_ref[e])) * _bdot(xb, wu_ref[e])
            acc = acc + _bdot(hid * gcol, wd_ref[e])
        y = _layer_norm(ALPHA * x + acc, g_ref[...], b_ref[...])
        for j in range(X_PLANES):
            o_ref[j] = y[:, j * LANES:(j + 1) * LANES]


def _group_slots(group_idx, n, tm):
    n_tiles = n // tm + N_GROUPS
    onehot = (group_idx[:, None] == jnp.arange(N_GROUPS)[None, :]).astype(jnp.int32)
    csum = jnp.cumsum(onehot, axis=0)
    counts = csum[-1]
    rank = jnp.sum(onehot * csum, axis=1) - 1
    tiles_g = (counts + tm - 1) // tm
    tile_end = jnp.cumsum(tiles_g)
    slot_base = (tile_end - tiles_g) * tm
    slot = (jnp.sum(onehot * slot_base[None, :], axis=1) + rank).astype(jnp.int32)
    tile_group = jnp.sum((jnp.arange(n_tiles)[:, None] >= tile_end[None, :]).astype(jnp.int32), axis=1)
    tile_group = jnp.minimum(tile_group, N_GROUPS - 1).astype(jnp.int32)
    return slot, tile_group, tile_end[-1:].astype(jnp.int32)


def moe_routed_sc(xg, wg, wu, wd, g, b, layer, n, tm, during_scatter, during_gather):
    slot, tile_group, n_used = _group_slots(xg[X_PLANES, :, 0].astype(jnp.int32), n, tm)
    n_tiles = tile_group.shape[0]
    n_slots = n_tiles * tm
    plane_base = lambda planes: jnp.arange(planes, dtype=jnp.int32)[:, None] * n_slots
    idx_in = (plane_base(XG_PLANES) + slot[None, :]).reshape(-1)
    idx_out = (plane_base(X_PLANES) + slot[None, :]).reshape(-1)
    xs = sc_scatter_rows(xg.reshape(XG_PLANES * n, LANES), idx_in, XG_PLANES * n_slots)
    xs = xs.reshape(XG_PLANES, n_slots, LANES)
    xs, side = lax.optimization_barrier((xs, during_scatter()))
    wspec = lambda shp: pl.BlockSpec((None, None, EXP_PER_GROUP) + shp,
                                     lambda t, tg, nu: (layer, tg[t], 0, 0, 0))
    grid_spec = pltpu.PrefetchScalarGridSpec(
        num_scalar_prefetch=2,
        grid=(n_tiles,),
        in_specs=[pl.BlockSpec((XG_PLANES, tm, LANES), lambda t, tg, nu: (0, t, 0)),
                  wspec((D_MODEL, EXP_FF)), wspec((D_MODEL, EXP_FF)), wspec((EXP_FF, D_MODEL)),
                  pl.BlockSpec((1, D_MODEL), lambda t, tg, nu: (0, 0)),
                  pl.BlockSpec((1, D_MODEL), lambda t, tg, nu: (0, 0))],
        out_specs=pl.BlockSpec((X_PLANES, tm, LANES), lambda t, tg, nu: (0, t, 0)),
        scratch_shapes=[pltpu.VMEM((EXP_PER_GROUP, D_MODEL, EXP_FF), BF16),
                        pltpu.VMEM((EXP_PER_GROUP, D_MODEL, EXP_FF), BF16),
                        pltpu.VMEM((EXP_PER_GROUP, EXP_FF, D_MODEL), BF16)])
    grouped = lambda w: w.reshape(w.shape[0], N_GROUPS, EXP_PER_GROUP, w.shape[2], w.shape[3])
    ys = pl.pallas_call(
        _moe_sorted_kernel,
        grid_spec=grid_spec,
        out_shape=jax.ShapeDtypeStruct((X_PLANES, n_slots, LANES), F32),
        compiler_params=_cparams(("arbitrary",)),
        name="moe_sorted",
    )(tile_group, n_used, xs, grouped(wg), grouped(wu), grouped(wd), g, b)
    y = sc_gather_rows(ys.reshape(X_PLANES * n_slots, LANES), idx_out)
    y, side = lax.optimization_barrier((y, during_gather(side)))
    return y.reshape(X_PLANES, n, LANES), side


def _t5_bucket(dist):
    max_exact = N_BUCKETS // 2
    d = jnp.maximum(dist, 0)
    df = jnp.maximum(d, 1).astype(F32)
    log_b = max_exact + (jnp.log(df / max_exact) / math.log(MAX_DISTANCE / max_exact)
                         * (N_BUCKETS - max_exact)).astype(jnp.int32)
    return jnp.where(d < max_exact, d, jnp.minimum(log_b, N_BUCKETS - 1))


def _bucket_lookup(rel_bias, bucket):
    out = jnp.zeros(bucket.shape + (rel_bias.shape[1],), F32)
    for i in range(N_BUCKETS):
        out = jnp.where((bucket == i)[..., None], rel_bias[i].astype(F32), out)
    return out


def _prompt_bias(rel_bias):
    qi = jnp.arange(WINDOW)[:, None]
    kj = jnp.arange(2 * WINDOW)[None, :]
    dist = qi + WINDOW - kj
    bias = _bucket_lookup(rel_bias, _t5_bucket(dist)).transpose(2, 0, 1)
    valid = (dist >= 0) & (dist <= WINDOW)
    return jnp.where(valid[None], bias, NEG)


def _decode_table(rel_bias, attn_sink):
    dist = WINDOW - jnp.arange(WINDOW + 1)
    bias = _bucket_lookup(rel_bias, _t5_bucket(dist)).T
    depth = attn_sink.shape[0]
    wide = lambda v: jnp.broadcast_to(v[..., None], v.shape + (WINDOW,))
    per_layer = lambda t: jnp.broadcast_to(t[None], (depth,) + t.shape)
    return jnp.stack([per_layer(bias[:, :WINDOW]), per_layer(wide(bias[:, WINDOW])),
                      wide(attn_sink.astype(F32))], axis=1)


def _block_ones(width):
    idx = jnp.arange(width) // HEAD_DIM
    return (idx[:, None] == idx[None, :])


def _head_rows_mask():
    head = jnp.arange(HG_WIDTH)[None, :] // HEAD_DIM
    return (head == jnp.arange(HG_MROWS)[:, None]).astype(F32)


def _block_diag(w):
    nblk, s, _ = w.shape
    eye = jnp.eye(nblk, dtype=w.dtype)
    return (eye[:, None, :, None] * w[:, :, None, :]).reshape(nblk * s, nblk * s)


def kernel(x_prompt, x_sample, mem_prompt, cache_win_k, cache_win_v, state_hgrn, state_conv, state_lru, cache_mem_k, cache_mem_v, rel_bias, hg_lb, w_in, attn_sink, hg_gain, conv_w, conv_b, lru_wa, lru_ba, lru_wx, lru_bx, lru_lam, w_out, ln1_g, ln1_b, x_wq, x_wk, x_wv, x_wo, ln2_g, ln2_b, r_gw, r_gb, r_ew, r_eb, e_wg, e_wu, e_wd, ln3_g, ln3_b):
    bp, seq, d = x_prompt.shape
    n_tok = bp * seq
    nd = x_sample.shape[0]
    depth = w_in.shape[0]

    lbs = jnp.cumsum(jax.nn.softmax(hg_lb.astype(F32), axis=0), axis=0)
    lbs = lbs - lbs[0]
    loglb = jnp.log(lbs)
    log1mlb = jnp.log1p(-lbs)
    gain4 = jnp.tile(hg_gain, (1, HG_HEADS))

    bias_p = _prompt_bias(rel_bias)
    bdm256 = _block_ones(HG_WIDTH).astype(F32)
    hmask = _head_rows_mask()

    w_in_b = w_in.astype(BF16)
    w_out_b = w_out.astype(BF16)
    wq_b = x_wq.astype(BF16)
    wkv_b = jnp.concatenate([x_wk, x_wv], axis=-1).astype(BF16)
    wo_b = x_wo.astype(BF16)
    rew = r_ew.transpose(0, 2, 1, 3).reshape(depth, d, N_EXPERTS)
    wr = jnp.concatenate([r_gw, rew, jnp.zeros((depth, d, ROUTER_LANES - N_GROUPS - N_EXPERTS), F32)], -1)
    br = jnp.concatenate([r_gb, r_eb.reshape(depth, N_EXPERTS),
                          jnp.zeros((depth, ROUTER_LANES - N_GROUPS - N_EXPERTS), F32)], -1)

    a_w = A_HEADS * HEAD_DIM
    xp = x_prompt.reshape(bp * seq, d)
    xs = x_sample.reshape(nd, d)
    mem = mem_prompt.reshape(bp * N_MEM, d)
    ckt = cache_win_k.transpose(0, 1, 3, 4, 2).reshape(depth, nd, LANES, WINDOW)
    cvt = cache_win_v.transpose(0, 1, 3, 4, 2).reshape(depth, nd, LANES, WINDOW)
    cmkt = cache_mem_k.transpose(0, 1, 3, 4, 2).reshape(depth, nd, X_WIDTH, N_MEM)
    cmvt = cache_mem_v.transpose(0, 1, 3, 4, 2).reshape(depth, nd, X_WIDTH, N_MEM)
    state_t = state_hgrn.transpose(0, 2, 3, 4, 1).reshape(depth, HG_HEADS * HEAD_DIM * HEAD_DIM, nd)
    dec_tab = _decode_table(rel_bias, attn_sink)
    xq_mask = _head_rows_mask()[:X_QROWS]
    head_group = jnp.arange(A_HEADS) // A_REP

    p_wk, p_wv, p_s, p_cb, p_h, p_mk, p_mv = [], [], [], [], [], [], []
    s_cb, s_h = [], []
    for l in range(depth):
        row = lambda v: v[l].reshape(1, -1)
        wa_bd = _block_diag(lru_wa[l]).astype(BF16)
        wx_bd = _block_diag(lru_wx[l]).astype(BF16)
        lru_args = (conv_w[l], row(conv_b), wa_bd, row(lru_ba), wx_bd, row(lru_bx), row(lru_lam))
        wo_parts = [w_out_b[l, :a_w], w_out_b[l, a_w:a_w + HG_WIDTH], w_out_b[l, a_w + HG_WIDTH:]]

        proj = matmul(xp, w_in_b[l], 1024, IN_COLS)
        oa = attn_prompt(proj, attn_sink[l], bias_p, bp, seq)
        ob, st = hgrn_prompt(proj, row(loglb), row(log1mlb), row(gain4), bdm256, hmask, bp, seq, 1024)
        oc, hl = lru_prompt(proj, *lru_args, bp, seq, 512)
        xp = proj_res_ln(xp, [oa, ob, oc], wo_parts, row(ln1_g), row(ln1_b), 1024)
        mkv = matmul(mem, wkv_b[l], 256, 512)
        xg = xattn_prompt(xp, wq_b[l], mkv, wo_b[l], row(ln2_g), row(ln2_b), wr[l], br[l:l + 1],
                          bp, seq, 1024)
        def decode_mixers(xs=xs, ckt=ckt, cvt=cvt, state_t=state_t, l=l, lru_args=lru_args,
                          wo_parts=wo_parts, row=row):
            projd = matmul(xs, w_in_b[l], nd, 768)
            q3 = projd[:, :a_w].reshape(nd, A_HEADS, 1, HEAD_DIM)
            on_group = head_group[None, :, None, None] == jnp.arange(A_KV_HEADS)[None, None, :, None]
            qblk = jnp.where(on_group, q3, 0.0).reshape(nd, A_HEADS, LANES)
            o3, ckt, cvt = attn_decode(qblk, projd, dec_tab[l], ckt, cvt, l, 16)
            o4 = o3.reshape(nd, A_HEADS, A_KV_HEADS, HEAD_DIM)
            oa = jnp.sum(jnp.where(on_group, o4, 0.0), axis=2).reshape(nd, a_w)
            gates_t = projd[:, a_w + 2 * LANES:a_w + 2 * LANES + 4 * HG_WIDTH].T
            bc = lambda v: jnp.broadcast_to(v[:, None], (v.shape[0], nd))
            ob_t, state_t = hgrn_decode(gates_t, bc(loglb[l]), bc(log1mlb[l]), bc(hg_gain[l]), state_t, l)
            oc, nh, nbuf = lru_decode(projd, state_conv[l].reshape(nd, -1), state_lru[l], *lru_args)
            xs = proj_res_ln(xs, [oa, ob_t.T, oc], wo_parts, row(ln1_g), row(ln1_b), nd)
            return xs, ckt, cvt, state_t, nh, nbuf

        def decode_rest(side, l=l, row=row):
            xs = side[0]
            qd = matmul(xs, wq_b[l], nd, X_WIDTH)
            qdb = qd[:, None, :] * xq_mask[None, :, :]
            od = xattn_decode(qdb, cmkt, cmvt, xq_mask, l, 16)
            xs = proj_res_ln(xs, [od], [wo_b[l]], row(ln2_g), row(ln2_b), nd)
            xs = moe_dense(xs, wr[l], br[l:l + 1], e_wg, e_wu, e_wd, row(ln3_g), row(ln3_b), l, nd)
            return (xs,) + tuple(side[1:])

        xp, (xs, ckt, cvt, state_t, nh, nbuf) = moe_routed_sc(
            xg, e_wg, e_wu, e_wd, row(ln3_g), row(ln3_b), l, n_tok, MOE_TM, decode_mixers, decode_rest)

        proj3 = proj.reshape(bp, seq, IN_COLS)
        p_wk.append(proj3[:, seq - WINDOW:, a_w:a_w + LANES].reshape(bp, WINDOW, A_KV_HEADS, HEAD_DIM))
        p_wv.append(proj3[:, seq - WINDOW:, a_w + LANES:a_w + 2 * LANES].reshape(bp, WINDOW, A_KV_HEADS, HEAD_DIM))
        st5 = st.reshape(bp, HG_HEADS, HEAD_DIM, HG_HEADS, HEAD_DIM)
        p_s.append(jnp.stack([st5[:, h, :, h, :] for h in range(HG_HEADS)], 1).transpose(0, 1, 3, 2))
        p_cb.append(proj3[:, seq - (CONV_W - 1):, IN_COLS - 2 * LRU_WIDTH:IN_COLS - LRU_WIDTH])
        p_h.append(hl.reshape(bp, LRU_WIDTH))
        p_mk.append(mkv[:, :X_WIDTH].reshape(bp, N_MEM, X_HEADS, HEAD_DIM))
        p_mv.append(mkv[:, X_WIDTH:].reshape(bp, N_MEM, X_HEADS, HEAD_DIM))

        s_cb.append(nbuf.reshape(nd, CONV_W - 1, LRU_WIDTH))
        s_h.append(nh)

    xp = jnp.transpose(xp, (1, 0, 2))
    unkey = lambda c: c.reshape(depth, nd, A_KV_HEADS, HEAD_DIM, WINDOW).transpose(0, 1, 4, 2, 3)
    s_s = state_t.reshape(depth, HG_HEADS, HEAD_DIM, HEAD_DIM, nd).transpose(0, 4, 1, 2, 3)
    return (xp.reshape(bp, seq, d), xs.reshape(nd, 1, d),
            jnp.stack(p_wk), jnp.stack(p_wv), jnp.stack(p_s), jnp.stack(p_cb), jnp.stack(p_h),
            jnp.stack(p_mk), jnp.stack(p_mv),
            unkey(ckt), unkey(cvt), s_s, jnp.stack(s_cb), jnp.stack(s_h))
```

```python
import functools
import math

import jax
import jax.numpy as jnp
from jax import lax
from jax.experimental import pallas as pl
from jax.experimental.pallas import tpu as pltpu
from jax.experimental.pallas import tpu_sc as plsc

F32 = jnp.float32
BF16 = jnp.bfloat16

D_MODEL = 1024
DEPTH = 4
HEAD_DIM = 64
A_HEADS = 8
A_KV_HEADS = 2
A_REP = A_HEADS // A_KV_HEADS
WINDOW = 128
A_QB = 2
N_BUCKETS = 32
MAX_DISTANCE = 128
HG_WIDTH = 256
HG_HEADS = 4
HG_CHUNK = 64
HG_TB = 16
HG_MROWS = 8
LOG2E = math.log2(math.e)
LRU_WIDTH = 256
LRU_BLOCKS = 4
CONV_W = 4
LRU_C = 8.0
N_MEM = 256
X_HEADS = 4
X_WIDTH = X_HEADS * HEAD_DIM
X_QROWS = 8
N_GROUPS = 4
EXP_PER_GROUP = 4
N_EXPERTS = N_GROUPS * EXP_PER_GROUP
EXP_FF = D_MODEL // 4
ALPHA = (2 * DEPTH) ** 0.25
LN_EPS = 1e-5
RMS_EPS = 1e-6
IN_COLS = 2304
SCALE = HEAD_DIM ** -0.5
NEG = -1e30
LANES = 128
SUBLANES = 8
ROUTER_LANES = 128
XG_WIDTH = D_MODEL + ROUTER_LANES
XG_PLANES = XG_WIDTH // LANES
X_PLANES = D_MODEL // LANES
MOE_TM = 512
SC_WINDOW = 128
VMEM_LIMIT = 48 * 1024 * 1024


def _cparams(sem):
    return pltpu.CompilerParams(dimension_semantics=sem, vmem_limit_bytes=VMEM_LIMIT)


def _bdot(a, b):
    return jnp.dot(a.astype(BF16), b.astype(BF16), preferred_element_type=F32)


def _bdot_nt(a, b):
    return lax.dot_general(a.astype(BF16), b.astype(BF16), (((1,), (1,)), ((), ())),
                           preferred_element_type=F32)


def _bdot_tn(a, b):
    return lax.dot_general(a.astype(BF16), b.astype(BF16), (((0,), (0,)), ((), ())),
                           preferred_element_type=F32)


def _rb(x):
    return x.astype(BF16).astype(F32)


def _silu(x):
    return x * jax.nn.sigmoid(x)


def _neg_expm1(x):
    return -jnp.tanh(0.5 * x) * (jnp.exp(x) + 1.0)


def _softplus(x):
    return jnp.maximum(x, 0.0) + jnp.log1p(jnp.exp(-jnp.abs(x)))


def _gelu_tanh(x):
    return 0.5 * x * (1.0 + jnp.tanh(math.sqrt(2.0 / math.pi) * (x + 0.044715 * (x * x * x))))


def _layer_norm(y, g, b):
    mu = jnp.mean(y, -1, keepdims=True)
    yc = y - mu
    var = jnp.mean(yc * yc, -1, keepdims=True)
    return yc * lax.rsqrt(var + LN_EPS) * g + b


def _rows(x_ref):
    if len(x_ref.shape) == 2:
        return x_ref[...]
    return jnp.concatenate([x_ref[j] for j in range(x_ref.shape[0])], axis=1)


def _rows_spec(x, tm, nargs):
    if x.ndim == 2:
        return pl.BlockSpec((tm, x.shape[1]), (lambda i: (i, 0)) if nargs == 1 else (lambda i, j: (i, 0)))
    blk = (x.shape[0], tm, LANES)
    return pl.BlockSpec(blk, (lambda i: (0, i, 0)) if nargs == 1 else (lambda i, j: (0, i, 0)))


def _mm_kernel(x_ref, w_ref, o_ref):
    o_ref[...] = _bdot(_rows(x_ref), w_ref[...])


def matmul(x, w, tm, tn):
    m = x.shape[-2]
    k, n = w.shape
    return pl.pallas_call(
        _mm_kernel,
        grid=(m // tm, n // tn),
        in_specs=[_rows_spec(x, tm, 2),
                  pl.BlockSpec((k, tn), lambda i, j: (0, j))],
        out_specs=pl.BlockSpec((tm, tn), lambda i, j: (i, j)),
        out_shape=jax.ShapeDtypeStruct((m, n), F32),
        compiler_params=_cparams(("parallel", "parallel")),
        name="matmul",
    )(x, w)


def _proj_res_ln_kernel(n_in, x_ref, *refs):
    a_refs = refs[:n_in]
    w_refs = refs[n_in:2 * n_in]
    g_ref, b_ref, o_ref = refs[2 * n_in:]
    y = ALPHA * _rows(x_ref)
    for a_ref, w_ref in zip(a_refs, w_refs):
        y = y + _bdot(a_ref[...], w_ref[...])
    o_ref[...] = _layer_norm(y, g_ref[...], b_ref[...])


def proj_res_ln(x, a_list, w_list, g, b, tm):
    m = x.shape[-2]
    d = w_list[0].shape[1]
    n_in = len(a_list)
    in_specs = [_rows_spec(x, tm, 1)]
    in_specs += [pl.BlockSpec((tm, a.shape[1]), lambda i: (i, 0)) for a in a_list]
    in_specs += [pl.BlockSpec(w.shape, lambda i: (0, 0)) for w in w_list]
    in_specs += [pl.BlockSpec((1, d), lambda i: (0, 0))] * 2
    return pl.pallas_call(
        functools.partial(_proj_res_ln_kernel, n_in),
        grid=(m // tm,),
        in_specs=in_specs,
        out_specs=pl.BlockSpec((tm, d), lambda i: (i, 0)),
        out_shape=jax.ShapeDtypeStruct((m, d), F32),
        compiler_params=_cparams(("parallel",)),
        name="proj_res_ln",
    )(x, *a_list, *w_list, g, b)


def _attn_prompt_kernel(sink_ref, q_ref, kc_ref, kp_ref, vc_ref, vp_ref, bias_ref, o_ref):
    n = pl.program_id(1)
    col = lax.broadcasted_iota(jnp.int32, (WINDOW, 2 * WINDOW), 1)
    first = jnp.where((n == 0) & (col < WINDOW), NEG, 0.0)
    kk = jnp.concatenate([kp_ref[...], kc_ref[...]], axis=0).astype(BF16)
    vv = jnp.concatenate([vp_ref[...], vc_ref[...]], axis=0).astype(BF16)
    q = q_ref[...].astype(BF16)
    for u in range(A_QB):
        rows = slice(u * WINDOW, (u + 1) * WINDOW)
        keys = slice(u * WINDOW, (u + 2) * WINDOW)
        outs = []
        for h in range(A_HEADS):
            g = h // A_REP
            qh = q[rows, h * HEAD_DIM:(h + 1) * HEAD_DIM]
            kg = kk[keys, g * HEAD_DIM:(g + 1) * HEAD_DIM]
            vg = vv[keys, g * HEAD_DIM:(g + 1) * HEAD_DIM]
            s = _bdot_nt(qh, kg) * SCALE + bias_ref[h]
            if u == 0:
                s = s + first
            sink = sink_ref[h]
            m = jnp.maximum(jnp.max(s, -1, keepdims=True), sink)
            p = jnp.exp(s - m)
            den = jnp.sum(p, -1, keepdims=True) + jnp.exp(sink - m)
            outs.append(_bdot(p / den, vg))
        o_ref[rows, :] = jnp.concatenate(outs, axis=1)


def attn_prompt(proj, sink, bias, batch, seq):
    nb = seq // WINDOW
    ns = nb // A_QB
    tq = A_QB * WINDOW
    qcol = 0
    kcol = (A_HEADS * HEAD_DIM) // LANES
    vcol = kcol + 1

    def cur(c):
        return lambda b, n: (b * ns + n, c)

    def prev(c):
        return lambda b, n: (b * nb + jnp.maximum(n * A_QB - 1, 0), c)

    return pl.pallas_call(
        _attn_prompt_kernel,
        grid=(batch, ns),
        in_specs=[pl.BlockSpec(memory_space=pltpu.SMEM),
                  pl.BlockSpec((tq, A_HEADS * HEAD_DIM), cur(qcol)),
                  pl.BlockSpec((tq, LANES), cur(kcol)),
                  pl.BlockSpec((WINDOW, LANES), prev(kcol)),
                  pl.BlockSpec((tq, LANES), cur(vcol)),
                  pl.BlockSpec((WINDOW, LANES), prev(vcol)),
                  pl.BlockSpec((A_HEADS, WINDOW, 2 * WINDOW), lambda b, n: (0, 0, 0))],
        out_specs=pl.BlockSpec((tq, A_HEADS * HEAD_DIM), cur(0)),
        out_shape=jax.ShapeDtypeStruct((batch * seq, A_HEADS * HEAD_DIM), F32),
        compiler_params=_cparams(("parallel", "parallel")),
        name="attn_prompt",
    )(sink, proj, proj, proj, proj, proj, bias)


def _attn_decode_kernel(qb_ref, kn_ref, vn_ref, knt_ref, vnt_ref, ck_ref, cv_ref, tab_ref,
                        o_ref, ok_ref, ov_ref):
    bb = qb_ref.shape[0]
    ck = ck_ref[...]
    cv = cv_ref[...]
    qb = qb_ref[...]
    kn = kn_ref[...]
    vn = vn_ref[...]
    bias_j = tab_ref[0]
    bias_n = tab_ref[1][:, 0:1]
    sink = tab_ref[2][:, 0:1]
    s = lax.dot_general(qb.astype(BF16), ck.astype(BF16), (((2,), (1,)), ((0,), (0,))),
                        preferred_element_type=F32) * SCALE + bias_j[None]
    sn = jnp.sum(_rb(qb) * _rb(kn)[:, None, :], -1, keepdims=True) * SCALE + bias_n[None]
    m = jnp.maximum(jnp.maximum(jnp.max(s, -1, keepdims=True), sn), sink[None])
    p = jnp.exp(s - m)
    pn = jnp.exp(sn - m)
    den = jnp.sum(p, -1, keepdims=True) + pn + jnp.exp(sink[None] - m)
    o = lax.dot_general((p / den).astype(BF16), cv.astype(BF16), (((2,), (2,)), ((0,), (0,))),
                        preferred_element_type=F32)
    o_ref[...] = o + _rb(pn / den) * _rb(vn)[:, None, :]
    lane = lax.broadcasted_iota(jnp.int32, (LANES, LANES), 1)
    for b in range(bb):
        ok_ref[b] = jnp.where(lane == WINDOW - 1, knt_ref[:, b:b + 1], pltpu.roll(ck[b], WINDOW - 1, 1))
        ov_ref[b] = jnp.where(lane == WINDOW - 1, vnt_ref[:, b:b + 1], pltpu.roll(cv[b], WINDOW - 1, 1))


def attn_decode(qblk, proj_d, table, cache_k, cache_v, layer, bb):
    nbatch = proj_d.shape[0]
    a_w = A_HEADS * HEAD_DIM
    cols = lambda c: proj_d[:, c:c + LANES].reshape(nbatch // bb, bb, LANES).transpose(0, 2, 1)
    knt, vnt = cols(a_w), cols(a_w + LANES)
    kcol = (A_HEADS * HEAD_DIM) // LANES
    cache_spec = pl.BlockSpec((None, bb, LANES, WINDOW), lambda i: (layer, i, 0, 0))
    col_spec = pl.BlockSpec((None, LANES, bb), lambda i: (i, 0, 0))
    return pl.pallas_call(
        _attn_decode_kernel,
        grid=(nbatch // bb,),
        in_specs=[pl.BlockSpec((bb, A_HEADS, LANES), lambda i: (i, 0, 0)),
                  pl.BlockSpec((bb, LANES), lambda i: (i, kcol)),
                  pl.BlockSpec((bb, LANES), lambda i: (i, kcol + 1)),
                  col_spec, col_spec, cache_spec, cache_spec,
                  pl.BlockSpec((3, A_HEADS, WINDOW), lambda i: (0, 0, 0))],
        out_specs=[pl.BlockSpec((bb, A_HEADS, LANES), lambda i: (i, 0, 0)), cache_spec, cache_spec],
        out_shape=[jax.ShapeDtypeStruct((nbatch, A_HEADS, LANES), F32),
                   jax.ShapeDtypeStruct(cache_k.shape, F32),
                   jax.ShapeDtypeStruct(cache_v.shape, F32)],
        input_output_aliases={5: 1, 6: 2},
        compiler_params=_cparams(("arbitrary",)),
        name="attn_decode",
    )(qblk, proj_d, proj_d, knt, vnt, cache_k, cache_v, table)


def _xattn_decode_kernel(qb_ref, mk_ref, mv_ref, hm_ref, o_ref):
    qb = qb_ref[...]
    s = lax.dot_general(qb.astype(BF16), mk_ref[...].astype(BF16), (((2,), (1,)), ((0,), (0,))),
                        preferred_element_type=F32) * SCALE
    m = jnp.max(s, -1, keepdims=True)
    p = jnp.exp(s - m)
    p = p / jnp.sum(p, -1, keepdims=True)
    o = lax.dot_general(p.astype(BF16), mv_ref[...].astype(BF16), (((2,), (2,)), ((0,), (0,))),
                        preferred_element_type=F32)
    o_ref[...] = jnp.sum(o * hm_ref[...][None], axis=1)


def xattn_decode(qblk, mem_k, mem_v, hmask, layer, bb):
    nbatch = qblk.shape[0]
    mem_spec = pl.BlockSpec((None, bb, X_WIDTH, N_MEM), lambda i: (layer, i, 0, 0))
    return pl.pallas_call(
        _xattn_decode_kernel,
        grid=(nbatch // bb,),
        in_specs=[pl.BlockSpec((bb, X_QROWS, X_WIDTH), lambda i: (i, 0, 0)), mem_spec, mem_spec,
                  pl.BlockSpec((X_QROWS, X_WIDTH), lambda i: (0, 0))],
        out_specs=pl.BlockSpec((bb, X_WIDTH), lambda i: (i, 0)),
        out_shape=jax.ShapeDtypeStruct((nbatch, X_WIDTH), F32),
        compiler_params=_cparams(("parallel",)),
        name="xattn_decode",
    )(qblk, mem_k, mem_v, hmask)


def _hgrn_gates(hq, hf, loglb, log1mlb):
    ls = jnp.minimum(hf, 0.0) - jnp.log1p(jnp.exp(-jnp.abs(hf)))
    b = log1mlb + ls
    lf = jnp.maximum(loglb, b) + jnp.log1p(jnp.exp(-jnp.abs(loglb - b)))
    return _silu(hq), lf, _neg_expm1(lf)


def _hgrn_prompt_kernel(hq_ref, hf_ref, hi_ref, hg_ref, loglb_ref, log1mlb_ref, gain_ref,
                        bdm_ref, hm_ref, ob_ref, st_ref, st_scr, q_scr, k_scr, cum_scr, o_scr):
    i = pl.program_id(1)
    tt = hq_ref.shape[0]
    c = HG_CHUNK
    tb = HG_TB

    @pl.when(i == 0)
    def _():
        st_scr[...] = jnp.zeros_like(st_scr)

    qs, lf, kk = _hgrn_gates(hq_ref[...], hf_ref[...], loglb_ref[...], log1mlb_ref[...])
    row = lax.broadcasted_iota(jnp.int32, (tt, HG_WIDTH), 0) & (c - 1)
    cum = lf
    sh = 1
    while sh < c:
        cum = cum + jnp.where(row >= sh, pltpu.roll(cum, sh, 0), 0.0)
        sh *= 2
    q_scr[...] = qs
    k_scr[...] = kk
    cum_scr[...] = cum

    bdm = bdm_ref[...]
    hmask = hm_ref[...]
    def chunk(ci, carry):
        r0 = pl.multiple_of(ci * c, c)
        r = pl.ds(r0, c)
        cu = cum_scr[r, :]
        q = q_scr[r, :]
        k = k_scr[r, :]
        v = hi_ref[r, :]
        vb = v.astype(BF16)
        qb = _rb(q)
        cu2 = cu * LOG2E
        last = cu[c - 1:c, :]
        st = st_scr[...]
        o_inter = _bdot_nt(q * jnp.exp(cu), st)
        for j in range(c // tb):
            ns = tb * (j + 1)
            ti = lax.broadcasted_iota(jnp.int32, (tb, ns, HG_WIDTH), 0) + tb * j
            si = lax.broadcasted_iota(jnp.int32, (tb, ns, HG_WIDTH), 1)
            cut = cu2[tb * j:tb * (j + 1), :]
            dec = jnp.exp2(jnp.where(ti >= si, cut[:, None, :] - cu2[None, :ns, :], NEG))
            a2 = (dec * k[None, :ns, :]).astype(BF16)
            q4 = qb[tb * j:tb * (j + 1), None, :] * hmask[None, :, :]
            att = lax.dot_general(q4, a2, (((2,), (2,)), ((0,), (0,))),
                                  preferred_element_type=F32)
            w = jnp.dot(_rb(att.reshape(tb * HG_MROWS, ns)), vb[:ns, :],
                        preferred_element_type=F32).reshape(tb, HG_MROWS, HG_WIDTH)
            o_intra = jnp.sum(w * hmask[None, :, :], axis=1)
            o_scr[pl.ds(r0 + tb * j, tb), :] = o_intra + o_inter[tb * j:tb * (j + 1), :]
        upd = _bdot_tn(v, k * jnp.exp(last - cu))
        st_scr[...] = st * jnp.exp(last) + upd * bdm
        return carry

    lax.fori_loop(0, tt // c, chunk, 0, unroll=2)

    o = o_scr[...]
    ms = jnp.dot(o * o, bdm, precision=lax.Precision.HIGHEST,
                 preferred_element_type=F32) * (1.0 / HEAD_DIM)
    ob_ref[...] = o * lax.rsqrt(ms + RMS_EPS) * gain_ref[...] * _silu(hg_ref[...])

    @pl.when(i == pl.num_programs(1) - 1)
    def _():
        st_ref[...] = st_scr[...]


def hgrn_prompt(proj, loglb, log1mlb, gain4, bdm, hmask, batch, seq, tt):
    nt = seq // tt
    base = (A_HEADS + 2 * A_KV_HEADS) * HEAD_DIM // HG_WIDTH

    def col(cblk):
        return pl.BlockSpec((tt, HG_WIDTH), lambda b, i: (b * nt + i, cblk))

    row_spec = pl.BlockSpec((1, HG_WIDTH), lambda b, i: (0, 0))
    mat_spec = pl.BlockSpec((HG_WIDTH, HG_WIDTH), lambda b, i: (0, 0))
    return pl.pallas_call(
        _hgrn_prompt_kernel,
        grid=(batch, nt),
        in_specs=[col(base), col(base + 1), col(base + 2), col(base + 3),
                  row_spec, row_spec, row_spec, mat_spec,
                  pl.BlockSpec((HG_MROWS, HG_WIDTH), lambda b, i: (0, 0))],
        out_specs=[pl.BlockSpec((tt, HG_WIDTH), lambda b, i: (b * nt + i, 0)),
                   pl.BlockSpec((None, HG_WIDTH, HG_WIDTH), lambda b, i: (b, 0, 0))],
        out_shape=[jax.ShapeDtypeStruct((batch * seq, HG_WIDTH), F32),
                   jax.ShapeDtypeStruct((batch, HG_WIDTH, HG_WIDTH), F32)],
        scratch_shapes=[pltpu.VMEM((HG_WIDTH, HG_WIDTH), F32),
                        pltpu.VMEM((tt, HG_WIDTH), F32),
                        pltpu.VMEM((tt, HG_WIDTH), F32),
                        pltpu.VMEM((tt, HG_WIDTH), F32),
                        pltpu.VMEM((tt, HG_WIDTH), F32)],
        compiler_params=_cparams(("parallel", "arbitrary")),
        name="hgrn_prompt",
    )(proj, proj, proj, proj, loglb, log1mlb, gain4, bdm, hmask)


def _hgrn_decode_kernel(hq_ref, hf_ref, hi_ref, hg_ref, loglb_ref, log1mlb_ref, gain_ref, s_ref,
                        ob_ref, so_ref):
    nb = hq_ref.shape[1]
    qs, lf, kk = _hgrn_gates(hq_ref[...], hf_ref[...], loglb_ref[...], log1mlb_ref[...])
    v = hi_ref[...]
    f = jnp.exp(lf)
    s = s_ref[...].reshape(HEAD_DIM, HEAD_DIM, nb)
    att = jnp.sum(_rb(qs) * _rb(kk), axis=0, keepdims=True)
    o = _rb(att) * _rb(v) + jnp.sum(_rb(qs * f)[:, None, :] * _rb(s), axis=0)
    s_new = f[:, None, :] * s + _rb(kk)[:, None, :] * _rb(v)[None, :, :]
    so_ref[...] = s_new.reshape(HEAD_DIM * HEAD_DIM, nb)
    ms = jnp.mean(o * o, axis=0, keepdims=True)
    ob_ref[...] = o * lax.rsqrt(ms + RMS_EPS) * gain_ref[...] * _silu(hg_ref[...])


def hgrn_decode(gates_t, loglb_t, log1mlb_t, gain_t, state_t, layer):
    nb = gates_t.shape[1]

    def blk(off):
        return pl.BlockSpec((HEAD_DIM, nb), lambda h: (off * HG_HEADS + h, 0))

    par = pl.BlockSpec((HEAD_DIM, nb), lambda h: (h, 0))
    st = pl.BlockSpec((None, HEAD_DIM * HEAD_DIM, nb), lambda h: (layer, h, 0))
    return pl.pallas_call(
        _hgrn_decode_kernel,
        grid=(HG_HEADS,),
        in_specs=[blk(0), blk(1), blk(2), blk(3), par, par,
                  pl.BlockSpec((HEAD_DIM, nb), lambda h: (0, 0)), st],
        out_specs=[par, st],
        out_shape=[jax.ShapeDtypeStruct((HG_WIDTH, nb), F32),
                   jax.ShapeDtypeStruct(state_t.shape, F32)],
        input_output_aliases={7: 1},
        compiler_params=_cparams(("arbitrary",)),
        name="hgrn_decode",
    )(gates_t, gates_t, gates_t, gates_t, loglb_t, log1mlb_t, gain_t, state_t)


def _lru_gates(xc, wa_ref, ba_ref, wx_ref, bx_ref, lam_ref):
    r = jax.nn.sigmoid(_bdot(xc, wa_ref[...]) + ba_ref[...])
    gi = jax.nn.sigmoid(_bdot(xc, wx_ref[...]) + bx_ref[...])
    log_a = -LRU_C * r * _softplus(-lam_ref[...])
    a = jnp.exp(log_a)
    bterm = jnp.sqrt(_neg_expm1(2.0 * log_a)) * (gi * xc)
    return a, bterm


def _lru_prompt_kernel(lx_ref, lg_ref, cw_ref, cb_ref, wa_ref, ba_ref, wx_ref, bx_ref, lam_ref,
                       oc_ref, hl_ref, ext_scr, h_scr):
    i = pl.program_id(1)
    tt = lx_ref.shape[0]
    pad = 8

    @pl.when(i == 0)
    def _():
        ext_scr[0:pad, :] = jnp.zeros((pad, LRU_WIDTH), F32)
        h_scr[...] = jnp.zeros_like(h_scr)

    x = lx_ref[...]
    ext_scr[pad:pad + tt, :] = x
    xc = cb_ref[...] + cw_ref[CONV_W - 1:CONV_W, :] * x
    for j in range(CONV_W - 1):
        back = CONV_W - 1 - j
        xc = xc + cw_ref[j:j + 1, :] * ext_scr[pad - back:pad - back + tt, :]
    ext_scr[0:pad, :] = x[tt - pad:tt, :]

    a, bterm = _lru_gates(xc, wa_ref, ba_ref, wx_ref, bx_ref, lam_ref)
    row = lax.broadcasted_iota(jnp.int32, (tt, LRU_WIDTH), 0)
    sh = 1
    while sh < tt:
        keep = row >= sh
        b_s = jnp.where(keep, pltpu.roll(bterm, sh, 0), 0.0)
        a_s = jnp.where(keep, pltpu.roll(a, sh, 0), 1.0)
        bterm = a * b_s + bterm
        a = a * a_s
        sh *= 2
    h = a * h_scr[...] + bterm
    h_scr[...] = h[tt - 1:tt, :]
    oc_ref[...] = h * _gelu_tanh(lg_ref[...])

    @pl.when(i == pl.num_programs(1) - 1)
    def _():
        hl_ref[...] = h[tt - 1:tt, :]


def lru_prompt(proj, conv_w, conv_b, wa_bd, ba, wx_bd, bx, lam, batch, seq, tt):
    nt = seq // tt
    base = IN_COLS // LRU_WIDTH - 2

    def col(cblk):
        return pl.BlockSpec((tt, LRU_WIDTH), lambda b, i: (b * nt + i, cblk))

    row_spec = pl.BlockSpec((1, LRU_WIDTH), lambda b, i: (0, 0))
    mat_spec = pl.BlockSpec((LRU_WIDTH, LRU_WIDTH), lambda b, i: (0, 0))
    return pl.pallas_call(
        _lru_prompt_kernel,
        grid=(batch, nt),
        in_specs=[col(base), col(base + 1),
                  pl.BlockSpec((CONV_W, LRU_WIDTH), lambda b, i: (0, 0)), row_spec,
                  mat_spec, row_spec, mat_spec, row_spec, row_spec],
        out_specs=[pl.BlockSpec((tt, LRU_WIDTH), lambda b, i: (b * nt + i, 0)),
                   pl.BlockSpec((None, 1, LRU_WIDTH), lambda b, i: (b, 0, 0))],
        out_shape=[jax.ShapeDtypeStruct((batch * seq, LRU_WIDTH), F32),
                   jax.ShapeDtypeStruct((batch, 1, LRU_WIDTH), F32)],
        scratch_shapes=[pltpu.VMEM((tt + 8, LRU_WIDTH), F32),
                        pltpu.VMEM((1, LRU_WIDTH), F32)],
        compiler_params=_cparams(("parallel", "arbitrary")),
        name="lru_prompt",
    )(proj, proj, conv_w, conv_b, wa_bd, ba, wx_bd, bx, lam)


def _lru_decode_kernel(lx_ref, lg_ref, buf_ref, h0_ref, cw_ref, cb_ref, wa_ref, ba_ref, wx_ref,
                       bx_ref, lam_ref, oc_ref, hn_ref, nbuf_ref):
    x = lx_ref[...]
    buf = buf_ref[...]
    xc = cb_ref[...] + cw_ref[CONV_W - 1:CONV_W, :] * x
    for j in range(CONV_W - 1):
        xc = xc + cw_ref[j:j + 1, :] * buf[:, j * LRU_WIDTH:(j + 1) * LRU_WIDTH]
    a, bterm = _lru_gates(xc, wa_ref, ba_ref, wx_ref, bx_ref, lam_ref)
    h = a * h0_ref[...] + bterm
    hn_ref[...] = h
    oc_ref[...] = h * _gelu_tanh(lg_ref[...])
    nbuf_ref[...] = jnp.concatenate([buf[:, LRU_WIDTH:], x], axis=1)


def lru_decode(proj_d, conv_buf, h0, conv_w, conv_b, wa_bd, ba, wx_bd, bx, lam):
    nb = proj_d.shape[0]
    base = IN_COLS // LRU_WIDTH - 2
    row_spec = pl.BlockSpec((1, LRU_WIDTH), lambda i: (0, 0))
    mat_spec = pl.BlockSpec((LRU_WIDTH, LRU_WIDTH), lambda i: (0, 0))
    act = pl.BlockSpec((nb, LRU_WIDTH), lambda i: (0, 0))
    bufs = pl.BlockSpec((nb, (CONV_W - 1) * LRU_WIDTH), lambda i: (0, 0))
    return pl.pallas_call(
        _lru_decode_kernel,
        grid=(1,),
        in_specs=[pl.BlockSpec((nb, LRU_WIDTH), lambda i: (0, base)),
                  pl.BlockSpec((nb, LRU_WIDTH), lambda i: (0, base + 1)),
                  bufs, act, pl.BlockSpec((CONV_W, LRU_WIDTH), lambda i: (0, 0)), row_spec,
                  mat_spec, row_spec, mat_spec, row_spec, row_spec],
        out_specs=[act, act, bufs],
        out_shape=[jax.ShapeDtypeStruct((nb, LRU_WIDTH), F32),
                   jax.ShapeDtypeStruct((nb, LRU_WIDTH), F32),
                   jax.ShapeDtypeStruct((nb, (CONV_W - 1) * LRU_WIDTH), F32)],
        compiler_params=_cparams(("arbitrary",)),
        name="lru_decode",
    )(proj_d, proj_d, conv_buf, h0, conv_w, conv_b, wa_bd, ba, wx_bd, bx, lam)


def _xattn_prompt_kernel(x_ref, wq_ref, mk_ref, mv_ref, wo_ref, g_ref, b_ref, wr_ref, br_ref, o_ref):
    x = x_ref[...]
    q = _bdot(x, wq_ref[...]).astype(BF16)
    mk = mk_ref[...].astype(BF16)
    mv = mv_ref[...].astype(BF16)
    outs = []
    for h in range(X_HEADS):
        sl = slice(h * HEAD_DIM, (h + 1) * HEAD_DIM)
        s = _bdot_nt(q[:, sl], mk[:, sl]) * SCALE
        m = jnp.max(s, -1, keepdims=True)
        p = jnp.exp(s - m)
        p = p / jnp.sum(p, -1, keepdims=True)
        outs.append(_bdot(p, mv[:, sl]))
    o = jnp.concatenate(outs, axis=1)
    y = _layer_norm(ALPHA * x + _bdot(o, wo_ref[...]), g_ref[...], b_ref[...])
    logits = _bdot(y, wr_ref[...]) + br_ref[...]
    gate, g_idx = _route(logits)
    lane = lax.broadcasted_iota(jnp.int32, gate.shape, 1)
    for j in range(D_MODEL // LANES):
        o_ref[j] = y[:, j * LANES:(j + 1) * LANES]
    o_ref[D_MODEL // LANES] = jnp.where(lane == 0, g_idx.astype(F32), gate)


def xattn_prompt(x, wq, mem_kv, wo, g, b, wr, br, batch, seq, tt):
    nt = seq // tt
    const = lambda bi, i: (0, 0)
    return pl.pallas_call(
        _xattn_prompt_kernel,
        grid=(batch, nt),
        in_specs=[pl.BlockSpec((tt, D_MODEL), lambda bi, i: (bi * nt + i, 0)),
                  pl.BlockSpec((D_MODEL, X_WIDTH), const),
                  pl.BlockSpec((N_MEM, X_WIDTH), lambda bi, i: (bi, 0)),
                  pl.BlockSpec((N_MEM, X_WIDTH), lambda bi, i: (bi, 1)),
                  pl.BlockSpec((X_WIDTH, D_MODEL), const),
                  pl.BlockSpec((1, D_MODEL), const),
                  pl.BlockSpec((1, D_MODEL), const),
                  pl.BlockSpec((D_MODEL, ROUTER_LANES), const),
                  pl.BlockSpec((1, ROUTER_LANES), const)],
        out_specs=pl.BlockSpec((XG_PLANES, tt, LANES), lambda bi, i: (0, bi * nt + i, 0)),
        out_shape=jax.ShapeDtypeStruct((XG_PLANES, batch * seq, LANES), F32),
        compiler_params=_cparams(("parallel", "parallel")),
        name="xattn_prompt",
    )(x, wq, mem_kv, mem_kv, wo, g, b, wr, br)


def _route(logits):
    lane = lax.broadcasted_iota(jnp.int32, logits.shape, 1)
    big = jnp.int32(ROUTER_LANES)
    ninf = -jnp.inf
    gl = jnp.where(lane < N_GROUPS, logits, ninf)
    gm = jnp.max(gl, -1, keepdims=True)
    g_val = 1.0 / jnp.sum(jnp.exp(gl - gm), -1, keepdims=True)
    g_idx = jnp.min(jnp.where(gl == gm, lane, big), -1, keepdims=True)
    lo = N_GROUPS + EXP_PER_GROUP * g_idx
    el = jnp.where((lane >= lo) & (lane < lo + EXP_PER_GROUP), logits, ninf)
    v1 = jnp.max(el, -1, keepdims=True)
    i1 = jnp.min(jnp.where(el == v1, lane, big), -1, keepdims=True)
    el2 = jnp.where(lane == i1, ninf, el)
    v2 = jnp.max(el2, -1, keepdims=True)
    i2 = jnp.min(jnp.where(el2 == v2, lane, big), -1, keepdims=True)
    e2 = jnp.exp(v2 - v1)
    w1 = g_val / (1.0 + e2)
    w2 = g_val * e2 / (1.0 + e2)
    return jnp.where(lane == i1, w1, 0.0) + jnp.where(lane == i2, w2, 0.0), g_idx


def _moe_dense_kernel(x_ref, wr_ref, br_ref, wg_ref, wu_ref, wd_ref, g_ref, b_ref, o_ref,
                      gate_scr, acc_scr):
    e = pl.program_id(1)

    @pl.when(e == 0)
    def _():
        logits = _bdot(x_ref[...], wr_ref[...]) + br_ref[...]
        gate_scr[...] = _route(logits)[0]
        acc_scr[...] = jnp.zeros_like(acc_scr)

    xb = x_ref[...].astype(BF16)
    lane = lax.broadcasted_iota(jnp.int32, gate_scr.shape, 1)
    gcol = jnp.sum(jnp.where(lane == e + N_GROUPS, gate_scr[...], 0.0), -1, keepdims=True)
    hid = _silu(_bdot(xb, wg_ref[...])) * _bdot(xb, wu_ref[...])
    acc_scr[...] += _bdot(hid * gcol, wd_ref[...])

    @pl.when(e == pl.num_programs(1) - 1)
    def _():
        o_ref[...] = _layer_norm(ALPHA * x_ref[...] + acc_scr[...], g_ref[...], b_ref[...])


def moe_dense(x, wr, br, wg, wu, wd, g, b, layer, tm):
    m = x.shape[0]
    return pl.pallas_call(
        _moe_dense_kernel,
        grid=(m // tm, N_EXPERTS),
        in_specs=[pl.BlockSpec((tm, D_MODEL), lambda i, e: (i, 0)),
                  pl.BlockSpec((D_MODEL, ROUTER_LANES), lambda i, e: (0, 0)),
                  pl.BlockSpec((1, ROUTER_LANES), lambda i, e: (0, 0)),
                  pl.BlockSpec((None, None, D_MODEL, EXP_FF), lambda i, e: (layer, e, 0, 0)),
                  pl.BlockSpec((None, None, D_MODEL, EXP_FF), lambda i, e: (layer, e, 0, 0)),
                  pl.BlockSpec((None, None, EXP_FF, D_MODEL), lambda i, e: (layer, e, 0, 0)),
                  pl.BlockSpec((1, D_MODEL), lambda i, e: (0, 0)),
                  pl.BlockSpec((1, D_MODEL), lambda i, e: (0, 0))],
        out_specs=pl.BlockSpec((tm, D_MODEL), lambda i, e: (i, 0)),
        out_shape=jax.ShapeDtypeStruct((m, D_MODEL), F32),
        scratch_shapes=[pltpu.VMEM((tm, ROUTER_LANES), F32), pltpu.VMEM((tm, D_MODEL), F32)],
        compiler_params=_cparams(("parallel", "arbitrary")),
        name="moe_dense",
    )(x, wr, br, wg, wu, wd, g, b)


def _sc_mesh():
    return plsc.VectorSubcoreMesh(core_axis_name="core", subcore_axis_name="subcore")


def sc_scatter_rows(x, idx, n_out):
    r = x.shape[0]

    @functools.partial(pl.kernel, out_type=jax.ShapeDtypeStruct((n_out, LANES), x.dtype),
                       mesh=_sc_mesh(), scratch_types=[], name="sc_scatter_rows")
    def k(x_hbm, i_hbm, o_hbm):
        def body(x_vmem, i_vmem):
            pltpu.sync_copy(x_vmem, o_hbm.at[i_vmem.at[0]])

        pltpu.emit_pipeline(
            body,
            grid=(r // SC_WINDOW,),
            in_specs=[pl.BlockSpec((SC_WINDOW, LANES), lambda i: (i, 0)),
                      pl.BlockSpec((1, SC_WINDOW), lambda i: (0, i))],
            out_specs=[],
            core_axis_name=("core", "subcore"),
            dimension_semantics=(pltpu.PARALLEL,),
        )(x_hbm, i_hbm)

    return k(x, idx.reshape(1, r))


def sc_gather_rows(table, idx):
    r = idx.shape[0]

    @functools.partial(pl.kernel, out_type=jax.ShapeDtypeStruct((r, LANES), table.dtype),
                       mesh=_sc_mesh(), scratch_types=[], name="sc_gather_rows")
    def k(t_hbm, i_hbm, o_hbm):
        def body(i_vmem, o_vmem):
            pltpu.sync_copy(t_hbm.at[i_vmem.at[0]], o_vmem)

        pltpu.emit_pipeline(
            body,
            grid=(r // SC_WINDOW,),
            in_specs=[pl.BlockSpec((1, SC_WINDOW), lambda i: (0, i))],
            out_specs=[pl.BlockSpec((SC_WINDOW, LANES), lambda i: (i, 0))],
            core_axis_name=("core", "subcore"),
            dimension_semantics=(pltpu.PARALLEL,),
        )(i_hbm, o_hbm)

    return k(table, idx.reshape(1, r))


def _moe_sorted_kernel(tg_ref, nused_ref, xs_ref, wg32_ref, wu32_ref, wd32_ref, g_ref, b_ref, o_ref,
                       wg_ref, wu_ref, wd_ref):
    t = pl.program_id(0)
    used = t < nused_ref[0]

    @pl.when(used & ((t == 0) | (tg_ref[t] != tg_ref[jnp.maximum(t - 1, 0)])))
    def _():
        wg_ref[...] = wg32_ref[...].astype(BF16)
        wu_ref[...] = wu32_ref[...].astype(BF16)
        wd_ref[...] = wd32_ref[...].astype(BF16)

    @pl.when(used)
    def _():
        x = jnp.concatenate([xs_ref[j] for j in range(X_PLANES)], axis=1)
        gate = xs_ref[X_PLANES]
        xb = x.astype(BF16)
        lane = lax.broadcasted_iota(jnp.int32, gate.shape, 1)
        first = N_GROUPS + EXP_PER_GROUP * tg_ref[t]
        acc = jnp.zeros(x.shape, F32)
        for e in range(EXP_PER_GROUP):
            gcol = jnp.sum(jnp.where(lane == first + e, gate, 0.0), -1, keepdims=True)
            hid = _silu(_bdot(xb, wg_ref[e])) * _bdot(xb, wu_ref[e])
            acc = acc + _bdot(hid * gcol, wd_ref[e])
        y = _layer_norm(ALPHA * x + acc, g_ref[...], b_ref[...])
        for j in range(X_PLANES):
            o_ref[j] = y[:, j * LANES:(j + 1) * LANES]


def _group_slots(group_idx, n, tm):
    n_tiles = n // tm + N_GROUPS
    onehot = (group_idx[:, None] == jnp.arange(N_GROUPS)[None, :]).astype(jnp.int32)
    csum = jnp.cumsum(onehot, axis=0)
    counts = csum[-1]
    rank = jnp.sum(onehot * csum, axis=1) - 1
    tiles_g = (counts + tm - 1) // tm
    tile_end = jnp.cumsum(tiles_g)
    slot_base = (tile_end - tiles_g) * tm
    slot = (jnp.sum(onehot * slot_base[None, :], axis=1) + rank).astype(jnp.int32)
    tile_group = jnp.sum((jnp.arange(n_tiles)[:, None] >= tile_end[None, :]).astype(jnp.int32), axis=1)
    tile_group = jnp.minimum(tile_group, N_GROUPS - 1).astype(jnp.int32)
    return slot, tile_group, tile_end[-1:].astype(jnp.int32)


def moe_routed_sc(xg, wg, wu, wd, g, b, layer, n, tm, during_scatter, during_gather, tiles_out):
    slot, tile_group, n_used = _group_slots(xg[X_PLANES, :, 0].astype(jnp.int32), n, tm)
    n_tiles = tile_group.shape[0]
    n_slots = n_tiles * tm
    plane_base = lambda planes: jnp.arange(planes, dtype=jnp.int32)[:, None] * n_slots
    idx_in = (plane_base(XG_PLANES) + slot[None, :]).reshape(-1)
    if tiles_out:
        idx_out = (plane_base(X_PLANES)[None] + slot.reshape(n // SUBLANES, 1, SUBLANES)).reshape(-1)
    else:
        idx_out = (plane_base(X_PLANES) + slot[None, :]).reshape(-1)
    xs = sc_scatter_rows(xg.reshape(XG_PLANES * n, LANES), idx_in, XG_PLANES * n_slots)
    xs = xs.reshape(XG_PLANES, n_slots, LANES)
    xs, side = lax.optimization_barrier((xs, during_scatter()))
    wspec = lambda shp: pl.BlockSpec((None, None, EXP_PER_GROUP) + shp,
                                     lambda t, tg, nu: (layer, tg[t], 0, 0, 0))
    grid_spec = pltpu.PrefetchScalarGridSpec(
        num_scalar_prefetch=2,
        grid=(n_tiles,),
        in_specs=[pl.BlockSpec((XG_PLANES, tm, LANES), lambda t, tg, nu: (0, t, 0)),
                  wspec((D_MODEL, EXP_FF)), wspec((D_MODEL, EXP_FF)), wspec((EXP_FF, D_MODEL)),
                  pl.BlockSpec((1, D_MODEL), lambda t, tg, nu: (0, 0)),
                  pl.BlockSpec((1, D_MODEL), lambda t, tg, nu: (0, 0))],
        out_specs=pl.BlockSpec((X_PLANES, tm, LANES), lambda t, tg, nu: (0, t, 0)),
        scratch_shapes=[pltpu.VMEM((EXP_PER_GROUP, D_MODEL, EXP_FF), BF16),
                        pltpu.VMEM((EXP_PER_GROUP, D_MODEL, EXP_FF), BF16),
                        pltpu.VMEM((EXP_PER_GROUP, EXP_FF, D_MODEL), BF16)])
    grouped = lambda w: w.reshape(w.shape[0], N_GROUPS, EXP_PER_GROUP, w.shape[2], w.shape[3])
    ys = pl.pallas_call(
        _moe_sorted_kernel,
        grid_spec=grid_spec,
        out_shape=jax.ShapeDtypeStruct((X_PLANES, n_slots, LANES), F32),
        compiler_params=_cparams(("arbitrary",)),
        name="moe_sorted",
    )(tile_group, n_used, xs, grouped(wg), grouped(wu), grouped(wd), g, b)
    y = sc_gather_rows(ys.reshape(X_PLANES * n_slots, LANES), idx_out)
    y, side = lax.optimization_barrier((y, during_gather(side)))
    if tiles_out:
        y = y.reshape(n // SUBLANES, X_PLANES, SUBLANES, LANES).transpose(0, 2, 1, 3)
        return y.reshape(n, D_MODEL), side
    return y.reshape(X_PLANES, n, LANES), side


def _t5_bucket(dist):
    max_exact = N_BUCKETS // 2
    d = jnp.maximum(dist, 0)
    df = jnp.maximum(d, 1).astype(F32)
    log_b = max_exact + (jnp.log(df / max_exact) / math.log(MAX_DISTANCE / max_exact)
                         * (N_BUCKETS - max_exact)).astype(jnp.int32)
    return jnp.where(d < max_exact, d, jnp.minimum(log_b, N_BUCKETS - 1))


def _bucket_lookup(rel_bias, bucket):
    out = jnp.zeros(bucket.shape + (rel_bias.shape[1],), F32)
    for i in range(N_BUCKETS):
        out = jnp.where((bucket == i)[..., None], rel_bias[i].astype(F32), out)
    return out


def _prompt_bias(rel_bias):
    qi = jnp.arange(WINDOW)[:, None]
    kj = jnp.arange(2 * WINDOW)[None, :]
    dist = qi + WINDOW - kj
    bias = _bucket_lookup(rel_bias, _t5_bucket(dist)).transpose(2, 0, 1)
    valid = (dist >= 0) & (dist <= WINDOW)
    return jnp.where(valid[None], bias, NEG)


def _decode_table(rel_bias, attn_sink):
    dist = WINDOW - jnp.arange(WINDOW + 1)
    bias = _bucket_lookup(rel_bias, _t5_bucket(dist)).T
    depth = attn_sink.shape[0]
    wide = lambda v: jnp.broadcast_to(v[..., None], v.shape + (WINDOW,))
    per_layer = lambda t: jnp.broadcast_to(t[None], (depth,) + t.shape)
    return jnp.stack([per_layer(bias[:, :WINDOW]), per_layer(wide(bias[:, WINDOW])),
                      wide(attn_sink.astype(F32))], axis=1)


def _block_ones(width):
    idx = jnp.arange(width) // HEAD_DIM
    return (idx[:, None] == idx[None, :])


def _head_rows_mask():
    head = jnp.arange(HG_WIDTH)[None, :] // HEAD_DIM
    return (head == jnp.arange(HG_MROWS)[:, None]).astype(F32)


def _block_diag(w):
    nblk, s, _ = w.shape
    eye = jnp.eye(nblk, dtype=w.dtype)
    return (eye[:, None, :, None] * w[:, :, None, :]).reshape(nblk * s, nblk * s)


def kernel(x_prompt, x_sample, mem_prompt, cache_win_k, cache_win_v, state_hgrn, state_conv, state_lru, cache_mem_k, cache_mem_v, rel_bias, hg_lb, w_in, attn_sink, hg_gain, conv_w, conv_b, lru_wa, lru_ba, lru_wx, lru_bx, lru_lam, w_out, ln1_g, ln1_b, x_wq, x_wk, x_wv, x_wo, ln2_g, ln2_b, r_gw, r_gb, r_ew, r_eb, e_wg, e_wu, e_wd, ln3_g, ln3_b):
    bp, seq, d = x_prompt.shape
    n_tok = bp * seq
    nd = x_sample.shape[0]
    depth = w_in.shape[0]

    lbs = jnp.cumsum(jax.nn.softmax(hg_lb.astype(F32), axis=0), axis=0)
    lbs = lbs - lbs[0]
    loglb = jnp.log(lbs)
    log1mlb = jnp.log1p(-lbs)
    gain4 = jnp.tile(hg_gain, (1, HG_HEADS))

    bias_p = _prompt_bias(rel_bias)
    bdm256 = _block_ones(HG_WIDTH).astype(F32)
    hmask = _head_rows_mask()

    w_in_b = w_in.astype(BF16)
    w_out_b = w_out.astype(BF16)
    wq_b = x_wq.astype(BF16)
    wkv_b = jnp.concatenate([x_wk, x_wv], axis=-1).astype(BF16)
    wo_b = x_wo.astype(BF16)
    rew = r_ew.transpose(0, 2, 1, 3).reshape(depth, d, N_EXPERTS)
    wr = jnp.concatenate([r_gw, rew, jnp.zeros((depth, d, ROUTER_LANES - N_GROUPS - N_EXPERTS), F32)], -1)
    br = jnp.concatenate([r_gb, r_eb.reshape(depth, N_EXPERTS),
                          jnp.zeros((depth, ROUTER_LANES - N_GROUPS - N_EXPERTS), F32)], -1)

    a_w = A_HEADS * HEAD_DIM
    xp = x_prompt.reshape(bp * seq, d)
    xs = x_sample.reshape(nd, d)
    mem = mem_prompt.reshape(bp * N_MEM, d)
    ckt = cache_win_k.transpose(0, 1, 3, 4, 2).reshape(depth, nd, LANES, WINDOW)
    cvt = cache_win_v.transpose(0, 1, 3, 4, 2).reshape(depth, nd, LANES, WINDOW)
    cmkt = cache_mem_k.transpose(0, 1, 3, 4, 2).reshape(depth, nd, X_WIDTH, N_MEM)
    cmvt = cache_mem_v.transpose(0, 1, 3, 4, 2).reshape(depth, nd, X_WIDTH, N_MEM)
    state_t = state_hgrn.transpose(0, 2, 3, 4, 1).reshape(depth, HG_HEADS * HEAD_DIM * HEAD_DIM, nd)
    dec_tab = _decode_table(rel_bias, attn_sink)
    xq_mask = _head_rows_mask()[:X_QROWS]
    head_group = jnp.arange(A_HEADS) // A_REP

    p_wk, p_wv, p_s, p_cb, p_h, p_mk, p_mv = [], [], [], [], [], [], []
    s_cb, s_h = [], []
    for l in range(depth):
        row = lambda v: v[l].reshape(1, -1)
        wa_bd = _block_diag(lru_wa[l]).astype(BF16)
        wx_bd = _block_diag(lru_wx[l]).astype(BF16)
        lru_args = (conv_w[l], row(conv_b), wa_bd, row(lru_ba), wx_bd, row(lru_bx), row(lru_lam))
        wo_parts = [w_out_b[l, :a_w], w_out_b[l, a_w:a_w + HG_WIDTH], w_out_b[l, a_w + HG_WIDTH:]]

        proj = matmul(xp, w_in_b[l], 1024, IN_COLS)
        oa = attn_prompt(proj, attn_sink[l], bias_p, bp, seq)
        ob, st = hgrn_prompt(proj, row(loglb), row(log1mlb), row(gain4), bdm256, hmask, bp, seq, 1024)
        oc, hl = lru_prompt(proj, *lru_args, bp, seq, 512)
        xp = proj_res_ln(xp, [oa, ob, oc], wo_parts, row(ln1_g), row(ln1_b), 1024)
        mkv = matmul(mem, wkv_b[l], 256, 512)
        xg = xattn_prompt(xp, wq_b[l], mkv, wo_b[l], row(ln2_g), row(ln2_b), wr[l], br[l:l + 1],
                          bp, seq, 1024)
        def decode_mixers(xs=xs, ckt=ckt, cvt=cvt, state_t=state_t, l=l, lru_args=lru_args,
                          wo_parts=wo_parts, row=row):
            projd = matmul(xs, w_in_b[l], nd, 768)
            q3 = projd[:, :a_w].reshape(nd, A_HEADS, 1, HEAD_DIM)
            on_group = head_group[None, :, None, None] == jnp.arange(A_KV_HEADS)[None, None, :, None]
            qblk = jnp.where(on_group, q3, 0.0).reshape(nd, A_HEADS, LANES)
            o3, ckt, cvt = attn_decode(qblk, projd, dec_tab[l], ckt, cvt, l, 16)
            o4 = o3.reshape(nd, A_HEADS, A_KV_HEADS, HEAD_DIM)
            oa = jnp.sum(jnp.where(on_group, o4, 0.0), axis=2).reshape(nd, a_w)
            gates_t = projd[:, a_w + 2 * LANES:a_w + 2 * LANES + 4 * HG_WIDTH].T
            bc = lambda v: jnp.broadcast_to(v[:, None], (v.shape[0], nd))
            ob_t, state_t = hgrn_decode(gates_t, bc(loglb[l]), bc(log1mlb[l]), bc(hg_gain[l]), state_t, l)
            oc, nh, nbuf = lru_decode(projd, state_conv[l].reshape(nd, -1), state_lru[l], *lru_args)
            xs = proj_res_ln(xs, [oa, ob_t.T, oc], wo_parts, row(ln1_g), row(ln1_b), nd)
            return xs, ckt, cvt, state_t, nh, nbuf

        def decode_rest(side, l=l, row=row):
            xs = side[0]
            qd = matmul(xs, wq_b[l], nd, X_WIDTH)
            qdb = qd[:, None, :] * xq_mask[None, :, :]
            od = xattn_decode(qdb, cmkt, cmvt, xq_mask, l, 16)
            xs = proj_res_ln(xs, [od], [wo_b[l]], row(ln2_g), row(ln2_b), nd)
            xs = moe_dense(xs, wr[l], br[l:l + 1], e_wg, e_wu, e_wd, row(ln3_g), row(ln3_b), l, nd)
            return (xs,) + tuple(side[1:])

        xp, (xs, ckt, cvt, state_t, nh, nbuf) = moe_routed_sc(
            xg, e_wg, e_wu, e_wd, row(ln3_g), row(ln3_b), l, n_tok, MOE_TM, decode_mixers, decode_rest,
            tiles_out=(l == depth - 1))

        proj3 = proj.reshape(bp, seq, IN_COLS)
        p_wk.append(proj3[:, seq - WINDOW:, a_w:a_w + LANES].reshape(bp, WINDOW, A_KV_HEADS, HEAD_DIM))
        p_wv.append(proj3[:, seq - WINDOW:, a_w + LANES:a_w + 2 * LANES].reshape(bp, WINDOW, A_KV_HEADS, HEAD_DIM))
        st5 = st.reshape(bp, HG_HEADS, HEAD_DIM, HG_HEADS, HEAD_DIM)
        p_s.append(jnp.stack([st5[:, h, :, h, :] for h in range(HG_HEADS)], 1).transpose(0, 1, 3, 2))
        p_cb.append(proj3[:, seq - (CONV_W - 1):, IN_COLS - 2 * LRU_WIDTH:IN_COLS - LRU_WIDTH])
        p_h.append(hl.reshape(bp, LRU_WIDTH))
        p_mk.append(mkv[:, :X_WIDTH].reshape(bp, N_MEM, X_HEADS, HEAD_DIM))
        p_mv.append(mkv[:, X_WIDTH:].reshape(bp, N_MEM, X_HEADS, HEAD_DIM))

        s_cb.append(nbuf.reshape(nd, CONV_W - 1, LRU_WIDTH))
        s_h.append(nh)

    unkey = lambda c: c.reshape(depth, nd, A_KV_HEADS, HEAD_DIM, WINDOW).transpose(0, 1, 4, 2, 3)
    s_s = state_t.reshape(depth, HG_HEADS, HEAD_DIM, HEAD_DIM, nd).transpose(0, 4, 1, 2, 3)
    return (xp.reshape(bp, seq, d), xs.reshape(nd, 1, d),
            jnp.stack(p_wk), jnp.stack(p_wv), jnp.stack(p_s), jnp.stack(p_cb), jnp.stack(p_h),
            jnp.stack(p_mk), jnp.stack(p_mv),
            unkey(ckt), unkey(cvt), s_s, jnp.stack(s_cb), jnp.stack(s_h))
```

```python
import functools
import math

import jax
import jax.numpy as jnp
from jax import lax
from jax.experimental import pallas as pl
from jax.experimental.pallas import tpu as pltpu
from jax.experimental.pallas import tpu_sc as plsc

F32 = jnp.float32
BF16 = jnp.bfloat16
MIX_DTYPE = BF16

D_MODEL = 1024
DEPTH = 4
HEAD_DIM = 64
A_HEADS = 8
A_KV_HEADS = 2
A_REP = A_HEADS // A_KV_HEADS
WINDOW = 128
A_QB = 2
N_BUCKETS = 32
MAX_DISTANCE = 128
HG_WIDTH = 256
HG_HEADS = 4
HG_CHUNK = 64
HG_TB = 16
HG_MROWS = 8
LOG2E = math.log2(math.e)
LRU_WIDTH = 256
LRU_BLOCKS = 4
CONV_W = 4
LRU_C = 8.0
N_MEM = 256
X_HEADS = 4
X_WIDTH = X_HEADS * HEAD_DIM
X_QROWS = 8
N_GROUPS = 4
EXP_PER_GROUP = 4
N_EXPERTS = N_GROUPS * EXP_PER_GROUP
EXP_FF = D_MODEL // 4
ALPHA = (2 * DEPTH) ** 0.25
LN_EPS = 1e-5
RMS_EPS = 1e-6
IN_COLS = 2304
SCALE = HEAD_DIM ** -0.5
NEG = -1e30
LANES = 128
SUBLANES = 8
ROUTER_LANES = 128
XG_WIDTH = D_MODEL + ROUTER_LANES
XG_PLANES = XG_WIDTH // LANES
X_PLANES = D_MODEL // LANES
MOE_TM = 512
SC_WINDOW = 128
VMEM_LIMIT = 48 * 1024 * 1024


def _cparams(sem):
    return pltpu.CompilerParams(dimension_semantics=sem, vmem_limit_bytes=VMEM_LIMIT)


def _bdot(a, b):
    return jnp.dot(a.astype(BF16), b.astype(BF16), preferred_element_type=F32)


def _bdot_nt(a, b):
    return lax.dot_general(a.astype(BF16), b.astype(BF16), (((1,), (1,)), ((), ())),
                           preferred_element_type=F32)


def _bdot_tn(a, b):
    return lax.dot_general(a.astype(BF16), b.astype(BF16), (((0,), (0,)), ((), ())),
                           preferred_element_type=F32)


def _rb(x):
    return x.astype(BF16).astype(F32)


def _silu(x):
    return x * jax.nn.sigmoid(x)


def _neg_expm1(x):
    return -jnp.tanh(0.5 * x) * (jnp.exp(x) + 1.0)


def _softplus(x):
    return jnp.maximum(x, 0.0) + jnp.log1p(jnp.exp(-jnp.abs(x)))


def _gelu_tanh(x):
    return 0.5 * x * (1.0 + jnp.tanh(math.sqrt(2.0 / math.pi) * (x + 0.044715 * (x * x * x))))


def _layer_norm(y, g, b):
    mu = jnp.mean(y, -1, keepdims=True)
    yc = y - mu
    var = jnp.mean(yc * yc, -1, keepdims=True)
    return yc * lax.rsqrt(var + LN_EPS) * g + b


def _rows(x_ref):
    if len(x_ref.shape) == 2:
        return x_ref[...]
    return jnp.concatenate([x_ref[j] for j in range(x_ref.shape[0])], axis=1)


def _rows_spec(x, tm, nargs):
    if x.ndim == 2:
        return pl.BlockSpec((tm, x.shape[1]), (lambda i: (i, 0)) if nargs == 1 else (lambda i, j: (i, 0)))
    blk = (x.shape[0], tm, LANES)
    return pl.BlockSpec(blk, (lambda i: (0, i, 0)) if nargs == 1 else (lambda i, j: (0, i, 0)))


def _mm_kernel(x_ref, w_ref, o_ref):
    o_ref[...] = _bdot(_rows(x_ref), w_ref[...])


def matmul(x, w, tm, tn):
    m = x.shape[-2]
    k, n = w.shape
    return pl.pallas_call(
        _mm_kernel,
        grid=(m // tm, n // tn),
        in_specs=[_rows_spec(x, tm, 2),
                  pl.BlockSpec((k, tn), lambda i, j: (0, j))],
        out_specs=pl.BlockSpec((tm, tn), lambda i, j: (i, j)),
        out_shape=jax.ShapeDtypeStruct((m, n), F32),
        compiler_params=_cparams(("parallel", "parallel")),
        name="matmul",
    )(x, w)


def _proj_res_ln_kernel(n_in, x_ref, *refs):
    a_refs = refs[:n_in]
    w_refs = refs[n_in:2 * n_in]
    g_ref, b_ref, o_ref = refs[2 * n_in:]
    y = ALPHA * _rows(x_ref)
    for a_ref, w_ref in zip(a_refs, w_refs):
        y = y + _bdot(a_ref[...], w_ref[...])
    o_ref[...] = _layer_norm(y, g_ref[...], b_ref[...])


def proj_res_ln(x, a_list, w_list, g, b, tm):
    m = x.shape[-2]
    d = w_list[0].shape[1]
    n_in = len(a_list)
    in_specs = [_rows_spec(x, tm, 1)]
    in_specs += [pl.BlockSpec((tm, a.shape[1]), lambda i: (i, 0)) for a in a_list]
    in_specs += [pl.BlockSpec(w.shape, lambda i: (0, 0)) for w in w_list]
    in_specs += [pl.BlockSpec((1, d), lambda i: (0, 0))] * 2
    return pl.pallas_call(
        functools.partial(_proj_res_ln_kernel, n_in),
        grid=(m // tm,),
        in_specs=in_specs,
        out_specs=pl.BlockSpec((tm, d), lambda i: (i, 0)),
        out_shape=jax.ShapeDtypeStruct((m, d), F32),
        compiler_params=_cparams(("parallel",)),
        name="proj_res_ln",
    )(x, *a_list, *w_list, g, b)


def _attn_prompt_kernel(sink_ref, q_ref, kc_ref, kp_ref, vc_ref, vp_ref, bias_ref, o_ref):
    n = pl.program_id(1)
    col = lax.broadcasted_iota(jnp.int32, (WINDOW, 2 * WINDOW), 1)
    first = jnp.where((n == 0) & (col < WINDOW), NEG, 0.0)
    kk = jnp.concatenate([kp_ref[...], kc_ref[...]], axis=0).astype(BF16)
    vv = jnp.concatenate([vp_ref[...], vc_ref[...]], axis=0).astype(BF16)
    q = q_ref[...].astype(BF16)
    for u in range(A_QB):
        rows = slice(u * WINDOW, (u + 1) * WINDOW)
        keys = slice(u * WINDOW, (u + 2) * WINDOW)
        outs = []
        for h in range(A_HEADS):
            g = h // A_REP
            qh = q[rows, h * HEAD_DIM:(h + 1) * HEAD_DIM]
            kg = kk[keys, g * HEAD_DIM:(g + 1) * HEAD_DIM]
            vg = vv[keys, g * HEAD_DIM:(g + 1) * HEAD_DIM]
            s = _bdot_nt(qh, kg) * SCALE + bias_ref[h]
            if u == 0:
                s = s + first
            sink = sink_ref[h]
            m = jnp.maximum(jnp.max(s, -1, keepdims=True), sink)
            p = jnp.exp(s - m)
            den = jnp.sum(p, -1, keepdims=True) + jnp.exp(sink - m)
            outs.append(_bdot(p / den, vg))
        o_ref[rows, :] = jnp.concatenate(outs, axis=1).astype(o_ref.dtype)


def attn_prompt(proj, sink, bias, batch, seq):
    nb = seq // WINDOW
    ns = nb // A_QB
    tq = A_QB * WINDOW
    qcol = 0
    kcol = (A_HEADS * HEAD_DIM) // LANES
    vcol = kcol + 1

    def cur(c):
        return lambda b, n: (b * ns + n, c)

    def prev(c):
        return lambda b, n: (b * nb + jnp.maximum(n * A_QB - 1, 0), c)

    return pl.pallas_call(
        _attn_prompt_kernel,
        grid=(batch, ns),
        in_specs=[pl.BlockSpec(memory_space=pltpu.SMEM),
                  pl.BlockSpec((tq, A_HEADS * HEAD_DIM), cur(qcol)),
                  pl.BlockSpec((tq, LANES), cur(kcol)),
                  pl.BlockSpec((WINDOW, LANES), prev(kcol)),
                  pl.BlockSpec((tq, LANES), cur(vcol)),
                  pl.BlockSpec((WINDOW, LANES), prev(vcol)),
                  pl.BlockSpec((A_HEADS, WINDOW, 2 * WINDOW), lambda b, n: (0, 0, 0))],
        out_specs=pl.BlockSpec((tq, A_HEADS * HEAD_DIM), cur(0)),
        out_shape=jax.ShapeDtypeStruct((batch * seq, A_HEADS * HEAD_DIM), MIX_DTYPE),
        compiler_params=_cparams(("parallel", "parallel")),
        name="attn_prompt",
    )(sink, proj, proj, proj, proj, proj, bias)


def _attn_decode_kernel(qb_ref, kn_ref, vn_ref, knt_ref, vnt_ref, ck_ref, cv_ref, tab_ref,
                        o_ref, ok_ref, ov_ref):
    bb = qb_ref.shape[0]
    ck = ck_ref[...]
    cv = cv_ref[...]
    qb = qb_ref[...]
    kn = kn_ref[...]
    vn = vn_ref[...]
    bias_j = tab_ref[0]
    bias_n = tab_ref[1][:, 0:1]
    sink = tab_ref[2][:, 0:1]
    s = lax.dot_general(qb.astype(BF16), ck.astype(BF16), (((2,), (1,)), ((0,), (0,))),
                        preferred_element_type=F32) * SCALE + bias_j[None]
    sn = jnp.sum(_rb(qb) * _rb(kn)[:, None, :], -1, keepdims=True) * SCALE + bias_n[None]
    m = jnp.maximum(jnp.maximum(jnp.max(s, -1, keepdims=True), sn), sink[None])
    p = jnp.exp(s - m)
    pn = jnp.exp(sn - m)
    den = jnp.sum(p, -1, keepdims=True) + pn + jnp.exp(sink[None] - m)
    o = lax.dot_general((p / den).astype(BF16), cv.astype(BF16), (((2,), (2,)), ((0,), (0,))),
                        preferred_element_type=F32)
    o_ref[...] = o + _rb(pn / den) * _rb(vn)[:, None, :]
    lane = lax.broadcasted_iota(jnp.int32, (LANES, LANES), 1)
    for b in range(bb):
        ok_ref[b] = jnp.where(lane == WINDOW - 1, knt_ref[:, b:b + 1], pltpu.roll(ck[b], WINDOW - 1, 1))
        ov_ref[b] = jnp.where(lane == WINDOW - 1, vnt_ref[:, b:b + 1], pltpu.roll(cv[b], WINDOW - 1, 1))


def attn_decode(qblk, proj_d, table, cache_k, cache_v, layer, bb):
    nbatch = proj_d.shape[0]
    a_w = A_HEADS * HEAD_DIM
    cols = lambda c: proj_d[:, c:c + LANES].reshape(nbatch // bb, bb, LANES).transpose(0, 2, 1)
    knt, vnt = cols(a_w), cols(a_w + LANES)
    kcol = (A_HEADS * HEAD_DIM) // LANES
    cache_spec = pl.BlockSpec((None, bb, LANES, WINDOW), lambda i: (layer, i, 0, 0))
    col_spec = pl.BlockSpec((None, LANES, bb), lambda i: (i, 0, 0))
    return pl.pallas_call(
        _attn_decode_kernel,
        grid=(nbatch // bb,),
        in_specs=[pl.BlockSpec((bb, A_HEADS, LANES), lambda i: (i, 0, 0)),
                  pl.BlockSpec((bb, LANES), lambda i: (i, kcol)),
                  pl.BlockSpec((bb, LANES), lambda i: (i, kcol + 1)),
                  col_spec, col_spec, cache_spec, cache_spec,
                  pl.BlockSpec((3, A_HEADS, WINDOW), lambda i: (0, 0, 0))],
        out_specs=[pl.BlockSpec((bb, A_HEADS, LANES), lambda i: (i, 0, 0)), cache_spec, cache_spec],
        out_shape=[jax.ShapeDtypeStruct((nbatch, A_HEADS, LANES), F32),
                   jax.ShapeDtypeStruct(cache_k.shape, F32),
                   jax.ShapeDtypeStruct(cache_v.shape, F32)],
        input_output_aliases={5: 1, 6: 2},
        compiler_params=_cparams(("arbitrary",)),
        name="attn_decode",
    )(qblk, proj_d, proj_d, knt, vnt, cache_k, cache_v, table)


def _xattn_decode_kernel(qb_ref, mk_ref, mv_ref, hm_ref, o_ref):
    qb = qb_ref[...]
    s = lax.dot_general(qb.astype(BF16), mk_ref[...].astype(BF16), (((2,), (1,)), ((0,), (0,))),
                        preferred_element_type=F32) * SCALE
    m = jnp.max(s, -1, keepdims=True)
    p = jnp.exp(s - m)
    p = p / jnp.sum(p, -1, keepdims=True)
    o = lax.dot_general(p.astype(BF16), mv_ref[...].astype(BF16), (((2,), (2,)), ((0,), (0,))),
                        preferred_element_type=F32)
    o_ref[...] = jnp.sum(o * hm_ref[...][None], axis=1)


def xattn_decode(qblk, mem_k, mem_v, hmask, layer, bb):
    nbatch = qblk.shape[0]
    mem_spec = pl.BlockSpec((None, bb, X_WIDTH, N_MEM), lambda i: (layer, i, 0, 0))
    return pl.pallas_call(
        _xattn_decode_kernel,
        grid=(nbatch // bb,),
        in_specs=[pl.BlockSpec((bb, X_QROWS, X_WIDTH), lambda i: (i, 0, 0)), mem_spec, mem_spec,
                  pl.BlockSpec((X_QROWS, X_WIDTH), lambda i: (0, 0))],
        out_specs=pl.BlockSpec((bb, X_WIDTH), lambda i: (i, 0)),
        out_shape=jax.ShapeDtypeStruct((nbatch, X_WIDTH), F32),
        compiler_params=_cparams(("parallel",)),
        name="xattn_decode",
    )(qblk, mem_k, mem_v, hmask)


def _hgrn_gates(hq, hf, loglb, log1mlb):
    ls = jnp.minimum(hf, 0.0) - jnp.log1p(jnp.exp(-jnp.abs(hf)))
    b = log1mlb + ls
    lf = jnp.maximum(loglb, b) + jnp.log1p(jnp.exp(-jnp.abs(loglb - b)))
    return _silu(hq), lf, _neg_expm1(lf)


def _hgrn_prompt_kernel(hq_ref, hf_ref, hi_ref, hg_ref, loglb_ref, log1mlb_ref, gain_ref,
                        bdm_ref, hm_ref, ob_ref, st_ref, st_scr, q_scr, k_scr, cum_scr, o_scr):
    i = pl.program_id(1)
    tt = hq_ref.shape[0]
    c = HG_CHUNK
    tb = HG_TB

    @pl.when(i == 0)
    def _():
        st_scr[...] = jnp.zeros_like(st_scr)

    qs, lf, kk = _hgrn_gates(hq_ref[...], hf_ref[...], loglb_ref[...], log1mlb_ref[...])
    row = lax.broadcasted_iota(jnp.int32, (tt, HG_WIDTH), 0) & (c - 1)
    cum = lf
    sh = 1
    while sh < c:
        cum = cum + jnp.where(row >= sh, pltpu.roll(cum, sh, 0), 0.0)
        sh *= 2
    q_scr[...] = qs
    k_scr[...] = kk
    cum_scr[...] = cum

    bdm = bdm_ref[...]
    hmask = hm_ref[...]
    def chunk(ci, carry):
        r0 = pl.multiple_of(ci * c, c)
        r = pl.ds(r0, c)
        cu = cum_scr[r, :]
        q = q_scr[r, :]
        k = k_scr[r, :]
        v = hi_ref[r, :]
        vb = v.astype(BF16)
        qb = _rb(q)
        cu2 = cu * LOG2E
        last = cu[c - 1:c, :]
        st = st_scr[...]
        o_inter = _bdot_nt(q * jnp.exp(cu), st)
        for j in range(c // tb):
            ns = tb * (j + 1)
            ti = lax.broadcasted_iota(jnp.int32, (tb, ns, HG_WIDTH), 0) + tb * j
            si = lax.broadcasted_iota(jnp.int32, (tb, ns, HG_WIDTH), 1)
            cut = cu2[tb * j:tb * (j + 1), :]
            dec = jnp.exp2(jnp.where(ti >= si, cut[:, None, :] - cu2[None, :ns, :], NEG))
            a2 = (dec * k[None, :ns, :]).astype(BF16)
            q4 = qb[tb * j:tb * (j + 1), None, :] * hmask[None, :, :]
            att = lax.dot_general(q4, a2, (((2,), (2,)), ((0,), (0,))),
                                  preferred_element_type=F32)
            w = jnp.dot(_rb(att.reshape(tb * HG_MROWS, ns)), vb[:ns, :],
                        preferred_element_type=F32).reshape(tb, HG_MROWS, HG_WIDTH)
            o_intra = jnp.sum(w * hmask[None, :, :], axis=1)
            o_scr[pl.ds(r0 + tb * j, tb), :] = o_intra + o_inter[tb * j:tb * (j + 1), :]
        upd = _bdot_tn(v, k * jnp.exp(last - cu))
        st_scr[...] = st * jnp.exp(last) + upd * bdm
        return carry

    lax.fori_loop(0, tt // c, chunk, 0, unroll=2)

    o = o_scr[...]
    ms = jnp.dot(o * o, bdm, precision=lax.Precision.HIGHEST,
                 preferred_element_type=F32) * (1.0 / HEAD_DIM)
    ob_ref[...] = (o * lax.rsqrt(ms + RMS_EPS) * gain_ref[...] * _silu(hg_ref[...])).astype(ob_ref.dtype)

    @pl.when(i == pl.num_programs(1) - 1)
    def _():
        st_ref[...] = st_scr[...]


def hgrn_prompt(proj, loglb, log1mlb, gain4, bdm, hmask, batch, seq, tt):
    nt = seq // tt
    base = (A_HEADS + 2 * A_KV_HEADS) * HEAD_DIM // HG_WIDTH

    def col(cblk):
        return pl.BlockSpec((tt, HG_WIDTH), lambda b, i: (b * nt + i, cblk))

    row_spec = pl.BlockSpec((1, HG_WIDTH), lambda b, i: (0, 0))
    mat_spec = pl.BlockSpec((HG_WIDTH, HG_WIDTH), lambda b, i: (0, 0))
    return pl.pallas_call(
        _hgrn_prompt_kernel,
        grid=(batch, nt),
        in_specs=[col(base), col(base + 1), col(base + 2), col(base + 3),
                  row_spec, row_spec, row_spec, mat_spec,
                  pl.BlockSpec((HG_MROWS, HG_WIDTH), lambda b, i: (0, 0))],
        out_specs=[pl.BlockSpec((tt, HG_WIDTH), lambda b, i: (b * nt + i, 0)),
                   pl.BlockSpec((None, HG_WIDTH, HG_WIDTH), lambda b, i: (b, 0, 0))],
        out_shape=[jax.ShapeDtypeStruct((batch * seq, HG_WIDTH), MIX_DTYPE),
                   jax.ShapeDtypeStruct((batch, HG_WIDTH, HG_WIDTH), F32)],
        scratch_shapes=[pltpu.VMEM((HG_WIDTH, HG_WIDTH), F32),
                        pltpu.VMEM((tt, HG_WIDTH), F32),
                        pltpu.VMEM((tt, HG_WIDTH), F32),
                        pltpu.VMEM((tt, HG_WIDTH), F32),
                        pltpu.VMEM((tt, HG_WIDTH), F32)],
        compiler_params=_cparams(("parallel", "arbitrary")),
        name="hgrn_prompt",
    )(proj, proj, proj, proj, loglb, log1mlb, gain4, bdm, hmask)


def _hgrn_decode_kernel(hq_ref, hf_ref, hi_ref, hg_ref, loglb_ref, log1mlb_ref, gain_ref, s_ref,
                        ob_ref, so_ref):
    nb = hq_ref.shape[1]
    qs, lf, kk = _hgrn_gates(hq_ref[...], hf_ref[...], loglb_ref[...], log1mlb_ref[...])
    v = hi_ref[...]
    f = jnp.exp(lf)
    s = s_ref[...].reshape(HEAD_DIM, HEAD_DIM, nb)
    att = jnp.sum(_rb(qs) * _rb(kk), axis=0, keepdims=True)
    o = _rb(att) * _rb(v) + jnp.sum(_rb(qs * f)[:, None, :] * _rb(s), axis=0)
    s_new = f[:, None, :] * s + _rb(kk)[:, None, :] * _rb(v)[None, :, :]
    so_ref[...] = s_new.reshape(HEAD_DIM * HEAD_DIM, nb)
    ms = jnp.mean(o * o, axis=0, keepdims=True)
    ob_ref[...] = (o * lax.rsqrt(ms + RMS_EPS) * gain_ref[...] * _silu(hg_ref[...])).astype(ob_ref.dtype)


def hgrn_decode(gates_t, loglb_t, log1mlb_t, gain_t, state_t, layer):
    nb = gates_t.shape[1]

    def blk(off):
        return pl.BlockSpec((HEAD_DIM, nb), lambda h: (off * HG_HEADS + h, 0))

    par = pl.BlockSpec((HEAD_DIM, nb), lambda h: (h, 0))
    st = pl.BlockSpec((None, HEAD_DIM * HEAD_DIM, nb), lambda h: (layer, h, 0))
    return pl.pallas_call(
        _hgrn_decode_kernel,
        grid=(HG_HEADS,),
        in_specs=[blk(0), blk(1), blk(2), blk(3), par, par,
                  pl.BlockSpec((HEAD_DIM, nb), lambda h: (0, 0)), st],
        out_specs=[par, st],
        out_shape=[jax.ShapeDtypeStruct((HG_WIDTH, nb), F32),
                   jax.ShapeDtypeStruct(state_t.shape, F32)],
        input_output_aliases={7: 1},
        compiler_params=_cparams(("arbitrary",)),
        name="hgrn_decode",
    )(gates_t, gates_t, gates_t, gates_t, loglb_t, log1mlb_t, gain_t, state_t)


def _lru_gates(xc, wa_ref, ba_ref, wx_ref, bx_ref, lam_ref):
    r = jax.nn.sigmoid(_bdot(xc, wa_ref[...]) + ba_ref[...])
    gi = jax.nn.sigmoid(_bdot(xc, wx_ref[...]) + bx_ref[...])
    log_a = -LRU_C * r * _softplus(-lam_ref[...])
    a = jnp.exp(log_a)
    bterm = jnp.sqrt(_neg_expm1(2.0 * log_a)) * (gi * xc)
    return a, bterm


def _lru_prompt_kernel(lx_ref, lg_ref, cw_ref, cb_ref, wa_ref, ba_ref, wx_ref, bx_ref, lam_ref,
                       oc_ref, hl_ref, ext_scr, h_scr):
    i = pl.program_id(1)
    tt = lx_ref.shape[0]
    pad = 8

    @pl.when(i == 0)
    def _():
        ext_scr[0:pad, :] = jnp.zeros((pad, LRU_WIDTH), F32)
        h_scr[...] = jnp.zeros_like(h_scr)

    x = lx_ref[...]
    ext_scr[pad:pad + tt, :] = x
    xc = cb_ref[...] + cw_ref[CONV_W - 1:CONV_W, :] * x
    for j in range(CONV_W - 1):
        back = CONV_W - 1 - j
        xc = xc + cw_ref[j:j + 1, :] * ext_scr[pad - back:pad - back + tt, :]
    ext_scr[0:pad, :] = x[tt - pad:tt, :]

    a, bterm = _lru_gates(xc, wa_ref, ba_ref, wx_ref, bx_ref, lam_ref)
    row = lax.broadcasted_iota(jnp.int32, (tt, LRU_WIDTH), 0)
    sh = 1
    while sh < tt:
        keep = row >= sh
        b_s = jnp.where(keep, pltpu.roll(bterm, sh, 0), 0.0)
        a_s = jnp.where(keep, pltpu.roll(a, sh, 0), 1.0)
        bterm = a * b_s + bterm
        a = a * a_s
        sh *= 2
    h = a * h_scr[...] + bterm
    h_scr[...] = h[tt - 1:tt, :]
    oc_ref[...] = (h * _gelu_tanh(lg_ref[...])).astype(oc_ref.dtype)

    @pl.when(i == pl.num_programs(1) - 1)
    def _():
        hl_ref[...] = h[tt - 1:tt, :]


def lru_prompt(proj, conv_w, conv_b, wa_bd, ba, wx_bd, bx, lam, batch, seq, tt):
    nt = seq // tt
    base = IN_COLS // LRU_WIDTH - 2

    def col(cblk):
        return pl.BlockSpec((tt, LRU_WIDTH), lambda b, i: (b * nt + i, cblk))

    row_spec = pl.BlockSpec((1, LRU_WIDTH), lambda b, i: (0, 0))
    mat_spec = pl.BlockSpec((LRU_WIDTH, LRU_WIDTH), lambda b, i: (0, 0))
    return pl.pallas_call(
        _lru_prompt_kernel,
        grid=(batch, nt),
        in_specs=[col(base), col(base + 1),
                  pl.BlockSpec((CONV_W, LRU_WIDTH), lambda b, i: (0, 0)), row_spec,
                  mat_spec, row_spec, mat_spec, row_spec, row_spec],
        out_specs=[pl.BlockSpec((tt, LRU_WIDTH), lambda b, i: (b * nt + i, 0)),
                   pl.BlockSpec((None, 1, LRU_WIDTH), lambda b, i: (b, 0, 0))],
        out_shape=[jax.ShapeDtypeStruct((batch * seq, LRU_WIDTH), MIX_DTYPE),
                   jax.ShapeDtypeStruct((batch, 1, LRU_WIDTH), F32)],
        scratch_shapes=[pltpu.VMEM((tt + 8, LRU_WIDTH), F32),
                        pltpu.VMEM((1, LRU_WIDTH), F32)],
        compiler_params=_cparams(("parallel", "arbitrary")),
        name="lru_prompt",
    )(proj, proj, conv_w, conv_b, wa_bd, ba, wx_bd, bx, lam)


def _lru_decode_kernel(lx_ref, lg_ref, buf_ref, h0_ref, cw_ref, cb_ref, wa_ref, ba_ref, wx_ref,
                       bx_ref, lam_ref, oc_ref, hn_ref, nbuf_ref):
    x = lx_ref[...]
    buf = buf_ref[...]
    xc = cb_ref[...] + cw_ref[CONV_W - 1:CONV_W, :] * x
    for j in range(CONV_W - 1):
        xc = xc + cw_ref[j:j + 1, :] * buf[:, j * LRU_WIDTH:(j + 1) * LRU_WIDTH]
    a, bterm = _lru_gates(xc, wa_ref, ba_ref, wx_ref, bx_ref, lam_ref)
    h = a * h0_ref[...] + bterm
    hn_ref[...] = h
    oc_ref[...] = (h * _gelu_tanh(lg_ref[...])).astype(oc_ref.dtype)
    nbuf_ref[...] = jnp.concatenate([buf[:, LRU_WIDTH:], x], axis=1)


def lru_decode(proj_d, conv_buf, h0, conv_w, conv_b, wa_bd, ba, wx_bd, bx, lam):
    nb = proj_d.shape[0]
    base = IN_COLS // LRU_WIDTH - 2
    row_spec = pl.BlockSpec((1, LRU_WIDTH), lambda i: (0, 0))
    mat_spec = pl.BlockSpec((LRU_WIDTH, LRU_WIDTH), lambda i: (0, 0))
    act = pl.BlockSpec((nb, LRU_WIDTH), lambda i: (0, 0))
    bufs = pl.BlockSpec((nb, (CONV_W - 1) * LRU_WIDTH), lambda i: (0, 0))
    return pl.pallas_call(
        _lru_decode_kernel,
        grid=(1,),
        in_specs=[pl.BlockSpec((nb, LRU_WIDTH), lambda i: (0, base)),
                  pl.BlockSpec((nb, LRU_WIDTH), lambda i: (0, base + 1)),
                  bufs, act, pl.BlockSpec((CONV_W, LRU_WIDTH), lambda i: (0, 0)), row_spec,
                  mat_spec, row_spec, mat_spec, row_spec, row_spec],
        out_specs=[act, act, bufs],
        out_shape=[jax.ShapeDtypeStruct((nb, LRU_WIDTH), F32),
                   jax.ShapeDtypeStruct((nb, LRU_WIDTH), F32),
                   jax.ShapeDtypeStruct((nb, (CONV_W - 1) * LRU_WIDTH), F32)],
        compiler_params=_cparams(("arbitrary",)),
        name="lru_decode",
    )(proj_d, proj_d, conv_buf, h0, conv_w, conv_b, wa_bd, ba, wx_bd, bx, lam)


def _xattn_prompt_kernel(x_ref, wq_ref, mk_ref, mv_ref, wo_ref, g_ref, b_ref, wr_ref, br_ref, o_ref):
    x = x_ref[...]
    q = _bdot(x, wq_ref[...]).astype(BF16)
    mk = mk_ref[...].astype(BF16)
    mv = mv_ref[...].astype(BF16)
    outs = []
    for h in range(X_HEADS):
        sl = slice(h * HEAD_DIM, (h + 1) * HEAD_DIM)
        s = _bdot_nt(q[:, sl], mk[:, sl]) * SCALE
        m = jnp.max(s, -1, keepdims=True)
        p = jnp.exp(s - m)
        p = p / jnp.sum(p, -1, keepdims=True)
        outs.append(_bdot(p, mv[:, sl]))
    o = jnp.concatenate(outs, axis=1)
    y = _layer_norm(ALPHA * x + _bdot(o, wo_ref[...]), g_ref[...], b_ref[...])
    logits = _bdot(y, wr_ref[...]) + br_ref[...]
    gate, g_idx = _route(logits)
    lane = lax.broadcasted_iota(jnp.int32, gate.shape, 1)
    for j in range(D_MODEL // LANES):
        o_ref[j] = y[:, j * LANES:(j + 1) * LANES]
    o_ref[D_MODEL // LANES] = jnp.where(lane == 0, g_idx.astype(F32), gate)


def xattn_prompt(x, wq, mem_kv, wo, g, b, wr, br, batch, seq, tt):
    nt = seq // tt
    const = lambda bi, i: (0, 0)
    return pl.pallas_call(
        _xattn_prompt_kernel,
        grid=(batch, nt),
        in_specs=[pl.BlockSpec((tt, D_MODEL), lambda bi, i: (bi * nt + i, 0)),
                  pl.BlockSpec((D_MODEL, X_WIDTH), const),
                  pl.BlockSpec((N_MEM, X_WIDTH), lambda bi, i: (bi, 0)),
                  pl.BlockSpec((N_MEM, X_WIDTH), lambda bi, i: (bi, 1)),
                  pl.BlockSpec((X_WIDTH, D_MODEL), const),
                  pl.BlockSpec((1, D_MODEL), const),
                  pl.BlockSpec((1, D_MODEL), const),
                  pl.BlockSpec((D_MODEL, ROUTER_LANES), const),
                  pl.BlockSpec((1, ROUTER_LANES), const)],
        out_specs=pl.BlockSpec((XG_PLANES, tt, LANES), lambda bi, i: (0, bi * nt + i, 0)),
        out_shape=jax.ShapeDtypeStruct((XG_PLANES, batch * seq, LANES), F32),
        compiler_params=_cparams(("parallel", "parallel")),
        name="xattn_prompt",
    )(x, wq, mem_kv, mem_kv, wo, g, b, wr, br)


def _route(logits):
    lane = lax.broadcasted_iota(jnp.int32, logits.shape, 1)
    big = jnp.int32(ROUTER_LANES)
    ninf = -jnp.inf
    gl = jnp.where(lane < N_GROUPS, logits, ninf)
    gm = jnp.max(gl, -1, keepdims=True)
    g_val = 1.0 / jnp.sum(jnp.exp(gl - gm), -1, keepdims=True)
    g_idx = jnp.min(jnp.where(gl == gm, lane, big), -1, keepdims=True)
    lo = N_GROUPS + EXP_PER_GROUP * g_idx
    el = jnp.where((lane >= lo) & (lane < lo + EXP_PER_GROUP), logits, ninf)
    v1 = jnp.max(el, -1, keepdims=True)
    i1 = jnp.min(jnp.where(el == v1, lane, big), -1, keepdims=True)
    el2 = jnp.where(lane == i1, ninf, el)
    v2 = jnp.max(el2, -1, keepdims=True)
    i2 = jnp.min(jnp.where(el2 == v2, lane, big), -1, keepdims=True)
    e2 = jnp.exp(v2 - v1)
    w1 = g_val / (1.0 + e2)
    w2 = g_val * e2 / (1.0 + e2)
    return jnp.where(lane == i1, w1, 0.0) + jnp.where(lane == i2, w2, 0.0), g_idx


def _moe_dense_kernel(x_ref, wr_ref, br_ref, wg_ref, wu_ref, wd_ref, g_ref, b_ref, o_ref,
                      gate_scr, acc_scr):
    e = pl.program_id(1)

    @pl.when(e == 0)
    def _():
        logits = _bdot(x_ref[...], wr_ref[...]) + br_ref[...]
        gate_scr[...] = _route(logits)[0]
        acc_scr[...] = jnp.zeros_like(acc_scr)

    xb = x_ref[...].astype(BF16)
    lane = lax.broadcasted_iota(jnp.int32, gate_scr.shape, 1)
    gcol = jnp.sum(jnp.where(lane == e + N_GROUPS, gate_scr[...], 0.0), -1, keepdims=True)
    hid = _silu(_bdot(xb, wg_ref[...])) * _bdot(xb, wu_ref[...])
    acc_scr[...] += _bdot(hid * gcol, wd_ref[...])

    @pl.when(e == pl.num_programs(1) - 1)
    def _():
        o_ref[...] = _layer_norm(ALPHA * x_ref[...] + acc_scr[...], g_ref[...], b_ref[...])


def moe_dense(x, wr, br, wg, wu, wd, g, b, layer, tm):
    m = x.shape[0]
    return pl.pallas_call(
        _moe_dense_kernel,
        grid=(m // tm, N_EXPERTS),
        in_specs=[pl.BlockSpec((tm, D_MODEL), lambda i, e: (i, 0)),
                  pl.BlockSpec((D_MODEL, ROUTER_LANES), lambda i, e: (0, 0)),
                  pl.BlockSpec((1, ROUTER_LANES), lambda i, e: (0, 0)),
                  pl.BlockSpec((None, None, D_MODEL, EXP_FF), lambda i, e: (layer, e, 0, 0)),
                  pl.BlockSpec((None, None, D_MODEL, EXP_FF), lambda i, e: (layer, e, 0, 0)),
                  pl.BlockSpec((None, None, EXP_FF, D_MODEL), lambda i, e: (layer, e, 0, 0)),
                  pl.BlockSpec((1, D_MODEL), lambda i, e: (0, 0)),
                  pl.BlockSpec((1, D_MODEL), lambda i, e: (0, 0))],
        out_specs=pl.BlockSpec((tm, D_MODEL), lambda i, e: (i, 0)),
        out_shape=jax.ShapeDtypeStruct((m, D_MODEL), F32),
        scratch_shapes=[pltpu.VMEM((tm, ROUTER_LANES), F32), pltpu.VMEM((tm, D_MODEL), F32)],
        compiler_params=_cparams(("parallel", "arbitrary")),
        name="moe_dense",
    )(x, wr, br, wg, wu, wd, g, b)


def _sc_mesh():
    return plsc.VectorSubcoreMesh(core_axis_name="core", subcore_axis_name="subcore")


def sc_scatter_rows(x, idx, n_out):
    r = x.shape[0]

    @functools.partial(pl.kernel, out_type=jax.ShapeDtypeStruct((n_out, LANES), x.dtype),
                       mesh=_sc_mesh(), scratch_types=[], name="sc_scatter_rows")
    def k(x_hbm, i_hbm, o_hbm):
        def body(x_vmem, i_vmem):
            pltpu.sync_copy(x_vmem, o_hbm.at[i_vmem.at[0]])

        pltpu.emit_pipeline(
            body,
            grid=(r // SC_WINDOW,),
            in_specs=[pl.BlockSpec((SC_WINDOW, LANES), lambda i: (i, 0)),
                      pl.BlockSpec((1, SC_WINDOW), lambda i: (0, i))],
            out_specs=[],
            core_axis_name=("core", "subcore"),
            dimension_semantics=(pltpu.PARALLEL,),
        )(x_hbm, i_hbm)

    return k(x, idx.reshape(1, r))


def sc_gather_rows(table, idx):
    r = idx.shape[0]

    @functools.partial(pl.kernel, out_type=jax.ShapeDtypeStruct((r, LANES), table.dtype),
                       mesh=_sc_mesh(), scratch_types=[], name="sc_gather_rows")
    def k(t_hbm, i_hbm, o_hbm):
        def body(i_vmem, o_vmem):
            pltpu.sync_copy(t_hbm.at[i_vmem.at[0]], o_vmem)

        pltpu.emit_pipeline(
            body,
            grid=(r // SC_WINDOW,),
            in_specs=[pl.BlockSpec((1, SC_WINDOW), lambda i: (0, i))],
            out_specs=[pl.BlockSpec((SC_WINDOW, LANES), lambda i: (i, 0))],
            core_axis_name=("core", "subcore"),
            dimension_semantics=(pltpu.PARALLEL,),
        )(i_hbm, o_hbm)

    return k(table, idx.reshape(1, r))


def _moe_sorted_kernel(tg_ref, nused_ref, xs_ref, wg32_ref, wu32_ref, wd32_ref, g_ref, b_ref, o_ref,
                       wg_ref, wu_ref, wd_ref):
    t = pl.program_id(0)
    used = t < nused_ref[0]

    @pl.when(used & ((t == 0) | (tg_ref[t] != tg_ref[jnp.maximum(t - 1, 0)])))
    def _():
        wg_ref[...] = wg32_ref[...].astype(BF16)
        wu_ref[...] = wu32_ref[...].astype(BF16)
        wd_ref[...] = wd32_ref[...].astype(BF16)

    @pl.when(used)
    def _():
        x = jnp.concatenate([xs_ref[j] for j in range(X_PLANES)], axis=1)
        gate = xs_ref[X_PLANES]
        xb = x.astype(BF16)
        lane = lax.broadcasted_iota(jnp.int32, gate.shape, 1)
        first = N_GROUPS + EXP_PER_GROUP * tg_ref[t]
        acc = jnp.zeros(x.shape, F32)
        for e in range(EXP_PER_GROUP):
            gcol = jnp.sum(jnp.where(lane == first + e, gate, 0.0), -1, keepdims=True)
            hid = _silu(_bdot(xb, wg_ref[e])) * _bdot(xb, wu_ref[e])
            acc = acc + _bdot(hid * gcol, wd_ref[e])
        y = _layer_norm(ALPHA * x + acc, g_ref[...], b_ref[...])
        for j in range(X_PLANES):
            o_ref[j] = y[:, j * LANES:(j + 1) * LANES]


def _group_slots(group_idx, n, tm):
    n_tiles = n // tm + N_GROUPS
    onehot = (group_idx[:, None] == jnp.arange(N_GROUPS)[None, :]).astype(jnp.int32)
    csum = jnp.cumsum(onehot, axis=0)
    counts = csum[-1]
    rank = jnp.sum(onehot * csum, axis=1) - 1
    tiles_g = (counts + tm - 1) // tm
    tile_end = jnp.cumsum(tiles_g)
    slot_base = (tile_end - tiles_g) * tm
    slot = (jnp.sum(onehot * slot_base[None, :], axis=1) + rank).astype(jnp.int32)
    tile_group = jnp.sum((jnp.arange(n_tiles)[:, None] >= tile_end[None, :]).astype(jnp.int32), axis=1)
    tile_group = jnp.minimum(tile_group, N_GROUPS - 1).astype(jnp.int32)
    return slot, tile_group, tile_end[-1:].astype(jnp.int32)


def moe_routed_sc(xg, wg, wu, wd, g, b, layer, n, tm, during_scatter, during_gather, tiles_out):
    slot, tile_group, n_used = _group_slots(xg[X_PLANES, :, 0].astype(jnp.int32), n, tm)
    n_tiles = tile_group.shape[0]
    n_slots = n_tiles * tm
    plane_base = lambda planes: jnp.arange(planes, dtype=jnp.int32)[:, None] * n_slots
    idx_in = (plane_base(XG_PLANES) + slot[None, :]).reshape(-1)
    if tiles_out:
        idx_out = (plane_base(X_PLANES)[None] + slot.reshape(n // SUBLANES, 1, SUBLANES)).reshape(-1)
    else:
        idx_out = (plane_base(X_PLANES) + slot[None, :]).reshape(-1)
    xs = sc_scatter_rows(xg.reshape(XG_PLANES * n, LANES), idx_in, XG_PLANES * n_slots)
    xs = xs.reshape(XG_PLANES, n_slots, LANES)
    xs, side = lax.optimization_barrier((xs, during_scatter()))
    wspec = lambda shp: pl.BlockSpec((None, None, EXP_PER_GROUP) + shp,
                                     lambda t, tg, nu: (layer, tg[t], 0, 0, 0))
    grid_spec = pltpu.PrefetchScalarGridSpec(
        num_scalar_prefetch=2,
        grid=(n_tiles,),
        in_specs=[pl.BlockSpec((XG_PLANES, tm, LANES), lambda t, tg, nu: (0, t, 0)),
                  wspec((D_MODEL, EXP_FF)), wspec((D_MODEL, EXP_FF)), wspec((EXP_FF, D_MODEL)),
                  pl.BlockSpec((1, D_MODEL), lambda t, tg, nu: (0, 0)),
                  pl.BlockSpec((1, D_MODEL), lambda t, tg, nu: (0, 0))],
        out_specs=pl.BlockSpec((X_PLANES, tm, LANES), lambda t, tg, nu: (0, t, 0)),
        scratch_shapes=[pltpu.VMEM((EXP_PER_GROUP, D_MODEL, EXP_FF), BF16),
                        pltpu.VMEM((EXP_PER_GROUP, D_MODEL, EXP_FF), BF16),
                        pltpu.VMEM((EXP_PER_GROUP, EXP_FF, D_MODEL), BF16)])
    grouped = lambda w: w.reshape(w.shape[0], N_GROUPS, EXP_PER_GROUP, w.shape[2], w.shape[3])
    ys = pl.pallas_call(
        _moe_sorted_kernel,
        grid_spec=grid_spec,
        out_shape=jax.ShapeDtypeStruct((X_PLANES, n_slots, LANES), F32),
        compiler_params=_cparams(("arbitrary",)),
        name="moe_sorted",
    )(tile_group, n_used, xs, grouped(wg), grouped(wu), grouped(wd), g, b)
    y = sc_gather_rows(ys.reshape(X_PLANES * n_slots, LANES), idx_out)
    y, side = lax.optimization_barrier((y, during_gather(side)))
    if tiles_out:
        y = y.reshape(n // SUBLANES, X_PLANES, SUBLANES, LANES).transpose(0, 2, 1, 3)
        return y.reshape(n, D_MODEL), side
    return y.reshape(X_PLANES, n, LANES), side


def _t5_bucket(dist):
    max_exact = N_BUCKETS // 2
    d = jnp.maximum(dist, 0)
    df = jnp.maximum(d, 1).astype(F32)
    log_b = max_exact + (jnp.log(df / max_exact) / math.log(MAX_DISTANCE / max_exact)
                         * (N_BUCKETS - max_exact)).astype(jnp.int32)
    return jnp.where(d < max_exact, d, jnp.minimum(log_b, N_BUCKETS - 1))


def _bucket_lookup(rel_bias, bucket):
    out = jnp.zeros(bucket.shape + (rel_bias.shape[1],), F32)
    for i in range(N_BUCKETS):
        out = jnp.where((bucket == i)[..., None], rel_bias[i].astype(F32), out)
    return out


def _prompt_bias(rel_bias):
    qi = jnp.arange(WINDOW)[:, None]
    kj = jnp.arange(2 * WINDOW)[None, :]
    dist = qi + WINDOW - kj
    bias = _bucket_lookup(rel_bias, _t5_bucket(dist)).transpose(2, 0, 1)
    valid = (dist >= 0) & (dist <= WINDOW)
    return jnp.where(valid[None], bias, NEG)


def _decode_table(rel_bias, attn_sink):
    dist = WINDOW - jnp.arange(WINDOW + 1)
    bias = _bucket_lookup(rel_bias, _t5_bucket(dist)).T
    depth = attn_sink.shape[0]
    wide = lambda v: jnp.broadcast_to(v[..., None], v.shape + (WINDOW,))
    per_layer = lambda t: jnp.broadcast_to(t[None], (depth,) + t.shape)
    return jnp.stack([per_layer(bias[:, :WINDOW]), per_layer(wide(bias[:, WINDOW])),
                      wide(attn_sink.astype(F32))], axis=1)


def _block_ones(width):
    idx = jnp.arange(width) // HEAD_DIM
    return (idx[:, None] == idx[None, :])


def _head_rows_mask():
    head = jnp.arange(HG_WIDTH)[None, :] // HEAD_DIM
    return (head == jnp.arange(HG_MROWS)[:, None]).astype(F32)


def _block_diag(w):
    nblk, s, _ = w.shape
    eye = jnp.eye(nblk, dtype=w.dtype)
    return (eye[:, None, :, None] * w[:, :, None, :]).reshape(nblk * s, nblk * s)


def kernel(x_prompt, x_sample, mem_prompt, cache_win_k, cache_win_v, state_hgrn, state_conv, state_lru, cache_mem_k, cache_mem_v, rel_bias, hg_lb, w_in, attn_sink, hg_gain, conv_w, conv_b, lru_wa, lru_ba, lru_wx, lru_bx, lru_lam, w_out, ln1_g, ln1_b, x_wq, x_wk, x_wv, x_wo, ln2_g, ln2_b, r_gw, r_gb, r_ew, r_eb, e_wg, e_wu, e_wd, ln3_g, ln3_b):
    bp, seq, d = x_prompt.shape
    n_tok = bp * seq
    nd = x_sample.shape[0]
    depth = w_in.shape[0]

    lbs = jnp.cumsum(jax.nn.softmax(hg_lb.astype(F32), axis=0), axis=0)
    lbs = lbs - lbs[0]
    loglb = jnp.log(lbs)
    log1mlb = jnp.log1p(-lbs)
    gain4 = jnp.tile(hg_gain, (1, HG_HEADS))

    bias_p = _prompt_bias(rel_bias)
    bdm256 = _block_ones(HG_WIDTH).astype(F32)
    hmask = _head_rows_mask()

    w_in_b = w_in.astype(BF16)
    w_out_b = w_out.astype(BF16)
    wq_b = x_wq.astype(BF16)
    wkv_b = jnp.concatenate([x_wk, x_wv], axis=-1).astype(BF16)
    wo_b = x_wo.astype(BF16)
    rew = r_ew.transpose(0, 2, 1, 3).reshape(depth, d, N_EXPERTS)
    wr = jnp.concatenate([r_gw, rew, jnp.zeros((depth, d, ROUTER_LANES - N_GROUPS - N_EXPERTS), F32)], -1)
    br = jnp.concatenate([r_gb, r_eb.reshape(depth, N_EXPERTS),
                          jnp.zeros((depth, ROUTER_LANES - N_GROUPS - N_EXPERTS), F32)], -1)

    a_w = A_HEADS * HEAD_DIM
    xp = x_prompt.reshape(bp * seq, d)
    xs = x_sample.reshape(nd, d)
    mem = mem_prompt.reshape(bp * N_MEM, d)
    ckt = cache_win_k.transpose(0, 1, 3, 4, 2).reshape(depth, nd, LANES, WINDOW)
    cvt = cache_win_v.transpose(0, 1, 3, 4, 2).reshape(depth, nd, LANES, WINDOW)
    cmkt = cache_mem_k.transpose(0, 1, 3, 4, 2).reshape(depth, nd, X_WIDTH, N_MEM)
    cmvt = cache_mem_v.transpose(0, 1, 3, 4, 2).reshape(depth, nd, X_WIDTH, N_MEM)
    state_t = state_hgrn.transpose(0, 2, 3, 4, 1).reshape(depth, HG_HEADS * HEAD_DIM * HEAD_DIM, nd)
    dec_tab = _decode_table(rel_bias, attn_sink)
    xq_mask = _head_rows_mask()[:X_QROWS]
    head_group = jnp.arange(A_HEADS) // A_REP

    p_wk, p_wv, p_s, p_cb, p_h, p_mk, p_mv = [], [], [], [], [], [], []
    s_cb, s_h = [], []
    for l in range(depth):
        row = lambda v: v[l].reshape(1, -1)
        wa_bd = _block_diag(lru_wa[l]).astype(BF16)
        wx_bd = _block_diag(lru_wx[l]).astype(BF16)
        lru_args = (conv_w[l], row(conv_b), wa_bd, row(lru_ba), wx_bd, row(lru_bx), row(lru_lam))
        wo_parts = [w_out_b[l, :a_w], w_out_b[l, a_w:a_w + HG_WIDTH], w_out_b[l, a_w + HG_WIDTH:]]

        proj = matmul(xp, w_in_b[l], 1024, IN_COLS)
        oa = attn_prompt(proj, attn_sink[l], bias_p, bp, seq)
        ob, st = hgrn_prompt(proj, row(loglb), row(log1mlb), row(gain4), bdm256, hmask, bp, seq, 1024)
        oc, hl = lru_prompt(proj, *lru_args, bp, seq, 512)
        xp = proj_res_ln(xp, [oa, ob, oc], wo_parts, row(ln1_g), row(ln1_b), 1024)
        mkv = matmul(mem, wkv_b[l], 256, 512)
        xg = xattn_prompt(xp, wq_b[l], mkv, wo_b[l], row(ln2_g), row(ln2_b), wr[l], br[l:l + 1],
                          bp, seq, 1024)
        def decode_mixers(xs=xs, ckt=ckt, cvt=cvt, state_t=state_t, l=l, lru_args=lru_args,
                          wo_parts=wo_parts, row=row):
            projd = matmul(xs, w_in_b[l], nd, 768)
            q3 = projd[:, :a_w].reshape(nd, A_HEADS, 1, HEAD_DIM)
            on_group = head_group[None, :, None, None] == jnp.arange(A_KV_HEADS)[None, None, :, None]
            qblk = jnp.where(on_group, q3, 0.0).reshape(nd, A_HEADS, LANES)
            o3, ckt, cvt = attn_decode(qblk, projd, dec_tab[l], ckt, cvt, l, 16)
            o4 = o3.reshape(nd, A_HEADS, A_KV_HEADS, HEAD_DIM)
            oa = jnp.sum(jnp.where(on_group, o4, 0.0), axis=2).reshape(nd, a_w)
            gates_t = projd[:, a_w + 2 * LANES:a_w + 2 * LANES + 4 * HG_WIDTH].T
            bc = lambda v: jnp.broadcast_to(v[:, None], (v.shape[0], nd))
            ob_t, state_t = hgrn_decode(gates_t, bc(loglb[l]), bc(log1mlb[l]), bc(hg_gain[l]), state_t, l)
            oc, nh, nbuf = lru_decode(projd, state_conv[l].reshape(nd, -1), state_lru[l], *lru_args)
            xs = proj_res_ln(xs, [oa, ob_t.T, oc], wo_parts, row(ln1_g), row(ln1_b), nd)
            return xs, ckt, cvt, state_t, nh, nbuf

        def decode_rest(side, l=l, row=row):
            xs = side[0]
            qd = matmul(xs, wq_b[l], nd, X_WIDTH)
            qdb = qd[:, None, :] * xq_mask[None, :, :]
            od = xattn_decode(qdb, cmkt, cmvt, xq_mask, l, 16)
            xs = proj_res_ln(xs, [od], [wo_b[l]], row(ln2_g), row(ln2_b), nd)
            xs = moe_dense(xs, wr[l], br[l:l + 1], e_wg, e_wu, e_wd, row(ln3_g), row(ln3_b), l, nd)
            return (xs,) + tuple(side[1:])

        xp, (xs, ckt, cvt, state_t, nh, nbuf) = moe_routed_sc(
            xg, e_wg, e_wu, e_wd, row(ln3_g), row(ln3_b), l, n_tok, MOE_TM, decode_mixers, decode_rest,
            tiles_out=(l == depth - 1))

        proj3 = proj.reshape(bp, seq, IN_COLS)
        p_wk.append(proj3[:, seq - WINDOW:, a_w:a_w + LANES].reshape(bp, WINDOW, A_KV_HEADS, HEAD_DIM))
        p_wv.append(proj3[:, seq - WINDOW:, a_w + LANES:a_w + 2 * LANES].reshape(bp, WINDOW, A_KV_HEADS, HEAD_DIM))
        st5 = st.reshape(bp, HG_HEADS, HEAD_DIM, HG_HEADS, HEAD_DIM)
        p_s.append(jnp.stack([st5[:, h, :, h, :] for h in range(HG_HEADS)], 1).transpose(0, 1, 3, 2))
        p_cb.append(proj3[:, seq - (CONV_W - 1):, IN_COLS - 2 * LRU_WIDTH:IN_COLS - LRU_WIDTH])
        p_h.append(hl.reshape(bp, LRU_WIDTH))
        p_mk.append(mkv[:, :X_WIDTH].reshape(bp, N_MEM, X_HEADS, HEAD_DIM))
        p_mv.append(mkv[:, X_WIDTH:].reshape(bp, N_MEM, X_HEADS, HEAD_DIM))

        s_cb.append(nbuf.reshape(nd, CONV_W - 1, LRU_WIDTH))
        s_h.append(nh)

    unkey = lambda c: c.reshape(depth, nd, A_KV_HEADS, HEAD_DIM, WINDOW).transpose(0, 1, 4, 2, 3)
    s_s = state_t.reshape(depth, HG_HEADS, HEAD_DIM, HEAD_DIM, nd).transpose(0, 4, 1, 2, 3)
    return (xp.reshape(bp, seq, d), xs.reshape(nd, 1, d),
            jnp.stack(p_wk), jnp.stack(p_wv), jnp.stack(p_s), jnp.stack(p_cb), jnp.stack(p_h),
            jnp.stack(p_mk), jnp.stack(p_mv),
            unkey(ckt), unkey(cvt), s_s, jnp.stack(s_cb), jnp.stack(s_h))
```

```python
import functools
import math

import jax
import jax.numpy as jnp
from jax import lax
from jax.experimental import pallas as pl
from jax.experimental.pallas import tpu as pltpu
from jax.experimental.pallas import tpu_sc as plsc

F32 = jnp.float32
BF16 = jnp.bfloat16
MIX_DTYPE = BF16

D_MODEL = 1024
DEPTH = 4
HEAD_DIM = 64
A_HEADS = 8
A_KV_HEADS = 2
A_REP = A_HEADS // A_KV_HEADS
WINDOW = 128
A_QB = 2
N_BUCKETS = 32
MAX_DISTANCE = 128
HG_WIDTH = 256
HG_HEADS = 4
HG_CHUNK = 64
HG_TB = 16
HG_MROWS = 8
LOG2E = math.log2(math.e)
LRU_WIDTH = 256
LRU_BLOCKS = 4
CONV_W = 4
LRU_C = 8.0
N_MEM = 256
X_HEADS = 4
X_WIDTH = X_HEADS * HEAD_DIM
X_QROWS = 8
N_GROUPS = 4
EXP_PER_GROUP = 4
N_EXPERTS = N_GROUPS * EXP_PER_GROUP
EXP_FF = D_MODEL // 4
ALPHA = (2 * DEPTH) ** 0.25
LN_EPS = 1e-5
RMS_EPS = 1e-6
IN_COLS = 2304
SCALE = HEAD_DIM ** -0.5
NEG = -1e30
LANES = 128
SUBLANES = 8
ROUTER_LANES = 128
XG_WIDTH = D_MODEL + ROUTER_LANES
XG_PLANES = XG_WIDTH // LANES
X_PLANES = D_MODEL // LANES
MOE_TM = 512
ROW_TILE = 1024
LRU_TILE = 512
DEC_BLOCK = 16
SC_WINDOW = 128
VMEM_LIMIT = 48 * 1024 * 1024


def _cparams(sem):
    return pltpu.CompilerParams(dimension_semantics=sem, vmem_limit_bytes=VMEM_LIMIT)


def _bdot(a, b):
    return jnp.dot(a.astype(BF16), b.astype(BF16), preferred_element_type=F32)


def _bdot_nt(a, b):
    return lax.dot_general(a.astype(BF16), b.astype(BF16), (((1,), (1,)), ((), ())),
                           preferred_element_type=F32)


def _bdot_tn(a, b):
    return lax.dot_general(a.astype(BF16), b.astype(BF16), (((0,), (0,)), ((), ())),
                           preferred_element_type=F32)


def _rb(x):
    return x.astype(BF16).astype(F32)


def _silu(x):
    return x * jax.nn.sigmoid(x)


def _neg_expm1(x):
    return -jnp.tanh(0.5 * x) * (jnp.exp(x) + 1.0)


def _softplus(x):
    return jnp.maximum(x, 0.0) + jnp.log1p(jnp.exp(-jnp.abs(x)))


def _gelu_tanh(x):
    return 0.5 * x * (1.0 + jnp.tanh(math.sqrt(2.0 / math.pi) * (x + 0.044715 * (x * x * x))))


def _layer_norm(y, g, b):
    mu = jnp.mean(y, -1, keepdims=True)
    yc = y - mu
    var = jnp.mean(yc * yc, -1, keepdims=True)
    return yc * lax.rsqrt(var + LN_EPS) * g + b


def _rows(x_ref):
    if len(x_ref.shape) == 2:
        return x_ref[...]
    return jnp.concatenate([x_ref[j] for j in range(x_ref.shape[0])], axis=1)


def _rows_spec(x, tm, nargs):
    if x.ndim == 2:
        return pl.BlockSpec((tm, x.shape[1]), (lambda i: (i, 0)) if nargs == 1 else (lambda i, j: (i, 0)))
    blk = (x.shape[0], tm, LANES)
    return pl.BlockSpec(blk, (lambda i: (0, i, 0)) if nargs == 1 else (lambda i, j: (0, i, 0)))


def _mm_kernel(x_ref, w_ref, o_ref):
    o_ref[...] = _bdot(_rows(x_ref), w_ref[...])


def matmul(x, w, tm, tn):
    m = x.shape[-2]
    k, n = w.shape
    return pl.pallas_call(
        _mm_kernel,
        grid=(m // tm, n // tn),
        in_specs=[_rows_spec(x, tm, 2),
                  pl.BlockSpec((k, tn), lambda i, j: (0, j))],
        out_specs=pl.BlockSpec((tm, tn), lambda i, j: (i, j)),
        out_shape=jax.ShapeDtypeStruct((m, n), F32),
        compiler_params=_cparams(("parallel", "parallel")),
        name="matmul",
    )(x, w)


def _proj_res_ln_kernel(n_in, x_ref, *refs):
    a_refs = refs[:n_in]
    w_refs = refs[n_in:2 * n_in]
    g_ref, b_ref, o_ref = refs[2 * n_in:]
    y = ALPHA * _rows(x_ref)
    for a_ref, w_ref in zip(a_refs, w_refs):
        y = y + _bdot(a_ref[...], w_ref[...])
    o_ref[...] = _layer_norm(y, g_ref[...], b_ref[...])


def proj_res_ln(x, a_list, w_list, g, b, tm):
    m = x.shape[-2]
    d = w_list[0].shape[1]
    n_in = len(a_list)
    in_specs = [_rows_spec(x, tm, 1)]
    in_specs += [pl.BlockSpec((tm, a.shape[1]), lambda i: (i, 0)) for a in a_list]
    in_specs += [pl.BlockSpec(w.shape, lambda i: (0, 0)) for w in w_list]
    in_specs += [pl.BlockSpec((1, d), lambda i: (0, 0))] * 2
    return pl.pallas_call(
        functools.partial(_proj_res_ln_kernel, n_in),
        grid=(m // tm,),
        in_specs=in_specs,
        out_specs=pl.BlockSpec((tm, d), lambda i: (i, 0)),
        out_shape=jax.ShapeDtypeStruct((m, d), F32),
        compiler_params=_cparams(("parallel",)),
        name="proj_res_ln",
    )(x, *a_list, *w_list, g, b)


def _attn_prompt_kernel(sink_ref, q_ref, kc_ref, kp_ref, vc_ref, vp_ref, bias_ref, o_ref):
    n = pl.program_id(1)
    col = lax.broadcasted_iota(jnp.int32, (WINDOW, 2 * WINDOW), 1)
    first = jnp.where((n == 0) & (col < WINDOW), NEG, 0.0)
    kk = jnp.concatenate([kp_ref[...], kc_ref[...]], axis=0).astype(BF16)
    vv = jnp.concatenate([vp_ref[...], vc_ref[...]], axis=0).astype(BF16)
    q = q_ref[...].astype(BF16)
    for u in range(A_QB):
        rows = slice(u * WINDOW, (u + 1) * WINDOW)
        keys = slice(u * WINDOW, (u + 2) * WINDOW)
        outs = []
        for h in range(A_HEADS):
            g = h // A_REP
            qh = q[rows, h * HEAD_DIM:(h + 1) * HEAD_DIM]
            kg = kk[keys, g * HEAD_DIM:(g + 1) * HEAD_DIM]
            vg = vv[keys, g * HEAD_DIM:(g + 1) * HEAD_DIM]
            s = _bdot_nt(qh, kg) * SCALE + bias_ref[h]
            if u == 0:
                s = s + first
            sink = sink_ref[h]
            m = jnp.maximum(jnp.max(s, -1, keepdims=True), sink)
            p = jnp.exp(s - m)
            den = jnp.sum(p, -1, keepdims=True) + jnp.exp(sink - m)
            outs.append(_bdot(p / den, vg))
        o_ref[rows, :] = jnp.concatenate(outs, axis=1).astype(o_ref.dtype)


def attn_prompt(proj, sink, bias, batch, seq):
    nb = seq // WINDOW
    ns = nb // A_QB
    tq = A_QB * WINDOW
    qcol = 0
    kcol = (A_HEADS * HEAD_DIM) // LANES
    vcol = kcol + 1

    def cur(c):
        return lambda b, n: (b * ns + n, c)

    def prev(c):
        return lambda b, n: (b * nb + jnp.maximum(n * A_QB - 1, 0), c)

    return pl.pallas_call(
        _attn_prompt_kernel,
        grid=(batch, ns),
        in_specs=[pl.BlockSpec(memory_space=pltpu.SMEM),
                  pl.BlockSpec((tq, A_HEADS * HEAD_DIM), cur(qcol)),
                  pl.BlockSpec((tq, LANES), cur(kcol)),
                  pl.BlockSpec((WINDOW, LANES), prev(kcol)),
                  pl.BlockSpec((tq, LANES), cur(vcol)),
                  pl.BlockSpec((WINDOW, LANES), prev(vcol)),
                  pl.BlockSpec((A_HEADS, WINDOW, 2 * WINDOW), lambda b, n: (0, 0, 0))],
        out_specs=pl.BlockSpec((tq, A_HEADS * HEAD_DIM), cur(0)),
        out_shape=jax.ShapeDtypeStruct((batch * seq, A_HEADS * HEAD_DIM), MIX_DTYPE),
        compiler_params=_cparams(("parallel", "parallel")),
        name="attn_prompt",
    )(sink, proj, proj, proj, proj, proj, bias)


def _attn_decode_kernel(qb_ref, kn_ref, vn_ref, knt_ref, vnt_ref, ck_ref, cv_ref, tab_ref,
                        o_ref, ok_ref, ov_ref):
    bb = qb_ref.shape[0]
    ck = ck_ref[...]
    cv = cv_ref[...]
    qb = qb_ref[...]
    kn = kn_ref[...]
    vn = vn_ref[...]
    bias_j = tab_ref[0]
    bias_n = tab_ref[1][:, 0:1]
    sink = tab_ref[2][:, 0:1]
    s = lax.dot_general(qb.astype(BF16), ck.astype(BF16), (((2,), (1,)), ((0,), (0,))),
                        preferred_element_type=F32) * SCALE + bias_j[None]
    sn = jnp.sum(_rb(qb) * _rb(kn)[:, None, :], -1, keepdims=True) * SCALE + bias_n[None]
    m = jnp.maximum(jnp.maximum(jnp.max(s, -1, keepdims=True), sn), sink[None])
    p = jnp.exp(s - m)
    pn = jnp.exp(sn - m)
    den = jnp.sum(p, -1, keepdims=True) + pn + jnp.exp(sink[None] - m)
    o = lax.dot_general((p / den).astype(BF16), cv.astype(BF16), (((2,), (2,)), ((0,), (0,))),
                        preferred_element_type=F32)
    o_ref[...] = o + _rb(pn / den) * _rb(vn)[:, None, :]
    lane = lax.broadcasted_iota(jnp.int32, (LANES, LANES), 1)
    for b in range(bb):
        ok_ref[b] = jnp.where(lane == WINDOW - 1, knt_ref[:, b:b + 1], pltpu.roll(ck[b], WINDOW - 1, 1))
        ov_ref[b] = jnp.where(lane == WINDOW - 1, vnt_ref[:, b:b + 1], pltpu.roll(cv[b], WINDOW - 1, 1))


def attn_decode(qblk, proj_d, table, cache_k, cache_v, layer, bb):
    nbatch = proj_d.shape[0]
    a_w = A_HEADS * HEAD_DIM
    cols = lambda c: proj_d[:, c:c + LANES].reshape(nbatch // bb, bb, LANES).transpose(0, 2, 1)
    knt, vnt = cols(a_w), cols(a_w + LANES)
    kcol = (A_HEADS * HEAD_DIM) // LANES
    cache_spec = pl.BlockSpec((None, bb, LANES, WINDOW), lambda i: (layer, i, 0, 0))
    col_spec = pl.BlockSpec((None, LANES, bb), lambda i: (i, 0, 0))
    return pl.pallas_call(
        _attn_decode_kernel,
        grid=(nbatch // bb,),
        in_specs=[pl.BlockSpec((bb, A_HEADS, LANES), lambda i: (i, 0, 0)),
                  pl.BlockSpec((bb, LANES), lambda i: (i, kcol)),
                  pl.BlockSpec((bb, LANES), lambda i: (i, kcol + 1)),
                  col_spec, col_spec, cache_spec, cache_spec,
                  pl.BlockSpec((3, A_HEADS, WINDOW), lambda i: (0, 0, 0))],
        out_specs=[pl.BlockSpec((bb, A_HEADS, LANES), lambda i: (i, 0, 0)), cache_spec, cache_spec],
        out_shape=[jax.ShapeDtypeStruct((nbatch, A_HEADS, LANES), F32),
                   jax.ShapeDtypeStruct(cache_k.shape, F32),
                   jax.ShapeDtypeStruct(cache_v.shape, F32)],
        input_output_aliases={5: 1, 6: 2},
        compiler_params=_cparams(("arbitrary",)),
        name="attn_decode",
    )(qblk, proj_d, proj_d, knt, vnt, cache_k, cache_v, table)


def _xattn_decode_kernel(qb_ref, mk_ref, mv_ref, hm_ref, o_ref):
    qb = qb_ref[...]
    s = lax.dot_general(qb.astype(BF16), mk_ref[...].astype(BF16), (((2,), (1,)), ((0,), (0,))),
                        preferred_element_type=F32) * SCALE
    m = jnp.max(s, -1, keepdims=True)
    p = jnp.exp(s - m)
    p = p / jnp.sum(p, -1, keepdims=True)
    o = lax.dot_general(p.astype(BF16), mv_ref[...].astype(BF16), (((2,), (2,)), ((0,), (0,))),
                        preferred_element_type=F32)
    o_ref[...] = jnp.sum(o * hm_ref[...][None], axis=1)


def xattn_decode(qblk, mem_k, mem_v, hmask, layer, bb):
    nbatch = qblk.shape[0]
    mem_spec = pl.BlockSpec((None, bb, X_WIDTH, N_MEM), lambda i: (layer, i, 0, 0))
    return pl.pallas_call(
        _xattn_decode_kernel,
        grid=(nbatch // bb,),
        in_specs=[pl.BlockSpec((bb, X_QROWS, X_WIDTH), lambda i: (i, 0, 0)), mem_spec, mem_spec,
                  pl.BlockSpec((X_QROWS, X_WIDTH), lambda i: (0, 0))],
        out_specs=pl.BlockSpec((bb, X_WIDTH), lambda i: (i, 0)),
        out_shape=jax.ShapeDtypeStruct((nbatch, X_WIDTH), F32),
        compiler_params=_cparams(("parallel",)),
        name="xattn_decode",
    )(qblk, mem_k, mem_v, hmask)


def _hgrn_gates(hq, hf, loglb, log1mlb):
    ls = jnp.minimum(hf, 0.0) - jnp.log1p(jnp.exp(-jnp.abs(hf)))
    b = log1mlb + ls
    lf = jnp.maximum(loglb, b) + jnp.log1p(jnp.exp(-jnp.abs(loglb - b)))
    return _silu(hq), lf, _neg_expm1(lf)


def _hgrn_prompt_kernel(hq_ref, hf_ref, hi_ref, hg_ref, loglb_ref, log1mlb_ref, gain_ref,
                        bdm_ref, hm_ref, ob_ref, st_ref, st_scr, q_scr, k_scr, cum_scr, o_scr):
    i = pl.program_id(1)
    tt = hq_ref.shape[0]
    c = HG_CHUNK
    tb = HG_TB

    @pl.when(i == 0)
    def _():
        st_scr[...] = jnp.zeros_like(st_scr)

    qs, lf, kk = _hgrn_gates(hq_ref[...], hf_ref[...], loglb_ref[...], log1mlb_ref[...])
    row = lax.broadcasted_iota(jnp.int32, (tt, HG_WIDTH), 0) & (c - 1)
    cum = lf
    sh = 1
    while sh < c:
        cum = cum + jnp.where(row >= sh, pltpu.roll(cum, sh, 0), 0.0)
        sh *= 2
    q_scr[...] = qs
    k_scr[...] = kk
    cum_scr[...] = cum

    bdm = bdm_ref[...]
    hmask = hm_ref[...]
    def chunk(ci, carry):
        r0 = pl.multiple_of(ci * c, c)
        r = pl.ds(r0, c)
        cu = cum_scr[r, :]
        q = q_scr[r, :]
        k = k_scr[r, :]
        v = hi_ref[r, :]
        vb = v.astype(BF16)
        qb = _rb(q)
        cu2 = cu * LOG2E
        last = cu[c - 1:c, :]
        st = st_scr[...]
        o_inter = _bdot_nt(q * jnp.exp(cu), st)
        for j in range(c // tb):
            ns = tb * (j + 1)
            ti = lax.broadcasted_iota(jnp.int32, (tb, ns, HG_WIDTH), 0) + tb * j
            si = lax.broadcasted_iota(jnp.int32, (tb, ns, HG_WIDTH), 1)
            cut = cu2[tb * j:tb * (j + 1), :]
            dec = jnp.exp2(jnp.where(ti >= si, cut[:, None, :] - cu2[None, :ns, :], NEG))
            a2 = (dec * k[None, :ns, :]).astype(BF16)
            q4 = qb[tb * j:tb * (j + 1), None, :] * hmask[None, :, :]
            att = lax.dot_general(q4, a2, (((2,), (2,)), ((0,), (0,))),
                                  preferred_element_type=F32)
            w = jnp.dot(_rb(att.reshape(tb * HG_MROWS, ns)), vb[:ns, :],
                        preferred_element_type=F32).reshape(tb, HG_MROWS, HG_WIDTH)
            o_intra = jnp.sum(w * hmask[None, :, :], axis=1)
            o_scr[pl.ds(r0 + tb * j, tb), :] = o_intra + o_inter[tb * j:tb * (j + 1), :]
        upd = _bdot_tn(v, k * jnp.exp(last - cu))
        st_scr[...] = st * jnp.exp(last) + upd * bdm
        return carry

    lax.fori_loop(0, tt // c, chunk, 0, unroll=2)

    o = o_scr[...]
    ms = jnp.dot(o * o, bdm, precision=lax.Precision.HIGHEST,
                 preferred_element_type=F32) * (1.0 / HEAD_DIM)
    ob_ref[...] = (o * lax.rsqrt(ms + RMS_EPS) * gain_ref[...] * _silu(hg_ref[...])).astype(ob_ref.dtype)

    @pl.when(i == pl.num_programs(1) - 1)
    def _():
        st_ref[...] = st_scr[...]


def hgrn_prompt(proj, loglb, log1mlb, gain4, bdm, hmask, batch, seq, tt):
    nt = seq // tt
    base = (A_HEADS + 2 * A_KV_HEADS) * HEAD_DIM // HG_WIDTH

    def col(cblk):
        return pl.BlockSpec((tt, HG_WIDTH), lambda b, i: (b * nt + i, cblk))

    row_spec = pl.BlockSpec((1, HG_WIDTH), lambda b, i: (0, 0))
    mat_spec = pl.BlockSpec((HG_WIDTH, HG_WIDTH), lambda b, i: (0, 0))
    return pl.pallas_call(
        _hgrn_prompt_kernel,
        grid=(batch, nt),
        in_specs=[col(base), col(base + 1), col(base + 2), col(base + 3),
                  row_spec, row_spec, row_spec, mat_spec,
                  pl.BlockSpec((HG_MROWS, HG_WIDTH), lambda b, i: (0, 0))],
        out_specs=[pl.BlockSpec((tt, HG_WIDTH), lambda b, i: (b * nt + i, 0)),
                   pl.BlockSpec((None, HG_WIDTH, HG_WIDTH), lambda b, i: (b, 0, 0))],
        out_shape=[jax.ShapeDtypeStruct((batch * seq, HG_WIDTH), MIX_DTYPE),
                   jax.ShapeDtypeStruct((batch, HG_WIDTH, HG_WIDTH), F32)],
        scratch_shapes=[pltpu.VMEM((HG_WIDTH, HG_WIDTH), F32),
                        pltpu.VMEM((tt, HG_WIDTH), F32),
                        pltpu.VMEM((tt, HG_WIDTH), F32),
                        pltpu.VMEM((tt, HG_WIDTH), F32),
                        pltpu.VMEM((tt, HG_WIDTH), F32)],
        compiler_params=_cparams(("parallel", "arbitrary")),
        name="hgrn_prompt",
    )(proj, proj, proj, proj, loglb, log1mlb, gain4, bdm, hmask)


def _hgrn_decode_kernel(hq_ref, hf_ref, hi_ref, hg_ref, loglb_ref, log1mlb_ref, gain_ref, s_ref,
                        ob_ref, so_ref):
    nb = hq_ref.shape[1]
    qs, lf, kk = _hgrn_gates(hq_ref[...], hf_ref[...], loglb_ref[...], log1mlb_ref[...])
    v = hi_ref[...]
    f = jnp.exp(lf)
    s = s_ref[...].reshape(HEAD_DIM, HEAD_DIM, nb)
    att = jnp.sum(_rb(qs) * _rb(kk), axis=0, keepdims=True)
    o = _rb(att) * _rb(v) + jnp.sum(_rb(qs * f)[:, None, :] * _rb(s), axis=0)
    s_new = f[:, None, :] * s + _rb(kk)[:, None, :] * _rb(v)[None, :, :]
    so_ref[...] = s_new.reshape(HEAD_DIM * HEAD_DIM, nb)
    ms = jnp.mean(o * o, axis=0, keepdims=True)
    ob_ref[...] = (o * lax.rsqrt(ms + RMS_EPS) * gain_ref[...] * _silu(hg_ref[...])).astype(ob_ref.dtype)


def hgrn_decode(gates_t, loglb_t, log1mlb_t, gain_t, state_t, layer):
    nb = gates_t.shape[1]

    def blk(off):
        return pl.BlockSpec((HEAD_DIM, nb), lambda h: (off * HG_HEADS + h, 0))

    par = pl.BlockSpec((HEAD_DIM, nb), lambda h: (h, 0))
    st = pl.BlockSpec((None, HEAD_DIM * HEAD_DIM, nb), lambda h: (layer, h, 0))
    return pl.pallas_call(
        _hgrn_decode_kernel,
        grid=(HG_HEADS,),
        in_specs=[blk(0), blk(1), blk(2), blk(3), par, par,
                  pl.BlockSpec((HEAD_DIM, nb), lambda h: (0, 0)), st],
        out_specs=[par, st],
        out_shape=[jax.ShapeDtypeStruct((HG_WIDTH, nb), F32),
                   jax.ShapeDtypeStruct(state_t.shape, F32)],
        input_output_aliases={7: 1},
        compiler_params=_cparams(("arbitrary",)),
        name="hgrn_decode",
    )(gates_t, gates_t, gates_t, gates_t, loglb_t, log1mlb_t, gain_t, state_t)


def _lru_gates(xc, wa_ref, ba_ref, wx_ref, bx_ref, lam_ref):
    r = jax.nn.sigmoid(_bdot(xc, wa_ref[...]) + ba_ref[...])
    gi = jax.nn.sigmoid(_bdot(xc, wx_ref[...]) + bx_ref[...])
    log_a = -LRU_C * r * _softplus(-lam_ref[...])
    a = jnp.exp(log_a)
    bterm = jnp.sqrt(_neg_expm1(2.0 * log_a)) * (gi * xc)
    return a, bterm


def _lru_prompt_kernel(lx_ref, lg_ref, cw_ref, cb_ref, wa_ref, ba_ref, wx_ref, bx_ref, lam_ref,
                       oc_ref, hl_ref, ext_scr, h_scr):
    i = pl.program_id(1)
    tt = lx_ref.shape[0]
    pad = SUBLANES

    @pl.when(i == 0)
    def _():
        ext_scr[0:pad, :] = jnp.zeros((pad, LRU_WIDTH), F32)
        h_scr[...] = jnp.zeros_like(h_scr)

    x = lx_ref[...]
    ext_scr[pad:pad + tt, :] = x
    xc = cb_ref[...] + cw_ref[CONV_W - 1:CONV_W, :] * x
    for j in range(CONV_W - 1):
        back = CONV_W - 1 - j
        xc = xc + cw_ref[j:j + 1, :] * ext_scr[pad - back:pad - back + tt, :]
    ext_scr[0:pad, :] = x[tt - pad:tt, :]

    a, bterm = _lru_gates(xc, wa_ref, ba_ref, wx_ref, bx_ref, lam_ref)
    row = lax.broadcasted_iota(jnp.int32, (tt, LRU_WIDTH), 0)
    sh = 1
    while sh < tt:
        keep = row >= sh
        b_s = jnp.where(keep, pltpu.roll(bterm, sh, 0), 0.0)
        a_s = jnp.where(keep, pltpu.roll(a, sh, 0), 1.0)
        bterm = a * b_s + bterm
        a = a * a_s
        sh *= 2
    h = a * h_scr[...] + bterm
    h_scr[...] = h[tt - 1:tt, :]
    oc_ref[...] = (h * _gelu_tanh(lg_ref[...])).astype(oc_ref.dtype)

    @pl.when(i == pl.num_programs(1) - 1)
    def _():
        hl_ref[...] = h[tt - 1:tt, :]


def lru_prompt(proj, conv_w, conv_b, wa_bd, ba, wx_bd, bx, lam, batch, seq, tt):
    nt = seq // tt
    base = IN_COLS // LRU_WIDTH - 2

    def col(cblk):
        return pl.BlockSpec((tt, LRU_WIDTH), lambda b, i: (b * nt + i, cblk))

    row_spec = pl.BlockSpec((1, LRU_WIDTH), lambda b, i: (0, 0))
    mat_spec = pl.BlockSpec((LRU_WIDTH, LRU_WIDTH), lambda b, i: (0, 0))
    return pl.pallas_call(
        _lru_prompt_kernel,
        grid=(batch, nt),
        in_specs=[col(base), col(base + 1),
                  pl.BlockSpec((CONV_W, LRU_WIDTH), lambda b, i: (0, 0)), row_spec,
                  mat_spec, row_spec, mat_spec, row_spec, row_spec],
        out_specs=[pl.BlockSpec((tt, LRU_WIDTH), lambda b, i: (b * nt + i, 0)),
                   pl.BlockSpec((None, 1, LRU_WIDTH), lambda b, i: (b, 0, 0))],
        out_shape=[jax.ShapeDtypeStruct((batch * seq, LRU_WIDTH), MIX_DTYPE),
                   jax.ShapeDtypeStruct((batch, 1, LRU_WIDTH), F32)],
        scratch_shapes=[pltpu.VMEM((tt + 8, LRU_WIDTH), F32),
                        pltpu.VMEM((1, LRU_WIDTH), F32)],
        compiler_params=_cparams(("parallel", "arbitrary")),
        name="lru_prompt",
    )(proj, proj, conv_w, conv_b, wa_bd, ba, wx_bd, bx, lam)


def _lru_decode_kernel(lx_ref, lg_ref, buf_ref, h0_ref, cw_ref, cb_ref, wa_ref, ba_ref, wx_ref,
                       bx_ref, lam_ref, oc_ref, hn_ref, nbuf_ref):
    x = lx_ref[...]
    buf = buf_ref[...]
    xc = cb_ref[...] + cw_ref[CONV_W - 1:CONV_W, :] * x
    for j in range(CONV_W - 1):
        xc = xc + cw_ref[j:j + 1, :] * buf[:, j * LRU_WIDTH:(j + 1) * LRU_WIDTH]
    a, bterm = _lru_gates(xc, wa_ref, ba_ref, wx_ref, bx_ref, lam_ref)
    h = a * h0_ref[...] + bterm
    hn_ref[...] = h
    oc_ref[...] = (h * _gelu_tanh(lg_ref[...])).astype(oc_ref.dtype)
    nbuf_ref[...] = jnp.concatenate([buf[:, LRU_WIDTH:], x], axis=1)


def lru_decode(proj_d, conv_buf, h0, conv_w, conv_b, wa_bd, ba, wx_bd, bx, lam):
    nb = proj_d.shape[0]
    base = IN_COLS // LRU_WIDTH - 2
    row_spec = pl.BlockSpec((1, LRU_WIDTH), lambda i: (0, 0))
    mat_spec = pl.BlockSpec((LRU_WIDTH, LRU_WIDTH), lambda i: (0, 0))
    act = pl.BlockSpec((nb, LRU_WIDTH), lambda i: (0, 0))
    bufs = pl.BlockSpec((nb, (CONV_W - 1) * LRU_WIDTH), lambda i: (0, 0))
    return pl.pallas_call(
        _lru_decode_kernel,
        grid=(1,),
        in_specs=[pl.BlockSpec((nb, LRU_WIDTH), lambda i: (0, base)),
                  pl.BlockSpec((nb, LRU_WIDTH), lambda i: (0, base + 1)),
                  bufs, act, pl.BlockSpec((CONV_W, LRU_WIDTH), lambda i: (0, 0)), row_spec,
                  mat_spec, row_spec, mat_spec, row_spec, row_spec],
        out_specs=[act, act, bufs],
        out_shape=[jax.ShapeDtypeStruct((nb, LRU_WIDTH), F32),
                   jax.ShapeDtypeStruct((nb, LRU_WIDTH), F32),
                   jax.ShapeDtypeStruct((nb, (CONV_W - 1) * LRU_WIDTH), F32)],
        compiler_params=_cparams(("arbitrary",)),
        name="lru_decode",
    )(proj_d, proj_d, conv_buf, h0, conv_w, conv_b, wa_bd, ba, wx_bd, bx, lam)


def _xattn_prompt_kernel(x_ref, wq_ref, mk_ref, mv_ref, wo_ref, g_ref, b_ref, wr_ref, br_ref, o_ref):
    x = x_ref[...]
    q = _bdot(x, wq_ref[...]).astype(BF16)
    mk = mk_ref[...].astype(BF16)
    mv = mv_ref[...].astype(BF16)
    outs = []
    for h in range(X_HEADS):
        sl = slice(h * HEAD_DIM, (h + 1) * HEAD_DIM)
        s = _bdot_nt(q[:, sl], mk[:, sl]) * SCALE
        m = jnp.max(s, -1, keepdims=True)
        p = jnp.exp(s - m)
        p = p / jnp.sum(p, -1, keepdims=True)
        outs.append(_bdot(p, mv[:, sl]))
    o = jnp.concatenate(outs, axis=1)
    y = _layer_norm(ALPHA * x + _bdot(o, wo_ref[...]), g_ref[...], b_ref[...])
    logits = _bdot(y, wr_ref[...]) + br_ref[...]
    gate, g_idx = _route(logits)
    lane = lax.broadcasted_iota(jnp.int32, gate.shape, 1)
    for j in range(D_MODEL // LANES):
        o_ref[j] = y[:, j * LANES:(j + 1) * LANES]
    o_ref[D_MODEL // LANES] = jnp.where(lane == 0, g_idx.astype(F32), gate)


def xattn_prompt(x, wq, mem_kv, wo, g, b, wr, br, batch, seq, tt):
    nt = seq // tt
    const = lambda bi, i: (0, 0)
    return pl.pallas_call(
        _xattn_prompt_kernel,
        grid=(batch, nt),
        in_specs=[pl.BlockSpec((tt, D_MODEL), lambda bi, i: (bi * nt + i, 0)),
                  pl.BlockSpec((D_MODEL, X_WIDTH), const),
                  pl.BlockSpec((N_MEM, X_WIDTH), lambda bi, i: (bi, 0)),
                  pl.BlockSpec((N_MEM, X_WIDTH), lambda bi, i: (bi, 1)),
                  pl.BlockSpec((X_WIDTH, D_MODEL), const),
                  pl.BlockSpec((1, D_MODEL), const),
                  pl.BlockSpec((1, D_MODEL), const),
                  pl.BlockSpec((D_MODEL, ROUTER_LANES), const),
                  pl.BlockSpec((1, ROUTER_LANES), const)],
        out_specs=pl.BlockSpec((XG_PLANES, tt, LANES), lambda bi, i: (0, bi * nt + i, 0)),
        out_shape=jax.ShapeDtypeStruct((XG_PLANES, batch * seq, LANES), F32),
        compiler_params=_cparams(("parallel", "parallel")),
        name="xattn_prompt",
    )(x, wq, mem_kv, mem_kv, wo, g, b, wr, br)


def _route(logits):
    lane = lax.broadcasted_iota(jnp.int32, logits.shape, 1)
    big = jnp.int32(ROUTER_LANES)
    ninf = -jnp.inf
    gl = jnp.where(lane < N_GROUPS, logits, ninf)
    gm = jnp.max(gl, -1, keepdims=True)
    g_val = 1.0 / jnp.sum(jnp.exp(gl - gm), -1, keepdims=True)
    g_idx = jnp.min(jnp.where(gl == gm, lane, big), -1, keepdims=True)
    lo = N_GROUPS + EXP_PER_GROUP * g_idx
    el = jnp.where((lane >= lo) & (lane < lo + EXP_PER_GROUP), logits, ninf)
    v1 = jnp.max(el, -1, keepdims=True)
    i1 = jnp.min(jnp.where(el == v1, lane, big), -1, keepdims=True)
    el2 = jnp.where(lane == i1, ninf, el)
    v2 = jnp.max(el2, -1, keepdims=True)
    i2 = jnp.min(jnp.where(el2 == v2, lane, big), -1, keepdims=True)
    e2 = jnp.exp(v2 - v1)
    w1 = g_val / (1.0 + e2)
    w2 = g_val * e2 / (1.0 + e2)
    return jnp.where(lane == i1, w1, 0.0) + jnp.where(lane == i2, w2, 0.0), g_idx


def _moe_dense_kernel(x_ref, wr_ref, br_ref, wg_ref, wu_ref, wd_ref, g_ref, b_ref, o_ref,
                      gate_scr, acc_scr):
    e = pl.program_id(1)

    @pl.when(e == 0)
    def _():
        logits = _bdot(x_ref[...], wr_ref[...]) + br_ref[...]
        gate_scr[...] = _route(logits)[0]
        acc_scr[...] = jnp.zeros_like(acc_scr)

    xb = x_ref[...].astype(BF16)
    lane = lax.broadcasted_iota(jnp.int32, gate_scr.shape, 1)
    gcol = jnp.sum(jnp.where(lane == e + N_GROUPS, gate_scr[...], 0.0), -1, keepdims=True)
    hid = _silu(_bdot(xb, wg_ref[...])) * _bdot(xb, wu_ref[...])
    acc_scr[...] += _bdot(hid * gcol, wd_ref[...])

    @pl.when(e == pl.num_programs(1) - 1)
    def _():
        o_ref[...] = _layer_norm(ALPHA * x_ref[...] + acc_scr[...], g_ref[...], b_ref[...])


def moe_dense(x, wr, br, wg, wu, wd, g, b, layer, tm):
    m = x.shape[0]
    return pl.pallas_call(
        _moe_dense_kernel,
        grid=(m // tm, N_EXPERTS),
        in_specs=[pl.BlockSpec((tm, D_MODEL), lambda i, e: (i, 0)),
                  pl.BlockSpec((D_MODEL, ROUTER_LANES), lambda i, e: (0, 0)),
                  pl.BlockSpec((1, ROUTER_LANES), lambda i, e: (0, 0)),
                  pl.BlockSpec((None, None, D_MODEL, EXP_FF), lambda i, e: (layer, e, 0, 0)),
                  pl.BlockSpec((None, None, D_MODEL, EXP_FF), lambda i, e: (layer, e, 0, 0)),
                  pl.BlockSpec((None, None, EXP_FF, D_MODEL), lambda i, e: (layer, e, 0, 0)),
                  pl.BlockSpec((1, D_MODEL), lambda i, e: (0, 0)),
                  pl.BlockSpec((1, D_MODEL), lambda i, e: (0, 0))],
        out_specs=pl.BlockSpec((tm, D_MODEL), lambda i, e: (i, 0)),
        out_shape=jax.ShapeDtypeStruct((m, D_MODEL), F32),
        scratch_shapes=[pltpu.VMEM((tm, ROUTER_LANES), F32), pltpu.VMEM((tm, D_MODEL), F32)],
        compiler_params=_cparams(("parallel", "arbitrary")),
        name="moe_dense",
    )(x, wr, br, wg, wu, wd, g, b)


def _sc_mesh():
    return plsc.VectorSubcoreMesh(core_axis_name="core", subcore_axis_name="subcore")


def sc_scatter_rows(x, idx, n_out):
    r = x.shape[0]

    @functools.partial(pl.kernel, out_type=jax.ShapeDtypeStruct((n_out, LANES), x.dtype),
                       mesh=_sc_mesh(), scratch_types=[], name="sc_scatter_rows")
    def k(x_hbm, i_hbm, o_hbm):
        def body(x_vmem, i_vmem):
            pltpu.sync_copy(x_vmem, o_hbm.at[i_vmem.at[0]])

        pltpu.emit_pipeline(
            body,
            grid=(r // SC_WINDOW,),
            in_specs=[pl.BlockSpec((SC_WINDOW, LANES), lambda i: (i, 0)),
                      pl.BlockSpec((1, SC_WINDOW), lambda i: (0, i))],
            out_specs=[],
            core_axis_name=("core", "subcore"),
            dimension_semantics=(pltpu.PARALLEL,),
        )(x_hbm, i_hbm)

    return k(x, idx.reshape(1, r))


def sc_gather_rows(table, idx):
    r = idx.shape[0]

    @functools.partial(pl.kernel, out_type=jax.ShapeDtypeStruct((r, LANES), table.dtype),
                       mesh=_sc_mesh(), scratch_types=[], name="sc_gather_rows")
    def k(t_hbm, i_hbm, o_hbm):
        def body(i_vmem, o_vmem):
            pltpu.sync_copy(t_hbm.at[i_vmem.at[0]], o_vmem)

        pltpu.emit_pipeline(
            body,
            grid=(r // SC_WINDOW,),
            in_specs=[pl.BlockSpec((1, SC_WINDOW), lambda i: (0, i))],
            out_specs=[pl.BlockSpec((SC_WINDOW, LANES), lambda i: (i, 0))],
            core_axis_name=("core", "subcore"),
            dimension_semantics=(pltpu.PARALLEL,),
        )(i_hbm, o_hbm)

    return k(table, idx.reshape(1, r))


def _moe_sorted_kernel(tg_ref, nused_ref, xs_ref, wg32_ref, wu32_ref, wd32_ref, g_ref, b_ref, o_ref,
                       wg_ref, wu_ref, wd_ref):
    t = pl.program_id(0)
    used = t < nused_ref[0]

    @pl.when(used & ((t == 0) | (tg_ref[t] != tg_ref[jnp.maximum(t - 1, 0)])))
    def _():
        wg_ref[...] = wg32_ref[...].astype(BF16)
        wu_ref[...] = wu32_ref[...].astype(BF16)
        wd_ref[...] = wd32_ref[...].astype(BF16)

    @pl.when(used)
    def _():
        x = jnp.concatenate([xs_ref[j] for j in range(X_PLANES)], axis=1)
        gate = xs_ref[X_PLANES]
        xb = x.astype(BF16)
        lane = lax.broadcasted_iota(jnp.int32, gate.shape, 1)
        first = N_GROUPS + EXP_PER_GROUP * tg_ref[t]
        acc = jnp.zeros(x.shape, F32)
        for e in range(EXP_PER_GROUP):
            gcol = jnp.sum(jnp.where(lane == first + e, gate, 0.0), -1, keepdims=True)
            hid = _silu(_bdot(xb, wg_ref[e])) * _bdot(xb, wu_ref[e])
            acc = acc + _bdot(hid * gcol, wd_ref[e])
        y = _layer_norm(ALPHA * x + acc, g_ref[...], b_ref[...])
        for j in range(X_PLANES):
            o_ref[j] = y[:, j * LANES:(j + 1) * LANES]


def _group_slots(group_idx, n, tm):
    n_tiles = n // tm + N_GROUPS
    onehot = (group_idx[:, None] == jnp.arange(N_GROUPS)[None, :]).astype(jnp.int32)
    csum = jnp.cumsum(onehot, axis=0)
    counts = csum[-1]
    rank = jnp.sum(onehot * csum, axis=1) - 1
    tiles_g = (counts + tm - 1) // tm
    tile_end = jnp.cumsum(tiles_g)
    slot_base = (tile_end - tiles_g) * tm
    slot = (jnp.sum(onehot * slot_base[None, :], axis=1) + rank).astype(jnp.int32)
    tile_group = jnp.sum((jnp.arange(n_tiles)[:, None] >= tile_end[None, :]).astype(jnp.int32), axis=1)
    tile_group = jnp.minimum(tile_group, N_GROUPS - 1).astype(jnp.int32)
    return slot, tile_group, tile_end[-1:].astype(jnp.int32)


def moe_routed_sc(xg, wg, wu, wd, g, b, layer, n, tm, during_scatter, during_gather, tiles_out):
    slot, tile_group, n_used = _group_slots(xg[X_PLANES, :, 0].astype(jnp.int32), n, tm)
    n_tiles = tile_group.shape[0]
    n_slots = n_tiles * tm
    plane_base = lambda planes: jnp.arange(planes, dtype=jnp.int32)[:, None] * n_slots
    idx_in = (plane_base(XG_PLANES) + slot[None, :]).reshape(-1)
    if tiles_out:
        idx_out = (plane_base(X_PLANES)[None] + slot.reshape(n // SUBLANES, 1, SUBLANES)).reshape(-1)
    else:
        idx_out = (plane_base(X_PLANES) + slot[None, :]).reshape(-1)
    xs = sc_scatter_rows(xg.reshape(XG_PLANES * n, LANES), idx_in, XG_PLANES * n_slots)
    xs = xs.reshape(XG_PLANES, n_slots, LANES)
    xs, side = lax.optimization_barrier((xs, during_scatter()))
    wspec = lambda shp: pl.BlockSpec((None, None, EXP_PER_GROUP) + shp,
                                     lambda t, tg, nu: (layer, tg[t], 0, 0, 0))
    grid_spec = pltpu.PrefetchScalarGridSpec(
        num_scalar_prefetch=2,
        grid=(n_tiles,),
        in_specs=[pl.BlockSpec((XG_PLANES, tm, LANES), lambda t, tg, nu: (0, t, 0)),
                  wspec((D_MODEL, EXP_FF)), wspec((D_MODEL, EXP_FF)), wspec((EXP_FF, D_MODEL)),
                  pl.BlockSpec((1, D_MODEL), lambda t, tg, nu: (0, 0)),
                  pl.BlockSpec((1, D_MODEL), lambda t, tg, nu: (0, 0))],
        out_specs=pl.BlockSpec((X_PLANES, tm, LANES), lambda t, tg, nu: (0, t, 0)),
        scratch_shapes=[pltpu.VMEM((EXP_PER_GROUP, D_MODEL, EXP_FF), BF16),
                        pltpu.VMEM((EXP_PER_GROUP, D_MODEL, EXP_FF), BF16),
                        pltpu.VMEM((EXP_PER_GROUP, EXP_FF, D_MODEL), BF16)])
    grouped = lambda w: w.reshape(w.shape[0], N_GROUPS, EXP_PER_GROUP, w.shape[2], w.shape[3])
    ys = pl.pallas_call(
        _moe_sorted_kernel,
        grid_spec=grid_spec,
        out_shape=jax.ShapeDtypeStruct((X_PLANES, n_slots, LANES), F32),
        compiler_params=_cparams(("arbitrary",)),
        name="moe_sorted",
    )(tile_group, n_used, xs, grouped(wg), grouped(wu), grouped(wd), g, b)
    y = sc_gather_rows(ys.reshape(X_PLANES * n_slots, LANES), idx_out)
    y, side = lax.optimization_barrier((y, during_gather(side)))
    if tiles_out:
        y = y.reshape(n // SUBLANES, X_PLANES, SUBLANES, LANES).transpose(0, 2, 1, 3)
        return y.reshape(n, D_MODEL), side
    return y.reshape(X_PLANES, n, LANES), side


def _t5_bucket(dist):
    max_exact = N_BUCKETS // 2
    d = jnp.maximum(dist, 0)
    df = jnp.maximum(d, 1).astype(F32)
    log_b = max_exact + (jnp.log(df / max_exact) / math.log(MAX_DISTANCE / max_exact)
                         * (N_BUCKETS - max_exact)).astype(jnp.int32)
    return jnp.where(d < max_exact, d, jnp.minimum(log_b, N_BUCKETS - 1))


def _bucket_lookup(rel_bias, bucket):
    out = jnp.zeros(bucket.shape + (rel_bias.shape[1],), F32)
    for i in range(N_BUCKETS):
        out = jnp.where((bucket == i)[..., None], rel_bias[i].astype(F32), out)
    return out


def _prompt_bias(rel_bias):
    qi = jnp.arange(WINDOW)[:, None]
    kj = jnp.arange(2 * WINDOW)[None, :]
    dist = qi + WINDOW - kj
    bias = _bucket_lookup(rel_bias, _t5_bucket(dist)).transpose(2, 0, 1)
    valid = (dist >= 0) & (dist <= WINDOW)
    return jnp.where(valid[None], bias, NEG)


def _decode_table(rel_bias, attn_sink):
    dist = WINDOW - jnp.arange(WINDOW + 1)
    bias = _bucket_lookup(rel_bias, _t5_bucket(dist)).T
    depth = attn_sink.shape[0]
    wide = lambda v: jnp.broadcast_to(v[..., None], v.shape + (WINDOW,))
    per_layer = lambda t: jnp.broadcast_to(t[None], (depth,) + t.shape)
    return jnp.stack([per_layer(bias[:, :WINDOW]), per_layer(wide(bias[:, WINDOW])),
                      wide(attn_sink.astype(F32))], axis=1)


def _block_ones(width):
    idx = jnp.arange(width) // HEAD_DIM
    return (idx[:, None] == idx[None, :])


def _head_rows_mask():
    head = jnp.arange(HG_WIDTH)[None, :] // HEAD_DIM
    return (head == jnp.arange(HG_MROWS)[:, None]).astype(F32)


def _block_diag(w):
    nblk, s, _ = w.shape
    eye = jnp.eye(nblk, dtype=w.dtype)
    return (eye[:, None, :, None] * w[:, :, None, :]).reshape(nblk * s, nblk * s)


def kernel(x_prompt, x_sample, mem_prompt, cache_win_k, cache_win_v, state_hgrn, state_conv, state_lru, cache_mem_k, cache_mem_v, rel_bias, hg_lb, w_in, attn_sink, hg_gain, conv_w, conv_b, lru_wa, lru_ba, lru_wx, lru_bx, lru_lam, w_out, ln1_g, ln1_b, x_wq, x_wk, x_wv, x_wo, ln2_g, ln2_b, r_gw, r_gb, r_ew, r_eb, e_wg, e_wu, e_wd, ln3_g, ln3_b):
    bp, seq, d = x_prompt.shape
    n_tok = bp * seq
    nd = x_sample.shape[0]
    depth = w_in.shape[0]

    lbs = jnp.cumsum(jax.nn.softmax(hg_lb.astype(F32), axis=0), axis=0)
    lbs = lbs - lbs[0]
    loglb = jnp.log(lbs)
    log1mlb = jnp.log1p(-lbs)
    gain4 = jnp.tile(hg_gain, (1, HG_HEADS))

    bias_p = _prompt_bias(rel_bias)
    bdm256 = _block_ones(HG_WIDTH).astype(F32)
    hmask = _head_rows_mask()

    w_in_b = w_in.astype(BF16)
    w_out_b = w_out.astype(BF16)
    wq_b = x_wq.astype(BF16)
    wkv_b = jnp.concatenate([x_wk, x_wv], axis=-1).astype(BF16)
    wo_b = x_wo.astype(BF16)
    rew = r_ew.transpose(0, 2, 1, 3).reshape(depth, d, N_EXPERTS)
    wr = jnp.concatenate([r_gw, rew, jnp.zeros((depth, d, ROUTER_LANES - N_GROUPS - N_EXPERTS), F32)], -1)
    br = jnp.concatenate([r_gb, r_eb.reshape(depth, N_EXPERTS),
                          jnp.zeros((depth, ROUTER_LANES - N_GROUPS - N_EXPERTS), F32)], -1)

    a_w = A_HEADS * HEAD_DIM
    xp = x_prompt.reshape(bp * seq, d)
    xs = x_sample.reshape(nd, d)
    mem = mem_prompt.reshape(bp * N_MEM, d)
    ckt = cache_win_k.transpose(0, 1, 3, 4, 2).reshape(depth, nd, LANES, WINDOW)
    cvt = cache_win_v.transpose(0, 1, 3, 4, 2).reshape(depth, nd, LANES, WINDOW)
    cmkt = cache_mem_k.transpose(0, 1, 3, 4, 2).reshape(depth, nd, X_WIDTH, N_MEM)
    cmvt = cache_mem_v.transpose(0, 1, 3, 4, 2).reshape(depth, nd, X_WIDTH, N_MEM)
    state_t = state_hgrn.transpose(0, 2, 3, 4, 1).reshape(depth, HG_HEADS * HEAD_DIM * HEAD_DIM, nd)
    dec_tab = _decode_table(rel_bias, attn_sink)
    xq_mask = _head_rows_mask()[:X_QROWS]
    head_group = jnp.arange(A_HEADS) // A_REP

    p_wk, p_wv, p_s, p_cb, p_h, p_mk, p_mv = [], [], [], [], [], [], []
    s_cb, s_h = [], []
    for l in range(depth):
        row = lambda v: v[l].reshape(1, -1)
        wa_bd = _block_diag(lru_wa[l]).astype(BF16)
        wx_bd = _block_diag(lru_wx[l]).astype(BF16)
        lru_args = (conv_w[l], row(conv_b), wa_bd, row(lru_ba), wx_bd, row(lru_bx), row(lru_lam))
        wo_parts = [w_out_b[l, :a_w], w_out_b[l, a_w:a_w + HG_WIDTH], w_out_b[l, a_w + HG_WIDTH:]]

        proj = matmul(xp, w_in_b[l], ROW_TILE, IN_COLS)
        oa = attn_prompt(proj, attn_sink[l], bias_p, bp, seq)
        ob, st = hgrn_prompt(proj, row(loglb), row(log1mlb), row(gain4), bdm256, hmask, bp, seq, ROW_TILE)
        oc, hl = lru_prompt(proj, *lru_args, bp, seq, LRU_TILE)
        xp = proj_res_ln(xp, [oa, ob, oc], wo_parts, row(ln1_g), row(ln1_b), ROW_TILE)
        mkv = matmul(mem, wkv_b[l], N_MEM, 2 * X_WIDTH)
        xg = xattn_prompt(xp, wq_b[l], mkv, wo_b[l], row(ln2_g), row(ln2_b), wr[l], br[l:l + 1],
                          bp, seq, ROW_TILE)
        def decode_mixers(xs=xs, ckt=ckt, cvt=cvt, state_t=state_t, l=l, lru_args=lru_args,
                          wo_parts=wo_parts, row=row):
            projd = matmul(xs, w_in_b[l], nd, IN_COLS)
            q3 = projd[:, :a_w].reshape(nd, A_HEADS, 1, HEAD_DIM)
            on_group = head_group[None, :, None, None] == jnp.arange(A_KV_HEADS)[None, None, :, None]
            qblk = jnp.where(on_group, q3, 0.0).reshape(nd, A_HEADS, LANES)
            o3, ckt, cvt = attn_decode(qblk, projd, dec_tab[l], ckt, cvt, l, DEC_BLOCK)
            o4 = o3.reshape(nd, A_HEADS, A_KV_HEADS, HEAD_DIM)
            oa = jnp.sum(jnp.where(on_group, o4, 0.0), axis=2).reshape(nd, a_w)
            gates_t = projd[:, a_w + 2 * LANES:a_w + 2 * LANES + 4 * HG_WIDTH].T
            bc = lambda v: jnp.broadcast_to(v[:, None], (v.shape[0], nd))
            ob_t, state_t = hgrn_decode(gates_t, bc(loglb[l]), bc(log1mlb[l]), bc(hg_gain[l]), state_t, l)
            oc, nh, nbuf = lru_decode(projd, state_conv[l].reshape(nd, -1), state_lru[l], *lru_args)
            xs = proj_res_ln(xs, [oa, ob_t.T, oc], wo_parts, row(ln1_g), row(ln1_b), nd)
            return xs, ckt, cvt, state_t, nh, nbuf

        def decode_rest(side, l=l, row=row):
            xs = side[0]
            qd = matmul(xs, wq_b[l], nd, X_WIDTH)
            qdb = qd[:, None, :] * xq_mask[None, :, :]
            od = xattn_decode(qdb, cmkt, cmvt, xq_mask, l, DEC_BLOCK)
            xs = proj_res_ln(xs, [od], [wo_b[l]], row(ln2_g), row(ln2_b), nd)
            xs = moe_dense(xs, wr[l], br[l:l + 1], e_wg, e_wu, e_wd, row(ln3_g), row(ln3_b), l, nd)
            return (xs,) + tuple(side[1:])

        xp, (xs, ckt, cvt, state_t, nh, nbuf) = moe_routed_sc(
            xg, e_wg, e_wu, e_wd, row(ln3_g), row(ln3_b), l, n_tok, MOE_TM, decode_mixers, decode_rest,
            tiles_out=(l == depth - 1))

        proj3 = proj.reshape(bp, seq, IN_COLS)
        p_wk.append(proj3[:, seq - WINDOW:, a_w:a_w + LANES].reshape(bp, WINDOW, A_KV_HEADS, HEAD_DIM))
        p_wv.append(proj3[:, seq - WINDOW:, a_w + LANES:a_w + 2 * LANES].reshape(bp, WINDOW, A_KV_HEADS, HEAD_DIM))
        st5 = st.reshape(bp, HG_HEADS, HEAD_DIM, HG_HEADS, HEAD_DIM)
        p_s.append(jnp.stack([st5[:, h, :, h, :] for h in range(HG_HEADS)], 1).transpose(0, 1, 3, 2))
        p_cb.append(proj3[:, seq - (CONV_W - 1):, IN_COLS - 2 * LRU_WIDTH:IN_COLS - LRU_WIDTH])
        p_h.append(hl.reshape(bp, LRU_WIDTH))
        p_mk.append(mkv[:, :X_WIDTH].reshape(bp, N_MEM, X_HEADS, HEAD_DIM))
        p_mv.append(mkv[:, X_WIDTH:].reshape(bp, N_MEM, X_HEADS, HEAD_DIM))

        s_cb.append(nbuf.reshape(nd, CONV_W - 1, LRU_WIDTH))
        s_h.append(nh)

    unkey = lambda c: c.reshape(depth, nd, A_KV_HEADS, HEAD_DIM, WINDOW).transpose(0, 1, 4, 2, 3)
    s_s = state_t.reshape(depth, HG_HEADS, HEAD_DIM, HEAD_DIM, nd).transpose(0, 4, 1, 2, 3)
    return (xp.reshape(bp, seq, d), xs.reshape(nd, 1, d),
            jnp.stack(p_wk), jnp.stack(p_wv), jnp.stack(p_s), jnp.stack(p_cb), jnp.stack(p_h),
            jnp.stack(p_mk), jnp.stack(p_mv),
            unkey(ckt), unkey(cvt), s_s, jnp.stack(s_cb), jnp.stack(s_h))
```

```python
import functools
import math

import jax
import jax.numpy as jnp
from jax import lax
from jax.experimental import pallas as pl
from jax.experimental.pallas import tpu as pltpu
from jax.experimental.pallas import tpu_sc as plsc

F32 = jnp.float32
BF16 = jnp.bfloat16
MIX_DTYPE = BF16

D_MODEL = 1024
DEPTH = 4
HEAD_DIM = 64
A_HEADS = 8
A_KV_HEADS = 2
A_REP = A_HEADS // A_KV_HEADS
WINDOW = 128
A_QB = 2
N_BUCKETS = 32
MAX_DISTANCE = 128
HG_WIDTH = 256
HG_HEADS = 4
HG_CHUNK = 64
HG_TB = 32
HG_MROWS = 8
LOG2E = math.log2(math.e)
LRU_WIDTH = 256
LRU_BLOCKS = 4
CONV_W = 4
LRU_C = 8.0
N_MEM = 256
X_HEADS = 4
X_WIDTH = X_HEADS * HEAD_DIM
X_QROWS = 8
N_GROUPS = 4
EXP_PER_GROUP = 4
N_EXPERTS = N_GROUPS * EXP_PER_GROUP
EXP_FF = D_MODEL // 4
ALPHA = (2 * DEPTH) ** 0.25
LN_EPS = 1e-5
RMS_EPS = 1e-6
IN_COLS = 2304
SCALE = HEAD_DIM ** -0.5
NEG = -1e30
LANES = 128
SUBLANES = 8
ROUTER_LANES = 128
XG_WIDTH = D_MODEL + ROUTER_LANES
XG_PLANES = XG_WIDTH // LANES
X_PLANES = D_MODEL // LANES
MOE_TM = 512
ROW_TILE = 1024
LRU_TILE = 512
DEC_BLOCK = 16
SC_WINDOW = 128
VMEM_LIMIT = 48 * 1024 * 1024


def _cparams(sem):
    return pltpu.CompilerParams(dimension_semantics=sem, vmem_limit_bytes=VMEM_LIMIT)


def _bdot(a, b):
    return jnp.dot(a.astype(BF16), b.astype(BF16), preferred_element_type=F32)


def _bdot_nt(a, b):
    return lax.dot_general(a.astype(BF16), b.astype(BF16), (((1,), (1,)), ((), ())),
                           preferred_element_type=F32)


def _bdot_tn(a, b):
    return lax.dot_general(a.astype(BF16), b.astype(BF16), (((0,), (0,)), ((), ())),
                           preferred_element_type=F32)


def _rb(x):
    return x.astype(BF16).astype(F32)


def _silu(x):
    return x * jax.nn.sigmoid(x)


def _neg_expm1(x):
    return -jnp.tanh(0.5 * x) * (jnp.exp(x) + 1.0)


def _softplus(x):
    return jnp.maximum(x, 0.0) + jnp.log1p(jnp.exp(-jnp.abs(x)))


def _gelu_tanh(x):
    return 0.5 * x * (1.0 + jnp.tanh(math.sqrt(2.0 / math.pi) * (x + 0.044715 * (x * x * x))))


def _layer_norm(y, g, b):
    mu = jnp.mean(y, -1, keepdims=True)
    yc = y - mu
    var = jnp.mean(yc * yc, -1, keepdims=True)
    return yc * lax.rsqrt(var + LN_EPS) * g + b


def _rows(x_ref):
    if len(x_ref.shape) == 2:
        return x_ref[...]
    return jnp.concatenate([x_ref[j] for j in range(x_ref.shape[0])], axis=1)


def _rows_spec(x, tm, nargs):
    if x.ndim == 2:
        return pl.BlockSpec((tm, x.shape[1]), (lambda i: (i, 0)) if nargs == 1 else (lambda i, j: (i, 0)))
    blk = (x.shape[0], tm, LANES)
    return pl.BlockSpec(blk, (lambda i: (0, i, 0)) if nargs == 1 else (lambda i, j: (0, i, 0)))


def _mm_kernel(x_ref, w_ref, o_ref):
    o_ref[...] = _bdot(_rows(x_ref), w_ref[...])


def matmul(x, w, tm, tn):
    m = x.shape[-2]
    k, n = w.shape
    return pl.pallas_call(
        _mm_kernel,
        grid=(m // tm, n // tn),
        in_specs=[_rows_spec(x, tm, 2),
                  pl.BlockSpec((k, tn), lambda i, j: (0, j))],
        out_specs=pl.BlockSpec((tm, tn), lambda i, j: (i, j)),
        out_shape=jax.ShapeDtypeStruct((m, n), F32),
        compiler_params=_cparams(("parallel", "parallel")),
        name="matmul",
    )(x, w)


def _proj_res_ln_kernel(n_in, x_ref, *refs):
    a_refs = refs[:n_in]
    w_refs = refs[n_in:2 * n_in]
    g_ref, b_ref, o_ref = refs[2 * n_in:]
    y = ALPHA * _rows(x_ref)
    for a_ref, w_ref in zip(a_refs, w_refs):
        y = y + _bdot(a_ref[...], w_ref[...])
    o_ref[...] = _layer_norm(y, g_ref[...], b_ref[...])


def proj_res_ln(x, a_list, w_list, g, b, tm):
    m = x.shape[-2]
    d = w_list[0].shape[1]
    n_in = len(a_list)
    in_specs = [_rows_spec(x, tm, 1)]
    in_specs += [pl.BlockSpec((tm, a.shape[1]), lambda i: (i, 0)) for a in a_list]
    in_specs += [pl.BlockSpec(w.shape, lambda i: (0, 0)) for w in w_list]
    in_specs += [pl.BlockSpec((1, d), lambda i: (0, 0))] * 2
    return pl.pallas_call(
        functools.partial(_proj_res_ln_kernel, n_in),
        grid=(m // tm,),
        in_specs=in_specs,
        out_specs=pl.BlockSpec((tm, d), lambda i: (i, 0)),
        out_shape=jax.ShapeDtypeStruct((m, d), F32),
        compiler_params=_cparams(("parallel",)),
        name="proj_res_ln",
    )(x, *a_list, *w_list, g, b)


def _attn_prompt_kernel(sink_ref, q_ref, kc_ref, kp_ref, vc_ref, vp_ref, bias_ref, o_ref):
    n = pl.program_id(1)
    col = lax.broadcasted_iota(jnp.int32, (WINDOW, 2 * WINDOW), 1)
    first = jnp.where((n == 0) & (col < WINDOW), NEG, 0.0)
    kk = jnp.concatenate([kp_ref[...], kc_ref[...]], axis=0).astype(BF16)
    vv = jnp.concatenate([vp_ref[...], vc_ref[...]], axis=0).astype(BF16)
    q = q_ref[...].astype(BF16)
    for u in range(A_QB):
        rows = slice(u * WINDOW, (u + 1) * WINDOW)
        keys = slice(u * WINDOW, (u + 2) * WINDOW)
        outs = []
        for h in range(A_HEADS):
            g = h // A_REP
            qh = q[rows, h * HEAD_DIM:(h + 1) * HEAD_DIM]
            kg = kk[keys, g * HEAD_DIM:(g + 1) * HEAD_DIM]
            vg = vv[keys, g * HEAD_DIM:(g + 1) * HEAD_DIM]
            s = _bdot_nt(qh, kg) * SCALE + bias_ref[h]
            if u == 0:
                s = s + first
            sink = sink_ref[h]
            m = jnp.maximum(jnp.max(s, -1, keepdims=True), sink)
            p = jnp.exp(s - m)
            den = jnp.sum(p, -1, keepdims=True) + jnp.exp(sink - m)
            outs.append(_bdot(p / den, vg))
        o_ref[rows, :] = jnp.concatenate(outs, axis=1).astype(o_ref.dtype)


def attn_prompt(proj, sink, bias, batch, seq):
    nb = seq // WINDOW
    ns = nb // A_QB
    tq = A_QB * WINDOW
    qcol = 0
    kcol = (A_HEADS * HEAD_DIM) // LANES
    vcol = kcol + 1

    def cur(c):
        return lambda b, n: (b * ns + n, c)

    def prev(c):
        return lambda b, n: (b * nb + jnp.maximum(n * A_QB - 1, 0), c)

    return pl.pallas_call(
        _attn_prompt_kernel,
        grid=(batch, ns),
        in_specs=[pl.BlockSpec(memory_space=pltpu.SMEM),
                  pl.BlockSpec((tq, A_HEADS * HEAD_DIM), cur(qcol)),
                  pl.BlockSpec((tq, LANES), cur(kcol)),
                  pl.BlockSpec((WINDOW, LANES), prev(kcol)),
                  pl.BlockSpec((tq, LANES), cur(vcol)),
                  pl.BlockSpec((WINDOW, LANES), prev(vcol)),
                  pl.BlockSpec((A_HEADS, WINDOW, 2 * WINDOW), lambda b, n: (0, 0, 0))],
        out_specs=pl.BlockSpec((tq, A_HEADS * HEAD_DIM), cur(0)),
        out_shape=jax.ShapeDtypeStruct((batch * seq, A_HEADS * HEAD_DIM), MIX_DTYPE),
        compiler_params=_cparams(("parallel", "parallel")),
        name="attn_prompt",
    )(sink, proj, proj, proj, proj, proj, bias)


def _attn_decode_kernel(qb_ref, kn_ref, vn_ref, knt_ref, vnt_ref, ck_ref, cv_ref, tab_ref,
                        o_ref, ok_ref, ov_ref):
    bb = qb_ref.shape[0]
    ck = ck_ref[...]
    cv = cv_ref[...]
    qb = qb_ref[...]
    kn = kn_ref[...]
    vn = vn_ref[...]
    bias_j = tab_ref[0]
    bias_n = tab_ref[1][:, 0:1]
    sink = tab_ref[2][:, 0:1]
    s = lax.dot_general(qb.astype(BF16), ck.astype(BF16), (((2,), (1,)), ((0,), (0,))),
                        preferred_element_type=F32) * SCALE + bias_j[None]
    sn = jnp.sum(_rb(qb) * _rb(kn)[:, None, :], -1, keepdims=True) * SCALE + bias_n[None]
    m = jnp.maximum(jnp.maximum(jnp.max(s, -1, keepdims=True), sn), sink[None])
    p = jnp.exp(s - m)
    pn = jnp.exp(sn - m)
    den = jnp.sum(p, -1, keepdims=True) + pn + jnp.exp(sink[None] - m)
    o = lax.dot_general((p / den).astype(BF16), cv.astype(BF16), (((2,), (2,)), ((0,), (0,))),
                        preferred_element_type=F32)
    o_ref[...] = o + _rb(pn / den) * _rb(vn)[:, None, :]
    lane = lax.broadcasted_iota(jnp.int32, (LANES, LANES), 1)
    for b in range(bb):
        ok_ref[b] = jnp.where(lane == WINDOW - 1, knt_ref[:, b:b + 1], pltpu.roll(ck[b], WINDOW - 1, 1))
        ov_ref[b] = jnp.where(lane == WINDOW - 1, vnt_ref[:, b:b + 1], pltpu.roll(cv[b], WINDOW - 1, 1))


def attn_decode(qblk, proj_d, table, cache_k, cache_v, layer, bb):
    nbatch = proj_d.shape[0]
    a_w = A_HEADS * HEAD_DIM
    cols = lambda c: proj_d[:, c:c + LANES].reshape(nbatch // bb, bb, LANES).transpose(0, 2, 1)
    knt, vnt = cols(a_w), cols(a_w + LANES)
    kcol = (A_HEADS * HEAD_DIM) // LANES
    cache_spec = pl.BlockSpec((None, bb, LANES, WINDOW), lambda i: (layer, i, 0, 0))
    col_spec = pl.BlockSpec((None, LANES, bb), lambda i: (i, 0, 0))
    return pl.pallas_call(
        _attn_decode_kernel,
        grid=(nbatch // bb,),
        in_specs=[pl.BlockSpec((bb, A_HEADS, LANES), lambda i: (i, 0, 0)),
                  pl.BlockSpec((bb, LANES), lambda i: (i, kcol)),
                  pl.BlockSpec((bb, LANES), lambda i: (i, kcol + 1)),
                  col_spec, col_spec, cache_spec, cache_spec,
                  pl.BlockSpec((3, A_HEADS, WINDOW), lambda i: (0, 0, 0))],
        out_specs=[pl.BlockSpec((bb, A_HEADS, LANES), lambda i: (i, 0, 0)), cache_spec, cache_spec],
        out_shape=[jax.ShapeDtypeStruct((nbatch, A_HEADS, LANES), F32),
                   jax.ShapeDtypeStruct(cache_k.shape, F32),
                   jax.ShapeDtypeStruct(cache_v.shape, F32)],
        input_output_aliases={5: 1, 6: 2},
        compiler_params=_cparams(("arbitrary",)),
        name="attn_decode",
    )(qblk, proj_d, proj_d, knt, vnt, cache_k, cache_v, table)


def _xattn_decode_kernel(qb_ref, mk_ref, mv_ref, hm_ref, o_ref):
    qb = qb_ref[...]
    s = lax.dot_general(qb.astype(BF16), mk_ref[...].astype(BF16), (((2,), (1,)), ((0,), (0,))),
                        preferred_element_type=F32) * SCALE
    m = jnp.max(s, -1, keepdims=True)
    p = jnp.exp(s - m)
    p = p / jnp.sum(p, -1, keepdims=True)
    o = lax.dot_general(p.astype(BF16), mv_ref[...].astype(BF16), (((2,), (2,)), ((0,), (0,))),
                        preferred_element_type=F32)
    o_ref[...] = jnp.sum(o * hm_ref[...][None], axis=1)


def xattn_decode(qblk, mem_k, mem_v, hmask, layer, bb):
    nbatch = qblk.shape[0]
    mem_spec = pl.BlockSpec((None, bb, X_WIDTH, N_MEM), lambda i: (layer, i, 0, 0))
    return pl.pallas_call(
        _xattn_decode_kernel,
        grid=(nbatch // bb,),
        in_specs=[pl.BlockSpec((bb, X_QROWS, X_WIDTH), lambda i: (i, 0, 0)), mem_spec, mem_spec,
                  pl.BlockSpec((X_QROWS, X_WIDTH), lambda i: (0, 0))],
        out_specs=pl.BlockSpec((bb, X_WIDTH), lambda i: (i, 0)),
        out_shape=jax.ShapeDtypeStruct((nbatch, X_WIDTH), F32),
        compiler_params=_cparams(("parallel",)),
        name="xattn_decode",
    )(qblk, mem_k, mem_v, hmask)


def _hgrn_gates(hq, hf, loglb, log1mlb):
    ls = jnp.minimum(hf, 0.0) - jnp.log1p(jnp.exp(-jnp.abs(hf)))
    b = log1mlb + ls
    lf = jnp.maximum(loglb, b) + jnp.log1p(jnp.exp(-jnp.abs(loglb - b)))
    return _silu(hq), lf, _neg_expm1(lf)


def _hgrn_prompt_kernel(hq_ref, hf_ref, hi_ref, hg_ref, loglb_ref, log1mlb_ref, gain_ref,
                        bdm_ref, hm_ref, ob_ref, st_ref, st_scr, q_scr, k_scr, cum_scr, o_scr):
    i = pl.program_id(1)
    tt = hq_ref.shape[0]
    c = HG_CHUNK
    tb = HG_TB

    @pl.when(i == 0)
    def _():
        st_scr[...] = jnp.zeros_like(st_scr)

    qs, lf, kk = _hgrn_gates(hq_ref[...], hf_ref[...], loglb_ref[...], log1mlb_ref[...])
    row = lax.broadcasted_iota(jnp.int32, (tt, HG_WIDTH), 0) & (c - 1)
    cum = lf
    sh = 1
    while sh < c:
        cum = cum + jnp.where(row >= sh, pltpu.roll(cum, sh, 0), 0.0)
        sh *= 2
    q_scr[...] = qs
    k_scr[...] = kk
    cum_scr[...] = cum

    bdm = bdm_ref[...]
    hmask = hm_ref[...]
    def chunk(ci, carry):
        r0 = pl.multiple_of(ci * c, c)
        r = pl.ds(r0, c)
        cu = cum_scr[r, :]
        q = q_scr[r, :]
        k = k_scr[r, :]
        v = hi_ref[r, :]
        vb = v.astype(BF16)
        qb = _rb(q)
        cu2 = cu * LOG2E
        last = cu[c - 1:c, :]
        st = st_scr[...]
        o_inter = _bdot_nt(q * jnp.exp(cu), st)
        for j in range(c // tb):
            ns = tb * (j + 1)
            ti = lax.broadcasted_iota(jnp.int32, (tb, ns, HG_WIDTH), 0) + tb * j
            si = lax.broadcasted_iota(jnp.int32, (tb, ns, HG_WIDTH), 1)
            cut = cu2[tb * j:tb * (j + 1), :]
            dec = jnp.exp2(jnp.where(ti >= si, cut[:, None, :] - cu2[None, :ns, :], NEG))
            a2 = (dec * k[None, :ns, :]).astype(BF16)
            q4 = qb[tb * j:tb * (j + 1), None, :] * hmask[None, :, :]
            att = lax.dot_general(q4, a2, (((2,), (2,)), ((0,), (0,))),
                                  preferred_element_type=F32)
            w = jnp.dot(_rb(att.reshape(tb * HG_MROWS, ns)), vb[:ns, :],
                        preferred_element_type=F32).reshape(tb, HG_MROWS, HG_WIDTH)
            o_intra = jnp.sum(w * hmask[None, :, :], axis=1)
            o_scr[pl.ds(r0 + tb * j, tb), :] = o_intra + o_inter[tb * j:tb * (j + 1), :]
        upd = _bdot_tn(v, k * jnp.exp(last - cu))
        st_scr[...] = st * jnp.exp(last) + upd * bdm
        return carry

    lax.fori_loop(0, tt // c, chunk, 0, unroll=2)

    o = o_scr[...]
    ms = jnp.dot(o * o, bdm, precision=lax.Precision.HIGHEST,
                 preferred_element_type=F32) * (1.0 / HEAD_DIM)
    ob_ref[...] = (o * lax.rsqrt(ms + RMS_EPS) * gain_ref[...] * _silu(hg_ref[...])).astype(ob_ref.dtype)

    @pl.when(i == pl.num_programs(1) - 1)
    def _():
        st_ref[...] = st_scr[...]


def hgrn_prompt(proj, loglb, log1mlb, gain4, bdm, hmask, batch, seq, tt):
    nt = seq // tt
    base = (A_HEADS + 2 * A_KV_HEADS) * HEAD_DIM // HG_WIDTH

    def col(cblk):
        return pl.BlockSpec((tt, HG_WIDTH), lambda b, i: (b * nt + i, cblk))

    row_spec = pl.BlockSpec((1, HG_WIDTH), lambda b, i: (0, 0))
    mat_spec = pl.BlockSpec((HG_WIDTH, HG_WIDTH), lambda b, i: (0, 0))
    return pl.pallas_call(
        _hgrn_prompt_kernel,
        grid=(batch, nt),
        in_specs=[col(base), col(base + 1), col(base + 2), col(base + 3),
                  row_spec, row_spec, row_spec, mat_spec,
                  pl.BlockSpec((HG_MROWS, HG_WIDTH), lambda b, i: (0, 0))],
        out_specs=[pl.BlockSpec((tt, HG_WIDTH), lambda b, i: (b * nt + i, 0)),
                   pl.BlockSpec((None, HG_WIDTH, HG_WIDTH), lambda b, i: (b, 0, 0))],
        out_shape=[jax.ShapeDtypeStruct((batch * seq, HG_WIDTH), MIX_DTYPE),
                   jax.ShapeDtypeStruct((batch, HG_WIDTH, HG_WIDTH), F32)],
        scratch_shapes=[pltpu.VMEM((HG_WIDTH, HG_WIDTH), F32),
                        pltpu.VMEM((tt, HG_WIDTH), F32),
                        pltpu.VMEM((tt, HG_WIDTH), F32),
                        pltpu.VMEM((tt, HG_WIDTH), F32),
                        pltpu.VMEM((tt, HG_WIDTH), F32)],
        compiler_params=_cparams(("parallel", "arbitrary")),
        name="hgrn_prompt",
    )(proj, proj, proj, proj, loglb, log1mlb, gain4, bdm, hmask)


def _hgrn_decode_kernel(hq_ref, hf_ref, hi_ref, hg_ref, loglb_ref, log1mlb_ref, gain_ref, s_ref,
                        ob_ref, so_ref):
    nb = hq_ref.shape[1]
    qs, lf, kk = _hgrn_gates(hq_ref[...], hf_ref[...], loglb_ref[...], log1mlb_ref[...])
    v = hi_ref[...]
    f = jnp.exp(lf)
    s = s_ref[...].reshape(HEAD_DIM, HEAD_DIM, nb)
    att = jnp.sum(_rb(qs) * _rb(kk), axis=0, keepdims=True)
    o = _rb(att) * _rb(v) + jnp.sum(_rb(qs * f)[:, None, :] * _rb(s), axis=0)
    s_new = f[:, None, :] * s + _rb(kk)[:, None, :] * _rb(v)[None, :, :]
    so_ref[...] = s_new.reshape(HEAD_DIM * HEAD_DIM, nb)
    ms = jnp.mean(o * o, axis=0, keepdims=True)
    ob_ref[...] = (o * lax.rsqrt(ms + RMS_EPS) * gain_ref[...] * _silu(hg_ref[...])).astype(ob_ref.dtype)


def hgrn_decode(gates_t, loglb_t, log1mlb_t, gain_t, state_t, layer):
    nb = gates_t.shape[1]

    def blk(off):
        return pl.BlockSpec((HEAD_DIM, nb), lambda h: (off * HG_HEADS + h, 0))

    par = pl.BlockSpec((HEAD_DIM, nb), lambda h: (h, 0))
    st = pl.BlockSpec((None, HEAD_DIM * HEAD_DIM, nb), lambda h: (layer, h, 0))
    return pl.pallas_call(
        _hgrn_decode_kernel,
        grid=(HG_HEADS,),
        in_specs=[blk(0), blk(1), blk(2), blk(3), par, par,
                  pl.BlockSpec((HEAD_DIM, nb), lambda h: (0, 0)), st],
        out_specs=[par, st],
        out_shape=[jax.ShapeDtypeStruct((HG_WIDTH, nb), F32),
                   jax.ShapeDtypeStruct(state_t.shape, F32)],
        input_output_aliases={7: 1},
        compiler_params=_cparams(("arbitrary",)),
        name="hgrn_decode",
    )(gates_t, gates_t, gates_t, gates_t, loglb_t, log1mlb_t, gain_t, state_t)


def _lru_gates(xc, wa_ref, ba_ref, wx_ref, bx_ref, lam_ref):
    r = jax.nn.sigmoid(_bdot(xc, wa_ref[...]) + ba_ref[...])
    gi = jax.nn.sigmoid(_bdot(xc, wx_ref[...]) + bx_ref[...])
    log_a = -LRU_C * r * _softplus(-lam_ref[...])
    a = jnp.exp(log_a)
    bterm = jnp.sqrt(_neg_expm1(2.0 * log_a)) * (gi * xc)
    return a, bterm


def _lru_prompt_kernel(lx_ref, lg_ref, cw_ref, cb_ref, wa_ref, ba_ref, wx_ref, bx_ref, lam_ref,
                       oc_ref, hl_ref, ext_scr, h_scr):
    i = pl.program_id(1)
    tt = lx_ref.shape[0]
    pad = SUBLANES

    @pl.when(i == 0)
    def _():
        ext_scr[0:pad, :] = jnp.zeros((pad, LRU_WIDTH), F32)
        h_scr[...] = jnp.zeros_like(h_scr)

    x = lx_ref[...]
    ext_scr[pad:pad + tt, :] = x
    xc = cb_ref[...] + cw_ref[CONV_W - 1:CONV_W, :] * x
    for j in range(CONV_W - 1):
        back = CONV_W - 1 - j
        xc = xc + cw_ref[j:j + 1, :] * ext_scr[pad - back:pad - back + tt, :]
    ext_scr[0:pad, :] = x[tt - pad:tt, :]

    a, bterm = _lru_gates(xc, wa_ref, ba_ref, wx_ref, bx_ref, lam_ref)
    row = lax.broadcasted_iota(jnp.int32, (tt, LRU_WIDTH), 0)
    sh = 1
    while sh < tt:
        keep = row >= sh
        b_s = jnp.where(keep, pltpu.roll(bterm, sh, 0), 0.0)
        a_s = jnp.where(keep, pltpu.roll(a, sh, 0), 1.0)
        bterm = a * b_s + bterm
        a = a * a_s
        sh *= 2
    h = a * h_scr[...] + bterm
    h_scr[...] = h[tt - 1:tt, :]
    oc_ref[...] = (h * _gelu_tanh(lg_ref[...])).astype(oc_ref.dtype)

    @pl.when(i == pl.num_programs(1) - 1)
    def _():
        hl_ref[...] = h[tt - 1:tt, :]


def lru_prompt(proj, conv_w, conv_b, wa_bd, ba, wx_bd, bx, lam, batch, seq, tt):
    nt = seq // tt
    base = IN_COLS // LRU_WIDTH - 2

    def col(cblk):
        return pl.BlockSpec((tt, LRU_WIDTH), lambda b, i: (b * nt + i, cblk))

    row_spec = pl.BlockSpec((1, LRU_WIDTH), lambda b, i: (0, 0))
    mat_spec = pl.BlockSpec((LRU_WIDTH, LRU_WIDTH), lambda b, i: (0, 0))
    return pl.pallas_call(
        _lru_prompt_kernel,
        grid=(batch, nt),
        in_specs=[col(base), col(base + 1),
                  pl.BlockSpec((CONV_W, LRU_WIDTH), lambda b, i: (0, 0)), row_spec,
                  mat_spec, row_spec, mat_spec, row_spec, row_spec],
        out_specs=[pl.BlockSpec((tt, LRU_WIDTH), lambda b, i: (b * nt + i, 0)),
                   pl.BlockSpec((None, 1, LRU_WIDTH), lambda b, i: (b, 0, 0))],
        out_shape=[jax.ShapeDtypeStruct((batch * seq, LRU_WIDTH), MIX_DTYPE),
                   jax.ShapeDtypeStruct((batch, 1, LRU_WIDTH), F32)],
        scratch_shapes=[pltpu.VMEM((tt + 8, LRU_WIDTH), F32),
                        pltpu.VMEM((1, LRU_WIDTH), F32)],
        compiler_params=_cparams(("parallel", "arbitrary")),
        name="lru_prompt",
    )(proj, proj, conv_w, conv_b, wa_bd, ba, wx_bd, bx, lam)


def _lru_decode_kernel(lx_ref, lg_ref, buf_ref, h0_ref, cw_ref, cb_ref, wa_ref, ba_ref, wx_ref,
                       bx_ref, lam_ref, oc_ref, hn_ref, nbuf_ref):
    x = lx_ref[...]
    buf = buf_ref[...]
    xc = cb_ref[...] + cw_ref[CONV_W - 1:CONV_W, :] * x
    for j in range(CONV_W - 1):
        xc = xc + cw_ref[j:j + 1, :] * buf[:, j * LRU_WIDTH:(j + 1) * LRU_WIDTH]
    a, bterm = _lru_gates(xc, wa_ref, ba_ref, wx_ref, bx_ref, lam_ref)
    h = a * h0_ref[...] + bterm
    hn_ref[...] = h
    oc_ref[...] = (h * _gelu_tanh(lg_ref[...])).astype(oc_ref.dtype)
    nbuf_ref[...] = jnp.concatenate([buf[:, LRU_WIDTH:], x], axis=1)


def lru_decode(proj_d, conv_buf, h0, conv_w, conv_b, wa_bd, ba, wx_bd, bx, lam):
    nb = proj_d.shape[0]
    base = IN_COLS // LRU_WIDTH - 2
    row_spec = pl.BlockSpec((1, LRU_WIDTH), lambda i: (0, 0))
    mat_spec = pl.BlockSpec((LRU_WIDTH, LRU_WIDTH), lambda i: (0, 0))
    act = pl.BlockSpec((nb, LRU_WIDTH), lambda i: (0, 0))
    bufs = pl.BlockSpec((nb, (CONV_W - 1) * LRU_WIDTH), lambda i: (0, 0))
    return pl.pallas_call(
        _lru_decode_kernel,
        grid=(1,),
        in_specs=[pl.BlockSpec((nb, LRU_WIDTH), lambda i: (0, base)),
                  pl.BlockSpec((nb, LRU_WIDTH), lambda i: (0, base + 1)),
                  bufs, act, pl.BlockSpec((CONV_W, LRU_WIDTH), lambda i: (0, 0)), row_spec,
                  mat_spec, row_spec, mat_spec, row_spec, row_spec],
        out_specs=[act, act, bufs],
        out_shape=[jax.ShapeDtypeStruct((nb, LRU_WIDTH), F32),
                   jax.ShapeDtypeStruct((nb, LRU_WIDTH), F32),
                   jax.ShapeDtypeStruct((nb, (CONV_W - 1) * LRU_WIDTH), F32)],
        compiler_params=_cparams(("arbitrary",)),
        name="lru_decode",
    )(proj_d, proj_d, conv_buf, h0, conv_w, conv_b, wa_bd, ba, wx_bd, bx, lam)


def _xattn_prompt_kernel(x_ref, wq_ref, mk_ref, mv_ref, wo_ref, g_ref, b_ref, wr_ref, br_ref, o_ref):
    x = x_ref[...]
    q = _bdot(x, wq_ref[...]).astype(BF16)
    mk = mk_ref[...].astype(BF16)
    mv = mv_ref[...].astype(BF16)
    outs = []
    for h in range(X_HEADS):
        sl = slice(h * HEAD_DIM, (h + 1) * HEAD_DIM)
        s = _bdot_nt(q[:, sl], mk[:, sl]) * SCALE
        m = jnp.max(s, -1, keepdims=True)
        p = jnp.exp(s - m)
        p = p / jnp.sum(p, -1, keepdims=True)
        outs.append(_bdot(p, mv[:, sl]))
    o = jnp.concatenate(outs, axis=1)
    y = _layer_norm(ALPHA * x + _bdot(o, wo_ref[...]), g_ref[...], b_ref[...])
    logits = _bdot(y, wr_ref[...]) + br_ref[...]
    gate, g_idx = _route(logits)
    lane = lax.broadcasted_iota(jnp.int32, gate.shape, 1)
    for j in range(D_MODEL // LANES):
        o_ref[j] = y[:, j * LANES:(j + 1) * LANES]
    o_ref[D_MODEL // LANES] = jnp.where(lane == 0, g_idx.astype(F32), gate)


def xattn_prompt(x, wq, mem_kv, wo, g, b, wr, br, batch, seq, tt):
    nt = seq // tt
    const = lambda bi, i: (0, 0)
    return pl.pallas_call(
        _xattn_prompt_kernel,
        grid=(batch, nt),
        in_specs=[pl.BlockSpec((tt, D_MODEL), lambda bi, i: (bi * nt + i, 0)),
                  pl.BlockSpec((D_MODEL, X_WIDTH), const),
                  pl.BlockSpec((N_MEM, X_WIDTH), lambda bi, i: (bi, 0)),
                  pl.BlockSpec((N_MEM, X_WIDTH), lambda bi, i: (bi, 1)),
                  pl.BlockSpec((X_WIDTH, D_MODEL), const),
                  pl.BlockSpec((1, D_MODEL), const),
                  pl.BlockSpec((1, D_MODEL), const),
                  pl.BlockSpec((D_MODEL, ROUTER_LANES), const),
                  pl.BlockSpec((1, ROUTER_LANES), const)],
        out_specs=pl.BlockSpec((XG_PLANES, tt, LANES), lambda bi, i: (0, bi * nt + i, 0)),
        out_shape=jax.ShapeDtypeStruct((XG_PLANES, batch * seq, LANES), F32),
        compiler_params=_cparams(("parallel", "parallel")),
        name="xattn_prompt",
    )(x, wq, mem_kv, mem_kv, wo, g, b, wr, br)


def _route(logits):
    lane = lax.broadcasted_iota(jnp.int32, logits.shape, 1)
    big = jnp.int32(ROUTER_LANES)
    ninf = -jnp.inf
    gl = jnp.where(lane < N_GROUPS, logits, ninf)
    gm = jnp.max(gl, -1, keepdims=True)
    g_val = 1.0 / jnp.sum(jnp.exp(gl - gm), -1, keepdims=True)
    g_idx = jnp.min(jnp.where(gl == gm, lane, big), -1, keepdims=True)
    lo = N_GROUPS + EXP_PER_GROUP * g_idx
    el = jnp.where((lane >= lo) & (lane < lo + EXP_PER_GROUP), logits, ninf)
    v1 = jnp.max(el, -1, keepdims=True)
    i1 = jnp.min(jnp.where(el == v1, lane, big), -1, keepdims=True)
    el2 = jnp.where(lane == i1, ninf, el)
    v2 = jnp.max(el2, -1, keepdims=True)
    i2 = jnp.min(jnp.where(el2 == v2, lane, big), -1, keepdims=True)
    e2 = jnp.exp(v2 - v1)
    w1 = g_val / (1.0 + e2)
    w2 = g_val * e2 / (1.0 + e2)
    return jnp.where(lane == i1, w1, 0.0) + jnp.where(lane == i2, w2, 0.0), g_idx


def _moe_dense_kernel(x_ref, wr_ref, br_ref, wg_ref, wu_ref, wd_ref, g_ref, b_ref, o_ref,
                      gate_scr, acc_scr):
    e = pl.program_id(1)

    @pl.when(e == 0)
    def _():
        logits = _bdot(x_ref[...], wr_ref[...]) + br_ref[...]
        gate_scr[...] = _route(logits)[0]
        acc_scr[...] = jnp.zeros_like(acc_scr)

    xb = x_ref[...].astype(BF16)
    lane = lax.broadcasted_iota(jnp.int32, gate_scr.shape, 1)
    gcol = jnp.sum(jnp.where(lane == e + N_GROUPS, gate_scr[...], 0.0), -1, keepdims=True)
    hid = _silu(_bdot(xb, wg_ref[...])) * _bdot(xb, wu_ref[...])
    acc_scr[...] += _bdot(hid * gcol, wd_ref[...])

    @pl.when(e == pl.num_programs(1) - 1)
    def _():
        o_ref[...] = _layer_norm(ALPHA * x_ref[...] + acc_scr[...], g_ref[...], b_ref[...])


def moe_dense(x, wr, br, wg, wu, wd, g, b, layer, tm):
    m = x.shape[0]
    return pl.pallas_call(
        _moe_dense_kernel,
        grid=(m // tm, N_EXPERTS),
        in_specs=[pl.BlockSpec((tm, D_MODEL), lambda i, e: (i, 0)),
                  pl.BlockSpec((D_MODEL, ROUTER_LANES), lambda i, e: (0, 0)),
                  pl.BlockSpec((1, ROUTER_LANES), lambda i, e: (0, 0)),
                  pl.BlockSpec((None, None, D_MODEL, EXP_FF), lambda i, e: (layer, e, 0, 0)),
                  pl.BlockSpec((None, None, D_MODEL, EXP_FF), lambda i, e: (layer, e, 0, 0)),
                  pl.BlockSpec((None, None, EXP_FF, D_MODEL), lambda i, e: (layer, e, 0, 0)),
                  pl.BlockSpec((1, D_MODEL), lambda i, e: (0, 0)),
                  pl.BlockSpec((1, D_MODEL), lambda i, e: (0, 0))],
        out_specs=pl.BlockSpec((tm, D_MODEL), lambda i, e: (i, 0)),
        out_shape=jax.ShapeDtypeStruct((m, D_MODEL), F32),
        scratch_shapes=[pltpu.VMEM((tm, ROUTER_LANES), F32), pltpu.VMEM((tm, D_MODEL), F32)],
        compiler_params=_cparams(("parallel", "arbitrary")),
        name="moe_dense",
    )(x, wr, br, wg, wu, wd, g, b)


def _sc_mesh():
    return plsc.VectorSubcoreMesh(core_axis_name="core", subcore_axis_name="subcore")


def sc_scatter_rows(x, idx, n_out):
    r = x.shape[0]

    @functools.partial(pl.kernel, out_type=jax.ShapeDtypeStruct((n_out, LANES), x.dtype),
                       mesh=_sc_mesh(), scratch_types=[], name="sc_scatter_rows")
    def k(x_hbm, i_hbm, o_hbm):
        def body(x_vmem, i_vmem):
            pltpu.sync_copy(x_vmem, o_hbm.at[i_vmem.at[0]])

        pltpu.emit_pipeline(
            body,
            grid=(r // SC_WINDOW,),
            in_specs=[pl.BlockSpec((SC_WINDOW, LANES), lambda i: (i, 0)),
                      pl.BlockSpec((1, SC_WINDOW), lambda i: (0, i))],
            out_specs=[],
            core_axis_name=("core", "subcore"),
            dimension_semantics=(pltpu.PARALLEL,),
        )(x_hbm, i_hbm)

    return k(x, idx.reshape(1, r))


def sc_gather_rows(table, idx):
    r = idx.shape[0]

    @functools.partial(pl.kernel, out_type=jax.ShapeDtypeStruct((r, LANES), table.dtype),
                       mesh=_sc_mesh(), scratch_types=[], name="sc_gather_rows")
    def k(t_hbm, i_hbm, o_hbm):
        def body(i_vmem, o_vmem):
            pltpu.sync_copy(t_hbm.at[i_vmem.at[0]], o_vmem)

        pltpu.emit_pipeline(
            body,
            grid=(r // SC_WINDOW,),
            in_specs=[pl.BlockSpec((1, SC_WINDOW), lambda i: (0, i))],
            out_specs=[pl.BlockSpec((SC_WINDOW, LANES), lambda i: (i, 0))],
            core_axis_name=("core", "subcore"),
            dimension_semantics=(pltpu.PARALLEL,),
        )(i_hbm, o_hbm)

    return k(table, idx.reshape(1, r))


def _moe_sorted_kernel(tg_ref, nused_ref, xs_ref, wg32_ref, wu32_ref, wd32_ref, g_ref, b_ref, o_ref,
                       wg_ref, wu_ref, wd_ref):
    t = pl.program_id(0)
    used = t < nused_ref[0]

    @pl.when(used & ((t == 0) | (tg_ref[t] != tg_ref[jnp.maximum(t - 1, 0)])))
    def _():
        wg_ref[...] = wg32_ref[...].astype(BF16)
        wu_ref[...] = wu32_ref[...].astype(BF16)
        wd_ref[...] = wd32_ref[...].astype(BF16)

    @pl.when(used)
    def _():
        x = jnp.concatenate([xs_ref[j] for j in range(X_PLANES)], axis=1)
        gate = xs_ref[X_PLANES]
        xb = x.astype(BF16)
        lane = lax.broadcasted_iota(jnp.int32, gate.shape, 1)
        first = N_GROUPS + EXP_PER_GROUP * tg_ref[t]
        acc = jnp.zeros(x.shape, F32)
        for e in range(EXP_PER_GROUP):
            gcol = jnp.sum(jnp.where(lane == first + e, gate, 0.0), -1, keepdims=True)
            hid = _silu(_bdot(xb, wg_ref[e])) * _bdot(xb, wu_ref[e])
            acc = acc + _bdot(hid * gcol, wd_ref[e])
        y = _layer_norm(ALPHA * x + acc, g_ref[...], b_ref[...])
        for j in range(X_PLANES):
            o_ref[j] = y[:, j * LANES:(j + 1) * LANES]


def _group_slots(group_idx, n, tm):
    n_tiles = n // tm + N_GROUPS
    onehot = (group_idx[:, None] == jnp.arange(N_GROUPS)[None, :]).astype(jnp.int32)
    csum = jnp.cumsum(onehot, axis=0)
    counts = csum[-1]
    rank = jnp.sum(onehot * csum, axis=1) - 1
    tiles_g = (counts + tm - 1) // tm
    tile_end = jnp.cumsum(tiles_g)
    slot_base = (tile_end - tiles_g) * tm
    slot = (jnp.sum(onehot * slot_base[None, :], axis=1) + rank).astype(jnp.int32)
    tile_group = jnp.sum((jnp.arange(n_tiles)[:, None] >= tile_end[None, :]).astype(jnp.int32), axis=1)
    tile_group = jnp.minimum(tile_group, N_GROUPS - 1).astype(jnp.int32)
    return slot, tile_group, tile_end[-1:].astype(jnp.int32)


def moe_routed_sc(xg, wg, wu, wd, g, b, layer, n, tm, during_scatter, during_gather, tiles_out):
    slot, tile_group, n_used = _group_slots(xg[X_PLANES, :, 0].astype(jnp.int32), n, tm)
    n_tiles = tile_group.shape[0]
    n_slots = n_tiles * tm
    plane_base = lambda planes: jnp.arange(planes, dtype=jnp.int32)[:, None] * n_slots
    idx_in = (plane_base(XG_PLANES) + slot[None, :]).reshape(-1)
    if tiles_out:
        idx_out = (plane_base(X_PLANES)[None] + slot.reshape(n // SUBLANES, 1, SUBLANES)).reshape(-1)
    else:
        idx_out = (plane_base(X_PLANES) + slot[None, :]).reshape(-1)
    xs = sc_scatter_rows(xg.reshape(XG_PLANES * n, LANES), idx_in, XG_PLANES * n_slots)
    xs = xs.reshape(XG_PLANES, n_slots, LANES)
    xs, side = lax.optimization_barrier((xs, during_scatter()))
    wspec = lambda shp: pl.BlockSpec((None, None, EXP_PER_GROUP) + shp,
                                     lambda t, tg, nu: (layer, tg[t], 0, 0, 0))
    grid_spec = pltpu.PrefetchScalarGridSpec(
        num_scalar_prefetch=2,
        grid=(n_tiles,),
        in_specs=[pl.BlockSpec((XG_PLANES, tm, LANES), lambda t, tg, nu: (0, t, 0)),
                  wspec((D_MODEL, EXP_FF)), wspec((D_MODEL, EXP_FF)), wspec((EXP_FF, D_MODEL)),
                  pl.BlockSpec((1, D_MODEL), lambda t, tg, nu: (0, 0)),
                  pl.BlockSpec((1, D_MODEL), lambda t, tg, nu: (0, 0))],
        out_specs=pl.BlockSpec((X_PLANES, tm, LANES), lambda t, tg, nu: (0, t, 0)),
        scratch_shapes=[pltpu.VMEM((EXP_PER_GROUP, D_MODEL, EXP_FF), BF16),
                        pltpu.VMEM((EXP_PER_GROUP, D_MODEL, EXP_FF), BF16),
                        pltpu.VMEM((EXP_PER_GROUP, EXP_FF, D_MODEL), BF16)])
    grouped = lambda w: w.reshape(w.shape[0], N_GROUPS, EXP_PER_GROUP, w.shape[2], w.shape[3])
    ys = pl.pallas_call(
        _moe_sorted_kernel,
        grid_spec=grid_spec,
        out_shape=jax.ShapeDtypeStruct((X_PLANES, n_slots, LANES), F32),
        compiler_params=_cparams(("arbitrary",)),
        name="moe_sorted",
    )(tile_group, n_used, xs, grouped(wg), grouped(wu), grouped(wd), g, b)
    y = sc_gather_rows(ys.reshape(X_PLANES * n_slots, LANES), idx_out)
    y, side = lax.optimization_barrier((y, during_gather(side)))
    if tiles_out:
        y = y.reshape(n // SUBLANES, X_PLANES, SUBLANES, LANES).transpose(0, 2, 1, 3)
        return y.reshape(n, D_MODEL), side
    return y.reshape(X_PLANES, n, LANES), side


def _t5_bucket(dist):
    max_exact = N_BUCKETS // 2
    d = jnp.maximum(dist, 0)
    df = jnp.maximum(d, 1).astype(F32)
    log_b = max_exact + (jnp.log(df / max_exact) / math.log(MAX_DISTANCE / max_exact)
                         * (N_BUCKETS - max_exact)).astype(jnp.int32)
    return jnp.where(d < max_exact, d, jnp.minimum(log_b, N_BUCKETS - 1))


def _bucket_lookup(rel_bias, bucket):
    out = jnp.zeros(bucket.shape + (rel_bias.shape[1],), F32)
    for i in range(N_BUCKETS):
        out = jnp.where((bucket == i)[..., None], rel_bias[i].astype(F32), out)
    return out


def _prompt_bias(rel_bias):
    qi = jnp.arange(WINDOW)[:, None]
    kj = jnp.arange(2 * WINDOW)[None, :]
    dist = qi + WINDOW - kj
    bias = _bucket_lookup(rel_bias, _t5_bucket(dist)).transpose(2, 0, 1)
    valid = (dist >= 0) & (dist <= WINDOW)
    return jnp.where(valid[None], bias, NEG)


def _decode_table(rel_bias, attn_sink):
    dist = WINDOW - jnp.arange(WINDOW + 1)
    bias = _bucket_lookup(rel_bias, _t5_bucket(dist)).T
    depth = attn_sink.shape[0]
    wide = lambda v: jnp.broadcast_to(v[..., None], v.shape + (WINDOW,))
    per_layer = lambda t: jnp.broadcast_to(t[None], (depth,) + t.shape)
    return jnp.stack([per_layer(bias[:, :WINDOW]), per_layer(wide(bias[:, WINDOW])),
                      wide(attn_sink.astype(F32))], axis=1)


def _block_ones(width):
    idx = jnp.arange(width) // HEAD_DIM
    return (idx[:, None] == idx[None, :])


def _head_rows_mask():
    head = jnp.arange(HG_WIDTH)[None, :] // HEAD_DIM
    return (head == jnp.arange(HG_MROWS)[:, None]).astype(F32)


def _block_diag(w):
    nblk, s, _ = w.shape
    eye = jnp.eye(nblk, dtype=w.dtype)
    return (eye[:, None, :, None] * w[:, :, None, :]).reshape(nblk * s, nblk * s)


def kernel(x_prompt, x_sample, mem_prompt, cache_win_k, cache_win_v, state_hgrn, state_conv, state_lru, cache_mem_k, cache_mem_v, rel_bias, hg_lb, w_in, attn_sink, hg_gain, conv_w, conv_b, lru_wa, lru_ba, lru_wx, lru_bx, lru_lam, w_out, ln1_g, ln1_b, x_wq, x_wk, x_wv, x_wo, ln2_g, ln2_b, r_gw, r_gb, r_ew, r_eb, e_wg, e_wu, e_wd, ln3_g, ln3_b):
    bp, seq, d = x_prompt.shape
    n_tok = bp * seq
    nd = x_sample.shape[0]
    depth = w_in.shape[0]

    lbs = jnp.cumsum(jax.nn.softmax(hg_lb.astype(F32), axis=0), axis=0)
    lbs = lbs - lbs[0]
    loglb = jnp.log(lbs)
    log1mlb = jnp.log1p(-lbs)
    gain4 = jnp.tile(hg_gain, (1, HG_HEADS))

    bias_p = _prompt_bias(rel_bias)
    bdm256 = _block_ones(HG_WIDTH).astype(F32)
    hmask = _head_rows_mask()

    w_in_b = w_in.astype(BF16)
    w_out_b = w_out.astype(BF16)
    wq_b = x_wq.astype(BF16)
    wkv_b = jnp.concatenate([x_wk, x_wv], axis=-1).astype(BF16)
    wo_b = x_wo.astype(BF16)
    rew = r_ew.transpose(0, 2, 1, 3).reshape(depth, d, N_EXPERTS)
    wr = jnp.concatenate([r_gw, rew, jnp.zeros((depth, d, ROUTER_LANES - N_GROUPS - N_EXPERTS), F32)], -1)
    br = jnp.concatenate([r_gb, r_eb.reshape(depth, N_EXPERTS),
                          jnp.zeros((depth, ROUTER_LANES - N_GROUPS - N_EXPERTS), F32)], -1)

    a_w = A_HEADS * HEAD_DIM
    xp = x_prompt.reshape(bp * seq, d)
    xs = x_sample.reshape(nd, d)
    mem = mem_prompt.reshape(bp * N_MEM, d)
    ckt = cache_win_k.transpose(0, 1, 3, 4, 2).reshape(depth, nd, LANES, WINDOW)
    cvt = cache_win_v.transpose(0, 1, 3, 4, 2).reshape(depth, nd, LANES, WINDOW)
    cmkt = cache_mem_k.transpose(0, 1, 3, 4, 2).reshape(depth, nd, X_WIDTH, N_MEM)
    cmvt = cache_mem_v.transpose(0, 1, 3, 4, 2).reshape(depth, nd, X_WIDTH, N_MEM)
    state_t = state_hgrn.transpose(0, 2, 3, 4, 1).reshape(depth, HG_HEADS * HEAD_DIM * HEAD_DIM, nd)
    dec_tab = _decode_table(rel_bias, attn_sink)
    xq_mask = _head_rows_mask()[:X_QROWS]
    head_group = jnp.arange(A_HEADS) // A_REP

    p_wk, p_wv, p_s, p_cb, p_h, p_mk, p_mv = [], [], [], [], [], [], []
    s_cb, s_h = [], []
    for l in range(depth):
        row = lambda v: v[l].reshape(1, -1)
        wa_bd = _block_diag(lru_wa[l]).astype(BF16)
        wx_bd = _block_diag(lru_wx[l]).astype(BF16)
        lru_args = (conv_w[l], row(conv_b), wa_bd, row(lru_ba), wx_bd, row(lru_bx), row(lru_lam))
        wo_parts = [w_out_b[l, :a_w], w_out_b[l, a_w:a_w + HG_WIDTH], w_out_b[l, a_w + HG_WIDTH:]]

        proj = matmul(xp, w_in_b[l], ROW_TILE, IN_COLS)
        oa = attn_prompt(proj, attn_sink[l], bias_p, bp, seq)
        ob, st = hgrn_prompt(proj, row(loglb), row(log1mlb), row(gain4), bdm256, hmask, bp, seq, ROW_TILE)
        oc, hl = lru_prompt(proj, *lru_args, bp, seq, LRU_TILE)
        xp = proj_res_ln(xp, [oa, ob, oc], wo_parts, row(ln1_g), row(ln1_b), ROW_TILE)
        mkv = matmul(mem, wkv_b[l], N_MEM, 2 * X_WIDTH)
        xg = xattn_prompt(xp, wq_b[l], mkv, wo_b[l], row(ln2_g), row(ln2_b), wr[l], br[l:l + 1],
                          bp, seq, ROW_TILE)
        def decode_mixers(xs=xs, ckt=ckt, cvt=cvt, state_t=state_t, l=l, lru_args=lru_args,
                          wo_parts=wo_parts, row=row):
            projd = matmul(xs, w_in_b[l], nd, IN_COLS)
            q3 = projd[:, :a_w].reshape(nd, A_HEADS, 1, HEAD_DIM)
            on_group = head_group[None, :, None, None] == jnp.arange(A_KV_HEADS)[None, None, :, None]
            qblk = jnp.where(on_group, q3, 0.0).reshape(nd, A_HEADS, LANES)
            o3, ckt, cvt = attn_decode(qblk, projd, dec_tab[l], ckt, cvt, l, DEC_BLOCK)
            o4 = o3.reshape(nd, A_HEADS, A_KV_HEADS, HEAD_DIM)
            oa = jnp.sum(jnp.where(on_group, o4, 0.0), axis=2).reshape(nd, a_w)
            gates_t = projd[:, a_w + 2 * LANES:a_w + 2 * LANES + 4 * HG_WIDTH].T
            bc = lambda v: jnp.broadcast_to(v[:, None], (v.shape[0], nd))
            ob_t, state_t = hgrn_decode(gates_t, bc(loglb[l]), bc(log1mlb[l]), bc(hg_gain[l]), state_t, l)
            oc, nh, nbuf = lru_decode(projd, state_conv[l].reshape(nd, -1), state_lru[l], *lru_args)
            xs = proj_res_ln(xs, [oa, ob_t.T, oc], wo_parts, row(ln1_g), row(ln1_b), nd)
            return xs, ckt, cvt, state_t, nh, nbuf

        def decode_rest(side, l=l, row=row):
            xs = side[0]
            qd = matmul(xs, wq_b[l], nd, X_WIDTH)
            qdb = qd[:, None, :] * xq_mask[None, :, :]
            od = xattn_decode(qdb, cmkt, cmvt, xq_mask, l, DEC_BLOCK)
            xs = proj_res_ln(xs, [od], [wo_b[l]], row(ln2_g), row(ln2_b), nd)
            xs = moe_dense(xs, wr[l], br[l:l + 1], e_wg, e_wu, e_wd, row(ln3_g), row(ln3_b), l, nd)
            return (xs,) + tuple(side[1:])

        xp, (xs, ckt, cvt, state_t, nh, nbuf) = moe_routed_sc(
            xg, e_wg, e_wu, e_wd, row(ln3_g), row(ln3_b), l, n_tok, MOE_TM, decode_mixers, decode_rest,
            tiles_out=(l == depth - 1))

        proj3 = proj.reshape(bp, seq, IN_COLS)
        p_wk.append(proj3[:, seq - WINDOW:, a_w:a_w + LANES].reshape(bp, WINDOW, A_KV_HEADS, HEAD_DIM))
        p_wv.append(proj3[:, seq - WINDOW:, a_w + LANES:a_w + 2 * LANES].reshape(bp, WINDOW, A_KV_HEADS, HEAD_DIM))
        st5 = st.reshape(bp, HG_HEADS, HEAD_DIM, HG_HEADS, HEAD_DIM)
        p_s.append(jnp.stack([st5[:, h, :, h, :] for h in range(HG_HEADS)], 1).transpose(0, 1, 3, 2))
        p_cb.append(proj3[:, seq - (CONV_W - 1):, IN_COLS - 2 * LRU_WIDTH:IN_COLS - LRU_WIDTH])
        p_h.append(hl.reshape(bp, LRU_WIDTH))
        p_mk.append(mkv[:, :X_WIDTH].reshape(bp, N_MEM, X_HEADS, HEAD_DIM))
        p_mv.append(mkv[:, X_WIDTH:].reshape(bp, N_MEM, X_HEADS, HEAD_DIM))

        s_cb.append(nbuf.reshape(nd, CONV_W - 1, LRU_WIDTH))
        s_h.append(nh)

    unkey = lambda c: c.reshape(depth, nd, A_KV_HEADS, HEAD_DIM, WINDOW).transpose(0, 1, 4, 2, 3)
    s_s = state_t.reshape(depth, HG_HEADS, HEAD_DIM, HEAD_DIM, nd).transpose(0, 4, 1, 2, 3)
    return (xp.reshape(bp, seq, d), xs.reshape(nd, 1, d),
            jnp.stack(p_wk), jnp.stack(p_wv), jnp.stack(p_s), jnp.stack(p_cb), jnp.stack(p_h),
            jnp.stack(p_mk), jnp.stack(p_mv),
            unkey(ckt), unkey(cvt), s_s, jnp.stack(s_cb), jnp.stack(s_h))
```

```python
import functools
import math

import jax
import jax.numpy as jnp
from jax import lax
from jax.experimental import pallas as pl
from jax.experimental.pallas import tpu as pltpu
from jax.experimental.pallas import tpu_sc as plsc

F32 = jnp.float32
BF16 = jnp.bfloat16
MIX_DTYPE = BF16

D_MODEL = 1024
DEPTH = 4
HEAD_DIM = 64
A_HEADS = 8
A_KV_HEADS = 2
A_REP = A_HEADS // A_KV_HEADS
WINDOW = 128
A_QB = 2
N_BUCKETS = 32
MAX_DISTANCE = 128
HG_WIDTH = 256
HG_HEADS = 4
HG_CHUNK = 64
HG_TB = 32
HG_MROWS = 8
LOG2E = math.log2(math.e)
LRU_WIDTH = 256
LRU_BLOCKS = 4
CONV_W = 4
LRU_C = 8.0
N_MEM = 256
X_HEADS = 4
X_WIDTH = X_HEADS * HEAD_DIM
X_QROWS = 8
N_GROUPS = 4
EXP_PER_GROUP = 4
N_EXPERTS = N_GROUPS * EXP_PER_GROUP
EXP_FF = D_MODEL // 4
ALPHA = (2 * DEPTH) ** 0.25
LN_EPS = 1e-5
RMS_EPS = 1e-6
IN_COLS = 2304
SCALE = HEAD_DIM ** -0.5
NEG = -1e30
LANES = 128
SUBLANES = 8
ROUTER_LANES = 128
ROUTER_ROWS = 32
XG_WIDTH = D_MODEL + ROUTER_LANES
XG_PLANES = XG_WIDTH // LANES
X_PLANES = D_MODEL // LANES
MOE_TM = 512
ROW_TILE = 1024
LRU_TILE = 512
DEC_BLOCK = 16
SC_WINDOW = 128
VMEM_LIMIT = 48 * 1024 * 1024


def _cparams(sem):
    return pltpu.CompilerParams(dimension_semantics=sem, vmem_limit_bytes=VMEM_LIMIT)


def _bdot(a, b):
    return jnp.dot(a.astype(BF16), b.astype(BF16), preferred_element_type=F32)


def _bdot_nt(a, b):
    return lax.dot_general(a.astype(BF16), b.astype(BF16), (((1,), (1,)), ((), ())),
                           preferred_element_type=F32)


def _bdot_tn(a, b):
    return lax.dot_general(a.astype(BF16), b.astype(BF16), (((0,), (0,)), ((), ())),
                           preferred_element_type=F32)


def _rb(x):
    return x.astype(BF16).astype(F32)


def _silu(x):
    return x * jax.nn.sigmoid(x)


def _neg_expm1(x):
    return -jnp.tanh(0.5 * x) * (jnp.exp(x) + 1.0)


def _softplus(x):
    return jnp.maximum(x, 0.0) + jnp.log1p(jnp.exp(-jnp.abs(x)))


def _gelu_tanh(x):
    return 0.5 * x * (1.0 + jnp.tanh(math.sqrt(2.0 / math.pi) * (x + 0.044715 * (x * x * x))))


def _layer_norm(y, g, b):
    mu = jnp.mean(y, -1, keepdims=True)
    yc = y - mu
    var = jnp.mean(yc * yc, -1, keepdims=True)
    return yc * lax.rsqrt(var + LN_EPS) * g + b


def _rows(x_ref):
    if len(x_ref.shape) == 2:
        return x_ref[...]
    return jnp.concatenate([x_ref[j] for j in range(x_ref.shape[0])], axis=1)


def _rows_spec(x, tm, nargs):
    if x.ndim == 2:
        return pl.BlockSpec((tm, x.shape[1]), (lambda i: (i, 0)) if nargs == 1 else (lambda i, j: (i, 0)))
    blk = (x.shape[0], tm, LANES)
    return pl.BlockSpec(blk, (lambda i: (0, i, 0)) if nargs == 1 else (lambda i, j: (0, i, 0)))


def _mm_kernel(x_ref, w_ref, o_ref):
    o_ref[...] = _bdot(_rows(x_ref), w_ref[...])


def matmul(x, w, tm, tn):
    m = x.shape[-2]
    k, n = w.shape
    return pl.pallas_call(
        _mm_kernel,
        grid=(m // tm, n // tn),
        in_specs=[_rows_spec(x, tm, 2),
                  pl.BlockSpec((k, tn), lambda i, j: (0, j))],
        out_specs=pl.BlockSpec((tm, tn), lambda i, j: (i, j)),
        out_shape=jax.ShapeDtypeStruct((m, n), F32),
        compiler_params=_cparams(("parallel", "parallel")),
        name="matmul",
    )(x, w)


def _proj_res_ln_kernel(n_in, x_ref, *refs):
    a_refs = refs[:n_in]
    w_refs = refs[n_in:2 * n_in]
    g_ref, b_ref, o_ref = refs[2 * n_in:]
    y = ALPHA * _rows(x_ref)
    for a_ref, w_ref in zip(a_refs, w_refs):
        y = y + _bdot(a_ref[...], w_ref[...])
    o_ref[...] = _layer_norm(y, g_ref[...], b_ref[...])


def proj_res_ln(x, a_list, w_list, g, b, tm):
    m = x.shape[-2]
    d = w_list[0].shape[1]
    n_in = len(a_list)
    in_specs = [_rows_spec(x, tm, 1)]
    in_specs += [pl.BlockSpec((tm, a.shape[1]), lambda i: (i, 0)) for a in a_list]
    in_specs += [pl.BlockSpec(w.shape, lambda i: (0, 0)) for w in w_list]
    in_specs += [pl.BlockSpec((1, d), lambda i: (0, 0))] * 2
    return pl.pallas_call(
        functools.partial(_proj_res_ln_kernel, n_in),
        grid=(m // tm,),
        in_specs=in_specs,
        out_specs=pl.BlockSpec((tm, d), lambda i: (i, 0)),
        out_shape=jax.ShapeDtypeStruct((m, d), F32),
        compiler_params=_cparams(("parallel",)),
        name="proj_res_ln",
    )(x, *a_list, *w_list, g, b)


def _attn_prompt_kernel(sink_ref, q_ref, kc_ref, kp_ref, vc_ref, vp_ref, bias_ref, o_ref):
    n = pl.program_id(1)
    col = lax.broadcasted_iota(jnp.int32, (WINDOW, 2 * WINDOW), 1)
    first = jnp.where((n == 0) & (col < WINDOW), NEG, 0.0)
    kk = jnp.concatenate([kp_ref[...], kc_ref[...]], axis=0).astype(BF16)
    vv = jnp.concatenate([vp_ref[...], vc_ref[...]], axis=0).astype(BF16)
    q = q_ref[...].astype(BF16)
    for u in range(A_QB):
        rows = slice(u * WINDOW, (u + 1) * WINDOW)
        keys = slice(u * WINDOW, (u + 2) * WINDOW)
        outs = []
        for h in range(A_HEADS):
            g = h // A_REP
            qh = q[rows, h * HEAD_DIM:(h + 1) * HEAD_DIM]
            kg = kk[keys, g * HEAD_DIM:(g + 1) * HEAD_DIM]
            vg = vv[keys, g * HEAD_DIM:(g + 1) * HEAD_DIM]
            s = _bdot_nt(qh, kg) * SCALE + bias_ref[h]
            if u == 0:
                s = s + first
            sink = sink_ref[h]
            m = jnp.maximum(jnp.max(s, -1, keepdims=True), sink)
            p = jnp.exp(s - m)
            den = jnp.sum(p, -1, keepdims=True) + jnp.exp(sink - m)
            outs.append(_bdot(p / den, vg))
        o_ref[rows, :] = jnp.concatenate(outs, axis=1).astype(o_ref.dtype)


def attn_prompt(proj, sink, bias, batch, seq):
    nb = seq // WINDOW
    ns = nb // A_QB
    tq = A_QB * WINDOW
    qcol = 0
    kcol = (A_HEADS * HEAD_DIM) // LANES
    vcol = kcol + 1

    def cur(c):
        return lambda b, n: (b * ns + n, c)

    def prev(c):
        return lambda b, n: (b * nb + jnp.maximum(n * A_QB - 1, 0), c)

    return pl.pallas_call(
        _attn_prompt_kernel,
        grid=(batch, ns),
        in_specs=[pl.BlockSpec(memory_space=pltpu.SMEM),
                  pl.BlockSpec((tq, A_HEADS * HEAD_DIM), cur(qcol)),
                  pl.BlockSpec((tq, LANES), cur(kcol)),
                  pl.BlockSpec((WINDOW, LANES), prev(kcol)),
                  pl.BlockSpec((tq, LANES), cur(vcol)),
                  pl.BlockSpec((WINDOW, LANES), prev(vcol)),
                  pl.BlockSpec((A_HEADS, WINDOW, 2 * WINDOW), lambda b, n: (0, 0, 0))],
        out_specs=pl.BlockSpec((tq, A_HEADS * HEAD_DIM), cur(0)),
        out_shape=jax.ShapeDtypeStruct((batch * seq, A_HEADS * HEAD_DIM), MIX_DTYPE),
        compiler_params=_cparams(("parallel", "parallel")),
        name="attn_prompt",
    )(sink, proj, proj, proj, proj, proj, bias)


def _attn_decode_kernel(qb_ref, kn_ref, vn_ref, knt_ref, vnt_ref, ck_ref, cv_ref, tab_ref,
                        o_ref, ok_ref, ov_ref):
    bb = qb_ref.shape[0]
    ck = ck_ref[...]
    cv = cv_ref[...]
    qb = qb_ref[...]
    kn = kn_ref[...]
    vn = vn_ref[...]
    bias_j = tab_ref[0]
    bias_n = tab_ref[1][:, 0:1]
    sink = tab_ref[2][:, 0:1]
    s = lax.dot_general(qb.astype(BF16), ck.astype(BF16), (((2,), (1,)), ((0,), (0,))),
                        preferred_element_type=F32) * SCALE + bias_j[None]
    sn = jnp.sum(_rb(qb) * _rb(kn)[:, None, :], -1, keepdims=True) * SCALE + bias_n[None]
    m = jnp.maximum(jnp.maximum(jnp.max(s, -1, keepdims=True), sn), sink[None])
    p = jnp.exp(s - m)
    pn = jnp.exp(sn - m)
    den = jnp.sum(p, -1, keepdims=True) + pn + jnp.exp(sink[None] - m)
    o = lax.dot_general((p / den).astype(BF16), cv.astype(BF16), (((2,), (2,)), ((0,), (0,))),
                        preferred_element_type=F32)
    o_ref[...] = o + _rb(pn / den) * _rb(vn)[:, None, :]
    lane = lax.broadcasted_iota(jnp.int32, (LANES, LANES), 1)
    for b in range(bb):
        ok_ref[b] = jnp.where(lane == WINDOW - 1, knt_ref[:, b:b + 1], pltpu.roll(ck[b], WINDOW - 1, 1))
        ov_ref[b] = jnp.where(lane == WINDOW - 1, vnt_ref[:, b:b + 1], pltpu.roll(cv[b], WINDOW - 1, 1))


def attn_decode(qblk, proj_d, table, cache_k, cache_v, layer, bb):
    nbatch = proj_d.shape[0]
    a_w = A_HEADS * HEAD_DIM
    cols = lambda c: proj_d[:, c:c + LANES].reshape(nbatch // bb, bb, LANES).transpose(0, 2, 1)
    knt, vnt = cols(a_w), cols(a_w + LANES)
    kcol = (A_HEADS * HEAD_DIM) // LANES
    cache_spec = pl.BlockSpec((None, bb, LANES, WINDOW), lambda i: (layer, i, 0, 0))
    col_spec = pl.BlockSpec((None, LANES, bb), lambda i: (i, 0, 0))
    return pl.pallas_call(
        _attn_decode_kernel,
        grid=(nbatch // bb,),
        in_specs=[pl.BlockSpec((bb, A_HEADS, LANES), lambda i: (i, 0, 0)),
                  pl.BlockSpec((bb, LANES), lambda i: (i, kcol)),
                  pl.BlockSpec((bb, LANES), lambda i: (i, kcol + 1)),
                  col_spec, col_spec, cache_spec, cache_spec,
                  pl.BlockSpec((3, A_HEADS, WINDOW), lambda i: (0, 0, 0))],
        out_specs=[pl.BlockSpec((bb, A_HEADS, LANES), lambda i: (i, 0, 0)), cache_spec, cache_spec],
        out_shape=[jax.ShapeDtypeStruct((nbatch, A_HEADS, LANES), F32),
                   jax.ShapeDtypeStruct(cache_k.shape, F32),
                   jax.ShapeDtypeStruct(cache_v.shape, F32)],
        input_output_aliases={5: 1, 6: 2},
        compiler_params=_cparams(("arbitrary",)),
        name="attn_decode",
    )(qblk, proj_d, proj_d, knt, vnt, cache_k, cache_v, table)


def _xattn_decode_kernel(qb_ref, mk_ref, mv_ref, hm_ref, o_ref):
    qb = qb_ref[...]
    s = lax.dot_general(qb.astype(BF16), mk_ref[...].astype(BF16), (((2,), (1,)), ((0,), (0,))),
                        preferred_element_type=F32) * SCALE
    m = jnp.max(s, -1, keepdims=True)
    p = jnp.exp(s - m)
    p = p / jnp.sum(p, -1, keepdims=True)
    o = lax.dot_general(p.astype(BF16), mv_ref[...].astype(BF16), (((2,), (2,)), ((0,), (0,))),
                        preferred_element_type=F32)
    o_ref[...] = jnp.sum(o * hm_ref[...][None], axis=1)


def xattn_decode(qblk, mem_k, mem_v, hmask, layer, bb):
    nbatch = qblk.shape[0]
    mem_spec = pl.BlockSpec((None, bb, X_WIDTH, N_MEM), lambda i: (layer, i, 0, 0))
    return pl.pallas_call(
        _xattn_decode_kernel,
        grid=(nbatch // bb,),
        in_specs=[pl.BlockSpec((bb, X_QROWS, X_WIDTH), lambda i: (i, 0, 0)), mem_spec, mem_spec,
                  pl.BlockSpec((X_QROWS, X_WIDTH), lambda i: (0, 0))],
        out_specs=pl.BlockSpec((bb, X_WIDTH), lambda i: (i, 0)),
        out_shape=jax.ShapeDtypeStruct((nbatch, X_WIDTH), F32),
        compiler_params=_cparams(("parallel",)),
        name="xattn_decode",
    )(qblk, mem_k, mem_v, hmask)


def _hgrn_gates(hq, hf, loglb, log1mlb):
    ls = jnp.minimum(hf, 0.0) - jnp.log1p(jnp.exp(-jnp.abs(hf)))
    b = log1mlb + ls
    lf = jnp.maximum(loglb, b) + jnp.log1p(jnp.exp(-jnp.abs(loglb - b)))
    return _silu(hq), lf, _neg_expm1(lf)


def _hgrn_prompt_kernel(hq_ref, hf_ref, hi_ref, hg_ref, loglb_ref, log1mlb_ref, gain_ref,
                        bdm_ref, hm_ref, ob_ref, st_ref, st_scr, q_scr, k_scr, cum_scr, o_scr):
    i = pl.program_id(1)
    tt = hq_ref.shape[0]
    c = HG_CHUNK
    tb = HG_TB

    @pl.when(i == 0)
    def _():
        st_scr[...] = jnp.zeros_like(st_scr)

    qs, lf, kk = _hgrn_gates(hq_ref[...], hf_ref[...], loglb_ref[...], log1mlb_ref[...])
    row = lax.broadcasted_iota(jnp.int32, (tt, HG_WIDTH), 0) & (c - 1)
    cum = lf
    sh = 1
    while sh < c:
        cum = cum + jnp.where(row >= sh, pltpu.roll(cum, sh, 0), 0.0)
        sh *= 2
    q_scr[...] = qs
    k_scr[...] = kk
    cum_scr[...] = cum

    bdm = bdm_ref[...]
    hmask = hm_ref[...]
    def chunk(ci, carry):
        r0 = pl.multiple_of(ci * c, c)
        r = pl.ds(r0, c)
        cu = cum_scr[r, :]
        q = q_scr[r, :]
        k = k_scr[r, :]
        v = hi_ref[r, :]
        vb = v.astype(BF16)
        qb = _rb(q)
        cu2 = cu * LOG2E
        last = cu[c - 1:c, :]
        st = st_scr[...]
        o_inter = _bdot_nt(q * jnp.exp(cu), st)
        for j in range(c // tb):
            ns = tb * (j + 1)
            ti = lax.broadcasted_iota(jnp.int32, (tb, ns, HG_WIDTH), 0) + tb * j
            si = lax.broadcasted_iota(jnp.int32, (tb, ns, HG_WIDTH), 1)
            cut = cu2[tb * j:tb * (j + 1), :]
            dec = jnp.exp2(jnp.where(ti >= si, cut[:, None, :] - cu2[None, :ns, :], NEG))
            a2 = (dec * k[None, :ns, :]).astype(BF16)
            q4 = qb[tb * j:tb * (j + 1), None, :] * hmask[None, :, :]
            att = lax.dot_general(q4, a2, (((2,), (2,)), ((0,), (0,))),
                                  preferred_element_type=F32)
            w = jnp.dot(_rb(att.reshape(tb * HG_MROWS, ns)), vb[:ns, :],
                        preferred_element_type=F32).reshape(tb, HG_MROWS, HG_WIDTH)
            o_intra = jnp.sum(w * hmask[None, :, :], axis=1)
            o_scr[pl.ds(r0 + tb * j, tb), :] = o_intra + o_inter[tb * j:tb * (j + 1), :]
        upd = _bdot_tn(v, k * jnp.exp(last - cu))
        st_scr[...] = st * jnp.exp(last) + upd * bdm
        return carry

    lax.fori_loop(0, tt // c, chunk, 0, unroll=2)

    o = o_scr[...]
    ms = jnp.dot(o * o, bdm, precision=lax.Precision.HIGHEST,
                 preferred_element_type=F32) * (1.0 / HEAD_DIM)
    ob_ref[...] = (o * lax.rsqrt(ms + RMS_EPS) * gain_ref[...] * _silu(hg_ref[...])).astype(ob_ref.dtype)

    @pl.when(i == pl.num_programs(1) - 1)
    def _():
        st_ref[...] = st_scr[...]


def hgrn_prompt(proj, loglb, log1mlb, gain4, bdm, hmask, batch, seq, tt):
    nt = seq // tt
    base = (A_HEADS + 2 * A_KV_HEADS) * HEAD_DIM // HG_WIDTH

    def col(cblk):
        return pl.BlockSpec((tt, HG_WIDTH), lambda b, i: (b * nt + i, cblk))

    row_spec = pl.BlockSpec((1, HG_WIDTH), lambda b, i: (0, 0))
    mat_spec = pl.BlockSpec((HG_WIDTH, HG_WIDTH), lambda b, i: (0, 0))
    return pl.pallas_call(
        _hgrn_prompt_kernel,
        grid=(batch, nt),
        in_specs=[col(base), col(base + 1), col(base + 2), col(base + 3),
                  row_spec, row_spec, row_spec, mat_spec,
                  pl.BlockSpec((HG_MROWS, HG_WIDTH), lambda b, i: (0, 0))],
        out_specs=[pl.BlockSpec((tt, HG_WIDTH), lambda b, i: (b * nt + i, 0)),
                   pl.BlockSpec((None, HG_WIDTH, HG_WIDTH), lambda b, i: (b, 0, 0))],
        out_shape=[jax.ShapeDtypeStruct((batch * seq, HG_WIDTH), MIX_DTYPE),
                   jax.ShapeDtypeStruct((batch, HG_WIDTH, HG_WIDTH), F32)],
        scratch_shapes=[pltpu.VMEM((HG_WIDTH, HG_WIDTH), F32),
                        pltpu.VMEM((tt, HG_WIDTH), F32),
                        pltpu.VMEM((tt, HG_WIDTH), F32),
                        pltpu.VMEM((tt, HG_WIDTH), F32),
                        pltpu.VMEM((tt, HG_WIDTH), F32)],
        compiler_params=_cparams(("parallel", "arbitrary")),
        name="hgrn_prompt",
    )(proj, proj, proj, proj, loglb, log1mlb, gain4, bdm, hmask)


def _hgrn_decode_kernel(hq_ref, hf_ref, hi_ref, hg_ref, loglb_ref, log1mlb_ref, gain_ref, s_ref,
                        ob_ref, so_ref):
    nb = hq_ref.shape[1]
    qs, lf, kk = _hgrn_gates(hq_ref[...], hf_ref[...], loglb_ref[...], log1mlb_ref[...])
    v = hi_ref[...]
    f = jnp.exp(lf)
    s = s_ref[...].reshape(HEAD_DIM, HEAD_DIM, nb)
    att = jnp.sum(_rb(qs) * _rb(kk), axis=0, keepdims=True)
    o = _rb(att) * _rb(v) + jnp.sum(_rb(qs * f)[:, None, :] * _rb(s), axis=0)
    s_new = f[:, None, :] * s + _rb(kk)[:, None, :] * _rb(v)[None, :, :]
    so_ref[...] = s_new.reshape(HEAD_DIM * HEAD_DIM, nb)
    ms = jnp.mean(o * o, axis=0, keepdims=True)
    ob_ref[...] = (o * lax.rsqrt(ms + RMS_EPS) * gain_ref[...] * _silu(hg_ref[...])).astype(ob_ref.dtype)


def hgrn_decode(gates_t, loglb_t, log1mlb_t, gain_t, state_t, layer):
    nb = gates_t.shape[1]

    def blk(off):
        return pl.BlockSpec((HEAD_DIM, nb), lambda h: (off * HG_HEADS + h, 0))

    par = pl.BlockSpec((HEAD_DIM, nb), lambda h: (h, 0))
    st = pl.BlockSpec((None, HEAD_DIM * HEAD_DIM, nb), lambda h: (layer, h, 0))
    return pl.pallas_call(
        _hgrn_decode_kernel,
        grid=(HG_HEADS,),
        in_specs=[blk(0), blk(1), blk(2), blk(3), par, par,
                  pl.BlockSpec((HEAD_DIM, nb), lambda h: (0, 0)), st],
        out_specs=[par, st],
        out_shape=[jax.ShapeDtypeStruct((HG_WIDTH, nb), F32),
                   jax.ShapeDtypeStruct(state_t.shape, F32)],
        input_output_aliases={7: 1},
        compiler_params=_cparams(("arbitrary",)),
        name="hgrn_decode",
    )(gates_t, gates_t, gates_t, gates_t, loglb_t, log1mlb_t, gain_t, state_t)


def _lru_gates(xc, wa_ref, ba_ref, wx_ref, bx_ref, lam_ref):
    r = jax.nn.sigmoid(_bdot(xc, wa_ref[...]) + ba_ref[...])
    gi = jax.nn.sigmoid(_bdot(xc, wx_ref[...]) + bx_ref[...])
    log_a = -LRU_C * r * _softplus(-lam_ref[...])
    a = jnp.exp(log_a)
    bterm = jnp.sqrt(_neg_expm1(2.0 * log_a)) * (gi * xc)
    return a, bterm


def _lru_prompt_kernel(lx_ref, lg_ref, cw_ref, cb_ref, wa_ref, ba_ref, wx_ref, bx_ref, lam_ref,
                       oc_ref, hl_ref, ext_scr, h_scr):
    i = pl.program_id(1)
    tt = lx_ref.shape[0]
    pad = SUBLANES

    @pl.when(i == 0)
    def _():
        ext_scr[0:pad, :] = jnp.zeros((pad, LRU_WIDTH), F32)
        h_scr[...] = jnp.zeros_like(h_scr)

    x = lx_ref[...]
    ext_scr[pad:pad + tt, :] = x
    xc = cb_ref[...] + cw_ref[CONV_W - 1:CONV_W, :] * x
    for j in range(CONV_W - 1):
        back = CONV_W - 1 - j
        xc = xc + cw_ref[j:j + 1, :] * ext_scr[pad - back:pad - back + tt, :]
    ext_scr[0:pad, :] = x[tt - pad:tt, :]

    a, bterm = _lru_gates(xc, wa_ref, ba_ref, wx_ref, bx_ref, lam_ref)
    row = lax.broadcasted_iota(jnp.int32, (tt, LRU_WIDTH), 0)
    sh = 1
    while sh < tt:
        keep = row >= sh
        b_s = jnp.where(keep, pltpu.roll(bterm, sh, 0), 0.0)
        a_s = jnp.where(keep, pltpu.roll(a, sh, 0), 1.0)
        bterm = a * b_s + bterm
        a = a * a_s
        sh *= 2
    h = a * h_scr[...] + bterm
    h_scr[...] = h[tt - 1:tt, :]
    oc_ref[...] = (h * _gelu_tanh(lg_ref[...])).astype(oc_ref.dtype)

    @pl.when(i == pl.num_programs(1) - 1)
    def _():
        hl_ref[...] = h[tt - 1:tt, :]


def lru_prompt(proj, conv_w, conv_b, wa_bd, ba, wx_bd, bx, lam, batch, seq, tt):
    nt = seq // tt
    base = IN_COLS // LRU_WIDTH - 2

    def col(cblk):
        return pl.BlockSpec((tt, LRU_WIDTH), lambda b, i: (b * nt + i, cblk))

    row_spec = pl.BlockSpec((1, LRU_WIDTH), lambda b, i: (0, 0))
    mat_spec = pl.BlockSpec((LRU_WIDTH, LRU_WIDTH), lambda b, i: (0, 0))
    return pl.pallas_call(
        _lru_prompt_kernel,
        grid=(batch, nt),
        in_specs=[col(base), col(base + 1),
                  pl.BlockSpec((CONV_W, LRU_WIDTH), lambda b, i: (0, 0)), row_spec,
                  mat_spec, row_spec, mat_spec, row_spec, row_spec],
        out_specs=[pl.BlockSpec((tt, LRU_WIDTH), lambda b, i: (b * nt + i, 0)),
                   pl.BlockSpec((None, 1, LRU_WIDTH), lambda b, i: (b, 0, 0))],
        out_shape=[jax.ShapeDtypeStruct((batch * seq, LRU_WIDTH), MIX_DTYPE),
                   jax.ShapeDtypeStruct((batch, 1, LRU_WIDTH), F32)],
        scratch_shapes=[pltpu.VMEM((tt + 8, LRU_WIDTH), F32),
                        pltpu.VMEM((1, LRU_WIDTH), F32)],
        compiler_params=_cparams(("parallel", "arbitrary")),
        name="lru_prompt",
    )(proj, proj, conv_w, conv_b, wa_bd, ba, wx_bd, bx, lam)


def _lru_decode_kernel(lx_ref, lg_ref, buf_ref, h0_ref, cw_ref, cb_ref, wa_ref, ba_ref, wx_ref,
                       bx_ref, lam_ref, oc_ref, hn_ref, nbuf_ref):
    x = lx_ref[...]
    buf = buf_ref[...]
    xc = cb_ref[...] + cw_ref[CONV_W - 1:CONV_W, :] * x
    for j in range(CONV_W - 1):
        xc = xc + cw_ref[j:j + 1, :] * buf[:, j * LRU_WIDTH:(j + 1) * LRU_WIDTH]
    a, bterm = _lru_gates(xc, wa_ref, ba_ref, wx_ref, bx_ref, lam_ref)
    h = a * h0_ref[...] + bterm
    hn_ref[...] = h
    oc_ref[...] = (h * _gelu_tanh(lg_ref[...])).astype(oc_ref.dtype)
    nbuf_ref[...] = jnp.concatenate([buf[:, LRU_WIDTH:], x], axis=1)


def lru_decode(proj_d, conv_buf, h0, conv_w, conv_b, wa_bd, ba, wx_bd, bx, lam):
    nb = proj_d.shape[0]
    base = IN_COLS // LRU_WIDTH - 2
    row_spec = pl.BlockSpec((1, LRU_WIDTH), lambda i: (0, 0))
    mat_spec = pl.BlockSpec((LRU_WIDTH, LRU_WIDTH), lambda i: (0, 0))
    act = pl.BlockSpec((nb, LRU_WIDTH), lambda i: (0, 0))
    bufs = pl.BlockSpec((nb, (CONV_W - 1) * LRU_WIDTH), lambda i: (0, 0))
    return pl.pallas_call(
        _lru_decode_kernel,
        grid=(1,),
        in_specs=[pl.BlockSpec((nb, LRU_WIDTH), lambda i: (0, base)),
                  pl.BlockSpec((nb, LRU_WIDTH), lambda i: (0, base + 1)),
                  bufs, act, pl.BlockSpec((CONV_W, LRU_WIDTH), lambda i: (0, 0)), row_spec,
                  mat_spec, row_spec, mat_spec, row_spec, row_spec],
        out_specs=[act, act, bufs],
        out_shape=[jax.ShapeDtypeStruct((nb, LRU_WIDTH), F32),
                   jax.ShapeDtypeStruct((nb, LRU_WIDTH), F32),
                   jax.ShapeDtypeStruct((nb, (CONV_W - 1) * LRU_WIDTH), F32)],
        compiler_params=_cparams(("arbitrary",)),
        name="lru_decode",
    )(proj_d, proj_d, conv_buf, h0, conv_w, conv_b, wa_bd, ba, wx_bd, bx, lam)


def _xattn_prompt_kernel(x_ref, wq_ref, mk_ref, mv_ref, wo_ref, g_ref, b_ref, wr_ref, br_ref, o_ref):
    x = x_ref[...]
    q = _bdot(x, wq_ref[...]).astype(BF16)
    mk = mk_ref[...].astype(BF16)
    mv = mv_ref[...].astype(BF16)
    outs = []
    for h in range(X_HEADS):
        sl = slice(h * HEAD_DIM, (h + 1) * HEAD_DIM)
        s = _bdot_nt(q[:, sl], mk[:, sl]) * SCALE
        m = jnp.max(s, -1, keepdims=True)
        p = jnp.exp(s - m)
        p = p / jnp.sum(p, -1, keepdims=True)
        outs.append(_bdot(p, mv[:, sl]))
    o = jnp.concatenate(outs, axis=1)
    y = _layer_norm(ALPHA * x + _bdot(o, wo_ref[...]), g_ref[...], b_ref[...])
    logits_t = _bdot_nt(wr_ref[...], y) + br_ref[...]
    gate_t = _route_rows(logits_t[:ROUTER_ROWS, :])
    gate_t = jnp.concatenate([gate_t, jnp.zeros((ROUTER_LANES - ROUTER_ROWS, y.shape[0]), F32)], axis=0)
    for j in range(D_MODEL // LANES):
        o_ref[j] = y[:, j * LANES:(j + 1) * LANES]
    o_ref[D_MODEL // LANES] = gate_t.T


def xattn_prompt(x, wq, mem_kv, wo, g, b, wr, br, batch, seq, tt):
    nt = seq // tt
    const = lambda bi, i: (0, 0)
    return pl.pallas_call(
        _xattn_prompt_kernel,
        grid=(batch, nt),
        in_specs=[pl.BlockSpec((tt, D_MODEL), lambda bi, i: (bi * nt + i, 0)),
                  pl.BlockSpec((D_MODEL, X_WIDTH), const),
                  pl.BlockSpec((N_MEM, X_WIDTH), lambda bi, i: (bi, 0)),
                  pl.BlockSpec((N_MEM, X_WIDTH), lambda bi, i: (bi, 1)),
                  pl.BlockSpec((X_WIDTH, D_MODEL), const),
                  pl.BlockSpec((1, D_MODEL), const),
                  pl.BlockSpec((1, D_MODEL), const),
                  pl.BlockSpec((ROUTER_LANES, D_MODEL), const),
                  pl.BlockSpec((ROUTER_LANES, 1), const)],
        out_specs=pl.BlockSpec((XG_PLANES, tt, LANES), lambda bi, i: (0, bi * nt + i, 0)),
        out_shape=jax.ShapeDtypeStruct((XG_PLANES, batch * seq, LANES), F32),
        compiler_params=_cparams(("parallel", "parallel")),
        name="xattn_prompt",
    )(x, wq, mem_kv, mem_kv, wo, g, b, wr, br)


def _route(logits):
    lane = lax.broadcasted_iota(jnp.int32, logits.shape, 1)
    big = jnp.int32(ROUTER_LANES)
    ninf = -jnp.inf
    gl = jnp.where(lane < N_GROUPS, logits, ninf)
    gm = jnp.max(gl, -1, keepdims=True)
    g_val = 1.0 / jnp.sum(jnp.exp(gl - gm), -1, keepdims=True)
    g_idx = jnp.min(jnp.where(gl == gm, lane, big), -1, keepdims=True)
    lo = N_GROUPS + EXP_PER_GROUP * g_idx
    el = jnp.where((lane >= lo) & (lane < lo + EXP_PER_GROUP), logits, ninf)
    v1 = jnp.max(el, -1, keepdims=True)
    i1 = jnp.min(jnp.where(el == v1, lane, big), -1, keepdims=True)
    el2 = jnp.where(lane == i1, ninf, el)
    v2 = jnp.max(el2, -1, keepdims=True)
    i2 = jnp.min(jnp.where(el2 == v2, lane, big), -1, keepdims=True)
    e2 = jnp.exp(v2 - v1)
    w1 = g_val / (1.0 + e2)
    w2 = g_val * e2 / (1.0 + e2)
    return jnp.where(lane == i1, w1, 0.0) + jnp.where(lane == i2, w2, 0.0), g_idx


def _route_rows(logits):
    row = lax.broadcasted_iota(jnp.int32, logits.shape, 0)
    big = jnp.int32(ROUTER_ROWS)
    ninf = -jnp.inf
    gl = jnp.where(row < N_GROUPS, logits, ninf)
    gm = jnp.max(gl, 0, keepdims=True)
    g_val = 1.0 / jnp.sum(jnp.exp(gl - gm), 0, keepdims=True)
    g_idx = jnp.min(jnp.where(gl == gm, row, big), 0, keepdims=True)
    lo = N_GROUPS + EXP_PER_GROUP * g_idx
    el = jnp.where((row >= lo) & (row < lo + EXP_PER_GROUP), logits, ninf)
    v1 = jnp.max(el, 0, keepdims=True)
    i1 = jnp.min(jnp.where(el == v1, row, big), 0, keepdims=True)
    el2 = jnp.where(row == i1, ninf, el)
    v2 = jnp.max(el2, 0, keepdims=True)
    i2 = jnp.min(jnp.where(el2 == v2, row, big), 0, keepdims=True)
    e2 = jnp.exp(v2 - v1)
    w1 = g_val / (1.0 + e2)
    w2 = g_val * e2 / (1.0 + e2)
    gate = jnp.where(row == i1, w1, 0.0) + jnp.where(row == i2, w2, 0.0)
    return jnp.where(row == 0, g_idx.astype(F32), gate)


def _moe_dense_kernel(x_ref, wr_ref, br_ref, wg_ref, wu_ref, wd_ref, g_ref, b_ref, o_ref,
                      gate_scr, acc_scr):
    e = pl.program_id(1)

    @pl.when(e == 0)
    def _():
        logits = _bdot(x_ref[...], wr_ref[...]) + br_ref[...]
        gate_scr[...] = _route(logits)[0]
        acc_scr[...] = jnp.zeros_like(acc_scr)

    xb = x_ref[...].astype(BF16)
    lane = lax.broadcasted_iota(jnp.int32, gate_scr.shape, 1)
    gcol = jnp.sum(jnp.where(lane == e + N_GROUPS, gate_scr[...], 0.0), -1, keepdims=True)
    hid = _silu(_bdot(xb, wg_ref[...])) * _bdot(xb, wu_ref[...])
    acc_scr[...] += _bdot(hid * gcol, wd_ref[...])

    @pl.when(e == pl.num_programs(1) - 1)
    def _():
        o_ref[...] = _layer_norm(ALPHA * x_ref[...] + acc_scr[...], g_ref[...], b_ref[...])


def moe_dense(x, wr, br, wg, wu, wd, g, b, layer, tm):
    m = x.shape[0]
    return pl.pallas_call(
        _moe_dense_kernel,
        grid=(m // tm, N_EXPERTS),
        in_specs=[pl.BlockSpec((tm, D_MODEL), lambda i, e: (i, 0)),
                  pl.BlockSpec((D_MODEL, ROUTER_LANES), lambda i, e: (0, 0)),
                  pl.BlockSpec((1, ROUTER_LANES), lambda i, e: (0, 0)),
                  pl.BlockSpec((None, None, D_MODEL, EXP_FF), lambda i, e: (layer, e, 0, 0)),
                  pl.BlockSpec((None, None, D_MODEL, EXP_FF), lambda i, e: (layer, e, 0, 0)),
                  pl.BlockSpec((None, None, EXP_FF, D_MODEL), lambda i, e: (layer, e, 0, 0)),
                  pl.BlockSpec((1, D_MODEL), lambda i, e: (0, 0)),
                  pl.BlockSpec((1, D_MODEL), lambda i, e: (0, 0))],
        out_specs=pl.BlockSpec((tm, D_MODEL), lambda i, e: (i, 0)),
        out_shape=jax.ShapeDtypeStruct((m, D_MODEL), F32),
        scratch_shapes=[pltpu.VMEM((tm, ROUTER_LANES), F32), pltpu.VMEM((tm, D_MODEL), F32)],
        compiler_params=_cparams(("parallel", "arbitrary")),
        name="moe_dense",
    )(x, wr, br, wg, wu, wd, g, b)


def _sc_mesh():
    return plsc.VectorSubcoreMesh(core_axis_name="core", subcore_axis_name="subcore")


def sc_scatter_rows(x, idx, n_out):
    r = x.shape[0]

    @functools.partial(pl.kernel, out_type=jax.ShapeDtypeStruct((n_out, LANES), x.dtype),
                       mesh=_sc_mesh(), scratch_types=[], name="sc_scatter_rows")
    def k(x_hbm, i_hbm, o_hbm):
        def body(x_vmem, i_vmem):
            pltpu.sync_copy(x_vmem, o_hbm.at[i_vmem.at[0]])

        pltpu.emit_pipeline(
            body,
            grid=(r // SC_WINDOW,),
            in_specs=[pl.BlockSpec((SC_WINDOW, LANES), lambda i: (i, 0)),
                      pl.BlockSpec((1, SC_WINDOW), lambda i: (0, i))],
            out_specs=[],
            core_axis_name=("core", "subcore"),
            dimension_semantics=(pltpu.PARALLEL,),
        )(x_hbm, i_hbm)

    return k(x, idx.reshape(1, r))


def sc_gather_rows(table, idx):
    r = idx.shape[0]

    @functools.partial(pl.kernel, out_type=jax.ShapeDtypeStruct((r, LANES), table.dtype),
                       mesh=_sc_mesh(), scratch_types=[], name="sc_gather_rows")
    def k(t_hbm, i_hbm, o_hbm):
        def body(i_vmem, o_vmem):
            pltpu.sync_copy(t_hbm.at[i_vmem.at[0]], o_vmem)

        pltpu.emit_pipeline(
            body,
            grid=(r // SC_WINDOW,),
            in_specs=[pl.BlockSpec((1, SC_WINDOW), lambda i: (0, i))],
            out_specs=[pl.BlockSpec((SC_WINDOW, LANES), lambda i: (i, 0))],
            core_axis_name=("core", "subcore"),
            dimension_semantics=(pltpu.PARALLEL,),
        )(i_hbm, o_hbm)

    return k(table, idx.reshape(1, r))


def _moe_sorted_kernel(tg_ref, nused_ref, xs_ref, wg32_ref, wu32_ref, wd32_ref, g_ref, b_ref, o_ref,
                       wg_ref, wu_ref, wd_ref):
    t = pl.program_id(0)
    used = t < nused_ref[0]

    @pl.when(used & ((t == 0) | (tg_ref[t] != tg_ref[jnp.maximum(t - 1, 0)])))
    def _():
        wg_ref[...] = wg32_ref[...].astype(BF16)
        wu_ref[...] = wu32_ref[...].astype(BF16)
        wd_ref[...] = wd32_ref[...].astype(BF16)

    @pl.when(used)
    def _():
        x = jnp.concatenate([xs_ref[j] for j in range(X_PLANES)], axis=1)
        gate = xs_ref[X_PLANES]
        xb = x.astype(BF16)
        lane = lax.broadcasted_iota(jnp.int32, gate.shape, 1)
        first = N_GROUPS + EXP_PER_GROUP * tg_ref[t]
        acc = jnp.zeros(x.shape, F32)
        for e in range(EXP_PER_GROUP):
            gcol = jnp.sum(jnp.where(lane == first + e, gate, 0.0), -1, keepdims=True)
            hid = _silu(_bdot(xb, wg_ref[e])) * _bdot(xb, wu_ref[e])
            acc = acc + _bdot(hid * gcol, wd_ref[e])
        y = _layer_norm(ALPHA * x + acc, g_ref[...], b_ref[...])
        for j in range(X_PLANES):
            o_ref[j] = y[:, j * LANES:(j + 1) * LANES]


def _group_slots(group_idx, n, tm):
    n_tiles = n // tm + N_GROUPS
    onehot = (group_idx[:, None] == jnp.arange(N_GROUPS)[None, :]).astype(jnp.int32)
    csum = jnp.cumsum(onehot, axis=0)
    counts = csum[-1]
    rank = jnp.sum(onehot * csum, axis=1) - 1
    tiles_g = (counts + tm - 1) // tm
    tile_end = jnp.cumsum(tiles_g)
    slot_base = (tile_end - tiles_g) * tm
    slot = (jnp.sum(onehot * slot_base[None, :], axis=1) + rank).astype(jnp.int32)
    tile_group = jnp.sum((jnp.arange(n_tiles)[:, None] >= tile_end[None, :]).astype(jnp.int32), axis=1)
    tile_group = jnp.minimum(tile_group, N_GROUPS - 1).astype(jnp.int32)
    return slot, tile_group, tile_end[-1:].astype(jnp.int32)


def moe_routed_sc(xg, wg, wu, wd, g, b, layer, n, tm, during_scatter, during_gather, tiles_out):
    slot, tile_group, n_used = _group_slots(xg[X_PLANES, :, 0].astype(jnp.int32), n, tm)
    n_tiles = tile_group.shape[0]
    n_slots = n_tiles * tm
    plane_base = lambda planes: jnp.arange(planes, dtype=jnp.int32)[:, None] * n_slots
    idx_in = (plane_base(XG_PLANES) + slot[None, :]).reshape(-1)
    if tiles_out:
        idx_out = (plane_base(X_PLANES)[None] + slot.reshape(n // SUBLANES, 1, SUBLANES)).reshape(-1)
    else:
        idx_out = (plane_base(X_PLANES) + slot[None, :]).reshape(-1)
    xs = sc_scatter_rows(xg.reshape(XG_PLANES * n, LANES), idx_in, XG_PLANES * n_slots)
    xs = xs.reshape(XG_PLANES, n_slots, LANES)
    xs, side = lax.optimization_barrier((xs, during_scatter()))
    wspec = lambda shp: pl.BlockSpec((None, None, EXP_PER_GROUP) + shp,
                                     lambda t, tg, nu: (layer, tg[t], 0, 0, 0))
    grid_spec = pltpu.PrefetchScalarGridSpec(
        num_scalar_prefetch=2,
        grid=(n_tiles,),
        in_specs=[pl.BlockSpec((XG_PLANES, tm, LANES), lambda t, tg, nu: (0, t, 0)),
                  wspec((D_MODEL, EXP_FF)), wspec((D_MODEL, EXP_FF)), wspec((EXP_FF, D_MODEL)),
                  pl.BlockSpec((1, D_MODEL), lambda t, tg, nu: (0, 0)),
                  pl.BlockSpec((1, D_MODEL), lambda t, tg, nu: (0, 0))],
        out_specs=pl.BlockSpec((X_PLANES, tm, LANES), lambda t, tg, nu: (0, t, 0)),
        scratch_shapes=[pltpu.VMEM((EXP_PER_GROUP, D_MODEL, EXP_FF), BF16),
                        pltpu.VMEM((EXP_PER_GROUP, D_MODEL, EXP_FF), BF16),
                        pltpu.VMEM((EXP_PER_GROUP, EXP_FF, D_MODEL), BF16)])
    grouped = lambda w: w.reshape(w.shape[0], N_GROUPS, EXP_PER_GROUP, w.shape[2], w.shape[3])
    ys = pl.pallas_call(
        _moe_sorted_kernel,
        grid_spec=grid_spec,
        out_shape=jax.ShapeDtypeStruct((X_PLANES, n_slots, LANES), F32),
        compiler_params=_cparams(("arbitrary",)),
        name="moe_sorted",
    )(tile_group, n_used, xs, grouped(wg), grouped(wu), grouped(wd), g, b)
    y = sc_gather_rows(ys.reshape(X_PLANES * n_slots, LANES), idx_out)
    y, side = lax.optimization_barrier((y, during_gather(side)))
    if tiles_out:
        y = y.reshape(n // SUBLANES, X_PLANES, SUBLANES, LANES).transpose(0, 2, 1, 3)
        return y.reshape(n, D_MODEL), side
    return y.reshape(X_PLANES, n, LANES), side


def _t5_bucket(dist):
    max_exact = N_BUCKETS // 2
    d = jnp.maximum(dist, 0)
    df = jnp.maximum(d, 1).astype(F32)
    log_b = max_exact + (jnp.log(df / max_exact) / math.log(MAX_DISTANCE / max_exact)
                         * (N_BUCKETS - max_exact)).astype(jnp.int32)
    return jnp.where(d < max_exact, d, jnp.minimum(log_b, N_BUCKETS - 1))


def _bucket_lookup(rel_bias, bucket):
    out = jnp.zeros(bucket.shape + (rel_bias.shape[1],), F32)
    for i in range(N_BUCKETS):
        out = jnp.where((bucket == i)[..., None], rel_bias[i].astype(F32), out)
    return out


def _prompt_bias(rel_bias):
    qi = jnp.arange(WINDOW)[:, None]
    kj = jnp.arange(2 * WINDOW)[None, :]
    dist = qi + WINDOW - kj
    bias = _bucket_lookup(rel_bias, _t5_bucket(dist)).transpose(2, 0, 1)
    valid = (dist >= 0) & (dist <= WINDOW)
    return jnp.where(valid[None], bias, NEG)


def _decode_table(rel_bias, attn_sink):
    dist = WINDOW - jnp.arange(WINDOW + 1)
    bias = _bucket_lookup(rel_bias, _t5_bucket(dist)).T
    depth = attn_sink.shape[0]
    wide = lambda v: jnp.broadcast_to(v[..., None], v.shape + (WINDOW,))
    per_layer = lambda t: jnp.broadcast_to(t[None], (depth,) + t.shape)
    return jnp.stack([per_layer(bias[:, :WINDOW]), per_layer(wide(bias[:, WINDOW])),
                      wide(attn_sink.astype(F32))], axis=1)


def _block_ones(width):
    idx = jnp.arange(width) // HEAD_DIM
    return (idx[:, None] == idx[None, :])


def _head_rows_mask():
    head = jnp.arange(HG_WIDTH)[None, :] // HEAD_DIM
    return (head == jnp.arange(HG_MROWS)[:, None]).astype(F32)


def _block_diag(w):
    nblk, s, _ = w.shape
    eye = jnp.eye(nblk, dtype=w.dtype)
    return (eye[:, None, :, None] * w[:, :, None, :]).reshape(nblk * s, nblk * s)


def kernel(x_prompt, x_sample, mem_prompt, cache_win_k, cache_win_v, state_hgrn, state_conv, state_lru, cache_mem_k, cache_mem_v, rel_bias, hg_lb, w_in, attn_sink, hg_gain, conv_w, conv_b, lru_wa, lru_ba, lru_wx, lru_bx, lru_lam, w_out, ln1_g, ln1_b, x_wq, x_wk, x_wv, x_wo, ln2_g, ln2_b, r_gw, r_gb, r_ew, r_eb, e_wg, e_wu, e_wd, ln3_g, ln3_b):
    bp, seq, d = x_prompt.shape
    n_tok = bp * seq
    nd = x_sample.shape[0]
    depth = w_in.shape[0]

    lbs = jnp.cumsum(jax.nn.softmax(hg_lb.astype(F32), axis=0), axis=0)
    lbs = lbs - lbs[0]
    loglb = jnp.log(lbs)
    log1mlb = jnp.log1p(-lbs)
    gain4 = jnp.tile(hg_gain, (1, HG_HEADS))

    bias_p = _prompt_bias(rel_bias)
    bdm256 = _block_ones(HG_WIDTH).astype(F32)
    hmask = _head_rows_mask()

    w_in_b = w_in.astype(BF16)
    w_out_b = w_out.astype(BF16)
    wq_b = x_wq.astype(BF16)
    wkv_b = jnp.concatenate([x_wk, x_wv], axis=-1).astype(BF16)
    wo_b = x_wo.astype(BF16)
    rew = r_ew.transpose(0, 2, 1, 3).reshape(depth, d, N_EXPERTS)
    wr = jnp.concatenate([r_gw, rew, jnp.zeros((depth, d, ROUTER_LANES - N_GROUPS - N_EXPERTS), F32)], -1)
    br = jnp.concatenate([r_gb, r_eb.reshape(depth, N_EXPERTS),
                          jnp.zeros((depth, ROUTER_LANES - N_GROUPS - N_EXPERTS), F32)], -1)

    a_w = A_HEADS * HEAD_DIM
    xp = x_prompt.reshape(bp * seq, d)
    xs = x_sample.reshape(nd, d)
    mem = mem_prompt.reshape(bp * N_MEM, d)
    ckt = cache_win_k.transpose(0, 1, 3, 4, 2).reshape(depth, nd, LANES, WINDOW)
    cvt = cache_win_v.transpose(0, 1, 3, 4, 2).reshape(depth, nd, LANES, WINDOW)
    cmkt = cache_mem_k.transpose(0, 1, 3, 4, 2).reshape(depth, nd, X_WIDTH, N_MEM)
    cmvt = cache_mem_v.transpose(0, 1, 3, 4, 2).reshape(depth, nd, X_WIDTH, N_MEM)
    state_t = state_hgrn.transpose(0, 2, 3, 4, 1).reshape(depth, HG_HEADS * HEAD_DIM * HEAD_DIM, nd)
    dec_tab = _decode_table(rel_bias, attn_sink)
    xq_mask = _head_rows_mask()[:X_QROWS]
    head_group = jnp.arange(A_HEADS) // A_REP

    p_wk, p_wv, p_s, p_cb, p_h, p_mk, p_mv = [], [], [], [], [], [], []
    s_cb, s_h = [], []
    for l in range(depth):
        row = lambda v: v[l].reshape(1, -1)
        wa_bd = _block_diag(lru_wa[l]).astype(BF16)
        wx_bd = _block_diag(lru_wx[l]).astype(BF16)
        lru_args = (conv_w[l], row(conv_b), wa_bd, row(lru_ba), wx_bd, row(lru_bx), row(lru_lam))
        wo_parts = [w_out_b[l, :a_w], w_out_b[l, a_w:a_w + HG_WIDTH], w_out_b[l, a_w + HG_WIDTH:]]

        proj = matmul(xp, w_in_b[l], ROW_TILE, IN_COLS)
        oa = attn_prompt(proj, attn_sink[l], bias_p, bp, seq)
        ob, st = hgrn_prompt(proj, row(loglb), row(log1mlb), row(gain4), bdm256, hmask, bp, seq, ROW_TILE)
        oc, hl = lru_prompt(proj, *lru_args, bp, seq, LRU_TILE)
        xp = proj_res_ln(xp, [oa, ob, oc], wo_parts, row(ln1_g), row(ln1_b), ROW_TILE)
        mkv = matmul(mem, wkv_b[l], N_MEM, 2 * X_WIDTH)
        xg = xattn_prompt(xp, wq_b[l], mkv, wo_b[l], row(ln2_g), row(ln2_b), wr[l].T, br[l][:, None],
                          bp, seq, ROW_TILE)
        def decode_mixers(xs=xs, ckt=ckt, cvt=cvt, state_t=state_t, l=l, lru_args=lru_args,
                          wo_parts=wo_parts, row=row):
            projd = matmul(xs, w_in_b[l], nd, IN_COLS)
            q3 = projd[:, :a_w].reshape(nd, A_HEADS, 1, HEAD_DIM)
            on_group = head_group[None, :, None, None] == jnp.arange(A_KV_HEADS)[None, None, :, None]
            qblk = jnp.where(on_group, q3, 0.0).reshape(nd, A_HEADS, LANES)
            o3, ckt, cvt = attn_decode(qblk, projd, dec_tab[l], ckt, cvt, l, DEC_BLOCK)
            o4 = o3.reshape(nd, A_HEADS, A_KV_HEADS, HEAD_DIM)
            oa = jnp.sum(jnp.where(on_group, o4, 0.0), axis=2).reshape(nd, a_w)
            gates_t = projd[:, a_w + 2 * LANES:a_w + 2 * LANES + 4 * HG_WIDTH].T
            bc = lambda v: jnp.broadcast_to(v[:, None], (v.shape[0], nd))
            ob_t, state_t = hgrn_decode(gates_t, bc(loglb[l]), bc(log1mlb[l]), bc(hg_gain[l]), state_t, l)
            oc, nh, nbuf = lru_decode(projd, state_conv[l].reshape(nd, -1), state_lru[l], *lru_args)
            xs = proj_res_ln(xs, [oa, ob_t.T, oc], wo_parts, row(ln1_g), row(ln1_b), nd)
            return xs, ckt, cvt, state_t, nh, nbuf

        def decode_rest(side, l=l, row=row):
            xs = side[0]
            qd = matmul(xs, wq_b[l], nd, X_WIDTH)
            qdb = qd[:, None, :] * xq_mask[None, :, :]
            od = xattn_decode(qdb, cmkt, cmvt, xq_mask, l, DEC_BLOCK)
            xs = proj_res_ln(xs, [od], [wo_b[l]], row(ln2_g), row(ln2_b), nd)
            xs = moe_dense(xs, wr[l], br[l:l + 1], e_wg, e_wu, e_wd, row(ln3_g), row(ln3_b), l, nd)
            return (xs,) + tuple(side[1:])

        xp, (xs, ckt, cvt, state_t, nh, nbuf) = moe_routed_sc(
            xg, e_wg, e_wu, e_wd, row(ln3_g), row(ln3_b), l, n_tok, MOE_TM, decode_mixers, decode_rest,
            tiles_out=(l == depth - 1))

        proj3 = proj.reshape(bp, seq, IN_COLS)
        p_wk.append(proj3[:, seq - WINDOW:, a_w:a_w + LANES].reshape(bp, WINDOW, A_KV_HEADS, HEAD_DIM))
        p_wv.append(proj3[:, seq - WINDOW:, a_w + LANES:a_w + 2 * LANES].reshape(bp, WINDOW, A_KV_HEADS, HEAD_DIM))
        st5 = st.reshape(bp, HG_HEADS, HEAD_DIM, HG_HEADS, HEAD_DIM)
        p_s.append(jnp.stack([st5[:, h, :, h, :] for h in range(HG_HEADS)], 1).transpose(0, 1, 3, 2))
        p_cb.append(proj3[:, seq - (CONV_W - 1):, IN_COLS - 2 * LRU_WIDTH:IN_COLS - LRU_WIDTH])
        p_h.append(hl.reshape(bp, LRU_WIDTH))
        p_mk.append(mkv[:, :X_WIDTH].reshape(bp, N_MEM, X_HEADS, HEAD_DIM))
        p_mv.append(mkv[:, X_WIDTH:].reshape(bp, N_MEM, X_HEADS, HEAD_DIM))

        s_cb.append(nbuf.reshape(nd, CONV_W - 1, LRU_WIDTH))
        s_h.append(nh)

    unkey = lambda c: c.reshape(depth, nd, A_KV_HEADS, HEAD_DIM, WINDOW).transpose(0, 1, 4, 2, 3)
    s_s = state_t.reshape(depth, HG_HEADS, HEAD_DIM, HEAD_DIM, nd).transpose(0, 4, 1, 2, 3)
    return (xp.reshape(bp, seq, d), xs.reshape(nd, 1, d),
            jnp.stack(p_wk), jnp.stack(p_wv), jnp.stack(p_s), jnp.stack(p_cb), jnp.stack(p_h),
            jnp.stack(p_mk), jnp.stack(p_mv),
            unkey(ckt), unkey(cvt), s_s, jnp.stack(s_cb), jnp.stack(s_h))
```

```python
import functools
import math

import jax
import jax.numpy as jnp
from jax import lax
from jax.experimental import pallas as pl
from jax.experimental.pallas import tpu as pltpu
from jax.experimental.pallas import tpu_sc as plsc

F32 = jnp.float32
BF16 = jnp.bfloat16
MIX_DTYPE = BF16

D_MODEL = 1024
DEPTH = 4
HEAD_DIM = 64
A_HEADS = 8
A_KV_HEADS = 2
A_REP = A_HEADS // A_KV_HEADS
WINDOW = 128
A_QB = 2
N_BUCKETS = 32
MAX_DISTANCE = 128
HG_WIDTH = 256
HG_HEADS = 4
HG_CHUNK = 64
HG_TB = 32
HG_MROWS = 8
LOG2E = math.log2(math.e)
LRU_WIDTH = 256
CONV_W = 4
LRU_C = 8.0
N_MEM = 256
X_HEADS = 4
X_WIDTH = X_HEADS * HEAD_DIM
X_QROWS = 8
N_GROUPS = 4
EXP_PER_GROUP = 4
N_EXPERTS = N_GROUPS * EXP_PER_GROUP
EXP_FF = D_MODEL // 4
ALPHA = (2 * DEPTH) ** 0.25
LN_EPS = 1e-5
RMS_EPS = 1e-6
IN_COLS = 2304
SCALE = HEAD_DIM ** -0.5
NEG = -1e30
LANES = 128
SUBLANES = 8
ROUTER_LANES = 128
ROUTER_ROWS = 32
XG_WIDTH = D_MODEL + ROUTER_LANES
XG_PLANES = XG_WIDTH // LANES
X_PLANES = D_MODEL // LANES
MOE_TM = 512
ROW_TILE = 1024
LRU_TILE = 512
DEC_BLOCK = 16
SC_WINDOW = 128
VMEM_LIMIT = 48 * 1024 * 1024


def _cparams(sem):
    return pltpu.CompilerParams(dimension_semantics=sem, vmem_limit_bytes=VMEM_LIMIT)


def _bdot(a, b):
    return jnp.dot(a.astype(BF16), b.astype(BF16), preferred_element_type=F32)


def _bdot_nt(a, b):
    return lax.dot_general(a.astype(BF16), b.astype(BF16), (((1,), (1,)), ((), ())),
                           preferred_element_type=F32)


def _bdot_tn(a, b):
    return lax.dot_general(a.astype(BF16), b.astype(BF16), (((0,), (0,)), ((), ())),
                           preferred_element_type=F32)


def _rb(x):
    return x.astype(BF16).astype(F32)


def _silu(x):
    return x * jax.nn.sigmoid(x)


def _neg_expm1(x):
    return -jnp.tanh(0.5 * x) * (jnp.exp(x) + 1.0)


def _softplus(x):
    return jnp.maximum(x, 0.0) + jnp.log1p(jnp.exp(-jnp.abs(x)))


def _gelu_tanh(x):
    return 0.5 * x * (1.0 + jnp.tanh(math.sqrt(2.0 / math.pi) * (x + 0.044715 * (x * x * x))))


def _layer_norm(y, g, b):
    mu = jnp.mean(y, -1, keepdims=True)
    yc = y - mu
    var = jnp.mean(yc * yc, -1, keepdims=True)
    return yc * lax.rsqrt(var + LN_EPS) * g + b


def _rows(x_ref):
    if len(x_ref.shape) == 2:
        return x_ref[...]
    return jnp.concatenate([x_ref[j] for j in range(x_ref.shape[0])], axis=1)


def _rows_spec(x, tm, nargs):
    if x.ndim == 2:
        return pl.BlockSpec((tm, x.shape[1]), (lambda i: (i, 0)) if nargs == 1 else (lambda i, j: (i, 0)))
    blk = (x.shape[0], tm, LANES)
    return pl.BlockSpec(blk, (lambda i: (0, i, 0)) if nargs == 1 else (lambda i, j: (0, i, 0)))


def _mm_kernel(x_ref, w_ref, o_ref):
    o_ref[...] = _bdot(_rows(x_ref), w_ref[...])


def matmul(x, w, tm, tn):
    m = x.shape[-2]
    k, n = w.shape
    return pl.pallas_call(
        _mm_kernel,
        grid=(m // tm, n // tn),
        in_specs=[_rows_spec(x, tm, 2),
                  pl.BlockSpec((k, tn), lambda i, j: (0, j))],
        out_specs=pl.BlockSpec((tm, tn), lambda i, j: (i, j)),
        out_shape=jax.ShapeDtypeStruct((m, n), F32),
        compiler_params=_cparams(("parallel", "parallel")),
        name="matmul",
    )(x, w)


def _proj_res_ln_kernel(n_in, x_ref, *refs):
    a_refs = refs[:n_in]
    w_refs = refs[n_in:2 * n_in]
    g_ref, b_ref, o_ref = refs[2 * n_in:]
    y = ALPHA * _rows(x_ref)
    for a_ref, w_ref in zip(a_refs, w_refs):
        y = y + _bdot(a_ref[...], w_ref[...])
    o_ref[...] = _layer_norm(y, g_ref[...], b_ref[...])


def proj_res_ln(x, a_list, w_list, g, b, tm):
    m = x.shape[-2]
    d = w_list[0].shape[1]
    n_in = len(a_list)
    in_specs = [_rows_spec(x, tm, 1)]
    in_specs += [pl.BlockSpec((tm, a.shape[1]), lambda i: (i, 0)) for a in a_list]
    in_specs += [pl.BlockSpec(w.shape, lambda i: (0, 0)) for w in w_list]
    in_specs += [pl.BlockSpec((1, d), lambda i: (0, 0))] * 2
    return pl.pallas_call(
        functools.partial(_proj_res_ln_kernel, n_in),
        grid=(m // tm,),
        in_specs=in_specs,
        out_specs=pl.BlockSpec((tm, d), lambda i: (i, 0)),
        out_shape=jax.ShapeDtypeStruct((m, d), F32),
        compiler_params=_cparams(("parallel",)),
        name="proj_res_ln",
    )(x, *a_list, *w_list, g, b)


def _attn_prompt_kernel(sink_ref, q_ref, kc_ref, kp_ref, vc_ref, vp_ref, bias_ref, o_ref):
    n = pl.program_id(1)
    col = lax.broadcasted_iota(jnp.int32, (WINDOW, 2 * WINDOW), 1)
    first = jnp.where((n == 0) & (col < WINDOW), NEG, 0.0)
    kk = jnp.concatenate([kp_ref[...], kc_ref[...]], axis=0).astype(BF16)
    vv = jnp.concatenate([vp_ref[...], vc_ref[...]], axis=0).astype(BF16)
    q = q_ref[...].astype(BF16)
    for u in range(A_QB):
        rows = slice(u * WINDOW, (u + 1) * WINDOW)
        keys = slice(u * WINDOW, (u + 2) * WINDOW)
        outs = []
        for h in range(A_HEADS):
            g = h // A_REP
            qh = q[rows, h * HEAD_DIM:(h + 1) * HEAD_DIM]
            kg = kk[keys, g * HEAD_DIM:(g + 1) * HEAD_DIM]
            vg = vv[keys, g * HEAD_DIM:(g + 1) * HEAD_DIM]
            s = _bdot_nt(qh, kg) * SCALE + bias_ref[h]
            if u == 0:
                s = s + first
            sink = sink_ref[h]
            m = jnp.maximum(jnp.max(s, -1, keepdims=True), sink)
            p = jnp.exp(s - m)
            den = jnp.sum(p, -1, keepdims=True) + jnp.exp(sink - m)
            outs.append(_bdot(p / den, vg))
        o_ref[rows, :] = jnp.concatenate(outs, axis=1).astype(o_ref.dtype)


def attn_prompt(proj, sink, bias, batch, seq):
    nb = seq // WINDOW
    ns = nb // A_QB
    tq = A_QB * WINDOW
    qcol = 0
    kcol = (A_HEADS * HEAD_DIM) // LANES
    vcol = kcol + 1

    def cur(c):
        return lambda b, n: (b * ns + n, c)

    def prev(c):
        return lambda b, n: (b * nb + jnp.maximum(n * A_QB - 1, 0), c)

    return pl.pallas_call(
        _attn_prompt_kernel,
        grid=(batch, ns),
        in_specs=[pl.BlockSpec(memory_space=pltpu.SMEM),
                  pl.BlockSpec((tq, A_HEADS * HEAD_DIM), cur(qcol)),
                  pl.BlockSpec((tq, LANES), cur(kcol)),
                  pl.BlockSpec((WINDOW, LANES), prev(kcol)),
                  pl.BlockSpec((tq, LANES), cur(vcol)),
                  pl.BlockSpec((WINDOW, LANES), prev(vcol)),
                  pl.BlockSpec((A_HEADS, WINDOW, 2 * WINDOW), lambda b, n: (0, 0, 0))],
        out_specs=pl.BlockSpec((tq, A_HEADS * HEAD_DIM), cur(0)),
        out_shape=jax.ShapeDtypeStruct((batch * seq, A_HEADS * HEAD_DIM), MIX_DTYPE),
        compiler_params=_cparams(("parallel", "parallel")),
        name="attn_prompt",
    )(sink, proj, proj, proj, proj, proj, bias)


def _attn_decode_kernel(qb_ref, kn_ref, vn_ref, knt_ref, vnt_ref, ck_ref, cv_ref, tab_ref,
                        o_ref, ok_ref, ov_ref):
    bb = qb_ref.shape[0]
    ck = ck_ref[...]
    cv = cv_ref[...]
    qb = qb_ref[...]
    kn = kn_ref[...]
    vn = vn_ref[...]
    bias_j = tab_ref[0]
    bias_n = tab_ref[1][:, 0:1]
    sink = tab_ref[2][:, 0:1]
    s = lax.dot_general(qb.astype(BF16), ck.astype(BF16), (((2,), (1,)), ((0,), (0,))),
                        preferred_element_type=F32) * SCALE + bias_j[None]
    sn = jnp.sum(_rb(qb) * _rb(kn)[:, None, :], -1, keepdims=True) * SCALE + bias_n[None]
    m = jnp.maximum(jnp.maximum(jnp.max(s, -1, keepdims=True), sn), sink[None])
    p = jnp.exp(s - m)
    pn = jnp.exp(sn - m)
    den = jnp.sum(p, -1, keepdims=True) + pn + jnp.exp(sink[None] - m)
    o = lax.dot_general((p / den).astype(BF16), cv.astype(BF16), (((2,), (2,)), ((0,), (0,))),
                        preferred_element_type=F32)
    o_ref[...] = o + _rb(pn / den) * _rb(vn)[:, None, :]
    lane = lax.broadcasted_iota(jnp.int32, (LANES, LANES), 1)
    for b in range(bb):
        ok_ref[b] = jnp.where(lane == WINDOW - 1, knt_ref[:, b:b + 1], pltpu.roll(ck[b], WINDOW - 1, 1))
        ov_ref[b] = jnp.where(lane == WINDOW - 1, vnt_ref[:, b:b + 1], pltpu.roll(cv[b], WINDOW - 1, 1))


def attn_decode(qblk, proj_d, table, cache_k, cache_v, layer, bb):
    nbatch = proj_d.shape[0]
    a_w = A_HEADS * HEAD_DIM
    cols = lambda c: proj_d[:, c:c + LANES].reshape(nbatch // bb, bb, LANES).transpose(0, 2, 1)
    knt, vnt = cols(a_w), cols(a_w + LANES)
    kcol = (A_HEADS * HEAD_DIM) // LANES
    cache_spec = pl.BlockSpec((None, bb, LANES, WINDOW), lambda i: (layer, i, 0, 0))
    col_spec = pl.BlockSpec((None, LANES, bb), lambda i: (i, 0, 0))
    return pl.pallas_call(
        _attn_decode_kernel,
        grid=(nbatch // bb,),
        in_specs=[pl.BlockSpec((bb, A_HEADS, LANES), lambda i: (i, 0, 0)),
                  pl.BlockSpec((bb, LANES), lambda i: (i, kcol)),
                  pl.BlockSpec((bb, LANES), lambda i: (i, kcol + 1)),
                  col_spec, col_spec, cache_spec, cache_spec,
                  pl.BlockSpec((3, A_HEADS, WINDOW), lambda i: (0, 0, 0))],
        out_specs=[pl.BlockSpec((bb, A_HEADS, LANES), lambda i: (i, 0, 0)), cache_spec, cache_spec],
        out_shape=[jax.ShapeDtypeStruct((nbatch, A_HEADS, LANES), F32),
                   jax.ShapeDtypeStruct(cache_k.shape, F32),
                   jax.ShapeDtypeStruct(cache_v.shape, F32)],
        input_output_aliases={5: 1, 6: 2},
        compiler_params=_cparams(("arbitrary",)),
        name="attn_decode",
    )(qblk, proj_d, proj_d, knt, vnt, cache_k, cache_v, table)


def _xattn_decode_kernel(qb_ref, mk_ref, mv_ref, hm_ref, o_ref):
    qb = qb_ref[...]
    s = lax.dot_general(qb.astype(BF16), mk_ref[...].astype(BF16), (((2,), (1,)), ((0,), (0,))),
                        preferred_element_type=F32) * SCALE
    m = jnp.max(s, -1, keepdims=True)
    p = jnp.exp(s - m)
    p = p / jnp.sum(p, -1, keepdims=True)
    o = lax.dot_general(p.astype(BF16), mv_ref[...].astype(BF16), (((2,), (2,)), ((0,), (0,))),
                        preferred_element_type=F32)
    o_ref[...] = jnp.sum(o * hm_ref[...][None], axis=1)


def xattn_decode(qblk, mem_k, mem_v, hmask, layer, bb):
    nbatch = qblk.shape[0]
    mem_spec = pl.BlockSpec((None, bb, X_WIDTH, N_MEM), lambda i: (layer, i, 0, 0))
    return pl.pallas_call(
        _xattn_decode_kernel,
        grid=(nbatch // bb,),
        in_specs=[pl.BlockSpec((bb, X_QROWS, X_WIDTH), lambda i: (i, 0, 0)), mem_spec, mem_spec,
                  pl.BlockSpec((X_QROWS, X_WIDTH), lambda i: (0, 0))],
        out_specs=pl.BlockSpec((bb, X_WIDTH), lambda i: (i, 0)),
        out_shape=jax.ShapeDtypeStruct((nbatch, X_WIDTH), F32),
        compiler_params=_cparams(("parallel",)),
        name="xattn_decode",
    )(qblk, mem_k, mem_v, hmask)


def _hgrn_gates(hq, hf, loglb, log1mlb):
    ls = jnp.minimum(hf, 0.0) - jnp.log1p(jnp.exp(-jnp.abs(hf)))
    b = log1mlb + ls
    lf = jnp.maximum(loglb, b) + jnp.log1p(jnp.exp(-jnp.abs(loglb - b)))
    return _silu(hq), lf, _neg_expm1(lf)


def _hgrn_prompt_kernel(hq_ref, hf_ref, hi_ref, hg_ref, loglb_ref, log1mlb_ref, gain_ref,
                        bdm_ref, hm_ref, ob_ref, st_ref, st_scr, q_scr, k_scr, cum_scr, o_scr):
    i = pl.program_id(1)
    tt = hq_ref.shape[0]
    c = HG_CHUNK
    tb = HG_TB

    @pl.when(i == 0)
    def _():
        st_scr[...] = jnp.zeros_like(st_scr)

    qs, lf, kk = _hgrn_gates(hq_ref[...], hf_ref[...], loglb_ref[...], log1mlb_ref[...])
    row = lax.broadcasted_iota(jnp.int32, (tt, HG_WIDTH), 0) & (c - 1)
    cum = lf
    sh = 1
    while sh < c:
        cum = cum + jnp.where(row >= sh, pltpu.roll(cum, sh, 0), 0.0)
        sh *= 2
    q_scr[...] = qs
    k_scr[...] = kk
    cum_scr[...] = cum

    bdm = bdm_ref[...]
    hmask = hm_ref[...]
    def chunk(ci, carry):
        r0 = pl.multiple_of(ci * c, c)
        r = pl.ds(r0, c)
        cu = cum_scr[r, :]
        q = q_scr[r, :]
        k = k_scr[r, :]
        v = hi_ref[r, :]
        vb = v.astype(BF16)
        qb = _rb(q)
        cu2 = cu * LOG2E
        last = cu[c - 1:c, :]
        st = st_scr[...]
        o_inter = _bdot_nt(q * jnp.exp(cu), st)
        for j in range(c // tb):
            ns = tb * (j + 1)
            ti = lax.broadcasted_iota(jnp.int32, (tb, ns, HG_WIDTH), 0) + tb * j
            si = lax.broadcasted_iota(jnp.int32, (tb, ns, HG_WIDTH), 1)
            cut = cu2[tb * j:tb * (j + 1), :]
            dec = jnp.exp2(jnp.where(ti >= si, cut[:, None, :] - cu2[None, :ns, :], NEG))
            a2 = (dec * k[None, :ns, :]).astype(BF16)
            q4 = qb[tb * j:tb * (j + 1), None, :] * hmask[None, :, :]
            att = lax.dot_general(q4, a2, (((2,), (2,)), ((0,), (0,))),
                                  preferred_element_type=F32)
            w = jnp.dot(_rb(att.reshape(tb * HG_MROWS, ns)), vb[:ns, :],
                        preferred_element_type=F32).reshape(tb, HG_MROWS, HG_WIDTH)
            o_intra = jnp.sum(w * hmask[None, :, :], axis=1)
            o_scr[pl.ds(r0 + tb * j, tb), :] = o_intra + o_inter[tb * j:tb * (j + 1), :]
        upd = _bdot_tn(v, k * jnp.exp(last - cu))
        st_scr[...] = st * jnp.exp(last) + upd * bdm
        return carry

    lax.fori_loop(0, tt // c, chunk, 0, unroll=2)

    o = o_scr[...]
    ms = jnp.dot(o * o, bdm, precision=lax.Precision.HIGHEST,
                 preferred_element_type=F32) * (1.0 / HEAD_DIM)
    ob_ref[...] = (o * lax.rsqrt(ms + RMS_EPS) * gain_ref[...] * _silu(hg_ref[...])).astype(ob_ref.dtype)

    @pl.when(i == pl.num_programs(1) - 1)
    def _():
        st_ref[...] = st_scr[...]


def hgrn_prompt(proj, loglb, log1mlb, gain4, bdm, hmask, batch, seq, tt):
    nt = seq // tt
    base = (A_HEADS + 2 * A_KV_HEADS) * HEAD_DIM // HG_WIDTH

    def col(cblk):
        return pl.BlockSpec((tt, HG_WIDTH), lambda b, i: (b * nt + i, cblk))

    row_spec = pl.BlockSpec((1, HG_WIDTH), lambda b, i: (0, 0))
    mat_spec = pl.BlockSpec((HG_WIDTH, HG_WIDTH), lambda b, i: (0, 0))
    return pl.pallas_call(
        _hgrn_prompt_kernel,
        grid=(batch, nt),
        in_specs=[col(base), col(base + 1), col(base + 2), col(base + 3),
                  row_spec, row_spec, row_spec, mat_spec,
                  pl.BlockSpec((HG_MROWS, HG_WIDTH), lambda b, i: (0, 0))],
        out_specs=[pl.BlockSpec((tt, HG_WIDTH), lambda b, i: (b * nt + i, 0)),
                   pl.BlockSpec((None, HG_WIDTH, HG_WIDTH), lambda b, i: (b, 0, 0))],
        out_shape=[jax.ShapeDtypeStruct((batch * seq, HG_WIDTH), MIX_DTYPE),
                   jax.ShapeDtypeStruct((batch, HG_WIDTH, HG_WIDTH), F32)],
        scratch_shapes=[pltpu.VMEM((HG_WIDTH, HG_WIDTH), F32),
                        pltpu.VMEM((tt, HG_WIDTH), F32),
                        pltpu.VMEM((tt, HG_WIDTH), F32),
                        pltpu.VMEM((tt, HG_WIDTH), F32),
                        pltpu.VMEM((tt, HG_WIDTH), F32)],
        compiler_params=_cparams(("parallel", "arbitrary")),
        name="hgrn_prompt",
    )(proj, proj, proj, proj, loglb, log1mlb, gain4, bdm, hmask)


def _hgrn_decode_kernel(hq_ref, hf_ref, hi_ref, hg_ref, loglb_ref, log1mlb_ref, gain_ref, s_ref,
                        ob_ref, so_ref):
    nb = hq_ref.shape[1]
    qs, lf, kk = _hgrn_gates(hq_ref[...], hf_ref[...], loglb_ref[...], log1mlb_ref[...])
    v = hi_ref[...]
    f = jnp.exp(lf)
    s = s_ref[...].reshape(HEAD_DIM, HEAD_DIM, nb)
    att = jnp.sum(_rb(qs) * _rb(kk), axis=0, keepdims=True)
    o = _rb(att) * _rb(v) + jnp.sum(_rb(qs * f)[:, None, :] * _rb(s), axis=0)
    s_new = f[:, None, :] * s + _rb(kk)[:, None, :] * _rb(v)[None, :, :]
    so_ref[...] = s_new.reshape(HEAD_DIM * HEAD_DIM, nb)
    ms = jnp.mean(o * o, axis=0, keepdims=True)
    ob_ref[...] = (o * lax.rsqrt(ms + RMS_EPS) * gain_ref[...] * _silu(hg_ref[...])).astype(ob_ref.dtype)


def hgrn_decode(gates_t, loglb_t, log1mlb_t, gain_t, state_t, layer):
    nb = gates_t.shape[1]

    def blk(off):
        return pl.BlockSpec((HEAD_DIM, nb), lambda h: (off * HG_HEADS + h, 0))

    par = pl.BlockSpec((HEAD_DIM, nb), lambda h: (h, 0))
    st = pl.BlockSpec((None, HEAD_DIM * HEAD_DIM, nb), lambda h: (layer, h, 0))
    return pl.pallas_call(
        _hgrn_decode_kernel,
        grid=(HG_HEADS,),
        in_specs=[blk(0), blk(1), blk(2), blk(3), par, par,
                  pl.BlockSpec((HEAD_DIM, nb), lambda h: (0, 0)), st],
        out_specs=[par, st],
        out_shape=[jax.ShapeDtypeStruct((HG_WIDTH, nb), F32),
                   jax.ShapeDtypeStruct(state_t.shape, F32)],
        input_output_aliases={7: 1},
        compiler_params=_cparams(("arbitrary",)),
        name="hgrn_decode",
    )(gates_t, gates_t, gates_t, gates_t, loglb_t, log1mlb_t, gain_t, state_t)


def _lru_gates(xc, wa_ref, ba_ref, wx_ref, bx_ref, lam_ref):
    r = jax.nn.sigmoid(_bdot(xc, wa_ref[...]) + ba_ref[...])
    gi = jax.nn.sigmoid(_bdot(xc, wx_ref[...]) + bx_ref[...])
    log_a = -LRU_C * r * _softplus(-lam_ref[...])
    a = jnp.exp(log_a)
    bterm = jnp.sqrt(_neg_expm1(2.0 * log_a)) * (gi * xc)
    return a, bterm


def _lru_prompt_kernel(lx_ref, lg_ref, cw_ref, cb_ref, wa_ref, ba_ref, wx_ref, bx_ref, lam_ref,
                       oc_ref, hl_ref, ext_scr, h_scr):
    i = pl.program_id(1)
    tt = lx_ref.shape[0]
    pad = SUBLANES

    @pl.when(i == 0)
    def _():
        ext_scr[0:pad, :] = jnp.zeros((pad, LRU_WIDTH), F32)
        h_scr[...] = jnp.zeros_like(h_scr)

    x = lx_ref[...]
    ext_scr[pad:pad + tt, :] = x
    xc = cb_ref[...] + cw_ref[CONV_W - 1:CONV_W, :] * x
    for j in range(CONV_W - 1):
        back = CONV_W - 1 - j
        xc = xc + cw_ref[j:j + 1, :] * ext_scr[pad - back:pad - back + tt, :]
    ext_scr[0:pad, :] = x[tt - pad:tt, :]

    a, bterm = _lru_gates(xc, wa_ref, ba_ref, wx_ref, bx_ref, lam_ref)
    row = lax.broadcasted_iota(jnp.int32, (tt, LRU_WIDTH), 0)
    sh = 1
    while sh < tt:
        keep = row >= sh
        b_s = jnp.where(keep, pltpu.roll(bterm, sh, 0), 0.0)
        a_s = jnp.where(keep, pltpu.roll(a, sh, 0), 1.0)
        bterm = a * b_s + bterm
        a = a * a_s
        sh *= 2
    h = a * h_scr[...] + bterm
    h_scr[...] = h[tt - 1:tt, :]
    oc_ref[...] = (h * _gelu_tanh(lg_ref[...])).astype(oc_ref.dtype)

    @pl.when(i == pl.num_programs(1) - 1)
    def _():
        hl_ref[...] = h[tt - 1:tt, :]


def lru_prompt(proj, conv_w, conv_b, wa_bd, ba, wx_bd, bx, lam, batch, seq, tt):
    nt = seq // tt
    base = IN_COLS // LRU_WIDTH - 2

    def col(cblk):
        return pl.BlockSpec((tt, LRU_WIDTH), lambda b, i: (b * nt + i, cblk))

    row_spec = pl.BlockSpec((1, LRU_WIDTH), lambda b, i: (0, 0))
    mat_spec = pl.BlockSpec((LRU_WIDTH, LRU_WIDTH), lambda b, i: (0, 0))
    return pl.pallas_call(
        _lru_prompt_kernel,
        grid=(batch, nt),
        in_specs=[col(base), col(base + 1),
                  pl.BlockSpec((CONV_W, LRU_WIDTH), lambda b, i: (0, 0)), row_spec,
                  mat_spec, row_spec, mat_spec, row_spec, row_spec],
        out_specs=[pl.BlockSpec((tt, LRU_WIDTH), lambda b, i: (b * nt + i, 0)),
                   pl.BlockSpec((None, 1, LRU_WIDTH), lambda b, i: (b, 0, 0))],
        out_shape=[jax.ShapeDtypeStruct((batch * seq, LRU_WIDTH), MIX_DTYPE),
                   jax.ShapeDtypeStruct((batch, 1, LRU_WIDTH), F32)],
        scratch_shapes=[pltpu.VMEM((tt + 8, LRU_WIDTH), F32),
                        pltpu.VMEM((1, LRU_WIDTH), F32)],
        compiler_params=_cparams(("parallel", "arbitrary")),
        name="lru_prompt",
    )(proj, proj, conv_w, conv_b, wa_bd, ba, wx_bd, bx, lam)


def _lru_decode_kernel(lx_ref, lg_ref, buf_ref, h0_ref, cw_ref, cb_ref, wa_ref, ba_ref, wx_ref,
                       bx_ref, lam_ref, oc_ref, hn_ref, nbuf_ref):
    x = lx_ref[...]
    buf = buf_ref[...]
    xc = cb_ref[...] + cw_ref[CONV_W - 1:CONV_W, :] * x
    for j in range(CONV_W - 1):
        xc = xc + cw_ref[j:j + 1, :] * buf[:, j * LRU_WIDTH:(j + 1) * LRU_WIDTH]
    a, bterm = _lru_gates(xc, wa_ref, ba_ref, wx_ref, bx_ref, lam_ref)
    h = a * h0_ref[...] + bterm
    hn_ref[...] = h
    oc_ref[...] = (h * _gelu_tanh(lg_ref[...])).astype(oc_ref.dtype)
    nbuf_ref[...] = jnp.concatenate([buf[:, LRU_WIDTH:], x], axis=1)


def lru_decode(proj_d, conv_buf, h0, conv_w, conv_b, wa_bd, ba, wx_bd, bx, lam):
    nb = proj_d.shape[0]
    base = IN_COLS // LRU_WIDTH - 2
    row_spec = pl.BlockSpec((1, LRU_WIDTH), lambda i: (0, 0))
    mat_spec = pl.BlockSpec((LRU_WIDTH, LRU_WIDTH), lambda i: (0, 0))
    act = pl.BlockSpec((nb, LRU_WIDTH), lambda i: (0, 0))
    bufs = pl.BlockSpec((nb, (CONV_W - 1) * LRU_WIDTH), lambda i: (0, 0))
    return pl.pallas_call(
        _lru_decode_kernel,
        grid=(1,),
        in_specs=[pl.BlockSpec((nb, LRU_WIDTH), lambda i: (0, base)),
                  pl.BlockSpec((nb, LRU_WIDTH), lambda i: (0, base + 1)),
                  bufs, act, pl.BlockSpec((CONV_W, LRU_WIDTH), lambda i: (0, 0)), row_spec,
                  mat_spec, row_spec, mat_spec, row_spec, row_spec],
        out_specs=[act, act, bufs],
        out_shape=[jax.ShapeDtypeStruct((nb, LRU_WIDTH), F32),
                   jax.ShapeDtypeStruct((nb, LRU_WIDTH), F32),
                   jax.ShapeDtypeStruct((nb, (CONV_W - 1) * LRU_WIDTH), F32)],
        compiler_params=_cparams(("arbitrary",)),
        name="lru_decode",
    )(proj_d, proj_d, conv_buf, h0, conv_w, conv_b, wa_bd, ba, wx_bd, bx, lam)


def _xattn_prompt_kernel(x_ref, wq_ref, mk_ref, mv_ref, wo_ref, g_ref, b_ref, wr_ref, br_ref, o_ref):
    x = x_ref[...]
    q = _bdot(x, wq_ref[...]).astype(BF16)
    mk = mk_ref[...].astype(BF16)
    mv = mv_ref[...].astype(BF16)
    outs = []
    for h in range(X_HEADS):
        sl = slice(h * HEAD_DIM, (h + 1) * HEAD_DIM)
        s = _bdot_nt(q[:, sl], mk[:, sl]) * SCALE
        m = jnp.max(s, -1, keepdims=True)
        p = jnp.exp(s - m)
        p = p / jnp.sum(p, -1, keepdims=True)
        outs.append(_bdot(p, mv[:, sl]))
    o = jnp.concatenate(outs, axis=1)
    y = _layer_norm(ALPHA * x + _bdot(o, wo_ref[...]), g_ref[...], b_ref[...])
    logits_t = _bdot_nt(wr_ref[...], y) + br_ref[...]
    gate_t = _route_rows(logits_t[:ROUTER_ROWS, :])
    gate_t = jnp.concatenate([gate_t, jnp.zeros((ROUTER_LANES - ROUTER_ROWS, y.shape[0]), F32)], axis=0)
    for j in range(D_MODEL // LANES):
        o_ref[j] = y[:, j * LANES:(j + 1) * LANES]
    o_ref[D_MODEL // LANES] = gate_t.T


def xattn_prompt(x, wq, mem_kv, wo, g, b, wr, br, batch, seq, tt):
    nt = seq // tt
    const = lambda bi, i: (0, 0)
    return pl.pallas_call(
        _xattn_prompt_kernel,
        grid=(batch, nt),
        in_specs=[pl.BlockSpec((tt, D_MODEL), lambda bi, i: (bi * nt + i, 0)),
                  pl.BlockSpec((D_MODEL, X_WIDTH), const),
                  pl.BlockSpec((N_MEM, X_WIDTH), lambda bi, i: (bi, 0)),
                  pl.BlockSpec((N_MEM, X_WIDTH), lambda bi, i: (bi, 1)),
                  pl.BlockSpec((X_WIDTH, D_MODEL), const),
                  pl.BlockSpec((1, D_MODEL), const),
                  pl.BlockSpec((1, D_MODEL), const),
                  pl.BlockSpec((ROUTER_LANES, D_MODEL), const),
                  pl.BlockSpec((ROUTER_LANES, 1), const)],
        out_specs=pl.BlockSpec((XG_PLANES, tt, LANES), lambda bi, i: (0, bi * nt + i, 0)),
        out_shape=jax.ShapeDtypeStruct((XG_PLANES, batch * seq, LANES), F32),
        compiler_params=_cparams(("parallel", "parallel")),
        name="xattn_prompt",
    )(x, wq, mem_kv, mem_kv, wo, g, b, wr, br)


def _route(logits):
    lane = lax.broadcasted_iota(jnp.int32, logits.shape, 1)
    big = jnp.int32(ROUTER_LANES)
    ninf = -jnp.inf
    gl = jnp.where(lane < N_GROUPS, logits, ninf)
    gm = jnp.max(gl, -1, keepdims=True)
    g_val = 1.0 / jnp.sum(jnp.exp(gl - gm), -1, keepdims=True)
    g_idx = jnp.min(jnp.where(gl == gm, lane, big), -1, keepdims=True)
    lo = N_GROUPS + EXP_PER_GROUP * g_idx
    el = jnp.where((lane >= lo) & (lane < lo + EXP_PER_GROUP), logits, ninf)
    v1 = jnp.max(el, -1, keepdims=True)
    i1 = jnp.min(jnp.where(el == v1, lane, big), -1, keepdims=True)
    el2 = jnp.where(lane == i1, ninf, el)
    v2 = jnp.max(el2, -1, keepdims=True)
    i2 = jnp.min(jnp.where(el2 == v2, lane, big), -1, keepdims=True)
    e2 = jnp.exp(v2 - v1)
    w1 = g_val / (1.0 + e2)
    w2 = g_val * e2 / (1.0 + e2)
    return jnp.where(lane == i1, w1, 0.0) + jnp.where(lane == i2, w2, 0.0), g_idx


def _route_rows(logits):
    row = lax.broadcasted_iota(jnp.int32, logits.shape, 0)
    big = jnp.int32(ROUTER_ROWS)
    ninf = -jnp.inf
    gl = jnp.where(row < N_GROUPS, logits, ninf)
    gm = jnp.max(gl, 0, keepdims=True)
    g_val = 1.0 / jnp.sum(jnp.exp(gl - gm), 0, keepdims=True)
    g_idx = jnp.min(jnp.where(gl == gm, row, big), 0, keepdims=True)
    lo = N_GROUPS + EXP_PER_GROUP * g_idx
    el = jnp.where((row >= lo) & (row < lo + EXP_PER_GROUP), logits, ninf)
    v1 = jnp.max(el, 0, keepdims=True)
    i1 = jnp.min(jnp.where(el == v1, row, big), 0, keepdims=True)
    el2 = jnp.where(row == i1, ninf, el)
    v2 = jnp.max(el2, 0, keepdims=True)
    i2 = jnp.min(jnp.where(el2 == v2, row, big), 0, keepdims=True)
    e2 = jnp.exp(v2 - v1)
    w1 = g_val / (1.0 + e2)
    w2 = g_val * e2 / (1.0 + e2)
    gate = jnp.where(row == i1, w1, 0.0) + jnp.where(row == i2, w2, 0.0)
    return jnp.where(row == 0, g_idx.astype(F32), gate)


def _moe_dense_kernel(x_ref, wr_ref, br_ref, wg_ref, wu_ref, wd_ref, g_ref, b_ref, o_ref,
                      gate_scr, acc_scr):
    e = pl.program_id(1)

    @pl.when(e == 0)
    def _():
        logits = _bdot(x_ref[...], wr_ref[...]) + br_ref[...]
        gate_scr[...] = _route(logits)[0]
        acc_scr[...] = jnp.zeros_like(acc_scr)

    xb = x_ref[...].astype(BF16)
    lane = lax.broadcasted_iota(jnp.int32, gate_scr.shape, 1)
    gcol = jnp.sum(jnp.where(lane == e + N_GROUPS, gate_scr[...], 0.0), -1, keepdims=True)
    hid = _silu(_bdot(xb, wg_ref[...])) * _bdot(xb, wu_ref[...])
    acc_scr[...] += _bdot(hid * gcol, wd_ref[...])

    @pl.when(e == pl.num_programs(1) - 1)
    def _():
        o_ref[...] = _layer_norm(ALPHA * x_ref[...] + acc_scr[...], g_ref[...], b_ref[...])


def moe_dense(x, wr, br, wg, wu, wd, g, b, layer, tm):
    m = x.shape[0]
    return pl.pallas_call(
        _moe_dense_kernel,
        grid=(m // tm, N_EXPERTS),
        in_specs=[pl.BlockSpec((tm, D_MODEL), lambda i, e: (i, 0)),
                  pl.BlockSpec((D_MODEL, ROUTER_LANES), lambda i, e: (0, 0)),
                  pl.BlockSpec((1, ROUTER_LANES), lambda i, e: (0, 0)),
                  pl.BlockSpec((None, None, D_MODEL, EXP_FF), lambda i, e: (layer, e, 0, 0)),
                  pl.BlockSpec((None, None, D_MODEL, EXP_FF), lambda i, e: (layer, e, 0, 0)),
                  pl.BlockSpec((None, None, EXP_FF, D_MODEL), lambda i, e: (layer, e, 0, 0)),
                  pl.BlockSpec((1, D_MODEL), lambda i, e: (0, 0)),
                  pl.BlockSpec((1, D_MODEL), lambda i, e: (0, 0))],
        out_specs=pl.BlockSpec((tm, D_MODEL), lambda i, e: (i, 0)),
        out_shape=jax.ShapeDtypeStruct((m, D_MODEL), F32),
        scratch_shapes=[pltpu.VMEM((tm, ROUTER_LANES), F32), pltpu.VMEM((tm, D_MODEL), F32)],
        compiler_params=_cparams(("parallel", "arbitrary")),
        name="moe_dense",
    )(x, wr, br, wg, wu, wd, g, b)


def _sc_mesh():
    return plsc.VectorSubcoreMesh(core_axis_name="core", subcore_axis_name="subcore")


def sc_scatter_rows(x, idx, n_out):
    r = x.shape[0]

    @functools.partial(pl.kernel, out_type=jax.ShapeDtypeStruct((n_out, LANES), x.dtype),
                       mesh=_sc_mesh(), scratch_types=[], name="sc_scatter_rows")
    def k(x_hbm, i_hbm, o_hbm):
        def body(x_vmem, i_vmem):
            pltpu.sync_copy(x_vmem, o_hbm.at[i_vmem.at[0]])

        pltpu.emit_pipeline(
            body,
            grid=(r // SC_WINDOW,),
            in_specs=[pl.BlockSpec((SC_WINDOW, LANES), lambda i: (i, 0)),
                      pl.BlockSpec((1, SC_WINDOW), lambda i: (0, i))],
            out_specs=[],
            core_axis_name=("core", "subcore"),
            dimension_semantics=(pltpu.PARALLEL,),
        )(x_hbm, i_hbm)

    return k(x, idx.reshape(1, r))


def sc_gather_rows(table, idx):
    r = idx.shape[0]

    @functools.partial(pl.kernel, out_type=jax.ShapeDtypeStruct((r, LANES), table.dtype),
                       mesh=_sc_mesh(), scratch_types=[], name="sc_gather_rows")
    def k(t_hbm, i_hbm, o_hbm):
        def body(i_vmem, o_vmem):
            pltpu.sync_copy(t_hbm.at[i_vmem.at[0]], o_vmem)

        pltpu.emit_pipeline(
            body,
            grid=(r // SC_WINDOW,),
            in_specs=[pl.BlockSpec((1, SC_WINDOW), lambda i: (0, i))],
            out_specs=[pl.BlockSpec((SC_WINDOW, LANES), lambda i: (i, 0))],
            core_axis_name=("core", "subcore"),
            dimension_semantics=(pltpu.PARALLEL,),
        )(i_hbm, o_hbm)

    return k(table, idx.reshape(1, r))


def _moe_sorted_kernel(tg_ref, nused_ref, xs_ref, wg32_ref, wu32_ref, wd32_ref, g_ref, b_ref, o_ref,
                       wg_ref, wu_ref, wd_ref):
    t = pl.program_id(0)
    used = t < nused_ref[0]

    @pl.when(used & ((t == 0) | (tg_ref[t] != tg_ref[jnp.maximum(t - 1, 0)])))
    def _():
        wg_ref[...] = wg32_ref[...].astype(BF16)
        wu_ref[...] = wu32_ref[...].astype(BF16)
        wd_ref[...] = wd32_ref[...].astype(BF16)

    @pl.when(used)
    def _():
        x = jnp.concatenate([xs_ref[j] for j in range(X_PLANES)], axis=1)
        gate = xs_ref[X_PLANES]
        xb = x.astype(BF16)
        lane = lax.broadcasted_iota(jnp.int32, gate.shape, 1)
        first = N_GROUPS + EXP_PER_GROUP * tg_ref[t]
        acc = jnp.zeros(x.shape, F32)
        for e in range(EXP_PER_GROUP):
            gcol = jnp.sum(jnp.where(lane == first + e, gate, 0.0), -1, keepdims=True)
            hid = _silu(_bdot(xb, wg_ref[e])) * _bdot(xb, wu_ref[e])
            acc = acc + _bdot(hid * gcol, wd_ref[e])
        y = _layer_norm(ALPHA * x + acc, g_ref[...], b_ref[...])
        for j in range(X_PLANES):
            o_ref[j] = y[:, j * LANES:(j + 1) * LANES]


def _group_slots(group_idx, n, tm):
    n_tiles = n // tm + N_GROUPS
    onehot = (group_idx[:, None] == jnp.arange(N_GROUPS)[None, :]).astype(jnp.int32)
    csum = jnp.cumsum(onehot, axis=0)
    counts = csum[-1]
    rank = jnp.sum(onehot * csum, axis=1) - 1
    tiles_g = (counts + tm - 1) // tm
    tile_end = jnp.cumsum(tiles_g)
    slot_base = (tile_end - tiles_g) * tm
    slot = (jnp.sum(onehot * slot_base[None, :], axis=1) + rank).astype(jnp.int32)
    tile_group = jnp.sum((jnp.arange(n_tiles)[:, None] >= tile_end[None, :]).astype(jnp.int32), axis=1)
    tile_group = jnp.minimum(tile_group, N_GROUPS - 1).astype(jnp.int32)
    return slot, tile_group, tile_end[-1:].astype(jnp.int32)


def moe_routed_sc(xg, wg, wu, wd, g, b, layer, n, tm, during_scatter, during_gather, tiles_out):
    slot, tile_group, n_used = _group_slots(xg[X_PLANES, :, 0].astype(jnp.int32), n, tm)
    n_tiles = tile_group.shape[0]
    n_slots = n_tiles * tm
    plane_base = lambda planes: jnp.arange(planes, dtype=jnp.int32)[:, None] * n_slots
    idx_in = (plane_base(XG_PLANES) + slot[None, :]).reshape(-1)
    if tiles_out:
        idx_out = (plane_base(X_PLANES)[None] + slot.reshape(n // SUBLANES, 1, SUBLANES)).reshape(-1)
    else:
        idx_out = (plane_base(X_PLANES) + slot[None, :]).reshape(-1)
    xs = sc_scatter_rows(xg.reshape(XG_PLANES * n, LANES), idx_in, XG_PLANES * n_slots)
    xs = xs.reshape(XG_PLANES, n_slots, LANES)
    xs, side = lax.optimization_barrier((xs, during_scatter()))
    wspec = lambda shp: pl.BlockSpec((None, None, EXP_PER_GROUP) + shp,
                                     lambda t, tg, nu: (layer, tg[t], 0, 0, 0))
    used_tile = lambda t, tg, nu: (0, jnp.minimum(t, nu[0] - 1), 0)
    grid_spec = pltpu.PrefetchScalarGridSpec(
        num_scalar_prefetch=2,
        grid=(n_tiles,),
        in_specs=[pl.BlockSpec((XG_PLANES, tm, LANES), used_tile),
                  wspec((D_MODEL, EXP_FF)), wspec((D_MODEL, EXP_FF)), wspec((EXP_FF, D_MODEL)),
                  pl.BlockSpec((1, D_MODEL), lambda t, tg, nu: (0, 0)),
                  pl.BlockSpec((1, D_MODEL), lambda t, tg, nu: (0, 0))],
        out_specs=pl.BlockSpec((X_PLANES, tm, LANES), used_tile),
        scratch_shapes=[pltpu.VMEM((EXP_PER_GROUP, D_MODEL, EXP_FF), BF16),
                        pltpu.VMEM((EXP_PER_GROUP, D_MODEL, EXP_FF), BF16),
                        pltpu.VMEM((EXP_PER_GROUP, EXP_FF, D_MODEL), BF16)])
    grouped = lambda w: w.reshape(w.shape[0], N_GROUPS, EXP_PER_GROUP, w.shape[2], w.shape[3])
    ys = pl.pallas_call(
        _moe_sorted_kernel,
        grid_spec=grid_spec,
        out_shape=jax.ShapeDtypeStruct((X_PLANES, n_slots, LANES), F32),
        compiler_params=_cparams(("arbitrary",)),
        name="moe_sorted",
    )(tile_group, n_used, xs, grouped(wg), grouped(wu), grouped(wd), g, b)
    y = sc_gather_rows(ys.reshape(X_PLANES * n_slots, LANES), idx_out)
    y, side = lax.optimization_barrier((y, during_gather(side)))
    if tiles_out:
        y = y.reshape(n // SUBLANES, X_PLANES, SUBLANES, LANES).transpose(0, 2, 1, 3)
        return y.reshape(n, D_MODEL), side
    return y.reshape(X_PLANES, n, LANES), side


def _t5_bucket(dist):
    max_exact = N_BUCKETS // 2
    d = jnp.maximum(dist, 0)
    df = jnp.maximum(d, 1).astype(F32)
    log_b = max_exact + (jnp.log(df / max_exact) / math.log(MAX_DISTANCE / max_exact)
                         * (N_BUCKETS - max_exact)).astype(jnp.int32)
    return jnp.where(d < max_exact, d, jnp.minimum(log_b, N_BUCKETS - 1))


def _bucket_lookup(rel_bias, bucket):
    out = jnp.zeros(bucket.shape + (rel_bias.shape[1],), F32)
    for i in range(N_BUCKETS):
        out = jnp.where((bucket == i)[..., None], rel_bias[i].astype(F32), out)
    return out


def _prompt_bias(rel_bias):
    qi = jnp.arange(WINDOW)[:, None]
    kj = jnp.arange(2 * WINDOW)[None, :]
    dist = qi + WINDOW - kj
    bias = _bucket_lookup(rel_bias, _t5_bucket(dist)).transpose(2, 0, 1)
    valid = (dist >= 0) & (dist <= WINDOW)
    return jnp.where(valid[None], bias, NEG)


def _decode_table(rel_bias, attn_sink):
    dist = WINDOW - jnp.arange(WINDOW + 1)
    bias = _bucket_lookup(rel_bias, _t5_bucket(dist)).T
    depth = attn_sink.shape[0]
    wide = lambda v: jnp.broadcast_to(v[..., None], v.shape + (WINDOW,))
    per_layer = lambda t: jnp.broadcast_to(t[None], (depth,) + t.shape)
    return jnp.stack([per_layer(bias[:, :WINDOW]), per_layer(wide(bias[:, WINDOW])),
                      wide(attn_sink.astype(F32))], axis=1)


def _block_ones(width):
    idx = jnp.arange(width) // HEAD_DIM
    return (idx[:, None] == idx[None, :])


def _head_rows_mask():
    head = jnp.arange(HG_WIDTH)[None, :] // HEAD_DIM
    return (head == jnp.arange(HG_MROWS)[:, None]).astype(F32)


def _block_diag(w):
    nblk, s, _ = w.shape
    eye = jnp.eye(nblk, dtype=w.dtype)
    return (eye[:, None, :, None] * w[:, :, None, :]).reshape(nblk * s, nblk * s)


def kernel(x_prompt, x_sample, mem_prompt, cache_win_k, cache_win_v, state_hgrn, state_conv, state_lru, cache_mem_k, cache_mem_v, rel_bias, hg_lb, w_in, attn_sink, hg_gain, conv_w, conv_b, lru_wa, lru_ba, lru_wx, lru_bx, lru_lam, w_out, ln1_g, ln1_b, x_wq, x_wk, x_wv, x_wo, ln2_g, ln2_b, r_gw, r_gb, r_ew, r_eb, e_wg, e_wu, e_wd, ln3_g, ln3_b):
    bp, seq, d = x_prompt.shape
    n_tok = bp * seq
    nd = x_sample.shape[0]
    depth = w_in.shape[0]

    lbs = jnp.cumsum(jax.nn.softmax(hg_lb.astype(F32), axis=0), axis=0)
    lbs = lbs - lbs[0]
    loglb = jnp.log(lbs)
    log1mlb = jnp.log1p(-lbs)
    gain4 = jnp.tile(hg_gain, (1, HG_HEADS))

    bias_p = _prompt_bias(rel_bias)
    bdm256 = _block_ones(HG_WIDTH).astype(F32)
    hmask = _head_rows_mask()

    w_in_b = w_in.astype(BF16)
    w_out_b = w_out.astype(BF16)
    wq_b = x_wq.astype(BF16)
    wkv_b = jnp.concatenate([x_wk, x_wv], axis=-1).astype(BF16)
    wo_b = x_wo.astype(BF16)
    rew = r_ew.transpose(0, 2, 1, 3).reshape(depth, d, N_EXPERTS)
    wr = jnp.concatenate([r_gw, rew, jnp.zeros((depth, d, ROUTER_LANES - N_GROUPS - N_EXPERTS), F32)], -1)
    br = jnp.concatenate([r_gb, r_eb.reshape(depth, N_EXPERTS),
                          jnp.zeros((depth, ROUTER_LANES - N_GROUPS - N_EXPERTS), F32)], -1)

    a_w = A_HEADS * HEAD_DIM
    xp = x_prompt.reshape(bp * seq, d)
    xs = x_sample.reshape(nd, d)
    mem = mem_prompt.reshape(bp * N_MEM, d)
    ckt = cache_win_k.transpose(0, 1, 3, 4, 2).reshape(depth, nd, LANES, WINDOW)
    cvt = cache_win_v.transpose(0, 1, 3, 4, 2).reshape(depth, nd, LANES, WINDOW)
    cmkt = cache_mem_k.transpose(0, 1, 3, 4, 2).reshape(depth, nd, X_WIDTH, N_MEM)
    cmvt = cache_mem_v.transpose(0, 1, 3, 4, 2).reshape(depth, nd, X_WIDTH, N_MEM)
    state_t = state_hgrn.transpose(0, 2, 3, 4, 1).reshape(depth, HG_HEADS * HEAD_DIM * HEAD_DIM, nd)
    dec_tab = _decode_table(rel_bias, attn_sink)
    xq_mask = _head_rows_mask()[:X_QROWS]
    head_group = jnp.arange(A_HEADS) // A_REP

    p_wk, p_wv, p_s, p_cb, p_h, p_mk, p_mv = [], [], [], [], [], [], []
    s_cb, s_h = [], []
    for l in range(depth):
        row = lambda v: v[l].reshape(1, -1)
        wa_bd = _block_diag(lru_wa[l]).astype(BF16)
        wx_bd = _block_diag(lru_wx[l]).astype(BF16)
        lru_args = (conv_w[l], row(conv_b), wa_bd, row(lru_ba), wx_bd, row(lru_bx), row(lru_lam))
        wo_parts = [w_out_b[l, :a_w], w_out_b[l, a_w:a_w + HG_WIDTH], w_out_b[l, a_w + HG_WIDTH:]]

        proj = matmul(xp, w_in_b[l], ROW_TILE, IN_COLS)
        oa = attn_prompt(proj, attn_sink[l], bias_p, bp, seq)
        ob, st = hgrn_prompt(proj, row(loglb), row(log1mlb), row(gain4), bdm256, hmask, bp, seq, ROW_TILE)
        oc, hl = lru_prompt(proj, *lru_args, bp, seq, LRU_TILE)
        xp = proj_res_ln(xp, [oa, ob, oc], wo_parts, row(ln1_g), row(ln1_b), ROW_TILE)
        mkv = matmul(mem, wkv_b[l], N_MEM, 2 * X_WIDTH)
        xg = xattn_prompt(xp, wq_b[l], mkv, wo_b[l], row(ln2_g), row(ln2_b), wr[l].T, br[l][:, None],
                          bp, seq, ROW_TILE)
        def decode_mixers(xs=xs, ckt=ckt, cvt=cvt, state_t=state_t, l=l, lru_args=lru_args,
                          wo_parts=wo_parts, row=row):
            projd = matmul(xs, w_in_b[l], nd, IN_COLS)
            q3 = projd[:, :a_w].reshape(nd, A_HEADS, 1, HEAD_DIM)
            on_group = head_group[None, :, None, None] == jnp.arange(A_KV_HEADS)[None, None, :, None]
            qblk = jnp.where(on_group, q3, 0.0).reshape(nd, A_HEADS, LANES)
            o3, ckt, cvt = attn_decode(qblk, projd, dec_tab[l], ckt, cvt, l, DEC_BLOCK)
            o4 = o3.reshape(nd, A_HEADS, A_KV_HEADS, HEAD_DIM)
            oa = jnp.sum(jnp.where(on_group, o4, 0.0), axis=2).reshape(nd, a_w)
            gates_t = projd[:, a_w + 2 * LANES:a_w + 2 * LANES + 4 * HG_WIDTH].T
            bc = lambda v: jnp.broadcast_to(v[:, None], (v.shape[0], nd))
            ob_t, state_t = hgrn_decode(gates_t, bc(loglb[l]), bc(log1mlb[l]), bc(hg_gain[l]), state_t, l)
            oc, nh, nbuf = lru_decode(projd, state_conv[l].reshape(nd, -1), state_lru[l], *lru_args)
            xs = proj_res_ln(xs, [oa, ob_t.T, oc], wo_parts, row(ln1_g), row(ln1_b), nd)
            return xs, ckt, cvt, state_t, nh, nbuf

        def decode_rest(side, l=l, row=row):
            xs = side[0]
            qd = matmul(xs, wq_b[l], nd, X_WIDTH)
            qdb = qd[:, None, :] * xq_mask[None, :, :]
            od = xattn_decode(qdb, cmkt, cmvt, xq_mask, l, DEC_BLOCK)
            xs = proj_res_ln(xs, [od], [wo_b[l]], row(ln2_g), row(ln2_b), nd)
            xs = moe_dense(xs, wr[l], br[l:l + 1], e_wg, e_wu, e_wd, row(ln3_g), row(ln3_b), l, nd)
            return (xs,) + tuple(side[1:])

        xp, (xs, ckt, cvt, state_t, nh, nbuf) = moe_routed_sc(
            xg, e_wg, e_wu, e_wd, row(ln3_g), row(ln3_b), l, n_tok, MOE_TM, decode_mixers, decode_rest,
            tiles_out=(l == depth - 1))

        proj3 = proj.reshape(bp, seq, IN_COLS)
        p_wk.append(proj3[:, seq - WINDOW:, a_w:a_w + LANES].reshape(bp, WINDOW, A_KV_HEADS, HEAD_DIM))
        p_wv.append(proj3[:, seq - WINDOW:, a_w + LANES:a_w + 2 * LANES].reshape(bp, WINDOW, A_KV_HEADS, HEAD_DIM))
        st5 = st.reshape(bp, HG_HEADS, HEAD_DIM, HG_HEADS, HEAD_DIM)
        p_s.append(jnp.stack([st5[:, h, :, h, :] for h in range(HG_HEADS)], 1).transpose(0, 1, 3, 2))
        p_cb.append(proj3[:, seq - (CONV_W - 1):, IN_COLS - 2 * LRU_WIDTH:IN_COLS - LRU_WIDTH])
        p_h.append(hl.reshape(bp, LRU_WIDTH))
        p_mk.append(mkv[:, :X_WIDTH].reshape(bp, N_MEM, X_HEADS, HEAD_DIM))
        p_mv.append(mkv[:, X_WIDTH:].reshape(bp, N_MEM, X_HEADS, HEAD_DIM))

        s_cb.append(nbuf.reshape(nd, CONV_W - 1, LRU_WIDTH))
        s_h.append(nh)

    unkey = lambda c: c.reshape(depth, nd, A_KV_HEADS, HEAD_DIM, WINDOW).transpose(0, 1, 4, 2, 3)
    s_s = state_t.reshape(depth, HG_HEADS, HEAD_DIM, HEAD_DIM, nd).transpose(0, 4, 1, 2, 3)
    return (xp.reshape(bp, seq, d), xs.reshape(nd, 1, d),
            jnp.stack(p_wk), jnp.stack(p_wv), jnp.stack(p_s), jnp.stack(p_cb), jnp.stack(p_h),
            jnp.stack(p_mk), jnp.stack(p_mv),
            unkey(ckt), unkey(cvt), s_s, jnp.stack(s_cb), jnp.stack(s_h))
```

```python
import functools
import math

import jax
import jax.numpy as jnp
from jax import lax
from jax.experimental import pallas as pl
from jax.experimental.pallas import tpu as pltpu
from jax.experimental.pallas import tpu_sc as plsc

F32 = jnp.float32
BF16 = jnp.bfloat16
MIX_DTYPE = BF16

D_MODEL = 1024
DEPTH = 4
HEAD_DIM = 64
A_HEADS = 8
A_KV_HEADS = 2
A_REP = A_HEADS // A_KV_HEADS
WINDOW = 128
A_QB = 2
N_BUCKETS = 32
MAX_DISTANCE = 128
HG_WIDTH = 256
HG_HEADS = 4
HG_CHUNK = 64
HG_TB = 32
HG_MROWS = 8
LOG2E = math.log2(math.e)
LRU_WIDTH = 256
CONV_W = 4
LRU_C = 8.0
N_MEM = 256
X_HEADS = 4
X_WIDTH = X_HEADS * HEAD_DIM
X_QROWS = 8
N_GROUPS = 4
EXP_PER_GROUP = 4
N_EXPERTS = N_GROUPS * EXP_PER_GROUP
EXP_FF = D_MODEL // 4
ALPHA = (2 * DEPTH) ** 0.25
LN_EPS = 1e-5
RMS_EPS = 1e-6
IN_COLS = 2304
SCALE = HEAD_DIM ** -0.5
NEG = -1e30
LANES = 128
SUBLANES = 8
ROUTER_LANES = 128
ROUTER_ROWS = 32
XG_WIDTH = D_MODEL + ROUTER_LANES
XG_PLANES = XG_WIDTH // LANES
X_PLANES = D_MODEL // LANES
MOE_TM = 512
ROW_TILE = 1024
LRU_TILE = 512
DEC_BLOCK = 16
SC_WINDOW = 128
VMEM_LIMIT = 48 * 1024 * 1024


def _cparams(sem):
    return pltpu.CompilerParams(dimension_semantics=sem, vmem_limit_bytes=VMEM_LIMIT)


def _bdot(a, b):
    return jnp.dot(a.astype(BF16), b.astype(BF16), preferred_element_type=F32)


def _bdot_nt(a, b):
    return lax.dot_general(a.astype(BF16), b.astype(BF16), (((1,), (1,)), ((), ())),
                           preferred_element_type=F32)


def _bdot_tn(a, b):
    return lax.dot_general(a.astype(BF16), b.astype(BF16), (((0,), (0,)), ((), ())),
                           preferred_element_type=F32)


def _rb(x):
    return x.astype(BF16).astype(F32)


def _silu(x):
    return x * jax.nn.sigmoid(x)


def _neg_expm1(x):
    return -jnp.tanh(0.5 * x) * (jnp.exp(x) + 1.0)


def _softplus(x):
    return jnp.maximum(x, 0.0) + jnp.log1p(jnp.exp(-jnp.abs(x)))


def _gelu_tanh(x):
    return 0.5 * x * (1.0 + jnp.tanh(math.sqrt(2.0 / math.pi) * (x + 0.044715 * (x * x * x))))


def _layer_norm(y, g, b):
    mu = jnp.mean(y, -1, keepdims=True)
    yc = y - mu
    var = jnp.mean(yc * yc, -1, keepdims=True)
    return yc * lax.rsqrt(var + LN_EPS) * g + b


def _rows(x_ref):
    if len(x_ref.shape) == 2:
        return x_ref[...]
    return jnp.concatenate([x_ref[j] for j in range(x_ref.shape[0])], axis=1)


def _rows_spec(x, tm, nargs):
    if x.ndim == 2:
        return pl.BlockSpec((tm, x.shape[1]), (lambda i: (i, 0)) if nargs == 1 else (lambda i, j: (i, 0)))
    blk = (x.shape[0], tm, LANES)
    return pl.BlockSpec(blk, (lambda i: (0, i, 0)) if nargs == 1 else (lambda i, j: (0, i, 0)))


def _mm_kernel(x_ref, w_ref, o_ref):
    o_ref[...] = _bdot(_rows(x_ref), w_ref[...])


def matmul(x, w, tm, tn, layer=None):
    m = x.shape[-2]
    k, n = w.shape[-2:]
    if w.ndim == 2:
        w_spec = pl.BlockSpec((k, tn), lambda i, j: (0, j))
    else:
        w_spec = pl.BlockSpec((None, k, tn), lambda i, j: (layer, 0, j))
    return pl.pallas_call(
        _mm_kernel,
        grid=(m // tm, n // tn),
        in_specs=[_rows_spec(x, tm, 2), w_spec],
        out_specs=pl.BlockSpec((tm, tn), lambda i, j: (i, j)),
        out_shape=jax.ShapeDtypeStruct((m, n), F32),
        compiler_params=_cparams(("parallel", "parallel")),
        name="matmul",
    )(x, w)


def _proj_res_ln_kernel(n_in, x_ref, *refs):
    a_refs = refs[:n_in]
    w_refs = refs[n_in:2 * n_in]
    g_ref, b_ref, o_ref = refs[2 * n_in:]
    y = ALPHA * _rows(x_ref)
    for a_ref, w_ref in zip(a_refs, w_refs):
        y = y + _bdot(a_ref[...], w_ref[...])
    o_ref[...] = _layer_norm(y, g_ref[...], b_ref[...])


def proj_res_ln(x, a_list, w_list, g, b, tm, layer=None):
    m = x.shape[-2]
    n_in = len(a_list)
    in_specs = [_rows_spec(x, tm, 1)]
    in_specs += [pl.BlockSpec((tm, a.shape[1]), lambda i: (i, 0)) for a in a_list]
    if layer is None:
        d = w_list[0].shape[1]
        in_specs += [pl.BlockSpec(w.shape, lambda i: (0, 0)) for w in w_list]
    else:
        (w,) = w_list
        d = w.shape[2]
        row0 = 0
        for a in a_list:
            k = a.shape[1]
            in_specs.append(pl.BlockSpec((None, k, d), functools.partial(lambda i, blk: (layer, blk, 0),
                                                                       blk=row0 // k)))
            row0 += k
        w_list = [w] * n_in
    in_specs += [pl.BlockSpec((1, d), lambda i: (0, 0))] * 2
    return pl.pallas_call(
        functools.partial(_proj_res_ln_kernel, n_in),
        grid=(m // tm,),
        in_specs=in_specs,
        out_specs=pl.BlockSpec((tm, d), lambda i: (i, 0)),
        out_shape=jax.ShapeDtypeStruct((m, d), F32),
        compiler_params=_cparams(("parallel",)),
        name="proj_res_ln",
    )(x, *a_list, *w_list, g, b)


def _attn_prompt_kernel(sink_ref, q_ref, kc_ref, kp_ref, vc_ref, vp_ref, bias_ref, o_ref):
    n = pl.program_id(1)
    col = lax.broadcasted_iota(jnp.int32, (WINDOW, 2 * WINDOW), 1)
    first = jnp.where((n == 0) & (col < WINDOW), NEG, 0.0)
    kk = jnp.concatenate([kp_ref[...], kc_ref[...]], axis=0).astype(BF16)
    vv = jnp.concatenate([vp_ref[...], vc_ref[...]], axis=0).astype(BF16)
    q = q_ref[...].astype(BF16)
    for u in range(A_QB):
        rows = slice(u * WINDOW, (u + 1) * WINDOW)
        keys = slice(u * WINDOW, (u + 2) * WINDOW)
        outs = []
        for h in range(A_HEADS):
            g = h // A_REP
            qh = q[rows, h * HEAD_DIM:(h + 1) * HEAD_DIM]
            kg = kk[keys, g * HEAD_DIM:(g + 1) * HEAD_DIM]
            vg = vv[keys, g * HEAD_DIM:(g + 1) * HEAD_DIM]
            s = _bdot_nt(qh, kg) * SCALE + bias_ref[h]
            if u == 0:
                s = s + first
            sink = sink_ref[h]
            m = jnp.maximum(jnp.max(s, -1, keepdims=True), sink)
            p = jnp.exp(s - m)
            den = jnp.sum(p, -1, keepdims=True) + jnp.exp(sink - m)
            outs.append(_bdot(p / den, vg))
        o_ref[rows, :] = jnp.concatenate(outs, axis=1).astype(o_ref.dtype)


def attn_prompt(proj, sink, bias, batch, seq):
    nb = seq // WINDOW
    ns = nb // A_QB
    tq = A_QB * WINDOW
    qcol = 0
    kcol = (A_HEADS * HEAD_DIM) // LANES
    vcol = kcol + 1

    def cur(c):
        return lambda b, n: (b * ns + n, c)

    def prev(c):
        return lambda b, n: (b * nb + jnp.maximum(n * A_QB - 1, 0), c)

    return pl.pallas_call(
        _attn_prompt_kernel,
        grid=(batch, ns),
        in_specs=[pl.BlockSpec(memory_space=pltpu.SMEM),
                  pl.BlockSpec((tq, A_HEADS * HEAD_DIM), cur(qcol)),
                  pl.BlockSpec((tq, LANES), cur(kcol)),
                  pl.BlockSpec((WINDOW, LANES), prev(kcol)),
                  pl.BlockSpec((tq, LANES), cur(vcol)),
                  pl.BlockSpec((WINDOW, LANES), prev(vcol)),
                  pl.BlockSpec((A_HEADS, WINDOW, 2 * WINDOW), lambda b, n: (0, 0, 0))],
        out_specs=pl.BlockSpec((tq, A_HEADS * HEAD_DIM), cur(0)),
        out_shape=jax.ShapeDtypeStruct((batch * seq, A_HEADS * HEAD_DIM), MIX_DTYPE),
        compiler_params=_cparams(("parallel", "parallel")),
        name="attn_prompt",
    )(sink, proj, proj, proj, proj, proj, bias)


def _attn_decode_kernel(qb_ref, kn_ref, vn_ref, knt_ref, vnt_ref, ck_ref, cv_ref, tab_ref,
                        o_ref, ok_ref, ov_ref):
    bb = qb_ref.shape[0]
    ck = ck_ref[...]
    cv = cv_ref[...]
    qb = qb_ref[...]
    kn = kn_ref[...]
    vn = vn_ref[...]
    bias_j = tab_ref[0]
    bias_n = tab_ref[1][:, 0:1]
    sink = tab_ref[2][:, 0:1]
    s = lax.dot_general(qb.astype(BF16), ck.astype(BF16), (((2,), (1,)), ((0,), (0,))),
                        preferred_element_type=F32) * SCALE + bias_j[None]
    sn = jnp.sum(_rb(qb) * _rb(kn)[:, None, :], -1, keepdims=True) * SCALE + bias_n[None]
    m = jnp.maximum(jnp.maximum(jnp.max(s, -1, keepdims=True), sn), sink[None])
    p = jnp.exp(s - m)
    pn = jnp.exp(sn - m)
    den = jnp.sum(p, -1, keepdims=True) + pn + jnp.exp(sink[None] - m)
    o = lax.dot_general((p / den).astype(BF16), cv.astype(BF16), (((2,), (2,)), ((0,), (0,))),
                        preferred_element_type=F32)
    o_ref[...] = o + _rb(pn / den) * _rb(vn)[:, None, :]
    lane = lax.broadcasted_iota(jnp.int32, (LANES, LANES), 1)
    for b in range(bb):
        ok_ref[b] = jnp.where(lane == WINDOW - 1, knt_ref[:, b:b + 1], pltpu.roll(ck[b], WINDOW - 1, 1))
        ov_ref[b] = jnp.where(lane == WINDOW - 1, vnt_ref[:, b:b + 1], pltpu.roll(cv[b], WINDOW - 1, 1))


def attn_decode(qblk, proj_d, table, cache_k, cache_v, layer, bb):
    nbatch = proj_d.shape[0]
    a_w = A_HEADS * HEAD_DIM
    cols = lambda c: proj_d[:, c:c + LANES].reshape(nbatch // bb, bb, LANES).transpose(0, 2, 1)
    knt, vnt = cols(a_w), cols(a_w + LANES)
    kcol = (A_HEADS * HEAD_DIM) // LANES
    cache_spec = pl.BlockSpec((None, bb, LANES, WINDOW), lambda i: (layer, i, 0, 0))
    col_spec = pl.BlockSpec((None, LANES, bb), lambda i: (i, 0, 0))
    return pl.pallas_call(
        _attn_decode_kernel,
        grid=(nbatch // bb,),
        in_specs=[pl.BlockSpec((bb, A_HEADS, LANES), lambda i: (i, 0, 0)),
                  pl.BlockSpec((bb, LANES), lambda i: (i, kcol)),
                  pl.BlockSpec((bb, LANES), lambda i: (i, kcol + 1)),
                  col_spec, col_spec, cache_spec, cache_spec,
                  pl.BlockSpec((3, A_HEADS, WINDOW), lambda i: (0, 0, 0))],
        out_specs=[pl.BlockSpec((bb, A_HEADS, LANES), lambda i: (i, 0, 0)), cache_spec, cache_spec],
        out_shape=[jax.ShapeDtypeStruct((nbatch, A_HEADS, LANES), F32),
                   jax.ShapeDtypeStruct(cache_k.shape, F32),
                   jax.ShapeDtypeStruct(cache_v.shape, F32)],
        input_output_aliases={5: 1, 6: 2},
        compiler_params=_cparams(("arbitrary",)),
        name="attn_decode",
    )(qblk, proj_d, proj_d, knt, vnt, cache_k, cache_v, table)


def _xattn_decode_kernel(qb_ref, mk_ref, mv_ref, hm_ref, o_ref):
    qb = qb_ref[...]
    s = lax.dot_general(qb.astype(BF16), mk_ref[...].astype(BF16), (((2,), (1,)), ((0,), (0,))),
                        preferred_element_type=F32) * SCALE
    m = jnp.max(s, -1, keepdims=True)
    p = jnp.exp(s - m)
    p = p / jnp.sum(p, -1, keepdims=True)
    o = lax.dot_general(p.astype(BF16), mv_ref[...].astype(BF16), (((2,), (2,)), ((0,), (0,))),
                        preferred_element_type=F32)
    o_ref[...] = jnp.sum(o * hm_ref[...][None], axis=1)


def xattn_decode(qblk, mem_k, mem_v, hmask, layer, bb):
    nbatch = qblk.shape[0]
    mem_spec = pl.BlockSpec((None, bb, X_WIDTH, N_MEM), lambda i: (layer, i, 0, 0))
    return pl.pallas_call(
        _xattn_decode_kernel,
        grid=(nbatch // bb,),
        in_specs=[pl.BlockSpec((bb, X_QROWS, X_WIDTH), lambda i: (i, 0, 0)), mem_spec, mem_spec,
                  pl.BlockSpec((X_QROWS, X_WIDTH), lambda i: (0, 0))],
        out_specs=pl.BlockSpec((bb, X_WIDTH), lambda i: (i, 0)),
        out_shape=jax.ShapeDtypeStruct((nbatch, X_WIDTH), F32),
        compiler_params=_cparams(("parallel",)),
        name="xattn_decode",
    )(qblk, mem_k, mem_v, hmask)


def _hgrn_gates(hq, hf, loglb, log1mlb):
    ls = jnp.minimum(hf, 0.0) - jnp.log1p(jnp.exp(-jnp.abs(hf)))
    b = log1mlb + ls
    lf = jnp.maximum(loglb, b) + jnp.log1p(jnp.exp(-jnp.abs(loglb - b)))
    return _silu(hq), lf, _neg_expm1(lf)


def _hgrn_prompt_kernel(hq_ref, hf_ref, hi_ref, hg_ref, loglb_ref, log1mlb_ref, gain_ref,
                        bdm_ref, hm_ref, ob_ref, st_ref, st_scr, q_scr, k_scr, cum_scr, o_scr):
    i = pl.program_id(1)
    tt = hq_ref.shape[0]
    c = HG_CHUNK
    tb = HG_TB

    @pl.when(i == 0)
    def _():
        st_scr[...] = jnp.zeros_like(st_scr)

    qs, lf, kk = _hgrn_gates(hq_ref[...], hf_ref[...], loglb_ref[...], log1mlb_ref[...])
    row = lax.broadcasted_iota(jnp.int32, (tt, HG_WIDTH), 0) & (c - 1)
    cum = lf
    sh = 1
    while sh < c:
        cum = cum + jnp.where(row >= sh, pltpu.roll(cum, sh, 0), 0.0)
        sh *= 2
    q_scr[...] = qs
    k_scr[...] = kk
    cum_scr[...] = cum

    bdm = bdm_ref[...]
    hmask = hm_ref[...]
    def chunk(ci, carry):
        r0 = pl.multiple_of(ci * c, c)
        r = pl.ds(r0, c)
        cu = cum_scr[r, :]
        q = q_scr[r, :]
        k = k_scr[r, :]
        v = hi_ref[r, :]
        vb = v.astype(BF16)
        qb = _rb(q)
        cu2 = cu * LOG2E
        last = cu[c - 1:c, :]
        st = st_scr[...]
        o_inter = _bdot_nt(q * jnp.exp(cu), st)
        for j in range(c // tb):
            ns = tb * (j + 1)
            ti = lax.broadcasted_iota(jnp.int32, (tb, ns, HG_WIDTH), 0) + tb * j
            si = lax.broadcasted_iota(jnp.int32, (tb, ns, HG_WIDTH), 1)
            cut = cu2[tb * j:tb * (j + 1), :]
            dec = jnp.exp2(jnp.where(ti >= si, cut[:, None, :] - cu2[None, :ns, :], NEG))
            a2 = (dec * k[None, :ns, :]).astype(BF16)
            q4 = qb[tb * j:tb * (j + 1), None, :] * hmask[None, :, :]
            att = lax.dot_general(q4, a2, (((2,), (2,)), ((0,), (0,))),
                                  preferred_element_type=F32)
            w = jnp.dot(_rb(att.reshape(tb * HG_MROWS, ns)), vb[:ns, :],
                        preferred_element_type=F32).reshape(tb, HG_MROWS, HG_WIDTH)
            o_intra = jnp.sum(w * hmask[None, :, :], axis=1)
            o_scr[pl.ds(r0 + tb * j, tb), :] = o_intra + o_inter[tb * j:tb * (j + 1), :]
        upd = _bdot_tn(v, k * jnp.exp(last - cu))
        st_scr[...] = st * jnp.exp(last) + upd * bdm
        return carry

    lax.fori_loop(0, tt // c, chunk, 0, unroll=2)

    o = o_scr[...]
    ms = jnp.dot(o * o, bdm, precision=lax.Precision.HIGHEST,
                 preferred_element_type=F32) * (1.0 / HEAD_DIM)
    ob_ref[...] = (o * lax.rsqrt(ms + RMS_EPS) * gain_ref[...] * _silu(hg_ref[...])).astype(ob_ref.dtype)

    @pl.when(i == pl.num_programs(1) - 1)
    def _():
        st_ref[...] = st_scr[...]


def hgrn_prompt(proj, loglb, log1mlb, gain4, bdm, hmask, batch, seq, tt):
    nt = seq // tt
    base = (A_HEADS + 2 * A_KV_HEADS) * HEAD_DIM // HG_WIDTH

    def col(cblk):
        return pl.BlockSpec((tt, HG_WIDTH), lambda b, i: (b * nt + i, cblk))

    row_spec = pl.BlockSpec((1, HG_WIDTH), lambda b, i: (0, 0))
    mat_spec = pl.BlockSpec((HG_WIDTH, HG_WIDTH), lambda b, i: (0, 0))
    return pl.pallas_call(
        _hgrn_prompt_kernel,
        grid=(batch, nt),
        in_specs=[col(base), col(base + 1), col(base + 2), col(base + 3),
                  row_spec, row_spec, row_spec, mat_spec,
                  pl.BlockSpec((HG_MROWS, HG_WIDTH), lambda b, i: (0, 0))],
        out_specs=[pl.BlockSpec((tt, HG_WIDTH), lambda b, i: (b * nt + i, 0)),
                   pl.BlockSpec((None, HG_WIDTH, HG_WIDTH), lambda b, i: (b, 0, 0))],
        out_shape=[jax.ShapeDtypeStruct((batch * seq, HG_WIDTH), MIX_DTYPE),
                   jax.ShapeDtypeStruct((batch, HG_WIDTH, HG_WIDTH), F32)],
        scratch_shapes=[pltpu.VMEM((HG_WIDTH, HG_WIDTH), F32),
                        pltpu.VMEM((tt, HG_WIDTH), F32),
                        pltpu.VMEM((tt, HG_WIDTH), F32),
                        pltpu.VMEM((tt, HG_WIDTH), F32),
                        pltpu.VMEM((tt, HG_WIDTH), F32)],
        compiler_params=_cparams(("parallel", "arbitrary")),
        name="hgrn_prompt",
    )(proj, proj, proj, proj, loglb, log1mlb, gain4, bdm, hmask)


def _hgrn_decode_kernel(hq_ref, hf_ref, hi_ref, hg_ref, loglb_ref, log1mlb_ref, gain_ref, s_ref,
                        ob_ref, so_ref):
    nb = hq_ref.shape[1]
    qs, lf, kk = _hgrn_gates(hq_ref[...], hf_ref[...], loglb_ref[...], log1mlb_ref[...])
    v = hi_ref[...]
    f = jnp.exp(lf)
    s = s_ref[...].reshape(HEAD_DIM, HEAD_DIM, nb)
    att = jnp.sum(_rb(qs) * _rb(kk), axis=0, keepdims=True)
    o = _rb(att) * _rb(v) + jnp.sum(_rb(qs * f)[:, None, :] * _rb(s), axis=0)
    s_new = f[:, None, :] * s + _rb(kk)[:, None, :] * _rb(v)[None, :, :]
    so_ref[...] = s_new.reshape(HEAD_DIM * HEAD_DIM, nb)
    ms = jnp.mean(o * o, axis=0, keepdims=True)
    ob_ref[...] = (o * lax.rsqrt(ms + RMS_EPS) * gain_ref[...] * _silu(hg_ref[...])).astype(ob_ref.dtype)


def hgrn_decode(gates_t, loglb_t, log1mlb_t, gain_t, state_t, layer):
    nb = gates_t.shape[1]

    def blk(off):
        return pl.BlockSpec((HEAD_DIM, nb), lambda h: (off * HG_HEADS + h, 0))

    par = pl.BlockSpec((HEAD_DIM, nb), lambda h: (h, 0))
    st = pl.BlockSpec((None, HEAD_DIM * HEAD_DIM, nb), lambda h: (layer, h, 0))
    return pl.pallas_call(
        _hgrn_decode_kernel,
        grid=(HG_HEADS,),
        in_specs=[blk(0), blk(1), blk(2), blk(3), par, par,
                  pl.BlockSpec((HEAD_DIM, nb), lambda h: (0, 0)), st],
        out_specs=[par, st],
        out_shape=[jax.ShapeDtypeStruct((HG_WIDTH, nb), F32),
                   jax.ShapeDtypeStruct(state_t.shape, F32)],
        input_output_aliases={7: 1},
        compiler_params=_cparams(("arbitrary",)),
        name="hgrn_decode",
    )(gates_t, gates_t, gates_t, gates_t, loglb_t, log1mlb_t, gain_t, state_t)


def _lru_gates(xc, wa_ref, ba_ref, wx_ref, bx_ref, lam_ref):
    r = jax.nn.sigmoid(_bdot(xc, wa_ref[...]) + ba_ref[...])
    gi = jax.nn.sigmoid(_bdot(xc, wx_ref[...]) + bx_ref[...])
    log_a = -LRU_C * r * _softplus(-lam_ref[...])
    a = jnp.exp(log_a)
    bterm = jnp.sqrt(_neg_expm1(2.0 * log_a)) * (gi * xc)
    return a, bterm


def _lru_prompt_kernel(lx_ref, lg_ref, cw_ref, cb_ref, wa_ref, ba_ref, wx_ref, bx_ref, lam_ref,
                       oc_ref, hl_ref, ext_scr, h_scr):
    i = pl.program_id(1)
    tt = lx_ref.shape[0]
    pad = SUBLANES

    @pl.when(i == 0)
    def _():
        ext_scr[0:pad, :] = jnp.zeros((pad, LRU_WIDTH), F32)
        h_scr[...] = jnp.zeros_like(h_scr)

    x = lx_ref[...]
    ext_scr[pad:pad + tt, :] = x
    xc = cb_ref[...] + cw_ref[CONV_W - 1:CONV_W, :] * x
    for j in range(CONV_W - 1):
        back = CONV_W - 1 - j
        xc = xc + cw_ref[j:j + 1, :] * ext_scr[pad - back:pad - back + tt, :]
    ext_scr[0:pad, :] = x[tt - pad:tt, :]

    a, bterm = _lru_gates(xc, wa_ref, ba_ref, wx_ref, bx_ref, lam_ref)
    row = lax.broadcasted_iota(jnp.int32, (tt, LRU_WIDTH), 0)
    sh = 1
    while sh < tt:
        keep = row >= sh
        b_s = jnp.where(keep, pltpu.roll(bterm, sh, 0), 0.0)
        a_s = jnp.where(keep, pltpu.roll(a, sh, 0), 1.0)
        bterm = a * b_s + bterm
        a = a * a_s
        sh *= 2
    h = a * h_scr[...] + bterm
    h_scr[...] = h[tt - 1:tt, :]
    oc_ref[...] = (h * _gelu_tanh(lg_ref[...])).astype(oc_ref.dtype)

    @pl.when(i == pl.num_programs(1) - 1)
    def _():
        hl_ref[...] = h[tt - 1:tt, :]


def lru_prompt(proj, conv_w, conv_b, wa_bd, ba, wx_bd, bx, lam, batch, seq, tt):
    nt = seq // tt
    base = IN_COLS // LRU_WIDTH - 2

    def col(cblk):
        return pl.BlockSpec((tt, LRU_WIDTH), lambda b, i: (b * nt + i, cblk))

    row_spec = pl.BlockSpec((1, LRU_WIDTH), lambda b, i: (0, 0))
    mat_spec = pl.BlockSpec((LRU_WIDTH, LRU_WIDTH), lambda b, i: (0, 0))
    return pl.pallas_call(
        _lru_prompt_kernel,
        grid=(batch, nt),
        in_specs=[col(base), col(base + 1),
                  pl.BlockSpec((CONV_W, LRU_WIDTH), lambda b, i: (0, 0)), row_spec,
                  mat_spec, row_spec, mat_spec, row_spec, row_spec],
        out_specs=[pl.BlockSpec((tt, LRU_WIDTH), lambda b, i: (b * nt + i, 0)),
                   pl.BlockSpec((None, 1, LRU_WIDTH), lambda b, i: (b, 0, 0))],
        out_shape=[jax.ShapeDtypeStruct((batch * seq, LRU_WIDTH), MIX_DTYPE),
                   jax.ShapeDtypeStruct((batch, 1, LRU_WIDTH), F32)],
        scratch_shapes=[pltpu.VMEM((tt + 8, LRU_WIDTH), F32),
                        pltpu.VMEM((1, LRU_WIDTH), F32)],
        compiler_params=_cparams(("parallel", "arbitrary")),
        name="lru_prompt",
    )(proj, proj, conv_w, conv_b, wa_bd, ba, wx_bd, bx, lam)


def _lru_decode_kernel(lx_ref, lg_ref, buf_ref, h0_ref, cw_ref, cb_ref, wa_ref, ba_ref, wx_ref,
                       bx_ref, lam_ref, oc_ref, hn_ref, nbuf_ref):
    x = lx_ref[...]
    buf = buf_ref[...]
    xc = cb_ref[...] + cw_ref[CONV_W - 1:CONV_W, :] * x
    for j in range(CONV_W - 1):
        xc = xc + cw_ref[j:j + 1, :] * buf[:, j * LRU_WIDTH:(j + 1) * LRU_WIDTH]
    a, bterm = _lru_gates(xc, wa_ref, ba_ref, wx_ref, bx_ref, lam_ref)
    h = a * h0_ref[...] + bterm
    hn_ref[...] = h
    oc_ref[...] = (h * _gelu_tanh(lg_ref[...])).astype(oc_ref.dtype)
    nbuf_ref[...] = jnp.concatenate([buf[:, LRU_WIDTH:], x], axis=1)


def lru_decode(proj_d, conv_buf, h0, conv_w, conv_b, wa_bd, ba, wx_bd, bx, lam):
    nb = proj_d.shape[0]
    base = IN_COLS // LRU_WIDTH - 2
    row_spec = pl.BlockSpec((1, LRU_WIDTH), lambda i: (0, 0))
    mat_spec = pl.BlockSpec((LRU_WIDTH, LRU_WIDTH), lambda i: (0, 0))
    act = pl.BlockSpec((nb, LRU_WIDTH), lambda i: (0, 0))
    bufs = pl.BlockSpec((nb, (CONV_W - 1) * LRU_WIDTH), lambda i: (0, 0))
    return pl.pallas_call(
        _lru_decode_kernel,
        grid=(1,),
        in_specs=[pl.BlockSpec((nb, LRU_WIDTH), lambda i: (0, base)),
                  pl.BlockSpec((nb, LRU_WIDTH), lambda i: (0, base + 1)),
                  bufs, act, pl.BlockSpec((CONV_W, LRU_WIDTH), lambda i: (0, 0)), row_spec,
                  mat_spec, row_spec, mat_spec, row_spec, row_spec],
        out_specs=[act, act, bufs],
        out_shape=[jax.ShapeDtypeStruct((nb, LRU_WIDTH), F32),
                   jax.ShapeDtypeStruct((nb, LRU_WIDTH), F32),
                   jax.ShapeDtypeStruct((nb, (CONV_W - 1) * LRU_WIDTH), F32)],
        compiler_params=_cparams(("arbitrary",)),
        name="lru_decode",
    )(proj_d, proj_d, conv_buf, h0, conv_w, conv_b, wa_bd, ba, wx_bd, bx, lam)


def _xattn_prompt_kernel(x_ref, wq_ref, mk_ref, mv_ref, wo_ref, g_ref, b_ref, wr_ref, br_ref, o_ref):
    x = x_ref[...]
    q = _bdot(x, wq_ref[...]).astype(BF16)
    mk = mk_ref[...].astype(BF16)
    mv = mv_ref[...].astype(BF16)
    outs = []
    for h in range(X_HEADS):
        sl = slice(h * HEAD_DIM, (h + 1) * HEAD_DIM)
        s = _bdot_nt(q[:, sl], mk[:, sl]) * SCALE
        m = jnp.max(s, -1, keepdims=True)
        p = jnp.exp(s - m)
        p = p / jnp.sum(p, -1, keepdims=True)
        outs.append(_bdot(p, mv[:, sl]))
    o = jnp.concatenate(outs, axis=1)
    y = _layer_norm(ALPHA * x + _bdot(o, wo_ref[...]), g_ref[...], b_ref[...])
    logits_t = _bdot_nt(wr_ref[...], y) + br_ref[...]
    gate_t = _route_rows(logits_t[:ROUTER_ROWS, :])
    gate_t = jnp.concatenate([gate_t, jnp.zeros((ROUTER_LANES - ROUTER_ROWS, y.shape[0]), F32)], axis=0)
    for j in range(D_MODEL // LANES):
        o_ref[j] = y[:, j * LANES:(j + 1) * LANES]
    o_ref[D_MODEL // LANES] = gate_t.T


def xattn_prompt(x, wq, mem_kv, wo, g, b, wr, br, batch, seq, tt):
    nt = seq // tt
    const = lambda bi, i: (0, 0)
    return pl.pallas_call(
        _xattn_prompt_kernel,
        grid=(batch, nt),
        in_specs=[pl.BlockSpec((tt, D_MODEL), lambda bi, i: (bi * nt + i, 0)),
                  pl.BlockSpec((D_MODEL, X_WIDTH), const),
                  pl.BlockSpec((N_MEM, X_WIDTH), lambda bi, i: (bi, 0)),
                  pl.BlockSpec((N_MEM, X_WIDTH), lambda bi, i: (bi, 1)),
                  pl.BlockSpec((X_WIDTH, D_MODEL), const),
                  pl.BlockSpec((1, D_MODEL), const),
                  pl.BlockSpec((1, D_MODEL), const),
                  pl.BlockSpec((ROUTER_LANES, D_MODEL), const),
                  pl.BlockSpec((ROUTER_LANES, 1), const)],
        out_specs=pl.BlockSpec((XG_PLANES, tt, LANES), lambda bi, i: (0, bi * nt + i, 0)),
        out_shape=jax.ShapeDtypeStruct((XG_PLANES, batch * seq, LANES), F32),
        compiler_params=_cparams(("parallel", "parallel")),
        name="xattn_prompt",
    )(x, wq, mem_kv, mem_kv, wo, g, b, wr, br)


def _route(logits):
    lane = lax.broadcasted_iota(jnp.int32, logits.shape, 1)
    big = jnp.int32(ROUTER_LANES)
    ninf = -jnp.inf
    gl = jnp.where(lane < N_GROUPS, logits, ninf)
    gm = jnp.max(gl, -1, keepdims=True)
    g_val = 1.0 / jnp.sum(jnp.exp(gl - gm), -1, keepdims=True)
    g_idx = jnp.min(jnp.where(gl == gm, lane, big), -1, keepdims=True)
    lo = N_GROUPS + EXP_PER_GROUP * g_idx
    el = jnp.where((lane >= lo) & (lane < lo + EXP_PER_GROUP), logits, ninf)
    v1 = jnp.max(el, -1, keepdims=True)
    i1 = jnp.min(jnp.where(el == v1, lane, big), -1, keepdims=True)
    el2 = jnp.where(lane == i1, ninf, el)
    v2 = jnp.max(el2, -1, keepdims=True)
    i2 = jnp.min(jnp.where(el2 == v2, lane, big), -1, keepdims=True)
    e2 = jnp.exp(v2 - v1)
    w1 = g_val / (1.0 + e2)
    w2 = g_val * e2 / (1.0 + e2)
    return jnp.where(lane == i1, w1, 0.0) + jnp.where(lane == i2, w2, 0.0), g_idx


def _route_rows(logits):
    row = lax.broadcasted_iota(jnp.int32, logits.shape, 0)
    big = jnp.int32(ROUTER_ROWS)
    ninf = -jnp.inf
    gl = jnp.where(row < N_GROUPS, logits, ninf)
    gm = jnp.max(gl, 0, keepdims=True)
    g_val = 1.0 / jnp.sum(jnp.exp(gl - gm), 0, keepdims=True)
    g_idx = jnp.min(jnp.where(gl == gm, row, big), 0, keepdims=True)
    lo = N_GROUPS + EXP_PER_GROUP * g_idx
    el = jnp.where((row >= lo) & (row < lo + EXP_PER_GROUP), logits, ninf)
    v1 = jnp.max(el, 0, keepdims=True)
    i1 = jnp.min(jnp.where(el == v1, row, big), 0, keepdims=True)
    el2 = jnp.where(row == i1, ninf, el)
    v2 = jnp.max(el2, 0, keepdims=True)
    i2 = jnp.min(jnp.where(el2 == v2, row, big), 0, keepdims=True)
    e2 = jnp.exp(v2 - v1)
    w1 = g_val / (1.0 + e2)
    w2 = g_val * e2 / (1.0 + e2)
    gate = jnp.where(row == i1, w1, 0.0) + jnp.where(row == i2, w2, 0.0)
    return jnp.where(row == 0, g_idx.astype(F32), gate)


def _moe_dense_kernel(x_ref, wr_ref, br_ref, wg_ref, wu_ref, wd_ref, g_ref, b_ref, o_ref,
                      gate_scr, acc_scr):
    e = pl.program_id(1)

    @pl.when(e == 0)
    def _():
        logits = _bdot(x_ref[...], wr_ref[...]) + br_ref[...]
        gate_scr[...] = _route(logits)[0]
        acc_scr[...] = jnp.zeros_like(acc_scr)

    xb = x_ref[...].astype(BF16)
    lane = lax.broadcasted_iota(jnp.int32, gate_scr.shape, 1)
    gcol = jnp.sum(jnp.where(lane == e + N_GROUPS, gate_scr[...], 0.0), -1, keepdims=True)
    hid = _silu(_bdot(xb, wg_ref[...])) * _bdot(xb, wu_ref[...])
    acc_scr[...] += _bdot(hid * gcol, wd_ref[...])

    @pl.when(e == pl.num_programs(1) - 1)
    def _():
        o_ref[...] = _layer_norm(ALPHA * x_ref[...] + acc_scr[...], g_ref[...], b_ref[...])


def moe_dense(x, wr, br, wg, wu, wd, g, b, layer, tm):
    m = x.shape[0]
    return pl.pallas_call(
        _moe_dense_kernel,
        grid=(m // tm, N_EXPERTS),
        in_specs=[pl.BlockSpec((tm, D_MODEL), lambda i, e: (i, 0)),
                  pl.BlockSpec((D_MODEL, ROUTER_LANES), lambda i, e: (0, 0)),
                  pl.BlockSpec((1, ROUTER_LANES), lambda i, e: (0, 0)),
                  pl.BlockSpec((None, None, D_MODEL, EXP_FF), lambda i, e: (layer, e, 0, 0)),
                  pl.BlockSpec((None, None, D_MODEL, EXP_FF), lambda i, e: (layer, e, 0, 0)),
                  pl.BlockSpec((None, None, EXP_FF, D_MODEL), lambda i, e: (layer, e, 0, 0)),
                  pl.BlockSpec((1, D_MODEL), lambda i, e: (0, 0)),
                  pl.BlockSpec((1, D_MODEL), lambda i, e: (0, 0))],
        out_specs=pl.BlockSpec((tm, D_MODEL), lambda i, e: (i, 0)),
        out_shape=jax.ShapeDtypeStruct((m, D_MODEL), F32),
        scratch_shapes=[pltpu.VMEM((tm, ROUTER_LANES), F32), pltpu.VMEM((tm, D_MODEL), F32)],
        compiler_params=_cparams(("parallel", "arbitrary")),
        name="moe_dense",
    )(x, wr, br, wg, wu, wd, g, b)


def _sc_mesh():
    return plsc.VectorSubcoreMesh(core_axis_name="core", subcore_axis_name="subcore")


def sc_scatter_rows(x, idx, n_out):
    r = x.shape[0]

    @functools.partial(pl.kernel, out_type=jax.ShapeDtypeStruct((n_out, LANES), x.dtype),
                       mesh=_sc_mesh(), scratch_types=[], name="sc_scatter_rows")
    def k(x_hbm, i_hbm, o_hbm):
        def body(x_vmem, i_vmem):
            pltpu.sync_copy(x_vmem, o_hbm.at[i_vmem.at[0]])

        pltpu.emit_pipeline(
            body,
            grid=(r // SC_WINDOW,),
            in_specs=[pl.BlockSpec((SC_WINDOW, LANES), lambda i: (i, 0)),
                      pl.BlockSpec((1, SC_WINDOW), lambda i: (0, i))],
            out_specs=[],
            core_axis_name=("core", "subcore"),
            dimension_semantics=(pltpu.PARALLEL,),
        )(x_hbm, i_hbm)

    return k(x, idx.reshape(1, r))


def sc_gather_rows(table, idx):
    r = idx.shape[0]

    @functools.partial(pl.kernel, out_type=jax.ShapeDtypeStruct((r, LANES), table.dtype),
                       mesh=_sc_mesh(), scratch_types=[], name="sc_gather_rows")
    def k(t_hbm, i_hbm, o_hbm):
        def body(i_vmem, o_vmem):
            pltpu.sync_copy(t_hbm.at[i_vmem.at[0]], o_vmem)

        pltpu.emit_pipeline(
            body,
            grid=(r // SC_WINDOW,),
            in_specs=[pl.BlockSpec((1, SC_WINDOW), lambda i: (0, i))],
            out_specs=[pl.BlockSpec((SC_WINDOW, LANES), lambda i: (i, 0))],
            core_axis_name=("core", "subcore"),
            dimension_semantics=(pltpu.PARALLEL,),
        )(i_hbm, o_hbm)

    return k(table, idx.reshape(1, r))


def _moe_sorted_kernel(tg_ref, nused_ref, xs_ref, wg32_ref, wu32_ref, wd32_ref, g_ref, b_ref, o_ref,
                       wg_ref, wu_ref, wd_ref):
    t = pl.program_id(0)
    used = t < nused_ref[0]

    @pl.when(used & ((t == 0) | (tg_ref[t] != tg_ref[jnp.maximum(t - 1, 0)])))
    def _():
        wg_ref[...] = wg32_ref[...].astype(BF16)
        wu_ref[...] = wu32_ref[...].astype(BF16)
        wd_ref[...] = wd32_ref[...].astype(BF16)

    @pl.when(used)
    def _():
        x = jnp.concatenate([xs_ref[j] for j in range(X_PLANES)], axis=1)
        gate = xs_ref[X_PLANES]
        xb = x.astype(BF16)
        lane = lax.broadcasted_iota(jnp.int32, gate.shape, 1)
        first = N_GROUPS + EXP_PER_GROUP * tg_ref[t]
        acc = jnp.zeros(x.shape, F32)
        for e in range(EXP_PER_GROUP):
            gcol = jnp.sum(jnp.where(lane == first + e, gate, 0.0), -1, keepdims=True)
            hid = _silu(_bdot(xb, wg_ref[e])) * _bdot(xb, wu_ref[e])
            acc = acc + _bdot(hid * gcol, wd_ref[e])
        y = _layer_norm(ALPHA * x + acc, g_ref[...], b_ref[...])
        for j in range(X_PLANES):
            o_ref[j] = y[:, j * LANES:(j + 1) * LANES]


def _group_slots(group_idx, n, tm):
    n_tiles = n // tm + N_GROUPS
    onehot = (group_idx[:, None] == jnp.arange(N_GROUPS)[None, :]).astype(jnp.int32)
    csum = jnp.cumsum(onehot, axis=0)
    counts = csum[-1]
    rank = jnp.sum(onehot * csum, axis=1) - 1
    tiles_g = (counts + tm - 1) // tm
    tile_end = jnp.cumsum(tiles_g)
    slot_base = (tile_end - tiles_g) * tm
    slot = (jnp.sum(onehot * slot_base[None, :], axis=1) + rank).astype(jnp.int32)
    tile_group = jnp.sum((jnp.arange(n_tiles)[:, None] >= tile_end[None, :]).astype(jnp.int32), axis=1)
    tile_group = jnp.minimum(tile_group, N_GROUPS - 1).astype(jnp.int32)
    return slot, tile_group, tile_end[-1:].astype(jnp.int32)


def moe_routed_sc(xg, wg, wu, wd, g, b, layer, n, tm, during_scatter, during_gather, tiles_out):
    slot, tile_group, n_used = _group_slots(xg[X_PLANES, :, 0].astype(jnp.int32), n, tm)
    n_tiles = tile_group.shape[0]
    n_slots = n_tiles * tm
    plane_base = lambda planes: jnp.arange(planes, dtype=jnp.int32)[:, None] * n_slots
    idx_in = (plane_base(XG_PLANES) + slot[None, :]).reshape(-1)
    if tiles_out:
        idx_out = (plane_base(X_PLANES)[None] + slot.reshape(n // SUBLANES, 1, SUBLANES)).reshape(-1)
    else:
        idx_out = (plane_base(X_PLANES) + slot[None, :]).reshape(-1)
    xs = sc_scatter_rows(xg.reshape(XG_PLANES * n, LANES), idx_in, XG_PLANES * n_slots)
    xs = xs.reshape(XG_PLANES, n_slots, LANES)
    xs, side = lax.optimization_barrier((xs, during_scatter()))
    wspec = lambda shp: pl.BlockSpec((None, None, EXP_PER_GROUP) + shp,
                                     lambda t, tg, nu: (layer, tg[t], 0, 0, 0))
    used_tile = lambda t, tg, nu: (0, jnp.minimum(t, nu[0] - 1), 0)
    grid_spec = pltpu.PrefetchScalarGridSpec(
        num_scalar_prefetch=2,
        grid=(n_tiles,),
        in_specs=[pl.BlockSpec((XG_PLANES, tm, LANES), used_tile),
                  wspec((D_MODEL, EXP_FF)), wspec((D_MODEL, EXP_FF)), wspec((EXP_FF, D_MODEL)),
                  pl.BlockSpec((1, D_MODEL), lambda t, tg, nu: (0, 0)),
                  pl.BlockSpec((1, D_MODEL), lambda t, tg, nu: (0, 0))],
        out_specs=pl.BlockSpec((X_PLANES, tm, LANES), used_tile),
        scratch_shapes=[pltpu.VMEM((EXP_PER_GROUP, D_MODEL, EXP_FF), BF16),
                        pltpu.VMEM((EXP_PER_GROUP, D_MODEL, EXP_FF), BF16),
                        pltpu.VMEM((EXP_PER_GROUP, EXP_FF, D_MODEL), BF16)])
    grouped = lambda w: w.reshape(w.shape[0], N_GROUPS, EXP_PER_GROUP, w.shape[2], w.shape[3])
    ys = pl.pallas_call(
        _moe_sorted_kernel,
        grid_spec=grid_spec,
        out_shape=jax.ShapeDtypeStruct((X_PLANES, n_slots, LANES), F32),
        compiler_params=_cparams(("arbitrary",)),
        name="moe_sorted",
    )(tile_group, n_used, xs, grouped(wg), grouped(wu), grouped(wd), g, b)
    y = sc_gather_rows(ys.reshape(X_PLANES * n_slots, LANES), idx_out)
    y, side = lax.optimization_barrier((y, during_gather(side)))
    if tiles_out:
        y = y.reshape(n // SUBLANES, X_PLANES, SUBLANES, LANES).transpose(0, 2, 1, 3)
        return y.reshape(n, D_MODEL), side
    return y.reshape(X_PLANES, n, LANES), side


def _t5_bucket(dist):
    max_exact = N_BUCKETS // 2
    d = jnp.maximum(dist, 0)
    df = jnp.maximum(d, 1).astype(F32)
    log_b = max_exact + (jnp.log(df / max_exact) / math.log(MAX_DISTANCE / max_exact)
                         * (N_BUCKETS - max_exact)).astype(jnp.int32)
    return jnp.where(d < max_exact, d, jnp.minimum(log_b, N_BUCKETS - 1))


def _bucket_lookup(rel_bias, bucket):
    out = jnp.zeros(bucket.shape + (rel_bias.shape[1],), F32)
    for i in range(N_BUCKETS):
        out = jnp.where((bucket == i)[..., None], rel_bias[i].astype(F32), out)
    return out


def _prompt_bias(rel_bias):
    qi = jnp.arange(WINDOW)[:, None]
    kj = jnp.arange(2 * WINDOW)[None, :]
    dist = qi + WINDOW - kj
    bias = _bucket_lookup(rel_bias, _t5_bucket(dist)).transpose(2, 0, 1)
    valid = (dist >= 0) & (dist <= WINDOW)
    return jnp.where(valid[None], bias, NEG)


def _decode_table(rel_bias, attn_sink):
    dist = WINDOW - jnp.arange(WINDOW + 1)
    bias = _bucket_lookup(rel_bias, _t5_bucket(dist)).T
    depth = attn_sink.shape[0]
    wide = lambda v: jnp.broadcast_to(v[..., None], v.shape + (WINDOW,))
    per_layer = lambda t: jnp.broadcast_to(t[None], (depth,) + t.shape)
    return jnp.stack([per_layer(bias[:, :WINDOW]), per_layer(wide(bias[:, WINDOW])),
                      wide(attn_sink.astype(F32))], axis=1)


def _block_ones(width):
    idx = jnp.arange(width) // HEAD_DIM
    return (idx[:, None] == idx[None, :])


def _head_rows_mask():
    head = jnp.arange(HG_WIDTH)[None, :] // HEAD_DIM
    return (head == jnp.arange(HG_MROWS)[:, None]).astype(F32)


def _block_diag(w):
    nblk, s, _ = w.shape
    eye = jnp.eye(nblk, dtype=w.dtype)
    return (eye[:, None, :, None] * w[:, :, None, :]).reshape(nblk * s, nblk * s)


def kernel(x_prompt, x_sample, mem_prompt, cache_win_k, cache_win_v, state_hgrn, state_conv, state_lru, cache_mem_k, cache_mem_v, rel_bias, hg_lb, w_in, attn_sink, hg_gain, conv_w, conv_b, lru_wa, lru_ba, lru_wx, lru_bx, lru_lam, w_out, ln1_g, ln1_b, x_wq, x_wk, x_wv, x_wo, ln2_g, ln2_b, r_gw, r_gb, r_ew, r_eb, e_wg, e_wu, e_wd, ln3_g, ln3_b):
    bp, seq, d = x_prompt.shape
    n_tok = bp * seq
    nd = x_sample.shape[0]
    depth = w_in.shape[0]

    lbs = jnp.cumsum(jax.nn.softmax(hg_lb.astype(F32), axis=0), axis=0)
    lbs = lbs - lbs[0]
    loglb = jnp.log(lbs)
    log1mlb = jnp.log1p(-lbs)
    gain4 = jnp.tile(hg_gain, (1, HG_HEADS))

    bias_p = _prompt_bias(rel_bias)
    bdm256 = _block_ones(HG_WIDTH).astype(F32)
    hmask = _head_rows_mask()

    w_in_b = w_in.astype(BF16)
    w_out_b = w_out.astype(BF16)
    wq_b = x_wq.astype(BF16)
    wkv_b = jnp.concatenate([x_wk, x_wv], axis=-1).astype(BF16)
    wo_b = x_wo.astype(BF16)
    rew = r_ew.transpose(0, 2, 1, 3).reshape(depth, d, N_EXPERTS)
    wr = jnp.concatenate([r_gw, rew, jnp.zeros((depth, d, ROUTER_LANES - N_GROUPS - N_EXPERTS), F32)], -1)
    br = jnp.concatenate([r_gb, r_eb.reshape(depth, N_EXPERTS),
                          jnp.zeros((depth, ROUTER_LANES - N_GROUPS - N_EXPERTS), F32)], -1)

    a_w = A_HEADS * HEAD_DIM
    xp = x_prompt.reshape(bp * seq, d)
    xs = x_sample.reshape(nd, d)
    mem = mem_prompt.reshape(bp * N_MEM, d)
    ckt = cache_win_k.transpose(0, 1, 3, 4, 2).reshape(depth, nd, LANES, WINDOW)
    cvt = cache_win_v.transpose(0, 1, 3, 4, 2).reshape(depth, nd, LANES, WINDOW)
    cmkt = cache_mem_k.transpose(0, 1, 3, 4, 2).reshape(depth, nd, X_WIDTH, N_MEM)
    cmvt = cache_mem_v.transpose(0, 1, 3, 4, 2).reshape(depth, nd, X_WIDTH, N_MEM)
    state_t = state_hgrn.transpose(0, 2, 3, 4, 1).reshape(depth, HG_HEADS * HEAD_DIM * HEAD_DIM, nd)
    dec_tab = _decode_table(rel_bias, attn_sink)
    xq_mask = _head_rows_mask()[:X_QROWS]
    head_group = jnp.arange(A_HEADS) // A_REP

    p_wk, p_wv, p_s, p_cb, p_h, p_mk, p_mv = [], [], [], [], [], [], []
    s_cb, s_h = [], []
    for l in range(depth):
        row = lambda v: v[l].reshape(1, -1)
        wa_bd = _block_diag(lru_wa[l]).astype(BF16)
        wx_bd = _block_diag(lru_wx[l]).astype(BF16)
        lru_args = (conv_w[l], row(conv_b), wa_bd, row(lru_ba), wx_bd, row(lru_bx), row(lru_lam))

        proj = matmul(xp, w_in_b, ROW_TILE, IN_COLS, layer=l)
        oa = attn_prompt(proj, attn_sink[l], bias_p, bp, seq)
        ob, st = hgrn_prompt(proj, row(loglb), row(log1mlb), row(gain4), bdm256, hmask, bp, seq, ROW_TILE)
        oc, hl = lru_prompt(proj, *lru_args, bp, seq, LRU_TILE)
        xp = proj_res_ln(xp, [oa, ob, oc], [w_out_b], row(ln1_g), row(ln1_b), ROW_TILE, layer=l)
        mkv = matmul(mem, wkv_b[l], N_MEM, 2 * X_WIDTH)
        xg = xattn_prompt(xp, wq_b[l], mkv, wo_b[l], row(ln2_g), row(ln2_b), wr[l].T, br[l][:, None],
                          bp, seq, ROW_TILE)
        def decode_mixers(xs=xs, ckt=ckt, cvt=cvt, state_t=state_t, l=l, lru_args=lru_args,
                          row=row):
            projd = matmul(xs, w_in_b, nd, IN_COLS, layer=l)
            q3 = projd[:, :a_w].reshape(nd, A_HEADS, 1, HEAD_DIM)
            on_group = head_group[None, :, None, None] == jnp.arange(A_KV_HEADS)[None, None, :, None]
            qblk = jnp.where(on_group, q3, 0.0).reshape(nd, A_HEADS, LANES)
            o3, ckt, cvt = attn_decode(qblk, projd, dec_tab[l], ckt, cvt, l, DEC_BLOCK)
            o4 = o3.reshape(nd, A_HEADS, A_KV_HEADS, HEAD_DIM)
            oa = jnp.sum(jnp.where(on_group, o4, 0.0), axis=2).reshape(nd, a_w)
            gates_t = projd[:, a_w + 2 * LANES:a_w + 2 * LANES + 4 * HG_WIDTH].T
            bc = lambda v: jnp.broadcast_to(v[:, None], (v.shape[0], nd))
            ob_t, state_t = hgrn_decode(gates_t, bc(loglb[l]), bc(log1mlb[l]), bc(hg_gain[l]), state_t, l)
            oc, nh, nbuf = lru_decode(projd, state_conv[l].reshape(nd, -1), state_lru[l], *lru_args)
            xs = proj_res_ln(xs, [oa, ob_t.T, oc], [w_out_b], row(ln1_g), row(ln1_b), nd, layer=l)
            return xs, ckt, cvt, state_t, nh, nbuf

        def decode_rest(side, l=l, row=row):
            xs = side[0]
            qd = matmul(xs, wq_b[l], nd, X_WIDTH)
            qdb = qd[:, None, :] * xq_mask[None, :, :]
            od = xattn_decode(qdb, cmkt, cmvt, xq_mask, l, DEC_BLOCK)
            xs = proj_res_ln(xs, [od], [wo_b[l]], row(ln2_g), row(ln2_b), nd)
            xs = moe_dense(xs, wr[l], br[l:l + 1], e_wg, e_wu, e_wd, row(ln3_g), row(ln3_b), l, nd)
            return (xs,) + tuple(side[1:])

        xp, (xs, ckt, cvt, state_t, nh, nbuf) = moe_routed_sc(
            xg, e_wg, e_wu, e_wd, row(ln3_g), row(ln3_b), l, n_tok, MOE_TM, decode_mixers, decode_rest,
            tiles_out=(l == depth - 1))

        proj3 = proj.reshape(bp, seq, IN_COLS)
        p_wk.append(proj3[:, seq - WINDOW:, a_w:a_w + LANES].reshape(bp, WINDOW, A_KV_HEADS, HEAD_DIM))
        p_wv.append(proj3[:, seq - WINDOW:, a_w + LANES:a_w + 2 * LANES].reshape(bp, WINDOW, A_KV_HEADS, HEAD_DIM))
        st5 = st.reshape(bp, HG_HEADS, HEAD_DIM, HG_HEADS, HEAD_DIM)
        p_s.append(jnp.stack([st5[:, h, :, h, :] for h in range(HG_HEADS)], 1).transpose(0, 1, 3, 2))
        p_cb.append(proj3[:, seq - (CONV_W - 1):, IN_COLS - 2 * LRU_WIDTH:IN_COLS - LRU_WIDTH])
        p_h.append(hl.reshape(bp, LRU_WIDTH))
        p_mk.append(mkv[:, :X_WIDTH].reshape(bp, N_MEM, X_HEADS, HEAD_DIM))
        p_mv.append(mkv[:, X_WIDTH:].reshape(bp, N_MEM, X_HEADS, HEAD_DIM))

        s_cb.append(nbuf.reshape(nd, CONV_W - 1, LRU_WIDTH))
        s_h.append(nh)

    unkey = lambda c: c.reshape(depth, nd, A_KV_HEADS, HEAD_DIM, WINDOW).transpose(0, 1, 4, 2, 3)
    s_s = state_t.reshape(depth, HG_HEADS, HEAD_DIM, HEAD_DIM, nd).transpose(0, 4, 1, 2, 3)
    return (xp.reshape(bp, seq, d), xs.reshape(nd, 1, d),
            jnp.stack(p_wk), jnp.stack(p_wv), jnp.stack(p_s), jnp.stack(p_cb), jnp.stack(p_h),
            jnp.stack(p_mk), jnp.stack(p_mv),
            unkey(ckt), unkey(cvt), s_s, jnp.stack(s_cb), jnp.stack(s_h))
```

```python
import functools
import math

import jax
import jax.numpy as jnp
from jax import lax
from jax.experimental import pallas as pl
from jax.experimental.pallas import tpu as pltpu
from jax.experimental.pallas import tpu_sc as plsc

F32 = jnp.float32
BF16 = jnp.bfloat16
MIX_DTYPE = BF16

D_MODEL = 1024
DEPTH = 4
HEAD_DIM = 64
A_HEADS = 8
A_KV_HEADS = 2
A_REP = A_HEADS // A_KV_HEADS
WINDOW = 128
A_QB = 2
N_BUCKETS = 32
MAX_DISTANCE = 128
HG_WIDTH = 256
HG_HEADS = 4
HG_CHUNK = 64
HG_TB = 32
HG_MROWS = 8
LOG2E = math.log2(math.e)
LRU_WIDTH = 256
CONV_W = 4
LRU_C = 8.0
N_MEM = 256
X_HEADS = 4
X_WIDTH = X_HEADS * HEAD_DIM
X_QROWS = 8
N_GROUPS = 4
EXP_PER_GROUP = 4
N_EXPERTS = N_GROUPS * EXP_PER_GROUP
EXP_FF = D_MODEL // 4
ALPHA = (2 * DEPTH) ** 0.25
LN_EPS = 1e-5
RMS_EPS = 1e-6
IN_COLS = 2304
SCALE = HEAD_DIM ** -0.5
NEG = -1e30
LANES = 128
SUBLANES = 8
ROUTER_LANES = 128
ROUTER_ROWS = 32
XG_WIDTH = D_MODEL + ROUTER_LANES
XG_PLANES = XG_WIDTH // LANES
X_PLANES = D_MODEL // LANES
MOE_TM = 512
ROW_TILE = 1024
LRU_TILE = 512
DEC_BLOCK = 16
SC_WINDOW = 128
VMEM_LIMIT = 48 * 1024 * 1024


def _cparams(sem):
    return pltpu.CompilerParams(dimension_semantics=sem, vmem_limit_bytes=VMEM_LIMIT)


def _bdot(a, b):
    return jnp.dot(a.astype(BF16), b.astype(BF16), preferred_element_type=F32)


def _bdot_nt(a, b):
    return lax.dot_general(a.astype(BF16), b.astype(BF16), (((1,), (1,)), ((), ())),
                           preferred_element_type=F32)


def _bdot_tn(a, b):
    return lax.dot_general(a.astype(BF16), b.astype(BF16), (((0,), (0,)), ((), ())),
                           preferred_element_type=F32)


def _rb(x):
    return x.astype(BF16).astype(F32)


def _silu(x):
    return x * jax.nn.sigmoid(x)


def _neg_expm1(x):
    return -jnp.tanh(0.5 * x) * (jnp.exp(x) + 1.0)


def _softplus(x):
    return jnp.maximum(x, 0.0) + jnp.log1p(jnp.exp(-jnp.abs(x)))


def _gelu_tanh(x):
    return 0.5 * x * (1.0 + jnp.tanh(math.sqrt(2.0 / math.pi) * (x + 0.044715 * (x * x * x))))


def _layer_norm(y, g, b):
    mu = jnp.mean(y, -1, keepdims=True)
    yc = y - mu
    var = jnp.mean(yc * yc, -1, keepdims=True)
    return yc * lax.rsqrt(var + LN_EPS) * g + b


def _rows(x_ref):
    if len(x_ref.shape) == 2:
        return x_ref[...]
    return jnp.concatenate([x_ref[j] for j in range(x_ref.shape[0])], axis=1)


def _rows_spec(x, tm, nargs):
    if x.ndim == 2:
        return pl.BlockSpec((tm, x.shape[1]), (lambda i: (i, 0)) if nargs == 1 else (lambda i, j: (i, 0)))
    blk = (x.shape[0], tm, LANES)
    return pl.BlockSpec(blk, (lambda i: (0, i, 0)) if nargs == 1 else (lambda i, j: (0, i, 0)))


def _mm_kernel(x_ref, w_ref, o_ref):
    o_ref[...] = _bdot(_rows(x_ref), w_ref[...])


def matmul(x, w, tm, tn, layer=None):
    m = x.shape[-2]
    k, n = w.shape[-2:]
    if w.ndim == 2:
        w_spec = pl.BlockSpec((k, tn), lambda i, j: (0, j))
    else:
        w_spec = pl.BlockSpec((None, k, tn), lambda i, j: (layer, 0, j))
    return pl.pallas_call(
        _mm_kernel,
        grid=(m // tm, n // tn),
        in_specs=[_rows_spec(x, tm, 2), w_spec],
        out_specs=pl.BlockSpec((tm, tn), lambda i, j: (i, j)),
        out_shape=jax.ShapeDtypeStruct((m, n), F32),
        compiler_params=_cparams(("parallel", "parallel")),
        name="matmul",
    )(x, w)


def _proj_res_ln_kernel(n_in, x_ref, *refs):
    a_refs = refs[:n_in]
    w_refs = refs[n_in:2 * n_in]
    g_ref, b_ref, o_ref = refs[2 * n_in:]
    y = ALPHA * _rows(x_ref)
    for a_ref, w_ref in zip(a_refs, w_refs):
        y = y + _bdot(a_ref[...], w_ref[...])
    o_ref[...] = _layer_norm(y, g_ref[...], b_ref[...])


def proj_res_ln(x, a_list, w_list, g, b, tm, layer=None):
    m = x.shape[-2]
    n_in = len(a_list)
    in_specs = [_rows_spec(x, tm, 1)]
    in_specs += [pl.BlockSpec((tm, a.shape[1]), lambda i: (i, 0)) for a in a_list]
    if layer is None:
        d = w_list[0].shape[1]
        in_specs += [pl.BlockSpec(w.shape, lambda i: (0, 0)) for w in w_list]
    else:
        (w,) = w_list
        d = w.shape[2]
        row0 = 0
        for a in a_list:
            k = a.shape[1]
            in_specs.append(pl.BlockSpec((None, k, d), functools.partial(lambda i, blk: (layer, blk, 0),
                                                                       blk=row0 // k)))
            row0 += k
        w_list = [w] * n_in
    in_specs += [pl.BlockSpec((1, d), lambda i: (0, 0))] * 2
    return pl.pallas_call(
        functools.partial(_proj_res_ln_kernel, n_in),
        grid=(m // tm,),
        in_specs=in_specs,
        out_specs=pl.BlockSpec((tm, d), lambda i: (i, 0)),
        out_shape=jax.ShapeDtypeStruct((m, d), F32),
        compiler_params=_cparams(("parallel",)),
        name="proj_res_ln",
    )(x, *a_list, *w_list, g, b)


def _attn_prompt_kernel(sink_ref, q_ref, kc_ref, kp_ref, vc_ref, vp_ref, bias_ref, o_ref):
    n = pl.program_id(1)
    col = lax.broadcasted_iota(jnp.int32, (WINDOW, 2 * WINDOW), 1)
    first = jnp.where((n == 0) & (col < WINDOW), NEG, 0.0)
    kk = jnp.concatenate([kp_ref[...], kc_ref[...]], axis=0).astype(BF16)
    vv = jnp.concatenate([vp_ref[...], vc_ref[...]], axis=0).astype(BF16)
    q = q_ref[...].astype(BF16)
    for u in range(A_QB):
        rows = slice(u * WINDOW, (u + 1) * WINDOW)
        keys = slice(u * WINDOW, (u + 2) * WINDOW)
        outs = []
        for h in range(A_HEADS):
            g = h // A_REP
            qh = q[rows, h * HEAD_DIM:(h + 1) * HEAD_DIM]
            kg = kk[keys, g * HEAD_DIM:(g + 1) * HEAD_DIM]
            vg = vv[keys, g * HEAD_DIM:(g + 1) * HEAD_DIM]
            s = _bdot_nt(qh, kg) * SCALE + bias_ref[h]
            if u == 0:
                s = s + first
            sink = sink_ref[h]
            m = jnp.maximum(jnp.max(s, -1, keepdims=True), sink)
            p = jnp.exp(s - m)
            den = jnp.sum(p, -1, keepdims=True) + jnp.exp(sink - m)
            outs.append(_bdot(p / den, vg))
        o_ref[rows, :] = jnp.concatenate(outs, axis=1).astype(o_ref.dtype)


def attn_prompt(proj, sink, bias, batch, seq):
    nb = seq // WINDOW
    ns = nb // A_QB
    tq = A_QB * WINDOW
    qcol = 0
    kcol = (A_HEADS * HEAD_DIM) // LANES
    vcol = kcol + 1

    def cur(c):
        return lambda b, n: (b * ns + n, c)

    def prev(c):
        return lambda b, n: (b * nb + jnp.maximum(n * A_QB - 1, 0), c)

    return pl.pallas_call(
        _attn_prompt_kernel,
        grid=(batch, ns),
        in_specs=[pl.BlockSpec(memory_space=pltpu.SMEM),
                  pl.BlockSpec((tq, A_HEADS * HEAD_DIM), cur(qcol)),
                  pl.BlockSpec((tq, LANES), cur(kcol)),
                  pl.BlockSpec((WINDOW, LANES), prev(kcol)),
                  pl.BlockSpec((tq, LANES), cur(vcol)),
                  pl.BlockSpec((WINDOW, LANES), prev(vcol)),
                  pl.BlockSpec((A_HEADS, WINDOW, 2 * WINDOW), lambda b, n: (0, 0, 0))],
        out_specs=pl.BlockSpec((tq, A_HEADS * HEAD_DIM), cur(0)),
        out_shape=jax.ShapeDtypeStruct((batch * seq, A_HEADS * HEAD_DIM), MIX_DTYPE),
        compiler_params=_cparams(("parallel", "parallel")),
        name="attn_prompt",
    )(sink, proj, proj, proj, proj, proj, bias)


def _attn_decode_kernel(qb_ref, kn_ref, vn_ref, knt_ref, vnt_ref, ck_ref, cv_ref, tab_ref,
                        o_ref, ok_ref, ov_ref):
    bb = qb_ref.shape[0]
    ck = ck_ref[...]
    cv = cv_ref[...]
    qb = qb_ref[...]
    kn = kn_ref[...]
    vn = vn_ref[...]
    bias_j = tab_ref[0]
    bias_n = tab_ref[1][:, 0:1]
    sink = tab_ref[2][:, 0:1]
    s = lax.dot_general(qb.astype(BF16), ck.astype(BF16), (((2,), (1,)), ((0,), (0,))),
                        preferred_element_type=F32) * SCALE + bias_j[None]
    sn = jnp.sum(_rb(qb) * _rb(kn)[:, None, :], -1, keepdims=True) * SCALE + bias_n[None]
    m = jnp.maximum(jnp.maximum(jnp.max(s, -1, keepdims=True), sn), sink[None])
    p = jnp.exp(s - m)
    pn = jnp.exp(sn - m)
    den = jnp.sum(p, -1, keepdims=True) + pn + jnp.exp(sink[None] - m)
    o = lax.dot_general((p / den).astype(BF16), cv.astype(BF16), (((2,), (2,)), ((0,), (0,))),
                        preferred_element_type=F32)
    o_ref[...] = o + _rb(pn / den) * _rb(vn)[:, None, :]
    lane = lax.broadcasted_iota(jnp.int32, (LANES, LANES), 1)
    for b in range(bb):
        ok_ref[b] = jnp.where(lane == WINDOW - 1, knt_ref[:, b:b + 1], pltpu.roll(ck[b], WINDOW - 1, 1))
        ov_ref[b] = jnp.where(lane == WINDOW - 1, vnt_ref[:, b:b + 1], pltpu.roll(cv[b], WINDOW - 1, 1))


def attn_decode(qblk, proj_d, table, cache_k, cache_v, layer, bb):
    nbatch = proj_d.shape[0]
    a_w = A_HEADS * HEAD_DIM
    cols = lambda c: proj_d[:, c:c + LANES].reshape(nbatch // bb, bb, LANES).transpose(0, 2, 1)
    knt, vnt = cols(a_w), cols(a_w + LANES)
    kcol = (A_HEADS * HEAD_DIM) // LANES
    cache_spec = pl.BlockSpec((None, bb, LANES, WINDOW), lambda i: (layer, i, 0, 0))
    col_spec = pl.BlockSpec((None, LANES, bb), lambda i: (i, 0, 0))
    return pl.pallas_call(
        _attn_decode_kernel,
        grid=(nbatch // bb,),
        in_specs=[pl.BlockSpec((bb, A_HEADS, LANES), lambda i: (i, 0, 0)),
                  pl.BlockSpec((bb, LANES), lambda i: (i, kcol)),
                  pl.BlockSpec((bb, LANES), lambda i: (i, kcol + 1)),
                  col_spec, col_spec, cache_spec, cache_spec,
                  pl.BlockSpec((3, A_HEADS, WINDOW), lambda i: (0, 0, 0))],
        out_specs=[pl.BlockSpec((bb, A_HEADS, LANES), lambda i: (i, 0, 0)), cache_spec, cache_spec],
        out_shape=[jax.ShapeDtypeStruct((nbatch, A_HEADS, LANES), F32),
                   jax.ShapeDtypeStruct(cache_k.shape, F32),
                   jax.ShapeDtypeStruct(cache_v.shape, F32)],
        input_output_aliases={5: 1, 6: 2},
        compiler_params=_cparams(("arbitrary",)),
        name="attn_decode",
    )(qblk, proj_d, proj_d, knt, vnt, cache_k, cache_v, table)


def _xattn_decode_kernel(qb_ref, mk_ref, mv_ref, hm_ref, o_ref):
    qb = qb_ref[...]
    s = lax.dot_general(qb.astype(BF16), mk_ref[...].astype(BF16), (((2,), (1,)), ((0,), (0,))),
                        preferred_element_type=F32) * SCALE
    m = jnp.max(s, -1, keepdims=True)
    p = jnp.exp(s - m)
    p = p / jnp.sum(p, -1, keepdims=True)
    o = lax.dot_general(p.astype(BF16), mv_ref[...].astype(BF16), (((2,), (2,)), ((0,), (0,))),
                        preferred_element_type=F32)
    o_ref[...] = jnp.sum(o * hm_ref[...][None], axis=1)


def xattn_decode(qblk, mem_k, mem_v, hmask, layer, bb):
    nbatch = qblk.shape[0]
    mem_spec = pl.BlockSpec((None, bb, X_WIDTH, N_MEM), lambda i: (layer, i, 0, 0))
    return pl.pallas_call(
        _xattn_decode_kernel,
        grid=(nbatch // bb,),
        in_specs=[pl.BlockSpec((bb, X_QROWS, X_WIDTH), lambda i: (i, 0, 0)), mem_spec, mem_spec,
                  pl.BlockSpec((X_QROWS, X_WIDTH), lambda i: (0, 0))],
        out_specs=pl.BlockSpec((bb, X_WIDTH), lambda i: (i, 0)),
        out_shape=jax.ShapeDtypeStruct((nbatch, X_WIDTH), F32),
        compiler_params=_cparams(("parallel",)),
        name="xattn_decode",
    )(qblk, mem_k, mem_v, hmask)


def _hgrn_gates(hq, hf, loglb, log1mlb):
    ls = jnp.minimum(hf, 0.0) - jnp.log1p(jnp.exp(-jnp.abs(hf)))
    b = log1mlb + ls
    lf = jnp.maximum(loglb, b) + jnp.log1p(jnp.exp(-jnp.abs(loglb - b)))
    return _silu(hq), lf, _neg_expm1(lf)


def _hgrn_prompt_kernel(hq_ref, hf_ref, hi_ref, hg_ref, loglb_ref, log1mlb_ref, gain_ref,
                        bdm_ref, hm_ref, ob_ref, st_ref, st_scr, q_scr, k_scr, cum_scr, o_scr):
    i = pl.program_id(1)
    tt = hq_ref.shape[0]
    c = HG_CHUNK
    tb = HG_TB

    @pl.when(i == 0)
    def _():
        st_scr[...] = jnp.zeros_like(st_scr)

    qs, lf, kk = _hgrn_gates(hq_ref[...], hf_ref[...], loglb_ref[...], log1mlb_ref[...])
    row = lax.broadcasted_iota(jnp.int32, (tt, HG_WIDTH), 0) & (c - 1)
    cum = lf
    sh = 1
    while sh < c:
        cum = cum + jnp.where(row >= sh, pltpu.roll(cum, sh, 0), 0.0)
        sh *= 2
    q_scr[...] = qs
    k_scr[...] = kk
    cum_scr[...] = cum

    bdm = bdm_ref[...]
    hmask = hm_ref[...]
    def chunk(ci, carry):
        r0 = pl.multiple_of(ci * c, c)
        r = pl.ds(r0, c)
        cu = cum_scr[r, :]
        q = q_scr[r, :]
        k = k_scr[r, :]
        v = hi_ref[r, :]
        vb = v.astype(BF16)
        qb = _rb(q)
        cu2 = cu * LOG2E
        last = cu[c - 1:c, :]
        st = st_scr[...]
        o_inter = _bdot_nt(q * jnp.exp(cu), st)
        for j in range(c // tb):
            ns = tb * (j + 1)
            ti = lax.broadcasted_iota(jnp.int32, (tb, ns, HG_WIDTH), 0) + tb * j
            si = lax.broadcasted_iota(jnp.int32, (tb, ns, HG_WIDTH), 1)
            cut = cu2[tb * j:tb * (j + 1), :]
            dec = jnp.exp2(jnp.where(ti >= si, cut[:, None, :] - cu2[None, :ns, :], NEG))
            a2 = (dec * k[None, :ns, :]).astype(BF16)
            q4 = qb[tb * j:tb * (j + 1), None, :] * hmask[None, :, :]
            att = lax.dot_general(q4, a2, (((2,), (2,)), ((0,), (0,))),
                                  preferred_element_type=F32)
            w = jnp.dot(_rb(att.reshape(tb * HG_MROWS, ns)), vb[:ns, :],
                        preferred_element_type=F32).reshape(tb, HG_MROWS, HG_WIDTH)
            o_intra = jnp.sum(w * hmask[None, :, :], axis=1)
            o_scr[pl.ds(r0 + tb * j, tb), :] = o_intra + o_inter[tb * j:tb * (j + 1), :]
        upd = _bdot_tn(v, k * jnp.exp(last - cu))
        st_scr[...] = st * jnp.exp(last) + upd * bdm
        return carry

    lax.fori_loop(0, tt // c, chunk, 0, unroll=4)

    o = o_scr[...]
    ms = jnp.dot(o * o, bdm, precision=lax.Precision.HIGHEST,
                 preferred_element_type=F32) * (1.0 / HEAD_DIM)
    ob_ref[...] = (o * lax.rsqrt(ms + RMS_EPS) * gain_ref[...] * _silu(hg_ref[...])).astype(ob_ref.dtype)

    @pl.when(i == pl.num_programs(1) - 1)
    def _():
        st_ref[...] = st_scr[...]


def hgrn_prompt(proj, loglb, log1mlb, gain4, bdm, hmask, batch, seq, tt):
    nt = seq // tt
    base = (A_HEADS + 2 * A_KV_HEADS) * HEAD_DIM // HG_WIDTH

    def col(cblk):
        return pl.BlockSpec((tt, HG_WIDTH), lambda b, i: (b * nt + i, cblk))

    row_spec = pl.BlockSpec((1, HG_WIDTH), lambda b, i: (0, 0))
    mat_spec = pl.BlockSpec((HG_WIDTH, HG_WIDTH), lambda b, i: (0, 0))
    return pl.pallas_call(
        _hgrn_prompt_kernel,
        grid=(batch, nt),
        in_specs=[col(base), col(base + 1), col(base + 2), col(base + 3),
                  row_spec, row_spec, row_spec, mat_spec,
                  pl.BlockSpec((HG_MROWS, HG_WIDTH), lambda b, i: (0, 0))],
        out_specs=[pl.BlockSpec((tt, HG_WIDTH), lambda b, i: (b * nt + i, 0)),
                   pl.BlockSpec((None, HG_WIDTH, HG_WIDTH), lambda b, i: (b, 0, 0))],
        out_shape=[jax.ShapeDtypeStruct((batch * seq, HG_WIDTH), MIX_DTYPE),
                   jax.ShapeDtypeStruct((batch, HG_WIDTH, HG_WIDTH), F32)],
        scratch_shapes=[pltpu.VMEM((HG_WIDTH, HG_WIDTH), F32),
                        pltpu.VMEM((tt, HG_WIDTH), F32),
                        pltpu.VMEM((tt, HG_WIDTH), F32),
                        pltpu.VMEM((tt, HG_WIDTH), F32),
                        pltpu.VMEM((tt, HG_WIDTH), F32)],
        compiler_params=_cparams(("parallel", "arbitrary")),
        name="hgrn_prompt",
    )(proj, proj, proj, proj, loglb, log1mlb, gain4, bdm, hmask)


def _hgrn_decode_kernel(hq_ref, hf_ref, hi_ref, hg_ref, loglb_ref, log1mlb_ref, gain_ref, s_ref,
                        ob_ref, so_ref):
    nb = hq_ref.shape[1]
    qs, lf, kk = _hgrn_gates(hq_ref[...], hf_ref[...], loglb_ref[...], log1mlb_ref[...])
    v = hi_ref[...]
    f = jnp.exp(lf)
    s = s_ref[...].reshape(HEAD_DIM, HEAD_DIM, nb)
    att = jnp.sum(_rb(qs) * _rb(kk), axis=0, keepdims=True)
    o = _rb(att) * _rb(v) + jnp.sum(_rb(qs * f)[:, None, :] * _rb(s), axis=0)
    s_new = f[:, None, :] * s + _rb(kk)[:, None, :] * _rb(v)[None, :, :]
    so_ref[...] = s_new.reshape(HEAD_DIM * HEAD_DIM, nb)
    ms = jnp.mean(o * o, axis=0, keepdims=True)
    ob_ref[...] = (o * lax.rsqrt(ms + RMS_EPS) * gain_ref[...] * _silu(hg_ref[...])).astype(ob_ref.dtype)


def hgrn_decode(gates_t, loglb_t, log1mlb_t, gain_t, state_t, layer):
    nb = gates_t.shape[1]

    def blk(off):
        return pl.BlockSpec((HEAD_DIM, nb), lambda h: (off * HG_HEADS + h, 0))

    par = pl.BlockSpec((HEAD_DIM, nb), lambda h: (h, 0))
    st = pl.BlockSpec((None, HEAD_DIM * HEAD_DIM, nb), lambda h: (layer, h, 0))
    return pl.pallas_call(
        _hgrn_decode_kernel,
        grid=(HG_HEADS,),
        in_specs=[blk(0), blk(1), blk(2), blk(3), par, par,
                  pl.BlockSpec((HEAD_DIM, nb), lambda h: (0, 0)), st],
        out_specs=[par, st],
        out_shape=[jax.ShapeDtypeStruct((HG_WIDTH, nb), F32),
                   jax.ShapeDtypeStruct(state_t.shape, F32)],
        input_output_aliases={7: 1},
        compiler_params=_cparams(("arbitrary",)),
        name="hgrn_decode",
    )(gates_t, gates_t, gates_t, gates_t, loglb_t, log1mlb_t, gain_t, state_t)


def _lru_gates(xc, wa_ref, ba_ref, wx_ref, bx_ref, lam_ref):
    r = jax.nn.sigmoid(_bdot(xc, wa_ref[...]) + ba_ref[...])
    gi = jax.nn.sigmoid(_bdot(xc, wx_ref[...]) + bx_ref[...])
    log_a = -LRU_C * r * _softplus(-lam_ref[...])
    a = jnp.exp(log_a)
    bterm = jnp.sqrt(_neg_expm1(2.0 * log_a)) * (gi * xc)
    return a, bterm


def _lru_prompt_kernel(lx_ref, lg_ref, cw_ref, cb_ref, wa_ref, ba_ref, wx_ref, bx_ref, lam_ref,
                       oc_ref, hl_ref, ext_scr, h_scr):
    i = pl.program_id(1)
    tt = lx_ref.shape[0]
    pad = SUBLANES

    @pl.when(i == 0)
    def _():
        ext_scr[0:pad, :] = jnp.zeros((pad, LRU_WIDTH), F32)
        h_scr[...] = jnp.zeros_like(h_scr)

    x = lx_ref[...]
    ext_scr[pad:pad + tt, :] = x
    xc = cb_ref[...] + cw_ref[CONV_W - 1:CONV_W, :] * x
    for j in range(CONV_W - 1):
        back = CONV_W - 1 - j
        xc = xc + cw_ref[j:j + 1, :] * ext_scr[pad - back:pad - back + tt, :]
    ext_scr[0:pad, :] = x[tt - pad:tt, :]

    a, bterm = _lru_gates(xc, wa_ref, ba_ref, wx_ref, bx_ref, lam_ref)
    row = lax.broadcasted_iota(jnp.int32, (tt, LRU_WIDTH), 0)
    sh = 1
    while sh < tt:
        keep = row >= sh
        b_s = jnp.where(keep, pltpu.roll(bterm, sh, 0), 0.0)
        a_s = jnp.where(keep, pltpu.roll(a, sh, 0), 1.0)
        bterm = a * b_s + bterm
        a = a * a_s
        sh *= 2
    h = a * h_scr[...] + bterm
    h_scr[...] = h[tt - 1:tt, :]
    oc_ref[...] = (h * _gelu_tanh(lg_ref[...])).astype(oc_ref.dtype)

    @pl.when(i == pl.num_programs(1) - 1)
    def _():
        hl_ref[...] = h[tt - 1:tt, :]


def lru_prompt(proj, conv_w, conv_b, wa_bd, ba, wx_bd, bx, lam, batch, seq, tt):
    nt = seq // tt
    base = IN_COLS // LRU_WIDTH - 2

    def col(cblk):
        return pl.BlockSpec((tt, LRU_WIDTH), lambda b, i: (b * nt + i, cblk))

    row_spec = pl.BlockSpec((1, LRU_WIDTH), lambda b, i: (0, 0))
    mat_spec = pl.BlockSpec((LRU_WIDTH, LRU_WIDTH), lambda b, i: (0, 0))
    return pl.pallas_call(
        _lru_prompt_kernel,
        grid=(batch, nt),
        in_specs=[col(base), col(base + 1),
                  pl.BlockSpec((CONV_W, LRU_WIDTH), lambda b, i: (0, 0)), row_spec,
                  mat_spec, row_spec, mat_spec, row_spec, row_spec],
        out_specs=[pl.BlockSpec((tt, LRU_WIDTH), lambda b, i: (b * nt + i, 0)),
                   pl.BlockSpec((None, 1, LRU_WIDTH), lambda b, i: (b, 0, 0))],
        out_shape=[jax.ShapeDtypeStruct((batch * seq, LRU_WIDTH), MIX_DTYPE),
                   jax.ShapeDtypeStruct((batch, 1, LRU_WIDTH), F32)],
        scratch_shapes=[pltpu.VMEM((tt + 8, LRU_WIDTH), F32),
                        pltpu.VMEM((1, LRU_WIDTH), F32)],
        compiler_params=_cparams(("parallel", "arbitrary")),
        name="lru_prompt",
    )(proj, proj, conv_w, conv_b, wa_bd, ba, wx_bd, bx, lam)


def _lru_decode_kernel(lx_ref, lg_ref, buf_ref, h0_ref, cw_ref, cb_ref, wa_ref, ba_ref, wx_ref,
                       bx_ref, lam_ref, oc_ref, hn_ref, nbuf_ref):
    x = lx_ref[...]
    buf = buf_ref[...]
    xc = cb_ref[...] + cw_ref[CONV_W - 1:CONV_W, :] * x
    for j in range(CONV_W - 1):
        xc = xc + cw_ref[j:j + 1, :] * buf[:, j * LRU_WIDTH:(j + 1) * LRU_WIDTH]
    a, bterm = _lru_gates(xc, wa_ref, ba_ref, wx_ref, bx_ref, lam_ref)
    h = a * h0_ref[...] + bterm
    hn_ref[...] = h
    oc_ref[...] = (h * _gelu_tanh(lg_ref[...])).astype(oc_ref.dtype)
    nbuf_ref[...] = jnp.concatenate([buf[:, LRU_WIDTH:], x], axis=1)


def lru_decode(proj_d, conv_buf, h0, conv_w, conv_b, wa_bd, ba, wx_bd, bx, lam):
    nb = proj_d.shape[0]
    base = IN_COLS // LRU_WIDTH - 2
    row_spec = pl.BlockSpec((1, LRU_WIDTH), lambda i: (0, 0))
    mat_spec = pl.BlockSpec((LRU_WIDTH, LRU_WIDTH), lambda i: (0, 0))
    act = pl.BlockSpec((nb, LRU_WIDTH), lambda i: (0, 0))
    bufs = pl.BlockSpec((nb, (CONV_W - 1) * LRU_WIDTH), lambda i: (0, 0))
    return pl.pallas_call(
        _lru_decode_kernel,
        grid=(1,),
        in_specs=[pl.BlockSpec((nb, LRU_WIDTH), lambda i: (0, base)),
                  pl.BlockSpec((nb, LRU_WIDTH), lambda i: (0, base + 1)),
                  bufs, act, pl.BlockSpec((CONV_W, LRU_WIDTH), lambda i: (0, 0)), row_spec,
                  mat_spec, row_spec, mat_spec, row_spec, row_spec],
        out_specs=[act, act, bufs],
        out_shape=[jax.ShapeDtypeStruct((nb, LRU_WIDTH), F32),
                   jax.ShapeDtypeStruct((nb, LRU_WIDTH), F32),
                   jax.ShapeDtypeStruct((nb, (CONV_W - 1) * LRU_WIDTH), F32)],
        compiler_params=_cparams(("arbitrary",)),
        name="lru_decode",
    )(proj_d, proj_d, conv_buf, h0, conv_w, conv_b, wa_bd, ba, wx_bd, bx, lam)


def _xattn_prompt_kernel(x_ref, wq_ref, mk_ref, mv_ref, wo_ref, g_ref, b_ref, wr_ref, br_ref, o_ref):
    x = x_ref[...]
    q = _bdot(x, wq_ref[...]).astype(BF16)
    mk = mk_ref[...].astype(BF16)
    mv = mv_ref[...].astype(BF16)
    outs = []
    for h in range(X_HEADS):
        sl = slice(h * HEAD_DIM, (h + 1) * HEAD_DIM)
        s = _bdot_nt(q[:, sl], mk[:, sl]) * SCALE
        m = jnp.max(s, -1, keepdims=True)
        p = jnp.exp(s - m)
        p = p / jnp.sum(p, -1, keepdims=True)
        outs.append(_bdot(p, mv[:, sl]))
    o = jnp.concatenate(outs, axis=1)
    y = _layer_norm(ALPHA * x + _bdot(o, wo_ref[...]), g_ref[...], b_ref[...])
    logits_t = _bdot_nt(wr_ref[...], y) + br_ref[...]
    gate_t = _route_rows(logits_t[:ROUTER_ROWS, :])
    gate_t = jnp.concatenate([gate_t, jnp.zeros((ROUTER_LANES - ROUTER_ROWS, y.shape[0]), F32)], axis=0)
    for j in range(D_MODEL // LANES):
        o_ref[j] = y[:, j * LANES:(j + 1) * LANES]
    o_ref[D_MODEL // LANES] = gate_t.T


def xattn_prompt(x, wq, mem_kv, wo, g, b, wr, br, batch, seq, tt):
    nt = seq // tt
    const = lambda bi, i: (0, 0)
    return pl.pallas_call(
        _xattn_prompt_kernel,
        grid=(batch, nt),
        in_specs=[pl.BlockSpec((tt, D_MODEL), lambda bi, i: (bi * nt + i, 0)),
                  pl.BlockSpec((D_MODEL, X_WIDTH), const),
                  pl.BlockSpec((N_MEM, X_WIDTH), lambda bi, i: (bi, 0)),
                  pl.BlockSpec((N_MEM, X_WIDTH), lambda bi, i: (bi, 1)),
                  pl.BlockSpec((X_WIDTH, D_MODEL), const),
                  pl.BlockSpec((1, D_MODEL), const),
                  pl.BlockSpec((1, D_MODEL), const),
                  pl.BlockSpec((ROUTER_LANES, D_MODEL), const),
                  pl.BlockSpec((ROUTER_LANES, 1), const)],
        out_specs=pl.BlockSpec((XG_PLANES, tt, LANES), lambda bi, i: (0, bi * nt + i, 0)),
        out_shape=jax.ShapeDtypeStruct((XG_PLANES, batch * seq, LANES), F32),
        compiler_params=_cparams(("parallel", "parallel")),
        name="xattn_prompt",
    )(x, wq, mem_kv, mem_kv, wo, g, b, wr, br)


def _route(logits):
    lane = lax.broadcasted_iota(jnp.int32, logits.shape, 1)
    big = jnp.int32(ROUTER_LANES)
    ninf = -jnp.inf
    gl = jnp.where(lane < N_GROUPS, logits, ninf)
    gm = jnp.max(gl, -1, keepdims=True)
    g_val = 1.0 / jnp.sum(jnp.exp(gl - gm), -1, keepdims=True)
    g_idx = jnp.min(jnp.where(gl == gm, lane, big), -1, keepdims=True)
    lo = N_GROUPS + EXP_PER_GROUP * g_idx
    el = jnp.where((lane >= lo) & (lane < lo + EXP_PER_GROUP), logits, ninf)
    v1 = jnp.max(el, -1, keepdims=True)
    i1 = jnp.min(jnp.where(el == v1, lane, big), -1, keepdims=True)
    el2 = jnp.where(lane == i1, ninf, el)
    v2 = jnp.max(el2, -1, keepdims=True)
    i2 = jnp.min(jnp.where(el2 == v2, lane, big), -1, keepdims=True)
    e2 = jnp.exp(v2 - v1)
    w1 = g_val / (1.0 + e2)
    w2 = g_val * e2 / (1.0 + e2)
    return jnp.where(lane == i1, w1, 0.0) + jnp.where(lane == i2, w2, 0.0), g_idx


def _route_rows(logits):
    row = lax.broadcasted_iota(jnp.int32, logits.shape, 0)
    big = jnp.int32(ROUTER_ROWS)
    ninf = -jnp.inf
    gl = jnp.where(row < N_GROUPS, logits, ninf)
    gm = jnp.max(gl, 0, keepdims=True)
    g_val = 1.0 / jnp.sum(jnp.exp(gl - gm), 0, keepdims=True)
    g_idx = jnp.min(jnp.where(gl == gm, row, big), 0, keepdims=True)
    lo = N_GROUPS + EXP_PER_GROUP * g_idx
    el = jnp.where((row >= lo) & (row < lo + EXP_PER_GROUP), logits, ninf)
    v1 = jnp.max(el, 0, keepdims=True)
    i1 = jnp.min(jnp.where(el == v1, row, big), 0, keepdims=True)
    el2 = jnp.where(row == i1, ninf, el)
    v2 = jnp.max(el2, 0, keepdims=True)
    i2 = jnp.min(jnp.where(el2 == v2, row, big), 0, keepdims=True)
    e2 = jnp.exp(v2 - v1)
    w1 = g_val / (1.0 + e2)
    w2 = g_val * e2 / (1.0 + e2)
    gate = jnp.where(row == i1, w1, 0.0) + jnp.where(row == i2, w2, 0.0)
    return jnp.where(row == 0, g_idx.astype(F32), gate)


def _moe_dense_kernel(x_ref, wr_ref, br_ref, wg_ref, wu_ref, wd_ref, g_ref, b_ref, o_ref,
                      gate_scr, acc_scr):
    e = pl.program_id(1)

    @pl.when(e == 0)
    def _():
        logits = _bdot(x_ref[...], wr_ref[...]) + br_ref[...]
        gate_scr[...] = _route(logits)[0]
        acc_scr[...] = jnp.zeros_like(acc_scr)

    xb = x_ref[...].astype(BF16)
    lane = lax.broadcasted_iota(jnp.int32, gate_scr.shape, 1)
    gcol = jnp.sum(jnp.where(lane == e + N_GROUPS, gate_scr[...], 0.0), -1, keepdims=True)
    hid = _silu(_bdot(xb, wg_ref[...])) * _bdot(xb, wu_ref[...])
    acc_scr[...] += _bdot(hid * gcol, wd_ref[...])

    @pl.when(e == pl.num_programs(1) - 1)
    def _():
        o_ref[...] = _layer_norm(ALPHA * x_ref[...] + acc_scr[...], g_ref[...], b_ref[...])


def moe_dense(x, wr, br, wg, wu, wd, g, b, layer, tm):
    m = x.shape[0]
    return pl.pallas_call(
        _moe_dense_kernel,
        grid=(m // tm, N_EXPERTS),
        in_specs=[pl.BlockSpec((tm, D_MODEL), lambda i, e: (i, 0)),
                  pl.BlockSpec((D_MODEL, ROUTER_LANES), lambda i, e: (0, 0)),
                  pl.BlockSpec((1, ROUTER_LANES), lambda i, e: (0, 0)),
                  pl.BlockSpec((None, None, D_MODEL, EXP_FF), lambda i, e: (layer, e, 0, 0)),
                  pl.BlockSpec((None, None, D_MODEL, EXP_FF), lambda i, e: (layer, e, 0, 0)),
                  pl.BlockSpec((None, None, EXP_FF, D_MODEL), lambda i, e: (layer, e, 0, 0)),
                  pl.BlockSpec((1, D_MODEL), lambda i, e: (0, 0)),
                  pl.BlockSpec((1, D_MODEL), lambda i, e: (0, 0))],
        out_specs=pl.BlockSpec((tm, D_MODEL), lambda i, e: (i, 0)),
        out_shape=jax.ShapeDtypeStruct((m, D_MODEL), F32),
        scratch_shapes=[pltpu.VMEM((tm, ROUTER_LANES), F32), pltpu.VMEM((tm, D_MODEL), F32)],
        compiler_params=_cparams(("parallel", "arbitrary")),
        name="moe_dense",
    )(x, wr, br, wg, wu, wd, g, b)


def _sc_mesh():
    return plsc.VectorSubcoreMesh(core_axis_name="core", subcore_axis_name="subcore")


def sc_scatter_rows(x, idx, n_out):
    r = x.shape[0]

    @functools.partial(pl.kernel, out_type=jax.ShapeDtypeStruct((n_out, LANES), x.dtype),
                       mesh=_sc_mesh(), scratch_types=[], name="sc_scatter_rows")
    def k(x_hbm, i_hbm, o_hbm):
        def body(x_vmem, i_vmem):
            pltpu.sync_copy(x_vmem, o_hbm.at[i_vmem.at[0]])

        pltpu.emit_pipeline(
            body,
            grid=(r // SC_WINDOW,),
            in_specs=[pl.BlockSpec((SC_WINDOW, LANES), lambda i: (i, 0)),
                      pl.BlockSpec((1, SC_WINDOW), lambda i: (0, i))],
            out_specs=[],
            core_axis_name=("core", "subcore"),
            dimension_semantics=(pltpu.PARALLEL,),
        )(x_hbm, i_hbm)

    return k(x, idx.reshape(1, r))


def sc_gather_rows(table, idx):
    r = idx.shape[0]

    @functools.partial(pl.kernel, out_type=jax.ShapeDtypeStruct((r, LANES), table.dtype),
                       mesh=_sc_mesh(), scratch_types=[], name="sc_gather_rows")
    def k(t_hbm, i_hbm, o_hbm):
        def body(i_vmem, o_vmem):
            pltpu.sync_copy(t_hbm.at[i_vmem.at[0]], o_vmem)

        pltpu.emit_pipeline(
            body,
            grid=(r // SC_WINDOW,),
            in_specs=[pl.BlockSpec((1, SC_WINDOW), lambda i: (0, i))],
            out_specs=[pl.BlockSpec((SC_WINDOW, LANES), lambda i: (i, 0))],
            core_axis_name=("core", "subcore"),
            dimension_semantics=(pltpu.PARALLEL,),
        )(i_hbm, o_hbm)

    return k(table, idx.reshape(1, r))


def _moe_sorted_kernel(tg_ref, nused_ref, xs_ref, wg32_ref, wu32_ref, wd32_ref, g_ref, b_ref, o_ref,
                       wg_ref, wu_ref, wd_ref):
    t = pl.program_id(0)
    used = t < nused_ref[0]

    @pl.when(used & ((t == 0) | (tg_ref[t] != tg_ref[jnp.maximum(t - 1, 0)])))
    def _():
        wg_ref[...] = wg32_ref[...].astype(BF16)
        wu_ref[...] = wu32_ref[...].astype(BF16)
        wd_ref[...] = wd32_ref[...].astype(BF16)

    @pl.when(used)
    def _():
        x = jnp.concatenate([xs_ref[j] for j in range(X_PLANES)], axis=1)
        gate = xs_ref[X_PLANES]
        xb = x.astype(BF16)
        lane = lax.broadcasted_iota(jnp.int32, gate.shape, 1)
        first = N_GROUPS + EXP_PER_GROUP * tg_ref[t]
        acc = jnp.zeros(x.shape, F32)
        for e in range(EXP_PER_GROUP):
            gcol = jnp.sum(jnp.where(lane == first + e, gate, 0.0), -1, keepdims=True)
            hid = _silu(_bdot(xb, wg_ref[e])) * _bdot(xb, wu_ref[e])
            acc = acc + _bdot(hid * gcol, wd_ref[e])
        y = _layer_norm(ALPHA * x + acc, g_ref[...], b_ref[...])
        for j in range(X_PLANES):
            o_ref[j] = y[:, j * LANES:(j + 1) * LANES]


def _group_slots(group_idx, n, tm):
    n_tiles = n // tm + N_GROUPS
    onehot = (group_idx[:, None] == jnp.arange(N_GROUPS)[None, :]).astype(jnp.int32)
    csum = jnp.cumsum(onehot, axis=0)
    counts = csum[-1]
    rank = jnp.sum(onehot * csum, axis=1) - 1
    tiles_g = (counts + tm - 1) // tm
    tile_end = jnp.cumsum(tiles_g)
    slot_base = (tile_end - tiles_g) * tm
    slot = (jnp.sum(onehot * slot_base[None, :], axis=1) + rank).astype(jnp.int32)
    tile_group = jnp.sum((jnp.arange(n_tiles)[:, None] >= tile_end[None, :]).astype(jnp.int32), axis=1)
    tile_group = jnp.minimum(tile_group, N_GROUPS - 1).astype(jnp.int32)
    return slot, tile_group, tile_end[-1:].astype(jnp.int32)


def moe_routed_sc(xg, wg, wu, wd, g, b, layer, n, tm, during_scatter, during_gather, tiles_out):
    slot, tile_group, n_used = _group_slots(xg[X_PLANES, :, 0].astype(jnp.int32), n, tm)
    n_tiles = tile_group.shape[0]
    n_slots = n_tiles * tm
    plane_base = lambda planes: jnp.arange(planes, dtype=jnp.int32)[:, None] * n_slots
    idx_in = (plane_base(XG_PLANES) + slot[None, :]).reshape(-1)
    if tiles_out:
        idx_out = (plane_base(X_PLANES)[None] + slot.reshape(n // SUBLANES, 1, SUBLANES)).reshape(-1)
    else:
        idx_out = (plane_base(X_PLANES) + slot[None, :]).reshape(-1)
    xs = sc_scatter_rows(xg.reshape(XG_PLANES * n, LANES), idx_in, XG_PLANES * n_slots)
    xs = xs.reshape(XG_PLANES, n_slots, LANES)
    xs, side = lax.optimization_barrier((xs, during_scatter()))
    wspec = lambda shp: pl.BlockSpec((None, None, EXP_PER_GROUP) + shp,
                                     lambda t, tg, nu: (layer, tg[t], 0, 0, 0))
    used_tile = lambda t, tg, nu: (0, jnp.minimum(t, nu[0] - 1), 0)
    grid_spec = pltpu.PrefetchScalarGridSpec(
        num_scalar_prefetch=2,
        grid=(n_tiles,),
        in_specs=[pl.BlockSpec((XG_PLANES, tm, LANES), used_tile),
                  wspec((D_MODEL, EXP_FF)), wspec((D_MODEL, EXP_FF)), wspec((EXP_FF, D_MODEL)),
                  pl.BlockSpec((1, D_MODEL), lambda t, tg, nu: (0, 0)),
                  pl.BlockSpec((1, D_MODEL), lambda t, tg, nu: (0, 0))],
        out_specs=pl.BlockSpec((X_PLANES, tm, LANES), used_tile),
        scratch_shapes=[pltpu.VMEM((EXP_PER_GROUP, D_MODEL, EXP_FF), BF16),
                        pltpu.VMEM((EXP_PER_GROUP, D_MODEL, EXP_FF), BF16),
                        pltpu.VMEM((EXP_PER_GROUP, EXP_FF, D_MODEL), BF16)])
    grouped = lambda w: w.reshape(w.shape[0], N_GROUPS, EXP_PER_GROUP, w.shape[2], w.shape[3])
    ys = pl.pallas_call(
        _moe_sorted_kernel,
        grid_spec=grid_spec,
        out_shape=jax.ShapeDtypeStruct((X_PLANES, n_slots, LANES), F32),
        compiler_params=_cparams(("arbitrary",)),
        name="moe_sorted",
    )(tile_group, n_used, xs, grouped(wg), grouped(wu), grouped(wd), g, b)
    y = sc_gather_rows(ys.reshape(X_PLANES * n_slots, LANES), idx_out)
    y, side = lax.optimization_barrier((y, during_gather(side)))
    if tiles_out:
        y = y.reshape(n // SUBLANES, X_PLANES, SUBLANES, LANES).transpose(0, 2, 1, 3)
        return y.reshape(n, D_MODEL), side
    return y.reshape(X_PLANES, n, LANES), side


def _t5_bucket(dist):
    max_exact = N_BUCKETS // 2
    d = jnp.maximum(dist, 0)
    df = jnp.maximum(d, 1).astype(F32)
    log_b = max_exact + (jnp.log(df / max_exact) / math.log(MAX_DISTANCE / max_exact)
                         * (N_BUCKETS - max_exact)).astype(jnp.int32)
    return jnp.where(d < max_exact, d, jnp.minimum(log_b, N_BUCKETS - 1))


def _bucket_lookup(rel_bias, bucket):
    out = jnp.zeros(bucket.shape + (rel_bias.shape[1],), F32)
    for i in range(N_BUCKETS):
        out = jnp.where((bucket == i)[..., None], rel_bias[i].astype(F32), out)
    return out


def _prompt_bias(rel_bias):
    qi = jnp.arange(WINDOW)[:, None]
    kj = jnp.arange(2 * WINDOW)[None, :]
    dist = qi + WINDOW - kj
    bias = _bucket_lookup(rel_bias, _t5_bucket(dist)).transpose(2, 0, 1)
    valid = (dist >= 0) & (dist <= WINDOW)
    return jnp.where(valid[None], bias, NEG)


def _decode_table(rel_bias, attn_sink):
    dist = WINDOW - jnp.arange(WINDOW + 1)
    bias = _bucket_lookup(rel_bias, _t5_bucket(dist)).T
    depth = attn_sink.shape[0]
    wide = lambda v: jnp.broadcast_to(v[..., None], v.shape + (WINDOW,))
    per_layer = lambda t: jnp.broadcast_to(t[None], (depth,) + t.shape)
    return jnp.stack([per_layer(bias[:, :WINDOW]), per_layer(wide(bias[:, WINDOW])),
                      wide(attn_sink.astype(F32))], axis=1)


def _block_ones(width):
    idx = jnp.arange(width) // HEAD_DIM
    return (idx[:, None] == idx[None, :])


def _head_rows_mask():
    head = jnp.arange(HG_WIDTH)[None, :] // HEAD_DIM
    return (head == jnp.arange(HG_MROWS)[:, None]).astype(F32)


def _block_diag(w):
    nblk, s, _ = w.shape
    eye = jnp.eye(nblk, dtype=w.dtype)
    return (eye[:, None, :, None] * w[:, :, None, :]).reshape(nblk * s, nblk * s)


def kernel(x_prompt, x_sample, mem_prompt, cache_win_k, cache_win_v, state_hgrn, state_conv, state_lru, cache_mem_k, cache_mem_v, rel_bias, hg_lb, w_in, attn_sink, hg_gain, conv_w, conv_b, lru_wa, lru_ba, lru_wx, lru_bx, lru_lam, w_out, ln1_g, ln1_b, x_wq, x_wk, x_wv, x_wo, ln2_g, ln2_b, r_gw, r_gb, r_ew, r_eb, e_wg, e_wu, e_wd, ln3_g, ln3_b):
    bp, seq, d = x_prompt.shape
    n_tok = bp * seq
    nd = x_sample.shape[0]
    depth = w_in.shape[0]

    lbs = jnp.cumsum(jax.nn.softmax(hg_lb.astype(F32), axis=0), axis=0)
    lbs = lbs - lbs[0]
    loglb = jnp.log(lbs)
    log1mlb = jnp.log1p(-lbs)
    gain4 = jnp.tile(hg_gain, (1, HG_HEADS))

    bias_p = _prompt_bias(rel_bias)
    bdm256 = _block_ones(HG_WIDTH).astype(F32)
    hmask = _head_rows_mask()

    w_in_b = w_in.astype(BF16)
    w_out_b = w_out.astype(BF16)
    wq_b = x_wq.astype(BF16)
    wkv_b = jnp.concatenate([x_wk, x_wv], axis=-1).astype(BF16)
    wo_b = x_wo.astype(BF16)
    rew = r_ew.transpose(0, 2, 1, 3).reshape(depth, d, N_EXPERTS)
    wr = jnp.concatenate([r_gw, rew, jnp.zeros((depth, d, ROUTER_LANES - N_GROUPS - N_EXPERTS), F32)], -1)
    br = jnp.concatenate([r_gb, r_eb.reshape(depth, N_EXPERTS),
                          jnp.zeros((depth, ROUTER_LANES - N_GROUPS - N_EXPERTS), F32)], -1)

    a_w = A_HEADS * HEAD_DIM
    xp = x_prompt.reshape(bp * seq, d)
    xs = x_sample.reshape(nd, d)
    mem = mem_prompt.reshape(bp * N_MEM, d)
    ckt = cache_win_k.transpose(0, 1, 3, 4, 2).reshape(depth, nd, LANES, WINDOW)
    cvt = cache_win_v.transpose(0, 1, 3, 4, 2).reshape(depth, nd, LANES, WINDOW)
    cmkt = cache_mem_k.transpose(0, 1, 3, 4, 2).reshape(depth, nd, X_WIDTH, N_MEM)
    cmvt = cache_mem_v.transpose(0, 1, 3, 4, 2).reshape(depth, nd, X_WIDTH, N_MEM)
    state_t = state_hgrn.transpose(0, 2, 3, 4, 1).reshape(depth, HG_HEADS * HEAD_DIM * HEAD_DIM, nd)
    dec_tab = _decode_table(rel_bias, attn_sink)
    xq_mask = _head_rows_mask()[:X_QROWS]
    head_group = jnp.arange(A_HEADS) // A_REP

    p_wk, p_wv, p_s, p_cb, p_h, p_mk, p_mv = [], [], [], [], [], [], []
    s_cb, s_h = [], []
    for l in range(depth):
        row = lambda v: v[l].reshape(1, -1)
        wa_bd = _block_diag(lru_wa[l]).astype(BF16)
        wx_bd = _block_diag(lru_wx[l]).astype(BF16)
        lru_args = (conv_w[l], row(conv_b), wa_bd, row(lru_ba), wx_bd, row(lru_bx), row(lru_lam))

        proj = matmul(xp, w_in_b, ROW_TILE, IN_COLS, layer=l)
        oa = attn_prompt(proj, attn_sink[l], bias_p, bp, seq)
        ob, st = hgrn_prompt(proj, row(loglb), row(log1mlb), row(gain4), bdm256, hmask, bp, seq, ROW_TILE)
        oc, hl = lru_prompt(proj, *lru_args, bp, seq, LRU_TILE)
        xp = proj_res_ln(xp, [oa, ob, oc], [w_out_b], row(ln1_g), row(ln1_b), ROW_TILE, layer=l)
        mkv = matmul(mem, wkv_b[l], N_MEM, 2 * X_WIDTH)
        xg = xattn_prompt(xp, wq_b[l], mkv, wo_b[l], row(ln2_g), row(ln2_b), wr[l].T, br[l][:, None],
                          bp, seq, ROW_TILE)
        def decode_mixers(xs=xs, ckt=ckt, cvt=cvt, state_t=state_t, l=l, lru_args=lru_args,
                          row=row):
            projd = matmul(xs, w_in_b, nd, IN_COLS, layer=l)
            q3 = projd[:, :a_w].reshape(nd, A_HEADS, 1, HEAD_DIM)
            on_group = head_group[None, :, None, None] == jnp.arange(A_KV_HEADS)[None, None, :, None]
            qblk = jnp.where(on_group, q3, 0.0).reshape(nd, A_HEADS, LANES)
            o3, ckt, cvt = attn_decode(qblk, projd, dec_tab[l], ckt, cvt, l, DEC_BLOCK)
            o4 = o3.reshape(nd, A_HEADS, A_KV_HEADS, HEAD_DIM)
            oa = jnp.sum(jnp.where(on_group, o4, 0.0), axis=2).reshape(nd, a_w)
            gates_t = projd[:, a_w + 2 * LANES:a_w + 2 * LANES + 4 * HG_WIDTH].T
            bc = lambda v: jnp.broadcast_to(v[:, None], (v.shape[0], nd))
            ob_t, state_t = hgrn_decode(gates_t, bc(loglb[l]), bc(log1mlb[l]), bc(hg_gain[l]), state_t, l)
            oc, nh, nbuf = lru_decode(projd, state_conv[l].reshape(nd, -1), state_lru[l], *lru_args)
            xs = proj_res_ln(xs, [oa, ob_t.T, oc], [w_out_b], row(ln1_g), row(ln1_b), nd, layer=l)
            return xs, ckt, cvt, state_t, nh, nbuf

        def decode_rest(side, l=l, row=row):
            xs = side[0]
            qd = matmul(xs, wq_b[l], nd, X_WIDTH)
            qdb = qd[:, None, :] * xq_mask[None, :, :]
            od = xattn_decode(qdb, cmkt, cmvt, xq_mask, l, DEC_BLOCK)
            xs = proj_res_ln(xs, [od], [wo_b[l]], row(ln2_g), row(ln2_b), nd)
            xs = moe_dense(xs, wr[l], br[l:l + 1], e_wg, e_wu, e_wd, row(ln3_g), row(ln3_b), l, nd)
            return (xs,) + tuple(side[1:])

        xp, (xs, ckt, cvt, state_t, nh, nbuf) = moe_routed_sc(
            xg, e_wg, e_wu, e_wd, row(ln3_g), row(ln3_b), l, n_tok, MOE_TM, decode_mixers, decode_rest,
            tiles_out=(l == depth - 1))

        proj3 = proj.reshape(bp, seq, IN_COLS)
        p_wk.append(proj3[:, seq - WINDOW:, a_w:a_w + LANES].reshape(bp, WINDOW, A_KV_HEADS, HEAD_DIM))
        p_wv.append(proj3[:, seq - WINDOW:, a_w + LANES:a_w + 2 * LANES].reshape(bp, WINDOW, A_KV_HEADS, HEAD_DIM))
        st5 = st.reshape(bp, HG_HEADS, HEAD_DIM, HG_HEADS, HEAD_DIM)
        p_s.append(jnp.stack([st5[:, h, :, h, :] for h in range(HG_HEADS)], 1).transpose(0, 1, 3, 2))
        p_cb.append(proj3[:, seq - (CONV_W - 1):, IN_COLS - 2 * LRU_WIDTH:IN_COLS - LRU_WIDTH])
        p_h.append(hl.reshape(bp, LRU_WIDTH))
        p_mk.append(mkv[:, :X_WIDTH].reshape(bp, N_MEM, X_HEADS, HEAD_DIM))
        p_mv.append(mkv[:, X_WIDTH:].reshape(bp, N_MEM, X_HEADS, HEAD_DIM))

        s_cb.append(nbuf.reshape(nd, CONV_W - 1, LRU_WIDTH))
        s_h.append(nh)

    unkey = lambda c: c.reshape(depth, nd, A_KV_HEADS, HEAD_DIM, WINDOW).transpose(0, 1, 4, 2, 3)
    s_s = state_t.reshape(depth, HG_HEADS, HEAD_DIM, HEAD_DIM, nd).transpose(0, 4, 1, 2, 3)
    return (xp.reshape(bp, seq, d), xs.reshape(nd, 1, d),
            jnp.stack(p_wk), jnp.stack(p_wv), jnp.stack(p_s), jnp.stack(p_cb), jnp.stack(p_h),
            jnp.stack(p_mk), jnp.stack(p_mv),
            unkey(ckt), unkey(cvt), s_s, jnp.stack(s_cb), jnp.stack(s_h))
```

```python
import functools
import math

import jax
import jax.numpy as jnp
from jax import lax
from jax.experimental import pallas as pl
from jax.experimental.pallas import tpu as pltpu
from jax.experimental.pallas import tpu_sc as plsc

F32 = jnp.float32
BF16 = jnp.bfloat16
MIX_DTYPE = BF16

D_MODEL = 1024
DEPTH = 4
HEAD_DIM = 64
A_HEADS = 8
A_KV_HEADS = 2
A_REP = A_HEADS // A_KV_HEADS
WINDOW = 128
A_QB = 2
N_BUCKETS = 32
MAX_DISTANCE = 128
HG_WIDTH = 256
HG_HEADS = 4
HG_CHUNK = 64
HG_TB = 32
HG_MROWS = 8
LOG2E = math.log2(math.e)
LRU_WIDTH = 256
CONV_W = 4
LRU_C = 8.0
N_MEM = 256
X_HEADS = 4
X_WIDTH = X_HEADS * HEAD_DIM
X_QROWS = 8
N_GROUPS = 4
EXP_PER_GROUP = 4
N_EXPERTS = N_GROUPS * EXP_PER_GROUP
EXP_FF = D_MODEL // 4
ALPHA = (2 * DEPTH) ** 0.25
LN_EPS = 1e-5
RMS_EPS = 1e-6
IN_COLS = 2304
SCALE = HEAD_DIM ** -0.5
NEG = -1e30
LANES = 128
SUBLANES = 8
ROUTER_LANES = 128
ROUTER_ROWS = 32
XG_WIDTH = D_MODEL + ROUTER_LANES
XG_PLANES = XG_WIDTH // LANES
X_PLANES = D_MODEL // LANES
MOE_TM = 512
ROW_TILE = 1024
LRU_TILE = 512
DEC_BLOCK = 16
SC_WINDOW = 128
VMEM_LIMIT = 48 * 1024 * 1024


def _cparams(sem):
    return pltpu.CompilerParams(dimension_semantics=sem, vmem_limit_bytes=VMEM_LIMIT)


def _bdot(a, b):
    return jnp.dot(a.astype(BF16), b.astype(BF16), preferred_element_type=F32)


def _bdot_nt(a, b):
    return lax.dot_general(a.astype(BF16), b.astype(BF16), (((1,), (1,)), ((), ())),
                           preferred_element_type=F32)


def _bdot_tn(a, b):
    return lax.dot_general(a.astype(BF16), b.astype(BF16), (((0,), (0,)), ((), ())),
                           preferred_element_type=F32)


def _rb(x):
    return x.astype(BF16).astype(F32)


def _silu(x):
    return x * jax.nn.sigmoid(x)


def _neg_expm1(x):
    return -jnp.tanh(0.5 * x) * (jnp.exp(x) + 1.0)


def _softplus(x):
    return jnp.maximum(x, 0.0) + jnp.log1p(jnp.exp(-jnp.abs(x)))


def _gelu_tanh(x):
    return 0.5 * x * (1.0 + jnp.tanh(math.sqrt(2.0 / math.pi) * (x + 0.044715 * (x * x * x))))


def _layer_norm(y, g, b):
    mu = jnp.mean(y, -1, keepdims=True)
    yc = y - mu
    var = jnp.mean(yc * yc, -1, keepdims=True)
    return yc * lax.rsqrt(var + LN_EPS) * g + b


def _rows(x_ref):
    if len(x_ref.shape) == 2:
        return x_ref[...]
    return jnp.concatenate([x_ref[j] for j in range(x_ref.shape[0])], axis=1)


def _rows_spec(x, tm, nargs):
    if x.ndim == 2:
        return pl.BlockSpec((tm, x.shape[1]), (lambda i: (i, 0)) if nargs == 1 else (lambda i, j: (i, 0)))
    blk = (x.shape[0], tm, LANES)
    return pl.BlockSpec(blk, (lambda i: (0, i, 0)) if nargs == 1 else (lambda i, j: (0, i, 0)))


def _mm_kernel(x_ref, w_ref, o_ref):
    o_ref[...] = _bdot(_rows(x_ref), w_ref[...])


def matmul(x, w, tm, tn, layer=None):
    m = x.shape[-2]
    k, n = w.shape[-2:]
    if w.ndim == 2:
        w_spec = pl.BlockSpec((k, tn), lambda i, j: (0, j))
    else:
        w_spec = pl.BlockSpec((None, k, tn), lambda i, j: (layer, 0, j))
    return pl.pallas_call(
        _mm_kernel,
        grid=(m // tm, n // tn),
        in_specs=[_rows_spec(x, tm, 2), w_spec],
        out_specs=pl.BlockSpec((tm, tn), lambda i, j: (i, j)),
        out_shape=jax.ShapeDtypeStruct((m, n), F32),
        compiler_params=_cparams(("parallel", "parallel")),
        name="matmul",
    )(x, w)


def _proj_res_ln_kernel(n_in, x_ref, *refs):
    a_refs = refs[:n_in]
    w_refs = refs[n_in:2 * n_in]
    g_ref, b_ref, o_ref = refs[2 * n_in:]
    y = ALPHA * _rows(x_ref)
    for a_ref, w_ref in zip(a_refs, w_refs):
        y = y + _bdot(a_ref[...], w_ref[...])
    o_ref[...] = _layer_norm(y, g_ref[...], b_ref[...])


def proj_res_ln(x, a_list, w_list, g, b, tm, layer=None):
    m = x.shape[-2]
    n_in = len(a_list)
    in_specs = [_rows_spec(x, tm, 1)]
    in_specs += [pl.BlockSpec((tm, a.shape[1]), lambda i: (i, 0)) for a in a_list]
    if layer is None:
        d = w_list[0].shape[1]
        in_specs += [pl.BlockSpec(w.shape, lambda i: (0, 0)) for w in w_list]
    else:
        (w,) = w_list
        d = w.shape[2]
        row0 = 0
        for a in a_list:
            k = a.shape[1]
            in_specs.append(pl.BlockSpec((None, k, d), functools.partial(lambda i, blk: (layer, blk, 0),
                                                                       blk=row0 // k)))
            row0 += k
        w_list = [w] * n_in
    in_specs += [pl.BlockSpec((1, d), lambda i: (0, 0))] * 2
    return pl.pallas_call(
        functools.partial(_proj_res_ln_kernel, n_in),
        grid=(m // tm,),
        in_specs=in_specs,
        out_specs=pl.BlockSpec((tm, d), lambda i: (i, 0)),
        out_shape=jax.ShapeDtypeStruct((m, d), F32),
        compiler_params=_cparams(("parallel",)),
        name="proj_res_ln",
    )(x, *a_list, *w_list, g, b)


def _attn_prompt_kernel(sink_ref, q_ref, kc_ref, kp_ref, vc_ref, vp_ref, bias_ref, o_ref):
    n = pl.program_id(1)
    col = lax.broadcasted_iota(jnp.int32, (WINDOW, 2 * WINDOW), 1)
    first = jnp.where((n == 0) & (col < WINDOW), NEG, 0.0)
    kk = jnp.concatenate([kp_ref[...], kc_ref[...]], axis=0).astype(BF16)
    vv = jnp.concatenate([vp_ref[...], vc_ref[...]], axis=0).astype(BF16)
    q = q_ref[...].astype(BF16)
    for u in range(A_QB):
        rows = slice(u * WINDOW, (u + 1) * WINDOW)
        keys = slice(u * WINDOW, (u + 2) * WINDOW)
        outs = []
        for h in range(A_HEADS):
            g = h // A_REP
            qh = q[rows, h * HEAD_DIM:(h + 1) * HEAD_DIM]
            kg = kk[keys, g * HEAD_DIM:(g + 1) * HEAD_DIM]
            vg = vv[keys, g * HEAD_DIM:(g + 1) * HEAD_DIM]
            s = _bdot_nt(qh, kg) * SCALE + bias_ref[h]
            if u == 0:
                s = s + first
            sink = sink_ref[h]
            m = jnp.maximum(jnp.max(s, -1, keepdims=True), sink)
            p = jnp.exp(s - m)
            den = jnp.sum(p, -1, keepdims=True) + jnp.exp(sink - m)
            outs.append(_bdot(p / den, vg))
        o_ref[rows, :] = jnp.concatenate(outs, axis=1).astype(o_ref.dtype)


def attn_prompt(proj, sink, bias, batch, seq):
    nb = seq // WINDOW
    ns = nb // A_QB
    tq = A_QB * WINDOW
    qcol = 0
    kcol = (A_HEADS * HEAD_DIM) // LANES
    vcol = kcol + 1

    def cur(c):
        return lambda b, n: (b * ns + n, c)

    def prev(c):
        return lambda b, n: (b * nb + jnp.maximum(n * A_QB - 1, 0), c)

    return pl.pallas_call(
        _attn_prompt_kernel,
        grid=(batch, ns),
        in_specs=[pl.BlockSpec(memory_space=pltpu.SMEM),
                  pl.BlockSpec((tq, A_HEADS * HEAD_DIM), cur(qcol)),
                  pl.BlockSpec((tq, LANES), cur(kcol)),
                  pl.BlockSpec((WINDOW, LANES), prev(kcol)),
                  pl.BlockSpec((tq, LANES), cur(vcol)),
                  pl.BlockSpec((WINDOW, LANES), prev(vcol)),
                  pl.BlockSpec((A_HEADS, WINDOW, 2 * WINDOW), lambda b, n: (0, 0, 0))],
        out_specs=pl.BlockSpec((tq, A_HEADS * HEAD_DIM), cur(0)),
        out_shape=jax.ShapeDtypeStruct((batch * seq, A_HEADS * HEAD_DIM), MIX_DTYPE),
        compiler_params=_cparams(("parallel", "parallel")),
        name="attn_prompt",
    )(sink, proj, proj, proj, proj, proj, bias)


def _attn_decode_kernel(qb_ref, kn_ref, vn_ref, knt_ref, vnt_ref, ck_ref, cv_ref, tab_ref,
                        o_ref, ok_ref, ov_ref):
    bb = qb_ref.shape[0]
    ck = ck_ref[...]
    cv = cv_ref[...]
    qb = qb_ref[...]
    kn = kn_ref[...]
    vn = vn_ref[...]
    bias_j = tab_ref[0]
    bias_n = tab_ref[1][:, 0:1]
    sink = tab_ref[2][:, 0:1]
    s = lax.dot_general(qb.astype(BF16), ck.astype(BF16), (((2,), (1,)), ((0,), (0,))),
                        preferred_element_type=F32) * SCALE + bias_j[None]
    sn = jnp.sum(_rb(qb) * _rb(kn)[:, None, :], -1, keepdims=True) * SCALE + bias_n[None]
    m = jnp.maximum(jnp.maximum(jnp.max(s, -1, keepdims=True), sn), sink[None])
    p = jnp.exp(s - m)
    pn = jnp.exp(sn - m)
    den = jnp.sum(p, -1, keepdims=True) + pn + jnp.exp(sink[None] - m)
    o = lax.dot_general((p / den).astype(BF16), cv.astype(BF16), (((2,), (2,)), ((0,), (0,))),
                        preferred_element_type=F32)
    o_ref[...] = o + _rb(pn / den) * _rb(vn)[:, None, :]
    lane = lax.broadcasted_iota(jnp.int32, (LANES, LANES), 1)
    for b in range(bb):
        ok_ref[b] = jnp.where(lane == WINDOW - 1, knt_ref[:, b:b + 1], pltpu.roll(ck[b], WINDOW - 1, 1))
        ov_ref[b] = jnp.where(lane == WINDOW - 1, vnt_ref[:, b:b + 1], pltpu.roll(cv[b], WINDOW - 1, 1))


def attn_decode(qblk, proj_d, table, cache_k, cache_v, layer, bb):
    nbatch = proj_d.shape[0]
    a_w = A_HEADS * HEAD_DIM
    cols = lambda c: proj_d[:, c:c + LANES].reshape(nbatch // bb, bb, LANES).transpose(0, 2, 1)
    knt, vnt = cols(a_w), cols(a_w + LANES)
    kcol = (A_HEADS * HEAD_DIM) // LANES
    cache_spec = pl.BlockSpec((None, bb, LANES, WINDOW), lambda i: (layer, i, 0, 0))
    col_spec = pl.BlockSpec((None, LANES, bb), lambda i: (i, 0, 0))
    return pl.pallas_call(
        _attn_decode_kernel,
        grid=(nbatch // bb,),
        in_specs=[pl.BlockSpec((bb, A_HEADS, LANES), lambda i: (i, 0, 0)),
                  pl.BlockSpec((bb, LANES), lambda i: (i, kcol)),
                  pl.BlockSpec((bb, LANES), lambda i: (i, kcol + 1)),
                  col_spec, col_spec, cache_spec, cache_spec,
                  pl.BlockSpec((3, A_HEADS, WINDOW), lambda i: (0, 0, 0))],
        out_specs=[pl.BlockSpec((bb, A_HEADS, LANES), lambda i: (i, 0, 0)), cache_spec, cache_spec],
        out_shape=[jax.ShapeDtypeStruct((nbatch, A_HEADS, LANES), F32),
                   jax.ShapeDtypeStruct(cache_k.shape, F32),
                   jax.ShapeDtypeStruct(cache_v.shape, F32)],
        input_output_aliases={5: 1, 6: 2},
        compiler_params=_cparams(("arbitrary",)),
        name="attn_decode",
    )(qblk, proj_d, proj_d, knt, vnt, cache_k, cache_v, table)


def _xattn_decode_kernel(qb_ref, mk_ref, mv_ref, hm_ref, o_ref):
    qb = qb_ref[...]
    s = lax.dot_general(qb.astype(BF16), mk_ref[...].astype(BF16), (((2,), (1,)), ((0,), (0,))),
                        preferred_element_type=F32) * SCALE
    m = jnp.max(s, -1, keepdims=True)
    p = jnp.exp(s - m)
    p = p / jnp.sum(p, -1, keepdims=True)
    o = lax.dot_general(p.astype(BF16), mv_ref[...].astype(BF16), (((2,), (2,)), ((0,), (0,))),
                        preferred_element_type=F32)
    o_ref[...] = jnp.sum(o * hm_ref[...][None], axis=1)


def xattn_decode(qblk, mem_k, mem_v, hmask, layer, bb):
    nbatch = qblk.shape[0]
    mem_spec = pl.BlockSpec((None, bb, X_WIDTH, N_MEM), lambda i: (layer, i, 0, 0))
    return pl.pallas_call(
        _xattn_decode_kernel,
        grid=(nbatch // bb,),
        in_specs=[pl.BlockSpec((bb, X_QROWS, X_WIDTH), lambda i: (i, 0, 0)), mem_spec, mem_spec,
                  pl.BlockSpec((X_QROWS, X_WIDTH), lambda i: (0, 0))],
        out_specs=pl.BlockSpec((bb, X_WIDTH), lambda i: (i, 0)),
        out_shape=jax.ShapeDtypeStruct((nbatch, X_WIDTH), F32),
        compiler_params=_cparams(("parallel",)),
        name="xattn_decode",
    )(qblk, mem_k, mem_v, hmask)


def _hgrn_gates(hq, hf, loglb, log1mlb):
    ls = jnp.minimum(hf, 0.0) - jnp.log1p(jnp.exp(-jnp.abs(hf)))
    b = log1mlb + ls
    lf = jnp.maximum(loglb, b) + jnp.log1p(jnp.exp(-jnp.abs(loglb - b)))
    return _silu(hq), lf, _neg_expm1(lf)


def _hgrn_prompt_kernel(hq_ref, hf_ref, hi_ref, hg_ref, loglb_ref, log1mlb_ref, gain_ref,
                        bdm_ref, hm_ref, ob_ref, st_ref, st_scr, q_scr, k_scr, cum_scr, o_scr, w_scr):
    i = pl.program_id(1)
    tt = hq_ref.shape[0]
    c = HG_CHUNK
    tb = HG_TB

    @pl.when(i == 0)
    def _():
        st_scr[...] = jnp.zeros_like(st_scr)

    qs, lf, kk = _hgrn_gates(hq_ref[...], hf_ref[...], loglb_ref[...], log1mlb_ref[...])
    row = lax.broadcasted_iota(jnp.int32, (tt, HG_WIDTH), 0) & (c - 1)
    cum = lf
    sh = 1
    while sh < c:
        cum = cum + jnp.where(row >= sh, pltpu.roll(cum, sh, 0), 0.0)
        sh *= 2
    q_scr[...] = qs
    k_scr[...] = kk
    cum_scr[...] = cum

    bdm = bdm_ref[...]
    hmask = hm_ref[...]
    causal = (lax.broadcasted_iota(jnp.int32, (tb, tb, HG_WIDTH), 0)
              >= lax.broadcasted_iota(jnp.int32, (tb, tb, HG_WIDTH), 1))
    first_head = lax.broadcasted_iota(jnp.int32, (tb, LANES), 1) < HEAD_DIM

    def chunk(ci, carry):
        r0 = pl.multiple_of(ci * c, c)
        r = pl.ds(r0, c)
        cu = cum_scr[r, :]
        q = q_scr[r, :]
        k = k_scr[r, :]
        v = hi_ref[r, :]
        vb = v.astype(BF16)
        qb = _rb(q)
        cu2 = cu * LOG2E
        last = cu[c - 1:c, :]
        st = st_scr[...]
        o_inter = _bdot_nt(q * jnp.exp(cu), st)
        for j in range(c // tb):
            n0 = tb * j
            ns = n0 + tb
            cut = cu2[n0:ns, :]
            dec = jnp.exp2(jnp.where(causal, cut[:, None, :] - cut[None, :, :], NEG))
            a2 = (dec * k[None, n0:ns, :]).astype(BF16)
            if j:
                past = jnp.exp2(cut[:, None, :] - cu2[None, :n0, :]) * k[None, :n0, :]
                a2 = jnp.concatenate([past.astype(BF16), a2], axis=1)
            q4 = qb[tb * j:tb * (j + 1), None, :] * hmask[None, :, :]
            att = lax.dot_general(q4, a2, (((2,), (2,)), ((0,), (0,))),
                                  preferred_element_type=F32)
            slot = (ci & 1) * (c // tb) + j
            w = jnp.dot(_rb(att.reshape(tb * HG_MROWS, ns)), vb[:ns, :], preferred_element_type=F32)
            halves = []
            for p in range(HG_WIDTH // LANES):
                w_scr[slot, p] = w[:, p * LANES:(p + 1) * LANES]
                rows = [w_scr[slot, p, pl.ds(2 * p + e, tb, stride=HG_MROWS), :] for e in range(2)]
                halves.append(jnp.where(first_head, rows[0], rows[1]))
            o_intra = jnp.concatenate(halves, axis=1)
            o_scr[pl.ds(r0 + tb * j, tb), :] = o_intra + o_inter[tb * j:tb * (j + 1), :]
        upd = _bdot_tn(v, k * jnp.exp(last - cu))
        st_scr[...] = st * jnp.exp(last) + upd * bdm
        return carry

    lax.fori_loop(0, tt // c, chunk, 0, unroll=2)

    o = o_scr[...]
    ms = jnp.dot(o * o, bdm, precision=lax.Precision.HIGHEST,
                 preferred_element_type=F32) * (1.0 / HEAD_DIM)
    ob_ref[...] = (o * lax.rsqrt(ms + RMS_EPS) * gain_ref[...] * _silu(hg_ref[...])).astype(ob_ref.dtype)

    @pl.when(i == pl.num_programs(1) - 1)
    def _():
        st_ref[...] = st_scr[...]


def hgrn_prompt(proj, loglb, log1mlb, gain4, bdm, hmask, batch, seq, tt):
    nt = seq // tt
    base = (A_HEADS + 2 * A_KV_HEADS) * HEAD_DIM // HG_WIDTH

    def col(cblk):
        return pl.BlockSpec((tt, HG_WIDTH), lambda b, i: (b * nt + i, cblk))

    row_spec = pl.BlockSpec((1, HG_WIDTH), lambda b, i: (0, 0))
    mat_spec = pl.BlockSpec((HG_WIDTH, HG_WIDTH), lambda b, i: (0, 0))
    return pl.pallas_call(
        _hgrn_prompt_kernel,
        grid=(batch, nt),
        in_specs=[col(base), col(base + 1), col(base + 2), col(base + 3),
                  row_spec, row_spec, row_spec, mat_spec,
                  pl.BlockSpec((HG_MROWS, HG_WIDTH), lambda b, i: (0, 0))],
        out_specs=[pl.BlockSpec((tt, HG_WIDTH), lambda b, i: (b * nt + i, 0)),
                   pl.BlockSpec((None, HG_WIDTH, HG_WIDTH), lambda b, i: (b, 0, 0))],
        out_shape=[jax.ShapeDtypeStruct((batch * seq, HG_WIDTH), MIX_DTYPE),
                   jax.ShapeDtypeStruct((batch, HG_WIDTH, HG_WIDTH), F32)],
        scratch_shapes=[pltpu.VMEM((HG_WIDTH, HG_WIDTH), F32),
                        pltpu.VMEM((tt, HG_WIDTH), F32),
                        pltpu.VMEM((tt, HG_WIDTH), F32),
                        pltpu.VMEM((tt, HG_WIDTH), F32),
                        pltpu.VMEM((tt, HG_WIDTH), F32),
                        pltpu.VMEM((2 * (HG_CHUNK // HG_TB), HG_WIDTH // LANES, HG_TB * HG_MROWS, LANES), F32)],
        compiler_params=_cparams(("parallel", "arbitrary")),
        name="hgrn_prompt",
    )(proj, proj, proj, proj, loglb, log1mlb, gain4, bdm, hmask)


def _hgrn_decode_kernel(hq_ref, hf_ref, hi_ref, hg_ref, loglb_ref, log1mlb_ref, gain_ref, s_ref,
                        ob_ref, so_ref):
    nb = hq_ref.shape[1]
    qs, lf, kk = _hgrn_gates(hq_ref[...], hf_ref[...], loglb_ref[...], log1mlb_ref[...])
    v = hi_ref[...]
    f = jnp.exp(lf)
    s = s_ref[...].reshape(HEAD_DIM, HEAD_DIM, nb)
    att = jnp.sum(_rb(qs) * _rb(kk), axis=0, keepdims=True)
    o = _rb(att) * _rb(v) + jnp.sum(_rb(qs * f)[:, None, :] * _rb(s), axis=0)
    s_new = f[:, None, :] * s + _rb(kk)[:, None, :] * _rb(v)[None, :, :]
    so_ref[...] = s_new.reshape(HEAD_DIM * HEAD_DIM, nb)
    ms = jnp.mean(o * o, axis=0, keepdims=True)
    ob_ref[...] = (o * lax.rsqrt(ms + RMS_EPS) * gain_ref[...] * _silu(hg_ref[...])).astype(ob_ref.dtype)


def hgrn_decode(gates_t, loglb_t, log1mlb_t, gain_t, state_t, layer):
    nb = gates_t.shape[1]

    def blk(off):
        return pl.BlockSpec((HEAD_DIM, nb), lambda h: (off * HG_HEADS + h, 0))

    par = pl.BlockSpec((HEAD_DIM, nb), lambda h: (h, 0))
    st = pl.BlockSpec((None, HEAD_DIM * HEAD_DIM, nb), lambda h: (layer, h, 0))
    return pl.pallas_call(
        _hgrn_decode_kernel,
        grid=(HG_HEADS,),
        in_specs=[blk(0), blk(1), blk(2), blk(3), par, par,
                  pl.BlockSpec((HEAD_DIM, nb), lambda h: (0, 0)), st],
        out_specs=[par, st],
        out_shape=[jax.ShapeDtypeStruct((HG_WIDTH, nb), F32),
                   jax.ShapeDtypeStruct(state_t.shape, F32)],
        input_output_aliases={7: 1},
        compiler_params=_cparams(("arbitrary",)),
        name="hgrn_decode",
    )(gates_t, gates_t, gates_t, gates_t, loglb_t, log1mlb_t, gain_t, state_t)


def _lru_gates(xc, wa_ref, ba_ref, wx_ref, bx_ref, lam_ref):
    r = jax.nn.sigmoid(_bdot(xc, wa_ref[...]) + ba_ref[...])
    gi = jax.nn.sigmoid(_bdot(xc, wx_ref[...]) + bx_ref[...])
    log_a = -LRU_C * r * _softplus(-lam_ref[...])
    a = jnp.exp(log_a)
    bterm = jnp.sqrt(_neg_expm1(2.0 * log_a)) * (gi * xc)
    return a, bterm


def _lru_prompt_kernel(lx_ref, lg_ref, cw_ref, cb_ref, wa_ref, ba_ref, wx_ref, bx_ref, lam_ref,
                       oc_ref, hl_ref, ext_scr, h_scr):
    i = pl.program_id(1)
    tt = lx_ref.shape[0]
    pad = SUBLANES

    @pl.when(i == 0)
    def _():
        ext_scr[0:pad, :] = jnp.zeros((pad, LRU_WIDTH), F32)
        h_scr[...] = jnp.zeros_like(h_scr)

    x = lx_ref[...]
    ext_scr[pad:pad + tt, :] = x
    xc = cb_ref[...] + cw_ref[CONV_W - 1:CONV_W, :] * x
    for j in range(CONV_W - 1):
        back = CONV_W - 1 - j
        xc = xc + cw_ref[j:j + 1, :] * ext_scr[pad - back:pad - back + tt, :]
    ext_scr[0:pad, :] = x[tt - pad:tt, :]

    a, bterm = _lru_gates(xc, wa_ref, ba_ref, wx_ref, bx_ref, lam_ref)
    row = lax.broadcasted_iota(jnp.int32, (tt, LRU_WIDTH), 0)
    sh = 1
    while sh < tt:
        keep = row >= sh
        b_s = jnp.where(keep, pltpu.roll(bterm, sh, 0), 0.0)
        a_s = jnp.where(keep, pltpu.roll(a, sh, 0), 1.0)
        bterm = a * b_s + bterm
        a = a * a_s
        sh *= 2
    h = a * h_scr[...] + bterm
    h_scr[...] = h[tt - 1:tt, :]
    oc_ref[...] = (h * _gelu_tanh(lg_ref[...])).astype(oc_ref.dtype)

    @pl.when(i == pl.num_programs(1) - 1)
    def _():
        hl_ref[...] = h[tt - 1:tt, :]


def lru_prompt(proj, conv_w, conv_b, wa_bd, ba, wx_bd, bx, lam, batch, seq, tt):
    nt = seq // tt
    base = IN_COLS // LRU_WIDTH - 2

    def col(cblk):
        return pl.BlockSpec((tt, LRU_WIDTH), lambda b, i: (b * nt + i, cblk))

    row_spec = pl.BlockSpec((1, LRU_WIDTH), lambda b, i: (0, 0))
    mat_spec = pl.BlockSpec((LRU_WIDTH, LRU_WIDTH), lambda b, i: (0, 0))
    return pl.pallas_call(
        _lru_prompt_kernel,
        grid=(batch, nt),
        in_specs=[col(base), col(base + 1),
                  pl.BlockSpec((CONV_W, LRU_WIDTH), lambda b, i: (0, 0)), row_spec,
                  mat_spec, row_spec, mat_spec, row_spec, row_spec],
        out_specs=[pl.BlockSpec((tt, LRU_WIDTH), lambda b, i: (b * nt + i, 0)),
                   pl.BlockSpec((None, 1, LRU_WIDTH), lambda b, i: (b, 0, 0))],
        out_shape=[jax.ShapeDtypeStruct((batch * seq, LRU_WIDTH), MIX_DTYPE),
                   jax.ShapeDtypeStruct((batch, 1, LRU_WIDTH), F32)],
        scratch_shapes=[pltpu.VMEM((tt + 8, LRU_WIDTH), F32),
                        pltpu.VMEM((1, LRU_WIDTH), F32)],
        compiler_params=_cparams(("parallel", "arbitrary")),
        name="lru_prompt",
    )(proj, proj, conv_w, conv_b, wa_bd, ba, wx_bd, bx, lam)


def _lru_decode_kernel(lx_ref, lg_ref, buf_ref, h0_ref, cw_ref, cb_ref, wa_ref, ba_ref, wx_ref,
                       bx_ref, lam_ref, oc_ref, hn_ref, nbuf_ref):
    x = lx_ref[...]
    buf = buf_ref[...]
    xc = cb_ref[...] + cw_ref[CONV_W - 1:CONV_W, :] * x
    for j in range(CONV_W - 1):
        xc = xc + cw_ref[j:j + 1, :] * buf[:, j * LRU_WIDTH:(j + 1) * LRU_WIDTH]
    a, bterm = _lru_gates(xc, wa_ref, ba_ref, wx_ref, bx_ref, lam_ref)
    h = a * h0_ref[...] + bterm
    hn_ref[...] = h
    oc_ref[...] = (h * _gelu_tanh(lg_ref[...])).astype(oc_ref.dtype)
    nbuf_ref[...] = jnp.concatenate([buf[:, LRU_WIDTH:], x], axis=1)


def lru_decode(proj_d, conv_buf, h0, conv_w, conv_b, wa_bd, ba, wx_bd, bx, lam):
    nb = proj_d.shape[0]
    base = IN_COLS // LRU_WIDTH - 2
    row_spec = pl.BlockSpec((1, LRU_WIDTH), lambda i: (0, 0))
    mat_spec = pl.BlockSpec((LRU_WIDTH, LRU_WIDTH), lambda i: (0, 0))
    act = pl.BlockSpec((nb, LRU_WIDTH), lambda i: (0, 0))
    bufs = pl.BlockSpec((nb, (CONV_W - 1) * LRU_WIDTH), lambda i: (0, 0))
    return pl.pallas_call(
        _lru_decode_kernel,
        grid=(1,),
        in_specs=[pl.BlockSpec((nb, LRU_WIDTH), lambda i: (0, base)),
                  pl.BlockSpec((nb, LRU_WIDTH), lambda i: (0, base + 1)),
                  bufs, act, pl.BlockSpec((CONV_W, LRU_WIDTH), lambda i: (0, 0)), row_spec,
                  mat_spec, row_spec, mat_spec, row_spec, row_spec],
        out_specs=[act, act, bufs],
        out_shape=[jax.ShapeDtypeStruct((nb, LRU_WIDTH), F32),
                   jax.ShapeDtypeStruct((nb, LRU_WIDTH), F32),
                   jax.ShapeDtypeStruct((nb, (CONV_W - 1) * LRU_WIDTH), F32)],
        compiler_params=_cparams(("arbitrary",)),
        name="lru_decode",
    )(proj_d, proj_d, conv_buf, h0, conv_w, conv_b, wa_bd, ba, wx_bd, bx, lam)


def _xattn_prompt_kernel(x_ref, wq_ref, mk_ref, mv_ref, wo_ref, g_ref, b_ref, wr_ref, br_ref, o_ref):
    x = x_ref[...]
    q = _bdot(x, wq_ref[...]).astype(BF16)
    mk = mk_ref[...].astype(BF16)
    mv = mv_ref[...].astype(BF16)
    outs = []
    for h in range(X_HEADS):
        sl = slice(h * HEAD_DIM, (h + 1) * HEAD_DIM)
        s = _bdot_nt(q[:, sl], mk[:, sl]) * SCALE
        m = jnp.max(s, -1, keepdims=True)
        p = jnp.exp(s - m)
        p = p / jnp.sum(p, -1, keepdims=True)
        outs.append(_bdot(p, mv[:, sl]))
    o = jnp.concatenate(outs, axis=1)
    y = _layer_norm(ALPHA * x + _bdot(o, wo_ref[...]), g_ref[...], b_ref[...])
    logits_t = _bdot_nt(wr_ref[...], y) + br_ref[...]
    gate_t = _route_rows(logits_t[:ROUTER_ROWS, :])
    gate_t = jnp.concatenate([gate_t, jnp.zeros((ROUTER_LANES - ROUTER_ROWS, y.shape[0]), F32)], axis=0)
    for j in range(D_MODEL // LANES):
        o_ref[j] = y[:, j * LANES:(j + 1) * LANES]
    o_ref[D_MODEL // LANES] = gate_t.T


def xattn_prompt(x, wq, mem_kv, wo, g, b, wr, br, batch, seq, tt):
    nt = seq // tt
    const = lambda bi, i: (0, 0)
    return pl.pallas_call(
        _xattn_prompt_kernel,
        grid=(batch, nt),
        in_specs=[pl.BlockSpec((tt, D_MODEL), lambda bi, i: (bi * nt + i, 0)),
                  pl.BlockSpec((D_MODEL, X_WIDTH), const),
                  pl.BlockSpec((N_MEM, X_WIDTH), lambda bi, i: (bi, 0)),
                  pl.BlockSpec((N_MEM, X_WIDTH), lambda bi, i: (bi, 1)),
                  pl.BlockSpec((X_WIDTH, D_MODEL), const),
                  pl.BlockSpec((1, D_MODEL), const),
                  pl.BlockSpec((1, D_MODEL), const),
                  pl.BlockSpec((ROUTER_LANES, D_MODEL), const),
                  pl.BlockSpec((ROUTER_LANES, 1), const)],
        out_specs=pl.BlockSpec((XG_PLANES, tt, LANES), lambda bi, i: (0, bi * nt + i, 0)),
        out_shape=jax.ShapeDtypeStruct((XG_PLANES, batch * seq, LANES), F32),
        compiler_params=_cparams(("parallel", "parallel")),
        name="xattn_prompt",
    )(x, wq, mem_kv, mem_kv, wo, g, b, wr, br)


def _route(logits):
    lane = lax.broadcasted_iota(jnp.int32, logits.shape, 1)
    big = jnp.int32(ROUTER_LANES)
    ninf = -jnp.inf
    gl = jnp.where(lane < N_GROUPS, logits, ninf)
    gm = jnp.max(gl, -1, keepdims=True)
    g_val = 1.0 / jnp.sum(jnp.exp(gl - gm), -1, keepdims=True)
    g_idx = jnp.min(jnp.where(gl == gm, lane, big), -1, keepdims=True)
    lo = N_GROUPS + EXP_PER_GROUP * g_idx
    el = jnp.where((lane >= lo) & (lane < lo + EXP_PER_GROUP), logits, ninf)
    v1 = jnp.max(el, -1, keepdims=True)
    i1 = jnp.min(jnp.where(el == v1, lane, big), -1, keepdims=True)
    el2 = jnp.where(lane == i1, ninf, el)
    v2 = jnp.max(el2, -1, keepdims=True)
    i2 = jnp.min(jnp.where(el2 == v2, lane, big), -1, keepdims=True)
    e2 = jnp.exp(v2 - v1)
    w1 = g_val / (1.0 + e2)
    w2 = g_val * e2 / (1.0 + e2)
    return jnp.where(lane == i1, w1, 0.0) + jnp.where(lane == i2, w2, 0.0), g_idx


def _route_rows(logits):
    row = lax.broadcasted_iota(jnp.int32, logits.shape, 0)
    big = jnp.int32(ROUTER_ROWS)
    ninf = -jnp.inf
    gl = jnp.where(row < N_GROUPS, logits, ninf)
    gm = jnp.max(gl, 0, keepdims=True)
    g_val = 1.0 / jnp.sum(jnp.exp(gl - gm), 0, keepdims=True)
    g_idx = jnp.min(jnp.where(gl == gm, row, big), 0, keepdims=True)
    lo = N_GROUPS + EXP_PER_GROUP * g_idx
    el = jnp.where((row >= lo) & (row < lo + EXP_PER_GROUP), logits, ninf)
    v1 = jnp.max(el, 0, keepdims=True)
    i1 = jnp.min(jnp.where(el == v1, row, big), 0, keepdims=True)
    el2 = jnp.where(row == i1, ninf, el)
    v2 = jnp.max(el2, 0, keepdims=True)
    i2 = jnp.min(jnp.where(el2 == v2, row, big), 0, keepdims=True)
    e2 = jnp.exp(v2 - v1)
    w1 = g_val / (1.0 + e2)
    w2 = g_val * e2 / (1.0 + e2)
    gate = jnp.where(row == i1, w1, 0.0) + jnp.where(row == i2, w2, 0.0)
    return jnp.where(row == 0, g_idx.astype(F32), gate)


def _moe_dense_kernel(x_ref, wr_ref, br_ref, wg_ref, wu_ref, wd_ref, g_ref, b_ref, o_ref,
                      gate_scr, acc_scr):
    e = pl.program_id(1)

    @pl.when(e == 0)
    def _():
        logits = _bdot(x_ref[...], wr_ref[...]) + br_ref[...]
        gate_scr[...] = _route(logits)[0]
        acc_scr[...] = jnp.zeros_like(acc_scr)

    xb = x_ref[...].astype(BF16)
    lane = lax.broadcasted_iota(jnp.int32, gate_scr.shape, 1)
    gcol = jnp.sum(jnp.where(lane == e + N_GROUPS, gate_scr[...], 0.0), -1, keepdims=True)
    hid = _silu(_bdot(xb, wg_ref[...])) * _bdot(xb, wu_ref[...])
    acc_scr[...] += _bdot(hid * gcol, wd_ref[...])

    @pl.when(e == pl.num_programs(1) - 1)
    def _():
        o_ref[...] = _layer_norm(ALPHA * x_ref[...] + acc_scr[...], g_ref[...], b_ref[...])


def moe_dense(x, wr, br, wg, wu, wd, g, b, layer, tm):
    m = x.shape[0]
    return pl.pallas_call(
        _moe_dense_kernel,
        grid=(m // tm, N_EXPERTS),
        in_specs=[pl.BlockSpec((tm, D_MODEL), lambda i, e: (i, 0)),
                  pl.BlockSpec((D_MODEL, ROUTER_LANES), lambda i, e: (0, 0)),
                  pl.BlockSpec((1, ROUTER_LANES), lambda i, e: (0, 0)),
                  pl.BlockSpec((None, None, D_MODEL, EXP_FF), lambda i, e: (layer, e, 0, 0)),
                  pl.BlockSpec((None, None, D_MODEL, EXP_FF), lambda i, e: (layer, e, 0, 0)),
                  pl.BlockSpec((None, None, EXP_FF, D_MODEL), lambda i, e: (layer, e, 0, 0)),
                  pl.BlockSpec((1, D_MODEL), lambda i, e: (0, 0)),
                  pl.BlockSpec((1, D_MODEL), lambda i, e: (0, 0))],
        out_specs=pl.BlockSpec((tm, D_MODEL), lambda i, e: (i, 0)),
        out_shape=jax.ShapeDtypeStruct((m, D_MODEL), F32),
        scratch_shapes=[pltpu.VMEM((tm, ROUTER_LANES), F32), pltpu.VMEM((tm, D_MODEL), F32)],
        compiler_params=_cparams(("parallel", "arbitrary")),
        name="moe_dense",
    )(x, wr, br, wg, wu, wd, g, b)


def _sc_mesh():
    return plsc.VectorSubcoreMesh(core_axis_name="core", subcore_axis_name="subcore")


def sc_scatter_rows(x, idx, n_out):
    r = x.shape[0]

    @functools.partial(pl.kernel, out_type=jax.ShapeDtypeStruct((n_out, LANES), x.dtype),
                       mesh=_sc_mesh(), scratch_types=[], name="sc_scatter_rows")
    def k(x_hbm, i_hbm, o_hbm):
        def body(x_vmem, i_vmem):
            pltpu.sync_copy(x_vmem, o_hbm.at[i_vmem.at[0]])

        pltpu.emit_pipeline(
            body,
            grid=(r // SC_WINDOW,),
            in_specs=[pl.BlockSpec((SC_WINDOW, LANES), lambda i: (i, 0)),
                      pl.BlockSpec((1, SC_WINDOW), lambda i: (0, i))],
            out_specs=[],
            core_axis_name=("core", "subcore"),
            dimension_semantics=(pltpu.PARALLEL,),
        )(x_hbm, i_hbm)

    return k(x, idx.reshape(1, r))


def sc_gather_rows(table, idx):
    r = idx.shape[0]

    @functools.partial(pl.kernel, out_type=jax.ShapeDtypeStruct((r, LANES), table.dtype),
                       mesh=_sc_mesh(), scratch_types=[], name="sc_gather_rows")
    def k(t_hbm, i_hbm, o_hbm):
        def body(i_vmem, o_vmem):
            pltpu.sync_copy(t_hbm.at[i_vmem.at[0]], o_vmem)

        pltpu.emit_pipeline(
            body,
            grid=(r // SC_WINDOW,),
            in_specs=[pl.BlockSpec((1, SC_WINDOW), lambda i: (0, i))],
            out_specs=[pl.BlockSpec((SC_WINDOW, LANES), lambda i: (i, 0))],
            core_axis_name=("core", "subcore"),
            dimension_semantics=(pltpu.PARALLEL,),
        )(i_hbm, o_hbm)

    return k(table, idx.reshape(1, r))


def _moe_sorted_kernel(tg_ref, nused_ref, xs_ref, wg32_ref, wu32_ref, wd32_ref, g_ref, b_ref, o_ref,
                       wg_ref, wu_ref, wd_ref):
    t = pl.program_id(0)
    used = t < nused_ref[0]

    @pl.when(used & ((t == 0) | (tg_ref[t] != tg_ref[jnp.maximum(t - 1, 0)])))
    def _():
        wg_ref[...] = wg32_ref[...].astype(BF16)
        wu_ref[...] = wu32_ref[...].astype(BF16)
        wd_ref[...] = wd32_ref[...].astype(BF16)

    @pl.when(used)
    def _():
        x = jnp.concatenate([xs_ref[j] for j in range(X_PLANES)], axis=1)
        gate = xs_ref[X_PLANES]
        xb = x.astype(BF16)
        lane = lax.broadcasted_iota(jnp.int32, gate.shape, 1)
        first = N_GROUPS + EXP_PER_GROUP * tg_ref[t]
        acc = jnp.zeros(x.shape, F32)
        for e in range(EXP_PER_GROUP):
            gcol = jnp.sum(jnp.where(lane == first + e, gate, 0.0), -1, keepdims=True)
            hid = _silu(_bdot(xb, wg_ref[e])) * _bdot(xb, wu_ref[e])
            acc = acc + _bdot(hid * gcol, wd_ref[e])
        y = _layer_norm(ALPHA * x + acc, g_ref[...], b_ref[...])
        for j in range(X_PLANES):
            o_ref[j] = y[:, j * LANES:(j + 1) * LANES]


def _group_slots(group_idx, n, tm):
    n_tiles = n // tm + N_GROUPS
    onehot = (group_idx[:, None] == jnp.arange(N_GROUPS)[None, :]).astype(jnp.int32)
    csum = jnp.cumsum(onehot, axis=0)
    counts = csum[-1]
    rank = jnp.sum(onehot * csum, axis=1) - 1
    tiles_g = (counts + tm - 1) // tm
    tile_end = jnp.cumsum(tiles_g)
    slot_base = (tile_end - tiles_g) * tm
    slot = (jnp.sum(onehot * slot_base[None, :], axis=1) + rank).astype(jnp.int32)
    tile_group = jnp.sum((jnp.arange(n_tiles)[:, None] >= tile_end[None, :]).astype(jnp.int32), axis=1)
    tile_group = jnp.minimum(tile_group, N_GROUPS - 1).astype(jnp.int32)
    return slot, tile_group, tile_end[-1:].astype(jnp.int32)


def moe_routed_sc(xg, wg, wu, wd, g, b, layer, n, tm, during_scatter, during_gather, tiles_out):
    slot, tile_group, n_used = _group_slots(xg[X_PLANES, :, 0].astype(jnp.int32), n, tm)
    n_tiles = tile_group.shape[0]
    n_slots = n_tiles * tm
    plane_base = lambda planes: jnp.arange(planes, dtype=jnp.int32)[:, None] * n_slots
    idx_in = (plane_base(XG_PLANES) + slot[None, :]).reshape(-1)
    if tiles_out:
        idx_out = (plane_base(X_PLANES)[None] + slot.reshape(n // SUBLANES, 1, SUBLANES)).reshape(-1)
    else:
        idx_out = (plane_base(X_PLANES) + slot[None, :]).reshape(-1)
    xs = sc_scatter_rows(xg.reshape(XG_PLANES * n, LANES), idx_in, XG_PLANES * n_slots)
    xs = xs.reshape(XG_PLANES, n_slots, LANES)
    xs, side = lax.optimization_barrier((xs, during_scatter()))
    wspec = lambda shp: pl.BlockSpec((None, None, EXP_PER_GROUP) + shp,
                                     lambda t, tg, nu: (layer, tg[t], 0, 0, 0))
    used_tile = lambda t, tg, nu: (0, jnp.minimum(t, nu[0] - 1), 0)
    grid_spec = pltpu.PrefetchScalarGridSpec(
        num_scalar_prefetch=2,
        grid=(n_tiles,),
        in_specs=[pl.BlockSpec((XG_PLANES, tm, LANES), used_tile),
                  wspec((D_MODEL, EXP_FF)), wspec((D_MODEL, EXP_FF)), wspec((EXP_FF, D_MODEL)),
                  pl.BlockSpec((1, D_MODEL), lambda t, tg, nu: (0, 0)),
                  pl.BlockSpec((1, D_MODEL), lambda t, tg, nu: (0, 0))],
        out_specs=pl.BlockSpec((X_PLANES, tm, LANES), used_tile),
        scratch_shapes=[pltpu.VMEM((EXP_PER_GROUP, D_MODEL, EXP_FF), BF16),
                        pltpu.VMEM((EXP_PER_GROUP, D_MODEL, EXP_FF), BF16),
                        pltpu.VMEM((EXP_PER_GROUP, EXP_FF, D_MODEL), BF16)])
    grouped = lambda w: w.reshape(w.shape[0], N_GROUPS, EXP_PER_GROUP, w.shape[2], w.shape[3])
    ys = pl.pallas_call(
        _moe_sorted_kernel,
        grid_spec=grid_spec,
        out_shape=jax.ShapeDtypeStruct((X_PLANES, n_slots, LANES), F32),
        compiler_params=_cparams(("arbitrary",)),
        name="moe_sorted",
    )(tile_group, n_used, xs, grouped(wg), grouped(wu), grouped(wd), g, b)
    y = sc_gather_rows(ys.reshape(X_PLANES * n_slots, LANES), idx_out)
    y, side = lax.optimization_barrier((y, during_gather(side)))
    if tiles_out:
        y = y.reshape(n // SUBLANES, X_PLANES, SUBLANES, LANES).transpose(0, 2, 1, 3)
        return y.reshape(n, D_MODEL), side
    return y.reshape(X_PLANES, n, LANES), side


def _t5_bucket(dist):
    max_exact = N_BUCKETS // 2
    d = jnp.maximum(dist, 0)
    df = jnp.maximum(d, 1).astype(F32)
    log_b = max_exact + (jnp.log(df / max_exact) / math.log(MAX_DISTANCE / max_exact)
                         * (N_BUCKETS - max_exact)).astype(jnp.int32)
    return jnp.where(d < max_exact, d, jnp.minimum(log_b, N_BUCKETS - 1))


def _bucket_lookup(rel_bias, bucket):
    out = jnp.zeros(bucket.shape + (rel_bias.shape[1],), F32)
    for i in range(N_BUCKETS):
        out = jnp.where((bucket == i)[..., None], rel_bias[i].astype(F32), out)
    return out


def _prompt_bias(rel_bias):
    qi = jnp.arange(WINDOW)[:, None]
    kj = jnp.arange(2 * WINDOW)[None, :]
    dist = qi + WINDOW - kj
    bias = _bucket_lookup(rel_bias, _t5_bucket(dist)).transpose(2, 0, 1)
    valid = (dist >= 0) & (dist <= WINDOW)
    return jnp.where(valid[None], bias, NEG)


def _decode_table(rel_bias, attn_sink):
    dist = WINDOW - jnp.arange(WINDOW + 1)
    bias = _bucket_lookup(rel_bias, _t5_bucket(dist)).T
    depth = attn_sink.shape[0]
    wide = lambda v: jnp.broadcast_to(v[..., None], v.shape + (WINDOW,))
    per_layer = lambda t: jnp.broadcast_to(t[None], (depth,) + t.shape)
    return jnp.stack([per_layer(bias[:, :WINDOW]), per_layer(wide(bias[:, WINDOW])),
                      wide(attn_sink.astype(F32))], axis=1)


def _block_ones(width):
    idx = jnp.arange(width) // HEAD_DIM
    return (idx[:, None] == idx[None, :])


def _head_rows_mask():
    head = jnp.arange(HG_WIDTH)[None, :] // HEAD_DIM
    return (head == jnp.arange(HG_MROWS)[:, None]).astype(F32)


def _block_diag(w):
    nblk, s, _ = w.shape
    eye = jnp.eye(nblk, dtype=w.dtype)
    return (eye[:, None, :, None] * w[:, :, None, :]).reshape(nblk * s, nblk * s)


def kernel(x_prompt, x_sample, mem_prompt, cache_win_k, cache_win_v, state_hgrn, state_conv, state_lru, cache_mem_k, cache_mem_v, rel_bias, hg_lb, w_in, attn_sink, hg_gain, conv_w, conv_b, lru_wa, lru_ba, lru_wx, lru_bx, lru_lam, w_out, ln1_g, ln1_b, x_wq, x_wk, x_wv, x_wo, ln2_g, ln2_b, r_gw, r_gb, r_ew, r_eb, e_wg, e_wu, e_wd, ln3_g, ln3_b):
    bp, seq, d = x_prompt.shape
    n_tok = bp * seq
    nd = x_sample.shape[0]
    depth = w_in.shape[0]

    lbs = jnp.cumsum(jax.nn.softmax(hg_lb.astype(F32), axis=0), axis=0)
    lbs = lbs - lbs[0]
    loglb = jnp.log(lbs)
    log1mlb = jnp.log1p(-lbs)
    gain4 = jnp.tile(hg_gain, (1, HG_HEADS))

    bias_p = _prompt_bias(rel_bias)
    bdm256 = _block_ones(HG_WIDTH).astype(F32)
    hmask = _head_rows_mask()

    w_in_b = w_in.astype(BF16)
    w_out_b = w_out.astype(BF16)
    wq_b = x_wq.astype(BF16)
    wkv_b = jnp.concatenate([x_wk, x_wv], axis=-1).astype(BF16)
    wo_b = x_wo.astype(BF16)
    rew = r_ew.transpose(0, 2, 1, 3).reshape(depth, d, N_EXPERTS)
    wr = jnp.concatenate([r_gw, rew, jnp.zeros((depth, d, ROUTER_LANES - N_GROUPS - N_EXPERTS), F32)], -1)
    br = jnp.concatenate([r_gb, r_eb.reshape(depth, N_EXPERTS),
                          jnp.zeros((depth, ROUTER_LANES - N_GROUPS - N_EXPERTS), F32)], -1)

    a_w = A_HEADS * HEAD_DIM
    xp = x_prompt.reshape(bp * seq, d)
    xs = x_sample.reshape(nd, d)
    mem = mem_prompt.reshape(bp * N_MEM, d)
    ckt = cache_win_k.transpose(0, 1, 3, 4, 2).reshape(depth, nd, LANES, WINDOW)
    cvt = cache_win_v.transpose(0, 1, 3, 4, 2).reshape(depth, nd, LANES, WINDOW)
    cmkt = cache_mem_k.transpose(0, 1, 3, 4, 2).reshape(depth, nd, X_WIDTH, N_MEM)
    cmvt = cache_mem_v.transpose(0, 1, 3, 4, 2).reshape(depth, nd, X_WIDTH, N_MEM)
    state_t = state_hgrn.transpose(0, 2, 3, 4, 1).reshape(depth, HG_HEADS * HEAD_DIM * HEAD_DIM, nd)
    dec_tab = _decode_table(rel_bias, attn_sink)
    xq_mask = _head_rows_mask()[:X_QROWS]
    head_group = jnp.arange(A_HEADS) // A_REP

    p_wk, p_wv, p_s, p_cb, p_h, p_mk, p_mv = [], [], [], [], [], [], []
    s_cb, s_h = [], []
    for l in range(depth):
        row = lambda v: v[l].reshape(1, -1)
        wa_bd = _block_diag(lru_wa[l]).astype(BF16)
        wx_bd = _block_diag(lru_wx[l]).astype(BF16)
        lru_args = (conv_w[l], row(conv_b), wa_bd, row(lru_ba), wx_bd, row(lru_bx), row(lru_lam))

        proj = matmul(xp, w_in_b, ROW_TILE, IN_COLS, layer=l)
        oa = attn_prompt(proj, attn_sink[l], bias_p, bp, seq)
        ob, st = hgrn_prompt(proj, row(loglb), row(log1mlb), row(gain4), bdm256, hmask, bp, seq, ROW_TILE)
        oc, hl = lru_prompt(proj, *lru_args, bp, seq, LRU_TILE)
        xp = proj_res_ln(xp, [oa, ob, oc], [w_out_b], row(ln1_g), row(ln1_b), ROW_TILE, layer=l)
        mkv = matmul(mem, wkv_b[l], N_MEM, 2 * X_WIDTH)
        xg = xattn_prompt(xp, wq_b[l], mkv, wo_b[l], row(ln2_g), row(ln2_b), wr[l].T, br[l][:, None],
                          bp, seq, ROW_TILE)
        def decode_mixers(xs=xs, ckt=ckt, cvt=cvt, state_t=state_t, l=l, lru_args=lru_args,
                          row=row):
            projd = matmul(xs, w_in_b, nd, IN_COLS, layer=l)
            q3 = projd[:, :a_w].reshape(nd, A_HEADS, 1, HEAD_DIM)
            on_group = head_group[None, :, None, None] == jnp.arange(A_KV_HEADS)[None, None, :, None]
            qblk = jnp.where(on_group, q3, 0.0).reshape(nd, A_HEADS, LANES)
            o3, ckt, cvt = attn_decode(qblk, projd, dec_tab[l], ckt, cvt, l, DEC_BLOCK)
            o4 = o3.reshape(nd, A_HEADS, A_KV_HEADS, HEAD_DIM)
            oa = jnp.sum(jnp.where(on_group, o4, 0.0), axis=2).reshape(nd, a_w)
            gates_t = projd[:, a_w + 2 * LANES:a_w + 2 * LANES + 4 * HG_WIDTH].T
            bc = lambda v: jnp.broadcast_to(v[:, None], (v.shape[0], nd))
            ob_t, state_t = hgrn_decode(gates_t, bc(loglb[l]), bc(log1mlb[l]), bc(hg_gain[l]), state_t, l)
            oc, nh, nbuf = lru_decode(projd, state_conv[l].reshape(nd, -1), state_lru[l], *lru_args)
            xs = proj_res_ln(xs, [oa, ob_t.T, oc], [w_out_b], row(ln1_g), row(ln1_b), nd, layer=l)
            return xs, ckt, cvt, state_t, nh, nbuf

        def decode_rest(side, l=l, row=row):
            xs = side[0]
            qd = matmul(xs, wq_b[l], nd, X_WIDTH)
            qdb = qd[:, None, :] * xq_mask[None, :, :]
            od = xattn_decode(qdb, cmkt, cmvt, xq_mask, l, DEC_BLOCK)
            xs = proj_res_ln(xs, [od], [wo_b[l]], row(ln2_g), row(ln2_b), nd)
            xs = moe_dense(xs, wr[l], br[l:l + 1], e_wg, e_wu, e_wd, row(ln3_g), row(ln3_b), l, nd)
            return (xs,) + tuple(side[1:])

        xp, (xs, ckt, cvt, state_t, nh, nbuf) = moe_routed_sc(
            xg, e_wg, e_wu, e_wd, row(ln3_g), row(ln3_b), l, n_tok, MOE_TM, decode_mixers, decode_rest,
            tiles_out=(l == depth - 1))

        proj3 = proj.reshape(bp, seq, IN_COLS)
        p_wk.append(proj3[:, seq - WINDOW:, a_w:a_w + LANES].reshape(bp, WINDOW, A_KV_HEADS, HEAD_DIM))
        p_wv.append(proj3[:, seq - WINDOW:, a_w + LANES:a_w + 2 * LANES].reshape(bp, WINDOW, A_KV_HEADS, HEAD_DIM))
        st5 = st.reshape(bp, HG_HEADS, HEAD_DIM, HG_HEADS, HEAD_DIM)
        p_s.append(jnp.stack([st5[:, h, :, h, :] for h in range(HG_HEADS)], 1).transpose(0, 1, 3, 2))
        p_cb.append(proj3[:, seq - (CONV_W - 1):, IN_COLS - 2 * LRU_WIDTH:IN_COLS - LRU_WIDTH])
        p_h.append(hl.reshape(bp, LRU_WIDTH))
        p_mk.append(mkv[:, :X_WIDTH].reshape(bp, N_MEM, X_HEADS, HEAD_DIM))
        p_mv.append(mkv[:, X_WIDTH:].reshape(bp, N_MEM, X_HEADS, HEAD_DIM))

        s_cb.append(nbuf.reshape(nd, CONV_W - 1, LRU_WIDTH))
        s_h.append(nh)

    unkey = lambda c: c.reshape(depth, nd, A_KV_HEADS, HEAD_DIM, WINDOW).transpose(0, 1, 4, 2, 3)
    s_s = state_t.reshape(depth, HG_HEADS, HEAD_DIM, HEAD_DIM, nd).transpose(0, 4, 1, 2, 3)
    return (xp.reshape(bp, seq, d), xs.reshape(nd, 1, d),
            jnp.stack(p_wk), jnp.stack(p_wv), jnp.stack(p_s), jnp.stack(p_cb), jnp.stack(p_h),
            jnp.stack(p_mk), jnp.stack(p_mv),
            unkey(ckt), unkey(cvt), s_s, jnp.stack(s_cb), jnp.stack(s_h))
```

```python
import functools
import math

import jax
import jax.numpy as jnp
from jax import lax
from jax.experimental import pallas as pl
from jax.experimental.pallas import tpu as pltpu
from jax.experimental.pallas import tpu_sc as plsc

F32 = jnp.float32
BF16 = jnp.bfloat16
MIX_DTYPE = BF16

D_MODEL = 1024
DEPTH = 4
HEAD_DIM = 64
A_HEADS = 8
A_KV_HEADS = 2
A_REP = A_HEADS // A_KV_HEADS
WINDOW = 128
A_QB = 2
N_BUCKETS = 32
MAX_DISTANCE = 128
HG_WIDTH = 256
HG_HEADS = 4
HG_CHUNK = 64
HG_TB = 32
HG_UNROLL = 4
HG_MROWS = 8
LOG2E = math.log2(math.e)
LRU_WIDTH = 256
CONV_W = 4
LRU_C = 8.0
N_MEM = 256
X_HEADS = 4
X_WIDTH = X_HEADS * HEAD_DIM
X_QROWS = 8
N_GROUPS = 4
EXP_PER_GROUP = 4
N_EXPERTS = N_GROUPS * EXP_PER_GROUP
EXP_FF = D_MODEL // 4
ALPHA = (2 * DEPTH) ** 0.25
LN_EPS = 1e-5
RMS_EPS = 1e-6
IN_COLS = 2304
SCALE = HEAD_DIM ** -0.5
NEG = -1e30
LANES = 128
SUBLANES = 8
ROUTER_LANES = 128
ROUTER_ROWS = 32
XG_WIDTH = D_MODEL + ROUTER_LANES
XG_PLANES = XG_WIDTH // LANES
X_PLANES = D_MODEL // LANES
MOE_TM = 512
ROW_TILE = 1024
LRU_TILE = 512
DEC_BLOCK = 16
SC_WINDOW = 128
VMEM_LIMIT = 48 * 1024 * 1024


def _cparams(sem):
    return pltpu.CompilerParams(dimension_semantics=sem, vmem_limit_bytes=VMEM_LIMIT)


def _bdot(a, b):
    return jnp.dot(a.astype(BF16), b.astype(BF16), preferred_element_type=F32)


def _bdot_nt(a, b):
    return lax.dot_general(a.astype(BF16), b.astype(BF16), (((1,), (1,)), ((), ())),
                           preferred_element_type=F32)


def _bdot_tn(a, b):
    return lax.dot_general(a.astype(BF16), b.astype(BF16), (((0,), (0,)), ((), ())),
                           preferred_element_type=F32)


def _rb(x):
    return x.astype(BF16).astype(F32)


def _silu(x):
    return x * jax.nn.sigmoid(x)


def _neg_expm1(x):
    return -jnp.tanh(0.5 * x) * (jnp.exp(x) + 1.0)


def _softplus(x):
    return jnp.maximum(x, 0.0) + jnp.log1p(jnp.exp(-jnp.abs(x)))


def _gelu_tanh(x):
    return 0.5 * x * (1.0 + jnp.tanh(math.sqrt(2.0 / math.pi) * (x + 0.044715 * (x * x * x))))


def _layer_norm(y, g, b):
    mu = jnp.mean(y, -1, keepdims=True)
    yc = y - mu
    var = jnp.mean(yc * yc, -1, keepdims=True)
    return yc * lax.rsqrt(var + LN_EPS) * g + b


def _rows(x_ref):
    if len(x_ref.shape) == 2:
        return x_ref[...]
    return jnp.concatenate([x_ref[j] for j in range(x_ref.shape[0])], axis=1)


def _rows_spec(x, tm, nargs):
    if x.ndim == 2:
        return pl.BlockSpec((tm, x.shape[1]), (lambda i: (i, 0)) if nargs == 1 else (lambda i, j: (i, 0)))
    blk = (x.shape[0], tm, LANES)
    return pl.BlockSpec(blk, (lambda i: (0, i, 0)) if nargs == 1 else (lambda i, j: (0, i, 0)))


def _mm_kernel(x_ref, w_ref, o_ref):
    o_ref[...] = _bdot(_rows(x_ref), w_ref[...])


def matmul(x, w, tm, tn, layer=None):
    m = x.shape[-2]
    k, n = w.shape[-2:]
    if w.ndim == 2:
        w_spec = pl.BlockSpec((k, tn), lambda i, j: (0, j))
    else:
        w_spec = pl.BlockSpec((None, k, tn), lambda i, j: (layer, 0, j))
    return pl.pallas_call(
        _mm_kernel,
        grid=(m // tm, n // tn),
        in_specs=[_rows_spec(x, tm, 2), w_spec],
        out_specs=pl.BlockSpec((tm, tn), lambda i, j: (i, j)),
        out_shape=jax.ShapeDtypeStruct((m, n), F32),
        compiler_params=_cparams(("parallel", "parallel")),
        name="matmul",
    )(x, w)


def _proj_res_ln_kernel(n_in, x_ref, *refs):
    a_refs = refs[:n_in]
    w_refs = refs[n_in:2 * n_in]
    g_ref, b_ref, o_ref = refs[2 * n_in:]
    y = ALPHA * _rows(x_ref)
    for a_ref, w_ref in zip(a_refs, w_refs):
        y = y + _bdot(a_ref[...], w_ref[...])
    o_ref[...] = _layer_norm(y, g_ref[...], b_ref[...])


def proj_res_ln(x, a_list, w_list, g, b, tm, layer=None):
    m = x.shape[-2]
    n_in = len(a_list)
    in_specs = [_rows_spec(x, tm, 1)]
    in_specs += [pl.BlockSpec((tm, a.shape[1]), lambda i: (i, 0)) for a in a_list]
    if layer is None:
        d = w_list[0].shape[1]
        in_specs += [pl.BlockSpec(w.shape, lambda i: (0, 0)) for w in w_list]
    else:
        (w,) = w_list
        d = w.shape[2]
        row0 = 0
        for a in a_list:
            k = a.shape[1]
            in_specs.append(pl.BlockSpec((None, k, d), functools.partial(lambda i, blk: (layer, blk, 0),
                                                                       blk=row0 // k)))
            row0 += k
        w_list = [w] * n_in
    in_specs += [pl.BlockSpec((1, d), lambda i: (0, 0))] * 2
    return pl.pallas_call(
        functools.partial(_proj_res_ln_kernel, n_in),
        grid=(m // tm,),
        in_specs=in_specs,
        out_specs=pl.BlockSpec((tm, d), lambda i: (i, 0)),
        out_shape=jax.ShapeDtypeStruct((m, d), F32),
        compiler_params=_cparams(("parallel",)),
        name="proj_res_ln",
    )(x, *a_list, *w_list, g, b)


def _attn_prompt_kernel(sink_ref, q_ref, kc_ref, kp_ref, vc_ref, vp_ref, bias_ref, o_ref):
    n = pl.program_id(1)
    col = lax.broadcasted_iota(jnp.int32, (WINDOW, 2 * WINDOW), 1)
    first = jnp.where((n == 0) & (col < WINDOW), NEG, 0.0)
    kk = jnp.concatenate([kp_ref[...], kc_ref[...]], axis=0).astype(BF16)
    vv = jnp.concatenate([vp_ref[...], vc_ref[...]], axis=0).astype(BF16)
    q = q_ref[...].astype(BF16)
    for u in range(A_QB):
        rows = slice(u * WINDOW, (u + 1) * WINDOW)
        keys = slice(u * WINDOW, (u + 2) * WINDOW)
        outs = []
        for h in range(A_HEADS):
            g = h // A_REP
            qh = q[rows, h * HEAD_DIM:(h + 1) * HEAD_DIM]
            kg = kk[keys, g * HEAD_DIM:(g + 1) * HEAD_DIM]
            vg = vv[keys, g * HEAD_DIM:(g + 1) * HEAD_DIM]
            s = _bdot_nt(qh, kg) * SCALE + bias_ref[h]
            if u == 0:
                s = s + first
            sink = sink_ref[h]
            m = jnp.maximum(jnp.max(s, -1, keepdims=True), sink)
            p = jnp.exp(s - m)
            den = jnp.sum(p, -1, keepdims=True) + jnp.exp(sink - m)
            outs.append(_bdot(p / den, vg))
        o_ref[rows, :] = jnp.concatenate(outs, axis=1).astype(o_ref.dtype)


def attn_prompt(proj, sink, bias, batch, seq):
    nb = seq // WINDOW
    ns = nb // A_QB
    tq = A_QB * WINDOW
    qcol = 0
    kcol = (A_HEADS * HEAD_DIM) // LANES
    vcol = kcol + 1

    def cur(c):
        return lambda b, n: (b * ns + n, c)

    def prev(c):
        return lambda b, n: (b * nb + jnp.maximum(n * A_QB - 1, 0), c)

    return pl.pallas_call(
        _attn_prompt_kernel,
        grid=(batch, ns),
        in_specs=[pl.BlockSpec(memory_space=pltpu.SMEM),
                  pl.BlockSpec((tq, A_HEADS * HEAD_DIM), cur(qcol)),
                  pl.BlockSpec((tq, LANES), cur(kcol)),
                  pl.BlockSpec((WINDOW, LANES), prev(kcol)),
                  pl.BlockSpec((tq, LANES), cur(vcol)),
                  pl.BlockSpec((WINDOW, LANES), prev(vcol)),
                  pl.BlockSpec((A_HEADS, WINDOW, 2 * WINDOW), lambda b, n: (0, 0, 0))],
        out_specs=pl.BlockSpec((tq, A_HEADS * HEAD_DIM), cur(0)),
        out_shape=jax.ShapeDtypeStruct((batch * seq, A_HEADS * HEAD_DIM), MIX_DTYPE),
        compiler_params=_cparams(("parallel", "parallel")),
        name="attn_prompt",
    )(sink, proj, proj, proj, proj, proj, bias)


def _attn_decode_kernel(qb_ref, kn_ref, vn_ref, knt_ref, vnt_ref, ck_ref, cv_ref, tab_ref,
                        o_ref, ok_ref, ov_ref):
    bb = qb_ref.shape[0]
    ck = ck_ref[...]
    cv = cv_ref[...]
    qb = qb_ref[...]
    kn = kn_ref[...]
    vn = vn_ref[...]
    bias_j = tab_ref[0]
    bias_n = tab_ref[1][:, 0:1]
    sink = tab_ref[2][:, 0:1]
    s = lax.dot_general(qb.astype(BF16), ck.astype(BF16), (((2,), (1,)), ((0,), (0,))),
                        preferred_element_type=F32) * SCALE + bias_j[None]
    sn = jnp.sum(_rb(qb) * _rb(kn)[:, None, :], -1, keepdims=True) * SCALE + bias_n[None]
    m = jnp.maximum(jnp.maximum(jnp.max(s, -1, keepdims=True), sn), sink[None])
    p = jnp.exp(s - m)
    pn = jnp.exp(sn - m)
    den = jnp.sum(p, -1, keepdims=True) + pn + jnp.exp(sink[None] - m)
    o = lax.dot_general((p / den).astype(BF16), cv.astype(BF16), (((2,), (2,)), ((0,), (0,))),
                        preferred_element_type=F32)
    o_ref[...] = o + _rb(pn / den) * _rb(vn)[:, None, :]
    lane = lax.broadcasted_iota(jnp.int32, (LANES, LANES), 1)
    for b in range(bb):
        ok_ref[b] = jnp.where(lane == WINDOW - 1, knt_ref[:, b:b + 1], pltpu.roll(ck[b], WINDOW - 1, 1))
        ov_ref[b] = jnp.where(lane == WINDOW - 1, vnt_ref[:, b:b + 1], pltpu.roll(cv[b], WINDOW - 1, 1))


def attn_decode(qblk, proj_d, table, cache_k, cache_v, layer, bb):
    nbatch = proj_d.shape[0]
    a_w = A_HEADS * HEAD_DIM
    cols = lambda c: proj_d[:, c:c + LANES].reshape(nbatch // bb, bb, LANES).transpose(0, 2, 1)
    knt, vnt = cols(a_w), cols(a_w + LANES)
    kcol = (A_HEADS * HEAD_DIM) // LANES
    cache_spec = pl.BlockSpec((None, bb, LANES, WINDOW), lambda i: (layer, i, 0, 0))
    col_spec = pl.BlockSpec((None, LANES, bb), lambda i: (i, 0, 0))
    return pl.pallas_call(
        _attn_decode_kernel,
        grid=(nbatch // bb,),
        in_specs=[pl.BlockSpec((bb, A_HEADS, LANES), lambda i: (i, 0, 0)),
                  pl.BlockSpec((bb, LANES), lambda i: (i, kcol)),
                  pl.BlockSpec((bb, LANES), lambda i: (i, kcol + 1)),
                  col_spec, col_spec, cache_spec, cache_spec,
                  pl.BlockSpec((3, A_HEADS, WINDOW), lambda i: (0, 0, 0))],
        out_specs=[pl.BlockSpec((bb, A_HEADS, LANES), lambda i: (i, 0, 0)), cache_spec, cache_spec],
        out_shape=[jax.ShapeDtypeStruct((nbatch, A_HEADS, LANES), F32),
                   jax.ShapeDtypeStruct(cache_k.shape, F32),
                   jax.ShapeDtypeStruct(cache_v.shape, F32)],
        input_output_aliases={5: 1, 6: 2},
        compiler_params=_cparams(("arbitrary",)),
        name="attn_decode",
    )(qblk, proj_d, proj_d, knt, vnt, cache_k, cache_v, table)


def _xattn_decode_kernel(qb_ref, mk_ref, mv_ref, hm_ref, o_ref):
    qb = qb_ref[...]
    s = lax.dot_general(qb.astype(BF16), mk_ref[...].astype(BF16), (((2,), (1,)), ((0,), (0,))),
                        preferred_element_type=F32) * SCALE
    m = jnp.max(s, -1, keepdims=True)
    p = jnp.exp(s - m)
    p = p / jnp.sum(p, -1, keepdims=True)
    o = lax.dot_general(p.astype(BF16), mv_ref[...].astype(BF16), (((2,), (2,)), ((0,), (0,))),
                        preferred_element_type=F32)
    o_ref[...] = jnp.sum(o * hm_ref[...][None], axis=1)


def xattn_decode(qblk, mem_k, mem_v, hmask, layer, bb):
    nbatch = qblk.shape[0]
    mem_spec = pl.BlockSpec((None, bb, X_WIDTH, N_MEM), lambda i: (layer, i, 0, 0))
    return pl.pallas_call(
        _xattn_decode_kernel,
        grid=(nbatch // bb,),
        in_specs=[pl.BlockSpec((bb, X_QROWS, X_WIDTH), lambda i: (i, 0, 0)), mem_spec, mem_spec,
                  pl.BlockSpec((X_QROWS, X_WIDTH), lambda i: (0, 0))],
        out_specs=pl.BlockSpec((bb, X_WIDTH), lambda i: (i, 0)),
        out_shape=jax.ShapeDtypeStruct((nbatch, X_WIDTH), F32),
        compiler_params=_cparams(("parallel",)),
        name="xattn_decode",
    )(qblk, mem_k, mem_v, hmask)


def _hgrn_gates(hq, hf, loglb, log1mlb):
    ls = jnp.minimum(hf, 0.0) - jnp.log1p(jnp.exp(-jnp.abs(hf)))
    b = log1mlb + ls
    lf = jnp.maximum(loglb, b) + jnp.log1p(jnp.exp(-jnp.abs(loglb - b)))
    return _silu(hq), lf, _neg_expm1(lf)


def _hgrn_prompt_kernel(hq_ref, hf_ref, hi_ref, hg_ref, loglb_ref, log1mlb_ref, gain_ref,
                        bdm_ref, hm_ref, ob_ref, st_ref, st_scr, q_scr, k_scr, cum_scr, o_scr, w_scr):
    i = pl.program_id(1)
    tt = hq_ref.shape[0]
    c = HG_CHUNK
    tb = HG_TB

    @pl.when(i == 0)
    def _():
        st_scr[...] = jnp.zeros_like(st_scr)

    qs, lf, kk = _hgrn_gates(hq_ref[...], hf_ref[...], loglb_ref[...], log1mlb_ref[...])
    row = lax.broadcasted_iota(jnp.int32, (tt, HG_WIDTH), 0) & (c - 1)
    cum = lf
    sh = 1
    while sh < c:
        cum = cum + jnp.where(row >= sh, pltpu.roll(cum, sh, 0), 0.0)
        sh *= 2
    q_scr[...] = qs
    k_scr[...] = kk
    cum_scr[...] = cum

    bdm = bdm_ref[...]
    hmask = hm_ref[...]
    causal = (lax.broadcasted_iota(jnp.int32, (tb, tb, HG_WIDTH), 0)
              >= lax.broadcasted_iota(jnp.int32, (tb, tb, HG_WIDTH), 1))
    first_head = lax.broadcasted_iota(jnp.int32, (tb, LANES), 1) < HEAD_DIM

    def chunk(ci, carry):
        r0 = pl.multiple_of(ci * c, c)
        r = pl.ds(r0, c)
        cu = cum_scr[r, :]
        q = q_scr[r, :]
        k = k_scr[r, :]
        v = hi_ref[r, :]
        vb = v.astype(BF16)
        qb = _rb(q)
        cu2 = cu * LOG2E
        last = cu[c - 1:c, :]
        st = st_scr[...]
        o_inter = _bdot_nt(q * jnp.exp(cu), st)
        for j in range(c // tb):
            n0 = tb * j
            ns = n0 + tb
            cut = cu2[n0:ns, :]
            dec = jnp.exp2(jnp.where(causal, cut[:, None, :] - cut[None, :, :], NEG))
            a2 = (dec * k[None, n0:ns, :]).astype(BF16)
            if j:
                past = jnp.exp2(cut[:, None, :] - cu2[None, :n0, :]) * k[None, :n0, :]
                a2 = jnp.concatenate([past.astype(BF16), a2], axis=1)
            q4 = qb[tb * j:tb * (j + 1), None, :] * hmask[None, :, :]
            att = lax.dot_general(q4, a2, (((2,), (2,)), ((0,), (0,))),
                                  preferred_element_type=F32)
            slot = (ci & (HG_UNROLL - 1)) * (c // tb) + j
            w = jnp.dot(_rb(att.reshape(tb * HG_MROWS, ns)), vb[:ns, :], preferred_element_type=F32)
            halves = []
            for p in range(HG_WIDTH // LANES):
                w_scr[slot, p] = w[:, p * LANES:(p + 1) * LANES]
                rows = [w_scr[slot, p, pl.ds(2 * p + e, tb, stride=HG_MROWS), :] for e in range(2)]
                halves.append(jnp.where(first_head, rows[0], rows[1]))
            o_intra = jnp.concatenate(halves, axis=1)
            o_scr[pl.ds(r0 + tb * j, tb), :] = o_intra + o_inter[tb * j:tb * (j + 1), :]
        upd = _bdot_tn(v, k * jnp.exp(last - cu))
        st_scr[...] = st * jnp.exp(last) + upd * bdm
        return carry

    lax.fori_loop(0, tt // c, chunk, 0, unroll=HG_UNROLL)

    o = o_scr[...]
    ms = jnp.dot(o * o, bdm, precision=lax.Precision.HIGHEST,
                 preferred_element_type=F32) * (1.0 / HEAD_DIM)
    ob_ref[...] = (o * lax.rsqrt(ms + RMS_EPS) * gain_ref[...] * _silu(hg_ref[...])).astype(ob_ref.dtype)

    @pl.when(i == pl.num_programs(1) - 1)
    def _():
        st_ref[...] = st_scr[...]


def hgrn_prompt(proj, loglb, log1mlb, gain4, bdm, hmask, batch, seq, tt):
    nt = seq // tt
    base = (A_HEADS + 2 * A_KV_HEADS) * HEAD_DIM // HG_WIDTH

    def col(cblk):
        return pl.BlockSpec((tt, HG_WIDTH), lambda b, i: (b * nt + i, cblk))

    row_spec = pl.BlockSpec((1, HG_WIDTH), lambda b, i: (0, 0))
    mat_spec = pl.BlockSpec((HG_WIDTH, HG_WIDTH), lambda b, i: (0, 0))
    return pl.pallas_call(
        _hgrn_prompt_kernel,
        grid=(batch, nt),
        in_specs=[col(base), col(base + 1), col(base + 2), col(base + 3),
                  row_spec, row_spec, row_spec, mat_spec,
                  pl.BlockSpec((HG_MROWS, HG_WIDTH), lambda b, i: (0, 0))],
        out_specs=[pl.BlockSpec((tt, HG_WIDTH), lambda b, i: (b * nt + i, 0)),
                   pl.BlockSpec((None, HG_WIDTH, HG_WIDTH), lambda b, i: (b, 0, 0))],
        out_shape=[jax.ShapeDtypeStruct((batch * seq, HG_WIDTH), MIX_DTYPE),
                   jax.ShapeDtypeStruct((batch, HG_WIDTH, HG_WIDTH), F32)],
        scratch_shapes=[pltpu.VMEM((HG_WIDTH, HG_WIDTH), F32),
                        pltpu.VMEM((tt, HG_WIDTH), F32),
                        pltpu.VMEM((tt, HG_WIDTH), F32),
                        pltpu.VMEM((tt, HG_WIDTH), F32),
                        pltpu.VMEM((tt, HG_WIDTH), F32),
                        pltpu.VMEM((HG_UNROLL * (HG_CHUNK // HG_TB), HG_WIDTH // LANES, HG_TB * HG_MROWS, LANES), F32)],
        compiler_params=_cparams(("parallel", "arbitrary")),
        name="hgrn_prompt",
    )(proj, proj, proj, proj, loglb, log1mlb, gain4, bdm, hmask)


def _hgrn_decode_kernel(hq_ref, hf_ref, hi_ref, hg_ref, loglb_ref, log1mlb_ref, gain_ref, s_ref,
                        ob_ref, so_ref):
    nb = hq_ref.shape[1]
    qs, lf, kk = _hgrn_gates(hq_ref[...], hf_ref[...], loglb_ref[...], log1mlb_ref[...])
    v = hi_ref[...]
    f = jnp.exp(lf)
    s = s_ref[...].reshape(HEAD_DIM, HEAD_DIM, nb)
    att = jnp.sum(_rb(qs) * _rb(kk), axis=0, keepdims=True)
    o = _rb(att) * _rb(v) + jnp.sum(_rb(qs * f)[:, None, :] * _rb(s), axis=0)
    s_new = f[:, None, :] * s + _rb(kk)[:, None, :] * _rb(v)[None, :, :]
    so_ref[...] = s_new.reshape(HEAD_DIM * HEAD_DIM, nb)
    ms = jnp.mean(o * o, axis=0, keepdims=True)
    ob_ref[...] = (o * lax.rsqrt(ms + RMS_EPS) * gain_ref[...] * _silu(hg_ref[...])).astype(ob_ref.dtype)


def hgrn_decode(gates_t, loglb_t, log1mlb_t, gain_t, state_t, layer):
    nb = gates_t.shape[1]

    def blk(off):
        return pl.BlockSpec((HEAD_DIM, nb), lambda h: (off * HG_HEADS + h, 0))

    par = pl.BlockSpec((HEAD_DIM, nb), lambda h: (h, 0))
    st = pl.BlockSpec((None, HEAD_DIM * HEAD_DIM, nb), lambda h: (layer, h, 0))
    return pl.pallas_call(
        _hgrn_decode_kernel,
        grid=(HG_HEADS,),
        in_specs=[blk(0), blk(1), blk(2), blk(3), par, par,
                  pl.BlockSpec((HEAD_DIM, nb), lambda h: (0, 0)), st],
        out_specs=[par, st],
        out_shape=[jax.ShapeDtypeStruct((HG_WIDTH, nb), F32),
                   jax.ShapeDtypeStruct(state_t.shape, F32)],
        input_output_aliases={7: 1},
        compiler_params=_cparams(("arbitrary",)),
        name="hgrn_decode",
    )(gates_t, gates_t, gates_t, gates_t, loglb_t, log1mlb_t, gain_t, state_t)


def _lru_gates(xc, wa_ref, ba_ref, wx_ref, bx_ref, lam_ref):
    r = jax.nn.sigmoid(_bdot(xc, wa_ref[...]) + ba_ref[...])
    gi = jax.nn.sigmoid(_bdot(xc, wx_ref[...]) + bx_ref[...])
    log_a = -LRU_C * r * _softplus(-lam_ref[...])
    a = jnp.exp(log_a)
    bterm = jnp.sqrt(_neg_expm1(2.0 * log_a)) * (gi * xc)
    return a, bterm


def _lru_prompt_kernel(lx_ref, lg_ref, cw_ref, cb_ref, wa_ref, ba_ref, wx_ref, bx_ref, lam_ref,
                       oc_ref, hl_ref, ext_scr, h_scr):
    i = pl.program_id(1)
    tt = lx_ref.shape[0]
    pad = SUBLANES

    @pl.when(i == 0)
    def _():
        ext_scr[0:pad, :] = jnp.zeros((pad, LRU_WIDTH), F32)
        h_scr[...] = jnp.zeros_like(h_scr)

    x = lx_ref[...]
    ext_scr[pad:pad + tt, :] = x
    xc = cb_ref[...] + cw_ref[CONV_W - 1:CONV_W, :] * x
    for j in range(CONV_W - 1):
        back = CONV_W - 1 - j
        xc = xc + cw_ref[j:j + 1, :] * ext_scr[pad - back:pad - back + tt, :]
    ext_scr[0:pad, :] = x[tt - pad:tt, :]

    a, bterm = _lru_gates(xc, wa_ref, ba_ref, wx_ref, bx_ref, lam_ref)
    row = lax.broadcasted_iota(jnp.int32, (tt, LRU_WIDTH), 0)
    sh = 1
    while sh < tt:
        keep = row >= sh
        b_s = jnp.where(keep, pltpu.roll(bterm, sh, 0), 0.0)
        a_s = jnp.where(keep, pltpu.roll(a, sh, 0), 1.0)
        bterm = a * b_s + bterm
        a = a * a_s
        sh *= 2
    h = a * h_scr[...] + bterm
    h_scr[...] = h[tt - 1:tt, :]
    oc_ref[...] = (h * _gelu_tanh(lg_ref[...])).astype(oc_ref.dtype)

    @pl.when(i == pl.num_programs(1) - 1)
    def _():
        hl_ref[...] = h[tt - 1:tt, :]


def lru_prompt(proj, conv_w, conv_b, wa_bd, ba, wx_bd, bx, lam, batch, seq, tt):
    nt = seq // tt
    base = IN_COLS // LRU_WIDTH - 2

    def col(cblk):
        return pl.BlockSpec((tt, LRU_WIDTH), lambda b, i: (b * nt + i, cblk))

    row_spec = pl.BlockSpec((1, LRU_WIDTH), lambda b, i: (0, 0))
    mat_spec = pl.BlockSpec((LRU_WIDTH, LRU_WIDTH), lambda b, i: (0, 0))
    return pl.pallas_call(
        _lru_prompt_kernel,
        grid=(batch, nt),
        in_specs=[col(base), col(base + 1),
                  pl.BlockSpec((CONV_W, LRU_WIDTH), lambda b, i: (0, 0)), row_spec,
                  mat_spec, row_spec, mat_spec, row_spec, row_spec],
        out_specs=[pl.BlockSpec((tt, LRU_WIDTH), lambda b, i: (b * nt + i, 0)),
                   pl.BlockSpec((None, 1, LRU_WIDTH), lambda b, i: (b, 0, 0))],
        out_shape=[jax.ShapeDtypeStruct((batch * seq, LRU_WIDTH), MIX_DTYPE),
                   jax.ShapeDtypeStruct((batch, 1, LRU_WIDTH), F32)],
        scratch_shapes=[pltpu.VMEM((tt + 8, LRU_WIDTH), F32),
                        pltpu.VMEM((1, LRU_WIDTH), F32)],
        compiler_params=_cparams(("parallel", "arbitrary")),
        name="lru_prompt",
    )(proj, proj, conv_w, conv_b, wa_bd, ba, wx_bd, bx, lam)


def _lru_decode_kernel(lx_ref, lg_ref, buf_ref, h0_ref, cw_ref, cb_ref, wa_ref, ba_ref, wx_ref,
                       bx_ref, lam_ref, oc_ref, hn_ref, nbuf_ref):
    x = lx_ref[...]
    buf = buf_ref[...]
    xc = cb_ref[...] + cw_ref[CONV_W - 1:CONV_W, :] * x
    for j in range(CONV_W - 1):
        xc = xc + cw_ref[j:j + 1, :] * buf[:, j * LRU_WIDTH:(j + 1) * LRU_WIDTH]
    a, bterm = _lru_gates(xc, wa_ref, ba_ref, wx_ref, bx_ref, lam_ref)
    h = a * h0_ref[...] + bterm
    hn_ref[...] = h
    oc_ref[...] = (h * _gelu_tanh(lg_ref[...])).astype(oc_ref.dtype)
    nbuf_ref[...] = jnp.concatenate([buf[:, LRU_WIDTH:], x], axis=1)


def lru_decode(proj_d, conv_buf, h0, conv_w, conv_b, wa_bd, ba, wx_bd, bx, lam):
    nb = proj_d.shape[0]
    base = IN_COLS // LRU_WIDTH - 2
    row_spec = pl.BlockSpec((1, LRU_WIDTH), lambda i: (0, 0))
    mat_spec = pl.BlockSpec((LRU_WIDTH, LRU_WIDTH), lambda i: (0, 0))
    act = pl.BlockSpec((nb, LRU_WIDTH), lambda i: (0, 0))
    bufs = pl.BlockSpec((nb, (CONV_W - 1) * LRU_WIDTH), lambda i: (0, 0))
    return pl.pallas_call(
        _lru_decode_kernel,
        grid=(1,),
        in_specs=[pl.BlockSpec((nb, LRU_WIDTH), lambda i: (0, base)),
                  pl.BlockSpec((nb, LRU_WIDTH), lambda i: (0, base + 1)),
                  bufs, act, pl.BlockSpec((CONV_W, LRU_WIDTH), lambda i: (0, 0)), row_spec,
                  mat_spec, row_spec, mat_spec, row_spec, row_spec],
        out_specs=[act, act, bufs],
        out_shape=[jax.ShapeDtypeStruct((nb, LRU_WIDTH), F32),
                   jax.ShapeDtypeStruct((nb, LRU_WIDTH), F32),
                   jax.ShapeDtypeStruct((nb, (CONV_W - 1) * LRU_WIDTH), F32)],
        compiler_params=_cparams(("arbitrary",)),
        name="lru_decode",
    )(proj_d, proj_d, conv_buf, h0, conv_w, conv_b, wa_bd, ba, wx_bd, bx, lam)


def _xattn_prompt_kernel(x_ref, wq_ref, mk_ref, mv_ref, wo_ref, g_ref, b_ref, wr_ref, br_ref, o_ref):
    x = x_ref[...]
    q = _bdot(x, wq_ref[...]).astype(BF16)
    mk = mk_ref[...].astype(BF16)
    mv = mv_ref[...].astype(BF16)
    outs = []
    for h in range(X_HEADS):
        sl = slice(h * HEAD_DIM, (h + 1) * HEAD_DIM)
        s = _bdot_nt(q[:, sl], mk[:, sl]) * SCALE
        m = jnp.max(s, -1, keepdims=True)
        p = jnp.exp(s - m)
        p = p / jnp.sum(p, -1, keepdims=True)
        outs.append(_bdot(p, mv[:, sl]))
    o = jnp.concatenate(outs, axis=1)
    y = _layer_norm(ALPHA * x + _bdot(o, wo_ref[...]), g_ref[...], b_ref[...])
    logits_t = _bdot_nt(wr_ref[...], y) + br_ref[...]
    gate_t = _route_rows(logits_t[:ROUTER_ROWS, :])
    gate_t = jnp.concatenate([gate_t, jnp.zeros((ROUTER_LANES - ROUTER_ROWS, y.shape[0]), F32)], axis=0)
    for j in range(D_MODEL // LANES):
        o_ref[j] = y[:, j * LANES:(j + 1) * LANES]
    o_ref[D_MODEL // LANES] = gate_t.T


def xattn_prompt(x, wq, mem_kv, wo, g, b, wr, br, batch, seq, tt):
    nt = seq // tt
    const = lambda bi, i: (0, 0)
    return pl.pallas_call(
        _xattn_prompt_kernel,
        grid=(batch, nt),
        in_specs=[pl.BlockSpec((tt, D_MODEL), lambda bi, i: (bi * nt + i, 0)),
                  pl.BlockSpec((D_MODEL, X_WIDTH), const),
                  pl.BlockSpec((N_MEM, X_WIDTH), lambda bi, i: (bi, 0)),
                  pl.BlockSpec((N_MEM, X_WIDTH), lambda bi, i: (bi, 1)),
                  pl.BlockSpec((X_WIDTH, D_MODEL), const),
                  pl.BlockSpec((1, D_MODEL), const),
                  pl.BlockSpec((1, D_MODEL), const),
                  pl.BlockSpec((ROUTER_LANES, D_MODEL), const),
                  pl.BlockSpec((ROUTER_LANES, 1), const)],
        out_specs=pl.BlockSpec((XG_PLANES, tt, LANES), lambda bi, i: (0, bi * nt + i, 0)),
        out_shape=jax.ShapeDtypeStruct((XG_PLANES, batch * seq, LANES), F32),
        compiler_params=_cparams(("parallel", "parallel")),
        name="xattn_prompt",
    )(x, wq, mem_kv, mem_kv, wo, g, b, wr, br)


def _route(logits):
    lane = lax.broadcasted_iota(jnp.int32, logits.shape, 1)
    big = jnp.int32(ROUTER_LANES)
    ninf = -jnp.inf
    gl = jnp.where(lane < N_GROUPS, logits, ninf)
    gm = jnp.max(gl, -1, keepdims=True)
    g_val = 1.0 / jnp.sum(jnp.exp(gl - gm), -1, keepdims=True)
    g_idx = jnp.min(jnp.where(gl == gm, lane, big), -1, keepdims=True)
    lo = N_GROUPS + EXP_PER_GROUP * g_idx
    el = jnp.where((lane >= lo) & (lane < lo + EXP_PER_GROUP), logits, ninf)
    v1 = jnp.max(el, -1, keepdims=True)
    i1 = jnp.min(jnp.where(el == v1, lane, big), -1, keepdims=True)
    el2 = jnp.where(lane == i1, ninf, el)
    v2 = jnp.max(el2, -1, keepdims=True)
    i2 = jnp.min(jnp.where(el2 == v2, lane, big), -1, keepdims=True)
    e2 = jnp.exp(v2 - v1)
    w1 = g_val / (1.0 + e2)
    w2 = g_val * e2 / (1.0 + e2)
    return jnp.where(lane == i1, w1, 0.0) + jnp.where(lane == i2, w2, 0.0), g_idx


def _route_rows(logits):
    row = lax.broadcasted_iota(jnp.int32, logits.shape, 0)
    big = jnp.int32(ROUTER_ROWS)
    ninf = -jnp.inf
    gl = jnp.where(row < N_GROUPS, logits, ninf)
    gm = jnp.max(gl, 0, keepdims=True)
    g_val = 1.0 / jnp.sum(jnp.exp(gl - gm), 0, keepdims=True)
    g_idx = jnp.min(jnp.where(gl == gm, row, big), 0, keepdims=True)
    lo = N_GROUPS + EXP_PER_GROUP * g_idx
    el = jnp.where((row >= lo) & (row < lo + EXP_PER_GROUP), logits, ninf)
    v1 = jnp.max(el, 0, keepdims=True)
    i1 = jnp.min(jnp.where(el == v1, row, big), 0, keepdims=True)
    el2 = jnp.where(row == i1, ninf, el)
    v2 = jnp.max(el2, 0, keepdims=True)
    i2 = jnp.min(jnp.where(el2 == v2, row, big), 0, keepdims=True)
    e2 = jnp.exp(v2 - v1)
    w1 = g_val / (1.0 + e2)
    w2 = g_val * e2 / (1.0 + e2)
    gate = jnp.where(row == i1, w1, 0.0) + jnp.where(row == i2, w2, 0.0)
    return jnp.where(row == 0, g_idx.astype(F32), gate)


def _moe_dense_kernel(x_ref, wr_ref, br_ref, wg_ref, wu_ref, wd_ref, g_ref, b_ref, o_ref,
                      gate_scr, acc_scr):
    e = pl.program_id(1)

    @pl.when(e == 0)
    def _():
        logits = _bdot(x_ref[...], wr_ref[...]) + br_ref[...]
        gate_scr[...] = _route(logits)[0]
        acc_scr[...] = jnp.zeros_like(acc_scr)

    xb = x_ref[...].astype(BF16)
    lane = lax.broadcasted_iota(jnp.int32, gate_scr.shape, 1)
    gcol = jnp.sum(jnp.where(lane == e + N_GROUPS, gate_scr[...], 0.0), -1, keepdims=True)
    hid = _silu(_bdot(xb, wg_ref[...])) * _bdot(xb, wu_ref[...])
    acc_scr[...] += _bdot(hid * gcol, wd_ref[...])

    @pl.when(e == pl.num_programs(1) - 1)
    def _():
        o_ref[...] = _layer_norm(ALPHA * x_ref[...] + acc_scr[...], g_ref[...], b_ref[...])


def moe_dense(x, wr, br, wg, wu, wd, g, b, layer, tm):
    m = x.shape[0]
    return pl.pallas_call(
        _moe_dense_kernel,
        grid=(m // tm, N_EXPERTS),
        in_specs=[pl.BlockSpec((tm, D_MODEL), lambda i, e: (i, 0)),
                  pl.BlockSpec((D_MODEL, ROUTER_LANES), lambda i, e: (0, 0)),
                  pl.BlockSpec((1, ROUTER_LANES), lambda i, e: (0, 0)),
                  pl.BlockSpec((None, None, D_MODEL, EXP_FF), lambda i, e: (layer, e, 0, 0)),
                  pl.BlockSpec((None, None, D_MODEL, EXP_FF), lambda i, e: (layer, e, 0, 0)),
                  pl.BlockSpec((None, None, EXP_FF, D_MODEL), lambda i, e: (layer, e, 0, 0)),
                  pl.BlockSpec((1, D_MODEL), lambda i, e: (0, 0)),
                  pl.BlockSpec((1, D_MODEL), lambda i, e: (0, 0))],
        out_specs=pl.BlockSpec((tm, D_MODEL), lambda i, e: (i, 0)),
        out_shape=jax.ShapeDtypeStruct((m, D_MODEL), F32),
        scratch_shapes=[pltpu.VMEM((tm, ROUTER_LANES), F32), pltpu.VMEM((tm, D_MODEL), F32)],
        compiler_params=_cparams(("parallel", "arbitrary")),
        name="moe_dense",
    )(x, wr, br, wg, wu, wd, g, b)


def _sc_mesh():
    return plsc.VectorSubcoreMesh(core_axis_name="core", subcore_axis_name="subcore")


def sc_scatter_rows(x, idx, n_out):
    r = x.shape[0]

    @functools.partial(pl.kernel, out_type=jax.ShapeDtypeStruct((n_out, LANES), x.dtype),
                       mesh=_sc_mesh(), scratch_types=[], name="sc_scatter_rows")
    def k(x_hbm, i_hbm, o_hbm):
        def body(x_vmem, i_vmem):
            pltpu.sync_copy(x_vmem, o_hbm.at[i_vmem.at[0]])

        pltpu.emit_pipeline(
            body,
            grid=(r // SC_WINDOW,),
            in_specs=[pl.BlockSpec((SC_WINDOW, LANES), lambda i: (i, 0)),
                      pl.BlockSpec((1, SC_WINDOW), lambda i: (0, i))],
            out_specs=[],
            core_axis_name=("core", "subcore"),
            dimension_semantics=(pltpu.PARALLEL,),
        )(x_hbm, i_hbm)

    return k(x, idx.reshape(1, r))


def sc_gather_rows(table, idx):
    r = idx.shape[0]

    @functools.partial(pl.kernel, out_type=jax.ShapeDtypeStruct((r, LANES), table.dtype),
                       mesh=_sc_mesh(), scratch_types=[], name="sc_gather_rows")
    def k(t_hbm, i_hbm, o_hbm):
        def body(i_vmem, o_vmem):
            pltpu.sync_copy(t_hbm.at[i_vmem.at[0]], o_vmem)

        pltpu.emit_pipeline(
            body,
            grid=(r // SC_WINDOW,),
            in_specs=[pl.BlockSpec((1, SC_WINDOW), lambda i: (0, i))],
            out_specs=[pl.BlockSpec((SC_WINDOW, LANES), lambda i: (i, 0))],
            core_axis_name=("core", "subcore"),
            dimension_semantics=(pltpu.PARALLEL,),
        )(i_hbm, o_hbm)

    return k(table, idx.reshape(1, r))


def _moe_sorted_kernel(tg_ref, nused_ref, xs_ref, wg32_ref, wu32_ref, wd32_ref, g_ref, b_ref, o_ref,
                       wg_ref, wu_ref, wd_ref):
    t = pl.program_id(0)
    used = t < nused_ref[0]

    @pl.when(used & ((t == 0) | (tg_ref[t] != tg_ref[jnp.maximum(t - 1, 0)])))
    def _():
        wg_ref[...] = wg32_ref[...].astype(BF16)
        wu_ref[...] = wu32_ref[...].astype(BF16)
        wd_ref[...] = wd32_ref[...].astype(BF16)

    @pl.when(used)
    def _():
        x = jnp.concatenate([xs_ref[j] for j in range(X_PLANES)], axis=1)
        gate = xs_ref[X_PLANES]
        xb = x.astype(BF16)
        lane = lax.broadcasted_iota(jnp.int32, gate.shape, 1)
        first = N_GROUPS + EXP_PER_GROUP * tg_ref[t]
        acc = jnp.zeros(x.shape, F32)
        for e in range(EXP_PER_GROUP):
            gcol = jnp.sum(jnp.where(lane == first + e, gate, 0.0), -1, keepdims=True)
            hid = _silu(_bdot(xb, wg_ref[e])) * _bdot(xb, wu_ref[e])
            acc = acc + _bdot(hid * gcol, wd_ref[e])
        y = _layer_norm(ALPHA * x + acc, g_ref[...], b_ref[...])
        for j in range(X_PLANES):
            o_ref[j] = y[:, j * LANES:(j + 1) * LANES]


def _group_slots(group_idx, n, tm):
    n_tiles = n // tm + N_GROUPS
    onehot = (group_idx[:, None] == jnp.arange(N_GROUPS)[None, :]).astype(jnp.int32)
    csum = jnp.cumsum(onehot, axis=0)
    counts = csum[-1]
    rank = jnp.sum(onehot * csum, axis=1) - 1
    tiles_g = (counts + tm - 1) // tm
    tile_end = jnp.cumsum(tiles_g)
    slot_base = (tile_end - tiles_g) * tm
    slot = (jnp.sum(onehot * slot_base[None, :], axis=1) + rank).astype(jnp.int32)
    tile_group = jnp.sum((jnp.arange(n_tiles)[:, None] >= tile_end[None, :]).astype(jnp.int32), axis=1)
    tile_group = jnp.minimum(tile_group, N_GROUPS - 1).astype(jnp.int32)
    return slot, tile_group, tile_end[-1:].astype(jnp.int32)


def moe_routed_sc(xg, wg, wu, wd, g, b, layer, n, tm, during_scatter, during_gather, tiles_out):
    slot, tile_group, n_used = _group_slots(xg[X_PLANES, :, 0].astype(jnp.int32), n, tm)
    n_tiles = tile_group.shape[0]
    n_slots = n_tiles * tm
    plane_base = lambda planes: jnp.arange(planes, dtype=jnp.int32)[:, None] * n_slots
    idx_in = (plane_base(XG_PLANES) + slot[None, :]).reshape(-1)
    if tiles_out:
        idx_out = (plane_base(X_PLANES)[None] + slot.reshape(n // SUBLANES, 1, SUBLANES)).reshape(-1)
    else:
        idx_out = (plane_base(X_PLANES) + slot[None, :]).reshape(-1)
    xs = sc_scatter_rows(xg.reshape(XG_PLANES * n, LANES), idx_in, XG_PLANES * n_slots)
    xs = xs.reshape(XG_PLANES, n_slots, LANES)
    xs, side = lax.optimization_barrier((xs, during_scatter()))
    wspec = lambda shp: pl.BlockSpec((None, None, EXP_PER_GROUP) + shp,
                                     lambda t, tg, nu: (layer, tg[t], 0, 0, 0))
    used_tile = lambda t, tg, nu: (0, jnp.minimum(t, nu[0] - 1), 0)
    grid_spec = pltpu.PrefetchScalarGridSpec(
        num_scalar_prefetch=2,
        grid=(n_tiles,),
        in_specs=[pl.BlockSpec((XG_PLANES, tm, LANES), used_tile),
                  wspec((D_MODEL, EXP_FF)), wspec((D_MODEL, EXP_FF)), wspec((EXP_FF, D_MODEL)),
                  pl.BlockSpec((1, D_MODEL), lambda t, tg, nu: (0, 0)),
                  pl.BlockSpec((1, D_MODEL), lambda t, tg, nu: (0, 0))],
        out_specs=pl.BlockSpec((X_PLANES, tm, LANES), used_tile),
        scratch_shapes=[pltpu.VMEM((EXP_PER_GROUP, D_MODEL, EXP_FF), BF16),
                        pltpu.VMEM((EXP_PER_GROUP, D_MODEL, EXP_FF), BF16),
                        pltpu.VMEM((EXP_PER_GROUP, EXP_FF, D_MODEL), BF16)])
    grouped = lambda w: w.reshape(w.shape[0], N_GROUPS, EXP_PER_GROUP, w.shape[2], w.shape[3])
    ys = pl.pallas_call(
        _moe_sorted_kernel,
        grid_spec=grid_spec,
        out_shape=jax.ShapeDtypeStruct((X_PLANES, n_slots, LANES), F32),
        compiler_params=_cparams(("arbitrary",)),
        name="moe_sorted",
    )(tile_group, n_used, xs, grouped(wg), grouped(wu), grouped(wd), g, b)
    y = sc_gather_rows(ys.reshape(X_PLANES * n_slots, LANES), idx_out)
    y, side = lax.optimization_barrier((y, during_gather(side)))
    if tiles_out:
        y = y.reshape(n // SUBLANES, X_PLANES, SUBLANES, LANES).transpose(0, 2, 1, 3)
        return y.reshape(n, D_MODEL), side
    return y.reshape(X_PLANES, n, LANES), side


def _t5_bucket(dist):
    max_exact = N_BUCKETS // 2
    d = jnp.maximum(dist, 0)
    df = jnp.maximum(d, 1).astype(F32)
    log_b = max_exact + (jnp.log(df / max_exact) / math.log(MAX_DISTANCE / max_exact)
                         * (N_BUCKETS - max_exact)).astype(jnp.int32)
    return jnp.where(d < max_exact, d, jnp.minimum(log_b, N_BUCKETS - 1))


def _bucket_lookup(rel_bias, bucket):
    out = jnp.zeros(bucket.shape + (rel_bias.shape[1],), F32)
    for i in range(N_BUCKETS):
        out = jnp.where((bucket == i)[..., None], rel_bias[i].astype(F32), out)
    return out


def _prompt_bias(rel_bias):
    qi = jnp.arange(WINDOW)[:, None]
    kj = jnp.arange(2 * WINDOW)[None, :]
    dist = qi + WINDOW - kj
    bias = _bucket_lookup(rel_bias, _t5_bucket(dist)).transpose(2, 0, 1)
    valid = (dist >= 0) & (dist <= WINDOW)
    return jnp.where(valid[None], bias, NEG)


def _decode_table(rel_bias, attn_sink):
    dist = WINDOW - jnp.arange(WINDOW + 1)
    bias = _bucket_lookup(rel_bias, _t5_bucket(dist)).T
    depth = attn_sink.shape[0]
    wide = lambda v: jnp.broadcast_to(v[..., None], v.shape + (WINDOW,))
    per_layer = lambda t: jnp.broadcast_to(t[None], (depth,) + t.shape)
    return jnp.stack([per_layer(bias[:, :WINDOW]), per_layer(wide(bias[:, WINDOW])),
                      wide(attn_sink.astype(F32))], axis=1)


def _block_ones(width):
    idx = jnp.arange(width) // HEAD_DIM
    return (idx[:, None] == idx[None, :])


def _head_rows_mask():
    head = jnp.arange(HG_WIDTH)[None, :] // HEAD_DIM
    return (head == jnp.arange(HG_MROWS)[:, None]).astype(F32)


def _block_diag(w):
    nblk, s, _ = w.shape
    eye = jnp.eye(nblk, dtype=w.dtype)
    return (eye[:, None, :, None] * w[:, :, None, :]).reshape(nblk * s, nblk * s)


def kernel(x_prompt, x_sample, mem_prompt, cache_win_k, cache_win_v, state_hgrn, state_conv, state_lru, cache_mem_k, cache_mem_v, rel_bias, hg_lb, w_in, attn_sink, hg_gain, conv_w, conv_b, lru_wa, lru_ba, lru_wx, lru_bx, lru_lam, w_out, ln1_g, ln1_b, x_wq, x_wk, x_wv, x_wo, ln2_g, ln2_b, r_gw, r_gb, r_ew, r_eb, e_wg, e_wu, e_wd, ln3_g, ln3_b):
    bp, seq, d = x_prompt.shape
    n_tok = bp * seq
    nd = x_sample.shape[0]
    depth = w_in.shape[0]

    lbs = jnp.cumsum(jax.nn.softmax(hg_lb.astype(F32), axis=0), axis=0)
    lbs = lbs - lbs[0]
    loglb = jnp.log(lbs)
    log1mlb = jnp.log1p(-lbs)
    gain4 = jnp.tile(hg_gain, (1, HG_HEADS))

    bias_p = _prompt_bias(rel_bias)
    bdm256 = _block_ones(HG_WIDTH).astype(F32)
    hmask = _head_rows_mask()

    w_in_b = w_in.astype(BF16)
    w_out_b = w_out.astype(BF16)
    wq_b = x_wq.astype(BF16)
    wkv_b = jnp.concatenate([x_wk, x_wv], axis=-1).astype(BF16)
    wo_b = x_wo.astype(BF16)
    rew = r_ew.transpose(0, 2, 1, 3).reshape(depth, d, N_EXPERTS)
    wr = jnp.concatenate([r_gw, rew, jnp.zeros((depth, d, ROUTER_LANES - N_GROUPS - N_EXPERTS), F32)], -1)
    br = jnp.concatenate([r_gb, r_eb.reshape(depth, N_EXPERTS),
                          jnp.zeros((depth, ROUTER_LANES - N_GROUPS - N_EXPERTS), F32)], -1)

    a_w = A_HEADS * HEAD_DIM
    xp = x_prompt.reshape(bp * seq, d)
    xs = x_sample.reshape(nd, d)
    mem = mem_prompt.reshape(bp * N_MEM, d)
    ckt = cache_win_k.transpose(0, 1, 3, 4, 2).reshape(depth, nd, LANES, WINDOW)
    cvt = cache_win_v.transpose(0, 1, 3, 4, 2).reshape(depth, nd, LANES, WINDOW)
    cmkt = cache_mem_k.transpose(0, 1, 3, 4, 2).reshape(depth, nd, X_WIDTH, N_MEM)
    cmvt = cache_mem_v.transpose(0, 1, 3, 4, 2).reshape(depth, nd, X_WIDTH, N_MEM)
    state_t = state_hgrn.transpose(0, 2, 3, 4, 1).reshape(depth, HG_HEADS * HEAD_DIM * HEAD_DIM, nd)
    dec_tab = _decode_table(rel_bias, attn_sink)
    xq_mask = _head_rows_mask()[:X_QROWS]
    head_group = jnp.arange(A_HEADS) // A_REP

    p_wk, p_wv, p_s, p_cb, p_h, p_mk, p_mv = [], [], [], [], [], [], []
    s_cb, s_h = [], []
    for l in range(depth):
        row = lambda v: v[l].reshape(1, -1)
        wa_bd = _block_diag(lru_wa[l]).astype(BF16)
        wx_bd = _block_diag(lru_wx[l]).astype(BF16)
        lru_args = (conv_w[l], row(conv_b), wa_bd, row(lru_ba), wx_bd, row(lru_bx), row(lru_lam))

        proj = matmul(xp, w_in_b, ROW_TILE, IN_COLS, layer=l)
        oa = attn_prompt(proj, attn_sink[l], bias_p, bp, seq)
        ob, st = hgrn_prompt(proj, row(loglb), row(log1mlb), row(gain4), bdm256, hmask, bp, seq, ROW_TILE)
        oc, hl = lru_prompt(proj, *lru_args, bp, seq, LRU_TILE)
        xp = proj_res_ln(xp, [oa, ob, oc], [w_out_b], row(ln1_g), row(ln1_b), ROW_TILE, layer=l)
        mkv = matmul(mem, wkv_b[l], N_MEM, 2 * X_WIDTH)
        xg = xattn_prompt(xp, wq_b[l], mkv, wo_b[l], row(ln2_g), row(ln2_b), wr[l].T, br[l][:, None],
                          bp, seq, ROW_TILE)
        def decode_mixers(xs=xs, ckt=ckt, cvt=cvt, state_t=state_t, l=l, lru_args=lru_args,
                          row=row):
            projd = matmul(xs, w_in_b, nd, IN_COLS, layer=l)
            q3 = projd[:, :a_w].reshape(nd, A_HEADS, 1, HEAD_DIM)
            on_group = head_group[None, :, None, None] == jnp.arange(A_KV_HEADS)[None, None, :, None]
            qblk = jnp.where(on_group, q3, 0.0).reshape(nd, A_HEADS, LANES)
            o3, ckt, cvt = attn_decode(qblk, projd, dec_tab[l], ckt, cvt, l, DEC_BLOCK)
            o4 = o3.reshape(nd, A_HEADS, A_KV_HEADS, HEAD_DIM)
            oa = jnp.sum(jnp.where(on_group, o4, 0.0), axis=2).reshape(nd, a_w)
            gates_t = projd[:, a_w + 2 * LANES:a_w + 2 * LANES + 4 * HG_WIDTH].T
            bc = lambda v: jnp.broadcast_to(v[:, None], (v.shape[0], nd))
            ob_t, state_t = hgrn_decode(gates_t, bc(loglb[l]), bc(log1mlb[l]), bc(hg_gain[l]), state_t, l)
            oc, nh, nbuf = lru_decode(projd, state_conv[l].reshape(nd, -1), state_lru[l], *lru_args)
            xs = proj_res_ln(xs, [oa, ob_t.T, oc], [w_out_b], row(ln1_g), row(ln1_b), nd, layer=l)
            return xs, ckt, cvt, state_t, nh, nbuf

        def decode_rest(side, l=l, row=row):
            xs = side[0]
            qd = matmul(xs, wq_b[l], nd, X_WIDTH)
            qdb = qd[:, None, :] * xq_mask[None, :, :]
            od = xattn_decode(qdb, cmkt, cmvt, xq_mask, l, DEC_BLOCK)
            xs = proj_res_ln(xs, [od], [wo_b[l]], row(ln2_g), row(ln2_b), nd)
            xs = moe_dense(xs, wr[l], br[l:l + 1], e_wg, e_wu, e_wd, row(ln3_g), row(ln3_b), l, nd)
            return (xs,) + tuple(side[1:])

        xp, (xs, ckt, cvt, state_t, nh, nbuf) = moe_routed_sc(
            xg, e_wg, e_wu, e_wd, row(ln3_g), row(ln3_b), l, n_tok, MOE_TM, decode_mixers, decode_rest,
            tiles_out=(l == depth - 1))

        proj3 = proj.reshape(bp, seq, IN_COLS)
        p_wk.append(proj3[:, seq - WINDOW:, a_w:a_w + LANES].reshape(bp, WINDOW, A_KV_HEADS, HEAD_DIM))
        p_wv.append(proj3[:, seq - WINDOW:, a_w + LANES:a_w + 2 * LANES].reshape(bp, WINDOW, A_KV_HEADS, HEAD_DIM))
        st5 = st.reshape(bp, HG_HEADS, HEAD_DIM, HG_HEADS, HEAD_DIM)
        p_s.append(jnp.stack([st5[:, h, :, h, :] for h in range(HG_HEADS)], 1).transpose(0, 1, 3, 2))
        p_cb.append(proj3[:, seq - (CONV_W - 1):, IN_COLS - 2 * LRU_WIDTH:IN_COLS - LRU_WIDTH])
        p_h.append(hl.reshape(bp, LRU_WIDTH))
        p_mk.append(mkv[:, :X_WIDTH].reshape(bp, N_MEM, X_HEADS, HEAD_DIM))
        p_mv.append(mkv[:, X_WIDTH:].reshape(bp, N_MEM, X_HEADS, HEAD_DIM))

        s_cb.append(nbuf.reshape(nd, CONV_W - 1, LRU_WIDTH))
        s_h.append(nh)

    unkey = lambda c: c.reshape(depth, nd, A_KV_HEADS, HEAD_DIM, WINDOW).transpose(0, 1, 4, 2, 3)
    s_s = state_t.reshape(depth, HG_HEADS, HEAD_DIM, HEAD_DIM, nd).transpose(0, 4, 1, 2, 3)
    return (xp.reshape(bp, seq, d), xs.reshape(nd, 1, d),
            jnp.stack(p_wk), jnp.stack(p_wv), jnp.stack(p_s), jnp.stack(p_cb), jnp.stack(p_h),
            jnp.stack(p_mk), jnp.stack(p_mv),
            unkey(ckt), unkey(cvt), s_s, jnp.stack(s_cb), jnp.stack(s_h))
```
